```python
import jax
import jax.numpy as jnp
from jax import lax
import numpy as np

D_MODEL = 1024
BATCH = 8
SEQ = 2048
DEPTH = 4

CONV_K = 4
NORM_EPS = 1e-6
RG_WIDTH = 512
RG_BLOCKS = 8
RG_BLOCK = RG_WIDTH // RG_BLOCKS
RG_C = 8.0
ML_HEADS = 4
ML_DH = 128
ML_WIDTH = ML_HEADS * ML_DH
ML_CHUNK = 64
GD_HEADS = 4
GD_DK = 128
GD_DV = 128
GD_QK = GD_HEADS * GD_DK
GD_WIDTH = GD_HEADS * GD_DV
GD_CHUNK = 64
D_MIX = RG_WIDTH + ML_WIDTH + GD_WIDTH
IN_SIZES = (RG_WIDTH, RG_WIDTH,
            ML_WIDTH, ML_WIDTH, ML_WIDTH, ML_WIDTH, ML_WIDTH, ML_HEADS, ML_HEADS,
            GD_QK, GD_QK, GD_WIDTH, GD_WIDTH, GD_HEADS, GD_HEADS)
D_IN = sum(IN_SIZES)

kernel_name = "hymba_rglru_mlstm_gdn_trunk"


def rmsnorm(x, w):
    xf = x.astype(jnp.float32)
    r = lax.rsqrt(jnp.mean(xf * xf, axis=-1, keepdims=True) + NORM_EPS)
    return (xf * r).astype(x.dtype) * w


def _rms_f32(x):
    return x * lax.rsqrt(jnp.mean(x * x, axis=-1, keepdims=True) + NORM_EPS)


def _l2norm(x):
    return x * lax.rsqrt(jnp.sum(x * x, axis=-1, keepdims=True) + NORM_EPS)


def causal_dwconv(x, w):
    K = w.shape[0]
    S = x.shape[1]
    xp = jnp.pad(x, ((0, 0), (K - 1, 0), (0, 0)))
    y = xp[:, 0:S] * w[0]
    for k in range(1, K):
        y = y + xp[:, k:k + S] * w[k]
    return y


def _lin_combine(left, right):
    a_l, b_l = left
    a_r, b_r = right
    return a_l * a_r, a_r * b_l + b_r


def rglru_branch(xb, zb, conv_w, conv_b, gate_w, gate_b, lam):
    Bn, S, _ = xb.shape
    xc = causal_dwconv(xb, conv_w) + conv_b
    xblk = xc.reshape(Bn, S, RG_BLOCKS, RG_BLOCK)
    gates = jnp.einsum('bsnc,gncd->gbsnd', xblk, gate_w).reshape(2, Bn, S, RG_WIDTH)
    gates = gates + gate_b[:, None, None, :]
    r = jax.nn.sigmoid(gates[0])
    i = jax.nn.sigmoid(gates[1])
    log_a = -RG_C * r * jax.nn.softplus(-lam)
    a = jnp.exp(log_a)
    b = jnp.sqrt(-jnp.expm1(2.0 * log_a)) * (i * xc)
    _, h = lax.associative_scan(_lin_combine, (a, b), axis=1)
    return h * jax.nn.silu(zb)


def mlstm_branch(q, k, v, o_pre, z, i_pre, f_pre, gate_b, norm_w):
    Bn, S, _ = q.shape
    L = ML_CHUNK
    N = S // L

    def chunks(t):
        return t.reshape(Bn, N, L, ML_HEADS, ML_DH).transpose(1, 0, 3, 2, 4)

    def chunks_s(t):
        return t.reshape(Bn, N, L, ML_HEADS).transpose(1, 0, 3, 2)

    qc = chunks(q) * (ML_DH ** -0.5)
    kc = chunks(k)
    vc = chunks(v)
    li = chunks_s(i_pre + gate_b[0])
    lf = chunks_s(jax.nn.log_sigmoid(f_pre + gate_b[1]))
    bcum = jnp.cumsum(lf, axis=-1)
    causal = jnp.tril(jnp.ones((L, L), dtype=bool))
    dmat = jnp.where(causal, bcum[..., :, None] - bcum[..., None, :] + li[..., None, :], -jnp.inf)
    w_state = bcum[..., -1:] - bcum + li

    def step(carry, xs):
        C, n, m = carry
        dm, bq, ws, qi, ki, vi = xs
        m_inter = bq + m[..., None]
        m_t = jnp.maximum(m_inter, jnp.max(dm, axis=-1))
        p = jnp.exp(dm - m_t[..., None])
        s = jnp.einsum('bhtd,bhsd->bhts', qi, ki) * p
        sc = jnp.exp(m_inter - m_t)
        num = jnp.einsum('bhts,bhse->bhte', s, vi) + sc[..., None] * jnp.einsum('bhtd,bhde->bhte', qi, C)
        den = jnp.sum(s, axis=-1) + sc * jnp.einsum('bhtd,bhd->bht', qi, n)
        h = num / jnp.maximum(jnp.abs(den), jnp.exp(-m_t))[..., None]
        g = bq[..., -1]
        m_new = jnp.maximum(g + m, jnp.max(ws, axis=-1))
        dec = jnp.exp(g + m - m_new)
        wk = jnp.exp(ws - m_new[..., None])
        C = dec[..., None, None] * C + jnp.einsum('bhs,bhsd,bhse->bhde', wk, ki, vi)
        n = dec[..., None] * n + jnp.einsum('bhs,bhsd->bhd', wk, ki)
        return (C, n, m_new), h

    init = (jnp.zeros((Bn, ML_HEADS, ML_DH, ML_DH), jnp.float32),
            jnp.zeros((Bn, ML_HEADS, ML_DH), jnp.float32),
            jnp.zeros((Bn, ML_HEADS), jnp.float32))
    _, h = lax.scan(step, init, (dmat, bcum, w_state, qc, kc, vc))
    h = h.transpose(1, 0, 3, 2, 4).reshape(Bn, S, ML_HEADS, ML_DH)
    h = _rms_f32(h) * norm_w.reshape(ML_HEADS, ML_DH)
    h = h.reshape(Bn, S, ML_WIDTH)
    return h * jax.nn.sigmoid(o_pre) * jax.nn.silu(z)


def gdn_branch(q, k, v, z, a_pre, b_pre, conv_w, a_log, dt_bias, norm_w):
    Bn, S, _ = q.shape
    L = GD_CHUNK
    N = S // L
    qkv = jax.nn.silu(causal_dwconv(jnp.concatenate([q, k, v], axis=-1), conv_w))
    q, k, v = jnp.split(qkv, [GD_QK, 2 * GD_QK], axis=-1)

    def chunks(t, d):
        return t.reshape(Bn, N, L, GD_HEADS, d).transpose(0, 3, 1, 2, 4)

    def chunks_s(t):
        return t.reshape(Bn, N, L, GD_HEADS).transpose(0, 3, 1, 2)

    q = _l2norm(chunks(q, GD_DK)) * (GD_DK ** -0.5)
    k = _l2norm(chunks(k, GD_DK))
    v = chunks(v, GD_DV)
    beta = chunks_s(jax.nn.sigmoid(b_pre))
    g = chunks_s(-jnp.exp(a_log) * jax.nn.softplus(a_pre + dt_bias))
    gc = jnp.cumsum(g, axis=-1)
    incl = jnp.tril(jnp.ones((L, L), dtype=bool))
    strict = jnp.tril(jnp.ones((L, L), dtype=bool), k=-1)
    gam = jnp.exp(jnp.where(incl, gc[..., :, None] - gc[..., None, :], -jnp.inf))
    kb = k * beta[..., None]
    m_strict = jnp.where(strict, jnp.einsum('bhntd,bhnsd->bhnts', kb, k) * gam, 0.0)
    eye = jnp.eye(L, dtype=m_strict.dtype)
    t_inv = lax.linalg.triangular_solve(eye + m_strict, jnp.broadcast_to(eye, m_strict.shape),
                                        left_side=True, lower=True, unit_diagonal=True)
    u = t_inv @ (v * beta[..., None])
    w = t_inv @ (kb * jnp.exp(gc)[..., None])
    aqk = jnp.einsum('bhntd,bhnsd->bhnts', q, k) * gam
    q_dec = q * jnp.exp(gc)[..., None]
    g_last = gc[..., -1]
    k_dec = k * jnp.exp(g_last[..., None] - gc)[..., None]
    xs = (jnp.moveaxis(u, 2, 0), jnp.moveaxis(w, 2, 0), jnp.moveaxis(aqk, 2, 0),
          jnp.moveaxis(q_dec, 2, 0), jnp.moveaxis(k_dec, 2, 0), jnp.moveaxis(g_last, 2, 0))

    def step(state, xs_i):
        ui, wi, ai, qi, ki, gl = xs_i
        v_new = ui - wi @ state
        o = qi @ state + ai @ v_new
        state = state * jnp.exp(gl)[..., None, None] + jnp.swapaxes(ki, -1, -2) @ v_new
        return state, o

    _, o = lax.scan(step, jnp.zeros((Bn, GD_HEADS, GD_DK, GD_DV), jnp.float32), xs)
    o = o.transpose(1, 0, 3, 2, 4).reshape(Bn, S, GD_HEADS, GD_DV)
    o = _rms_f32(o) * norm_w * jax.nn.silu(z.reshape(Bn, S, GD_HEADS, GD_DV))
    return o.reshape(Bn, S, GD_WIDTH)


def _fwd_setup_inputs(seed: int = 0) -> dict:
    key = jax.random.key(seed)
    ks = jax.random.split(key, 17)
    f32 = jnp.float32
    nrm = jax.random.normal
    x = nrm(ks[0], (BATCH, SEQ, D_MODEL), f32)
    norm_w = 1.0 + 0.02 * nrm(ks[1], (DEPTH, D_MODEL), f32)
    w_in = nrm(ks[2], (DEPTH, D_MODEL, D_IN), f32) * (D_MODEL ** -0.5)
    rg_conv_w = nrm(ks[3], (DEPTH, CONV_K, RG_WIDTH), f32) * (CONV_K ** -0.5)
    rg_conv_b = 0.01 * nrm(ks[4], (DEPTH, RG_WIDTH), f32)
    rg_gate_w = nrm(ks[5], (DEPTH, 2, RG_BLOCKS, RG_BLOCK, RG_BLOCK), f32) * (RG_BLOCK ** -0.5)
    rg_gate_b = 0.01 * nrm(ks[6], (DEPTH, 2, RG_WIDTH), f32)
    a_c = jax.random.uniform(ks[7], (DEPTH, RG_WIDTH), f32, minval=0.9, maxval=0.999)
    a0 = a_c ** (1.0 / RG_C)
    rg_lambda = jnp.log(a0) - jnp.log1p(-a0)
    ml_i_b = 0.1 * nrm(ks[8], (DEPTH, ML_HEADS), f32)
    ml_f_b = jnp.linspace(3.0, 6.0, ML_HEADS, dtype=f32)[None, :] + 0.1 * nrm(ks[9], (DEPTH, ML_HEADS), f32)
    ml_gate_b = jnp.stack([ml_i_b, ml_f_b], axis=1)
    ml_norm_w = 1.0 + 0.02 * nrm(ks[10], (DEPTH, ML_WIDTH), f32)
    gd_conv_w = nrm(ks[11], (DEPTH, CONV_K, 2 * GD_QK + GD_WIDTH), f32) * (CONV_K ** -0.5)
    gd_a_log = jnp.log(jax.random.uniform(ks[12], (DEPTH, GD_HEADS), f32, minval=1.0, maxval=16.0))
    dt = jnp.exp(jax.random.uniform(ks[13], (DEPTH, GD_HEADS), f32,
                                    minval=float(np.log(1e-3)), maxval=float(np.log(1e-1))))
    gd_dt_bias = dt + jnp.log(-jnp.expm1(-dt))
    gd_norm_w = 1.0 + 0.02 * nrm(ks[14], (DEPTH, GD_DV), f32)
    w_out = nrm(ks[15], (DEPTH, D_MIX, D_MODEL), f32) * (D_MIX ** -0.5)
    final_norm_w = 1.0 + 0.02 * nrm(ks[16], (D_MODEL,), f32)
    return {"x": x, "norm_w": norm_w, "w_in": w_in, "rg_conv_w": rg_conv_w, "rg_conv_b": rg_conv_b,
            "rg_gate_w": rg_gate_w, "rg_gate_b": rg_gate_b, "rg_lambda": rg_lambda,
            "ml_gate_b": ml_gate_b, "ml_norm_w": ml_norm_w, "gd_conv_w": gd_conv_w,
            "gd_a_log": gd_a_log, "gd_dt_bias": gd_dt_bias, "gd_norm_w": gd_norm_w,
            "w_out": w_out, "final_norm_w": final_norm_w}


def _fwd_reference(x, norm_w, w_in, rg_conv_w, rg_conv_b, rg_gate_w, rg_gate_b, rg_lambda,
              ml_gate_b, ml_norm_w, gd_conv_w, gd_a_log, gd_dt_bias, gd_norm_w,
              w_out, final_norm_w):
    split_idx = [int(c) for c in np.cumsum(IN_SIZES)[:-1]]
    for l in range(DEPTH):
        hn = rmsnorm(x, norm_w[l])
        proj = (hn @ w_in[l]).astype(jnp.float32)
        (rg_x, rg_z, ml_q, ml_k, ml_v, ml_o, ml_z, ml_i, ml_f,
         gd_q, gd_k, gd_v, gd_z, gd_a, gd_b) = jnp.split(proj, split_idx, axis=-1)
        y_rg = rglru_branch(rg_x, rg_z, rg_conv_w[l], rg_conv_b[l], rg_gate_w[l], rg_gate_b[l], rg_lambda[l])
        y_ml = mlstm_branch(ml_q, ml_k, ml_v, ml_o, ml_z, ml_i, ml_f, ml_gate_b[l], ml_norm_w[l])
        y_gd = gdn_branch(gd_q, gd_k, gd_v, gd_z, gd_a, gd_b, gd_conv_w[l], gd_a_log[l],
                          gd_dt_bias[l], gd_norm_w[l])
        y = jnp.concatenate([y_rg, y_ml, y_gd], axis=-1)
        x = x + y.astype(x.dtype) @ w_out[l]
    return rmsnorm(x, final_norm_w)


import jax as _jax
import jax.numpy as _jnp

TWIN_FORMAT = 'train_step'
FWD_PARAMS = ['x', 'norm_w', 'w_in', 'rg_conv_w', 'rg_conv_b', 'rg_gate_w', 'rg_gate_b', 'rg_lambda', 'ml_gate_b', 'ml_norm_w', 'gd_conv_w', 'gd_a_log', 'gd_dt_bias', 'gd_norm_w', 'w_out', 'final_norm_w']
TWIN_WEIGHTS = ['norm_w', 'w_in', 'rg_conv_w', 'rg_conv_b', 'rg_gate_w', 'rg_gate_b', 'rg_lambda', 'ml_gate_b', 'ml_norm_w', 'gd_conv_w', 'gd_a_log', 'gd_dt_bias', 'gd_norm_w', 'w_out', 'final_norm_w']
TWIN_DIFF_INPUT = 'x'
TWIN_INPUTS = ['x', 'norm_w', 'w_in', 'rg_conv_w', 'rg_conv_b', 'rg_gate_w', 'rg_gate_b', 'rg_lambda', 'ml_gate_b', 'ml_norm_w', 'gd_conv_w', 'gd_a_log', 'gd_dt_bias', 'gd_norm_w', 'w_out', 'final_norm_w', 'loss_target', 'm_norm_w', 'm_w_in', 'm_rg_conv_w', 'm_rg_conv_b', 'm_rg_gate_w', 'm_rg_gate_b', 'm_rg_lambda', 'm_ml_gate_b', 'm_ml_norm_w', 'm_gd_conv_w', 'm_gd_a_log', 'm_gd_dt_bias', 'm_gd_norm_w', 'm_w_out', 'm_final_norm_w', 'v_norm_w', 'v_w_in', 'v_rg_conv_w', 'v_rg_conv_b', 'v_rg_gate_w', 'v_rg_gate_b', 'v_rg_lambda', 'v_ml_gate_b', 'v_ml_norm_w', 'v_gd_conv_w', 'v_gd_a_log', 'v_gd_dt_bias', 'v_gd_norm_w', 'v_w_out', 'v_final_norm_w']
TWIN_OUTPUTS = ['loss', 'grad_x', 'grad_norm_w', 'grad_w_in', 'grad_rg_conv_w', 'grad_rg_conv_b', 'grad_rg_gate_w', 'grad_rg_gate_b', 'grad_rg_lambda', 'grad_ml_gate_b', 'grad_ml_norm_w', 'grad_gd_conv_w', 'grad_gd_a_log', 'grad_gd_dt_bias', 'grad_gd_norm_w', 'grad_w_out', 'grad_final_norm_w', 'delta_norm_w', 'delta_w_in', 'delta_rg_conv_w', 'delta_rg_conv_b', 'delta_rg_gate_w', 'delta_rg_gate_b', 'delta_rg_lambda', 'delta_ml_gate_b', 'delta_ml_norm_w', 'delta_gd_conv_w', 'delta_gd_a_log', 'delta_gd_dt_bias', 'delta_gd_norm_w', 'delta_w_out', 'delta_final_norm_w', 'new_m_norm_w', 'new_m_w_in', 'new_m_rg_conv_w', 'new_m_rg_conv_b', 'new_m_rg_gate_w', 'new_m_rg_gate_b', 'new_m_rg_lambda', 'new_m_ml_gate_b', 'new_m_ml_norm_w', 'new_m_gd_conv_w', 'new_m_gd_a_log', 'new_m_gd_dt_bias', 'new_m_gd_norm_w', 'new_m_w_out', 'new_m_final_norm_w', 'new_v_norm_w', 'new_v_w_in', 'new_v_rg_conv_w', 'new_v_rg_conv_b', 'new_v_rg_gate_w', 'new_v_rg_gate_b', 'new_v_rg_lambda', 'new_v_ml_gate_b', 'new_v_ml_norm_w', 'new_v_gd_conv_w', 'new_v_gd_a_log', 'new_v_gd_dt_bias', 'new_v_gd_norm_w', 'new_v_w_out', 'new_v_final_norm_w']
TWIN_LEAF_KINDS = {'loss': 'loss', 'grad_x': 'grad_x', 'grad_norm_w': 'grad_w', 'grad_w_in': 'grad_w', 'grad_rg_conv_w': 'grad_w', 'grad_rg_conv_b': 'grad_w', 'grad_rg_gate_w': 'grad_w', 'grad_rg_gate_b': 'grad_w', 'grad_rg_lambda': 'grad_w', 'grad_ml_gate_b': 'grad_w', 'grad_ml_norm_w': 'grad_w', 'grad_gd_conv_w': 'grad_w', 'grad_gd_a_log': 'grad_w', 'grad_gd_dt_bias': 'grad_w', 'grad_gd_norm_w': 'grad_w', 'grad_w_out': 'grad_w', 'grad_final_norm_w': 'grad_w', 'delta_norm_w': 'delta_w', 'delta_w_in': 'delta_w', 'delta_rg_conv_w': 'delta_w', 'delta_rg_conv_b': 'delta_w', 'delta_rg_gate_w': 'delta_w', 'delta_rg_gate_b': 'delta_w', 'delta_rg_lambda': 'delta_w', 'delta_ml_gate_b': 'delta_w', 'delta_ml_norm_w': 'delta_w', 'delta_gd_conv_w': 'delta_w', 'delta_gd_a_log': 'delta_w', 'delta_gd_dt_bias': 'delta_w', 'delta_gd_norm_w': 'delta_w', 'delta_w_out': 'delta_w', 'delta_final_norm_w': 'delta_w', 'new_m_norm_w': 'new_m', 'new_m_w_in': 'new_m', 'new_m_rg_conv_w': 'new_m', 'new_m_rg_conv_b': 'new_m', 'new_m_rg_gate_w': 'new_m', 'new_m_rg_gate_b': 'new_m', 'new_m_rg_lambda': 'new_m', 'new_m_ml_gate_b': 'new_m', 'new_m_ml_norm_w': 'new_m', 'new_m_gd_conv_w': 'new_m', 'new_m_gd_a_log': 'new_m', 'new_m_gd_dt_bias': 'new_m', 'new_m_gd_norm_w': 'new_m', 'new_m_w_out': 'new_m', 'new_m_final_norm_w': 'new_m', 'new_v_norm_w': 'new_v', 'new_v_w_in': 'new_v', 'new_v_rg_conv_w': 'new_v', 'new_v_rg_conv_b': 'new_v', 'new_v_rg_gate_w': 'new_v', 'new_v_rg_gate_b': 'new_v', 'new_v_rg_lambda': 'new_v', 'new_v_ml_gate_b': 'new_v', 'new_v_ml_norm_w': 'new_v', 'new_v_gd_conv_w': 'new_v', 'new_v_gd_a_log': 'new_v', 'new_v_gd_dt_bias': 'new_v', 'new_v_gd_norm_w': 'new_v', 'new_v_w_out': 'new_v', 'new_v_final_norm_w': 'new_v'}


def _forward(args):
    return _fwd_reference(*[args[k] for k in FWD_PARAMS])


def _output_shape():
    out = _jax.eval_shape(lambda: _forward(_fwd_setup_inputs(0)))
    return out.shape, out.dtype

N_MICROBATCH = 1
ADAM_LR = 0.001
ADAM_B1 = 0.9
ADAM_B2 = 0.999
ADAM_EPS = 1e-08
ADAM_WD = 0.01
ADAM_STEP = 10
PER_EXAMPLE_BATCH_AXIS = {'x': 0, 'loss_target': 0}
SHARED_INPUTS = []
_WEIGHT_DTYPES = {'norm_w': _jnp.float32, 'w_in': _jnp.float32, 'rg_conv_w': _jnp.float32, 'rg_conv_b': _jnp.float32, 'rg_gate_w': _jnp.float32, 'rg_gate_b': _jnp.float32, 'rg_lambda': _jnp.float32, 'ml_gate_b': _jnp.float32, 'ml_norm_w': _jnp.float32, 'gd_conv_w': _jnp.float32, 'gd_a_log': _jnp.float32, 'gd_dt_bias': _jnp.float32, 'gd_norm_w': _jnp.float32, 'w_out': _jnp.float32, 'final_norm_w': _jnp.float32}
MOMENT_SCALE = {'norm_w': 1.034411e-01, 'w_in': 4.274913e-02, 'rg_conv_w': 5.787831e-02, 'rg_conv_b': 4.686237e-01, 'rg_gate_w': 2.386036e-02, 'rg_gate_b': 1.866613e-02, 'rg_lambda': 3.136747e-02, 'ml_gate_b': 1.364177e-01, 'ml_norm_w': 3.405655e-02, 'gd_conv_w': 4.634684e-02, 'gd_a_log': 2.438021e-01, 'gd_dt_bias': 2.331675e-01, 'gd_norm_w': 1.182037e-01, 'w_out': 6.406082e-02, 'final_norm_w': 1.602470e+01}


def _to_microbatches(a, axis):
    t = _jnp.moveaxis(a, axis, 0)
    t = t.reshape((N_MICROBATCH, t.shape[0] // N_MICROBATCH) + t.shape[1:])
    return _jnp.moveaxis(t, 1, axis + 1)


def setup_inputs(seed: int = 0) -> dict:
    inp = _fwd_setup_inputs(seed)
    key = _jax.random.fold_in(_jax.random.key(seed), 7919)
    shape, _ = _output_shape()
    out = dict(inp)
    out["loss_target"] = _jax.random.normal(_jax.random.fold_in(key, 0), shape, _jnp.float32)
    for i, name in enumerate(TWIN_WEIGHTS):
        w = inp[name].astype(_jnp.float32)
        if MOMENT_SCALE is None:
            s = _jnp.sqrt(_jnp.mean(_jnp.square(w)) + 1e-30)
        else:
            s = MOMENT_SCALE[name]
        km, kv = _jax.random.split(_jax.random.fold_in(key, i + 1))
        out[name] = w
        out["m_" + name] = s * _jax.random.normal(km, w.shape, _jnp.float32)
        out["v_" + name] = (s * s) * _jax.random.uniform(kv, w.shape, _jnp.float32, 0.5, 1.5)
    if N_MICROBATCH > 1:
        for name, axis in PER_EXAMPLE_BATCH_AXIS.items():
            out[name] = _to_microbatches(out[name], axis)
    return {'x': out['x'], 'norm_w': out['norm_w'], 'w_in': out['w_in'], 'rg_conv_w': out['rg_conv_w'], 'rg_conv_b': out['rg_conv_b'], 'rg_gate_w': out['rg_gate_w'], 'rg_gate_b': out['rg_gate_b'], 'rg_lambda': out['rg_lambda'], 'ml_gate_b': out['ml_gate_b'], 'ml_norm_w': out['ml_norm_w'], 'gd_conv_w': out['gd_conv_w'], 'gd_a_log': out['gd_a_log'], 'gd_dt_bias': out['gd_dt_bias'], 'gd_norm_w': out['gd_norm_w'], 'w_out': out['w_out'], 'final_norm_w': out['final_norm_w'], 'loss_target': out['loss_target'], 'm_norm_w': out['m_norm_w'], 'm_w_in': out['m_w_in'], 'm_rg_conv_w': out['m_rg_conv_w'], 'm_rg_conv_b': out['m_rg_conv_b'], 'm_rg_gate_w': out['m_rg_gate_w'], 'm_rg_gate_b': out['m_rg_gate_b'], 'm_rg_lambda': out['m_rg_lambda'], 'm_ml_gate_b': out['m_ml_gate_b'], 'm_ml_norm_w': out['m_ml_norm_w'], 'm_gd_conv_w': out['m_gd_conv_w'], 'm_gd_a_log': out['m_gd_a_log'], 'm_gd_dt_bias': out['m_gd_dt_bias'], 'm_gd_norm_w': out['m_gd_norm_w'], 'm_w_out': out['m_w_out'], 'm_final_norm_w': out['m_final_norm_w'], 'v_norm_w': out['v_norm_w'], 'v_w_in': out['v_w_in'], 'v_rg_conv_w': out['v_rg_conv_w'], 'v_rg_conv_b': out['v_rg_conv_b'], 'v_rg_gate_w': out['v_rg_gate_w'], 'v_rg_gate_b': out['v_rg_gate_b'], 'v_rg_lambda': out['v_rg_lambda'], 'v_ml_gate_b': out['v_ml_gate_b'], 'v_ml_norm_w': out['v_ml_norm_w'], 'v_gd_conv_w': out['v_gd_conv_w'], 'v_gd_a_log': out['v_gd_a_log'], 'v_gd_dt_bias': out['v_gd_dt_bias'], 'v_gd_norm_w': out['v_gd_norm_w'], 'v_w_out': out['v_w_out'], 'v_final_norm_w': out['v_final_norm_w']}


def _loss(weights, diff, rest, loss_target):
    with _jax.named_scope("forward"):
        args = {**rest, TWIN_DIFF_INPUT: diff, **{k: w.astype(_WEIGHT_DTYPES[k]) for k, w in weights.items()}}
        y = _forward(args)
    with _jax.named_scope("loss_head"):
        err = _jnp.square(y.astype(_jnp.float32) - loss_target)
        return 0.5 * _jnp.sum(_jnp.mean(err, axis=-1)) if err.ndim else 0.5 * err


def _adamw(w, g, m, v):
    m = ADAM_B1 * m + (1.0 - ADAM_B1) * g
    v = ADAM_B2 * v + (1.0 - ADAM_B2) * _jnp.square(g)
    m_hat = m / (1.0 - ADAM_B1 ** ADAM_STEP)
    v_hat = v / (1.0 - ADAM_B2 ** ADAM_STEP)
    delta = -ADAM_LR * (m_hat / (_jnp.sqrt(v_hat) + ADAM_EPS) + ADAM_WD * w)
    return delta, m, v


def reference(x, norm_w, w_in, rg_conv_w, rg_conv_b, rg_gate_w, rg_gate_b, rg_lambda, ml_gate_b, ml_norm_w, gd_conv_w, gd_a_log, gd_dt_bias, gd_norm_w, w_out, final_norm_w, loss_target, m_norm_w, m_w_in, m_rg_conv_w, m_rg_conv_b, m_rg_gate_w, m_rg_gate_b, m_rg_lambda, m_ml_gate_b, m_ml_norm_w, m_gd_conv_w, m_gd_a_log, m_gd_dt_bias, m_gd_norm_w, m_w_out, m_final_norm_w, v_norm_w, v_w_in, v_rg_conv_w, v_rg_conv_b, v_rg_gate_w, v_rg_gate_b, v_rg_lambda, v_ml_gate_b, v_ml_norm_w, v_gd_conv_w, v_gd_a_log, v_gd_dt_bias, v_gd_norm_w, v_w_out, v_final_norm_w):
    given = dict(x=x, norm_w=norm_w, w_in=w_in, rg_conv_w=rg_conv_w, rg_conv_b=rg_conv_b, rg_gate_w=rg_gate_w, rg_gate_b=rg_gate_b, rg_lambda=rg_lambda, ml_gate_b=ml_gate_b, ml_norm_w=ml_norm_w, gd_conv_w=gd_conv_w, gd_a_log=gd_a_log, gd_dt_bias=gd_dt_bias, gd_norm_w=gd_norm_w, w_out=w_out, final_norm_w=final_norm_w, loss_target=loss_target, m_norm_w=m_norm_w, m_w_in=m_w_in, m_rg_conv_w=m_rg_conv_w, m_rg_conv_b=m_rg_conv_b, m_rg_gate_w=m_rg_gate_w, m_rg_gate_b=m_rg_gate_b, m_rg_lambda=m_rg_lambda, m_ml_gate_b=m_ml_gate_b, m_ml_norm_w=m_ml_norm_w, m_gd_conv_w=m_gd_conv_w, m_gd_a_log=m_gd_a_log, m_gd_dt_bias=m_gd_dt_bias, m_gd_norm_w=m_gd_norm_w, m_w_out=m_w_out, m_final_norm_w=m_final_norm_w, v_norm_w=v_norm_w, v_w_in=v_w_in, v_rg_conv_w=v_rg_conv_w, v_rg_conv_b=v_rg_conv_b, v_rg_gate_w=v_rg_gate_w, v_rg_gate_b=v_rg_gate_b, v_rg_lambda=v_rg_lambda, v_ml_gate_b=v_ml_gate_b, v_ml_norm_w=v_ml_norm_w, v_gd_conv_w=v_gd_conv_w, v_gd_a_log=v_gd_a_log, v_gd_dt_bias=v_gd_dt_bias, v_gd_norm_w=v_gd_norm_w, v_w_out=v_w_out, v_final_norm_w=v_final_norm_w)
    weights = {n: given[n] for n in TWIN_WEIGHTS}
    shared = {n: given[n] for n in SHARED_INPUTS}
    per_example = {n: given[n] for n in ['x']}
    grad_fn = _jax.value_and_grad(_loss, argnums=(0, 1))

    def one_microbatch(ex, loss_target):
        ex = dict(ex)
        diff = ex.pop(TWIN_DIFF_INPUT)
        return grad_fn(weights, diff, {**shared, **ex}, loss_target)

    if N_MICROBATCH == 1:
        loss, (grad_w, grad_x) = one_microbatch(per_example, given["loss_target"])
    else:
        def body(carry, xs):
            loss_sum, grad_sum = carry
            l_k, (gw_k, gx_k) = one_microbatch(xs[0], xs[1])
            with _jax.named_scope("update"):
                return (loss_sum + l_k, _jax.tree.map(_jnp.add, grad_sum, gw_k)), gx_k

        init = (_jnp.zeros((), _jnp.float32), _jax.tree.map(_jnp.zeros_like, weights))
        (loss, grad_w), grad_x = _jax.lax.scan(body, init, (per_example, given["loss_target"]))
    with _jax.named_scope("update"):
        delta_w, new_m, new_v = {}, {}, {}
        for n in TWIN_WEIGHTS:
            delta_w[n], new_m[n], new_v[n] = _adamw(weights[n], grad_w[n], given["m_" + n], given["v_" + n])
    return (loss, grad_x, *[grad_w[n] for n in TWIN_WEIGHTS], *[delta_w[n] for n in TWIN_WEIGHTS],
            *[new_m[n] for n in TWIN_WEIGHTS], *[new_v[n] for n in TWIN_WEIGHTS])
```

```python
import functools

import jax
import jax.numpy as jnp
from jax import lax
from jax.experimental import pallas as pl
from jax.experimental.pallas import tpu as pltpu

f32 = jnp.float32
bf16 = jnp.bfloat16
HI = lax.Precision.HIGHEST
MESH = pl.DeviceIdType.MESH

N_DEV = 8
D = 1024
DEPTH = 4
EPS = 1e-6
SEG = 512
HD = 128
NH = 4
LC = 64
D_IN = 5648
DP = 5760
SMALL0 = 5632
RG_TILE = 256
RG_C = 8.0

ADAM_LR = 0.001
ADAM_B1 = 0.9
ADAM_B2 = 0.999
ADAM_EPS = 1e-08
ADAM_WD = 0.01
ADAM_STEP = 10


def _cp(vmem_mb):
    return pltpu.CompilerParams(vmem_limit_bytes=vmem_mb * 2 ** 20)


def _dot(a, b, ca, cb):
    return lax.dot_general(a.astype(bf16), b.astype(bf16), (((ca,), (cb,)), ((), ())), preferred_element_type=f32)


@jax.custom_vjp
def mm_nn(a, b):
    return _dot(a, b, 1, 0)


@jax.custom_vjp
def mm_nt(a, b):
    return _dot(a, b, 1, 1)


@jax.custom_vjp
def mm_tn(a, b):
    return _dot(a, b, 0, 0)


mm_nn.defvjp(lambda a, b: (mm_nn(a, b), (a, b)), lambda r, g: (mm_nt(g, r[1]), mm_tn(r[0], g)))
mm_nt.defvjp(lambda a, b: (mm_nt(a, b), (a, b)), lambda r, g: (mm_nn(g, r[1]), mm_tn(g, r[0])))
mm_tn.defvjp(lambda a, b: (mm_tn(a, b), (a, b)), lambda r, g: (mm_nt(r[1], g), mm_nn(r[0], g)))


def hdot(a, b):
    return lax.dot_general(a, b, (((1,), (0,)), ((), ())), preferred_element_type=f32, precision=HI)


def _tri(n, strict=False):
    r = lax.broadcasted_iota(jnp.int32, (n, n), 0)
    c = lax.broadcasted_iota(jnp.int32, (n, n), 1)
    return (r > c) if strict else (r >= c)


def _lane_col(v, j):
    lane = lax.broadcasted_iota(jnp.int32, v.shape, 1)
    return jnp.sum(jnp.where(lane == j, v, 0.0), axis=1, keepdims=True)


def _rows_from(pieces, rows, width):
    ridx = lax.broadcasted_iota(jnp.int32, (rows, width), 0)
    out = jnp.zeros((rows, width), f32)
    for h, p in enumerate(pieces):
        out = out + jnp.where(ridx == h, jnp.broadcast_to(p, (rows, width)), 0.0)
    return out


def causal_conv(halo8, x, w4):
    t = x.shape[0]
    xe = jnp.concatenate([halo8, x], axis=0)
    y = xe[5:5 + t] * w4[0:1]
    for k in range(1, 4):
        y = y + xe[5 + k:5 + k + t] * w4[k:k + 1]
    return y


def ml_chunk(q, k, v, o_pre, z, small, bias_row, norm_w, C, n, m):
    lane = lax.broadcasted_iota(jnp.int32, small.shape, 1)
    pre = small + bias_row
    lg = jnp.where(lane < NH, pre, jnp.where(lane < 2 * NH, jax.nn.log_sigmoid(pre), 0.0))
    tri = _tri(LC).astype(f32)
    bc = hdot(tri, lg)
    lg_t = lg.T
    bc_t = bc.T
    causal = _tri(LC)
    ys, c_new, n_new, m_new_rows = [], [], [], []
    for h in range(NH):
        sl = slice(h * HD, (h + 1) * HD)
        qh = q[:, sl] * (HD ** -0.5)
        kh = k[:, sl]
        vh = v[:, sl]
        li_col = _lane_col(lg, h)
        b_col = _lane_col(bc, NH + h)
        li_row = lg_t[h:h + 1, :]
        b_row = bc_t[NH + h:NH + h + 1, :]
        m_h = m[h:h + 1, 0:1]
        c_h = C[sl, :]
        n_h = n[h:h + 1, :]
        dmat = jnp.where(causal, b_col - b_row + li_row, -jnp.inf)
        m_inter = b_col + m_h
        m_t = jnp.maximum(m_inter, jnp.max(dmat, axis=-1, keepdims=True))
        p = jnp.exp(dmat - m_t)
        s = mm_nt(qh, kh) * p
        sc = jnp.exp(m_inter - m_t)
        num = mm_nn(s, vh) + sc * mm_nn(qh, c_h)
        den = jnp.sum(s, axis=-1, keepdims=True) + sc * jnp.sum(qh * n_h, axis=-1, keepdims=True)
        hh = num / jnp.maximum(jnp.abs(den), jnp.exp(-m_t))
        g = b_col[LC - 1:LC, :]
        ws = g - b_col + li_col
        m_new = jnp.maximum(g + m_h, jnp.max(ws, axis=0, keepdims=True))
        dec = jnp.exp(g + m_h - m_new)
        wk = jnp.exp(ws - m_new)
        kw = wk * kh
        c_new.append(dec * c_h + mm_tn(kw, vh))
        n_new.append(dec * n_h + jnp.sum(kw, axis=0, keepdims=True))
        m_new_rows.append(jnp.broadcast_to(m_new, (1, HD)))
        hn = hh * lax.rsqrt(jnp.mean(hh * hh, axis=-1, keepdims=True) + EPS) * norm_w[:, sl]
        ys.append(hn * jax.nn.sigmoid(o_pre[:, sl]) * jax.nn.silu(z[:, sl]))
    y = jnp.concatenate(ys, axis=1)
    return y, jnp.concatenate(c_new, axis=0), _rows_from(n_new, 8, HD), _rows_from(m_new_rows, 8, HD)


def _unit_lower_inverse(mat):
    eye = (lax.broadcasted_iota(jnp.int32, (LC, LC), 0) == lax.broadcasted_iota(jnp.int32, (LC, LC), 1)).astype(f32)
    p = -mat
    t = eye + p
    for _ in range(5):
        p = hdot(p, p)
        t = t + hdot(t, p)
    return t


def gd_chunk(qh8, q, kh8, k, vh8, v, z, small, conv_w, alog_row, dt_row, norm_w, st):
    lane = lax.broadcasted_iota(jnp.int32, small.shape, 1)
    is_a = (lane >= 2 * NH) & (lane < 3 * NH)
    g_all = jnp.where(is_a, -jnp.exp(alog_row) * jax.nn.softplus(small + dt_row), 0.0)
    beta_all = jax.nn.sigmoid(small)
    tri = _tri(LC).astype(f32)
    gc_all = hdot(tri, g_all)
    gc_t = gc_all.T
    qc = jax.nn.silu(causal_conv(qh8, q, conv_w[:, 0:SEG]))
    kc = jax.nn.silu(causal_conv(kh8, k, conv_w[:, SEG:2 * SEG]))
    vc = jax.nn.silu(causal_conv(vh8, v, conv_w[:, 2 * SEG:3 * SEG]))
    incl = _tri(LC)
    strict = _tri(LC, strict=True)
    ys, st_new = [], []
    for h in range(NH):
        sl = slice(h * HD, (h + 1) * HD)
        qh = qc[:, sl]
        kh = kc[:, sl]
        vh = vc[:, sl]
        qh = qh * lax.rsqrt(jnp.sum(qh * qh, axis=-1, keepdims=True) + EPS) * (HD ** -0.5)
        kh = kh * lax.rsqrt(jnp.sum(kh * kh, axis=-1, keepdims=True) + EPS)
        beta = _lane_col(beta_all, 3 * NH + h)
        gc = _lane_col(gc_all, 2 * NH + h)
        gc_row = gc_t[2 * NH + h:2 * NH + h + 1, :]
        s_h = st[sl, :]
        gam = jnp.exp(jnp.where(incl, gc - gc_row, -jnp.inf))
        kb = kh * beta
        m_strict = jnp.where(strict, mm_nt(kb, kh) * gam, 0.0)
        t_inv = _unit_lower_inverse(m_strict)
        u = mm_nn(t_inv, vh * beta)
        w = mm_nn(t_inv, kb * jnp.exp(gc))
        aqk = mm_nt(qh, kh) * gam
        q_dec = qh * jnp.exp(gc)
        g_last = gc[LC - 1:LC, :]
        k_dec = kh * jnp.exp(g_last - gc)
        v_new = u - mm_nn(w, s_h)
        o = mm_nn(q_dec, s_h) + mm_nn(aqk, v_new)
        st_new.append(s_h * jnp.exp(g_last) + mm_tn(k_dec, v_new))
        on = o * lax.rsqrt(jnp.mean(o * o, axis=-1, keepdims=True) + EPS) * norm_w
        ys.append(on * jax.nn.silu(z[:, sl]))
    return jnp.concatenate(ys, axis=1), jnp.concatenate(st_new, axis=0)


def rg_pre(xh8, x, conv_w, conv_b, gw_r, gw_i, gate_b, lam):
    xc = causal_conv(xh8, x, conv_w) + conv_b
    r = jax.nn.sigmoid(mm_nn(xc, gw_r) + gate_b[0:1])
    i = jax.nn.sigmoid(mm_nn(xc, gw_i) + gate_b[1:2])
    log_a = -RG_C * r * jax.nn.softplus(-lam)
    a = jnp.exp(log_a)
    th = jnp.tanh(log_a)
    one_minus_a2 = -2.0 * th / (1.0 - th)
    b = jnp.sqrt(one_minus_a2) * (i * xc)
    return a, b


def _scan_rows(a_ref, b_ref, out_ref, h0, n_rows, reverse):
    n_groups = n_rows // 8
    width = a_ref.shape[1]
    row = lax.broadcasted_iota(jnp.int32, (8, width), 0)

    def body(j, h):
        g = (n_groups - 1 - j) if reverse else j
        r0 = pl.multiple_of(g * 8, 8)
        av = a_ref[pl.ds(r0, 8), :]
        bv = b_ref[pl.ds(r0, 8), :]
        for d in (1, 2, 4):
            sh = (8 - d) if reverse else d
            a_s = pltpu.roll(av, sh, 0)
            b_s = pltpu.roll(bv, sh, 0)
            valid = (row < 8 - d) if reverse else (row >= d)
            bv = jnp.where(valid, av * b_s + bv, bv)
            av = jnp.where(valid, av * a_s, av)
        hv = av * h + bv
        out_ref[pl.ds(r0, 8), :] = hv
        return hv[0:1, :] if reverse else hv[7:8, :]

    return lax.fori_loop(0, n_groups, body, h0)


def _norm_rows(xv):
    r = lax.rsqrt(jnp.mean(xv * xv, axis=-1, keepdims=True) + EPS)
    return xv * r, r


def inproj(x, nw, w):
    s, ts = x.shape[0], 256

    def body(x_ref, nw_ref, w_ref, o_ref):
        xn, _ = _norm_rows(x_ref[...])
        hn = xn * nw_ref[...]
        o_ref[...] = jnp.dot(hn.astype(bf16), w_ref[...], preferred_element_type=f32)

    return pl.pallas_call(
        body, grid=(s // ts,),
        in_specs=[pl.BlockSpec((ts, D), lambda i: (i, 0)), pl.BlockSpec((1, D), lambda i: (0, 0)),
                  pl.BlockSpec((D, DP), lambda i: (0, 0))],
        out_specs=pl.BlockSpec((ts, DP), lambda i: (i, 0)),
        out_shape=jax.ShapeDtypeStruct((s, DP), f32), name="inproj", compiler_params=_cp(56))(x, nw, w)


def outproj(x, yr, ym, yg, wo):
    s, ts = x.shape[0], 256

    def body(x_ref, yr_ref, ym_ref, yg_ref, wo_ref, o_ref):
        acc = x_ref[...]
        for j, y_ref in enumerate((yr_ref, ym_ref, yg_ref)):
            acc = acc + jnp.dot(y_ref[...].astype(bf16), wo_ref[j * SEG:(j + 1) * SEG, :], preferred_element_type=f32)
        o_ref[...] = acc

    yspec = pl.BlockSpec((ts, SEG), lambda i: (i, 0))
    return pl.pallas_call(
        body, grid=(s // ts,),
        in_specs=[pl.BlockSpec((ts, D), lambda i: (i, 0)), yspec, yspec, yspec,
                  pl.BlockSpec((3 * SEG, D), lambda i: (0, 0))],
        out_specs=pl.BlockSpec((ts, D), lambda i: (i, 0)),
        out_shape=jax.ShapeDtypeStruct((s, D), f32), name="outproj", compiler_params=_cp(40))(x, yr, ym, yg, wo)


def head(x, fw, tgt):
    s, ts = x.shape[0], 256

    def body(x_ref, fw_ref, t_ref, dx_ref, loss_ref, dfw_ref):
        @pl.when(pl.program_id(0) == 0)
        def _():
            loss_ref[...] = jnp.zeros_like(loss_ref)
            dfw_ref[...] = jnp.zeros_like(dfw_ref)

        xn, r = _norm_rows(x_ref[...])
        fw_v = fw_ref[...]
        err = xn * fw_v - t_ref[...]
        loss_ref[...] += 0.5 * jnp.sum(jnp.mean(err * err, axis=-1, keepdims=True))
        dy = err * (1.0 / D)
        dfw_ref[0:1, :] += jnp.sum(dy * xn, axis=0, keepdims=True)
        dxn = dy * fw_v
        dx_ref[...] = r * (dxn - xn * jnp.mean(dxn * xn, axis=-1, keepdims=True))

    tile = pl.BlockSpec((ts, D), lambda i: (i, 0))
    return pl.pallas_call(
        body, grid=(s // ts,),
        in_specs=[tile, pl.BlockSpec((1, D), lambda i: (0, 0)), tile],
        out_specs=[tile, pl.BlockSpec((8, 128), lambda i: (0, 0)), pl.BlockSpec((8, D), lambda i: (0, 0))],
        out_shape=[jax.ShapeDtypeStruct((s, D), f32), jax.ShapeDtypeStruct((8, 128), f32),
                   jax.ShapeDtypeStruct((8, D), f32)], name="head")(x, fw, tgt)


def outproj_bwd(dx, yr, ym, yg, wo):
    s, ts = dx.shape[0], 256

    def body(dx_ref, yr_ref, ym_ref, yg_ref, wo_ref, dyr_ref, dym_ref, dyg_ref, gwo_ref):
        @pl.when(pl.program_id(0) == 0)
        def _():
            gwo_ref[...] = jnp.zeros_like(gwo_ref)

        dxb = dx_ref[...].astype(bf16)
        for j, (y_ref, dy_ref) in enumerate(((yr_ref, dyr_ref), (ym_ref, dym_ref), (yg_ref, dyg_ref))):
            rows = slice(j * SEG, (j + 1) * SEG)
            dy_ref[...] = lax.dot_general(dxb, wo_ref[rows, :], (((1,), (1,)), ((), ())), preferred_element_type=f32)
            gwo_ref[rows, :] += lax.dot_general(y_ref[...].astype(bf16), dxb, (((0,), (0,)), ((), ())),
                                                preferred_element_type=f32)

    yspec = pl.BlockSpec((ts, SEG), lambda i: (i, 0))
    wspec = pl.BlockSpec((3 * SEG, D), lambda i: (0, 0))
    ysh = jax.ShapeDtypeStruct((s, SEG), f32)
    return pl.pallas_call(
        body, grid=(s // ts,),
        in_specs=[pl.BlockSpec((ts, D), lambda i: (i, 0)), yspec, yspec, yspec, wspec],
        out_specs=[yspec, yspec, yspec, wspec],
        out_shape=[ysh, ysh, ysh, jax.ShapeDtypeStruct((3 * SEG, D), f32)],
        name="outproj_bwd", compiler_params=_cp(48))(dx, yr, ym, yg, wo)


def inproj_bwd_x(x, nw, w, dxo, d_rg, d_ml, d_gd, d_sa, d_sb):
    s, ts = x.shape[0], 256
    widths = (d_rg.shape[1], d_ml.shape[1], d_gd.shape[1], HD)

    def body(x_ref, nw_ref, w_ref, dxo_ref, rg_ref, ml_ref, gd_ref, sa_ref, sb_ref, dx_ref, dnw_ref):
        @pl.when(pl.program_id(0) == 0)
        def _():
            dnw_ref[...] = jnp.zeros_like(dnw_ref)

        xn, r = _norm_rows(x_ref[...])
        pieces = (rg_ref[...], ml_ref[...], gd_ref[...], sa_ref[...] + sb_ref[...])
        dhn = jnp.zeros((ts, D), f32)
        c0 = 0
        for piece, wd in zip(pieces, widths):
            dhn = dhn + lax.dot_general(piece.astype(bf16), w_ref[:, c0:c0 + wd], (((1,), (1,)), ((), ())),
                                        preferred_element_type=f32)
            c0 += wd
        dnw_ref[0:1, :] += jnp.sum(dhn * xn, axis=0, keepdims=True)
        dxn = dhn * nw_ref[...]
        dx_ref[...] = dxo_ref[...] + r * (dxn - xn * jnp.mean(dxn * xn, axis=-1, keepdims=True))

    tile = pl.BlockSpec((ts, D), lambda i: (i, 0))
    return pl.pallas_call(
        body, grid=(s // ts,),
        in_specs=[tile, pl.BlockSpec((1, D), lambda i: (0, 0)), pl.BlockSpec((D, DP), lambda i: (0, 0)), tile]
        + [pl.BlockSpec((ts, wd), lambda i: (i, 0)) for wd in widths] + [pl.BlockSpec((ts, HD), lambda i: (i, 0))],
        out_specs=[tile, pl.BlockSpec((8, D), lambda i: (0, 0))],
        out_shape=[jax.ShapeDtypeStruct((s, D), f32), jax.ShapeDtypeStruct((8, D), f32)],
        name="inproj_bwd_x", compiler_params=_cp(56))(x, nw, w, dxo, d_rg, d_ml, d_gd, d_sa, d_sb)


def wgrad(x, nw, dps, name):
    s, ts = x.shape[0], 512
    c = dps[0].shape[1]
    ct = min(c, SEG)
    n_dp = len(dps)

    def body(*refs):
        x_ref, nw_ref = refs[0], refs[1]
        dp_refs = refs[2:2 + n_dp]
        o_ref = refs[2 + n_dp]

        @pl.when(pl.program_id(1) == 0)
        def _():
            o_ref[...] = jnp.zeros_like(o_ref)

        xn, _ = _norm_rows(x_ref[...])
        hn = (xn * nw_ref[...]).astype(bf16)
        dp = dp_refs[0][...]
        for extra in dp_refs[1:]:
            dp = dp + extra[...]
        o_ref[...] += lax.dot_general(hn, dp.astype(bf16), (((0,), (0,)), ((), ())), preferred_element_type=f32)

    return pl.pallas_call(
        body, grid=(c // ct, s // ts),
        in_specs=[pl.BlockSpec((ts, D), lambda j, i: (i, 0)), pl.BlockSpec((1, D), lambda j, i: (0, 0))]
        + [pl.BlockSpec((ts, ct), lambda j, i: (i, j)) for _ in dps],
        out_specs=pl.BlockSpec((D, ct), lambda j, i: (0, j)),
        out_shape=jax.ShapeDtypeStruct((D, c), f32), name=name, compiler_params=_cp(40))(x, nw, *dps)


def _seg_spec(rows, seg, n_tiles=None):
    if n_tiles is None:
        return pl.BlockSpec((rows, SEG), lambda i: (i, seg))
    return pl.BlockSpec((rows, SEG), lambda i: (n_tiles - 1 - i, seg))


def _halo_spec(rows, seg, n_tiles=None):
    per = rows // 8
    if n_tiles is None:
        return pl.BlockSpec((8, SEG), lambda i: (jnp.maximum(i * per - 1, 0), seg))
    return pl.BlockSpec((8, SEG), lambda i: (jnp.maximum((n_tiles - 1 - i) * per - 1, 0), seg))


def _const_spec(shape):
    return pl.BlockSpec(shape, lambda i: tuple(0 for _ in shape))


def rglru_fwd(proj, conv_w, conv_b, gw_r, gw_i, gate_b, lam):
    s = proj.shape[0]
    tr = RG_TILE

    def body(xh_ref, x_ref, z_ref, cw_ref, cb_ref, gr_ref, gi_ref, gb_ref, lam_ref, y_ref, h_ref, a_s, b_s, hc):
        first = pl.program_id(0) == 0

        @pl.when(first)
        def _():
            hc[...] = jnp.zeros_like(hc)

        xh = jnp.where(first, 0.0, xh_ref[...])
        a, b = rg_pre(xh, x_ref[...], cw_ref[...], cb_ref[...], gr_ref[...], gi_ref[...], gb_ref[...], lam_ref[...])
        a_s[...] = a
        b_s[...] = b
        hc[0:1, :] = _scan_rows(a_s, b_s, h_ref, hc[0:1, :], tr, False)
        y_ref[...] = h_ref[...] * jax.nn.silu(z_ref[...])

    out = pl.BlockSpec((tr, SEG), lambda i: (i, 0))
    return pl.pallas_call(
        body, grid=(s // tr,),
        in_specs=[_halo_spec(tr, 0), _seg_spec(tr, 0), _seg_spec(tr, 1), _const_spec((4, SEG)), _const_spec((1, SEG)),
                  _const_spec((SEG, SEG)), _const_spec((SEG, SEG)), _const_spec((2, SEG)), _const_spec((1, SEG))],
        out_specs=[out, out],
        out_shape=[jax.ShapeDtypeStruct((s, SEG), f32), jax.ShapeDtypeStruct((s, SEG), f32)],
        scratch_shapes=[pltpu.VMEM((tr, SEG), f32), pltpu.VMEM((tr, SEG), f32), pltpu.VMEM((8, SEG), f32)],
        name="rglru_fwd", compiler_params=_cp(40))(proj, proj, proj, conv_w, conv_b, gw_r, gw_i, gate_b, lam)


def rglru_bwd(proj, hs, dy, conv_w, conv_b, gw_r, gw_i, gate_b, lam):
    s = proj.shape[0]
    tr = RG_TILE
    nt = s // tr

    def body(xh_ref, x_ref, z_ref, hh_ref, h_ref, dy_ref, cw_ref, cb_ref, gr_ref, gi_ref, gb_ref, lam_ref,
             dp_ref, dcw_ref, dcb_ref, dgr_ref, dgi_ref, dgb_ref, dlam_ref,
             an_s, g_s, dh_s, a_first, dh_first, dhalo):
        i = pl.program_id(0)
        first_tile = i == nt - 1

        @pl.when(i == 0)
        def _():
            for ref in (dcw_ref, dcb_ref, dgr_ref, dgi_ref, dgb_ref, dlam_ref, a_first, dh_first, dhalo):
                ref[...] = jnp.zeros_like(ref)

        xh = jnp.where(first_tile, 0.0, xh_ref[...])
        params = (cw_ref[...], cb_ref[...], gr_ref[...], gi_ref[...], gb_ref[...], lam_ref[...])
        (a, _), vjp = jax.vjp(rg_pre, xh, x_ref[...], *params)
        zv = z_ref[...]
        hv = h_ref[...]
        dyv = dy_ref[...]
        sig = jax.nn.sigmoid(zv)
        g_s[...] = dyv * (zv * sig)
        dp_ref[:, SEG:2 * SEG] = dyv * hv * (sig * (1.0 + zv * (1.0 - sig)))
        ridx = lax.broadcasted_iota(jnp.int32, (tr, SEG), 0)
        an_s[...] = jnp.where(ridx == tr - 1, jnp.broadcast_to(a_first[0:1, :], (tr, SEG)), pltpu.roll(a, tr - 1, 0))
        _scan_rows(an_s, g_s, dh_s, dh_first[0:1, :], tr, True)
        dh = dh_s[...]
        h_prev_last = jnp.where(first_tile, 0.0, hh_ref[...])[7:8, :]
        h_prev = pltpu.roll(hv, 1, 0)
        h_prev = jnp.where(ridx == 0, jnp.broadcast_to(h_prev_last, (tr, SEG)), h_prev)
        dxh, dx, dcw, dcb, dgr, dgi, dgb, dlam = vjp((dh * h_prev, dh))
        dp_ref[:, 0:SEG] = dx
        dp_ref[tr - 8:tr, 0:SEG] += dhalo[...]
        dhalo[...] = dxh
        a_first[0:1, :] = a[0:1, :]
        dh_first[0:1, :] = dh[0:1, :]
        dcw_ref[...] += dcw
        dcb_ref[...] += dcb
        dgr_ref[...] += dgr
        dgi_ref[...] += dgi
        dgb_ref[...] += dgb
        dlam_ref[...] += dlam

    pspecs = [_const_spec((4, SEG)), _const_spec((1, SEG)), _const_spec((SEG, SEG)), _const_spec((SEG, SEG)),
              _const_spec((2, SEG)), _const_spec((1, SEG))]
    pshapes = [jax.ShapeDtypeStruct(sh, f32) for sh in ((4, SEG), (1, SEG), (SEG, SEG), (SEG, SEG), (2, SEG), (1, SEG))]
    tile = pl.BlockSpec((tr, SEG), lambda i: (nt - 1 - i, 0))
    return pl.pallas_call(
        body, grid=(nt,),
        in_specs=[_halo_spec(tr, 0, nt), _seg_spec(tr, 0, nt), _seg_spec(tr, 1, nt),
                  pl.BlockSpec((8, SEG), lambda i: (jnp.maximum((nt - 1 - i) * (tr // 8) - 1, 0), 0)), tile, tile] + pspecs,
        out_specs=[pl.BlockSpec((tr, 2 * SEG), lambda i: (nt - 1 - i, 0))] + pspecs,
        out_shape=[jax.ShapeDtypeStruct((s, 2 * SEG), f32)] + pshapes,
        scratch_shapes=[pltpu.VMEM((tr, SEG), f32), pltpu.VMEM((tr, SEG), f32), pltpu.VMEM((tr, SEG), f32),
                        pltpu.VMEM((8, SEG), f32), pltpu.VMEM((8, SEG), f32), pltpu.VMEM((8, SEG), f32)],
        name="rglru_bwd", compiler_params=_cp(48))(proj, proj, proj, hs, hs, dy, conv_w, conv_b, gw_r, gw_i, gate_b, lam)


ML_SEGS = (2, 3, 4, 5, 6)
SMALL_BLK = SMALL0 // HD


def mlstm_fwd(proj, bias_row, norm_w):
    s = proj.shape[0]
    nc = s // LC

    def body(q_ref, k_ref, v_ref, o_ref, z_ref, sm_ref, b_ref, nw_ref, y_ref, cs_ref, ns_ref, ms_ref, c_s, n_s, m_s):
        @pl.when(pl.program_id(0) == 0)
        def _():
            c_s[...] = jnp.zeros_like(c_s)
            n_s[...] = jnp.zeros_like(n_s)
            m_s[...] = jnp.zeros_like(m_s)

        cs_ref[0] = c_s[...]
        ns_ref[0] = n_s[...]
        ms_ref[0] = m_s[...]
        y, c2, n2, m2 = ml_chunk(q_ref[...], k_ref[...], v_ref[...], o_ref[...], z_ref[...], sm_ref[...],
                                 b_ref[...], nw_ref[...], c_s[...], n_s[...], m_s[...])
        y_ref[...] = y
        c_s[...] = c2
        n_s[...] = n2
        m_s[...] = m2

    return pl.pallas_call(
        body, grid=(nc,),
        in_specs=[_seg_spec(LC, sg) for sg in ML_SEGS]
        + [pl.BlockSpec((LC, HD), lambda i: (i, SMALL_BLK)), _const_spec((1, HD)), _const_spec((1, SEG))],
        out_specs=[pl.BlockSpec((LC, SEG), lambda i: (i, 0)), pl.BlockSpec((1, SEG, HD), lambda i: (i, 0, 0)),
                   pl.BlockSpec((1, 8, HD), lambda i: (i, 0, 0)), pl.BlockSpec((1, 8, HD), lambda i: (i, 0, 0))],
        out_shape=[jax.ShapeDtypeStruct((s, SEG), f32), jax.ShapeDtypeStruct((nc, SEG, HD), f32),
                   jax.ShapeDtypeStruct((nc, 8, HD), f32), jax.ShapeDtypeStruct((nc, 8, HD), f32)],
        scratch_shapes=[pltpu.VMEM((SEG, HD), f32), pltpu.VMEM((8, HD), f32), pltpu.VMEM((8, HD), f32)],
        name="mlstm_fwd")(proj, proj, proj, proj, proj, proj, bias_row, norm_w)


def mlstm_bwd(proj, dy, cs, ns, ms, bias_row, norm_w):
    s = proj.shape[0]
    nc = s // LC

    def body(q_ref, k_ref, v_ref, o_ref, z_ref, sm_ref, dy_ref, cs_ref, ns_ref, ms_ref, b_ref, nw_ref,
             dp_ref, dsm_ref, db_ref, dnw_ref, dc_s, dn_s, dm_s):
        @pl.when(pl.program_id(0) == 0)
        def _():
            for ref in (db_ref, dnw_ref, dc_s, dn_s, dm_s):
                ref[...] = jnp.zeros_like(ref)

        _, vjp = jax.vjp(ml_chunk, q_ref[...], k_ref[...], v_ref[...], o_ref[...], z_ref[...], sm_ref[...],
                         b_ref[...], nw_ref[...], cs_ref[0], ns_ref[0], ms_ref[0])
        dq, dk, dv, do, dz, dsm, db, dnw, dc, dn, dm = vjp((dy_ref[...], dc_s[...], dn_s[...], dm_s[...]))
        for j, val in enumerate((dq, dk, dv, do, dz)):
            dp_ref[:, j * SEG:(j + 1) * SEG] = val
        dsm_ref[...] = dsm
        db_ref[0:1, :] += db
        dnw_ref[0:1, :] += dnw
        dc_s[...] = dc
        dn_s[...] = dn
        dm_s[...] = dm

    rev3 = lambda i: (nc - 1 - i, 0, 0)
    return pl.pallas_call(
        body, grid=(nc,),
        in_specs=[_seg_spec(LC, sg, nc) for sg in ML_SEGS]
        + [pl.BlockSpec((LC, HD), lambda i: (nc - 1 - i, SMALL_BLK)), pl.BlockSpec((LC, SEG), lambda i: (nc - 1 - i, 0)),
           pl.BlockSpec((1, SEG, HD), rev3), pl.BlockSpec((1, 8, HD), rev3), pl.BlockSpec((1, 8, HD), rev3),
           _const_spec((1, HD)), _const_spec((1, SEG))],
        out_specs=[pl.BlockSpec((LC, 5 * SEG), lambda i: (nc - 1 - i, 0)), pl.BlockSpec((LC, HD), lambda i: (nc - 1 - i, 0)),
                   _const_spec((8, HD)), _const_spec((8, SEG))],
        out_shape=[jax.ShapeDtypeStruct((s, 5 * SEG), f32), jax.ShapeDtypeStruct((s, HD), f32),
                   jax.ShapeDtypeStruct((8, HD), f32), jax.ShapeDtypeStruct((8, SEG), f32)],
        scratch_shapes=[pltpu.VMEM((SEG, HD), f32), pltpu.VMEM((8, HD), f32), pltpu.VMEM((8, HD), f32)],
        name="mlstm_bwd", compiler_params=_cp(48))(proj, proj, proj, proj, proj, proj, dy, cs, ns, ms, bias_row, norm_w)


GD_SEGS = (7, 8, 9)


def gdn_fwd(proj, conv_w, alog_row, dt_row, norm_w):
    s = proj.shape[0]
    nc = s // LC

    def body(qh_ref, q_ref, kh_ref, k_ref, vh_ref, v_ref, z_ref, sm_ref, cw_ref, al_ref, dt_ref, nw_ref,
             y_ref, ss_ref, st_s):
        first = pl.program_id(0) == 0

        @pl.when(first)
        def _():
            st_s[...] = jnp.zeros_like(st_s)

        ss_ref[0] = st_s[...]
        halo = [jnp.where(first, 0.0, r[...]) for r in (qh_ref, kh_ref, vh_ref)]
        y, st2 = gd_chunk(halo[0], q_ref[...], halo[1], k_ref[...], halo[2], v_ref[...], z_ref[...], sm_ref[...],
                          cw_ref[...], al_ref[...], dt_ref[...], nw_ref[...], st_s[...])
        y_ref[...] = y
        st_s[...] = st2

    qkv_specs = []
    for sg in GD_SEGS:
        qkv_specs += [_halo_spec(LC, sg), _seg_spec(LC, sg)]
    return pl.pallas_call(
        body, grid=(nc,),
        in_specs=qkv_specs + [_seg_spec(LC, 10), pl.BlockSpec((LC, HD), lambda i: (i, SMALL_BLK)),
                              _const_spec((4, 3 * SEG)), _const_spec((1, HD)), _const_spec((1, HD)), _const_spec((1, HD))],
        out_specs=[pl.BlockSpec((LC, SEG), lambda i: (i, 0)), pl.BlockSpec((1, SEG, HD), lambda i: (i, 0, 0))],
        out_shape=[jax.ShapeDtypeStruct((s, SEG), f32), jax.ShapeDtypeStruct((nc, SEG, HD), f32)],
        scratch_shapes=[pltpu.VMEM((SEG, HD), f32)],
        name="gdn_fwd")(proj, proj, proj, proj, proj, proj, proj, proj, conv_w, alog_row, dt_row, norm_w)


def gdn_bwd(proj, dy, ss, conv_w, alog_row, dt_row, norm_w):
    s = proj.shape[0]
    nc = s // LC

    def body(qh_ref, q_ref, kh_ref, k_ref, vh_ref, v_ref, z_ref, sm_ref, dy_ref, ss_ref, cw_ref, al_ref, dt_ref, nw_ref,
             dp_ref, dsm_ref, dcw_ref, dal_ref, ddt_ref, dnw_ref, dst_s, dhalo):
        i = pl.program_id(0)
        first_chunk = i == nc - 1

        @pl.when(i == 0)
        def _():
            for ref in (dcw_ref, dal_ref, ddt_ref, dnw_ref, dst_s, dhalo):
                ref[...] = jnp.zeros_like(ref)

        halo = [jnp.where(first_chunk, 0.0, r[...]) for r in (qh_ref, kh_ref, vh_ref)]
        _, vjp = jax.vjp(gd_chunk, halo[0], q_ref[...], halo[1], k_ref[...], halo[2], v_ref[...], z_ref[...], sm_ref[...],
                         cw_ref[...], al_ref[...], dt_ref[...], nw_ref[...], ss_ref[0])
        dqh, dq, dkh, dk, dvh, dv, dz, dsm, dcw, dal, ddt, dnw, dst = vjp((dy_ref[...], dst_s[...]))
        for j, val in enumerate((dq, dk, dv, dz)):
            dp_ref[:, j * SEG:(j + 1) * SEG] = val
        dp_ref[LC - 8:LC, 0:3 * SEG] += dhalo[...]
        for j, val in enumerate((dqh, dkh, dvh)):
            dhalo[:, j * SEG:(j + 1) * SEG] = val
        dsm_ref[...] = dsm
        dcw_ref[...] += dcw
        dal_ref[0:1, :] += dal
        ddt_ref[0:1, :] += ddt
        dnw_ref[0:1, :] += dnw
        dst_s[...] = dst

    qkv_specs = []
    for sg in GD_SEGS:
        qkv_specs += [_halo_spec(LC, sg, nc), _seg_spec(LC, sg, nc)]
    return pl.pallas_call(
        body, grid=(nc,),
        in_specs=qkv_specs + [_seg_spec(LC, 10, nc), pl.BlockSpec((LC, HD), lambda i: (nc - 1 - i, SMALL_BLK)),
                              pl.BlockSpec((LC, SEG), lambda i: (nc - 1 - i, 0)),
                              pl.BlockSpec((1, SEG, HD), lambda i: (nc - 1 - i, 0, 0)),
                              _const_spec((4, 3 * SEG)), _const_spec((1, HD)), _const_spec((1, HD)), _const_spec((1, HD))],
        out_specs=[pl.BlockSpec((LC, 4 * SEG), lambda i: (nc - 1 - i, 0)), pl.BlockSpec((LC, HD), lambda i: (nc - 1 - i, 0)),
                   _const_spec((4, 3 * SEG)), _const_spec((8, HD)), _const_spec((8, HD)), _const_spec((8, HD))],
        out_shape=[jax.ShapeDtypeStruct((s, 4 * SEG), f32), jax.ShapeDtypeStruct((s, HD), f32),
                   jax.ShapeDtypeStruct((4, 3 * SEG), f32), jax.ShapeDtypeStruct((8, HD), f32),
                   jax.ShapeDtypeStruct((8, HD), f32), jax.ShapeDtypeStruct((8, HD), f32)],
        scratch_shapes=[pltpu.VMEM((SEG, HD), f32), pltpu.VMEM((8, 3 * SEG), f32)],
        name="gdn_bwd", compiler_params=_cp(48))(proj, proj, proj, proj, proj, proj, proj, proj, dy, ss,
                                                 conv_w, alog_row, dt_row, norm_w)


def _my_place():
    return lax.axis_index("x"), lax.axis_index("y"), lax.axis_index("c")


def _slot(p):
    return 4 * p[0] + 2 * p[1] + p[2]


def _peer(me, j):
    flips = ((j >> 2) & 1, (j >> 1) & 1, j & 1)
    return tuple((1 - v) if fl else v for v, fl in zip(me, flips))


_ANY = pl.BlockSpec(memory_space=pl.ANY)


def exchange(blocks, by_slot, name):
    n = len(blocks)
    out_shapes = []
    for blk, sl in zip(blocks, by_slot):
        shape = blk.shape if sl else (N_DEV,) + blk.shape
        out_shapes.append(jax.ShapeDtypeStruct(shape, blk.dtype))

    def body(*refs):
        ins, outs = refs[:n], refs[n:2 * n]
        send_sems, recv_sems, local_sems = refs[2 * n:]
        me = _my_place()
        mine = _slot(me)
        local, remote = [], []
        for a in range(n):
            src_own = ins[a].at[mine] if by_slot[a] else ins[a]
            cp = pltpu.make_async_copy(src_own, outs[a].at[mine], local_sems.at[a])
            cp.start()
            local.append(cp)
            for j in range(1, N_DEV):
                peer = _peer(me, j)
                src = ins[a].at[_slot(peer)] if by_slot[a] else ins[a]
                rc = pltpu.make_async_remote_copy(src_ref=src, dst_ref=outs[a].at[mine], send_sem=send_sems.at[a, j - 1],
                                                  recv_sem=recv_sems.at[a, j - 1], device_id=peer, device_id_type=MESH)
                rc.start()
                remote.append(rc)
        for rc in remote:
            rc.wait()
        for cp in local:
            cp.wait()

    return pl.pallas_call(
        body, in_specs=[_ANY] * n, out_specs=[_ANY] * n, out_shape=out_shapes,
        scratch_shapes=[pltpu.SemaphoreType.DMA((n, N_DEV - 1)), pltpu.SemaphoreType.DMA((n, N_DEV - 1)),
                        pltpu.SemaphoreType.DMA((n,))],
        name=name)(*blocks)


def _adamw(w, g, m, v):
    m = ADAM_B1 * m + (1.0 - ADAM_B1) * g
    v = ADAM_B2 * v + (1.0 - ADAM_B2) * (g * g)
    m_hat = m / (1.0 - ADAM_B1 ** ADAM_STEP)
    v_hat = v / (1.0 - ADAM_B2 ** ADAM_STEP)
    delta = -ADAM_LR * (m_hat / (jnp.sqrt(v_hat) + ADAM_EPS) + ADAM_WD * w)
    return delta, m, v


def adam_slots(slots, w, m, v, rows, name):
    _, depth, r, c = slots.shape

    def body(s_ref, w_ref, m_ref, v_ref, g_ref, d_ref, m2_ref, v2_ref):
        g = s_ref[0, 0].astype(f32)
        for k in range(1, N_DEV):
            g = g + s_ref[k, 0].astype(f32)
        d, m2, v2 = _adamw(w_ref[0], g, m_ref[0], v_ref[0])
        g_ref[0] = g
        d_ref[0] = d
        m2_ref[0] = m2
        v2_ref[0] = v2

    blk = pl.BlockSpec((1, rows, c), lambda l, i: (l, i, 0))
    sh = jax.ShapeDtypeStruct((depth, r, c), f32)
    return pl.pallas_call(
        body, grid=(depth, r // rows),
        in_specs=[pl.BlockSpec((N_DEV, 1, rows, c), lambda l, i: (0, l, i, 0)), blk, blk, blk],
        out_specs=[blk] * 4, out_shape=[sh] * 4, name=name, compiler_params=_cp(40))(slots, w, m, v)


def sum_slots(slots, name):
    _, r, c = slots.shape

    def body(s_ref, o_ref):
        g = s_ref[0]
        for k in range(1, N_DEV):
            g = g + s_ref[k]
        o_ref[...] = g

    return pl.pallas_call(body, out_shape=jax.ShapeDtypeStruct((r, c), f32), name=name, compiler_params=_cp(40))(slots)


def adam_flat(g, w, m, v, name):
    def body(g_ref, w_ref, m_ref, v_ref, d_ref, m2_ref, v2_ref):
        d, m2, v2 = _adamw(w_ref[...], g_ref[...], m_ref[...], v_ref[...])
        d_ref[...] = d
        m2_ref[...] = m2
        v2_ref[...] = v2

    sh = jax.ShapeDtypeStruct(g.shape, f32)
    return pl.pallas_call(body, out_shape=[sh] * 3, name=name)(g, w, m, v)


def _pack(arrs):
    flat = jnp.concatenate([a.reshape(-1).astype(f32) for a in arrs])
    n = flat.shape[0]
    rows = -(-n // (8 * 128)) * 8
    flat = jnp.pad(flat, (0, rows * 128 - n))
    return flat.reshape(rows, 128)


def _unpack(packed, shapes):
    flat = packed.reshape(-1)
    out, off = [], 0
    for sh in shapes:
        n = 1
        for dim in sh:
            n *= dim
        out.append(flat[off:off + n].reshape(sh))
        off += n
    return out


def _regroup_cols(w):
    pad = jnp.zeros(w.shape[:-1] + (DP - D_IN,), w.dtype)
    return jnp.concatenate([w[..., :3584], w[..., 3592:5640], w[..., 3584:3592], w[..., 5640:5648], pad], axis=-1)


def _block_diag(gw):
    eye = jnp.eye(8, dtype=gw.dtype)
    return (gw[:, :, None, :] * eye[:, None, :, None]).reshape(SEG, SEG)


def _diag_blocks(dense):
    eye = jnp.eye(8, dtype=dense.dtype)
    return (dense.reshape(8, 64, 8, 64) * eye[:, None, :, None]).sum(axis=2)


def _lane_row(vals, first_lane):
    return jnp.zeros((1, HD), f32).at[0, first_lane:first_lane + NH].set(vals)


def kernel(x, norm_w, w_in, rg_conv_w, rg_conv_b, rg_gate_w, rg_gate_b, rg_lambda, ml_gate_b, ml_norm_w, gd_conv_w, gd_a_log, gd_dt_bias, gd_norm_w, w_out, final_norm_w, loss_target, m_norm_w, m_w_in, m_rg_conv_w, m_rg_conv_b, m_rg_gate_w, m_rg_gate_b, m_rg_lambda, m_ml_gate_b, m_ml_norm_w, m_gd_conv_w, m_gd_a_log, m_gd_dt_bias, m_gd_norm_w, m_w_out, m_final_norm_w, v_norm_w, v_w_in, v_rg_conv_w, v_rg_conv_b, v_rg_gate_w, v_rg_gate_b, v_rg_lambda, v_ml_gate_b, v_ml_norm_w, v_gd_conv_w, v_gd_a_log, v_gd_dt_bias, v_gd_norm_w, v_w_out, v_final_norm_w):
    s = x.shape[1]
    xs = x.reshape(s, D)
    tgt = loss_target.reshape(s, D)
    me = 4 * lax.axis_index("x") + 2 * lax.axis_index("y") + lax.axis_index("c")

    sharded_small = _pack([rg_conv_w, rg_gate_b, gd_conv_w])
    wi_g, wo_g, sm_g = exchange([w_in.astype(bf16), w_out.astype(bf16), sharded_small], [False, False, False], "gather_weights")
    w_in_full = _regroup_cols(wi_g.transpose(1, 2, 0, 3).reshape(DEPTH, D, D_IN))
    w_out_full = wo_g.transpose(1, 0, 2, 3).reshape(DEPTH, 3 * SEG, D)
    sm_parts = [_unpack(sm_g[k], [rg_conv_w.shape, rg_gate_b.shape, gd_conv_w.shape]) for k in range(N_DEV)]
    rg_conv_full = jnp.concatenate([p[0] for p in sm_parts], axis=-1)
    rg_gb_full = jnp.concatenate([p[1] for p in sm_parts], axis=-1)
    gd_conv_full = jnp.concatenate([p[2] for p in sm_parts], axis=-1)

    loss_part, dx, d_fw, g_w_in, g_w_out, g_small = local_step(
        xs, tgt, w_in_full, w_out_full, rg_conv_full, rg_gb_full, gd_conv_full, norm_w, rg_conv_b, rg_gate_w, rg_lambda,
        ml_gate_b, ml_norm_w, gd_a_log, gd_dt_bias, gd_norm_w, final_norm_w)
    given_w = dict(norm_w=norm_w, rg_conv_w=rg_conv_w, rg_conv_b=rg_conv_b, rg_gate_w=rg_gate_w, rg_gate_b=rg_gate_b,
                   rg_lambda=rg_lambda, ml_gate_b=ml_gate_b, ml_norm_w=ml_norm_w, gd_conv_w=gd_conv_w, gd_a_log=gd_a_log,
                   gd_dt_bias=gd_dt_bias, gd_norm_w=gd_norm_w, final_norm_w=final_norm_w, w_in=w_in, w_out=w_out)
    given_m = dict(norm_w=m_norm_w, rg_conv_w=m_rg_conv_w, rg_conv_b=m_rg_conv_b, rg_gate_w=m_rg_gate_w, rg_gate_b=m_rg_gate_b,
                   rg_lambda=m_rg_lambda, ml_gate_b=m_ml_gate_b, ml_norm_w=m_ml_norm_w, gd_conv_w=m_gd_conv_w,
                   gd_a_log=m_gd_a_log, gd_dt_bias=m_gd_dt_bias, gd_norm_w=m_gd_norm_w, final_norm_w=m_final_norm_w,
                   w_in=m_w_in, w_out=m_w_out)
    given_v = dict(norm_w=v_norm_w, rg_conv_w=v_rg_conv_w, rg_conv_b=v_rg_conv_b, rg_gate_w=v_rg_gate_w, rg_gate_b=v_rg_gate_b,
                   rg_lambda=v_rg_lambda, ml_gate_b=v_ml_gate_b, ml_norm_w=v_ml_norm_w, gd_conv_w=v_gd_conv_w,
                   gd_a_log=v_gd_a_log, gd_dt_bias=v_gd_dt_bias, gd_norm_w=v_gd_norm_w, final_norm_w=v_final_norm_w,
                   w_in=v_w_in, w_out=v_w_out)
    return finish_step(loss_part, dx, d_fw, g_w_in, g_w_out, g_small, s, me, given_w, given_m, given_v)


def local_step(xs, tgt, w_in_full, w_out_full, rg_conv_full, rg_gb_full, gd_conv_full, norm_w, rg_conv_b, rg_gate_w,
               rg_lambda, ml_gate_b, ml_norm_w, gd_a_log, gd_dt_bias, gd_norm_w, final_norm_w):
    acts = []
    for l in range(DEPTH):
        nw = norm_w[l].reshape(1, D)
        proj = inproj(xs, nw, w_in_full[l])
        rg_p = (rg_conv_full[l], rg_conv_b[l].reshape(1, SEG), _block_diag(rg_gate_w[l, 0]), _block_diag(rg_gate_w[l, 1]),
                rg_gb_full[l], rg_lambda[l].reshape(1, SEG))
        y_rg, hs = rglru_fwd(proj, *rg_p)
        ml_p = (jnp.zeros((1, HD), f32).at[0, 0:2 * NH].set(ml_gate_b[l].reshape(-1)), ml_norm_w[l].reshape(1, SEG))
        y_ml, cs, ns, ms = mlstm_fwd(proj, *ml_p)
        gd_p = (gd_conv_full[l], _lane_row(gd_a_log[l], 2 * NH), _lane_row(gd_dt_bias[l], 2 * NH), gd_norm_w[l].reshape(1, HD))
        y_gd, ss = gdn_fwd(proj, *gd_p)
        acts.append((xs, nw, proj, rg_p, y_rg, hs, ml_p, y_ml, cs, ns, ms, gd_p, y_gd, ss))
        xs = outproj(xs, y_rg, y_ml, y_gd, w_out_full[l])

    dx, loss_part, d_fw = head(xs, final_norm_w.reshape(1, D), tgt)

    g_w_in, g_w_out = [None] * DEPTH, [None] * DEPTH
    g_small = {k: [None] * DEPTH for k in ("norm_w", "rg_conv_w", "rg_conv_b", "rg_gate_w", "rg_gate_b", "rg_lambda",
                                           "ml_gate_b", "ml_norm_w", "gd_conv_w", "gd_a_log", "gd_dt_bias", "gd_norm_w")}
    for l in reversed(range(DEPTH)):
        x_l, nw, proj, rg_p, y_rg, hs, ml_p, y_ml, cs, ns, ms, gd_p, y_gd, ss = acts[l]
        dy_rg, dy_ml, dy_gd, g_wo = outproj_bwd(dx, y_rg, y_ml, y_gd, w_out_full[l])
        d_rg, d_cw, d_cb, d_gr, d_gi, d_gb, d_lam = rglru_bwd(proj, hs, dy_rg, *rg_p)
        d_ml, d_sm_ml, d_bias, d_mnw = mlstm_bwd(proj, dy_ml, cs, ns, ms, *ml_p)
        d_gd, d_sm_gd, d_gcw, d_al, d_dt, d_gnw = gdn_bwd(proj, dy_gd, ss, *gd_p)
        dx, d_nw = inproj_bwd_x(x_l, nw, w_in_full[l], dx, d_rg, d_ml, d_gd, d_sm_ml, d_sm_gd)
        gw_rg = wgrad(x_l, nw, [d_rg], "wgrad_rg")
        gw_ml = wgrad(x_l, nw, [d_ml], "wgrad_ml")
        gw_gd = wgrad(x_l, nw, [d_gd], "wgrad_gd")
        gw_sm = wgrad(x_l, nw, [d_sm_ml, d_sm_gd], "wgrad_small")
        g_w_in[l] = jnp.concatenate([gw_rg, gw_ml, gw_sm[:, 0:2 * NH], gw_gd, gw_sm[:, 2 * NH:4 * NH]], axis=-1)
        g_w_out[l] = g_wo
        g_small["norm_w"][l] = d_nw[0]
        g_small["rg_conv_w"][l] = d_cw
        g_small["rg_conv_b"][l] = d_cb[0]
        g_small["rg_gate_w"][l] = jnp.stack([_diag_blocks(d_gr), _diag_blocks(d_gi)])
        g_small["rg_gate_b"][l] = d_gb
        g_small["rg_lambda"][l] = d_lam[0]
        g_small["ml_gate_b"][l] = d_bias[0, 0:2 * NH].reshape(2, NH)
        g_small["ml_norm_w"][l] = d_mnw[0]
        g_small["gd_conv_w"][l] = d_gcw
        g_small["gd_a_log"][l] = d_al[0, 2 * NH:3 * NH]
        g_small["gd_dt_bias"][l] = d_dt[0, 2 * NH:3 * NH]
        g_small["gd_norm_w"][l] = d_gnw[0]
    return loss_part, dx, d_fw, g_w_in, g_w_out, g_small


def finish_step(loss_part, dx, d_fw, g_w_in, g_w_out, g_small, s, me, given_w, given_m, given_v):
    gwi = jnp.stack(g_w_in).reshape(DEPTH, D, N_DEV, D_IN // N_DEV).transpose(2, 0, 1, 3).astype(bf16)
    gwo = jnp.stack(g_w_out).reshape(DEPTH, N_DEV, 3 * SEG // N_DEV, D).transpose(1, 0, 2, 3).astype(bf16)
    small_names = ["norm_w", "rg_conv_w", "rg_conv_b", "rg_gate_w", "rg_gate_b", "rg_lambda", "ml_gate_b", "ml_norm_w",
                   "gd_conv_w", "gd_a_log", "gd_dt_bias", "gd_norm_w"]
    small_list = [loss_part[0, 0:1], d_fw[0]] + [jnp.stack(g_small[k]) for k in small_names]
    small_shapes = [a.shape for a in small_list]
    gwi_r, gwo_r, gsm_r = exchange([gwi, gwo, _pack(small_list)], [True, True, False], "scatter_grads")

    g_wi, d_wi, m_wi, v_wi = adam_slots(gwi_r, given_w["w_in"], given_m["w_in"], given_v["w_in"], 256, "adam_w_in")
    g_wo, d_wo, m_wo, v_wo = adam_slots(gwo_r, given_w["w_out"], given_m["w_out"], given_v["w_out"], 192, "adam_w_out")
    g_all = _unpack(sum_slots(gsm_r, "sum_small"), small_shapes)
    loss = g_all[0][0]
    grads = {"final_norm_w": g_all[1]}
    for k, g in zip(small_names, g_all[2:]):
        grads[k] = g
    for k, width in (("rg_conv_w", 64), ("rg_gate_b", 64), ("gd_conv_w", 192)):
        grads[k] = lax.dynamic_slice_in_dim(grads[k], me * width, width, axis=2)
    names = small_names + ["final_norm_w"]
    shapes = [given_w[k].shape for k in names]
    d_p, m_p, v_p = adam_flat(_pack([grads[k] for k in names]), _pack([given_w[k] for k in names]),
                              _pack([given_m[k] for k in names]), _pack([given_v[k] for k in names]), "adam_small")
    deltas = dict(zip(names, _unpack(d_p, shapes)))
    new_m = dict(zip(names, _unpack(m_p, shapes)))
    new_v = dict(zip(names, _unpack(v_p, shapes)))
    grads["w_in"], deltas["w_in"], new_m["w_in"], new_v["w_in"] = g_wi, d_wi, m_wi, v_wi
    grads["w_out"], deltas["w_out"], new_m["w_out"], new_v["w_out"] = g_wo, d_wo, m_wo, v_wo

    order = ["norm_w", "w_in", "rg_conv_w", "rg_conv_b", "rg_gate_w", "rg_gate_b", "rg_lambda", "ml_gate_b", "ml_norm_w",
             "gd_conv_w", "gd_a_log", "gd_dt_bias", "gd_norm_w", "w_out", "final_norm_w"]
    return (loss, dx.reshape(1, s, D), *[grads[k] for k in order], *[deltas[k] for k in order],
            *[new_m[k] for k in order], *[new_v[k] for k in order])
```

```python
import functools

import jax
import jax.numpy as jnp
from jax import lax
from jax.experimental import pallas as pl
from jax.experimental.pallas import tpu as pltpu

f32 = jnp.float32
bf16 = jnp.bfloat16
MESH = pl.DeviceIdType.MESH

N_DEV = 8
D = 1024
DEPTH = 4
EPS = 1e-6
SEG = 512
HD = 128
NH = 4
LC = 64
D_IN = 5648
DP = 5760
SMALL0 = 5632
RG_TILE = 256
RG_C = 8.0

ADAM_LR = 0.001
ADAM_B1 = 0.9
ADAM_B2 = 0.999
ADAM_EPS = 1e-08
ADAM_WD = 0.01
ADAM_STEP = 10


def _cp(vmem_mb):
    return pltpu.CompilerParams(vmem_limit_bytes=vmem_mb * 2 ** 20)


def _dot(a, b, ca, cb):
    return lax.dot_general(a.astype(bf16), b.astype(bf16), (((ca,), (cb,)), ((), ())), preferred_element_type=f32)


@jax.custom_vjp
def mm_nn(a, b):
    return _dot(a, b, 1, 0)


@jax.custom_vjp
def mm_nt(a, b):
    return _dot(a, b, 1, 1)


@jax.custom_vjp
def mm_tn(a, b):
    return _dot(a, b, 0, 0)


mm_nn.defvjp(lambda a, b: (mm_nn(a, b), (a, b)), lambda r, g: (mm_nt(g, r[1]), mm_tn(r[0], g)))
mm_nt.defvjp(lambda a, b: (mm_nt(a, b), (a, b)), lambda r, g: (mm_nn(g, r[1]), mm_tn(g, r[0])))
mm_tn.defvjp(lambda a, b: (mm_tn(a, b), (a, b)), lambda r, g: (mm_nt(r[1], g), mm_nn(r[0], g)))


def _split(x):
    hi = x.astype(bf16)
    return hi, (x - hi.astype(f32)).astype(bf16)


def dot3(a, b):
    ah, al = _split(a)
    bh, bl = _split(b)
    return jnp.dot(jnp.concatenate([ah, al, ah], axis=1), jnp.concatenate([bh, bh, bl], axis=0), preferred_element_type=f32)


def _tri_sum(x, reverse):
    n = x.shape[0]
    r = lax.broadcasted_iota(jnp.int32, (n, 3 * n), 0)
    c = lax.broadcasted_iota(jnp.int32, (n, 3 * n), 1) & (n - 1)
    ones = ((c >= r) if reverse else (c <= r)).astype(bf16)
    hi = x.astype(bf16)
    rest = x - hi.astype(f32)
    mid = rest.astype(bf16)
    lo = (rest - mid.astype(f32)).astype(bf16)
    return jnp.dot(ones, jnp.concatenate([hi, mid, lo], axis=0), preferred_element_type=f32)


@jax.custom_vjp
def cumsum_rows(x):
    return _tri_sum(x, False)


@jax.custom_vjp
def rev_cumsum_rows(x):
    return _tri_sum(x, True)


cumsum_rows.defvjp(lambda x: (cumsum_rows(x), None), lambda _, g: (rev_cumsum_rows(g),))
rev_cumsum_rows.defvjp(lambda x: (rev_cumsum_rows(x), None), lambda _, g: (cumsum_rows(g),))


def _tri(n, strict=False):
    r = lax.broadcasted_iota(jnp.int32, (n, n), 0)
    c = lax.broadcasted_iota(jnp.int32, (n, n), 1)
    return (r > c) if strict else (r >= c)


def _lane_col(v, j):
    lane = lax.broadcasted_iota(jnp.int32, v.shape, 1)
    return jnp.sum(jnp.where(lane == j, v, 0.0), axis=1, keepdims=True)


def _rows_from(pieces, rows, width):
    ridx = lax.broadcasted_iota(jnp.int32, (rows, width), 0)
    out = jnp.zeros((rows, width), f32)
    for h, p in enumerate(pieces):
        out = out + jnp.where(ridx == h, jnp.broadcast_to(p, (rows, width)), 0.0)
    return out


def causal_conv(halo8, x, w4):
    t = x.shape[0]
    xe = jnp.concatenate([halo8, x], axis=0)
    y = xe[5:5 + t] * w4[0:1]
    for k in range(1, 4):
        y = y + xe[5 + k:5 + k + t] * w4[k:k + 1]
    return y


def ml_chunk(q, k, v, o_pre, z, small, bias_row, norm_w, C, n, m):
    lane = lax.broadcasted_iota(jnp.int32, small.shape, 1)
    pre = small + bias_row
    lg = jnp.where(lane < NH, pre, jnp.where(lane < 2 * NH, jax.nn.log_sigmoid(pre), 0.0))
    bc = cumsum_rows(lg)
    lg_t = lg.T
    bc_t = bc.T
    causal = _tri(LC)
    ys, c_new, n_new, m_new_rows = [], [], [], []
    for h in range(NH):
        sl = slice(h * HD, (h + 1) * HD)
        qh = q[:, sl] * (HD ** -0.5)
        kh = k[:, sl]
        vh = v[:, sl]
        li_col = _lane_col(lg, h)
        b_col = _lane_col(bc, NH + h)
        li_row = lg_t[h:h + 1, :]
        b_row = bc_t[NH + h:NH + h + 1, :]
        m_h = m[h:h + 1, 0:1]
        c_h = C[sl, :]
        n_h = n[h:h + 1, :]
        dmat = jnp.where(causal, b_col - b_row + li_row, -jnp.inf)
        m_inter = b_col + m_h
        m_t = jnp.maximum(m_inter, jnp.max(dmat, axis=-1, keepdims=True))
        p = jnp.exp(dmat - m_t)
        s = mm_nt(qh, kh) * p
        sc = jnp.exp(m_inter - m_t)
        num = mm_nn(s, vh) + sc * mm_nn(qh, c_h)
        den = jnp.sum(s, axis=-1, keepdims=True) + sc * jnp.sum(qh * n_h, axis=-1, keepdims=True)
        hh = num / jnp.maximum(jnp.abs(den), jnp.exp(-m_t))
        g = b_col[LC - 1:LC, :]
        ws = g - b_col + li_col
        m_new = jnp.maximum(g + m_h, jnp.max(ws, axis=0, keepdims=True))
        dec = jnp.exp(g + m_h - m_new)
        wk = jnp.exp(ws - m_new)
        kw = wk * kh
        c_new.append(dec * c_h + mm_tn(kw, vh))
        n_new.append(dec * n_h + jnp.sum(kw, axis=0, keepdims=True))
        m_new_rows.append(jnp.broadcast_to(m_new, (1, HD)))
        hn = hh * lax.rsqrt(jnp.mean(hh * hh, axis=-1, keepdims=True) + EPS) * norm_w[:, sl]
        ys.append(hn * jax.nn.sigmoid(o_pre[:, sl]) * jax.nn.silu(z[:, sl]))
    y = jnp.concatenate(ys, axis=1)
    return y, jnp.concatenate(c_new, axis=0), _rows_from(n_new, 8, HD), _rows_from(m_new_rows, 8, HD)


@jax.custom_vjp
def _unit_lower_inverse(mat):
    eye = (lax.broadcasted_iota(jnp.int32, (LC, LC), 0) == lax.broadcasted_iota(jnp.int32, (LC, LC), 1)).astype(f32)
    p = -mat
    t = eye + p
    for _ in range(5):
        p = dot3(p, p)
        t = t + dot3(t, p)
    return t


def _unit_lower_inverse_fwd(mat):
    t = _unit_lower_inverse(mat)
    return t, t


def _unit_lower_inverse_bwd(t, g):
    tt = t.T
    return (-dot3(dot3(tt, g), tt),)


_unit_lower_inverse.defvjp(_unit_lower_inverse_fwd, _unit_lower_inverse_bwd)


def gd_chunk(qh8, q, kh8, k, vh8, v, z, small, conv_w, alog_row, dt_row, norm_w, st):
    lane = lax.broadcasted_iota(jnp.int32, small.shape, 1)
    is_a = (lane >= 2 * NH) & (lane < 3 * NH)
    g_all = jnp.where(is_a, -jnp.exp(alog_row) * jax.nn.softplus(small + dt_row), 0.0)
    beta_all = jax.nn.sigmoid(small)
    gc_all = cumsum_rows(g_all)
    gc_t = gc_all.T
    qc = jax.nn.silu(causal_conv(qh8, q, conv_w[:, 0:SEG]))
    kc = jax.nn.silu(causal_conv(kh8, k, conv_w[:, SEG:2 * SEG]))
    vc = jax.nn.silu(causal_conv(vh8, v, conv_w[:, 2 * SEG:3 * SEG]))
    incl = _tri(LC)
    strict = _tri(LC, strict=True)
    ys, st_new = [], []
    for h in range(NH):
        sl = slice(h * HD, (h + 1) * HD)
        qh = qc[:, sl]
        kh = kc[:, sl]
        vh = vc[:, sl]
        qh = qh * lax.rsqrt(jnp.sum(qh * qh, axis=-1, keepdims=True) + EPS) * (HD ** -0.5)
        kh = kh * lax.rsqrt(jnp.sum(kh * kh, axis=-1, keepdims=True) + EPS)
        beta = _lane_col(beta_all, 3 * NH + h)
        gc = _lane_col(gc_all, 2 * NH + h)
        gc_row = gc_t[2 * NH + h:2 * NH + h + 1, :]
        s_h = st[sl, :]
        gam = jnp.exp(jnp.where(incl, gc - gc_row, -jnp.inf))
        kb = kh * beta
        m_strict = jnp.where(strict, mm_nt(kb, kh) * gam, 0.0)
        t_inv = _unit_lower_inverse(m_strict)
        u = mm_nn(t_inv, vh * beta)
        w = mm_nn(t_inv, kb * jnp.exp(gc))
        aqk = mm_nt(qh, kh) * gam
        q_dec = qh * jnp.exp(gc)
        g_last = gc[LC - 1:LC, :]
        k_dec = kh * jnp.exp(g_last - gc)
        v_new = u - mm_nn(w, s_h)
        o = mm_nn(q_dec, s_h) + mm_nn(aqk, v_new)
        st_new.append(s_h * jnp.exp(g_last) + mm_tn(k_dec, v_new))
        on = o * lax.rsqrt(jnp.mean(o * o, axis=-1, keepdims=True) + EPS) * norm_w
        ys.append(on * jax.nn.silu(z[:, sl]))
    return jnp.concatenate(ys, axis=1), jnp.concatenate(st_new, axis=0)


def rg_pre(xh8, x, conv_w, conv_b, gw_r, gw_i, gate_b, lam):
    xc = causal_conv(xh8, x, conv_w) + conv_b
    r = jax.nn.sigmoid(mm_nn(xc, gw_r) + gate_b[0:1])
    i = jax.nn.sigmoid(mm_nn(xc, gw_i) + gate_b[1:2])
    log_a = -RG_C * r * jax.nn.softplus(-lam)
    a = jnp.exp(log_a)
    th = jnp.tanh(log_a)
    one_minus_a2 = -2.0 * th / (1.0 - th)
    b = jnp.sqrt(one_minus_a2) * (i * xc)
    return a, b


def _scan_rows(a_ref, b_ref, out_ref, h0, n_rows, reverse):
    n_groups = n_rows // 8
    width = a_ref.shape[1]
    row = lax.broadcasted_iota(jnp.int32, (8, width), 0)

    def body(j, h):
        g = (n_groups - 1 - j) if reverse else j
        r0 = pl.multiple_of(g * 8, 8)
        av = a_ref[pl.ds(r0, 8), :]
        bv = b_ref[pl.ds(r0, 8), :]
        for d in (1, 2, 4):
            sh = (8 - d) if reverse else d
            a_s = pltpu.roll(av, sh, 0)
            b_s = pltpu.roll(bv, sh, 0)
            valid = (row < 8 - d) if reverse else (row >= d)
            bv = jnp.where(valid, av * b_s + bv, bv)
            av = jnp.where(valid, av * a_s, av)
        hv = av * h + bv
        out_ref[pl.ds(r0, 8), :] = hv
        return hv[0:1, :] if reverse else hv[7:8, :]

    return lax.fori_loop(0, n_groups, body, h0)


def _norm_rows(xv):
    r = lax.rsqrt(jnp.mean(xv * xv, axis=-1, keepdims=True) + EPS)
    return xv * r, r


def inproj(x, nw, w):
    s, ts = x.shape[0], 256

    def body(x_ref, nw_ref, w_ref, o_ref):
        xn, _ = _norm_rows(x_ref[...])
        hn = xn * nw_ref[...]
        o_ref[...] = jnp.dot(hn.astype(bf16), w_ref[...], preferred_element_type=f32)

    return pl.pallas_call(
        body, grid=(s // ts,),
        in_specs=[pl.BlockSpec((ts, D), lambda i: (i, 0)), pl.BlockSpec((1, D), lambda i: (0, 0)),
                  pl.BlockSpec((D, DP), lambda i: (0, 0))],
        out_specs=pl.BlockSpec((ts, DP), lambda i: (i, 0)),
        out_shape=jax.ShapeDtypeStruct((s, DP), f32), name="inproj", compiler_params=_cp(56))(x, nw, w)


def outproj(x, yr, ym, yg, wo):
    s, ts = x.shape[0], 256

    def body(x_ref, yr_ref, ym_ref, yg_ref, wo_ref, o_ref):
        acc = x_ref[...]
        for j, y_ref in enumerate((yr_ref, ym_ref, yg_ref)):
            acc = acc + jnp.dot(y_ref[...].astype(bf16), wo_ref[j * SEG:(j + 1) * SEG, :], preferred_element_type=f32)
        o_ref[...] = acc

    yspec = pl.BlockSpec((ts, SEG), lambda i: (i, 0))
    return pl.pallas_call(
        body, grid=(s // ts,),
        in_specs=[pl.BlockSpec((ts, D), lambda i: (i, 0)), yspec, yspec, yspec,
                  pl.BlockSpec((3 * SEG, D), lambda i: (0, 0))],
        out_specs=pl.BlockSpec((ts, D), lambda i: (i, 0)),
        out_shape=jax.ShapeDtypeStruct((s, D), f32), name="outproj", compiler_params=_cp(40))(x, yr, ym, yg, wo)


def head(x, fw, tgt):
    s, ts = x.shape[0], 256

    def body(x_ref, fw_ref, t_ref, dx_ref, loss_ref, dfw_ref):
        @pl.when(pl.program_id(0) == 0)
        def _():
            loss_ref[...] = jnp.zeros_like(loss_ref)
            dfw_ref[...] = jnp.zeros_like(dfw_ref)

        xn, r = _norm_rows(x_ref[...])
        fw_v = fw_ref[...]
        err = xn * fw_v - t_ref[...]
        loss_ref[...] += 0.5 * jnp.sum(jnp.mean(err * err, axis=-1, keepdims=True))
        dy = err * (1.0 / D)
        dfw_ref[0:1, :] += jnp.sum(dy * xn, axis=0, keepdims=True)
        dxn = dy * fw_v
        dx_ref[...] = r * (dxn - xn * jnp.mean(dxn * xn, axis=-1, keepdims=True))

    tile = pl.BlockSpec((ts, D), lambda i: (i, 0))
    return pl.pallas_call(
        body, grid=(s // ts,),
        in_specs=[tile, pl.BlockSpec((1, D), lambda i: (0, 0)), tile],
        out_specs=[tile, pl.BlockSpec((8, 128), lambda i: (0, 0)), pl.BlockSpec((8, D), lambda i: (0, 0))],
        out_shape=[jax.ShapeDtypeStruct((s, D), f32), jax.ShapeDtypeStruct((8, 128), f32),
                   jax.ShapeDtypeStruct((8, D), f32)], name="head")(x, fw, tgt)


def outproj_bwd(dx, yr, ym, yg, wo):
    s, ts = dx.shape[0], 256

    def body(dx_ref, yr_ref, ym_ref, yg_ref, wo_ref, dyr_ref, dym_ref, dyg_ref, gwo_ref):
        @pl.when(pl.program_id(0) == 0)
        def _():
            gwo_ref[...] = jnp.zeros_like(gwo_ref)

        dxb = dx_ref[...].astype(bf16)
        for j, (y_ref, dy_ref) in enumerate(((yr_ref, dyr_ref), (ym_ref, dym_ref), (yg_ref, dyg_ref))):
            rows = slice(j * SEG, (j + 1) * SEG)
            dy_ref[...] = lax.dot_general(dxb, wo_ref[rows, :], (((1,), (1,)), ((), ())), preferred_element_type=f32)
            gwo_ref[rows, :] += lax.dot_general(y_ref[...].astype(bf16), dxb, (((0,), (0,)), ((), ())),
                                                preferred_element_type=f32)

    yspec = pl.BlockSpec((ts, SEG), lambda i: (i, 0))
    wspec = pl.BlockSpec((3 * SEG, D), lambda i: (0, 0))
    ysh = jax.ShapeDtypeStruct((s, SEG), f32)
    return pl.pallas_call(
        body, grid=(s // ts,),
        in_specs=[pl.BlockSpec((ts, D), lambda i: (i, 0)), yspec, yspec, yspec, wspec],
        out_specs=[yspec, yspec, yspec, wspec],
        out_shape=[ysh, ysh, ysh, jax.ShapeDtypeStruct((3 * SEG, D), f32)],
        name="outproj_bwd", compiler_params=_cp(48))(dx, yr, ym, yg, wo)


def inproj_bwd_x(x, nw, w, dxo, d_rg, d_ml, d_gd, d_sa, d_sb):
    s, ts = x.shape[0], 256
    widths = (d_rg.shape[1], d_ml.shape[1], d_gd.shape[1], HD)

    def body(x_ref, nw_ref, w_ref, dxo_ref, rg_ref, ml_ref, gd_ref, sa_ref, sb_ref, dx_ref, dnw_ref):
        @pl.when(pl.program_id(0) == 0)
        def _():
            dnw_ref[...] = jnp.zeros_like(dnw_ref)

        xn, r = _norm_rows(x_ref[...])
        pieces = (rg_ref[...], ml_ref[...], gd_ref[...], sa_ref[...] + sb_ref[...])
        dhn = jnp.zeros((ts, D), f32)
        c0 = 0
        for piece, wd in zip(pieces, widths):
            dhn = dhn + lax.dot_general(piece.astype(bf16), w_ref[:, c0:c0 + wd], (((1,), (1,)), ((), ())),
                                        preferred_element_type=f32)
            c0 += wd
        dnw_ref[0:1, :] += jnp.sum(dhn * xn, axis=0, keepdims=True)
        dxn = dhn * nw_ref[...]
        dx_ref[...] = dxo_ref[...] + r * (dxn - xn * jnp.mean(dxn * xn, axis=-1, keepdims=True))

    tile = pl.BlockSpec((ts, D), lambda i: (i, 0))
    return pl.pallas_call(
        body, grid=(s // ts,),
        in_specs=[tile, pl.BlockSpec((1, D), lambda i: (0, 0)), pl.BlockSpec((D, DP), lambda i: (0, 0)), tile]
        + [pl.BlockSpec((ts, wd), lambda i: (i, 0)) for wd in widths] + [pl.BlockSpec((ts, HD), lambda i: (i, 0))],
        out_specs=[tile, pl.BlockSpec((8, D), lambda i: (0, 0))],
        out_shape=[jax.ShapeDtypeStruct((s, D), f32), jax.ShapeDtypeStruct((8, D), f32)],
        name="inproj_bwd_x", compiler_params=_cp(56))(x, nw, w, dxo, d_rg, d_ml, d_gd, d_sa, d_sb)


def wgrad(x, nw, dps, name):
    s, ts = x.shape[0], 512
    c = dps[0].shape[1]
    ct = min(c, SEG)
    n_dp = len(dps)

    def body(*refs):
        x_ref, nw_ref = refs[0], refs[1]
        dp_refs = refs[2:2 + n_dp]
        o_ref = refs[2 + n_dp]

        @pl.when(pl.program_id(1) == 0)
        def _():
            o_ref[...] = jnp.zeros_like(o_ref)

        xn, _ = _norm_rows(x_ref[...])
        hn = (xn * nw_ref[...]).astype(bf16)
        dp = dp_refs[0][...]
        for extra in dp_refs[1:]:
            dp = dp + extra[...]
        o_ref[...] += lax.dot_general(hn, dp.astype(bf16), (((0,), (0,)), ((), ())), preferred_element_type=f32)

    return pl.pallas_call(
        body, grid=(c // ct, s // ts),
        in_specs=[pl.BlockSpec((ts, D), lambda j, i: (i, 0)), pl.BlockSpec((1, D), lambda j, i: (0, 0))]
        + [pl.BlockSpec((ts, ct), lambda j, i: (i, j)) for _ in dps],
        out_specs=pl.BlockSpec((D, ct), lambda j, i: (0, j)),
        out_shape=jax.ShapeDtypeStruct((D, c), f32), name=name, compiler_params=_cp(40))(x, nw, *dps)


def _seg_spec(rows, seg, n_tiles=None):
    if n_tiles is None:
        return pl.BlockSpec((rows, SEG), lambda i: (i, seg))
    return pl.BlockSpec((rows, SEG), lambda i: (n_tiles - 1 - i, seg))


def _halo_spec(rows, seg, n_tiles=None):
    per = rows // 8
    if n_tiles is None:
        return pl.BlockSpec((8, SEG), lambda i: (jnp.maximum(i * per - 1, 0), seg))
    return pl.BlockSpec((8, SEG), lambda i: (jnp.maximum((n_tiles - 1 - i) * per - 1, 0), seg))


def _const_spec(shape):
    return pl.BlockSpec(shape, lambda i: tuple(0 for _ in shape))


def rglru_fwd(proj, conv_w, conv_b, gw_r, gw_i, gate_b, lam):
    s = proj.shape[0]
    tr = RG_TILE

    def body(xh_ref, x_ref, z_ref, cw_ref, cb_ref, gr_ref, gi_ref, gb_ref, lam_ref, y_ref, h_ref, a_s, b_s, hc):
        first = pl.program_id(0) == 0

        @pl.when(first)
        def _():
            hc[...] = jnp.zeros_like(hc)

        xh = jnp.where(first, 0.0, xh_ref[...])
        a, b = rg_pre(xh, x_ref[...], cw_ref[...], cb_ref[...], gr_ref[...], gi_ref[...], gb_ref[...], lam_ref[...])
        a_s[...] = a
        b_s[...] = b
        hc[0:1, :] = _scan_rows(a_s, b_s, h_ref, hc[0:1, :], tr, False)
        y_ref[...] = h_ref[...] * jax.nn.silu(z_ref[...])

    out = pl.BlockSpec((tr, SEG), lambda i: (i, 0))
    return pl.pallas_call(
        body, grid=(s // tr,),
        in_specs=[_halo_spec(tr, 0), _seg_spec(tr, 0), _seg_spec(tr, 1), _const_spec((4, SEG)), _const_spec((1, SEG)),
                  _const_spec((SEG, SEG)), _const_spec((SEG, SEG)), _const_spec((2, SEG)), _const_spec((1, SEG))],
        out_specs=[out, out],
        out_shape=[jax.ShapeDtypeStruct((s, SEG), f32), jax.ShapeDtypeStruct((s, SEG), f32)],
        scratch_shapes=[pltpu.VMEM((tr, SEG), f32), pltpu.VMEM((tr, SEG), f32), pltpu.VMEM((8, SEG), f32)],
        name="rglru_fwd", compiler_params=_cp(40))(proj, proj, proj, conv_w, conv_b, gw_r, gw_i, gate_b, lam)


def rglru_bwd(proj, hs, dy, conv_w, conv_b, gw_r, gw_i, gate_b, lam):
    s = proj.shape[0]
    tr = RG_TILE
    nt = s // tr

    def body(xh_ref, x_ref, z_ref, hh_ref, h_ref, dy_ref, cw_ref, cb_ref, gr_ref, gi_ref, gb_ref, lam_ref,
             dp_ref, dcw_ref, dcb_ref, dgr_ref, dgi_ref, dgb_ref, dlam_ref,
             an_s, g_s, dh_s, a_first, dh_first, dhalo):
        i = pl.program_id(0)
        first_tile = i == nt - 1

        @pl.when(i == 0)
        def _():
            for ref in (dcw_ref, dcb_ref, dgr_ref, dgi_ref, dgb_ref, dlam_ref, a_first, dh_first, dhalo):
                ref[...] = jnp.zeros_like(ref)

        xh = jnp.where(first_tile, 0.0, xh_ref[...])
        params = (cw_ref[...], cb_ref[...], gr_ref[...], gi_ref[...], gb_ref[...], lam_ref[...])
        (a, _), vjp = jax.vjp(rg_pre, xh, x_ref[...], *params)
        zv = z_ref[...]
        hv = h_ref[...]
        dyv = dy_ref[...]
        sig = jax.nn.sigmoid(zv)
        g_s[...] = dyv * (zv * sig)
        dp_ref[:, SEG:2 * SEG] = dyv * hv * (sig * (1.0 + zv * (1.0 - sig)))
        ridx = lax.broadcasted_iota(jnp.int32, (tr, SEG), 0)
        an_s[...] = jnp.where(ridx == tr - 1, jnp.broadcast_to(a_first[0:1, :], (tr, SEG)), pltpu.roll(a, tr - 1, 0))
        _scan_rows(an_s, g_s, dh_s, dh_first[0:1, :], tr, True)
        dh = dh_s[...]
        h_prev_last = jnp.where(first_tile, 0.0, hh_ref[...])[7:8, :]
        h_prev = pltpu.roll(hv, 1, 0)
        h_prev = jnp.where(ridx == 0, jnp.broadcast_to(h_prev_last, (tr, SEG)), h_prev)
        dxh, dx, dcw, dcb, dgr, dgi, dgb, dlam = vjp((dh * h_prev, dh))
        dp_ref[:, 0:SEG] = dx
        dp_ref[tr - 8:tr, 0:SEG] += dhalo[...]
        dhalo[...] = dxh
        a_first[0:1, :] = a[0:1, :]
        dh_first[0:1, :] = dh[0:1, :]
        dcw_ref[...] += dcw
        dcb_ref[...] += dcb
        dgr_ref[...] += dgr
        dgi_ref[...] += dgi
        dgb_ref[...] += dgb
        dlam_ref[...] += dlam

    pspecs = [_const_spec((4, SEG)), _const_spec((1, SEG)), _const_spec((SEG, SEG)), _const_spec((SEG, SEG)),
              _const_spec((2, SEG)), _const_spec((1, SEG))]
    pshapes = [jax.ShapeDtypeStruct(sh, f32) for sh in ((4, SEG), (1, SEG), (SEG, SEG), (SEG, SEG), (2, SEG), (1, SEG))]
    tile = pl.BlockSpec((tr, SEG), lambda i: (nt - 1 - i, 0))
    return pl.pallas_call(
        body, grid=(nt,),
        in_specs=[_halo_spec(tr, 0, nt), _seg_spec(tr, 0, nt), _seg_spec(tr, 1, nt),
                  pl.BlockSpec((8, SEG), lambda i: (jnp.maximum((nt - 1 - i) * (tr // 8) - 1, 0), 0)), tile, tile] + pspecs,
        out_specs=[pl.BlockSpec((tr, 2 * SEG), lambda i: (nt - 1 - i, 0))] + pspecs,
        out_shape=[jax.ShapeDtypeStruct((s, 2 * SEG), f32)] + pshapes,
        scratch_shapes=[pltpu.VMEM((tr, SEG), f32), pltpu.VMEM((tr, SEG), f32), pltpu.VMEM((tr, SEG), f32),
                        pltpu.VMEM((8, SEG), f32), pltpu.VMEM((8, SEG), f32), pltpu.VMEM((8, SEG), f32)],
        name="rglru_bwd", compiler_params=_cp(48))(proj, proj, proj, hs, hs, dy, conv_w, conv_b, gw_r, gw_i, gate_b, lam)


ML_SEGS = (2, 3, 4, 5, 6)
SMALL_BLK = SMALL0 // HD


def mlstm_fwd(proj, bias_row, norm_w):
    s = proj.shape[0]
    nc = s // LC

    def body(q_ref, k_ref, v_ref, o_ref, z_ref, sm_ref, b_ref, nw_ref, y_ref, cs_ref, ns_ref, ms_ref, c_s, n_s, m_s):
        @pl.when(pl.program_id(0) == 0)
        def _():
            c_s[...] = jnp.zeros_like(c_s)
            n_s[...] = jnp.zeros_like(n_s)
            m_s[...] = jnp.zeros_like(m_s)

        cs_ref[0] = c_s[...]
        ns_ref[0] = n_s[...]
        ms_ref[0] = m_s[...]
        y, c2, n2, m2 = ml_chunk(q_ref[...], k_ref[...], v_ref[...], o_ref[...], z_ref[...], sm_ref[...],
                                 b_ref[...], nw_ref[...], c_s[...], n_s[...], m_s[...])
        y_ref[...] = y
        c_s[...] = c2
        n_s[...] = n2
        m_s[...] = m2

    return pl.pallas_call(
        body, grid=(nc,),
        in_specs=[_seg_spec(LC, sg) for sg in ML_SEGS]
        + [pl.BlockSpec((LC, HD), lambda i: (i, SMALL_BLK)), _const_spec((1, HD)), _const_spec((1, SEG))],
        out_specs=[pl.BlockSpec((LC, SEG), lambda i: (i, 0)), pl.BlockSpec((1, SEG, HD), lambda i: (i, 0, 0)),
                   pl.BlockSpec((1, 8, HD), lambda i: (i, 0, 0)), pl.BlockSpec((1, 8, HD), lambda i: (i, 0, 0))],
        out_shape=[jax.ShapeDtypeStruct((s, SEG), f32), jax.ShapeDtypeStruct((nc, SEG, HD), f32),
                   jax.ShapeDtypeStruct((nc, 8, HD), f32), jax.ShapeDtypeStruct((nc, 8, HD), f32)],
        scratch_shapes=[pltpu.VMEM((SEG, HD), f32), pltpu.VMEM((8, HD), f32), pltpu.VMEM((8, HD), f32)],
        name="mlstm_fwd")(proj, proj, proj, proj, proj, proj, bias_row, norm_w)


def mlstm_bwd(proj, dy, cs, ns, ms, bias_row, norm_w):
    s = proj.shape[0]
    nc = s // LC

    def body(q_ref, k_ref, v_ref, o_ref, z_ref, sm_ref, dy_ref, cs_ref, ns_ref, ms_ref, b_ref, nw_ref,
             dp_ref, dsm_ref, db_ref, dnw_ref, dc_s, dn_s, dm_s):
        @pl.when(pl.program_id(0) == 0)
        def _():
            for ref in (db_ref, dnw_ref, dc_s, dn_s, dm_s):
                ref[...] = jnp.zeros_like(ref)

        _, vjp = jax.vjp(ml_chunk, q_ref[...], k_ref[...], v_ref[...], o_ref[...], z_ref[...], sm_ref[...],
                         b_ref[...], nw_ref[...], cs_ref[0], ns_ref[0], ms_ref[0])
        dq, dk, dv, do, dz, dsm, db, dnw, dc, dn, dm = vjp((dy_ref[...], dc_s[...], dn_s[...], dm_s[...]))
        for j, val in enumerate((dq, dk, dv, do, dz)):
            dp_ref[:, j * SEG:(j + 1) * SEG] = val
        dsm_ref[...] = dsm
        db_ref[0:1, :] += db
        dnw_ref[0:1, :] += dnw
        dc_s[...] = dc
        dn_s[...] = dn
        dm_s[...] = dm

    rev3 = lambda i: (nc - 1 - i, 0, 0)
    return pl.pallas_call(
        body, grid=(nc,),
        in_specs=[_seg_spec(LC, sg, nc) for sg in ML_SEGS]
        + [pl.BlockSpec((LC, HD), lambda i: (nc - 1 - i, SMALL_BLK)), pl.BlockSpec((LC, SEG), lambda i: (nc - 1 - i, 0)),
           pl.BlockSpec((1, SEG, HD), rev3), pl.BlockSpec((1, 8, HD), rev3), pl.BlockSpec((1, 8, HD), rev3),
           _const_spec((1, HD)), _const_spec((1, SEG))],
        out_specs=[pl.BlockSpec((LC, 5 * SEG), lambda i: (nc - 1 - i, 0)), pl.BlockSpec((LC, HD), lambda i: (nc - 1 - i, 0)),
                   _const_spec((8, HD)), _const_spec((8, SEG))],
        out_shape=[jax.ShapeDtypeStruct((s, 5 * SEG), f32), jax.ShapeDtypeStruct((s, HD), f32),
                   jax.ShapeDtypeStruct((8, HD), f32), jax.ShapeDtypeStruct((8, SEG), f32)],
        scratch_shapes=[pltpu.VMEM((SEG, HD), f32), pltpu.VMEM((8, HD), f32), pltpu.VMEM((8, HD), f32)],
        name="mlstm_bwd", compiler_params=_cp(48))(proj, proj, proj, proj, proj, proj, dy, cs, ns, ms, bias_row, norm_w)


GD_SEGS = (7, 8, 9)


def gdn_fwd(proj, conv_w, alog_row, dt_row, norm_w):
    s = proj.shape[0]
    nc = s // LC

    def body(qh_ref, q_ref, kh_ref, k_ref, vh_ref, v_ref, z_ref, sm_ref, cw_ref, al_ref, dt_ref, nw_ref,
             y_ref, ss_ref, st_s):
        first = pl.program_id(0) == 0

        @pl.when(first)
        def _():
            st_s[...] = jnp.zeros_like(st_s)

        ss_ref[0] = st_s[...]
        halo = [jnp.where(first, 0.0, r[...]) for r in (qh_ref, kh_ref, vh_ref)]
        y, st2 = gd_chunk(halo[0], q_ref[...], halo[1], k_ref[...], halo[2], v_ref[...], z_ref[...], sm_ref[...],
                          cw_ref[...], al_ref[...], dt_ref[...], nw_ref[...], st_s[...])
        y_ref[...] = y
        st_s[...] = st2

    qkv_specs = []
    for sg in GD_SEGS:
        qkv_specs += [_halo_spec(LC, sg), _seg_spec(LC, sg)]
    return pl.pallas_call(
        body, grid=(nc,),
        in_specs=qkv_specs + [_seg_spec(LC, 10), pl.BlockSpec((LC, HD), lambda i: (i, SMALL_BLK)),
                              _const_spec((4, 3 * SEG)), _const_spec((1, HD)), _const_spec((1, HD)), _const_spec((1, HD))],
        out_specs=[pl.BlockSpec((LC, SEG), lambda i: (i, 0)), pl.BlockSpec((1, SEG, HD), lambda i: (i, 0, 0))],
        out_shape=[jax.ShapeDtypeStruct((s, SEG), f32), jax.ShapeDtypeStruct((nc, SEG, HD), f32)],
        scratch_shapes=[pltpu.VMEM((SEG, HD), f32)],
        name="gdn_fwd")(proj, proj, proj, proj, proj, proj, proj, proj, conv_w, alog_row, dt_row, norm_w)


def gdn_bwd(proj, dy, ss, conv_w, alog_row, dt_row, norm_w):
    s = proj.shape[0]
    nc = s // LC

    def body(qh_ref, q_ref, kh_ref, k_ref, vh_ref, v_ref, z_ref, sm_ref, dy_ref, ss_ref, cw_ref, al_ref, dt_ref, nw_ref,
             dp_ref, dsm_ref, dcw_ref, dal_ref, ddt_ref, dnw_ref, dst_s, dhalo):
        i = pl.program_id(0)
        first_chunk = i == nc - 1

        @pl.when(i == 0)
        def _():
            for ref in (dcw_ref, dal_ref, ddt_ref, dnw_ref, dst_s, dhalo):
                ref[...] = jnp.zeros_like(ref)

        halo = [jnp.where(first_chunk, 0.0, r[...]) for r in (qh_ref, kh_ref, vh_ref)]
        _, vjp = jax.vjp(gd_chunk, halo[0], q_ref[...], halo[1], k_ref[...], halo[2], v_ref[...], z_ref[...], sm_ref[...],
                         cw_ref[...], al_ref[...], dt_ref[...], nw_ref[...], ss_ref[0])
        dqh, dq, dkh, dk, dvh, dv, dz, dsm, dcw, dal, ddt, dnw, dst = vjp((dy_ref[...], dst_s[...]))
        for j, val in enumerate((dq, dk, dv, dz)):
            dp_ref[:, j * SEG:(j + 1) * SEG] = val
        dp_ref[LC - 8:LC, 0:3 * SEG] += dhalo[...]
        for j, val in enumerate((dqh, dkh, dvh)):
            dhalo[:, j * SEG:(j + 1) * SEG] = val
        dsm_ref[...] = dsm
        dcw_ref[...] += dcw
        dal_ref[0:1, :] += dal
        ddt_ref[0:1, :] += ddt
        dnw_ref[0:1, :] += dnw
        dst_s[...] = dst

    qkv_specs = []
    for sg in GD_SEGS:
        qkv_specs += [_halo_spec(LC, sg, nc), _seg_spec(LC, sg, nc)]
    return pl.pallas_call(
        body, grid=(nc,),
        in_specs=qkv_specs + [_seg_spec(LC, 10, nc), pl.BlockSpec((LC, HD), lambda i: (nc - 1 - i, SMALL_BLK)),
                              pl.BlockSpec((LC, SEG), lambda i: (nc - 1 - i, 0)),
                              pl.BlockSpec((1, SEG, HD), lambda i: (nc - 1 - i, 0, 0)),
                              _const_spec((4, 3 * SEG)), _const_spec((1, HD)), _const_spec((1, HD)), _const_spec((1, HD))],
        out_specs=[pl.BlockSpec((LC, 4 * SEG), lambda i: (nc - 1 - i, 0)), pl.BlockSpec((LC, HD), lambda i: (nc - 1 - i, 0)),
                   _const_spec((4, 3 * SEG)), _const_spec((8, HD)), _const_spec((8, HD)), _const_spec((8, HD))],
        out_shape=[jax.ShapeDtypeStruct((s, 4 * SEG), f32), jax.ShapeDtypeStruct((s, HD), f32),
                   jax.ShapeDtypeStruct((4, 3 * SEG), f32), jax.ShapeDtypeStruct((8, HD), f32),
                   jax.ShapeDtypeStruct((8, HD), f32), jax.ShapeDtypeStruct((8, HD), f32)],
        scratch_shapes=[pltpu.VMEM((SEG, HD), f32), pltpu.VMEM((8, 3 * SEG), f32)],
        name="gdn_bwd", compiler_params=_cp(48))(proj, proj, proj, proj, proj, proj, proj, proj, dy, ss,
                                                 conv_w, alog_row, dt_row, norm_w)


def _my_place():
    return lax.axis_index("x"), lax.axis_index("y"), lax.axis_index("c")


def _slot(p):
    return 4 * p[0] + 2 * p[1] + p[2]


def _peer(me, j):
    flips = ((j >> 2) & 1, (j >> 1) & 1, j & 1)
    return tuple((1 - v) if fl else v for v, fl in zip(me, flips))


_ANY = pl.BlockSpec(memory_space=pl.ANY)


def exchange(blocks, by_slot, name):
    n = len(blocks)
    out_shapes = []
    for blk, sl in zip(blocks, by_slot):
        shape = blk.shape if sl else (N_DEV,) + blk.shape
        out_shapes.append(jax.ShapeDtypeStruct(shape, blk.dtype))

    def body(*refs):
        ins, outs = refs[:n], refs[n:2 * n]
        send_sems, recv_sems, local_sems = refs[2 * n:]
        me = _my_place()
        mine = _slot(me)
        local, remote = [], []
        for a in range(n):
            src_own = ins[a].at[mine] if by_slot[a] else ins[a]
            cp = pltpu.make_async_copy(src_own, outs[a].at[mine], local_sems.at[a])
            cp.start()
            local.append(cp)
            for j in range(1, N_DEV):
                peer = _peer(me, j)
                src = ins[a].at[_slot(peer)] if by_slot[a] else ins[a]
                rc = pltpu.make_async_remote_copy(src_ref=src, dst_ref=outs[a].at[mine], send_sem=send_sems.at[a, j - 1],
                                                  recv_sem=recv_sems.at[a, j - 1], device_id=peer, device_id_type=MESH)
                rc.start()
                remote.append(rc)
        for rc in remote:
            rc.wait()
        for cp in local:
            cp.wait()

    return pl.pallas_call(
        body, in_specs=[_ANY] * n, out_specs=[_ANY] * n, out_shape=out_shapes,
        scratch_shapes=[pltpu.SemaphoreType.DMA((n, N_DEV - 1)), pltpu.SemaphoreType.DMA((n, N_DEV - 1)),
                        pltpu.SemaphoreType.DMA((n,))],
        name=name)(*blocks)


def _adamw(w, g, m, v):
    m = ADAM_B1 * m + (1.0 - ADAM_B1) * g
    v = ADAM_B2 * v + (1.0 - ADAM_B2) * (g * g)
    m_hat = m / (1.0 - ADAM_B1 ** ADAM_STEP)
    v_hat = v / (1.0 - ADAM_B2 ** ADAM_STEP)
    delta = -ADAM_LR * (m_hat / (jnp.sqrt(v_hat) + ADAM_EPS) + ADAM_WD * w)
    return delta, m, v


def adam_slots(slots, w, m, v, rows, name):
    _, depth, r, c = slots.shape

    def body(s_ref, w_ref, m_ref, v_ref, g_ref, d_ref, m2_ref, v2_ref):
        g = s_ref[0, 0].astype(f32)
        for k in range(1, N_DEV):
            g = g + s_ref[k, 0].astype(f32)
        d, m2, v2 = _adamw(w_ref[0], g, m_ref[0], v_ref[0])
        g_ref[0] = g
        d_ref[0] = d
        m2_ref[0] = m2
        v2_ref[0] = v2

    blk = pl.BlockSpec((1, rows, c), lambda l, i: (l, i, 0))
    sh = jax.ShapeDtypeStruct((depth, r, c), f32)
    return pl.pallas_call(
        body, grid=(depth, r // rows),
        in_specs=[pl.BlockSpec((N_DEV, 1, rows, c), lambda l, i: (0, l, i, 0)), blk, blk, blk],
        out_specs=[blk] * 4, out_shape=[sh] * 4, name=name, compiler_params=_cp(40))(slots, w, m, v)


def sum_slots(slots, name):
    _, r, c = slots.shape

    def body(s_ref, o_ref):
        g = s_ref[0]
        for k in range(1, N_DEV):
            g = g + s_ref[k]
        o_ref[...] = g

    return pl.pallas_call(body, out_shape=jax.ShapeDtypeStruct((r, c), f32), name=name, compiler_params=_cp(40))(slots)


def adam_flat(g, w, m, v, name):
    def body(g_ref, w_ref, m_ref, v_ref, d_ref, m2_ref, v2_ref):
        d, m2, v2 = _adamw(w_ref[...], g_ref[...], m_ref[...], v_ref[...])
        d_ref[...] = d
        m2_ref[...] = m2
        v2_ref[...] = v2

    sh = jax.ShapeDtypeStruct(g.shape, f32)
    return pl.pallas_call(body, out_shape=[sh] * 3, name=name)(g, w, m, v)


def _pack(arrs):
    flat = jnp.concatenate([a.reshape(-1).astype(f32) for a in arrs])
    n = flat.shape[0]
    rows = -(-n // (8 * 128)) * 8
    flat = jnp.pad(flat, (0, rows * 128 - n))
    return flat.reshape(rows, 128)


def _unpack(packed, shapes):
    flat = packed.reshape(-1)
    out, off = [], 0
    for sh in shapes:
        n = 1
        for dim in sh:
            n *= dim
        out.append(flat[off:off + n].reshape(sh))
        off += n
    return out


def _regroup_cols(w):
    pad = jnp.zeros(w.shape[:-1] + (DP - D_IN,), w.dtype)
    return jnp.concatenate([w[..., :3584], w[..., 3592:5640], w[..., 3584:3592], w[..., 5640:5648], pad], axis=-1)


def _block_diag(gw):
    eye = jnp.eye(8, dtype=gw.dtype)
    return (gw[:, :, None, :] * eye[:, None, :, None]).reshape(SEG, SEG)


def _diag_blocks(dense):
    eye = jnp.eye(8, dtype=dense.dtype)
    return (dense.reshape(8, 64, 8, 64) * eye[:, None, :, None]).sum(axis=2)


def _lane_row(vals, first_lane):
    return jnp.zeros((1, HD), f32).at[0, first_lane:first_lane + NH].set(vals)


def kernel(x, norm_w, w_in, rg_conv_w, rg_conv_b, rg_gate_w, rg_gate_b, rg_lambda, ml_gate_b, ml_norm_w, gd_conv_w, gd_a_log, gd_dt_bias, gd_norm_w, w_out, final_norm_w, loss_target, m_norm_w, m_w_in, m_rg_conv_w, m_rg_conv_b, m_rg_gate_w, m_rg_gate_b, m_rg_lambda, m_ml_gate_b, m_ml_norm_w, m_gd_conv_w, m_gd_a_log, m_gd_dt_bias, m_gd_norm_w, m_w_out, m_final_norm_w, v_norm_w, v_w_in, v_rg_conv_w, v_rg_conv_b, v_rg_gate_w, v_rg_gate_b, v_rg_lambda, v_ml_gate_b, v_ml_norm_w, v_gd_conv_w, v_gd_a_log, v_gd_dt_bias, v_gd_norm_w, v_w_out, v_final_norm_w):
    s = x.shape[1]
    xs = x.reshape(s, D)
    tgt = loss_target.reshape(s, D)
    me = 4 * lax.axis_index("x") + 2 * lax.axis_index("y") + lax.axis_index("c")

    sharded_small = _pack([rg_conv_w, rg_gate_b, gd_conv_w])
    wi_g, wo_g, sm_g = exchange([w_in.astype(bf16), w_out.astype(bf16), sharded_small], [False, False, False], "gather_weights")
    w_in_full = _regroup_cols(wi_g.transpose(1, 2, 0, 3).reshape(DEPTH, D, D_IN))
    w_out_full = wo_g.transpose(1, 0, 2, 3).reshape(DEPTH, 3 * SEG, D)
    sm_parts = [_unpack(sm_g[k], [rg_conv_w.shape, rg_gate_b.shape, gd_conv_w.shape]) for k in range(N_DEV)]
    rg_conv_full = jnp.concatenate([p[0] for p in sm_parts], axis=-1)
    rg_gb_full = jnp.concatenate([p[1] for p in sm_parts], axis=-1)
    gd_conv_full = jnp.concatenate([p[2] for p in sm_parts], axis=-1)

    loss_part, dx, d_fw, g_w_in, g_w_out, g_small = local_step(
        xs, tgt, w_in_full, w_out_full, rg_conv_full, rg_gb_full, gd_conv_full, norm_w, rg_conv_b, rg_gate_w, rg_lambda,
        ml_gate_b, ml_norm_w, gd_a_log, gd_dt_bias, gd_norm_w, final_norm_w)
    given_w = dict(norm_w=norm_w, rg_conv_w=rg_conv_w, rg_conv_b=rg_conv_b, rg_gate_w=rg_gate_w, rg_gate_b=rg_gate_b,
                   rg_lambda=rg_lambda, ml_gate_b=ml_gate_b, ml_norm_w=ml_norm_w, gd_conv_w=gd_conv_w, gd_a_log=gd_a_log,
                   gd_dt_bias=gd_dt_bias, gd_norm_w=gd_norm_w, final_norm_w=final_norm_w, w_in=w_in, w_out=w_out)
    given_m = dict(norm_w=m_norm_w, rg_conv_w=m_rg_conv_w, rg_conv_b=m_rg_conv_b, rg_gate_w=m_rg_gate_w, rg_gate_b=m_rg_gate_b,
                   rg_lambda=m_rg_lambda, ml_gate_b=m_ml_gate_b, ml_norm_w=m_ml_norm_w, gd_conv_w=m_gd_conv_w,
                   gd_a_log=m_gd_a_log, gd_dt_bias=m_gd_dt_bias, gd_norm_w=m_gd_norm_w, final_norm_w=m_final_norm_w,
                   w_in=m_w_in, w_out=m_w_out)
    given_v = dict(norm_w=v_norm_w, rg_conv_w=v_rg_conv_w, rg_conv_b=v_rg_conv_b, rg_gate_w=v_rg_gate_w, rg_gate_b=v_rg_gate_b,
                   rg_lambda=v_rg_lambda, ml_gate_b=v_ml_gate_b, ml_norm_w=v_ml_norm_w, gd_conv_w=v_gd_conv_w,
                   gd_a_log=v_gd_a_log, gd_dt_bias=v_gd_dt_bias, gd_norm_w=v_gd_norm_w, final_norm_w=v_final_norm_w,
                   w_in=v_w_in, w_out=v_w_out)
    return finish_step(loss_part, dx, d_fw, g_w_in, g_w_out, g_small, s, me, given_w, given_m, given_v)


def local_step(xs, tgt, w_in_full, w_out_full, rg_conv_full, rg_gb_full, gd_conv_full, norm_w, rg_conv_b, rg_gate_w,
               rg_lambda, ml_gate_b, ml_norm_w, gd_a_log, gd_dt_bias, gd_norm_w, final_norm_w):
    acts = []
    for l in range(DEPTH):
        nw = norm_w[l].reshape(1, D)
        proj = inproj(xs, nw, w_in_full[l])
        rg_p = (rg_conv_full[l], rg_conv_b[l].reshape(1, SEG), _block_diag(rg_gate_w[l, 0]), _block_diag(rg_gate_w[l, 1]),
                rg_gb_full[l], rg_lambda[l].reshape(1, SEG))
        y_rg, hs = rglru_fwd(proj, *rg_p)
        ml_p = (jnp.zeros((1, HD), f32).at[0, 0:2 * NH].set(ml_gate_b[l].reshape(-1)), ml_norm_w[l].reshape(1, SEG))
        y_ml, cs, ns, ms = mlstm_fwd(proj, *ml_p)
        gd_p = (gd_conv_full[l], _lane_row(gd_a_log[l], 2 * NH), _lane_row(gd_dt_bias[l], 2 * NH), gd_norm_w[l].reshape(1, HD))
        y_gd, ss = gdn_fwd(proj, *gd_p)
        acts.append((xs, nw, proj, rg_p, y_rg, hs, ml_p, y_ml, cs, ns, ms, gd_p, y_gd, ss))
        xs = outproj(xs, y_rg, y_ml, y_gd, w_out_full[l])

    dx, loss_part, d_fw = head(xs, final_norm_w.reshape(1, D), tgt)

    g_w_in, g_w_out = [None] * DEPTH, [None] * DEPTH
    g_small = {k: [None] * DEPTH for k in ("norm_w", "rg_conv_w", "rg_conv_b", "rg_gate_w", "rg_gate_b", "rg_lambda",
                                           "ml_gate_b", "ml_norm_w", "gd_conv_w", "gd_a_log", "gd_dt_bias", "gd_norm_w")}
    for l in reversed(range(DEPTH)):
        x_l, nw, proj, rg_p, y_rg, hs, ml_p, y_ml, cs, ns, ms, gd_p, y_gd, ss = acts[l]
        dy_rg, dy_ml, dy_gd, g_wo = outproj_bwd(dx, y_rg, y_ml, y_gd, w_out_full[l])
        d_rg, d_cw, d_cb, d_gr, d_gi, d_gb, d_lam = rglru_bwd(proj, hs, dy_rg, *rg_p)
        d_ml, d_sm_ml, d_bias, d_mnw = mlstm_bwd(proj, dy_ml, cs, ns, ms, *ml_p)
        d_gd, d_sm_gd, d_gcw, d_al, d_dt, d_gnw = gdn_bwd(proj, dy_gd, ss, *gd_p)
        dx, d_nw = inproj_bwd_x(x_l, nw, w_in_full[l], dx, d_rg, d_ml, d_gd, d_sm_ml, d_sm_gd)
        gw_rg = wgrad(x_l, nw, [d_rg], "wgrad_rg")
        gw_ml = wgrad(x_l, nw, [d_ml], "wgrad_ml")
        gw_gd = wgrad(x_l, nw, [d_gd], "wgrad_gd")
        gw_sm = wgrad(x_l, nw, [d_sm_ml, d_sm_gd], "wgrad_small")
        g_w_in[l] = jnp.concatenate([gw_rg, gw_ml, gw_sm[:, 0:2 * NH], gw_gd, gw_sm[:, 2 * NH:4 * NH]], axis=-1)
        g_w_out[l] = g_wo
        g_small["norm_w"][l] = d_nw[0]
        g_small["rg_conv_w"][l] = d_cw
        g_small["rg_conv_b"][l] = d_cb[0]
        g_small["rg_gate_w"][l] = jnp.stack([_diag_blocks(d_gr), _diag_blocks(d_gi)])
        g_small["rg_gate_b"][l] = d_gb
        g_small["rg_lambda"][l] = d_lam[0]
        g_small["ml_gate_b"][l] = d_bias[0, 0:2 * NH].reshape(2, NH)
        g_small["ml_norm_w"][l] = d_mnw[0]
        g_small["gd_conv_w"][l] = d_gcw
        g_small["gd_a_log"][l] = d_al[0, 2 * NH:3 * NH]
        g_small["gd_dt_bias"][l] = d_dt[0, 2 * NH:3 * NH]
        g_small["gd_norm_w"][l] = d_gnw[0]
    return loss_part, dx, d_fw, g_w_in, g_w_out, g_small


def finish_step(loss_part, dx, d_fw, g_w_in, g_w_out, g_small, s, me, given_w, given_m, given_v):
    gwi = jnp.stack(g_w_in).reshape(DEPTH, D, N_DEV, D_IN // N_DEV).transpose(2, 0, 1, 3).astype(bf16)
    gwo = jnp.stack(g_w_out).reshape(DEPTH, N_DEV, 3 * SEG // N_DEV, D).transpose(1, 0, 2, 3).astype(bf16)
    small_names = ["norm_w", "rg_conv_w", "rg_conv_b", "rg_gate_w", "rg_gate_b", "rg_lambda", "ml_gate_b", "ml_norm_w",
                   "gd_conv_w", "gd_a_log", "gd_dt_bias", "gd_norm_w"]
    small_list = [loss_part[0, 0:1], d_fw[0]] + [jnp.stack(g_small[k]) for k in small_names]
    small_shapes = [a.shape for a in small_list]
    gwi_r, gwo_r, gsm_r = exchange([gwi, gwo, _pack(small_list)], [True, True, False], "scatter_grads")

    g_wi, d_wi, m_wi, v_wi = adam_slots(gwi_r, given_w["w_in"], given_m["w_in"], given_v["w_in"], 256, "adam_w_in")
    g_wo, d_wo, m_wo, v_wo = adam_slots(gwo_r, given_w["w_out"], given_m["w_out"], given_v["w_out"], 192, "adam_w_out")
    g_all = _unpack(sum_slots(gsm_r, "sum_small"), small_shapes)
    loss = g_all[0][0]
    grads = {"final_norm_w": g_all[1]}
    for k, g in zip(small_names, g_all[2:]):
        grads[k] = g
    for k, width in (("rg_conv_w", 64), ("rg_gate_b", 64), ("gd_conv_w", 192)):
        grads[k] = lax.dynamic_slice_in_dim(grads[k], me * width, width, axis=2)
    names = small_names + ["final_norm_w"]
    shapes = [given_w[k].shape for k in names]
    d_p, m_p, v_p = adam_flat(_pack([grads[k] for k in names]), _pack([given_w[k] for k in names]),
                              _pack([given_m[k] for k in names]), _pack([given_v[k] for k in names]), "adam_small")
    deltas = dict(zip(names, _unpack(d_p, shapes)))
    new_m = dict(zip(names, _unpack(m_p, shapes)))
    new_v = dict(zip(names, _unpack(v_p, shapes)))
    grads["w_in"], deltas["w_in"], new_m["w_in"], new_v["w_in"] = g_wi, d_wi, m_wi, v_wi
    grads["w_out"], deltas["w_out"], new_m["w_out"], new_v["w_out"] = g_wo, d_wo, m_wo, v_wo

    order = ["norm_w", "w_in", "rg_conv_w", "rg_conv_b", "rg_gate_w", "rg_gate_b", "rg_lambda", "ml_gate_b", "ml_norm_w",
             "gd_conv_w", "gd_a_log", "gd_dt_bias", "gd_norm_w", "w_out", "final_norm_w"]
    return (loss, dx.reshape(1, s, D), *[grads[k] for k in order], *[deltas[k] for k in order],
            *[new_m[k] for k in order], *[new_v[k] for k in order])
```

```python
from typing import NamedTuple

import jax
import jax.numpy as jnp
from jax import lax
from jax.experimental import pallas as pl
from jax.experimental.pallas import tpu as pltpu

f32 = jnp.float32
bf16 = jnp.bfloat16
MESH = pl.DeviceIdType.MESH

N_DEV = 8
D = 1024
DEPTH = 4
EPS = 1e-6
SEG = 512
HD = 128
NH = 4
LC = 64
D_IN = 5648
DP = 5760
SMALL0 = 5632
RG_TILE = 256
RG_C = 8.0

ADAM_LR = 0.001
ADAM_B1 = 0.9
ADAM_B2 = 0.999
ADAM_EPS = 1e-08
ADAM_WD = 0.01
ADAM_STEP = 10


def _cp(vmem_mb):
    return pltpu.CompilerParams(vmem_limit_bytes=vmem_mb * 2 ** 20)


def _dot(a, b, ca, cb):
    return lax.dot_general(a.astype(bf16), b.astype(bf16), (((ca,), (cb,)), ((), ())), preferred_element_type=f32)


@jax.custom_vjp
def mm_nn(a, b):
    return _dot(a, b, 1, 0)


@jax.custom_vjp
def mm_nt(a, b):
    return _dot(a, b, 1, 1)


@jax.custom_vjp
def mm_tn(a, b):
    return _dot(a, b, 0, 0)


mm_nn.defvjp(lambda a, b: (mm_nn(a, b), (a, b)), lambda r, g: (mm_nt(g, r[1]), mm_tn(r[0], g)))
mm_nt.defvjp(lambda a, b: (mm_nt(a, b), (a, b)), lambda r, g: (mm_nn(g, r[1]), mm_tn(g, r[0])))
mm_tn.defvjp(lambda a, b: (mm_tn(a, b), (a, b)), lambda r, g: (mm_nt(r[1], g), mm_nn(r[0], g)))


def _split(x):
    hi = x.astype(bf16)
    return hi, (x - hi.astype(f32)).astype(bf16)


def dot3(a, b):
    ah, al = _split(a)
    bh, bl = _split(b)
    return jnp.dot(jnp.concatenate([ah, al, ah], axis=1), jnp.concatenate([bh, bh, bl], axis=0), preferred_element_type=f32)


def _tri_sum(x, reverse):
    n = x.shape[0]
    r = lax.broadcasted_iota(jnp.int32, (n, 3 * n), 0)
    c = lax.broadcasted_iota(jnp.int32, (n, 3 * n), 1) & (n - 1)
    ones = ((c >= r) if reverse else (c <= r)).astype(bf16)
    hi = x.astype(bf16)
    rest = x - hi.astype(f32)
    mid = rest.astype(bf16)
    lo = (rest - mid.astype(f32)).astype(bf16)
    return jnp.dot(ones, jnp.concatenate([hi, mid, lo], axis=0), preferred_element_type=f32)


@jax.custom_vjp
def cumsum_rows(x):
    return _tri_sum(x, False)


@jax.custom_vjp
def rev_cumsum_rows(x):
    return _tri_sum(x, True)


cumsum_rows.defvjp(lambda x: (cumsum_rows(x), None), lambda _, g: (rev_cumsum_rows(g),))
rev_cumsum_rows.defvjp(lambda x: (rev_cumsum_rows(x), None), lambda _, g: (cumsum_rows(g),))


def _tri(n, strict=False):
    r = lax.broadcasted_iota(jnp.int32, (n, n), 0)
    c = lax.broadcasted_iota(jnp.int32, (n, n), 1)
    return (r > c) if strict else (r >= c)


def _lane_col(v, j):
    lane = lax.broadcasted_iota(jnp.int32, v.shape, 1)
    return jnp.sum(jnp.where(lane == j, v, 0.0), axis=1, keepdims=True)


def _rows_from(pieces, rows, width):
    ridx = lax.broadcasted_iota(jnp.int32, (rows, width), 0)
    out = jnp.zeros((rows, width), f32)
    for h, p in enumerate(pieces):
        out = out + jnp.where(ridx == h, jnp.broadcast_to(p, (rows, width)), 0.0)
    return out


def causal_conv(halo8, x, w4):
    t = x.shape[0]
    xe = jnp.concatenate([halo8, x], axis=0)
    y = xe[5:5 + t] * w4[0:1]
    for k in range(1, 4):
        y = y + xe[5 + k:5 + k + t] * w4[k:k + 1]
    return y


def ml_chunk(q, k, v, o_pre, z, small, bias_row, norm_w, C, n, m):
    lane = lax.broadcasted_iota(jnp.int32, small.shape, 1)
    pre = small + bias_row
    lg = jnp.where(lane < NH, pre, jnp.where(lane < 2 * NH, jax.nn.log_sigmoid(pre), 0.0))
    bc = cumsum_rows(lg)
    lg_t = lg.T
    bc_t = bc.T
    causal = _tri(LC)
    ys, c_new, n_new, m_new_rows = [], [], [], []
    for h in range(NH):
        sl = slice(h * HD, (h + 1) * HD)
        qh = q[:, sl] * (HD ** -0.5)
        kh = k[:, sl]
        vh = v[:, sl]
        li_col = _lane_col(lg, h)
        b_col = _lane_col(bc, NH + h)
        li_row = lg_t[h:h + 1, :]
        b_row = bc_t[NH + h:NH + h + 1, :]
        m_h = m[h:h + 1, 0:1]
        c_h = C[sl, :]
        n_h = n[h:h + 1, :]
        dmat = jnp.where(causal, b_col - b_row + li_row, -jnp.inf)
        m_inter = b_col + m_h
        m_t = jnp.maximum(m_inter, jnp.max(dmat, axis=-1, keepdims=True))
        p = jnp.exp(dmat - m_t)
        s = mm_nt(qh, kh) * p
        sc = jnp.exp(m_inter - m_t)
        num = mm_nn(s, vh) + sc * mm_nn(qh, c_h)
        den = jnp.sum(s, axis=-1, keepdims=True) + sc * jnp.sum(qh * n_h, axis=-1, keepdims=True)
        hh = num / jnp.maximum(jnp.abs(den), jnp.exp(-m_t))
        g = b_col[LC - 1:LC, :]
        ws = g - b_col + li_col
        m_new = jnp.maximum(g + m_h, jnp.max(ws, axis=0, keepdims=True))
        dec = jnp.exp(g + m_h - m_new)
        wk = jnp.exp(ws - m_new)
        kw = wk * kh
        c_new.append(dec * c_h + mm_tn(kw, vh))
        n_new.append(dec * n_h + jnp.sum(kw, axis=0, keepdims=True))
        m_new_rows.append(jnp.broadcast_to(m_new, (1, HD)))
        hn = hh * lax.rsqrt(jnp.mean(hh * hh, axis=-1, keepdims=True) + EPS) * norm_w[:, sl]
        ys.append(hn * jax.nn.sigmoid(o_pre[:, sl]) * jax.nn.silu(z[:, sl]))
    y = jnp.concatenate(ys, axis=1)
    return y, jnp.concatenate(c_new, axis=0), _rows_from(n_new, 8, HD), _rows_from(m_new_rows, 8, HD)


@jax.custom_vjp
def _unit_lower_inverse(mat):
    eye = (lax.broadcasted_iota(jnp.int32, (LC, LC), 0) == lax.broadcasted_iota(jnp.int32, (LC, LC), 1)).astype(f32)
    p = -mat
    t = eye + p
    for _ in range(5):
        p = dot3(p, p)
        t = t + dot3(t, p)
    return t


def _unit_lower_inverse_fwd(mat):
    t = _unit_lower_inverse(mat)
    return t, t


def _unit_lower_inverse_bwd(t, g):
    tt = t.T
    return (-dot3(dot3(tt, g), tt),)


_unit_lower_inverse.defvjp(_unit_lower_inverse_fwd, _unit_lower_inverse_bwd)


def gd_chunk(qh8, q, kh8, k, vh8, v, z, small, conv_w, alog_row, dt_row, norm_w, st):
    lane = lax.broadcasted_iota(jnp.int32, small.shape, 1)
    is_a = (lane >= 2 * NH) & (lane < 3 * NH)
    g_all = jnp.where(is_a, -jnp.exp(alog_row) * jax.nn.softplus(small + dt_row), 0.0)
    beta_all = jax.nn.sigmoid(small)
    gc_all = cumsum_rows(g_all)
    gc_t = gc_all.T
    qc = jax.nn.silu(causal_conv(qh8, q, conv_w[:, 0:SEG]))
    kc = jax.nn.silu(causal_conv(kh8, k, conv_w[:, SEG:2 * SEG]))
    vc = jax.nn.silu(causal_conv(vh8, v, conv_w[:, 2 * SEG:3 * SEG]))
    incl = _tri(LC)
    strict = _tri(LC, strict=True)
    ys, st_new = [], []
    for h in range(NH):
        sl = slice(h * HD, (h + 1) * HD)
        qh = qc[:, sl]
        kh = kc[:, sl]
        vh = vc[:, sl]
        qh = qh * lax.rsqrt(jnp.sum(qh * qh, axis=-1, keepdims=True) + EPS) * (HD ** -0.5)
        kh = kh * lax.rsqrt(jnp.sum(kh * kh, axis=-1, keepdims=True) + EPS)
        beta = _lane_col(beta_all, 3 * NH + h)
        gc = _lane_col(gc_all, 2 * NH + h)
        gc_row = gc_t[2 * NH + h:2 * NH + h + 1, :]
        s_h = st[sl, :]
        gam = jnp.exp(jnp.where(incl, gc - gc_row, -jnp.inf))
        kb = kh * beta
        m_strict = jnp.where(strict, mm_nt(kb, kh) * gam, 0.0)
        t_inv = _unit_lower_inverse(m_strict)
        u = mm_nn(t_inv, vh * beta)
        w = mm_nn(t_inv, kb * jnp.exp(gc))
        aqk = mm_nt(qh, kh) * gam
        q_dec = qh * jnp.exp(gc)
        g_last = gc[LC - 1:LC, :]
        k_dec = kh * jnp.exp(g_last - gc)
        v_new = u - mm_nn(w, s_h)
        o = mm_nn(q_dec, s_h) + mm_nn(aqk, v_new)
        st_new.append(s_h * jnp.exp(g_last) + mm_tn(k_dec, v_new))
        on = o * lax.rsqrt(jnp.mean(o * o, axis=-1, keepdims=True) + EPS) * norm_w
        ys.append(on * jax.nn.silu(z[:, sl]))
    return jnp.concatenate(ys, axis=1), jnp.concatenate(st_new, axis=0)


def rg_pre(xh8, x, conv_w, conv_b, gw_r, gw_i, gate_b, lam):
    xc = causal_conv(xh8, x, conv_w) + conv_b
    r = jax.nn.sigmoid(mm_nn(xc, gw_r) + gate_b[0:1])
    i = jax.nn.sigmoid(mm_nn(xc, gw_i) + gate_b[1:2])
    log_a = -RG_C * r * jax.nn.softplus(-lam)
    a = jnp.exp(log_a)
    th = jnp.tanh(log_a)
    one_minus_a2 = -2.0 * th / (1.0 - th)
    b = jnp.sqrt(one_minus_a2) * (i * xc)
    return a, b


def _scan_rows(a_ref, b_ref, out_ref, h0, n_rows, reverse):
    n_groups = n_rows // 8
    width = a_ref.shape[1]
    row = lax.broadcasted_iota(jnp.int32, (8, width), 0)

    def body(j, h):
        g = (n_groups - 1 - j) if reverse else j
        r0 = pl.multiple_of(g * 8, 8)
        av = a_ref[pl.ds(r0, 8), :]
        bv = b_ref[pl.ds(r0, 8), :]
        for d in (1, 2, 4):
            sh = (8 - d) if reverse else d
            a_s = pltpu.roll(av, sh, 0)
            b_s = pltpu.roll(bv, sh, 0)
            valid = (row < 8 - d) if reverse else (row >= d)
            bv = jnp.where(valid, av * b_s + bv, bv)
            av = jnp.where(valid, av * a_s, av)
        hv = av * h + bv
        out_ref[pl.ds(r0, 8), :] = hv
        return hv[0:1, :] if reverse else hv[7:8, :]

    return lax.fori_loop(0, n_groups, body, h0)


def _norm_rows(xv):
    r = lax.rsqrt(jnp.mean(xv * xv, axis=-1, keepdims=True) + EPS)
    return xv * r, r


def inproj(x, nw, w):
    s, ts = x.shape[0], 256

    def body(x_ref, nw_ref, w_ref, o_ref):
        xn, _ = _norm_rows(x_ref[...])
        hn = xn * nw_ref[...]
        o_ref[...] = jnp.dot(hn.astype(bf16), w_ref[...], preferred_element_type=f32)

    return pl.pallas_call(
        body, grid=(s // ts,),
        in_specs=[pl.BlockSpec((ts, D), lambda i: (i, 0)), pl.BlockSpec((1, D), lambda i: (0, 0)),
                  pl.BlockSpec((D, DP), lambda i: (0, 0))],
        out_specs=pl.BlockSpec((ts, DP), lambda i: (i, 0)),
        out_shape=jax.ShapeDtypeStruct((s, DP), f32), name="inproj", compiler_params=_cp(56))(x, nw, w)


def outproj(x, yr, ym, yg, wo):
    s, ts = x.shape[0], 256

    def body(x_ref, yr_ref, ym_ref, yg_ref, wo_ref, o_ref):
        acc = x_ref[...]
        for j, y_ref in enumerate((yr_ref, ym_ref, yg_ref)):
            acc = acc + jnp.dot(y_ref[...].astype(bf16), wo_ref[j * SEG:(j + 1) * SEG, :], preferred_element_type=f32)
        o_ref[...] = acc

    yspec = pl.BlockSpec((ts, SEG), lambda i: (i, 0))
    return pl.pallas_call(
        body, grid=(s // ts,),
        in_specs=[pl.BlockSpec((ts, D), lambda i: (i, 0)), yspec, yspec, yspec,
                  pl.BlockSpec((3 * SEG, D), lambda i: (0, 0))],
        out_specs=pl.BlockSpec((ts, D), lambda i: (i, 0)),
        out_shape=jax.ShapeDtypeStruct((s, D), f32), name="outproj", compiler_params=_cp(40))(x, yr, ym, yg, wo)


def head(x, fw, tgt):
    s, ts = x.shape[0], 256

    def body(x_ref, fw_ref, t_ref, dx_ref, loss_ref, dfw_ref):
        @pl.when(pl.program_id(0) == 0)
        def _():
            loss_ref[...] = jnp.zeros_like(loss_ref)
            dfw_ref[...] = jnp.zeros_like(dfw_ref)

        xn, r = _norm_rows(x_ref[...])
        fw_v = fw_ref[...]
        err = xn * fw_v - t_ref[...]
        loss_ref[...] += 0.5 * jnp.sum(jnp.mean(err * err, axis=-1, keepdims=True))
        dy = err * (1.0 / D)
        dfw_ref[0:1, :] += jnp.sum(dy * xn, axis=0, keepdims=True)
        dxn = dy * fw_v
        dx_ref[...] = r * (dxn - xn * jnp.mean(dxn * xn, axis=-1, keepdims=True))

    tile = pl.BlockSpec((ts, D), lambda i: (i, 0))
    return pl.pallas_call(
        body, grid=(s // ts,),
        in_specs=[tile, pl.BlockSpec((1, D), lambda i: (0, 0)), tile],
        out_specs=[tile, pl.BlockSpec((8, 128), lambda i: (0, 0)), pl.BlockSpec((8, D), lambda i: (0, 0))],
        out_shape=[jax.ShapeDtypeStruct((s, D), f32), jax.ShapeDtypeStruct((8, 128), f32),
                   jax.ShapeDtypeStruct((8, D), f32)], name="head")(x, fw, tgt)


def outproj_bwd(dx, yr, ym, yg, wo):
    s, ts = dx.shape[0], 256

    def body(dx_ref, yr_ref, ym_ref, yg_ref, wo_ref, dyr_ref, dym_ref, dyg_ref, gwo_ref):
        @pl.when(pl.program_id(0) == 0)
        def _():
            gwo_ref[...] = jnp.zeros_like(gwo_ref)

        dxb = dx_ref[...].astype(bf16)
        for j, (y_ref, dy_ref) in enumerate(((yr_ref, dyr_ref), (ym_ref, dym_ref), (yg_ref, dyg_ref))):
            rows = slice(j * SEG, (j + 1) * SEG)
            dy_ref[...] = lax.dot_general(dxb, wo_ref[rows, :], (((1,), (1,)), ((), ())), preferred_element_type=f32)
            gwo_ref[rows, :] += lax.dot_general(y_ref[...].astype(bf16), dxb, (((0,), (0,)), ((), ())),
                                                preferred_element_type=f32)

    yspec = pl.BlockSpec((ts, SEG), lambda i: (i, 0))
    wspec = pl.BlockSpec((3 * SEG, D), lambda i: (0, 0))
    ysh = jax.ShapeDtypeStruct((s, SEG), f32)
    return pl.pallas_call(
        body, grid=(s // ts,),
        in_specs=[pl.BlockSpec((ts, D), lambda i: (i, 0)), yspec, yspec, yspec, wspec],
        out_specs=[yspec, yspec, yspec, wspec],
        out_shape=[ysh, ysh, ysh, jax.ShapeDtypeStruct((3 * SEG, D), f32)],
        name="outproj_bwd", compiler_params=_cp(48))(dx, yr, ym, yg, wo)


def inproj_bwd_x(x, nw, w, dxo, d_rg, d_ml, d_gd, d_sa, d_sb):
    s, ts = x.shape[0], 256
    widths = (d_rg.shape[1], d_ml.shape[1], d_gd.shape[1], HD)

    def body(x_ref, nw_ref, w_ref, dxo_ref, rg_ref, ml_ref, gd_ref, sa_ref, sb_ref, dx_ref, dnw_ref):
        @pl.when(pl.program_id(0) == 0)
        def _():
            dnw_ref[...] = jnp.zeros_like(dnw_ref)

        xn, r = _norm_rows(x_ref[...])
        pieces = (rg_ref[...], ml_ref[...], gd_ref[...], sa_ref[...] + sb_ref[...])
        dhn = jnp.zeros((ts, D), f32)
        c0 = 0
        for piece, wd in zip(pieces, widths):
            dhn = dhn + lax.dot_general(piece.astype(bf16), w_ref[:, c0:c0 + wd], (((1,), (1,)), ((), ())),
                                        preferred_element_type=f32)
            c0 += wd
        dnw_ref[0:1, :] += jnp.sum(dhn * xn, axis=0, keepdims=True)
        dxn = dhn * nw_ref[...]
        dx_ref[...] = dxo_ref[...] + r * (dxn - xn * jnp.mean(dxn * xn, axis=-1, keepdims=True))

    tile = pl.BlockSpec((ts, D), lambda i: (i, 0))
    return pl.pallas_call(
        body, grid=(s // ts,),
        in_specs=[tile, pl.BlockSpec((1, D), lambda i: (0, 0)), pl.BlockSpec((D, DP), lambda i: (0, 0)), tile]
        + [pl.BlockSpec((ts, wd), lambda i: (i, 0)) for wd in widths] + [pl.BlockSpec((ts, HD), lambda i: (i, 0))],
        out_specs=[tile, pl.BlockSpec((8, D), lambda i: (0, 0))],
        out_shape=[jax.ShapeDtypeStruct((s, D), f32), jax.ShapeDtypeStruct((8, D), f32)],
        name="inproj_bwd_x", compiler_params=_cp(56))(x, nw, w, dxo, d_rg, d_ml, d_gd, d_sa, d_sb)


def wgrad(x, nw, dps, name):
    s, ts = x.shape[0], 512
    c = dps[0].shape[1]
    ct = min(c, SEG)
    n_dp = len(dps)

    def body(*refs):
        x_ref, nw_ref = refs[0], refs[1]
        dp_refs = refs[2:2 + n_dp]
        o_ref = refs[2 + n_dp]

        @pl.when(pl.program_id(1) == 0)
        def _():
            o_ref[...] = jnp.zeros_like(o_ref)

        xn, _ = _norm_rows(x_ref[...])
        hn = (xn * nw_ref[...]).astype(bf16)
        dp = dp_refs[0][...]
        for extra in dp_refs[1:]:
            dp = dp + extra[...]
        o_ref[...] += lax.dot_general(hn, dp.astype(bf16), (((0,), (0,)), ((), ())), preferred_element_type=f32)

    return pl.pallas_call(
        body, grid=(c // ct, s // ts),
        in_specs=[pl.BlockSpec((ts, D), lambda j, i: (i, 0)), pl.BlockSpec((1, D), lambda j, i: (0, 0))]
        + [pl.BlockSpec((ts, ct), lambda j, i: (i, j)) for _ in dps],
        out_specs=pl.BlockSpec((D, ct), lambda j, i: (0, j)),
        out_shape=jax.ShapeDtypeStruct((D, c), f32), name=name, compiler_params=_cp(40))(x, nw, *dps)


def _seg_spec(rows, seg, n_tiles=None):
    if n_tiles is None:
        return pl.BlockSpec((rows, SEG), lambda i: (i, seg))
    return pl.BlockSpec((rows, SEG), lambda i: (n_tiles - 1 - i, seg))


def _halo_spec(rows, seg, n_tiles=None):
    per = rows // 8
    if n_tiles is None:
        return pl.BlockSpec((8, SEG), lambda i: (jnp.maximum(i * per - 1, 0), seg))
    return pl.BlockSpec((8, SEG), lambda i: (jnp.maximum((n_tiles - 1 - i) * per - 1, 0), seg))


def _const_spec(shape):
    return pl.BlockSpec(shape, lambda i: tuple(0 for _ in shape))


def rglru_fwd(proj, conv_w, conv_b, gw_r, gw_i, gate_b, lam):
    s = proj.shape[0]
    tr = RG_TILE

    def body(xh_ref, x_ref, z_ref, cw_ref, cb_ref, gr_ref, gi_ref, gb_ref, lam_ref, y_ref, h_ref, a_s, b_s, hc):
        first = pl.program_id(0) == 0

        @pl.when(first)
        def _():
            hc[...] = jnp.zeros_like(hc)

        xh = jnp.where(first, 0.0, xh_ref[...])
        a, b = rg_pre(xh, x_ref[...], cw_ref[...], cb_ref[...], gr_ref[...], gi_ref[...], gb_ref[...], lam_ref[...])
        a_s[...] = a
        b_s[...] = b
        hc[0:1, :] = _scan_rows(a_s, b_s, h_ref, hc[0:1, :], tr, False)
        y_ref[...] = h_ref[...] * jax.nn.silu(z_ref[...])

    out = pl.BlockSpec((tr, SEG), lambda i: (i, 0))
    return pl.pallas_call(
        body, grid=(s // tr,),
        in_specs=[_halo_spec(tr, 0), _seg_spec(tr, 0), _seg_spec(tr, 1), _const_spec((4, SEG)), _const_spec((1, SEG)),
                  _const_spec((SEG, SEG)), _const_spec((SEG, SEG)), _const_spec((2, SEG)), _const_spec((1, SEG))],
        out_specs=[out, out],
        out_shape=[jax.ShapeDtypeStruct((s, SEG), f32), jax.ShapeDtypeStruct((s, SEG), f32)],
        scratch_shapes=[pltpu.VMEM((tr, SEG), f32), pltpu.VMEM((tr, SEG), f32), pltpu.VMEM((8, SEG), f32)],
        name="rglru_fwd", compiler_params=_cp(40))(proj, proj, proj, conv_w, conv_b, gw_r, gw_i, gate_b, lam)


def rglru_bwd(proj, hs, dy, conv_w, conv_b, gw_r, gw_i, gate_b, lam):
    s = proj.shape[0]
    tr = RG_TILE
    nt = s // tr

    def body(xh_ref, x_ref, z_ref, hh_ref, h_ref, dy_ref, cw_ref, cb_ref, gr_ref, gi_ref, gb_ref, lam_ref,
             dp_ref, dcw_ref, dcb_ref, dgr_ref, dgi_ref, dgb_ref, dlam_ref,
             an_s, g_s, dh_s, a_first, dh_first, dhalo):
        i = pl.program_id(0)
        first_tile = i == nt - 1

        @pl.when(i == 0)
        def _():
            for ref in (dcw_ref, dcb_ref, dgr_ref, dgi_ref, dgb_ref, dlam_ref, a_first, dh_first, dhalo):
                ref[...] = jnp.zeros_like(ref)

        xh = jnp.where(first_tile, 0.0, xh_ref[...])
        params = (cw_ref[...], cb_ref[...], gr_ref[...], gi_ref[...], gb_ref[...], lam_ref[...])
        (a, _), vjp = jax.vjp(rg_pre, xh, x_ref[...], *params)
        zv = z_ref[...]
        hv = h_ref[...]
        dyv = dy_ref[...]
        sig = jax.nn.sigmoid(zv)
        g_s[...] = dyv * (zv * sig)
        dp_ref[:, SEG:2 * SEG] = dyv * hv * (sig * (1.0 + zv * (1.0 - sig)))
        ridx = lax.broadcasted_iota(jnp.int32, (tr, SEG), 0)
        an_s[...] = jnp.where(ridx == tr - 1, jnp.broadcast_to(a_first[0:1, :], (tr, SEG)), pltpu.roll(a, tr - 1, 0))
        _scan_rows(an_s, g_s, dh_s, dh_first[0:1, :], tr, True)
        dh = dh_s[...]
        h_prev_last = jnp.where(first_tile, 0.0, hh_ref[...])[7:8, :]
        h_prev = pltpu.roll(hv, 1, 0)
        h_prev = jnp.where(ridx == 0, jnp.broadcast_to(h_prev_last, (tr, SEG)), h_prev)
        dxh, dx, dcw, dcb, dgr, dgi, dgb, dlam = vjp((dh * h_prev, dh))
        dp_ref[:, 0:SEG] = dx
        dp_ref[tr - 8:tr, 0:SEG] += dhalo[...]
        dhalo[...] = dxh
        a_first[0:1, :] = a[0:1, :]
        dh_first[0:1, :] = dh[0:1, :]
        dcw_ref[...] += dcw
        dcb_ref[...] += dcb
        dgr_ref[...] += dgr
        dgi_ref[...] += dgi
        dgb_ref[...] += dgb
        dlam_ref[...] += dlam

    pspecs = [_const_spec((4, SEG)), _const_spec((1, SEG)), _const_spec((SEG, SEG)), _const_spec((SEG, SEG)),
              _const_spec((2, SEG)), _const_spec((1, SEG))]
    pshapes = [jax.ShapeDtypeStruct(sh, f32) for sh in ((4, SEG), (1, SEG), (SEG, SEG), (SEG, SEG), (2, SEG), (1, SEG))]
    tile = pl.BlockSpec((tr, SEG), lambda i: (nt - 1 - i, 0))
    return pl.pallas_call(
        body, grid=(nt,),
        in_specs=[_halo_spec(tr, 0, nt), _seg_spec(tr, 0, nt), _seg_spec(tr, 1, nt),
                  pl.BlockSpec((8, SEG), lambda i: (jnp.maximum((nt - 1 - i) * (tr // 8) - 1, 0), 0)), tile, tile] + pspecs,
        out_specs=[pl.BlockSpec((tr, 2 * SEG), lambda i: (nt - 1 - i, 0))] + pspecs,
        out_shape=[jax.ShapeDtypeStruct((s, 2 * SEG), f32)] + pshapes,
        scratch_shapes=[pltpu.VMEM((tr, SEG), f32), pltpu.VMEM((tr, SEG), f32), pltpu.VMEM((tr, SEG), f32),
                        pltpu.VMEM((8, SEG), f32), pltpu.VMEM((8, SEG), f32), pltpu.VMEM((8, SEG), f32)],
        name="rglru_bwd", compiler_params=_cp(48))(proj, proj, proj, hs, hs, dy, conv_w, conv_b, gw_r, gw_i, gate_b, lam)


ML_SEGS = (2, 3, 4, 5, 6)
SMALL_BLK = SMALL0 // HD


def mlstm_fwd(proj, bias_row, norm_w):
    s = proj.shape[0]
    nc = s // LC

    def body(q_ref, k_ref, v_ref, o_ref, z_ref, sm_ref, b_ref, nw_ref, y_ref, cs_ref, ns_ref, ms_ref, c_s, n_s, m_s):
        @pl.when(pl.program_id(0) == 0)
        def _():
            c_s[...] = jnp.zeros_like(c_s)
            n_s[...] = jnp.zeros_like(n_s)
            m_s[...] = jnp.zeros_like(m_s)

        cs_ref[0] = c_s[...]
        ns_ref[0] = n_s[...]
        ms_ref[0] = m_s[...]
        y, c2, n2, m2 = ml_chunk(q_ref[...], k_ref[...], v_ref[...], o_ref[...], z_ref[...], sm_ref[...],
                                 b_ref[...], nw_ref[...], c_s[...], n_s[...], m_s[...])
        y_ref[...] = y
        c_s[...] = c2
        n_s[...] = n2
        m_s[...] = m2

    return pl.pallas_call(
        body, grid=(nc,),
        in_specs=[_seg_spec(LC, sg) for sg in ML_SEGS]
        + [pl.BlockSpec((LC, HD), lambda i: (i, SMALL_BLK)), _const_spec((1, HD)), _const_spec((1, SEG))],
        out_specs=[pl.BlockSpec((LC, SEG), lambda i: (i, 0)), pl.BlockSpec((1, SEG, HD), lambda i: (i, 0, 0)),
                   pl.BlockSpec((1, 8, HD), lambda i: (i, 0, 0)), pl.BlockSpec((1, 8, HD), lambda i: (i, 0, 0))],
        out_shape=[jax.ShapeDtypeStruct((s, SEG), f32), jax.ShapeDtypeStruct((nc, SEG, HD), f32),
                   jax.ShapeDtypeStruct((nc, 8, HD), f32), jax.ShapeDtypeStruct((nc, 8, HD), f32)],
        scratch_shapes=[pltpu.VMEM((SEG, HD), f32), pltpu.VMEM((8, HD), f32), pltpu.VMEM((8, HD), f32)],
        name="mlstm_fwd")(proj, proj, proj, proj, proj, proj, bias_row, norm_w)


def mlstm_bwd(proj, dy, cs, ns, ms, bias_row, norm_w):
    s = proj.shape[0]
    nc = s // LC

    def body(q_ref, k_ref, v_ref, o_ref, z_ref, sm_ref, dy_ref, cs_ref, ns_ref, ms_ref, b_ref, nw_ref,
             dp_ref, dsm_ref, db_ref, dnw_ref, dc_s, dn_s, dm_s):
        @pl.when(pl.program_id(0) == 0)
        def _():
            for ref in (db_ref, dnw_ref, dc_s, dn_s, dm_s):
                ref[...] = jnp.zeros_like(ref)

        _, vjp = jax.vjp(ml_chunk, q_ref[...], k_ref[...], v_ref[...], o_ref[...], z_ref[...], sm_ref[...],
                         b_ref[...], nw_ref[...], cs_ref[0], ns_ref[0], ms_ref[0])
        dq, dk, dv, do, dz, dsm, db, dnw, dc, dn, dm = vjp((dy_ref[...], dc_s[...], dn_s[...], dm_s[...]))
        for j, val in enumerate((dq, dk, dv, do, dz)):
            dp_ref[:, j * SEG:(j + 1) * SEG] = val
        dsm_ref[...] = dsm
        db_ref[0:1, :] += db
        dnw_ref[0:1, :] += dnw
        dc_s[...] = dc
        dn_s[...] = dn
        dm_s[...] = dm

    rev3 = lambda i: (nc - 1 - i, 0, 0)
    return pl.pallas_call(
        body, grid=(nc,),
        in_specs=[_seg_spec(LC, sg, nc) for sg in ML_SEGS]
        + [pl.BlockSpec((LC, HD), lambda i: (nc - 1 - i, SMALL_BLK)), pl.BlockSpec((LC, SEG), lambda i: (nc - 1 - i, 0)),
           pl.BlockSpec((1, SEG, HD), rev3), pl.BlockSpec((1, 8, HD), rev3), pl.BlockSpec((1, 8, HD), rev3),
           _const_spec((1, HD)), _const_spec((1, SEG))],
        out_specs=[pl.BlockSpec((LC, 5 * SEG), lambda i: (nc - 1 - i, 0)), pl.BlockSpec((LC, HD), lambda i: (nc - 1 - i, 0)),
                   _const_spec((8, HD)), _const_spec((8, SEG))],
        out_shape=[jax.ShapeDtypeStruct((s, 5 * SEG), f32), jax.ShapeDtypeStruct((s, HD), f32),
                   jax.ShapeDtypeStruct((8, HD), f32), jax.ShapeDtypeStruct((8, SEG), f32)],
        scratch_shapes=[pltpu.VMEM((SEG, HD), f32), pltpu.VMEM((8, HD), f32), pltpu.VMEM((8, HD), f32)],
        name="mlstm_bwd", compiler_params=_cp(48))(proj, proj, proj, proj, proj, proj, dy, cs, ns, ms, bias_row, norm_w)


GD_SEGS = (7, 8, 9)


def _carried(xf, n_in, n_out):
    if xf is None:
        return [], [], [], [], {}, [], 0, 0
    operands, in_specs, out_shapes, out_specs, aliases, sems = _xfer_plumb(xf, n_in, n_out)
    return operands, in_specs, out_shapes, out_specs, aliases, sems, len(operands), len(xf.lands)


def gdn_fwd(proj, conv_w, alog_row, dt_row, norm_w, xf=None):
    s = proj.shape[0]
    nc = s // LC
    n_in, n_out = 12, 2
    x_ops, x_in_specs, x_out_shapes, x_out_specs, aliases, x_sems, n_xin, n_land = _carried(xf, n_in, n_out)

    def body(*refs):
        qh_ref, q_ref, kh_ref, k_ref, vh_ref, v_ref, z_ref, sm_ref, cw_ref, al_ref, dt_ref, nw_ref = refs[:n_in]
        x_refs = refs[n_in:n_in + n_xin]
        y_ref, ss_ref = refs[n_in + n_xin:n_in + n_xin + n_out]
        land_refs = refs[n_in + n_xin + n_out:n_in + n_xin + n_out + n_land]
        st_s = refs[n_in + n_xin + n_out + n_land]
        sem_refs = refs[n_in + n_xin + n_out + n_land + 1:]
        i = pl.program_id(0)
        first = i == 0

        @pl.when(first)
        def _():
            st_s[...] = jnp.zeros_like(st_s)
            if xf is not None:
                _xfer_start(xf, x_refs[:len(xf.srcs)], land_refs, sem_refs)

        ss_ref[0] = st_s[...]
        halo = [jnp.where(first, 0.0, r[...]) for r in (qh_ref, kh_ref, vh_ref)]
        y, st2 = gd_chunk(halo[0], q_ref[...], halo[1], k_ref[...], halo[2], v_ref[...], z_ref[...], sm_ref[...],
                          cw_ref[...], al_ref[...], dt_ref[...], nw_ref[...], st_s[...])
        y_ref[...] = y
        st_s[...] = st2

        if xf is not None:
            @pl.when(i == nc - 1)
            def _():
                _xfer_wait(xf, x_refs[:len(xf.srcs)], land_refs, sem_refs)

    qkv_specs = []
    for sg in GD_SEGS:
        qkv_specs += [_halo_spec(LC, sg), _seg_spec(LC, sg)]
    return pl.pallas_call(
        body, grid=(nc,),
        in_specs=qkv_specs + [_seg_spec(LC, 10), pl.BlockSpec((LC, HD), lambda i: (i, SMALL_BLK)),
                              _const_spec((4, 3 * SEG)), _const_spec((1, HD)), _const_spec((1, HD)), _const_spec((1, HD))]
        + x_in_specs,
        out_specs=[pl.BlockSpec((LC, SEG), lambda i: (i, 0)), pl.BlockSpec((1, SEG, HD), lambda i: (i, 0, 0))] + x_out_specs,
        out_shape=[jax.ShapeDtypeStruct((s, SEG), f32), jax.ShapeDtypeStruct((nc, SEG, HD), f32)] + x_out_shapes,
        scratch_shapes=[pltpu.VMEM((SEG, HD), f32)] + x_sems, input_output_aliases=aliases,
        name="gdn_fwd" if xf is None else "gdn_fwd_carrying")(
            proj, proj, proj, proj, proj, proj, proj, proj, conv_w, alog_row, dt_row, norm_w, *x_ops)


def gdn_bwd(proj, dy, ss, conv_w, alog_row, dt_row, norm_w, xf=None):
    s = proj.shape[0]
    nc = s // LC
    n_in, n_out = 14, 6
    x_ops, x_in_specs, x_out_shapes, x_out_specs, aliases, x_sems, n_xin, n_land = _carried(xf, n_in, n_out)

    def body(*refs):
        (qh_ref, q_ref, kh_ref, k_ref, vh_ref, v_ref, z_ref, sm_ref, dy_ref, ss_ref,
         cw_ref, al_ref, dt_ref, nw_ref) = refs[:n_in]
        x_refs = refs[n_in:n_in + n_xin]
        dp_ref, dsm_ref, dcw_ref, dal_ref, ddt_ref, dnw_ref = refs[n_in + n_xin:n_in + n_xin + n_out]
        land_refs = refs[n_in + n_xin + n_out:n_in + n_xin + n_out + n_land]
        dst_s, dhalo = refs[n_in + n_xin + n_out + n_land:n_in + n_xin + n_out + n_land + 2]
        sem_refs = refs[n_in + n_xin + n_out + n_land + 2:]
        i = pl.program_id(0)
        first_chunk = i == nc - 1

        @pl.when(i == 0)
        def _():
            for ref in (dcw_ref, dal_ref, ddt_ref, dnw_ref, dst_s, dhalo):
                ref[...] = jnp.zeros_like(ref)
            if xf is not None:
                _xfer_start(xf, x_refs[:len(xf.srcs)], land_refs, sem_refs)

        halo = [jnp.where(first_chunk, 0.0, r[...]) for r in (qh_ref, kh_ref, vh_ref)]
        _, vjp = jax.vjp(gd_chunk, halo[0], q_ref[...], halo[1], k_ref[...], halo[2], v_ref[...], z_ref[...], sm_ref[...],
                         cw_ref[...], al_ref[...], dt_ref[...], nw_ref[...], ss_ref[0])
        dqh, dq, dkh, dk, dvh, dv, dz, dsm, dcw, dal, ddt, dnw, dst = vjp((dy_ref[...], dst_s[...]))
        for j, val in enumerate((dq, dk, dv, dz)):
            dp_ref[:, j * SEG:(j + 1) * SEG] = val
        dp_ref[LC - 8:LC, 0:3 * SEG] += dhalo[...]
        for j, val in enumerate((dqh, dkh, dvh)):
            dhalo[:, j * SEG:(j + 1) * SEG] = val
        dsm_ref[...] = dsm
        dcw_ref[...] += dcw
        dal_ref[0:1, :] += dal
        ddt_ref[0:1, :] += ddt
        dnw_ref[0:1, :] += dnw
        dst_s[...] = dst

        if xf is not None:
            @pl.when(first_chunk)
            def _():
                _xfer_wait(xf, x_refs[:len(xf.srcs)], land_refs, sem_refs)

    qkv_specs = []
    for sg in GD_SEGS:
        qkv_specs += [_halo_spec(LC, sg, nc), _seg_spec(LC, sg, nc)]
    return pl.pallas_call(
        body, grid=(nc,),
        in_specs=qkv_specs + [_seg_spec(LC, 10, nc), pl.BlockSpec((LC, HD), lambda i: (nc - 1 - i, SMALL_BLK)),
                              pl.BlockSpec((LC, SEG), lambda i: (nc - 1 - i, 0)),
                              pl.BlockSpec((1, SEG, HD), lambda i: (nc - 1 - i, 0, 0)),
                              _const_spec((4, 3 * SEG)), _const_spec((1, HD)), _const_spec((1, HD)), _const_spec((1, HD))]
        + x_in_specs,
        out_specs=[pl.BlockSpec((LC, 4 * SEG), lambda i: (nc - 1 - i, 0)), pl.BlockSpec((LC, HD), lambda i: (nc - 1 - i, 0)),
                   _const_spec((4, 3 * SEG)), _const_spec((8, HD)), _const_spec((8, HD)), _const_spec((8, HD))] + x_out_specs,
        out_shape=[jax.ShapeDtypeStruct((s, 4 * SEG), f32), jax.ShapeDtypeStruct((s, HD), f32),
                   jax.ShapeDtypeStruct((4, 3 * SEG), f32), jax.ShapeDtypeStruct((8, HD), f32),
                   jax.ShapeDtypeStruct((8, HD), f32), jax.ShapeDtypeStruct((8, HD), f32)] + x_out_shapes,
        scratch_shapes=[pltpu.VMEM((SEG, HD), f32), pltpu.VMEM((8, 3 * SEG), f32)] + x_sems, input_output_aliases=aliases,
        name="gdn_bwd" if xf is None else "gdn_bwd_carrying", compiler_params=_cp(48))(
            proj, proj, proj, proj, proj, proj, proj, proj, dy, ss, conv_w, alog_row, dt_row, norm_w, *x_ops)


def _my_place():
    return lax.axis_index("x"), lax.axis_index("y"), lax.axis_index("c")


def _slot(p):
    return 4 * p[0] + 2 * p[1] + p[2]


def _peer(me, j):
    flips = ((j >> 2) & 1, (j >> 1) & 1, j & 1)
    return tuple((1 - v) if fl else v for v, fl in zip(me, flips))


_ANY = pl.BlockSpec(memory_space=pl.ANY)


class Xfer(NamedTuple):
    srcs: list
    kinds: list
    lands: list
    land_of: list
    layer: list


def _xfer_plumb(xf, n_in, n_out):
    n_src = len(xf.srcs)
    passed = [ld for ld in xf.lands if not isinstance(ld, jax.ShapeDtypeStruct)]
    aliases, k = {}, 0
    for li, ld in enumerate(xf.lands):
        if not isinstance(ld, jax.ShapeDtypeStruct):
            aliases[n_in + n_src + k] = n_out + li
            k += 1
    operands = list(xf.srcs) + passed
    out_shapes = [jax.ShapeDtypeStruct(ld.shape, ld.dtype) for ld in xf.lands]
    sems = [pltpu.SemaphoreType.DMA((n_src, N_DEV - 1)), pltpu.SemaphoreType.DMA((n_src, N_DEV - 1)),
            pltpu.SemaphoreType.DMA((n_src,))]
    return operands, [_ANY] * len(operands), out_shapes, [_ANY] * len(xf.lands), aliases, sems


def _xfer_copies(xf, src_refs, land_refs, send_sems, recv_sems, local_sems):
    me = _my_place()
    mine = _slot(me)
    local, remote = [], []
    for a, src_ref in enumerate(src_refs):
        land = land_refs[xf.land_of[a]]
        dst = land.at[mine] if xf.layer[a] is None else land.at[mine, xf.layer[a]]
        by_slot = xf.kinds[a] == "slot"
        local.append(pltpu.make_async_copy(src_ref.at[mine] if by_slot else src_ref, dst, local_sems.at[a]))
        for j in range(1, N_DEV):
            peer = _peer(me, j)
            remote.append(pltpu.make_async_remote_copy(
                src_ref=src_ref.at[_slot(peer)] if by_slot else src_ref, dst_ref=dst, send_sem=send_sems.at[a, j - 1],
                recv_sem=recv_sems.at[a, j - 1], device_id=peer, device_id_type=MESH))
    return local, remote


def _xfer_start(xf, src_refs, land_refs, sems):
    local, remote = _xfer_copies(xf, src_refs, land_refs, *sems)
    for cp in local + remote:
        cp.start()


def _xfer_wait(xf, src_refs, land_refs, sems):
    local, remote = _xfer_copies(xf, src_refs, land_refs, *sems)
    for cp in remote:
        cp.wait()
    for cp in local:
        cp.wait()


def exchange(xf, name):
    n_src, n_land = len(xf.srcs), len(xf.lands)
    operands, in_specs, out_shapes, out_specs, aliases, sems = _xfer_plumb(xf, 0, 0)

    def body(*refs):
        src_refs = refs[:n_src]
        land_refs = refs[len(operands):len(operands) + n_land]
        sem_refs = refs[len(operands) + n_land:]
        _xfer_start(xf, src_refs, land_refs, sem_refs)
        _xfer_wait(xf, src_refs, land_refs, sem_refs)

    return pl.pallas_call(body, in_specs=in_specs, out_specs=out_specs, out_shape=out_shapes, scratch_shapes=sems,
                          input_output_aliases=aliases, name=name)(*operands)


def _adamw(w, g, m, v):
    m = ADAM_B1 * m + (1.0 - ADAM_B1) * g
    v = ADAM_B2 * v + (1.0 - ADAM_B2) * (g * g)
    m_hat = m / (1.0 - ADAM_B1 ** ADAM_STEP)
    v_hat = v / (1.0 - ADAM_B2 ** ADAM_STEP)
    delta = -ADAM_LR * (m_hat / (jnp.sqrt(v_hat) + ADAM_EPS) + ADAM_WD * w)
    return delta, m, v


def adam_slots(slots, w, m, v, rows, name):
    _, depth, r, c = slots.shape

    def body(s_ref, w_ref, m_ref, v_ref, g_ref, d_ref, m2_ref, v2_ref):
        g = s_ref[0, 0].astype(f32)
        for k in range(1, N_DEV):
            g = g + s_ref[k, 0].astype(f32)
        d, m2, v2 = _adamw(w_ref[0], g, m_ref[0], v_ref[0])
        g_ref[0] = g
        d_ref[0] = d
        m2_ref[0] = m2
        v2_ref[0] = v2

    blk = pl.BlockSpec((1, rows, c), lambda l, i: (l, i, 0))
    sh = jax.ShapeDtypeStruct((depth, r, c), f32)
    return pl.pallas_call(
        body, grid=(depth, r // rows),
        in_specs=[pl.BlockSpec((N_DEV, 1, rows, c), lambda l, i: (0, l, i, 0)), blk, blk, blk],
        out_specs=[blk] * 4, out_shape=[sh] * 4, name=name, compiler_params=_cp(40))(slots, w, m, v)


def sum_slots(slots, name):
    _, r, c = slots.shape

    def body(s_ref, o_ref):
        g = s_ref[0]
        for k in range(1, N_DEV):
            g = g + s_ref[k]
        o_ref[...] = g

    return pl.pallas_call(body, out_shape=jax.ShapeDtypeStruct((r, c), f32), name=name, compiler_params=_cp(40))(slots)


def adam_flat(g, w, m, v, name):
    def body(g_ref, w_ref, m_ref, v_ref, d_ref, m2_ref, v2_ref):
        d, m2, v2 = _adamw(w_ref[...], g_ref[...], m_ref[...], v_ref[...])
        d_ref[...] = d
        m2_ref[...] = m2
        v2_ref[...] = v2

    sh = jax.ShapeDtypeStruct(g.shape, f32)
    return pl.pallas_call(body, out_shape=[sh] * 3, name=name)(g, w, m, v)


def _pack(arrs):
    flat = jnp.concatenate([a.reshape(-1).astype(f32) for a in arrs])
    n = flat.shape[0]
    rows = -(-n // (8 * 128)) * 8
    flat = jnp.pad(flat, (0, rows * 128 - n))
    return flat.reshape(rows, 128)


def _unpack(packed, shapes):
    flat = packed.reshape(-1)
    out, off = [], 0
    for sh in shapes:
        n = 1
        for dim in sh:
            n *= dim
        out.append(flat[off:off + n].reshape(sh))
        off += n
    return out


def _block_diag(gw):
    eye = jnp.eye(8, dtype=gw.dtype)
    return (gw[:, :, None, :] * eye[:, None, :, None]).reshape(SEG, SEG)


def _diag_blocks(dense):
    eye = jnp.eye(8, dtype=dense.dtype)
    return (dense.reshape(8, 64, 8, 64) * eye[:, None, :, None]).sum(axis=2)


def _lane_row(vals, first_lane):
    return jnp.zeros((1, HD), f32).at[0, first_lane:first_lane + NH].set(vals)


def kernel(x, norm_w, w_in, rg_conv_w, rg_conv_b, rg_gate_w, rg_gate_b, rg_lambda, ml_gate_b, ml_norm_w, gd_conv_w, gd_a_log, gd_dt_bias, gd_norm_w, w_out, final_norm_w, loss_target, m_norm_w, m_w_in, m_rg_conv_w, m_rg_conv_b, m_rg_gate_w, m_rg_gate_b, m_rg_lambda, m_ml_gate_b, m_ml_norm_w, m_gd_conv_w, m_gd_a_log, m_gd_dt_bias, m_gd_norm_w, m_w_out, m_final_norm_w, v_norm_w, v_w_in, v_rg_conv_w, v_rg_conv_b, v_rg_gate_w, v_rg_gate_b, v_rg_lambda, v_ml_gate_b, v_ml_norm_w, v_gd_conv_w, v_gd_a_log, v_gd_dt_bias, v_gd_norm_w, v_w_out, v_final_norm_w):
    s = x.shape[1]
    xs = x.reshape(s, D)
    tgt = loss_target.reshape(s, D)
    me = 4 * lax.axis_index("x") + 2 * lax.axis_index("y") + lax.axis_index("c")

    comm = MeshComm(w_in, w_out, [rg_conv_w, rg_gate_b, gd_conv_w])
    rg_conv_full, rg_gb_full, gd_conv_full = comm.small_weights
    loss_part, dx, d_fw, g_small = local_step(
        xs, tgt, comm, rg_conv_full, rg_gb_full, gd_conv_full, norm_w, rg_conv_b, rg_gate_w, rg_lambda,
        ml_gate_b, ml_norm_w, gd_a_log, gd_dt_bias, gd_norm_w, final_norm_w)
    given_w = dict(norm_w=norm_w, rg_conv_w=rg_conv_w, rg_conv_b=rg_conv_b, rg_gate_w=rg_gate_w, rg_gate_b=rg_gate_b,
                   rg_lambda=rg_lambda, ml_gate_b=ml_gate_b, ml_norm_w=ml_norm_w, gd_conv_w=gd_conv_w, gd_a_log=gd_a_log,
                   gd_dt_bias=gd_dt_bias, gd_norm_w=gd_norm_w, final_norm_w=final_norm_w, w_in=w_in, w_out=w_out)
    given_m = dict(norm_w=m_norm_w, rg_conv_w=m_rg_conv_w, rg_conv_b=m_rg_conv_b, rg_gate_w=m_rg_gate_w, rg_gate_b=m_rg_gate_b,
                   rg_lambda=m_rg_lambda, ml_gate_b=m_ml_gate_b, ml_norm_w=m_ml_norm_w, gd_conv_w=m_gd_conv_w,
                   gd_a_log=m_gd_a_log, gd_dt_bias=m_gd_dt_bias, gd_norm_w=m_gd_norm_w, final_norm_w=m_final_norm_w,
                   w_in=m_w_in, w_out=m_w_out)
    given_v = dict(norm_w=v_norm_w, rg_conv_w=v_rg_conv_w, rg_conv_b=v_rg_conv_b, rg_gate_w=v_rg_gate_w, rg_gate_b=v_rg_gate_b,
                   rg_lambda=v_rg_lambda, ml_gate_b=v_ml_gate_b, ml_norm_w=v_ml_norm_w, gd_conv_w=v_gd_conv_w,
                   gd_a_log=v_gd_a_log, gd_dt_bias=v_gd_dt_bias, gd_norm_w=v_gd_norm_w, final_norm_w=v_final_norm_w,
                   w_in=v_w_in, w_out=v_w_out)
    return finish_step(loss_part, dx, d_fw, g_small, comm, s, me, given_w, given_m, given_v)


def _gathered_pieces():
    per = D_IN // N_DEV
    pieces = []
    for lo, hi in ((0, 3584), (3592, 5640), (3584, 3592), (5640, 5648)):
        col = lo
        while col < hi:
            k = col // per
            end = min(hi, (k + 1) * per)
            pieces.append((k, col - k * per, end - k * per))
            col = end
    return pieces


class MeshComm:
    def __init__(self, w_in, w_out, small_shards):
        per = D_IN // N_DEV
        self.wi_sh = [w_in[l].astype(bf16) for l in range(DEPTH)]
        self.wo_sh = [w_out[l].astype(bf16) for l in range(DEPTH)]
        self.wi_land = jax.ShapeDtypeStruct((N_DEV, D, per), bf16)
        self.wo_land = jax.ShapeDtypeStruct((N_DEV, 3 * SEG // N_DEV, D), bf16)
        packed = _pack(small_shards)
        xf = Xfer([self.wi_sh[0], self.wo_sh[0], packed], ["all"] * 3,
                  [self.wi_land, self.wo_land, jax.ShapeDtypeStruct((N_DEV,) + packed.shape, f32)], [0, 1, 2], [None] * 3)
        wi_g, wo_g, sm_g = exchange(xf, "gather_first")
        self.landed = {0: (wi_g, wo_g)}
        shapes = [a.shape for a in small_shards]
        parts = [_unpack(sm_g[k], shapes) for k in range(N_DEV)]
        self.small_weights = [jnp.concatenate([p[j] for p in parts], axis=-1) for j in range(len(small_shards))]
        self.gwi_land = lax.empty((N_DEV, DEPTH, D, per), bf16)
        self.gwo_land = lax.empty((N_DEV, DEPTH, 3 * SEG // N_DEV, D), bf16)
        self.pending = None
        self.kept_gwo = None

    def weights(self, l):
        wi_g, wo_g = self.landed[l]
        cols = [wi_g[k, :, a:b] for k, a, b in _gathered_pieces()] + [jnp.zeros((D, DP - D_IN), bf16)]
        return jnp.concatenate(cols, axis=1), wo_g.reshape(3 * SEG, D)

    def fwd_xfer(self, l):
        if l + 1 >= DEPTH:
            return None
        return Xfer([self.wi_sh[l + 1], self.wo_sh[l + 1]], ["all", "all"], [self.wi_land, self.wo_land], [0, 1], [None, None])

    def fwd_landed(self, l, landed):
        self.landed[l + 1] = tuple(landed)

    def bwd_xfer(self, l, g_wo):
        gwo = g_wo.reshape(N_DEV, 3 * SEG // N_DEV, D).astype(bf16)
        srcs, land_of, layer = [], [], []
        if self.pending is not None:
            srcs, land_of, layer = list(self.pending), [0, 1], [l + 1, l + 1]
        if l == 0:
            srcs, land_of, layer = srcs + [gwo], land_of + [1], layer + [0]
        self.kept_gwo = gwo
        if not srcs:
            return None
        return Xfer(srcs, ["slot"] * len(srcs), [self.gwi_land, self.gwo_land], land_of, layer)

    def bwd_landed(self, landed):
        self.gwi_land, self.gwo_land = landed

    def grads_ready(self, l, g_wi):
        per = D_IN // N_DEV
        self.pending = (g_wi.reshape(D, N_DEV, per).transpose(1, 0, 2).astype(bf16), self.kept_gwo)

    def finish(self, packed_small):
        xf = Xfer([self.pending[0], packed_small], ["slot", "all"],
                  [self.gwi_land, jax.ShapeDtypeStruct((N_DEV,) + packed_small.shape, f32)], [0, 1], [0, None])
        gwi_land, gsm = exchange(xf, "scatter_last")
        return gwi_land, self.gwo_land, gsm


def local_step(xs, tgt, comm, rg_conv_full, rg_gb_full, gd_conv_full, norm_w, rg_conv_b, rg_gate_w,
               rg_lambda, ml_gate_b, ml_norm_w, gd_a_log, gd_dt_bias, gd_norm_w, final_norm_w):
    acts = []
    for l in range(DEPTH):
        nw = norm_w[l].reshape(1, D)
        w_in_l, w_out_l = comm.weights(l)
        proj = inproj(xs, nw, w_in_l)
        rg_p = (rg_conv_full[l], rg_conv_b[l].reshape(1, SEG), _block_diag(rg_gate_w[l, 0]), _block_diag(rg_gate_w[l, 1]),
                rg_gb_full[l], rg_lambda[l].reshape(1, SEG))
        y_rg, hs = rglru_fwd(proj, *rg_p)
        ml_p = (jnp.zeros((1, HD), f32).at[0, 0:2 * NH].set(ml_gate_b[l].reshape(-1)), ml_norm_w[l].reshape(1, SEG))
        y_ml, cs, ns, ms = mlstm_fwd(proj, *ml_p)
        gd_p = (gd_conv_full[l], _lane_row(gd_a_log[l], 2 * NH), _lane_row(gd_dt_bias[l], 2 * NH), gd_norm_w[l].reshape(1, HD))
        xf = comm.fwd_xfer(l)
        y_gd, ss, *landed = gdn_fwd(proj, *gd_p, xf=xf)
        if xf is not None:
            comm.fwd_landed(l, landed)
        acts.append((xs, nw, proj, w_in_l, w_out_l, rg_p, y_rg, hs, ml_p, y_ml, cs, ns, ms, gd_p, y_gd, ss))
        xs = outproj(xs, y_rg, y_ml, y_gd, w_out_l)

    dx, loss_part, d_fw = head(xs, final_norm_w.reshape(1, D), tgt)

    g_small = {k: [None] * DEPTH for k in ("norm_w", "rg_conv_w", "rg_conv_b", "rg_gate_w", "rg_gate_b", "rg_lambda",
                                           "ml_gate_b", "ml_norm_w", "gd_conv_w", "gd_a_log", "gd_dt_bias", "gd_norm_w")}
    for l in reversed(range(DEPTH)):
        x_l, nw, proj, w_in_l, w_out_l, rg_p, y_rg, hs, ml_p, y_ml, cs, ns, ms, gd_p, y_gd, ss = acts[l]
        dy_rg, dy_ml, dy_gd, g_wo = outproj_bwd(dx, y_rg, y_ml, y_gd, w_out_l)
        d_rg, d_cw, d_cb, d_gr, d_gi, d_gb, d_lam = rglru_bwd(proj, hs, dy_rg, *rg_p)
        d_ml, d_sm_ml, d_bias, d_mnw = mlstm_bwd(proj, dy_ml, cs, ns, ms, *ml_p)
        xf = comm.bwd_xfer(l, g_wo)
        d_gd, d_sm_gd, d_gcw, d_al, d_dt, d_gnw, *landed = gdn_bwd(proj, dy_gd, ss, *gd_p, xf=xf)
        if xf is not None:
            comm.bwd_landed(landed)
        dx, d_nw = inproj_bwd_x(x_l, nw, w_in_l, dx, d_rg, d_ml, d_gd, d_sm_ml, d_sm_gd)
        gw_rg = wgrad(x_l, nw, [d_rg], "wgrad_rg")
        gw_ml = wgrad(x_l, nw, [d_ml], "wgrad_ml")
        gw_gd = wgrad(x_l, nw, [d_gd], "wgrad_gd")
        gw_sm = wgrad(x_l, nw, [d_sm_ml, d_sm_gd], "wgrad_small")
        comm.grads_ready(l, jnp.concatenate([gw_rg, gw_ml, gw_sm[:, 0:2 * NH], gw_gd, gw_sm[:, 2 * NH:4 * NH]], axis=-1))
        g_small["norm_w"][l] = d_nw[0]
        g_small["rg_conv_w"][l] = d_cw
        g_small["rg_conv_b"][l] = d_cb[0]
        g_small["rg_gate_w"][l] = jnp.stack([_diag_blocks(d_gr), _diag_blocks(d_gi)])
        g_small["rg_gate_b"][l] = d_gb
        g_small["rg_lambda"][l] = d_lam[0]
        g_small["ml_gate_b"][l] = d_bias[0, 0:2 * NH].reshape(2, NH)
        g_small["ml_norm_w"][l] = d_mnw[0]
        g_small["gd_conv_w"][l] = d_gcw
        g_small["gd_a_log"][l] = d_al[0, 2 * NH:3 * NH]
        g_small["gd_dt_bias"][l] = d_dt[0, 2 * NH:3 * NH]
        g_small["gd_norm_w"][l] = d_gnw[0]
    return loss_part, dx, d_fw, g_small


def finish_step(loss_part, dx, d_fw, g_small, comm, s, me, given_w, given_m, given_v):
    small_names = ["norm_w", "rg_conv_w", "rg_conv_b", "rg_gate_w", "rg_gate_b", "rg_lambda", "ml_gate_b", "ml_norm_w",
                   "gd_conv_w", "gd_a_log", "gd_dt_bias", "gd_norm_w"]
    small_list = [loss_part[0, 0:1], d_fw[0]] + [jnp.stack(g_small[k]) for k in small_names]
    small_shapes = [a.shape for a in small_list]
    gwi_r, gwo_r, gsm_r = comm.finish(_pack(small_list))

    g_wi, d_wi, m_wi, v_wi = adam_slots(gwi_r, given_w["w_in"], given_m["w_in"], given_v["w_in"], 256, "adam_w_in")
    g_wo, d_wo, m_wo, v_wo = adam_slots(gwo_r, given_w["w_out"], given_m["w_out"], given_v["w_out"], 192, "adam_w_out")
    g_all = _unpack(sum_slots(gsm_r, "sum_small"), small_shapes)
    loss = g_all[0][0]
    grads = {"final_norm_w": g_all[1]}
    for k, g in zip(small_names, g_all[2:]):
        grads[k] = g
    for k, width in (("rg_conv_w", 64), ("rg_gate_b", 64), ("gd_conv_w", 192)):
        grads[k] = lax.dynamic_slice_in_dim(grads[k], me * width, width, axis=2)
    names = small_names + ["final_norm_w"]
    shapes = [given_w[k].shape for k in names]
    d_p, m_p, v_p = adam_flat(_pack([grads[k] for k in names]), _pack([given_w[k] for k in names]),
                              _pack([given_m[k] for k in names]), _pack([given_v[k] for k in names]), "adam_small")
    deltas = dict(zip(names, _unpack(d_p, shapes)))
    new_m = dict(zip(names, _unpack(m_p, shapes)))
    new_v = dict(zip(names, _unpack(v_p, shapes)))
    grads["w_in"], deltas["w_in"], new_m["w_in"], new_v["w_in"] = g_wi, d_wi, m_wi, v_wi
    grads["w_out"], deltas["w_out"], new_m["w_out"], new_v["w_out"] = g_wo, d_wo, m_wo, v_wo

    order = ["norm_w", "w_in", "rg_conv_w", "rg_conv_b", "rg_gate_w", "rg_gate_b", "rg_lambda", "ml_gate_b", "ml_norm_w",
             "gd_conv_w", "gd_a_log", "gd_dt_bias", "gd_norm_w", "w_out", "final_norm_w"]
    return (loss, dx.reshape(1, s, D), *[grads[k] for k in order], *[deltas[k] for k in order],
            *[new_m[k] for k in order], *[new_v[k] for k in order])
```

```python
from typing import NamedTuple

import jax
import jax.numpy as jnp
from jax import lax
from jax.experimental import pallas as pl
from jax.experimental.pallas import tpu as pltpu

f32 = jnp.float32
bf16 = jnp.bfloat16
MESH = pl.DeviceIdType.MESH

N_DEV = 8
D = 1024
DEPTH = 4
EPS = 1e-6
SEG = 512
HD = 128
NH = 4
LC = 64
D_IN = 5648
DP = 5760
SMALL0 = 5632
RG_TILE = 256
RG_C = 8.0

ADAM_LR = 0.001
ADAM_B1 = 0.9
ADAM_B2 = 0.999
ADAM_EPS = 1e-08
ADAM_WD = 0.01
ADAM_STEP = 10


def _cp(vmem_mb):
    return pltpu.CompilerParams(vmem_limit_bytes=vmem_mb * 2 ** 20)


def _dot(a, b, ca, cb):
    return lax.dot_general(a.astype(bf16), b.astype(bf16), (((ca,), (cb,)), ((), ())), preferred_element_type=f32)


@jax.custom_vjp
def mm_nn(a, b):
    return _dot(a, b, 1, 0)


@jax.custom_vjp
def mm_nt(a, b):
    return _dot(a, b, 1, 1)


@jax.custom_vjp
def mm_tn(a, b):
    return _dot(a, b, 0, 0)


mm_nn.defvjp(lambda a, b: (mm_nn(a, b), (a, b)), lambda r, g: (mm_nt(g, r[1]), mm_tn(r[0], g)))
mm_nt.defvjp(lambda a, b: (mm_nt(a, b), (a, b)), lambda r, g: (mm_nn(g, r[1]), mm_tn(g, r[0])))
mm_tn.defvjp(lambda a, b: (mm_tn(a, b), (a, b)), lambda r, g: (mm_nt(r[1], g), mm_nn(r[0], g)))


def _split(x):
    hi = x.astype(bf16)
    return hi, (x - hi.astype(f32)).astype(bf16)


def dot3(a, b):
    ah, al = _split(a)
    bh, bl = _split(b)
    return jnp.dot(jnp.concatenate([ah, al, ah], axis=1), jnp.concatenate([bh, bh, bl], axis=0), preferred_element_type=f32)


def _tri_sum(x, reverse):
    n = x.shape[0]
    r = lax.broadcasted_iota(jnp.int32, (n, 3 * n), 0)
    c = lax.broadcasted_iota(jnp.int32, (n, 3 * n), 1) & (n - 1)
    ones = ((c >= r) if reverse else (c <= r)).astype(bf16)
    hi = x.astype(bf16)
    rest = x - hi.astype(f32)
    mid = rest.astype(bf16)
    lo = (rest - mid.astype(f32)).astype(bf16)
    return jnp.dot(ones, jnp.concatenate([hi, mid, lo], axis=0), preferred_element_type=f32)


@jax.custom_vjp
def cumsum_rows(x):
    return _tri_sum(x, False)


@jax.custom_vjp
def rev_cumsum_rows(x):
    return _tri_sum(x, True)


cumsum_rows.defvjp(lambda x: (cumsum_rows(x), None), lambda _, g: (rev_cumsum_rows(g),))
rev_cumsum_rows.defvjp(lambda x: (rev_cumsum_rows(x), None), lambda _, g: (cumsum_rows(g),))


def _tri(n, strict=False):
    r = lax.broadcasted_iota(jnp.int32, (n, n), 0)
    c = lax.broadcasted_iota(jnp.int32, (n, n), 1)
    return (r > c) if strict else (r >= c)


def _lane_col(v, j):
    lane = lax.broadcasted_iota(jnp.int32, v.shape, 1)
    return jnp.sum(jnp.where(lane == j, v, 0.0), axis=1, keepdims=True)


def _rows_from(pieces, rows, width):
    ridx = lax.broadcasted_iota(jnp.int32, (rows, width), 0)
    out = jnp.zeros((rows, width), f32)
    for h, p in enumerate(pieces):
        out = out + jnp.where(ridx == h, jnp.broadcast_to(p, (rows, width)), 0.0)
    return out


def causal_conv(halo8, x, w4):
    t = x.shape[0]
    xe = jnp.concatenate([halo8, x], axis=0)
    y = xe[5:5 + t] * w4[0:1]
    for k in range(1, 4):
        y = y + xe[5 + k:5 + k + t] * w4[k:k + 1]
    return y


def ml_chunk(q, k, v, o_pre, z, small, bias_row, norm_w, C, n, m):
    lane = lax.broadcasted_iota(jnp.int32, small.shape, 1)
    pre = small + bias_row
    lg = jnp.where(lane < NH, pre, jnp.where(lane < 2 * NH, jax.nn.log_sigmoid(pre), 0.0))
    bc = cumsum_rows(lg)
    lg_t = lg.T
    bc_t = bc.T
    causal = _tri(LC)
    ys, c_new, n_new, m_new_rows = [], [], [], []
    for h in range(NH):
        sl = slice(h * HD, (h + 1) * HD)
        qh = q[:, sl] * (HD ** -0.5)
        kh = k[:, sl]
        vh = v[:, sl]
        li_col = _lane_col(lg, h)
        b_col = _lane_col(bc, NH + h)
        li_row = lg_t[h:h + 1, :]
        b_row = bc_t[NH + h:NH + h + 1, :]
        m_h = m[h:h + 1, 0:1]
        c_h = C[sl, :]
        n_h = n[h:h + 1, :]
        dmat = jnp.where(causal, b_col - b_row + li_row, -jnp.inf)
        m_inter = b_col + m_h
        m_t = jnp.maximum(m_inter, jnp.max(dmat, axis=-1, keepdims=True))
        p = jnp.exp(dmat - m_t)
        s = mm_nt(qh, kh) * p
        sc = jnp.exp(m_inter - m_t)
        num = mm_nn(s, vh) + sc * mm_nn(qh, c_h)
        den = jnp.sum(s, axis=-1, keepdims=True) + sc * jnp.sum(qh * n_h, axis=-1, keepdims=True)
        hh = num / jnp.maximum(jnp.abs(den), jnp.exp(-m_t))
        g = b_col[LC - 1:LC, :]
        ws = g - b_col + li_col
        m_new = jnp.maximum(g + m_h, jnp.max(ws, axis=0, keepdims=True))
        dec = jnp.exp(g + m_h - m_new)
        wk = jnp.exp(ws - m_new)
        kw = wk * kh
        c_new.append(dec * c_h + mm_tn(kw, vh))
        n_new.append(dec * n_h + jnp.sum(kw, axis=0, keepdims=True))
        m_new_rows.append(jnp.broadcast_to(m_new, (1, HD)))
        hn = hh * lax.rsqrt(jnp.mean(hh * hh, axis=-1, keepdims=True) + EPS) * norm_w[:, sl]
        ys.append(hn * jax.nn.sigmoid(o_pre[:, sl]) * jax.nn.silu(z[:, sl]))
    y = jnp.concatenate(ys, axis=1)
    return y, jnp.concatenate(c_new, axis=0), _rows_from(n_new, 8, HD), _rows_from(m_new_rows, 8, HD)


@jax.custom_vjp
def _unit_lower_inverse(mat):
    eye = (lax.broadcasted_iota(jnp.int32, (LC, LC), 0) == lax.broadcasted_iota(jnp.int32, (LC, LC), 1)).astype(f32)
    p = -mat
    t = eye + p
    for _ in range(5):
        p = dot3(p, p)
        t = t + dot3(t, p)
    return t


def _unit_lower_inverse_fwd(mat):
    t = _unit_lower_inverse(mat)
    return t, t


def _unit_lower_inverse_bwd(t, g):
    tt = t.T
    return (-dot3(dot3(tt, g), tt),)


_unit_lower_inverse.defvjp(_unit_lower_inverse_fwd, _unit_lower_inverse_bwd)


def gd_chunk(qh8, q, kh8, k, vh8, v, z, small, conv_w, alog_row, dt_row, norm_w, st):
    lane = lax.broadcasted_iota(jnp.int32, small.shape, 1)
    is_a = (lane >= 2 * NH) & (lane < 3 * NH)
    g_all = jnp.where(is_a, -jnp.exp(alog_row) * jax.nn.softplus(small + dt_row), 0.0)
    beta_all = jax.nn.sigmoid(small)
    gc_all = cumsum_rows(g_all)
    gc_t = gc_all.T
    qc = jax.nn.silu(causal_conv(qh8, q, conv_w[:, 0:SEG]))
    kc = jax.nn.silu(causal_conv(kh8, k, conv_w[:, SEG:2 * SEG]))
    vc = jax.nn.silu(causal_conv(vh8, v, conv_w[:, 2 * SEG:3 * SEG]))
    incl = _tri(LC)
    strict = _tri(LC, strict=True)
    ys, st_new = [], []
    for h in range(NH):
        sl = slice(h * HD, (h + 1) * HD)
        qh = qc[:, sl]
        kh = kc[:, sl]
        vh = vc[:, sl]
        qh = qh * lax.rsqrt(jnp.sum(qh * qh, axis=-1, keepdims=True) + EPS) * (HD ** -0.5)
        kh = kh * lax.rsqrt(jnp.sum(kh * kh, axis=-1, keepdims=True) + EPS)
        beta = _lane_col(beta_all, 3 * NH + h)
        gc = _lane_col(gc_all, 2 * NH + h)
        gc_row = gc_t[2 * NH + h:2 * NH + h + 1, :]
        s_h = st[sl, :]
        gam = jnp.exp(jnp.where(incl, gc - gc_row, -jnp.inf))
        kb = kh * beta
        m_strict = jnp.where(strict, mm_nt(kb, kh) * gam, 0.0)
        t_inv = _unit_lower_inverse(m_strict)
        u = mm_nn(t_inv, vh * beta)
        w = mm_nn(t_inv, kb * jnp.exp(gc))
        aqk = mm_nt(qh, kh) * gam
        q_dec = qh * jnp.exp(gc)
        g_last = gc[LC - 1:LC, :]
        k_dec = kh * jnp.exp(g_last - gc)
        v_new = u - mm_nn(w, s_h)
        o = mm_nn(q_dec, s_h) + mm_nn(aqk, v_new)
        st_new.append(s_h * jnp.exp(g_last) + mm_tn(k_dec, v_new))
        on = o * lax.rsqrt(jnp.mean(o * o, axis=-1, keepdims=True) + EPS) * norm_w
        ys.append(on * jax.nn.silu(z[:, sl]))
    return jnp.concatenate(ys, axis=1), jnp.concatenate(st_new, axis=0)


def rg_pre(xh8, x, conv_w, conv_b, gw_r, gw_i, gate_b, lam):
    xc = causal_conv(xh8, x, conv_w) + conv_b
    r = jax.nn.sigmoid(mm_nn(xc, gw_r) + gate_b[0:1])
    i = jax.nn.sigmoid(mm_nn(xc, gw_i) + gate_b[1:2])
    log_a = -RG_C * r * jax.nn.softplus(-lam)
    a = jnp.exp(log_a)
    th = jnp.tanh(log_a)
    one_minus_a2 = -2.0 * th / (1.0 - th)
    b = jnp.sqrt(one_minus_a2) * (i * xc)
    return a, b


def _scan_rows(a_ref, b_ref, out_ref, h0, n_rows, reverse):
    n_groups = n_rows // 8
    width = a_ref.shape[1]
    row = lax.broadcasted_iota(jnp.int32, (8, width), 0)

    def body(j, h):
        g = (n_groups - 1 - j) if reverse else j
        r0 = pl.multiple_of(g * 8, 8)
        av = a_ref[pl.ds(r0, 8), :]
        bv = b_ref[pl.ds(r0, 8), :]
        for d in (1, 2, 4):
            sh = (8 - d) if reverse else d
            a_s = pltpu.roll(av, sh, 0)
            b_s = pltpu.roll(bv, sh, 0)
            valid = (row < 8 - d) if reverse else (row >= d)
            bv = jnp.where(valid, av * b_s + bv, bv)
            av = jnp.where(valid, av * a_s, av)
        hv = av * h + bv
        out_ref[pl.ds(r0, 8), :] = hv
        return hv[0:1, :] if reverse else hv[7:8, :]

    return lax.fori_loop(0, n_groups, body, h0)


def _norm_rows(xv):
    r = lax.rsqrt(jnp.mean(xv * xv, axis=-1, keepdims=True) + EPS)
    return xv * r, r


def inproj(x, nw, w):
    s, ts = x.shape[0], 256

    def body(x_ref, nw_ref, w_ref, o_ref):
        xn, _ = _norm_rows(x_ref[...])
        hn = xn * nw_ref[...]
        o_ref[...] = jnp.dot(hn.astype(bf16), w_ref[...], preferred_element_type=f32)

    return pl.pallas_call(
        body, grid=(s // ts,),
        in_specs=[pl.BlockSpec((ts, D), lambda i: (i, 0)), pl.BlockSpec((1, D), lambda i: (0, 0)),
                  pl.BlockSpec((D, DP), lambda i: (0, 0))],
        out_specs=pl.BlockSpec((ts, DP), lambda i: (i, 0)),
        out_shape=jax.ShapeDtypeStruct((s, DP), f32), name="inproj", compiler_params=_cp(56))(x, nw, w)


def outproj(x, yr, ym, yg, wo):
    s, ts = x.shape[0], 256

    def body(x_ref, yr_ref, ym_ref, yg_ref, wo_ref, o_ref):
        acc = x_ref[...]
        for j, y_ref in enumerate((yr_ref, ym_ref, yg_ref)):
            acc = acc + jnp.dot(y_ref[...].astype(bf16), wo_ref[j * SEG:(j + 1) * SEG, :], preferred_element_type=f32)
        o_ref[...] = acc

    yspec = pl.BlockSpec((ts, SEG), lambda i: (i, 0))
    return pl.pallas_call(
        body, grid=(s // ts,),
        in_specs=[pl.BlockSpec((ts, D), lambda i: (i, 0)), yspec, yspec, yspec,
                  pl.BlockSpec((3 * SEG, D), lambda i: (0, 0))],
        out_specs=pl.BlockSpec((ts, D), lambda i: (i, 0)),
        out_shape=jax.ShapeDtypeStruct((s, D), f32), name="outproj", compiler_params=_cp(40))(x, yr, ym, yg, wo)


def head(x, fw, tgt):
    s, ts = x.shape[0], 256

    def body(x_ref, fw_ref, t_ref, dx_ref, loss_ref, dfw_ref):
        @pl.when(pl.program_id(0) == 0)
        def _():
            loss_ref[...] = jnp.zeros_like(loss_ref)
            dfw_ref[...] = jnp.zeros_like(dfw_ref)

        xn, r = _norm_rows(x_ref[...])
        fw_v = fw_ref[...]
        err = xn * fw_v - t_ref[...]
        loss_ref[...] += 0.5 * jnp.sum(jnp.mean(err * err, axis=-1, keepdims=True))
        dy = err * (1.0 / D)
        dfw_ref[0:1, :] += jnp.sum(dy * xn, axis=0, keepdims=True)
        dxn = dy * fw_v
        dx_ref[...] = r * (dxn - xn * jnp.mean(dxn * xn, axis=-1, keepdims=True))

    tile = pl.BlockSpec((ts, D), lambda i: (i, 0))
    return pl.pallas_call(
        body, grid=(s // ts,),
        in_specs=[tile, pl.BlockSpec((1, D), lambda i: (0, 0)), tile],
        out_specs=[tile, pl.BlockSpec((8, 128), lambda i: (0, 0)), pl.BlockSpec((8, D), lambda i: (0, 0))],
        out_shape=[jax.ShapeDtypeStruct((s, D), f32), jax.ShapeDtypeStruct((8, 128), f32),
                   jax.ShapeDtypeStruct((8, D), f32)], name="head")(x, fw, tgt)


def outproj_bwd(dx, yr, ym, yg, wo):
    s, ts = dx.shape[0], 256

    def body(dx_ref, yr_ref, ym_ref, yg_ref, wo_ref, dyr_ref, dym_ref, dyg_ref, gwo_ref):
        @pl.when(pl.program_id(0) == 0)
        def _():
            gwo_ref[...] = jnp.zeros_like(gwo_ref)

        dxb = dx_ref[...].astype(bf16)
        for j, (y_ref, dy_ref) in enumerate(((yr_ref, dyr_ref), (ym_ref, dym_ref), (yg_ref, dyg_ref))):
            rows = slice(j * SEG, (j + 1) * SEG)
            dy_ref[...] = lax.dot_general(dxb, wo_ref[rows, :], (((1,), (1,)), ((), ())), preferred_element_type=f32)
            gwo_ref[rows, :] += lax.dot_general(y_ref[...].astype(bf16), dxb, (((0,), (0,)), ((), ())),
                                                preferred_element_type=f32)

    yspec = pl.BlockSpec((ts, SEG), lambda i: (i, 0))
    wspec = pl.BlockSpec((3 * SEG, D), lambda i: (0, 0))
    ysh = jax.ShapeDtypeStruct((s, SEG), f32)
    return pl.pallas_call(
        body, grid=(s // ts,),
        in_specs=[pl.BlockSpec((ts, D), lambda i: (i, 0)), yspec, yspec, yspec, wspec],
        out_specs=[yspec, yspec, yspec, wspec],
        out_shape=[ysh, ysh, ysh, jax.ShapeDtypeStruct((3 * SEG, D), f32)],
        name="outproj_bwd", compiler_params=_cp(48))(dx, yr, ym, yg, wo)


def inproj_bwd_x(x, nw, w, dxo, d_rg, d_ml, d_gd, d_sa, d_sb):
    s, ts = x.shape[0], 256
    widths = (d_rg.shape[1], d_ml.shape[1], d_gd.shape[1], HD)

    def body(x_ref, nw_ref, w_ref, dxo_ref, rg_ref, ml_ref, gd_ref, sa_ref, sb_ref, dx_ref, dnw_ref):
        @pl.when(pl.program_id(0) == 0)
        def _():
            dnw_ref[...] = jnp.zeros_like(dnw_ref)

        xn, r = _norm_rows(x_ref[...])
        pieces = (rg_ref[...], ml_ref[...], gd_ref[...], sa_ref[...] + sb_ref[...])
        dhn = jnp.zeros((ts, D), f32)
        c0 = 0
        for piece, wd in zip(pieces, widths):
            dhn = dhn + lax.dot_general(piece.astype(bf16), w_ref[:, c0:c0 + wd], (((1,), (1,)), ((), ())),
                                        preferred_element_type=f32)
            c0 += wd
        dnw_ref[0:1, :] += jnp.sum(dhn * xn, axis=0, keepdims=True)
        dxn = dhn * nw_ref[...]
        dx_ref[...] = dxo_ref[...] + r * (dxn - xn * jnp.mean(dxn * xn, axis=-1, keepdims=True))

    tile = pl.BlockSpec((ts, D), lambda i: (i, 0))
    return pl.pallas_call(
        body, grid=(s // ts,),
        in_specs=[tile, pl.BlockSpec((1, D), lambda i: (0, 0)), pl.BlockSpec((D, DP), lambda i: (0, 0)), tile]
        + [pl.BlockSpec((ts, wd), lambda i: (i, 0)) for wd in widths] + [pl.BlockSpec((ts, HD), lambda i: (i, 0))],
        out_specs=[tile, pl.BlockSpec((8, D), lambda i: (0, 0))],
        out_shape=[jax.ShapeDtypeStruct((s, D), f32), jax.ShapeDtypeStruct((8, D), f32)],
        name="inproj_bwd_x", compiler_params=_cp(56))(x, nw, w, dxo, d_rg, d_ml, d_gd, d_sa, d_sb)


def wgrad(x, nw, dps, name):
    s, ts = x.shape[0], 512
    c = dps[0].shape[1]
    ct = min(c, SEG)
    n_dp = len(dps)

    def body(*refs):
        x_ref, nw_ref = refs[0], refs[1]
        dp_refs = refs[2:2 + n_dp]
        o_ref = refs[2 + n_dp]

        @pl.when(pl.program_id(1) == 0)
        def _():
            o_ref[...] = jnp.zeros_like(o_ref)

        xn, _ = _norm_rows(x_ref[...])
        hn = (xn * nw_ref[...]).astype(bf16)
        dp = dp_refs[0][...]
        for extra in dp_refs[1:]:
            dp = dp + extra[...]
        o_ref[...] += lax.dot_general(hn, dp.astype(bf16), (((0,), (0,)), ((), ())), preferred_element_type=f32)

    return pl.pallas_call(
        body, grid=(c // ct, s // ts),
        in_specs=[pl.BlockSpec((ts, D), lambda j, i: (i, 0)), pl.BlockSpec((1, D), lambda j, i: (0, 0))]
        + [pl.BlockSpec((ts, ct), lambda j, i: (i, j)) for _ in dps],
        out_specs=pl.BlockSpec((D, ct), lambda j, i: (0, j)),
        out_shape=jax.ShapeDtypeStruct((D, c), f32), name=name, compiler_params=_cp(40))(x, nw, *dps)


def _seg_spec(rows, seg, n_tiles=None):
    if n_tiles is None:
        return pl.BlockSpec((rows, SEG), lambda i: (i, seg))
    return pl.BlockSpec((rows, SEG), lambda i: (n_tiles - 1 - i, seg))


def _halo_spec(rows, seg, n_tiles=None):
    per = rows // 8
    if n_tiles is None:
        return pl.BlockSpec((8, SEG), lambda i: (jnp.maximum(i * per - 1, 0), seg))
    return pl.BlockSpec((8, SEG), lambda i: (jnp.maximum((n_tiles - 1 - i) * per - 1, 0), seg))


def _const_spec(shape):
    return pl.BlockSpec(shape, lambda i: tuple(0 for _ in shape))


def rglru_fwd(proj, conv_w, conv_b, gw_r, gw_i, gate_b, lam):
    s = proj.shape[0]
    tr = RG_TILE

    def body(xh_ref, x_ref, z_ref, cw_ref, cb_ref, gr_ref, gi_ref, gb_ref, lam_ref, y_ref, h_ref, a_s, b_s, hc):
        first = pl.program_id(0) == 0

        @pl.when(first)
        def _():
            hc[...] = jnp.zeros_like(hc)

        xh = jnp.where(first, 0.0, xh_ref[...])
        a, b = rg_pre(xh, x_ref[...], cw_ref[...], cb_ref[...], gr_ref[...], gi_ref[...], gb_ref[...], lam_ref[...])
        a_s[...] = a
        b_s[...] = b
        hc[0:1, :] = _scan_rows(a_s, b_s, h_ref, hc[0:1, :], tr, False)
        y_ref[...] = h_ref[...] * jax.nn.silu(z_ref[...])

    out = pl.BlockSpec((tr, SEG), lambda i: (i, 0))
    return pl.pallas_call(
        body, grid=(s // tr,),
        in_specs=[_halo_spec(tr, 0), _seg_spec(tr, 0), _seg_spec(tr, 1), _const_spec((4, SEG)), _const_spec((1, SEG)),
                  _const_spec((SEG, SEG)), _const_spec((SEG, SEG)), _const_spec((2, SEG)), _const_spec((1, SEG))],
        out_specs=[out, out],
        out_shape=[jax.ShapeDtypeStruct((s, SEG), f32), jax.ShapeDtypeStruct((s, SEG), f32)],
        scratch_shapes=[pltpu.VMEM((tr, SEG), f32), pltpu.VMEM((tr, SEG), f32), pltpu.VMEM((8, SEG), f32)],
        name="rglru_fwd", compiler_params=_cp(40))(proj, proj, proj, conv_w, conv_b, gw_r, gw_i, gate_b, lam)


def rglru_bwd(proj, hs, dy, conv_w, conv_b, gw_r, gw_i, gate_b, lam):
    s = proj.shape[0]
    tr = RG_TILE
    nt = s // tr

    def body(xh_ref, x_ref, z_ref, hh_ref, h_ref, dy_ref, cw_ref, cb_ref, gr_ref, gi_ref, gb_ref, lam_ref,
             dp_ref, dcw_ref, dcb_ref, dgr_ref, dgi_ref, dgb_ref, dlam_ref,
             an_s, g_s, dh_s, a_first, dh_first, dhalo):
        i = pl.program_id(0)
        first_tile = i == nt - 1

        @pl.when(i == 0)
        def _():
            for ref in (dcw_ref, dcb_ref, dgr_ref, dgi_ref, dgb_ref, dlam_ref, a_first, dh_first, dhalo):
                ref[...] = jnp.zeros_like(ref)

        xh = jnp.where(first_tile, 0.0, xh_ref[...])
        params = (cw_ref[...], cb_ref[...], gr_ref[...], gi_ref[...], gb_ref[...], lam_ref[...])
        (a, _), vjp = jax.vjp(rg_pre, xh, x_ref[...], *params)
        zv = z_ref[...]
        hv = h_ref[...]
        dyv = dy_ref[...]
        sig = jax.nn.sigmoid(zv)
        g_s[...] = dyv * (zv * sig)
        dp_ref[:, SEG:2 * SEG] = dyv * hv * (sig * (1.0 + zv * (1.0 - sig)))
        ridx = lax.broadcasted_iota(jnp.int32, (tr, SEG), 0)
        an_s[...] = jnp.where(ridx == tr - 1, jnp.broadcast_to(a_first[0:1, :], (tr, SEG)), pltpu.roll(a, tr - 1, 0))
        _scan_rows(an_s, g_s, dh_s, dh_first[0:1, :], tr, True)
        dh = dh_s[...]
        h_prev_last = jnp.where(first_tile, 0.0, hh_ref[...])[7:8, :]
        h_prev = pltpu.roll(hv, 1, 0)
        h_prev = jnp.where(ridx == 0, jnp.broadcast_to(h_prev_last, (tr, SEG)), h_prev)
        dxh, dx, dcw, dcb, dgr, dgi, dgb, dlam = vjp((dh * h_prev, dh))
        dp_ref[:, 0:SEG] = dx
        dp_ref[tr - 8:tr, 0:SEG] += dhalo[...]
        dhalo[...] = dxh
        a_first[0:1, :] = a[0:1, :]
        dh_first[0:1, :] = dh[0:1, :]
        dcw_ref[...] += dcw
        dcb_ref[...] += dcb
        dgr_ref[...] += dgr
        dgi_ref[...] += dgi
        dgb_ref[...] += dgb
        dlam_ref[...] += dlam

    pspecs = [_const_spec((4, SEG)), _const_spec((1, SEG)), _const_spec((SEG, SEG)), _const_spec((SEG, SEG)),
              _const_spec((2, SEG)), _const_spec((1, SEG))]
    pshapes = [jax.ShapeDtypeStruct(sh, f32) for sh in ((4, SEG), (1, SEG), (SEG, SEG), (SEG, SEG), (2, SEG), (1, SEG))]
    tile = pl.BlockSpec((tr, SEG), lambda i: (nt - 1 - i, 0))
    return pl.pallas_call(
        body, grid=(nt,),
        in_specs=[_halo_spec(tr, 0, nt), _seg_spec(tr, 0, nt), _seg_spec(tr, 1, nt),
                  pl.BlockSpec((8, SEG), lambda i: (jnp.maximum((nt - 1 - i) * (tr // 8) - 1, 0), 0)), tile, tile] + pspecs,
        out_specs=[pl.BlockSpec((tr, 2 * SEG), lambda i: (nt - 1 - i, 0))] + pspecs,
        out_shape=[jax.ShapeDtypeStruct((s, 2 * SEG), f32)] + pshapes,
        scratch_shapes=[pltpu.VMEM((tr, SEG), f32), pltpu.VMEM((tr, SEG), f32), pltpu.VMEM((tr, SEG), f32),
                        pltpu.VMEM((8, SEG), f32), pltpu.VMEM((8, SEG), f32), pltpu.VMEM((8, SEG), f32)],
        name="rglru_bwd", compiler_params=_cp(48))(proj, proj, proj, hs, hs, dy, conv_w, conv_b, gw_r, gw_i, gate_b, lam)


ML_SEGS = (2, 3, 4, 5, 6)
SMALL_BLK = SMALL0 // HD


def mlstm_fwd(proj, bias_row, norm_w):
    s = proj.shape[0]
    nc = s // LC

    def body(q_ref, k_ref, v_ref, o_ref, z_ref, sm_ref, b_ref, nw_ref, y_ref, cs_ref, ns_ref, ms_ref, c_s, n_s, m_s):
        @pl.when(pl.program_id(0) == 0)
        def _():
            c_s[...] = jnp.zeros_like(c_s)
            n_s[...] = jnp.zeros_like(n_s)
            m_s[...] = jnp.zeros_like(m_s)

        cs_ref[0] = c_s[...]
        ns_ref[0] = n_s[...]
        ms_ref[0] = m_s[...]
        y, c2, n2, m2 = ml_chunk(q_ref[...], k_ref[...], v_ref[...], o_ref[...], z_ref[...], sm_ref[...],
                                 b_ref[...], nw_ref[...], c_s[...], n_s[...], m_s[...])
        y_ref[...] = y
        c_s[...] = c2
        n_s[...] = n2
        m_s[...] = m2

    return pl.pallas_call(
        body, grid=(nc,),
        in_specs=[_seg_spec(LC, sg) for sg in ML_SEGS]
        + [pl.BlockSpec((LC, HD), lambda i: (i, SMALL_BLK)), _const_spec((1, HD)), _const_spec((1, SEG))],
        out_specs=[pl.BlockSpec((LC, SEG), lambda i: (i, 0)), pl.BlockSpec((1, SEG, HD), lambda i: (i, 0, 0)),
                   pl.BlockSpec((1, 8, HD), lambda i: (i, 0, 0)), pl.BlockSpec((1, 8, HD), lambda i: (i, 0, 0))],
        out_shape=[jax.ShapeDtypeStruct((s, SEG), f32), jax.ShapeDtypeStruct((nc, SEG, HD), f32),
                   jax.ShapeDtypeStruct((nc, 8, HD), f32), jax.ShapeDtypeStruct((nc, 8, HD), f32)],
        scratch_shapes=[pltpu.VMEM((SEG, HD), f32), pltpu.VMEM((8, HD), f32), pltpu.VMEM((8, HD), f32)],
        name="mlstm_fwd")(proj, proj, proj, proj, proj, proj, bias_row, norm_w)


def mlstm_bwd(proj, dy, cs, ns, ms, bias_row, norm_w):
    s = proj.shape[0]
    nc = s // LC

    def body(q_ref, k_ref, v_ref, o_ref, z_ref, sm_ref, dy_ref, cs_ref, ns_ref, ms_ref, b_ref, nw_ref,
             dp_ref, dsm_ref, db_ref, dnw_ref, dc_s, dn_s, dm_s):
        @pl.when(pl.program_id(0) == 0)
        def _():
            for ref in (db_ref, dnw_ref, dc_s, dn_s, dm_s):
                ref[...] = jnp.zeros_like(ref)

        _, vjp = jax.vjp(ml_chunk, q_ref[...], k_ref[...], v_ref[...], o_ref[...], z_ref[...], sm_ref[...],
                         b_ref[...], nw_ref[...], cs_ref[0], ns_ref[0], ms_ref[0])
        dq, dk, dv, do, dz, dsm, db, dnw, dc, dn, dm = vjp((dy_ref[...], dc_s[...], dn_s[...], dm_s[...]))
        for j, val in enumerate((dq, dk, dv, do, dz)):
            dp_ref[:, j * SEG:(j + 1) * SEG] = val
        dsm_ref[...] = dsm
        db_ref[0:1, :] += db
        dnw_ref[0:1, :] += dnw
        dc_s[...] = dc
        dn_s[...] = dn
        dm_s[...] = dm

    rev3 = lambda i: (nc - 1 - i, 0, 0)
    return pl.pallas_call(
        body, grid=(nc,),
        in_specs=[_seg_spec(LC, sg, nc) for sg in ML_SEGS]
        + [pl.BlockSpec((LC, HD), lambda i: (nc - 1 - i, SMALL_BLK)), pl.BlockSpec((LC, SEG), lambda i: (nc - 1 - i, 0)),
           pl.BlockSpec((1, SEG, HD), rev3), pl.BlockSpec((1, 8, HD), rev3), pl.BlockSpec((1, 8, HD), rev3),
           _const_spec((1, HD)), _const_spec((1, SEG))],
        out_specs=[pl.BlockSpec((LC, 5 * SEG), lambda i: (nc - 1 - i, 0)), pl.BlockSpec((LC, HD), lambda i: (nc - 1 - i, 0)),
                   _const_spec((8, HD)), _const_spec((8, SEG))],
        out_shape=[jax.ShapeDtypeStruct((s, 5 * SEG), f32), jax.ShapeDtypeStruct((s, HD), f32),
                   jax.ShapeDtypeStruct((8, HD), f32), jax.ShapeDtypeStruct((8, SEG), f32)],
        scratch_shapes=[pltpu.VMEM((SEG, HD), f32), pltpu.VMEM((8, HD), f32), pltpu.VMEM((8, HD), f32)],
        name="mlstm_bwd", compiler_params=_cp(48))(proj, proj, proj, proj, proj, proj, dy, cs, ns, ms, bias_row, norm_w)


GD_SEGS = (7, 8, 9)


def _carried(xf, n_in, n_out):
    if xf is None:
        return [], [], [], [], {}, [], 0, 0
    operands, in_specs, out_shapes, out_specs, aliases, sems = _xfer_plumb(xf, n_in, n_out)
    return operands, in_specs, out_shapes, out_specs, aliases, sems, len(operands), len(xf.lands)


def gdn_fwd(proj, conv_w, alog_row, dt_row, norm_w, xf=None):
    s = proj.shape[0]
    nc = s // LC
    n_in, n_out = 12, 2
    x_ops, x_in_specs, x_out_shapes, x_out_specs, aliases, x_sems, n_xin, n_land = _carried(xf, n_in, n_out)

    def body(*refs):
        qh_ref, q_ref, kh_ref, k_ref, vh_ref, v_ref, z_ref, sm_ref, cw_ref, al_ref, dt_ref, nw_ref = refs[:n_in]
        x_refs = refs[n_in:n_in + n_xin]
        y_ref, ss_ref = refs[n_in + n_xin:n_in + n_xin + n_out]
        land_refs = refs[n_in + n_xin + n_out:n_in + n_xin + n_out + n_land]
        st_s = refs[n_in + n_xin + n_out + n_land]
        sem_refs = refs[n_in + n_xin + n_out + n_land + 1:]
        i = pl.program_id(0)
        first = i == 0

        @pl.when(first)
        def _():
            st_s[...] = jnp.zeros_like(st_s)
            if xf is not None:
                _xfer_start(xf, x_refs[:len(xf.srcs)], land_refs, sem_refs)

        ss_ref[0] = st_s[...]
        halo = [jnp.where(first, 0.0, r[...]) for r in (qh_ref, kh_ref, vh_ref)]
        y, st2 = gd_chunk(halo[0], q_ref[...], halo[1], k_ref[...], halo[2], v_ref[...], z_ref[...], sm_ref[...],
                          cw_ref[...], al_ref[...], dt_ref[...], nw_ref[...], st_s[...])
        y_ref[...] = y
        st_s[...] = st2

        if xf is not None:
            @pl.when(i == nc - 1)
            def _():
                _xfer_wait(xf, x_refs[:len(xf.srcs)], land_refs, sem_refs)

    qkv_specs = []
    for sg in GD_SEGS:
        qkv_specs += [_halo_spec(LC, sg), _seg_spec(LC, sg)]
    return pl.pallas_call(
        body, grid=(nc,),
        in_specs=qkv_specs + [_seg_spec(LC, 10), pl.BlockSpec((LC, HD), lambda i: (i, SMALL_BLK)),
                              _const_spec((4, 3 * SEG)), _const_spec((1, HD)), _const_spec((1, HD)), _const_spec((1, HD))]
        + x_in_specs,
        out_specs=[pl.BlockSpec((LC, SEG), lambda i: (i, 0)), pl.BlockSpec((1, SEG, HD), lambda i: (i, 0, 0))] + x_out_specs,
        out_shape=[jax.ShapeDtypeStruct((s, SEG), f32), jax.ShapeDtypeStruct((nc, SEG, HD), f32)] + x_out_shapes,
        scratch_shapes=[pltpu.VMEM((SEG, HD), f32)] + x_sems, input_output_aliases=aliases,
        name="gdn_fwd" if xf is None else "gdn_fwd_carrying")(
            proj, proj, proj, proj, proj, proj, proj, proj, conv_w, alog_row, dt_row, norm_w, *x_ops)


def gdn_bwd(proj, dy, ss, conv_w, alog_row, dt_row, norm_w, xf=None):
    s = proj.shape[0]
    nc = s // LC
    n_in, n_out = 14, 6
    x_ops, x_in_specs, x_out_shapes, x_out_specs, aliases, x_sems, n_xin, n_land = _carried(xf, n_in, n_out)

    def body(*refs):
        (qh_ref, q_ref, kh_ref, k_ref, vh_ref, v_ref, z_ref, sm_ref, dy_ref, ss_ref,
         cw_ref, al_ref, dt_ref, nw_ref) = refs[:n_in]
        x_refs = refs[n_in:n_in + n_xin]
        dp_ref, dsm_ref, dcw_ref, dal_ref, ddt_ref, dnw_ref = refs[n_in + n_xin:n_in + n_xin + n_out]
        land_refs = refs[n_in + n_xin + n_out:n_in + n_xin + n_out + n_land]
        dst_s, dhalo = refs[n_in + n_xin + n_out + n_land:n_in + n_xin + n_out + n_land + 2]
        sem_refs = refs[n_in + n_xin + n_out + n_land + 2:]
        i = pl.program_id(0)
        first_chunk = i == nc - 1

        @pl.when(i == 0)
        def _():
            for ref in (dcw_ref, dal_ref, ddt_ref, dnw_ref, dst_s, dhalo):
                ref[...] = jnp.zeros_like(ref)
            if xf is not None:
                _xfer_start(xf, x_refs[:len(xf.srcs)], land_refs, sem_refs)

        halo = [jnp.where(first_chunk, 0.0, r[...]) for r in (qh_ref, kh_ref, vh_ref)]
        _, vjp = jax.vjp(gd_chunk, halo[0], q_ref[...], halo[1], k_ref[...], halo[2], v_ref[...], z_ref[...], sm_ref[...],
                         cw_ref[...], al_ref[...], dt_ref[...], nw_ref[...], ss_ref[0])
        dqh, dq, dkh, dk, dvh, dv, dz, dsm, dcw, dal, ddt, dnw, dst = vjp((dy_ref[...], dst_s[...]))
        for j, val in enumerate((dq, dk, dv, dz)):
            dp_ref[:, j * SEG:(j + 1) * SEG] = val
        dp_ref[LC - 8:LC, 0:3 * SEG] += dhalo[...]
        for j, val in enumerate((dqh, dkh, dvh)):
            dhalo[:, j * SEG:(j + 1) * SEG] = val
        dsm_ref[...] = dsm
        dcw_ref[...] += dcw
        dal_ref[0:1, :] += dal
        ddt_ref[0:1, :] += ddt
        dnw_ref[0:1, :] += dnw
        dst_s[...] = dst

        if xf is not None:
            @pl.when(first_chunk)
            def _():
                _xfer_wait(xf, x_refs[:len(xf.srcs)], land_refs, sem_refs)

    qkv_specs = []
    for sg in GD_SEGS:
        qkv_specs += [_halo_spec(LC, sg, nc), _seg_spec(LC, sg, nc)]
    return pl.pallas_call(
        body, grid=(nc,),
        in_specs=qkv_specs + [_seg_spec(LC, 10, nc), pl.BlockSpec((LC, HD), lambda i: (nc - 1 - i, SMALL_BLK)),
                              pl.BlockSpec((LC, SEG), lambda i: (nc - 1 - i, 0)),
                              pl.BlockSpec((1, SEG, HD), lambda i: (nc - 1 - i, 0, 0)),
                              _const_spec((4, 3 * SEG)), _const_spec((1, HD)), _const_spec((1, HD)), _const_spec((1, HD))]
        + x_in_specs,
        out_specs=[pl.BlockSpec((LC, 4 * SEG), lambda i: (nc - 1 - i, 0)), pl.BlockSpec((LC, HD), lambda i: (nc - 1 - i, 0)),
                   _const_spec((4, 3 * SEG)), _const_spec((8, HD)), _const_spec((8, HD)), _const_spec((8, HD))] + x_out_specs,
        out_shape=[jax.ShapeDtypeStruct((s, 4 * SEG), f32), jax.ShapeDtypeStruct((s, HD), f32),
                   jax.ShapeDtypeStruct((4, 3 * SEG), f32), jax.ShapeDtypeStruct((8, HD), f32),
                   jax.ShapeDtypeStruct((8, HD), f32), jax.ShapeDtypeStruct((8, HD), f32)] + x_out_shapes,
        scratch_shapes=[pltpu.VMEM((SEG, HD), f32), pltpu.VMEM((8, 3 * SEG), f32)] + x_sems, input_output_aliases=aliases,
        name="gdn_bwd" if xf is None else "gdn_bwd_carrying", compiler_params=_cp(48))(
            proj, proj, proj, proj, proj, proj, proj, proj, dy, ss, conv_w, alog_row, dt_row, norm_w, *x_ops)


def _my_place():
    return lax.axis_index("x"), lax.axis_index("y"), lax.axis_index("c")


def _slot(p):
    return 4 * p[0] + 2 * p[1] + p[2]


def _peer(me, j):
    flips = ((j >> 2) & 1, (j >> 1) & 1, j & 1)
    return tuple((1 - v) if fl else v for v, fl in zip(me, flips))


_ANY = pl.BlockSpec(memory_space=pl.ANY)


class Xfer(NamedTuple):
    srcs: list
    kinds: list
    lands: list
    land_of: list
    layer: list


def _xfer_plumb(xf, n_in, n_out):
    n_src = len(xf.srcs)
    passed = [ld for ld in xf.lands if not isinstance(ld, jax.ShapeDtypeStruct)]
    aliases, k = {}, 0
    for li, ld in enumerate(xf.lands):
        if not isinstance(ld, jax.ShapeDtypeStruct):
            aliases[n_in + n_src + k] = n_out + li
            k += 1
    operands = list(xf.srcs) + passed
    out_shapes = [jax.ShapeDtypeStruct(ld.shape, ld.dtype) for ld in xf.lands]
    sems = [pltpu.SemaphoreType.DMA((n_src, N_DEV - 1)), pltpu.SemaphoreType.DMA((n_src, N_DEV - 1)),
            pltpu.SemaphoreType.DMA((n_src,))]
    return operands, [_ANY] * len(operands), out_shapes, [_ANY] * len(xf.lands), aliases, sems


def _xfer_copies(xf, src_refs, land_refs, send_sems, recv_sems, local_sems):
    me = _my_place()
    mine = _slot(me)
    local, remote = [], []
    for a, src_ref in enumerate(src_refs):
        land = land_refs[xf.land_of[a]]
        dst = land.at[mine] if xf.layer[a] is None else land.at[mine, xf.layer[a]]
        by_slot = xf.kinds[a] == "slot"
        local.append(pltpu.make_async_copy(src_ref.at[mine] if by_slot else src_ref, dst, local_sems.at[a]))
        for j in range(1, N_DEV):
            peer = _peer(me, j)
            remote.append(pltpu.make_async_remote_copy(
                src_ref=src_ref.at[_slot(peer)] if by_slot else src_ref, dst_ref=dst, send_sem=send_sems.at[a, j - 1],
                recv_sem=recv_sems.at[a, j - 1], device_id=peer, device_id_type=MESH))
    return local, remote


def _xfer_start(xf, src_refs, land_refs, sems):
    local, remote = _xfer_copies(xf, src_refs, land_refs, *sems)
    for cp in local + remote:
        cp.start()


def _xfer_wait(xf, src_refs, land_refs, sems):
    local, remote = _xfer_copies(xf, src_refs, land_refs, *sems)
    for cp in remote:
        cp.wait()
    for cp in local:
        cp.wait()


def exchange(xf, name):
    n_src, n_land = len(xf.srcs), len(xf.lands)
    operands, in_specs, out_shapes, out_specs, aliases, sems = _xfer_plumb(xf, 0, 0)

    def body(*refs):
        src_refs = refs[:n_src]
        land_refs = refs[len(operands):len(operands) + n_land]
        sem_refs = refs[len(operands) + n_land:]
        _xfer_start(xf, src_refs, land_refs, sem_refs)
        _xfer_wait(xf, src_refs, land_refs, sem_refs)

    return pl.pallas_call(body, in_specs=in_specs, out_specs=out_specs, out_shape=out_shapes, scratch_shapes=sems,
                          input_output_aliases=aliases, name=name)(*operands)


def _adamw(w, g, m, v):
    m = ADAM_B1 * m + (1.0 - ADAM_B1) * g
    v = ADAM_B2 * v + (1.0 - ADAM_B2) * (g * g)
    m_hat = m / (1.0 - ADAM_B1 ** ADAM_STEP)
    v_hat = v / (1.0 - ADAM_B2 ** ADAM_STEP)
    delta = -ADAM_LR * (m_hat / (jnp.sqrt(v_hat) + ADAM_EPS) + ADAM_WD * w)
    return delta, m, v


def adam_slots(slots, w, m, v, rows, name):
    _, depth, r, c = slots.shape

    def body(s_ref, w_ref, m_ref, v_ref, g_ref, d_ref, m2_ref, v2_ref):
        g = s_ref[0, 0].astype(f32)
        for k in range(1, N_DEV):
            g = g + s_ref[k, 0].astype(f32)
        d, m2, v2 = _adamw(w_ref[0], g, m_ref[0], v_ref[0])
        g_ref[0] = g
        d_ref[0] = d
        m2_ref[0] = m2
        v2_ref[0] = v2

    blk = pl.BlockSpec((1, rows, c), lambda l, i: (l, i, 0))
    sh = jax.ShapeDtypeStruct((depth, r, c), f32)
    return pl.pallas_call(
        body, grid=(depth, r // rows),
        in_specs=[pl.BlockSpec((N_DEV, 1, rows, c), lambda l, i: (0, l, i, 0)), blk, blk, blk],
        out_specs=[blk] * 4, out_shape=[sh] * 4, name=name, compiler_params=_cp(40))(slots, w, m, v)


def sum_slots(slots, name):
    _, r, c = slots.shape

    def body(s_ref, o_ref):
        g = s_ref[0]
        for k in range(1, N_DEV):
            g = g + s_ref[k]
        o_ref[...] = g

    return pl.pallas_call(body, out_shape=jax.ShapeDtypeStruct((r, c), f32), name=name, compiler_params=_cp(40))(slots)


def adam_flat(g, w, m, v, name):
    def body(g_ref, w_ref, m_ref, v_ref, d_ref, m2_ref, v2_ref):
        d, m2, v2 = _adamw(w_ref[...], g_ref[...], m_ref[...], v_ref[...])
        d_ref[...] = d
        m2_ref[...] = m2
        v2_ref[...] = v2

    sh = jax.ShapeDtypeStruct(g.shape, f32)
    return pl.pallas_call(body, out_shape=[sh] * 3, name=name)(g, w, m, v)


def _rows_of(shape):
    n = 1
    for dim in shape:
        n *= dim
    return n, -(-n // (8 * 128)) * 8


def _pack(arrs):
    parts = []
    for a in arrs:
        n, rows = _rows_of(a.shape)
        parts.append(jnp.pad(a.reshape(-1).astype(f32), (0, rows * 128 - n)).reshape(rows, 128))
    return jnp.concatenate(parts, axis=0)


def _unpack(packed, shapes):
    out, row = [], 0
    for sh in shapes:
        n, rows = _rows_of(sh)
        out.append(packed[row:row + rows].reshape(-1)[:n].reshape(sh))
        row += rows
    return out


def _block_diag(gw):
    eye = jnp.eye(8, dtype=gw.dtype)
    return (gw[:, :, None, :] * eye[:, None, :, None]).reshape(SEG, SEG)


def _diag_blocks(dense):
    eye = jnp.eye(8, dtype=dense.dtype)
    return (dense.reshape(8, 64, 8, 64) * eye[:, None, :, None]).sum(axis=2)


def _lane_row(vals, first_lane):
    return jnp.zeros((1, HD), f32).at[0, first_lane:first_lane + NH].set(vals)


def kernel(x, norm_w, w_in, rg_conv_w, rg_conv_b, rg_gate_w, rg_gate_b, rg_lambda, ml_gate_b, ml_norm_w, gd_conv_w, gd_a_log, gd_dt_bias, gd_norm_w, w_out, final_norm_w, loss_target, m_norm_w, m_w_in, m_rg_conv_w, m_rg_conv_b, m_rg_gate_w, m_rg_gate_b, m_rg_lambda, m_ml_gate_b, m_ml_norm_w, m_gd_conv_w, m_gd_a_log, m_gd_dt_bias, m_gd_norm_w, m_w_out, m_final_norm_w, v_norm_w, v_w_in, v_rg_conv_w, v_rg_conv_b, v_rg_gate_w, v_rg_gate_b, v_rg_lambda, v_ml_gate_b, v_ml_norm_w, v_gd_conv_w, v_gd_a_log, v_gd_dt_bias, v_gd_norm_w, v_w_out, v_final_norm_w):
    s = x.shape[1]
    xs = x.reshape(s, D)
    tgt = loss_target.reshape(s, D)
    me = 4 * lax.axis_index("x") + 2 * lax.axis_index("y") + lax.axis_index("c")

    comm = MeshComm(w_in, w_out, [rg_conv_w, rg_gate_b, gd_conv_w])
    rg_conv_full, rg_gb_full, gd_conv_full = comm.small_weights
    loss_part, dx, d_fw, g_small = local_step(
        xs, tgt, comm, rg_conv_full, rg_gb_full, gd_conv_full, norm_w, rg_conv_b, rg_gate_w, rg_lambda,
        ml_gate_b, ml_norm_w, gd_a_log, gd_dt_bias, gd_norm_w, final_norm_w)
    given_w = dict(norm_w=norm_w, rg_conv_w=rg_conv_w, rg_conv_b=rg_conv_b, rg_gate_w=rg_gate_w, rg_gate_b=rg_gate_b,
                   rg_lambda=rg_lambda, ml_gate_b=ml_gate_b, ml_norm_w=ml_norm_w, gd_conv_w=gd_conv_w, gd_a_log=gd_a_log,
                   gd_dt_bias=gd_dt_bias, gd_norm_w=gd_norm_w, final_norm_w=final_norm_w, w_in=w_in, w_out=w_out)
    given_m = dict(norm_w=m_norm_w, rg_conv_w=m_rg_conv_w, rg_conv_b=m_rg_conv_b, rg_gate_w=m_rg_gate_w, rg_gate_b=m_rg_gate_b,
                   rg_lambda=m_rg_lambda, ml_gate_b=m_ml_gate_b, ml_norm_w=m_ml_norm_w, gd_conv_w=m_gd_conv_w,
                   gd_a_log=m_gd_a_log, gd_dt_bias=m_gd_dt_bias, gd_norm_w=m_gd_norm_w, final_norm_w=m_final_norm_w,
                   w_in=m_w_in, w_out=m_w_out)
    given_v = dict(norm_w=v_norm_w, rg_conv_w=v_rg_conv_w, rg_conv_b=v_rg_conv_b, rg_gate_w=v_rg_gate_w, rg_gate_b=v_rg_gate_b,
                   rg_lambda=v_rg_lambda, ml_gate_b=v_ml_gate_b, ml_norm_w=v_ml_norm_w, gd_conv_w=v_gd_conv_w,
                   gd_a_log=v_gd_a_log, gd_dt_bias=v_gd_dt_bias, gd_norm_w=v_gd_norm_w, final_norm_w=v_final_norm_w,
                   w_in=v_w_in, w_out=v_w_out)
    return finish_step(loss_part, dx, d_fw, g_small, comm, s, me, given_w, given_m, given_v)


def _gathered_pieces():
    per = D_IN // N_DEV
    pieces = []
    for lo, hi in ((0, 3584), (3592, 5640), (3584, 3592), (5640, 5648)):
        col = lo
        while col < hi:
            k = col // per
            end = min(hi, (k + 1) * per)
            pieces.append((k, col - k * per, end - k * per))
            col = end
    return pieces


class MeshComm:
    def __init__(self, w_in, w_out, small_shards):
        per = D_IN // N_DEV
        self.wi_sh = [w_in[l].astype(bf16) for l in range(DEPTH)]
        self.wo_sh = [w_out[l].astype(bf16) for l in range(DEPTH)]
        self.wi_land = jax.ShapeDtypeStruct((N_DEV, D, per), bf16)
        self.wo_land = jax.ShapeDtypeStruct((N_DEV, 3 * SEG // N_DEV, D), bf16)
        packed = _pack(small_shards)
        xf = Xfer([self.wi_sh[0], self.wo_sh[0], packed], ["all"] * 3,
                  [self.wi_land, self.wo_land, jax.ShapeDtypeStruct((N_DEV,) + packed.shape, f32)], [0, 1, 2], [None] * 3)
        wi_g, wo_g, sm_g = exchange(xf, "gather_first")
        self.landed = {0: (wi_g, wo_g)}
        shapes = [a.shape for a in small_shards]
        parts = [_unpack(sm_g[k], shapes) for k in range(N_DEV)]
        self.small_weights = [jnp.concatenate([p[j] for p in parts], axis=-1) for j in range(len(small_shards))]
        self.gwi_land = lax.empty((N_DEV, DEPTH, D, per), bf16)
        self.gwo_land = lax.empty((N_DEV, DEPTH, 3 * SEG // N_DEV, D), bf16)
        self.pending = None
        self.kept_gwo = None

    def weights(self, l):
        wi_g, wo_g = self.landed[l]
        cols = [wi_g[k, :, a:b] for k, a, b in _gathered_pieces()] + [jnp.zeros((D, DP - D_IN), bf16)]
        return jnp.concatenate(cols, axis=1), wo_g.reshape(3 * SEG, D)

    def fwd_xfer(self, l):
        if l + 1 >= DEPTH:
            return None
        return Xfer([self.wi_sh[l + 1], self.wo_sh[l + 1]], ["all", "all"], [self.wi_land, self.wo_land], [0, 1], [None, None])

    def fwd_landed(self, l, landed):
        self.landed[l + 1] = tuple(landed)

    def bwd_xfer(self, l, g_wo):
        gwo = g_wo.reshape(N_DEV, 3 * SEG // N_DEV, D).astype(bf16)
        srcs, land_of, layer = [], [], []
        if self.pending is not None:
            srcs, land_of, layer = list(self.pending), [0, 1], [l + 1, l + 1]
        if l == 0:
            srcs, land_of, layer = srcs + [gwo], land_of + [1], layer + [0]
        self.kept_gwo = gwo
        if not srcs:
            return None
        return Xfer(srcs, ["slot"] * len(srcs), [self.gwi_land, self.gwo_land], land_of, layer)

    def bwd_landed(self, landed):
        self.gwi_land, self.gwo_land = landed

    def grads_ready(self, l, pieces):
        per = D_IN // N_DEV
        g_wi = jnp.concatenate(pieces, axis=1)
        slots = jnp.stack([g_wi[:, k * per:(k + 1) * per] for k in range(N_DEV)]).astype(bf16)
        self.pending = (slots, self.kept_gwo)

    def finish(self, packed_small):
        xf = Xfer([self.pending[0], packed_small], ["slot", "all"],
                  [self.gwi_land, jax.ShapeDtypeStruct((N_DEV,) + packed_small.shape, f32)], [0, 1], [0, None])
        gwi_land, gsm = exchange(xf, "scatter_last")
        return gwi_land, self.gwo_land, gsm


def local_step(xs, tgt, comm, rg_conv_full, rg_gb_full, gd_conv_full, norm_w, rg_conv_b, rg_gate_w,
               rg_lambda, ml_gate_b, ml_norm_w, gd_a_log, gd_dt_bias, gd_norm_w, final_norm_w):
    acts = []
    for l in range(DEPTH):
        nw = norm_w[l].reshape(1, D)
        w_in_l, w_out_l = comm.weights(l)
        proj = inproj(xs, nw, w_in_l)
        rg_p = (rg_conv_full[l], rg_conv_b[l].reshape(1, SEG), _block_diag(rg_gate_w[l, 0]), _block_diag(rg_gate_w[l, 1]),
                rg_gb_full[l], rg_lambda[l].reshape(1, SEG))
        y_rg, hs = rglru_fwd(proj, *rg_p)
        ml_p = (jnp.zeros((1, HD), f32).at[0, 0:2 * NH].set(ml_gate_b[l].reshape(-1)), ml_norm_w[l].reshape(1, SEG))
        y_ml, cs, ns, ms = mlstm_fwd(proj, *ml_p)
        gd_p = (gd_conv_full[l], _lane_row(gd_a_log[l], 2 * NH), _lane_row(gd_dt_bias[l], 2 * NH), gd_norm_w[l].reshape(1, HD))
        xf = comm.fwd_xfer(l)
        y_gd, ss, *landed = gdn_fwd(proj, *gd_p, xf=xf)
        if xf is not None:
            comm.fwd_landed(l, landed)
        acts.append((xs, nw, proj, w_in_l, w_out_l, rg_p, y_rg, hs, ml_p, y_ml, cs, ns, ms, gd_p, y_gd, ss))
        xs = outproj(xs, y_rg, y_ml, y_gd, w_out_l)

    dx, loss_part, d_fw = head(xs, final_norm_w.reshape(1, D), tgt)

    g_small = {k: [None] * DEPTH for k in ("norm_w", "rg_conv_w", "rg_conv_b", "rg_gate_w", "rg_gate_b", "rg_lambda",
                                           "ml_gate_b", "ml_norm_w", "gd_conv_w", "gd_a_log", "gd_dt_bias", "gd_norm_w")}
    for l in reversed(range(DEPTH)):
        x_l, nw, proj, w_in_l, w_out_l, rg_p, y_rg, hs, ml_p, y_ml, cs, ns, ms, gd_p, y_gd, ss = acts[l]
        dy_rg, dy_ml, dy_gd, g_wo = outproj_bwd(dx, y_rg, y_ml, y_gd, w_out_l)
        d_rg, d_cw, d_cb, d_gr, d_gi, d_gb, d_lam = rglru_bwd(proj, hs, dy_rg, *rg_p)
        d_ml, d_sm_ml, d_bias, d_mnw = mlstm_bwd(proj, dy_ml, cs, ns, ms, *ml_p)
        xf = comm.bwd_xfer(l, g_wo)
        d_gd, d_sm_gd, d_gcw, d_al, d_dt, d_gnw, *landed = gdn_bwd(proj, dy_gd, ss, *gd_p, xf=xf)
        if xf is not None:
            comm.bwd_landed(landed)
        dx, d_nw = inproj_bwd_x(x_l, nw, w_in_l, dx, d_rg, d_ml, d_gd, d_sm_ml, d_sm_gd)
        gw_rg = wgrad(x_l, nw, [d_rg], "wgrad_rg")
        gw_ml = wgrad(x_l, nw, [d_ml], "wgrad_ml")
        gw_gd = wgrad(x_l, nw, [d_gd], "wgrad_gd")
        gw_sm = wgrad(x_l, nw, [d_sm_ml, d_sm_gd], "wgrad_small")
        comm.grads_ready(l, [gw_rg, gw_ml, gw_sm[:, 0:2 * NH], gw_gd, gw_sm[:, 2 * NH:4 * NH]])
        g_small["norm_w"][l] = d_nw[0]
        g_small["rg_conv_w"][l] = d_cw
        g_small["rg_conv_b"][l] = d_cb[0]
        g_small["rg_gate_w"][l] = jnp.stack([_diag_blocks(d_gr), _diag_blocks(d_gi)])
        g_small["rg_gate_b"][l] = d_gb
        g_small["rg_lambda"][l] = d_lam[0]
        g_small["ml_gate_b"][l] = d_bias[0, 0:2 * NH].reshape(2, NH)
        g_small["ml_norm_w"][l] = d_mnw[0]
        g_small["gd_conv_w"][l] = d_gcw
        g_small["gd_a_log"][l] = d_al[0, 2 * NH:3 * NH]
        g_small["gd_dt_bias"][l] = d_dt[0, 2 * NH:3 * NH]
        g_small["gd_norm_w"][l] = d_gnw[0]
    return loss_part, dx, d_fw, g_small


def finish_step(loss_part, dx, d_fw, g_small, comm, s, me, given_w, given_m, given_v):
    small_names = ["norm_w", "rg_conv_w", "rg_conv_b", "rg_gate_w", "rg_gate_b", "rg_lambda", "ml_gate_b", "ml_norm_w",
                   "gd_conv_w", "gd_a_log", "gd_dt_bias", "gd_norm_w"]
    small_list = [loss_part[0, 0:1], d_fw[0]] + [jnp.stack(g_small[k]) for k in small_names]
    small_shapes = [a.shape for a in small_list]
    gwi_r, gwo_r, gsm_r = comm.finish(_pack(small_list))

    g_wi, d_wi, m_wi, v_wi = adam_slots(gwi_r, given_w["w_in"], given_m["w_in"], given_v["w_in"], 256, "adam_w_in")
    g_wo, d_wo, m_wo, v_wo = adam_slots(gwo_r, given_w["w_out"], given_m["w_out"], given_v["w_out"], 192, "adam_w_out")
    g_all = _unpack(sum_slots(gsm_r, "sum_small"), small_shapes)
    loss = g_all[0][0]
    grads = {"final_norm_w": g_all[1]}
    for k, g in zip(small_names, g_all[2:]):
        grads[k] = g
    for k, width in (("rg_conv_w", 64), ("rg_gate_b", 64), ("gd_conv_w", 192)):
        grads[k] = lax.dynamic_slice_in_dim(grads[k], me * width, width, axis=2)
    names = small_names + ["final_norm_w"]
    shapes = [given_w[k].shape for k in names]
    d_p, m_p, v_p = adam_flat(_pack([grads[k] for k in names]), _pack([given_w[k] for k in names]),
                              _pack([given_m[k] for k in names]), _pack([given_v[k] for k in names]), "adam_small")
    deltas = dict(zip(names, _unpack(d_p, shapes)))
    new_m = dict(zip(names, _unpack(m_p, shapes)))
    new_v = dict(zip(names, _unpack(v_p, shapes)))
    grads["w_in"], deltas["w_in"], new_m["w_in"], new_v["w_in"] = g_wi, d_wi, m_wi, v_wi
    grads["w_out"], deltas["w_out"], new_m["w_out"], new_v["w_out"] = g_wo, d_wo, m_wo, v_wo

    order = ["norm_w", "w_in", "rg_conv_w", "rg_conv_b", "rg_gate_w", "rg_gate_b", "rg_lambda", "ml_gate_b", "ml_norm_w",
             "gd_conv_w", "gd_a_log", "gd_dt_bias", "gd_norm_w", "w_out", "final_norm_w"]
    return (loss, dx.reshape(1, s, D), *[grads[k] for k in order], *[deltas[k] for k in order],
            *[new_m[k] for k in order], *[new_v[k] for k in order])
```

```python
import functools
from typing import NamedTuple

import jax
import jax.numpy as jnp
from jax import lax
from jax.experimental import pallas as pl
from jax.experimental.pallas import tpu as pltpu

f32 = jnp.float32
bf16 = jnp.bfloat16
MESH = pl.DeviceIdType.MESH

N_DEV = 8
D = 1024
DEPTH = 4
EPS = 1e-6
SEG = 512
HD = 128
NH = 4
LC = 64
RUN = 4
LR = RUN * LC
D_IN = 5648
DP = 5760
SMALL0 = 5632
RG_TILE = 256
RG_C = 8.0

ADAM_LR = 0.001
ADAM_B1 = 0.9
ADAM_B2 = 0.999
ADAM_EPS = 1e-08
ADAM_WD = 0.01
ADAM_STEP = 10


def _cp(vmem_mb):
    return pltpu.CompilerParams(vmem_limit_bytes=vmem_mb * 2 ** 20)


def _dot(a, b, ca, cb):
    return lax.dot_general(a.astype(bf16), b.astype(bf16), (((ca,), (cb,)), ((), ())), preferred_element_type=f32)


@jax.custom_vjp
def mm_nn(a, b):
    return _dot(a, b, 1, 0)


@jax.custom_vjp
def mm_nt(a, b):
    return _dot(a, b, 1, 1)


@jax.custom_vjp
def mm_tn(a, b):
    return _dot(a, b, 0, 0)


mm_nn.defvjp(lambda a, b: (mm_nn(a, b), (a, b)), lambda r, g: (mm_nt(g, r[1]), mm_tn(r[0], g)))
mm_nt.defvjp(lambda a, b: (mm_nt(a, b), (a, b)), lambda r, g: (mm_nn(g, r[1]), mm_tn(g, r[0])))
mm_tn.defvjp(lambda a, b: (mm_tn(a, b), (a, b)), lambda r, g: (mm_nt(r[1], g), mm_nn(r[0], g)))


def _split(x):
    hi = x.astype(bf16)
    return hi, (x - hi.astype(f32)).astype(bf16)


def dot3(a, b):
    ah, al = _split(a)
    bh, bl = _split(b)
    d = functools.partial(jnp.dot, preferred_element_type=f32)
    return d(ah, bh) + (d(al, bh) + d(ah, bl))


def _tri_sum(x, reverse):
    n = x.shape[0]
    r = lax.broadcasted_iota(jnp.int32, (n, 3 * n), 0)
    c = lax.broadcasted_iota(jnp.int32, (n, 3 * n), 1) & (n - 1)
    ones = ((c >= r) if reverse else (c <= r)).astype(bf16)
    hi = x.astype(bf16)
    rest = x - hi.astype(f32)
    mid = rest.astype(bf16)
    lo = (rest - mid.astype(f32)).astype(bf16)
    return jnp.dot(ones, jnp.concatenate([hi, mid, lo], axis=0), preferred_element_type=f32)


@jax.custom_vjp
def cumsum_rows(x):
    return _tri_sum(x, False)


@jax.custom_vjp
def rev_cumsum_rows(x):
    return _tri_sum(x, True)


cumsum_rows.defvjp(lambda x: (cumsum_rows(x), None), lambda _, g: (rev_cumsum_rows(g),))
rev_cumsum_rows.defvjp(lambda x: (rev_cumsum_rows(x), None), lambda _, g: (cumsum_rows(g),))


def _tri(n, strict=False):
    r = lax.broadcasted_iota(jnp.int32, (n, n), 0)
    c = lax.broadcasted_iota(jnp.int32, (n, n), 1)
    return (r > c) if strict else (r >= c)


def _lane_col(v, j):
    lane = lax.broadcasted_iota(jnp.int32, v.shape, 1)
    return jnp.sum(jnp.where(lane == j, v, 0.0), axis=1, keepdims=True)


def _rows_from(pieces, rows, width):
    ridx = lax.broadcasted_iota(jnp.int32, (rows, width), 0)
    out = jnp.zeros((rows, width), f32)
    for h, p in enumerate(pieces):
        out = out + jnp.where(ridx == h, jnp.broadcast_to(p, (rows, width)), 0.0)
    return out


def causal_conv(halo8, x, w4):
    t = x.shape[0]
    xe = jnp.concatenate([halo8, x], axis=0)
    y = xe[5:5 + t] * w4[0:1]
    for k in range(1, 4):
        y = y + xe[5 + k:5 + k + t] * w4[k:k + 1]
    return y


def ml_chunk(q, k, v, o_pre, z, small, bias_row, norm_w, C, n, m):
    n_ch = q.shape[0] // LC
    lane = lax.broadcasted_iota(jnp.int32, small.shape, 1)
    pre = small + bias_row
    lg = jnp.where(lane < NH, pre, jnp.where(lane < 2 * NH, jax.nn.log_sigmoid(pre), 0.0))
    rows = [slice(c * LC, (c + 1) * LC) for c in range(n_ch)]
    lgs = [lg[r] for r in rows]
    bcs = [cumsum_rows(x) for x in lgs]
    lg_ts = [x.T for x in lgs]
    bc_ts = [x.T for x in bcs]
    causal = _tri(LC)
    pairs = [(c, h) for c in range(n_ch) for h in range(NH)]
    idx = range(len(pairs))
    sls = [slice(h * HD, (h + 1) * HD) for h in range(NH)]
    qs = [q[rows[c], sls[h]] * (HD ** -0.5) for c, h in pairs]
    ks = [k[rows[c], sls[h]] for c, h in pairs]
    vs = [v[rows[c], sls[h]] for c, h in pairs]
    li_cols = [_lane_col(lgs[c], h) for c, h in pairs]
    b_cols = [_lane_col(bcs[c], NH + h) for c, h in pairs]
    dms = [jnp.where(causal, b_cols[i] - bc_ts[c][NH + h:NH + h + 1, :] + lg_ts[c][h:h + 1, :], -jnp.inf)
           for i, (c, h) in enumerate(pairs)]
    dm_maxs = [jnp.max(dms[i], axis=-1, keepdims=True) for i in idx]
    gs = [b_cols[i][LC - 1:LC, :] for i in idx]
    wss = [gs[i] - b_cols[i] + li_cols[i] for i in idx]
    ws_maxs = [jnp.max(wss[i], axis=0, keepdims=True) for i in idx]
    qks = [mm_nt(qs[i], ks[i]) for i in idx]
    m_in = [None] * len(pairs)
    m_out = [None] * len(pairs)
    for h in range(NH):
        cur = m[h:h + 1, 0:1]
        for c in range(n_ch):
            i = c * NH + h
            m_in[i] = cur
            cur = jnp.maximum(gs[i] + cur, ws_maxs[i])
            m_out[i] = cur
    m_inters = [b_cols[i] + m_in[i] for i in idx]
    m_ts = [jnp.maximum(m_inters[i], dm_maxs[i]) for i in idx]
    ss = [qks[i] * jnp.exp(dms[i] - m_ts[i]) for i in idx]
    scs = [jnp.exp(m_inters[i] - m_ts[i]) for i in idx]
    decs = [jnp.exp(gs[i] + m_in[i] - m_out[i]) for i in idx]
    kws = [jnp.exp(wss[i] - m_out[i]) * ks[i] for i in idx]
    c_adds = [mm_tn(kws[i], vs[i]) for i in idx]
    n_adds = [jnp.sum(kws[i], axis=0, keepdims=True) for i in idx]
    svs = [mm_nn(ss[i], vs[i]) for i in idx]
    s_sums = [jnp.sum(ss[i], axis=-1, keepdims=True) for i in idx]
    c_hs = [C[sl, :] for sl in sls]
    n_hs = [n[h:h + 1, :] for h in range(NH)]
    hhs = [None] * len(pairs)
    for c in range(n_ch):
        for h in range(NH):
            i = c * NH + h
            num = svs[i] + scs[i] * mm_nn(qs[i], c_hs[h])
            den = s_sums[i] + scs[i] * jnp.sum(qs[i] * n_hs[h], axis=-1, keepdims=True)
            hhs[i] = num / jnp.maximum(jnp.abs(den), jnp.exp(-m_ts[i]))
        c_hs = [decs[c * NH + h] * c_hs[h] + c_adds[c * NH + h] for h in range(NH)]
        n_hs = [decs[c * NH + h] * n_hs[h] + n_adds[c * NH + h] for h in range(NH)]
    ys = [hhs[i] * lax.rsqrt(jnp.mean(hhs[i] * hhs[i], axis=-1, keepdims=True) + EPS) * norm_w[:, sls[h]]
          * jax.nn.sigmoid(o_pre[rows[c], sls[h]]) * jax.nn.silu(z[rows[c], sls[h]]) for i, (c, h) in enumerate(pairs)]
    y = jnp.concatenate([jnp.concatenate(ys[c * NH:(c + 1) * NH], axis=1) for c in range(n_ch)], axis=0)
    last = (n_ch - 1) * NH
    m_rows = [jnp.broadcast_to(m_out[last + h], (1, HD)) for h in range(NH)]
    return y, jnp.concatenate(c_hs, axis=0), _rows_from(n_hs, 8, HD), _rows_from(m_rows, 8, HD)


@jax.custom_vjp
def _unit_lower_inverses(mats):
    eye = (lax.broadcasted_iota(jnp.int32, (LC, LC), 0) == lax.broadcasted_iota(jnp.int32, (LC, LC), 1)).astype(f32)
    ps = [-m for m in mats]
    ts = [eye + p for p in ps]
    for _ in range(5):
        ps = [dot3(p, p) for p in ps]
        ts = [t + dot3(t, p) for t, p in zip(ts, ps)]
    return tuple(ts)


def _unit_lower_inverses_fwd(mats):
    ts = _unit_lower_inverses(mats)
    return ts, ts


def _unit_lower_inverses_bwd(ts, gs):
    tts = [t.T for t in ts]
    mid = [dot3(tt, g) for tt, g in zip(tts, gs)]
    return (tuple(-dot3(m, tt) for m, tt in zip(mid, tts)),)


_unit_lower_inverses.defvjp(_unit_lower_inverses_fwd, _unit_lower_inverses_bwd)


def gd_chunk(qh8, q, kh8, k, vh8, v, z, small, conv_w, alog_row, dt_row, norm_w, st):
    n_ch = q.shape[0] // LC
    lane = lax.broadcasted_iota(jnp.int32, small.shape, 1)
    is_a = (lane >= 2 * NH) & (lane < 3 * NH)
    g_all = jnp.where(is_a, -jnp.exp(alog_row) * jax.nn.softplus(small + dt_row), 0.0)
    beta_all = jax.nn.sigmoid(small)
    qc = jax.nn.silu(causal_conv(qh8, q, conv_w[:, 0:SEG]))
    kc = jax.nn.silu(causal_conv(kh8, k, conv_w[:, SEG:2 * SEG]))
    vc = jax.nn.silu(causal_conv(vh8, v, conv_w[:, 2 * SEG:3 * SEG]))
    incl = _tri(LC)
    strict = _tri(LC, strict=True)
    rows = [slice(c * LC, (c + 1) * LC) for c in range(n_ch)]
    gc_alls = [cumsum_rows(g_all[r]) for r in rows]
    gc_ts = [g.T for g in gc_alls]
    pairs = [(c, h) for c in range(n_ch) for h in range(NH)]
    idx = range(len(pairs))
    sls = [slice(h * HD, (h + 1) * HD) for h in range(NH)]
    qs = [qc[rows[c], sls[h]] for c, h in pairs]
    ks = [kc[rows[c], sls[h]] for c, h in pairs]
    vs = [vc[rows[c], sls[h]] for c, h in pairs]
    qs = [x * lax.rsqrt(jnp.sum(x * x, axis=-1, keepdims=True) + EPS) * (HD ** -0.5) for x in qs]
    ks = [x * lax.rsqrt(jnp.sum(x * x, axis=-1, keepdims=True) + EPS) for x in ks]
    betas = [_lane_col(beta_all[rows[c]], 3 * NH + h) for c, h in pairs]
    gcs = [_lane_col(gc_alls[c], 2 * NH + h) for c, h in pairs]
    gams = [jnp.exp(jnp.where(incl, gcs[i] - gc_ts[c][2 * NH + h:2 * NH + h + 1, :], -jnp.inf))
            for i, (c, h) in enumerate(pairs)]
    kbs = [ks[i] * betas[i] for i in idx]
    mats = tuple(jnp.where(strict, mm_nt(kbs[i], ks[i]) * gams[i], 0.0) for i in idx)
    aqks = [mm_nt(qs[i], ks[i]) * gams[i] for i in idx]
    t_invs = _unit_lower_inverses(mats)
    us = [mm_nn(t_invs[i], vs[i] * betas[i]) for i in idx]
    ws = [mm_nn(t_invs[i], kbs[i] * jnp.exp(gcs[i])) for i in idx]
    g_lasts = [gcs[i][LC - 1:LC, :] for i in idx]
    q_decs = [qs[i] * jnp.exp(gcs[i]) for i in idx]
    k_decs = [ks[i] * jnp.exp(g_lasts[i] - gcs[i]) for i in idx]
    e_lasts = [jnp.exp(g_lasts[i]) for i in idx]
    s_hs = [st[sl, :] for sl in sls]
    os_ = [None] * len(pairs)
    for c in range(n_ch):
        ids = [c * NH + h for h in range(NH)]
        v_news = [us[i] - mm_nn(ws[i], s_hs[h]) for h, i in enumerate(ids)]
        for h, i in enumerate(ids):
            os_[i] = mm_nn(q_decs[i], s_hs[h]) + mm_nn(aqks[i], v_news[h])
        s_hs = [s_hs[h] * e_lasts[i] + mm_tn(k_decs[i], v_news[h]) for h, i in enumerate(ids)]
    ys = [os_[i] * lax.rsqrt(jnp.mean(os_[i] * os_[i], axis=-1, keepdims=True) + EPS) * norm_w
          * jax.nn.silu(z[rows[c], sls[h]]) for i, (c, h) in enumerate(pairs)]
    y = jnp.concatenate([jnp.concatenate(ys[c * NH:(c + 1) * NH], axis=1) for c in range(n_ch)], axis=0)
    return y, jnp.concatenate(s_hs, axis=0)


def rg_pre(xh8, x, conv_w, conv_b, gw_r, gw_i, gate_b, lam):
    xc = causal_conv(xh8, x, conv_w) + conv_b
    r = jax.nn.sigmoid(mm_nn(xc, gw_r) + gate_b[0:1])
    i = jax.nn.sigmoid(mm_nn(xc, gw_i) + gate_b[1:2])
    log_a = -RG_C * r * jax.nn.softplus(-lam)
    a = jnp.exp(log_a)
    th = jnp.tanh(log_a)
    one_minus_a2 = -2.0 * th / (1.0 - th)
    b = jnp.sqrt(one_minus_a2) * (i * xc)
    return a, b


def _scan_rows(a_ref, b_ref, out_ref, h0, n_rows, reverse):
    n_groups = n_rows // 8
    width = a_ref.shape[1]
    row = lax.broadcasted_iota(jnp.int32, (8, width), 0)

    def body(j, h):
        g = (n_groups - 1 - j) if reverse else j
        r0 = pl.multiple_of(g * 8, 8)
        av = a_ref[pl.ds(r0, 8), :]
        bv = b_ref[pl.ds(r0, 8), :]
        for d in (1, 2, 4):
            sh = (8 - d) if reverse else d
            a_s = pltpu.roll(av, sh, 0)
            b_s = pltpu.roll(bv, sh, 0)
            valid = (row < 8 - d) if reverse else (row >= d)
            bv = jnp.where(valid, av * b_s + bv, bv)
            av = jnp.where(valid, av * a_s, av)
        hv = av * h + bv
        out_ref[pl.ds(r0, 8), :] = hv
        return hv[0:1, :] if reverse else hv[7:8, :]

    return lax.fori_loop(0, n_groups, body, h0)


def _norm_rows(xv):
    r = lax.rsqrt(jnp.mean(xv * xv, axis=-1, keepdims=True) + EPS)
    return xv * r, r


def inproj(x, nw, w):
    s, ts = x.shape[0], 256

    def body(x_ref, nw_ref, w_ref, o_ref):
        xn, _ = _norm_rows(x_ref[...])
        hn = xn * nw_ref[...]
        o_ref[...] = jnp.dot(hn.astype(bf16), w_ref[...], preferred_element_type=f32)

    return pl.pallas_call(
        body, grid=(s // ts,),
        in_specs=[pl.BlockSpec((ts, D), lambda i: (i, 0)), pl.BlockSpec((1, D), lambda i: (0, 0)),
                  pl.BlockSpec((D, DP), lambda i: (0, 0))],
        out_specs=pl.BlockSpec((ts, DP), lambda i: (i, 0)),
        out_shape=jax.ShapeDtypeStruct((s, DP), f32), name="inproj", compiler_params=_cp(56))(x, nw, w)


def outproj(x, yr, ym, yg, wo):
    s, ts = x.shape[0], 256

    def body(x_ref, yr_ref, ym_ref, yg_ref, wo_ref, o_ref):
        acc = x_ref[...]
        for j, y_ref in enumerate((yr_ref, ym_ref, yg_ref)):
            acc = acc + jnp.dot(y_ref[...].astype(bf16), wo_ref[j * SEG:(j + 1) * SEG, :], preferred_element_type=f32)
        o_ref[...] = acc

    yspec = pl.BlockSpec((ts, SEG), lambda i: (i, 0))
    return pl.pallas_call(
        body, grid=(s // ts,),
        in_specs=[pl.BlockSpec((ts, D), lambda i: (i, 0)), yspec, yspec, yspec,
                  pl.BlockSpec((3 * SEG, D), lambda i: (0, 0))],
        out_specs=pl.BlockSpec((ts, D), lambda i: (i, 0)),
        out_shape=jax.ShapeDtypeStruct((s, D), f32), name="outproj", compiler_params=_cp(40))(x, yr, ym, yg, wo)


def head(x, fw, tgt):
    s, ts = x.shape[0], 256

    def body(x_ref, fw_ref, t_ref, dx_ref, loss_ref, dfw_ref):
        @pl.when(pl.program_id(0) == 0)
        def _():
            loss_ref[...] = jnp.zeros_like(loss_ref)
            dfw_ref[...] = jnp.zeros_like(dfw_ref)

        xn, r = _norm_rows(x_ref[...])
        fw_v = fw_ref[...]
        err = xn * fw_v - t_ref[...]
        loss_ref[...] += 0.5 * jnp.sum(jnp.mean(err * err, axis=-1, keepdims=True))
        dy = err * (1.0 / D)
        dfw_ref[0:1, :] += jnp.sum(dy * xn, axis=0, keepdims=True)
        dxn = dy * fw_v
        dx_ref[...] = r * (dxn - xn * jnp.mean(dxn * xn, axis=-1, keepdims=True))

    tile = pl.BlockSpec((ts, D), lambda i: (i, 0))
    return pl.pallas_call(
        body, grid=(s // ts,),
        in_specs=[tile, pl.BlockSpec((1, D), lambda i: (0, 0)), tile],
        out_specs=[tile, pl.BlockSpec((8, 128), lambda i: (0, 0)), pl.BlockSpec((8, D), lambda i: (0, 0))],
        out_shape=[jax.ShapeDtypeStruct((s, D), f32), jax.ShapeDtypeStruct((8, 128), f32),
                   jax.ShapeDtypeStruct((8, D), f32)], name="head")(x, fw, tgt)


def outproj_bwd(dx, yr, ym, yg, wo):
    s, ts = dx.shape[0], 256

    def body(dx_ref, yr_ref, ym_ref, yg_ref, wo_ref, dyr_ref, dym_ref, dyg_ref, gwo_ref):
        @pl.when(pl.program_id(0) == 0)
        def _():
            gwo_ref[...] = jnp.zeros_like(gwo_ref)

        dxb = dx_ref[...].astype(bf16)
        for j, (y_ref, dy_ref) in enumerate(((yr_ref, dyr_ref), (ym_ref, dym_ref), (yg_ref, dyg_ref))):
            rows = slice(j * SEG, (j + 1) * SEG)
            dy_ref[...] = lax.dot_general(dxb, wo_ref[rows, :], (((1,), (1,)), ((), ())), preferred_element_type=f32)
            gwo_ref[rows, :] += lax.dot_general(y_ref[...].astype(bf16), dxb, (((0,), (0,)), ((), ())),
                                                preferred_element_type=f32)

    yspec = pl.BlockSpec((ts, SEG), lambda i: (i, 0))
    wspec = pl.BlockSpec((3 * SEG, D), lambda i: (0, 0))
    ysh = jax.ShapeDtypeStruct((s, SEG), f32)
    return pl.pallas_call(
        body, grid=(s // ts,),
        in_specs=[pl.BlockSpec((ts, D), lambda i: (i, 0)), yspec, yspec, yspec, wspec],
        out_specs=[yspec, yspec, yspec, wspec],
        out_shape=[ysh, ysh, ysh, jax.ShapeDtypeStruct((3 * SEG, D), f32)],
        name="outproj_bwd", compiler_params=_cp(48))(dx, yr, ym, yg, wo)


def inproj_bwd_x(x, nw, w, dxo, d_rg, d_ml, d_gd, d_sa, d_sb):
    s, ts = x.shape[0], 256
    widths = (d_rg.shape[1], d_ml.shape[1], d_gd.shape[1], HD)

    def body(x_ref, nw_ref, w_ref, dxo_ref, rg_ref, ml_ref, gd_ref, sa_ref, sb_ref, dx_ref, dnw_ref):
        @pl.when(pl.program_id(0) == 0)
        def _():
            dnw_ref[...] = jnp.zeros_like(dnw_ref)

        xn, r = _norm_rows(x_ref[...])
        pieces = (rg_ref[...], ml_ref[...], gd_ref[...], sa_ref[...] + sb_ref[...])
        dhn = jnp.zeros((ts, D), f32)
        c0 = 0
        for piece, wd in zip(pieces, widths):
            dhn = dhn + lax.dot_general(piece.astype(bf16), w_ref[:, c0:c0 + wd], (((1,), (1,)), ((), ())),
                                        preferred_element_type=f32)
            c0 += wd
        dnw_ref[0:1, :] += jnp.sum(dhn * xn, axis=0, keepdims=True)
        dxn = dhn * nw_ref[...]
        dx_ref[...] = dxo_ref[...] + r * (dxn - xn * jnp.mean(dxn * xn, axis=-1, keepdims=True))

    tile = pl.BlockSpec((ts, D), lambda i: (i, 0))
    return pl.pallas_call(
        body, grid=(s // ts,),
        in_specs=[tile, pl.BlockSpec((1, D), lambda i: (0, 0)), pl.BlockSpec((D, DP), lambda i: (0, 0)), tile]
        + [pl.BlockSpec((ts, wd), lambda i: (i, 0)) for wd in widths] + [pl.BlockSpec((ts, HD), lambda i: (i, 0))],
        out_specs=[tile, pl.BlockSpec((8, D), lambda i: (0, 0))],
        out_shape=[jax.ShapeDtypeStruct((s, D), f32), jax.ShapeDtypeStruct((8, D), f32)],
        name="inproj_bwd_x", compiler_params=_cp(56))(x, nw, w, dxo, d_rg, d_ml, d_gd, d_sa, d_sb)


def wgrad(x, nw, dps, name):
    s, ts = x.shape[0], 512
    c = dps[0].shape[1]
    ct = min(c, SEG)
    n_dp = len(dps)

    def body(*refs):
        x_ref, nw_ref = refs[0], refs[1]
        dp_refs = refs[2:2 + n_dp]
        o_ref = refs[2 + n_dp]

        @pl.when(pl.program_id(1) == 0)
        def _():
            o_ref[...] = jnp.zeros_like(o_ref)

        xn, _ = _norm_rows(x_ref[...])
        hn = (xn * nw_ref[...]).astype(bf16)
        dp = dp_refs[0][...]
        for extra in dp_refs[1:]:
            dp = dp + extra[...]
        o_ref[...] += lax.dot_general(hn, dp.astype(bf16), (((0,), (0,)), ((), ())), preferred_element_type=f32)

    return pl.pallas_call(
        body, grid=(c // ct, s // ts),
        in_specs=[pl.BlockSpec((ts, D), lambda j, i: (i, 0)), pl.BlockSpec((1, D), lambda j, i: (0, 0))]
        + [pl.BlockSpec((ts, ct), lambda j, i: (i, j)) for _ in dps],
        out_specs=pl.BlockSpec((D, ct), lambda j, i: (0, j)),
        out_shape=jax.ShapeDtypeStruct((D, c), f32), name=name, compiler_params=_cp(40))(x, nw, *dps)


def _seg_spec(rows, seg, n_tiles=None):
    if n_tiles is None:
        return pl.BlockSpec((rows, SEG), lambda i: (i, seg))
    return pl.BlockSpec((rows, SEG), lambda i: (n_tiles - 1 - i, seg))


def _halo_spec(rows, seg, n_tiles=None):
    per = rows // 8
    if n_tiles is None:
        return pl.BlockSpec((8, SEG), lambda i: (jnp.maximum(i * per - 1, 0), seg))
    return pl.BlockSpec((8, SEG), lambda i: (jnp.maximum((n_tiles - 1 - i) * per - 1, 0), seg))


def _const_spec(shape):
    return pl.BlockSpec(shape, lambda i: tuple(0 for _ in shape))


def rglru_fwd(proj, conv_w, conv_b, gw_r, gw_i, gate_b, lam):
    s = proj.shape[0]
    tr = RG_TILE

    def body(xh_ref, x_ref, z_ref, cw_ref, cb_ref, gr_ref, gi_ref, gb_ref, lam_ref, y_ref, h_ref, a_s, b_s, hc):
        first = pl.program_id(0) == 0

        @pl.when(first)
        def _():
            hc[...] = jnp.zeros_like(hc)

        xh = jnp.where(first, 0.0, xh_ref[...])
        a, b = rg_pre(xh, x_ref[...], cw_ref[...], cb_ref[...], gr_ref[...], gi_ref[...], gb_ref[...], lam_ref[...])
        a_s[...] = a
        b_s[...] = b
        hc[0:1, :] = _scan_rows(a_s, b_s, h_ref, hc[0:1, :], tr, False)
        y_ref[...] = h_ref[...] * jax.nn.silu(z_ref[...])

    out = pl.BlockSpec((tr, SEG), lambda i: (i, 0))
    return pl.pallas_call(
        body, grid=(s // tr,),
        in_specs=[_halo_spec(tr, 0), _seg_spec(tr, 0), _seg_spec(tr, 1), _const_spec((4, SEG)), _const_spec((1, SEG)),
                  _const_spec((SEG, SEG)), _const_spec((SEG, SEG)), _const_spec((2, SEG)), _const_spec((1, SEG))],
        out_specs=[out, out],
        out_shape=[jax.ShapeDtypeStruct((s, SEG), f32), jax.ShapeDtypeStruct((s, SEG), f32)],
        scratch_shapes=[pltpu.VMEM((tr, SEG), f32), pltpu.VMEM((tr, SEG), f32), pltpu.VMEM((8, SEG), f32)],
        name="rglru_fwd", compiler_params=_cp(40))(proj, proj, proj, conv_w, conv_b, gw_r, gw_i, gate_b, lam)


def rglru_bwd(proj, hs, dy, conv_w, conv_b, gw_r, gw_i, gate_b, lam):
    s = proj.shape[0]
    tr = RG_TILE
    nt = s // tr

    def body(xh_ref, x_ref, z_ref, hh_ref, h_ref, dy_ref, cw_ref, cb_ref, gr_ref, gi_ref, gb_ref, lam_ref,
             dp_ref, dcw_ref, dcb_ref, dgr_ref, dgi_ref, dgb_ref, dlam_ref,
             an_s, g_s, dh_s, a_first, dh_first, dhalo):
        i = pl.program_id(0)
        first_tile = i == nt - 1

        @pl.when(i == 0)
        def _():
            for ref in (dcw_ref, dcb_ref, dgr_ref, dgi_ref, dgb_ref, dlam_ref, a_first, dh_first, dhalo):
                ref[...] = jnp.zeros_like(ref)

        xh = jnp.where(first_tile, 0.0, xh_ref[...])
        params = (cw_ref[...], cb_ref[...], gr_ref[...], gi_ref[...], gb_ref[...], lam_ref[...])
        (a, _), vjp = jax.vjp(rg_pre, xh, x_ref[...], *params)
        zv = z_ref[...]
        hv = h_ref[...]
        dyv = dy_ref[...]
        sig = jax.nn.sigmoid(zv)
        g_s[...] = dyv * (zv * sig)
        dp_ref[:, SEG:2 * SEG] = dyv * hv * (sig * (1.0 + zv * (1.0 - sig)))
        ridx = lax.broadcasted_iota(jnp.int32, (tr, SEG), 0)
        an_s[...] = jnp.where(ridx == tr - 1, jnp.broadcast_to(a_first[0:1, :], (tr, SEG)), pltpu.roll(a, tr - 1, 0))
        _scan_rows(an_s, g_s, dh_s, dh_first[0:1, :], tr, True)
        dh = dh_s[...]
        h_prev_last = jnp.where(first_tile, 0.0, hh_ref[...])[7:8, :]
        h_prev = pltpu.roll(hv, 1, 0)
        h_prev = jnp.where(ridx == 0, jnp.broadcast_to(h_prev_last, (tr, SEG)), h_prev)
        dxh, dx, dcw, dcb, dgr, dgi, dgb, dlam = vjp((dh * h_prev, dh))
        dp_ref[:, 0:SEG] = dx
        dp_ref[tr - 8:tr, 0:SEG] += dhalo[...]
        dhalo[...] = dxh
        a_first[0:1, :] = a[0:1, :]
        dh_first[0:1, :] = dh[0:1, :]
        dcw_ref[...] += dcw
        dcb_ref[...] += dcb
        dgr_ref[...] += dgr
        dgi_ref[...] += dgi
        dgb_ref[...] += dgb
        dlam_ref[...] += dlam

    pspecs = [_const_spec((4, SEG)), _const_spec((1, SEG)), _const_spec((SEG, SEG)), _const_spec((SEG, SEG)),
              _const_spec((2, SEG)), _const_spec((1, SEG))]
    pshapes = [jax.ShapeDtypeStruct(sh, f32) for sh in ((4, SEG), (1, SEG), (SEG, SEG), (SEG, SEG), (2, SEG), (1, SEG))]
    tile = pl.BlockSpec((tr, SEG), lambda i: (nt - 1 - i, 0))
    return pl.pallas_call(
        body, grid=(nt,),
        in_specs=[_halo_spec(tr, 0, nt), _seg_spec(tr, 0, nt), _seg_spec(tr, 1, nt),
                  pl.BlockSpec((8, SEG), lambda i: (jnp.maximum((nt - 1 - i) * (tr // 8) - 1, 0), 0)), tile, tile] + pspecs,
        out_specs=[pl.BlockSpec((tr, 2 * SEG), lambda i: (nt - 1 - i, 0))] + pspecs,
        out_shape=[jax.ShapeDtypeStruct((s, 2 * SEG), f32)] + pshapes,
        scratch_shapes=[pltpu.VMEM((tr, SEG), f32), pltpu.VMEM((tr, SEG), f32), pltpu.VMEM((tr, SEG), f32),
                        pltpu.VMEM((8, SEG), f32), pltpu.VMEM((8, SEG), f32), pltpu.VMEM((8, SEG), f32)],
        name="rglru_bwd", compiler_params=_cp(48))(proj, proj, proj, hs, hs, dy, conv_w, conv_b, gw_r, gw_i, gate_b, lam)


ML_SEGS = (2, 3, 4, 5, 6)
SMALL_BLK = SMALL0 // HD


def mlstm_fwd(proj, bias_row, norm_w):
    s = proj.shape[0]
    nc = s // LR

    def body(q_ref, k_ref, v_ref, o_ref, z_ref, sm_ref, b_ref, nw_ref, y_ref, cs_ref, ns_ref, ms_ref, c_s, n_s, m_s):
        @pl.when(pl.program_id(0) == 0)
        def _():
            c_s[...] = jnp.zeros_like(c_s)
            n_s[...] = jnp.zeros_like(n_s)
            m_s[...] = jnp.zeros_like(m_s)

        cs_ref[0] = c_s[...]
        ns_ref[0] = n_s[...]
        ms_ref[0] = m_s[...]
        y, c2, n2, m2 = ml_chunk(q_ref[...], k_ref[...], v_ref[...], o_ref[...], z_ref[...], sm_ref[...],
                                 b_ref[...], nw_ref[...], c_s[...], n_s[...], m_s[...])
        y_ref[...] = y
        c_s[...] = c2
        n_s[...] = n2
        m_s[...] = m2

    return pl.pallas_call(
        body, grid=(nc,),
        in_specs=[_seg_spec(LR, sg) for sg in ML_SEGS]
        + [pl.BlockSpec((LR, HD), lambda i: (i, SMALL_BLK)), _const_spec((1, HD)), _const_spec((1, SEG))],
        out_specs=[pl.BlockSpec((LR, SEG), lambda i: (i, 0)), pl.BlockSpec((1, SEG, HD), lambda i: (i, 0, 0)),
                   pl.BlockSpec((1, 8, HD), lambda i: (i, 0, 0)), pl.BlockSpec((1, 8, HD), lambda i: (i, 0, 0))],
        out_shape=[jax.ShapeDtypeStruct((s, SEG), f32), jax.ShapeDtypeStruct((nc, SEG, HD), f32),
                   jax.ShapeDtypeStruct((nc, 8, HD), f32), jax.ShapeDtypeStruct((nc, 8, HD), f32)],
        scratch_shapes=[pltpu.VMEM((SEG, HD), f32), pltpu.VMEM((8, HD), f32), pltpu.VMEM((8, HD), f32)],
        name="mlstm_fwd")(proj, proj, proj, proj, proj, proj, bias_row, norm_w)


def mlstm_bwd(proj, dy, cs, ns, ms, bias_row, norm_w):
    s = proj.shape[0]
    nc = s // LR

    def body(q_ref, k_ref, v_ref, o_ref, z_ref, sm_ref, dy_ref, cs_ref, ns_ref, ms_ref, b_ref, nw_ref,
             dp_ref, dsm_ref, db_ref, dnw_ref, dc_s, dn_s, dm_s):
        @pl.when(pl.program_id(0) == 0)
        def _():
            for ref in (db_ref, dnw_ref, dc_s, dn_s, dm_s):
                ref[...] = jnp.zeros_like(ref)

        _, vjp = jax.vjp(ml_chunk, q_ref[...], k_ref[...], v_ref[...], o_ref[...], z_ref[...], sm_ref[...],
                         b_ref[...], nw_ref[...], cs_ref[0], ns_ref[0], ms_ref[0])
        dq, dk, dv, do, dz, dsm, db, dnw, dc, dn, dm = vjp((dy_ref[...], dc_s[...], dn_s[...], dm_s[...]))
        for j, val in enumerate((dq, dk, dv, do, dz)):
            dp_ref[:, j * SEG:(j + 1) * SEG] = val
        dsm_ref[...] = dsm
        db_ref[0:1, :] += db
        dnw_ref[0:1, :] += dnw
        dc_s[...] = dc
        dn_s[...] = dn
        dm_s[...] = dm

    rev3 = lambda i: (nc - 1 - i, 0, 0)
    return pl.pallas_call(
        body, grid=(nc,),
        in_specs=[_seg_spec(LR, sg, nc) for sg in ML_SEGS]
        + [pl.BlockSpec((LR, HD), lambda i: (nc - 1 - i, SMALL_BLK)), pl.BlockSpec((LR, SEG), lambda i: (nc - 1 - i, 0)),
           pl.BlockSpec((1, SEG, HD), rev3), pl.BlockSpec((1, 8, HD), rev3), pl.BlockSpec((1, 8, HD), rev3),
           _const_spec((1, HD)), _const_spec((1, SEG))],
        out_specs=[pl.BlockSpec((LR, 5 * SEG), lambda i: (nc - 1 - i, 0)), pl.BlockSpec((LR, HD), lambda i: (nc - 1 - i, 0)),
                   _const_spec((8, HD)), _const_spec((8, SEG))],
        out_shape=[jax.ShapeDtypeStruct((s, 5 * SEG), f32), jax.ShapeDtypeStruct((s, HD), f32),
                   jax.ShapeDtypeStruct((8, HD), f32), jax.ShapeDtypeStruct((8, SEG), f32)],
        scratch_shapes=[pltpu.VMEM((SEG, HD), f32), pltpu.VMEM((8, HD), f32), pltpu.VMEM((8, HD), f32)],
        name="mlstm_bwd", compiler_params=_cp(48))(proj, proj, proj, proj, proj, proj, dy, cs, ns, ms, bias_row, norm_w)


GD_SEGS = (7, 8, 9)


def _carried(xf, n_in, n_out):
    if xf is None:
        return [], [], [], [], {}, [], 0, 0
    operands, in_specs, out_shapes, out_specs, aliases, sems = _xfer_plumb(xf, n_in, n_out)
    return operands, in_specs, out_shapes, out_specs, aliases, sems, len(operands), len(xf.lands)


def gdn_fwd(proj, conv_w, alog_row, dt_row, norm_w, xf=None):
    s = proj.shape[0]
    nc = s // LR
    n_in, n_out = 12, 2
    x_ops, x_in_specs, x_out_shapes, x_out_specs, aliases, x_sems, n_xin, n_land = _carried(xf, n_in, n_out)

    def body(*refs):
        qh_ref, q_ref, kh_ref, k_ref, vh_ref, v_ref, z_ref, sm_ref, cw_ref, al_ref, dt_ref, nw_ref = refs[:n_in]
        x_refs = refs[n_in:n_in + n_xin]
        y_ref, ss_ref = refs[n_in + n_xin:n_in + n_xin + n_out]
        land_refs = refs[n_in + n_xin + n_out:n_in + n_xin + n_out + n_land]
        st_s = refs[n_in + n_xin + n_out + n_land]
        sem_refs = refs[n_in + n_xin + n_out + n_land + 1:]
        i = pl.program_id(0)
        first = i == 0

        @pl.when(first)
        def _():
            st_s[...] = jnp.zeros_like(st_s)
            if xf is not None:
                _xfer_start(xf, x_refs[:len(xf.srcs)], land_refs, sem_refs)

        ss_ref[0] = st_s[...]
        halo = [jnp.where(first, 0.0, r[...]) for r in (qh_ref, kh_ref, vh_ref)]
        y, st2 = gd_chunk(halo[0], q_ref[...], halo[1], k_ref[...], halo[2], v_ref[...], z_ref[...], sm_ref[...],
                          cw_ref[...], al_ref[...], dt_ref[...], nw_ref[...], st_s[...])
        y_ref[...] = y
        st_s[...] = st2

        if xf is not None:
            @pl.when(i == nc - 1)
            def _():
                _xfer_wait(xf, x_refs[:len(xf.srcs)], land_refs, sem_refs)

    qkv_specs = []
    for sg in GD_SEGS:
        qkv_specs += [_halo_spec(LR, sg), _seg_spec(LR, sg)]
    return pl.pallas_call(
        body, grid=(nc,),
        in_specs=qkv_specs + [_seg_spec(LR, 10), pl.BlockSpec((LR, HD), lambda i: (i, SMALL_BLK)),
                              _const_spec((4, 3 * SEG)), _const_spec((1, HD)), _const_spec((1, HD)), _const_spec((1, HD))]
        + x_in_specs,
        out_specs=[pl.BlockSpec((LR, SEG), lambda i: (i, 0)), pl.BlockSpec((1, SEG, HD), lambda i: (i, 0, 0))] + x_out_specs,
        out_shape=[jax.ShapeDtypeStruct((s, SEG), f32), jax.ShapeDtypeStruct((nc, SEG, HD), f32)] + x_out_shapes,
        scratch_shapes=[pltpu.VMEM((SEG, HD), f32)] + x_sems, input_output_aliases=aliases,
        name="gdn_fwd" if xf is None else "gdn_fwd_carrying")(
            proj, proj, proj, proj, proj, proj, proj, proj, conv_w, alog_row, dt_row, norm_w, *x_ops)


def gdn_bwd(proj, dy, ss, conv_w, alog_row, dt_row, norm_w, xf=None):
    s = proj.shape[0]
    nc = s // LR
    n_in, n_out = 14, 6
    x_ops, x_in_specs, x_out_shapes, x_out_specs, aliases, x_sems, n_xin, n_land = _carried(xf, n_in, n_out)

    def body(*refs):
        (qh_ref, q_ref, kh_ref, k_ref, vh_ref, v_ref, z_ref, sm_ref, dy_ref, ss_ref,
         cw_ref, al_ref, dt_ref, nw_ref) = refs[:n_in]
        x_refs = refs[n_in:n_in + n_xin]
        dp_ref, dsm_ref, dcw_ref, dal_ref, ddt_ref, dnw_ref = refs[n_in + n_xin:n_in + n_xin + n_out]
        land_refs = refs[n_in + n_xin + n_out:n_in + n_xin + n_out + n_land]
        dst_s, dhalo = refs[n_in + n_xin + n_out + n_land:n_in + n_xin + n_out + n_land + 2]
        sem_refs = refs[n_in + n_xin + n_out + n_land + 2:]
        i = pl.program_id(0)
        first_chunk = i == nc - 1

        @pl.when(i == 0)
        def _():
            for ref in (dcw_ref, dal_ref, ddt_ref, dnw_ref, dst_s, dhalo):
                ref[...] = jnp.zeros_like(ref)
            if xf is not None:
                _xfer_start(xf, x_refs[:len(xf.srcs)], land_refs, sem_refs)

        halo = [jnp.where(first_chunk, 0.0, r[...]) for r in (qh_ref, kh_ref, vh_ref)]
        _, vjp = jax.vjp(gd_chunk, halo[0], q_ref[...], halo[1], k_ref[...], halo[2], v_ref[...], z_ref[...], sm_ref[...],
                         cw_ref[...], al_ref[...], dt_ref[...], nw_ref[...], ss_ref[0])
        dqh, dq, dkh, dk, dvh, dv, dz, dsm, dcw, dal, ddt, dnw, dst = vjp((dy_ref[...], dst_s[...]))
        for j, val in enumerate((dq, dk, dv, dz)):
            dp_ref[:, j * SEG:(j + 1) * SEG] = val
        dp_ref[LR - 8:LR, 0:3 * SEG] += dhalo[...]
        for j, val in enumerate((dqh, dkh, dvh)):
            dhalo[:, j * SEG:(j + 1) * SEG] = val
        dsm_ref[...] = dsm
        dcw_ref[...] += dcw
        dal_ref[0:1, :] += dal
        ddt_ref[0:1, :] += ddt
        dnw_ref[0:1, :] += dnw
        dst_s[...] = dst

        if xf is not None:
            @pl.when(first_chunk)
            def _():
                _xfer_wait(xf, x_refs[:len(xf.srcs)], land_refs, sem_refs)

    qkv_specs = []
    for sg in GD_SEGS:
        qkv_specs += [_halo_spec(LR, sg, nc), _seg_spec(LR, sg, nc)]
    return pl.pallas_call(
        body, grid=(nc,),
        in_specs=qkv_specs + [_seg_spec(LR, 10, nc), pl.BlockSpec((LR, HD), lambda i: (nc - 1 - i, SMALL_BLK)),
                              pl.BlockSpec((LR, SEG), lambda i: (nc - 1 - i, 0)),
                              pl.BlockSpec((1, SEG, HD), lambda i: (nc - 1 - i, 0, 0)),
                              _const_spec((4, 3 * SEG)), _const_spec((1, HD)), _const_spec((1, HD)), _const_spec((1, HD))]
        + x_in_specs,
        out_specs=[pl.BlockSpec((LR, 4 * SEG), lambda i: (nc - 1 - i, 0)), pl.BlockSpec((LR, HD), lambda i: (nc - 1 - i, 0)),
                   _const_spec((4, 3 * SEG)), _const_spec((8, HD)), _const_spec((8, HD)), _const_spec((8, HD))] + x_out_specs,
        out_shape=[jax.ShapeDtypeStruct((s, 4 * SEG), f32), jax.ShapeDtypeStruct((s, HD), f32),
                   jax.ShapeDtypeStruct((4, 3 * SEG), f32), jax.ShapeDtypeStruct((8, HD), f32),
                   jax.ShapeDtypeStruct((8, HD), f32), jax.ShapeDtypeStruct((8, HD), f32)] + x_out_shapes,
        scratch_shapes=[pltpu.VMEM((SEG, HD), f32), pltpu.VMEM((8, 3 * SEG), f32)] + x_sems, input_output_aliases=aliases,
        name="gdn_bwd" if xf is None else "gdn_bwd_carrying", compiler_params=_cp(48))(
            proj, proj, proj, proj, proj, proj, proj, proj, dy, ss, conv_w, alog_row, dt_row, norm_w, *x_ops)


def _my_place():
    return lax.axis_index("x"), lax.axis_index("y"), lax.axis_index("c")


def _slot(p):
    return 4 * p[0] + 2 * p[1] + p[2]


def _peer(me, j):
    flips = ((j >> 2) & 1, (j >> 1) & 1, j & 1)
    return tuple((1 - v) if fl else v for v, fl in zip(me, flips))


_ANY = pl.BlockSpec(memory_space=pl.ANY)


class Xfer(NamedTuple):
    srcs: list
    kinds: list
    lands: list
    land_of: list
    layer: list


def _xfer_plumb(xf, n_in, n_out):
    n_src = len(xf.srcs)
    passed = [ld for ld in xf.lands if not isinstance(ld, jax.ShapeDtypeStruct)]
    aliases, k = {}, 0
    for li, ld in enumerate(xf.lands):
        if not isinstance(ld, jax.ShapeDtypeStruct):
            aliases[n_in + n_src + k] = n_out + li
            k += 1
    operands = list(xf.srcs) + passed
    out_shapes = [jax.ShapeDtypeStruct(ld.shape, ld.dtype) for ld in xf.lands]
    sems = [pltpu.SemaphoreType.DMA((n_src, N_DEV - 1)), pltpu.SemaphoreType.DMA((n_src, N_DEV - 1)),
            pltpu.SemaphoreType.DMA((n_src,))]
    return operands, [_ANY] * len(operands), out_shapes, [_ANY] * len(xf.lands), aliases, sems


def _xfer_copies(xf, src_refs, land_refs, send_sems, recv_sems, local_sems):
    me = _my_place()
    mine = _slot(me)
    local, remote = [], []
    for a, src_ref in enumerate(src_refs):
        land = land_refs[xf.land_of[a]]
        dst = land.at[mine] if xf.layer[a] is None else land.at[mine, xf.layer[a]]
        by_slot = xf.kinds[a] == "slot"
        local.append(pltpu.make_async_copy(src_ref.at[mine] if by_slot else src_ref, dst, local_sems.at[a]))
        for j in range(1, N_DEV):
            peer = _peer(me, j)
            remote.append(pltpu.make_async_remote_copy(
                src_ref=src_ref.at[_slot(peer)] if by_slot else src_ref, dst_ref=dst, send_sem=send_sems.at[a, j - 1],
                recv_sem=recv_sems.at[a, j - 1], device_id=peer, device_id_type=MESH))
    return local, remote


def _xfer_start(xf, src_refs, land_refs, sems):
    local, remote = _xfer_copies(xf, src_refs, land_refs, *sems)
    for cp in local + remote:
        cp.start()


def _xfer_wait(xf, src_refs, land_refs, sems):
    local, remote = _xfer_copies(xf, src_refs, land_refs, *sems)
    for cp in remote:
        cp.wait()
    for cp in local:
        cp.wait()


def exchange(xf, name):
    n_src, n_land = len(xf.srcs), len(xf.lands)
    operands, in_specs, out_shapes, out_specs, aliases, sems = _xfer_plumb(xf, 0, 0)

    def body(*refs):
        src_refs = refs[:n_src]
        land_refs = refs[len(operands):len(operands) + n_land]
        sem_refs = refs[len(operands) + n_land:]
        _xfer_start(xf, src_refs, land_refs, sem_refs)
        _xfer_wait(xf, src_refs, land_refs, sem_refs)

    return pl.pallas_call(body, in_specs=in_specs, out_specs=out_specs, out_shape=out_shapes, scratch_shapes=sems,
                          input_output_aliases=aliases, name=name)(*operands)


def _adamw(w, g, m, v):
    m = ADAM_B1 * m + (1.0 - ADAM_B1) * g
    v = ADAM_B2 * v + (1.0 - ADAM_B2) * (g * g)
    m_hat = m / (1.0 - ADAM_B1 ** ADAM_STEP)
    v_hat = v / (1.0 - ADAM_B2 ** ADAM_STEP)
    delta = -ADAM_LR * (m_hat / (jnp.sqrt(v_hat) + ADAM_EPS) + ADAM_WD * w)
    return delta, m, v


def adam_slots(slots, w, m, v, rows, name):
    _, depth, r, c = slots.shape

    def body(s_ref, w_ref, m_ref, v_ref, g_ref, d_ref, m2_ref, v2_ref):
        g = s_ref[0, 0].astype(f32)
        for k in range(1, N_DEV):
            g = g + s_ref[k, 0].astype(f32)
        d, m2, v2 = _adamw(w_ref[0], g, m_ref[0], v_ref[0])
        g_ref[0] = g
        d_ref[0] = d
        m2_ref[0] = m2
        v2_ref[0] = v2

    blk = pl.BlockSpec((1, rows, c), lambda l, i: (l, i, 0))
    sh = jax.ShapeDtypeStruct((depth, r, c), f32)
    return pl.pallas_call(
        body, grid=(depth, r // rows),
        in_specs=[pl.BlockSpec((N_DEV, 1, rows, c), lambda l, i: (0, l, i, 0)), blk, blk, blk],
        out_specs=[blk] * 4, out_shape=[sh] * 4, name=name, compiler_params=_cp(40))(slots, w, m, v)


def sum_slots(slots, name):
    _, r, c = slots.shape

    def body(s_ref, o_ref):
        g = s_ref[0]
        for k in range(1, N_DEV):
            g = g + s_ref[k]
        o_ref[...] = g

    return pl.pallas_call(body, out_shape=jax.ShapeDtypeStruct((r, c), f32), name=name, compiler_params=_cp(40))(slots)


def adam_flat(g, w, m, v, name):
    def body(g_ref, w_ref, m_ref, v_ref, d_ref, m2_ref, v2_ref):
        d, m2, v2 = _adamw(w_ref[...], g_ref[...], m_ref[...], v_ref[...])
        d_ref[...] = d
        m2_ref[...] = m2
        v2_ref[...] = v2

    sh = jax.ShapeDtypeStruct(g.shape, f32)
    return pl.pallas_call(body, out_shape=[sh] * 3, name=name)(g, w, m, v)


def _rows_of(shape):
    n = 1
    for dim in shape:
        n *= dim
    return n, -(-n // (8 * 128)) * 8


def _pack(arrs):
    parts = []
    for a in arrs:
        n, rows = _rows_of(a.shape)
        parts.append(jnp.pad(a.reshape(-1).astype(f32), (0, rows * 128 - n)).reshape(rows, 128))
    return jnp.concatenate(parts, axis=0)


def _unpack(packed, shapes):
    out, row = [], 0
    for sh in shapes:
        n, rows = _rows_of(sh)
        out.append(packed[row:row + rows].reshape(-1)[:n].reshape(sh))
        row += rows
    return out


def _block_diag(gw):
    eye = jnp.eye(8, dtype=gw.dtype)
    return (gw[:, :, None, :] * eye[:, None, :, None]).reshape(SEG, SEG)


def _diag_blocks(dense):
    eye = jnp.eye(8, dtype=dense.dtype)
    return (dense.reshape(8, 64, 8, 64) * eye[:, None, :, None]).sum(axis=2)


def _lane_row(vals, first_lane):
    return jnp.zeros((1, HD), f32).at[0, first_lane:first_lane + NH].set(vals)


def kernel(x, norm_w, w_in, rg_conv_w, rg_conv_b, rg_gate_w, rg_gate_b, rg_lambda, ml_gate_b, ml_norm_w, gd_conv_w, gd_a_log, gd_dt_bias, gd_norm_w, w_out, final_norm_w, loss_target, m_norm_w, m_w_in, m_rg_conv_w, m_rg_conv_b, m_rg_gate_w, m_rg_gate_b, m_rg_lambda, m_ml_gate_b, m_ml_norm_w, m_gd_conv_w, m_gd_a_log, m_gd_dt_bias, m_gd_norm_w, m_w_out, m_final_norm_w, v_norm_w, v_w_in, v_rg_conv_w, v_rg_conv_b, v_rg_gate_w, v_rg_gate_b, v_rg_lambda, v_ml_gate_b, v_ml_norm_w, v_gd_conv_w, v_gd_a_log, v_gd_dt_bias, v_gd_norm_w, v_w_out, v_final_norm_w):
    s = x.shape[1]
    xs = x.reshape(s, D)
    tgt = loss_target.reshape(s, D)
    me = 4 * lax.axis_index("x") + 2 * lax.axis_index("y") + lax.axis_index("c")

    comm = MeshComm(w_in, w_out, [rg_conv_w, rg_gate_b, gd_conv_w])
    rg_conv_full, rg_gb_full, gd_conv_full = comm.small_weights
    loss_part, dx, d_fw, g_small = local_step(
        xs, tgt, comm, rg_conv_full, rg_gb_full, gd_conv_full, norm_w, rg_conv_b, rg_gate_w, rg_lambda,
        ml_gate_b, ml_norm_w, gd_a_log, gd_dt_bias, gd_norm_w, final_norm_w)
    given_w = dict(norm_w=norm_w, rg_conv_w=rg_conv_w, rg_conv_b=rg_conv_b, rg_gate_w=rg_gate_w, rg_gate_b=rg_gate_b,
                   rg_lambda=rg_lambda, ml_gate_b=ml_gate_b, ml_norm_w=ml_norm_w, gd_conv_w=gd_conv_w, gd_a_log=gd_a_log,
                   gd_dt_bias=gd_dt_bias, gd_norm_w=gd_norm_w, final_norm_w=final_norm_w, w_in=w_in, w_out=w_out)
    given_m = dict(norm_w=m_norm_w, rg_conv_w=m_rg_conv_w, rg_conv_b=m_rg_conv_b, rg_gate_w=m_rg_gate_w, rg_gate_b=m_rg_gate_b,
                   rg_lambda=m_rg_lambda, ml_gate_b=m_ml_gate_b, ml_norm_w=m_ml_norm_w, gd_conv_w=m_gd_conv_w,
                   gd_a_log=m_gd_a_log, gd_dt_bias=m_gd_dt_bias, gd_norm_w=m_gd_norm_w, final_norm_w=m_final_norm_w,
                   w_in=m_w_in, w_out=m_w_out)
    given_v = dict(norm_w=v_norm_w, rg_conv_w=v_rg_conv_w, rg_conv_b=v_rg_conv_b, rg_gate_w=v_rg_gate_w, rg_gate_b=v_rg_gate_b,
                   rg_lambda=v_rg_lambda, ml_gate_b=v_ml_gate_b, ml_norm_w=v_ml_norm_w, gd_conv_w=v_gd_conv_w,
                   gd_a_log=v_gd_a_log, gd_dt_bias=v_gd_dt_bias, gd_norm_w=v_gd_norm_w, final_norm_w=v_final_norm_w,
                   w_in=v_w_in, w_out=v_w_out)
    return finish_step(loss_part, dx, d_fw, g_small, comm, s, me, given_w, given_m, given_v)


def _gathered_pieces():
    per = D_IN // N_DEV
    pieces = []
    for lo, hi in ((0, 3584), (3592, 5640), (3584, 3592), (5640, 5648)):
        col = lo
        while col < hi:
            k = col // per
            end = min(hi, (k + 1) * per)
            pieces.append((k, col - k * per, end - k * per))
            col = end
    return pieces


class MeshComm:
    def __init__(self, w_in, w_out, small_shards):
        per = D_IN // N_DEV
        self.wi_sh = [w_in[l].astype(bf16) for l in range(DEPTH)]
        self.wo_sh = [w_out[l].astype(bf16) for l in range(DEPTH)]
        self.wi_land = jax.ShapeDtypeStruct((N_DEV, D, per), bf16)
        self.wo_land = jax.ShapeDtypeStruct((N_DEV, 3 * SEG // N_DEV, D), bf16)
        packed = _pack(small_shards)
        xf = Xfer([self.wi_sh[0], self.wo_sh[0], packed], ["all"] * 3,
                  [self.wi_land, self.wo_land, jax.ShapeDtypeStruct((N_DEV,) + packed.shape, f32)], [0, 1, 2], [None] * 3)
        wi_g, wo_g, sm_g = exchange(xf, "gather_first")
        self.landed = {0: (wi_g, wo_g)}
        shapes = [a.shape for a in small_shards]
        parts = [_unpack(sm_g[k], shapes) for k in range(N_DEV)]
        self.small_weights = [jnp.concatenate([p[j] for p in parts], axis=-1) for j in range(len(small_shards))]
        self.gwi_land = lax.empty((N_DEV, DEPTH, D, per), bf16)
        self.gwo_land = lax.empty((N_DEV, DEPTH, 3 * SEG // N_DEV, D), bf16)
        self.pending = None
        self.kept_gwo = None

    def weights(self, l):
        wi_g, wo_g = self.landed[l]
        cols = [wi_g[k, :, a:b] for k, a, b in _gathered_pieces()] + [jnp.zeros((D, DP - D_IN), bf16)]
        return jnp.concatenate(cols, axis=1), wo_g.reshape(3 * SEG, D)

    def fwd_xfer(self, l):
        if l + 1 >= DEPTH:
            return None
        return Xfer([self.wi_sh[l + 1], self.wo_sh[l + 1]], ["all", "all"], [self.wi_land, self.wo_land], [0, 1], [None, None])

    def fwd_landed(self, l, landed):
        self.landed[l + 1] = tuple(landed)

    def bwd_xfer(self, l, g_wo):
        gwo = g_wo.reshape(N_DEV, 3 * SEG // N_DEV, D).astype(bf16)
        srcs, land_of, layer = [], [], []
        if self.pending is not None:
            srcs, land_of, layer = list(self.pending), [0, 1], [l + 1, l + 1]
        if l == 0:
            srcs, land_of, layer = srcs + [gwo], land_of + [1], layer + [0]
        self.kept_gwo = gwo
        if not srcs:
            return None
        return Xfer(srcs, ["slot"] * len(srcs), [self.gwi_land, self.gwo_land], land_of, layer)

    def bwd_landed(self, landed):
        self.gwi_land, self.gwo_land = landed

    def grads_ready(self, l, pieces):
        per = D_IN // N_DEV
        g_wi = jnp.concatenate(pieces, axis=1)
        slots = jnp.stack([g_wi[:, k * per:(k + 1) * per] for k in range(N_DEV)]).astype(bf16)
        self.pending = (slots, self.kept_gwo)

    def finish(self, packed_small):
        xf = Xfer([self.pending[0], packed_small], ["slot", "all"],
                  [self.gwi_land, jax.ShapeDtypeStruct((N_DEV,) + packed_small.shape, f32)], [0, 1], [0, None])
        gwi_land, gsm = exchange(xf, "scatter_last")
        return gwi_land, self.gwo_land, gsm


def local_step(xs, tgt, comm, rg_conv_full, rg_gb_full, gd_conv_full, norm_w, rg_conv_b, rg_gate_w,
               rg_lambda, ml_gate_b, ml_norm_w, gd_a_log, gd_dt_bias, gd_norm_w, final_norm_w):
    acts = []
    for l in range(DEPTH):
        nw = norm_w[l].reshape(1, D)
        w_in_l, w_out_l = comm.weights(l)
        proj = inproj(xs, nw, w_in_l)
        rg_p = (rg_conv_full[l], rg_conv_b[l].reshape(1, SEG), _block_diag(rg_gate_w[l, 0]), _block_diag(rg_gate_w[l, 1]),
                rg_gb_full[l], rg_lambda[l].reshape(1, SEG))
        y_rg, hs = rglru_fwd(proj, *rg_p)
        ml_p = (jnp.zeros((1, HD), f32).at[0, 0:2 * NH].set(ml_gate_b[l].reshape(-1)), ml_norm_w[l].reshape(1, SEG))
        y_ml, cs, ns, ms = mlstm_fwd(proj, *ml_p)
        gd_p = (gd_conv_full[l], _lane_row(gd_a_log[l], 2 * NH), _lane_row(gd_dt_bias[l], 2 * NH), gd_norm_w[l].reshape(1, HD))
        xf = comm.fwd_xfer(l)
        y_gd, ss, *landed = gdn_fwd(proj, *gd_p, xf=xf)
        if xf is not None:
            comm.fwd_landed(l, landed)
        acts.append((xs, nw, proj, w_in_l, w_out_l, rg_p, y_rg, hs, ml_p, y_ml, cs, ns, ms, gd_p, y_gd, ss))
        xs = outproj(xs, y_rg, y_ml, y_gd, w_out_l)

    dx, loss_part, d_fw = head(xs, final_norm_w.reshape(1, D), tgt)

    g_small = {k: [None] * DEPTH for k in ("norm_w", "rg_conv_w", "rg_conv_b", "rg_gate_w", "rg_gate_b", "rg_lambda",
                                           "ml_gate_b", "ml_norm_w", "gd_conv_w", "gd_a_log", "gd_dt_bias", "gd_norm_w")}
    for l in reversed(range(DEPTH)):
        x_l, nw, proj, w_in_l, w_out_l, rg_p, y_rg, hs, ml_p, y_ml, cs, ns, ms, gd_p, y_gd, ss = acts[l]
        dy_rg, dy_ml, dy_gd, g_wo = outproj_bwd(dx, y_rg, y_ml, y_gd, w_out_l)
        d_rg, d_cw, d_cb, d_gr, d_gi, d_gb, d_lam = rglru_bwd(proj, hs, dy_rg, *rg_p)
        d_ml, d_sm_ml, d_bias, d_mnw = mlstm_bwd(proj, dy_ml, cs, ns, ms, *ml_p)
        xf = comm.bwd_xfer(l, g_wo)
        d_gd, d_sm_gd, d_gcw, d_al, d_dt, d_gnw, *landed = gdn_bwd(proj, dy_gd, ss, *gd_p, xf=xf)
        if xf is not None:
            comm.bwd_landed(landed)
        dx, d_nw = inproj_bwd_x(x_l, nw, w_in_l, dx, d_rg, d_ml, d_gd, d_sm_ml, d_sm_gd)
        gw_rg = wgrad(x_l, nw, [d_rg], "wgrad_rg")
        gw_ml = wgrad(x_l, nw, [d_ml], "wgrad_ml")
        gw_gd = wgrad(x_l, nw, [d_gd], "wgrad_gd")
        gw_sm = wgrad(x_l, nw, [d_sm_ml, d_sm_gd], "wgrad_small")
        comm.grads_ready(l, [gw_rg, gw_ml, gw_sm[:, 0:2 * NH], gw_gd, gw_sm[:, 2 * NH:4 * NH]])
        g_small["norm_w"][l] = d_nw[0]
        g_small["rg_conv_w"][l] = d_cw
        g_small["rg_conv_b"][l] = d_cb[0]
        g_small["rg_gate_w"][l] = jnp.stack([_diag_blocks(d_gr), _diag_blocks(d_gi)])
        g_small["rg_gate_b"][l] = d_gb
        g_small["rg_lambda"][l] = d_lam[0]
        g_small["ml_gate_b"][l] = d_bias[0, 0:2 * NH].reshape(2, NH)
        g_small["ml_norm_w"][l] = d_mnw[0]
        g_small["gd_conv_w"][l] = d_gcw
        g_small["gd_a_log"][l] = d_al[0, 2 * NH:3 * NH]
        g_small["gd_dt_bias"][l] = d_dt[0, 2 * NH:3 * NH]
        g_small["gd_norm_w"][l] = d_gnw[0]
    return loss_part, dx, d_fw, g_small


def finish_step(loss_part, dx, d_fw, g_small, comm, s, me, given_w, given_m, given_v):
    small_names = ["norm_w", "rg_conv_w", "rg_conv_b", "rg_gate_w", "rg_gate_b", "rg_lambda", "ml_gate_b", "ml_norm_w",
                   "gd_conv_w", "gd_a_log", "gd_dt_bias", "gd_norm_w"]
    small_list = [loss_part[0, 0:1], d_fw[0]] + [jnp.stack(g_small[k]) for k in small_names]
    small_shapes = [a.shape for a in small_list]
    gwi_r, gwo_r, gsm_r = comm.finish(_pack(small_list))

    g_wi, d_wi, m_wi, v_wi = adam_slots(gwi_r, given_w["w_in"], given_m["w_in"], given_v["w_in"], 256, "adam_w_in")
    g_wo, d_wo, m_wo, v_wo = adam_slots(gwo_r, given_w["w_out"], given_m["w_out"], given_v["w_out"], 192, "adam_w_out")
    g_all = _unpack(sum_slots(gsm_r, "sum_small"), small_shapes)
    loss = g_all[0][0]
    grads = {"final_norm_w": g_all[1]}
    for k, g in zip(small_names, g_all[2:]):
        grads[k] = g
    for k, width in (("rg_conv_w", 64), ("rg_gate_b", 64), ("gd_conv_w", 192)):
        grads[k] = lax.dynamic_slice_in_dim(grads[k], me * width, width, axis=2)
    names = small_names + ["final_norm_w"]
    shapes = [given_w[k].shape for k in names]
    d_p, m_p, v_p = adam_flat(_pack([grads[k] for k in names]), _pack([given_w[k] for k in names]),
                              _pack([given_m[k] for k in names]), _pack([given_v[k] for k in names]), "adam_small")
    deltas = dict(zip(names, _unpack(d_p, shapes)))
    new_m = dict(zip(names, _unpack(m_p, shapes)))
    new_v = dict(zip(names, _unpack(v_p, shapes)))
    grads["w_in"], deltas["w_in"], new_m["w_in"], new_v["w_in"] = g_wi, d_wi, m_wi, v_wi
    grads["w_out"], deltas["w_out"], new_m["w_out"], new_v["w_out"] = g_wo, d_wo, m_wo, v_wo

    order = ["norm_w", "w_in", "rg_conv_w", "rg_conv_b", "rg_gate_w", "rg_gate_b", "rg_lambda", "ml_gate_b", "ml_norm_w",
             "gd_conv_w", "gd_a_log", "gd_dt_bias", "gd_norm_w", "w_out", "final_norm_w"]
    return (loss, dx.reshape(1, s, D), *[grads[k] for k in order], *[deltas[k] for k in order],
            *[new_m[k] for k in order], *[new_v[k] for k in order])
```

```python
import functools
from typing import NamedTuple

import jax
import jax.numpy as jnp
from jax import lax
from jax.experimental import pallas as pl
from jax.experimental.pallas import tpu as pltpu

f32 = jnp.float32
bf16 = jnp.bfloat16
MESH = pl.DeviceIdType.MESH

N_DEV = 8
D = 1024
DEPTH = 4
EPS = 1e-6
SEG = 512
HD = 128
NH = 4
LC = 64
RUN = 4
LR = RUN * LC
D_IN = 5648
DP = 5760
SMALL0 = 5632
RG_TILE = 256
RG_C = 8.0

ADAM_LR = 0.001
ADAM_B1 = 0.9
ADAM_B2 = 0.999
ADAM_EPS = 1e-08
ADAM_WD = 0.01
ADAM_STEP = 10


def _cp(vmem_mb):
    return pltpu.CompilerParams(vmem_limit_bytes=vmem_mb * 2 ** 20)


def _dot(a, b, ca, cb):
    return lax.dot_general(a.astype(bf16), b.astype(bf16), (((ca,), (cb,)), ((), ())), preferred_element_type=f32)


@jax.custom_vjp
def mm_nn(a, b):
    return _dot(a, b, 1, 0)


@jax.custom_vjp
def mm_nt(a, b):
    return _dot(a, b, 1, 1)


@jax.custom_vjp
def mm_tn(a, b):
    return _dot(a, b, 0, 0)


mm_nn.defvjp(lambda a, b: (mm_nn(a, b), (a, b)), lambda r, g: (mm_nt(g, r[1]), mm_tn(r[0], g)))
mm_nt.defvjp(lambda a, b: (mm_nt(a, b), (a, b)), lambda r, g: (mm_nn(g, r[1]), mm_tn(g, r[0])))
mm_tn.defvjp(lambda a, b: (mm_tn(a, b), (a, b)), lambda r, g: (mm_nt(r[1], g), mm_nn(r[0], g)))


def _split(x):
    hi = x.astype(bf16)
    return hi, (x - hi.astype(f32)).astype(bf16)


def dot3(a, b):
    ah, al = _split(a)
    bh, bl = _split(b)
    d = functools.partial(jnp.dot, preferred_element_type=f32)
    return d(ah, bh) + (d(al, bh) + d(ah, bl))


def _tri_sum(x, reverse):
    n = x.shape[0]
    r = lax.broadcasted_iota(jnp.int32, (n, 3 * n), 0)
    c = lax.broadcasted_iota(jnp.int32, (n, 3 * n), 1) & (n - 1)
    ones = ((c >= r) if reverse else (c <= r)).astype(bf16)
    hi = x.astype(bf16)
    rest = x - hi.astype(f32)
    mid = rest.astype(bf16)
    lo = (rest - mid.astype(f32)).astype(bf16)
    return jnp.dot(ones, jnp.concatenate([hi, mid, lo], axis=0), preferred_element_type=f32)


@jax.custom_vjp
def cumsum_rows(x):
    return _tri_sum(x, False)


@jax.custom_vjp
def rev_cumsum_rows(x):
    return _tri_sum(x, True)


cumsum_rows.defvjp(lambda x: (cumsum_rows(x), None), lambda _, g: (rev_cumsum_rows(g),))
rev_cumsum_rows.defvjp(lambda x: (rev_cumsum_rows(x), None), lambda _, g: (cumsum_rows(g),))


def _tri(n, strict=False):
    r = lax.broadcasted_iota(jnp.int32, (n, n), 0)
    c = lax.broadcasted_iota(jnp.int32, (n, n), 1)
    return (r > c) if strict else (r >= c)


def _lane_col(v, j):
    lane = lax.broadcasted_iota(jnp.int32, v.shape, 1)
    return jnp.sum(jnp.where(lane == j, v, 0.0), axis=1, keepdims=True)


def _rows_from(pieces, rows, width):
    ridx = lax.broadcasted_iota(jnp.int32, (rows, width), 0)
    out = jnp.zeros((rows, width), f32)
    for h, p in enumerate(pieces):
        out = out + jnp.where(ridx == h, jnp.broadcast_to(p, (rows, width)), 0.0)
    return out


def causal_conv(halo8, x, w4):
    t = x.shape[0]
    xe = jnp.concatenate([halo8, x], axis=0)
    y = xe[5:5 + t] * w4[0:1]
    for k in range(1, 4):
        y = y + xe[5 + k:5 + k + t] * w4[k:k + 1]
    return y


def ml_chunk(q, k, v, o_pre, z, small, bias_row, norm_w, C, n, m):
    n_ch = q.shape[0] // LC
    lane = lax.broadcasted_iota(jnp.int32, small.shape, 1)
    pre = small + bias_row
    lg = jnp.where(lane < NH, pre, jnp.where(lane < 2 * NH, jax.nn.log_sigmoid(pre), 0.0))
    rows = [slice(c * LC, (c + 1) * LC) for c in range(n_ch)]
    lgs = [lg[r] for r in rows]
    bcs = [cumsum_rows(x) for x in lgs]
    lg_ts = [x.T for x in lgs]
    bc_ts = [x.T for x in bcs]
    causal = _tri(LC)
    pairs = [(c, h) for c in range(n_ch) for h in range(NH)]
    idx = range(len(pairs))
    sls = [slice(h * HD, (h + 1) * HD) for h in range(NH)]
    qs = [q[rows[c], sls[h]] * (HD ** -0.5) for c, h in pairs]
    ks = [k[rows[c], sls[h]] for c, h in pairs]
    vs = [v[rows[c], sls[h]] for c, h in pairs]
    li_cols = [_lane_col(lgs[c], h) for c, h in pairs]
    b_cols = [_lane_col(bcs[c], NH + h) for c, h in pairs]
    dms = [jnp.where(causal, b_cols[i] - bc_ts[c][NH + h:NH + h + 1, :] + lg_ts[c][h:h + 1, :], -jnp.inf)
           for i, (c, h) in enumerate(pairs)]
    dm_maxs = [jnp.max(dms[i], axis=-1, keepdims=True) for i in idx]
    gs = [b_cols[i][LC - 1:LC, :] for i in idx]
    wss = [gs[i] - b_cols[i] + li_cols[i] for i in idx]
    ws_maxs = [jnp.max(wss[i], axis=0, keepdims=True) for i in idx]
    qks = [mm_nt(qs[i], ks[i]) for i in idx]
    m_in = [None] * len(pairs)
    m_out = [None] * len(pairs)
    for h in range(NH):
        cur = m[h:h + 1, 0:1]
        for c in range(n_ch):
            i = c * NH + h
            m_in[i] = cur
            cur = jnp.maximum(gs[i] + cur, ws_maxs[i])
            m_out[i] = cur
    m_inters = [b_cols[i] + m_in[i] for i in idx]
    m_ts = [jnp.maximum(m_inters[i], dm_maxs[i]) for i in idx]
    ss = [qks[i] * jnp.exp(dms[i] - m_ts[i]) for i in idx]
    scs = [jnp.exp(m_inters[i] - m_ts[i]) for i in idx]
    decs = [jnp.exp(gs[i] + m_in[i] - m_out[i]) for i in idx]
    kws = [jnp.exp(wss[i] - m_out[i]) * ks[i] for i in idx]
    c_adds = [mm_tn(kws[i], vs[i]) for i in idx]
    n_adds = [jnp.sum(kws[i], axis=0, keepdims=True) for i in idx]
    svs = [mm_nn(ss[i], vs[i]) for i in idx]
    s_sums = [jnp.sum(ss[i], axis=-1, keepdims=True) for i in idx]
    c_hs = [C[sl, :] for sl in sls]
    n_hs = [n[h:h + 1, :] for h in range(NH)]
    hhs = [None] * len(pairs)
    for c in range(n_ch):
        for h in range(NH):
            i = c * NH + h
            num = svs[i] + scs[i] * mm_nn(qs[i], c_hs[h])
            den = s_sums[i] + scs[i] * jnp.sum(qs[i] * n_hs[h], axis=-1, keepdims=True)
            hhs[i] = num / jnp.maximum(jnp.abs(den), jnp.exp(-m_ts[i]))
        c_hs = [decs[c * NH + h] * c_hs[h] + c_adds[c * NH + h] for h in range(NH)]
        n_hs = [decs[c * NH + h] * n_hs[h] + n_adds[c * NH + h] for h in range(NH)]
    ys = [hhs[i] * lax.rsqrt(jnp.mean(hhs[i] * hhs[i], axis=-1, keepdims=True) + EPS) * norm_w[:, sls[h]]
          * jax.nn.sigmoid(o_pre[rows[c], sls[h]]) * jax.nn.silu(z[rows[c], sls[h]]) for i, (c, h) in enumerate(pairs)]
    y = jnp.concatenate([jnp.concatenate(ys[c * NH:(c + 1) * NH], axis=1) for c in range(n_ch)], axis=0)
    last = (n_ch - 1) * NH
    m_rows = [jnp.broadcast_to(m_out[last + h], (1, HD)) for h in range(NH)]
    return y, jnp.concatenate(c_hs, axis=0), _rows_from(n_hs, 8, HD), _rows_from(m_rows, 8, HD)


@jax.custom_vjp
def _unit_lower_inverses(mats):
    eye = (lax.broadcasted_iota(jnp.int32, (LC, LC), 0) == lax.broadcasted_iota(jnp.int32, (LC, LC), 1)).astype(f32)
    ps = [-m for m in mats]
    ts = [eye + p for p in ps]
    for _ in range(5):
        ps = [dot3(p, p) for p in ps]
        ts = [t + dot3(t, p) for t, p in zip(ts, ps)]
    return tuple(ts)


def _unit_lower_inverses_fwd(mats):
    ts = _unit_lower_inverses(mats)
    return ts, ts


def _unit_lower_inverses_bwd(ts, gs):
    tts = [t.T for t in ts]
    mid = [dot3(tt, g) for tt, g in zip(tts, gs)]
    return (tuple(-dot3(m, tt) for m, tt in zip(mid, tts)),)


_unit_lower_inverses.defvjp(_unit_lower_inverses_fwd, _unit_lower_inverses_bwd)


def gd_chunk(qh8, q, kh8, k, vh8, v, z, small, conv_w, alog_row, dt_row, norm_w, st):
    n_ch = q.shape[0] // LC
    lane = lax.broadcasted_iota(jnp.int32, small.shape, 1)
    is_a = (lane >= 2 * NH) & (lane < 3 * NH)
    g_all = jnp.where(is_a, -jnp.exp(alog_row) * jax.nn.softplus(small + dt_row), 0.0)
    beta_all = jax.nn.sigmoid(small)
    qc = jax.nn.silu(causal_conv(qh8, q, conv_w[:, 0:SEG]))
    kc = jax.nn.silu(causal_conv(kh8, k, conv_w[:, SEG:2 * SEG]))
    vc = jax.nn.silu(causal_conv(vh8, v, conv_w[:, 2 * SEG:3 * SEG]))
    incl = _tri(LC)
    strict = _tri(LC, strict=True)
    rows = [slice(c * LC, (c + 1) * LC) for c in range(n_ch)]
    gc_alls = [cumsum_rows(g_all[r]) for r in rows]
    gc_ts = [g.T for g in gc_alls]
    pairs = [(c, h) for c in range(n_ch) for h in range(NH)]
    idx = range(len(pairs))
    sls = [slice(h * HD, (h + 1) * HD) for h in range(NH)]
    qs = [qc[rows[c], sls[h]] for c, h in pairs]
    ks = [kc[rows[c], sls[h]] for c, h in pairs]
    vs = [vc[rows[c], sls[h]] for c, h in pairs]
    qs = [x * lax.rsqrt(jnp.sum(x * x, axis=-1, keepdims=True) + EPS) * (HD ** -0.5) for x in qs]
    ks = [x * lax.rsqrt(jnp.sum(x * x, axis=-1, keepdims=True) + EPS) for x in ks]
    betas = [_lane_col(beta_all[rows[c]], 3 * NH + h) for c, h in pairs]
    gcs = [_lane_col(gc_alls[c], 2 * NH + h) for c, h in pairs]
    gams = [jnp.exp(jnp.where(incl, gcs[i] - gc_ts[c][2 * NH + h:2 * NH + h + 1, :], -jnp.inf))
            for i, (c, h) in enumerate(pairs)]
    kbs = [ks[i] * betas[i] for i in idx]
    mats = tuple(jnp.where(strict, mm_nt(kbs[i], ks[i]) * gams[i], 0.0) for i in idx)
    aqks = [mm_nt(qs[i], ks[i]) * gams[i] for i in idx]
    t_invs = _unit_lower_inverses(mats)
    us = [mm_nn(t_invs[i], vs[i] * betas[i]) for i in idx]
    ws = [mm_nn(t_invs[i], kbs[i] * jnp.exp(gcs[i])) for i in idx]
    g_lasts = [gcs[i][LC - 1:LC, :] for i in idx]
    q_decs = [qs[i] * jnp.exp(gcs[i]) for i in idx]
    k_decs = [ks[i] * jnp.exp(g_lasts[i] - gcs[i]) for i in idx]
    e_lasts = [jnp.exp(g_lasts[i]) for i in idx]
    s_hs = [st[sl, :] for sl in sls]
    os_ = [None] * len(pairs)
    for c in range(n_ch):
        ids = [c * NH + h for h in range(NH)]
        v_news = [us[i] - mm_nn(ws[i], s_hs[h]) for h, i in enumerate(ids)]
        for h, i in enumerate(ids):
            os_[i] = mm_nn(q_decs[i], s_hs[h]) + mm_nn(aqks[i], v_news[h])
        s_hs = [s_hs[h] * e_lasts[i] + mm_tn(k_decs[i], v_news[h]) for h, i in enumerate(ids)]
    ys = [os_[i] * lax.rsqrt(jnp.mean(os_[i] * os_[i], axis=-1, keepdims=True) + EPS) * norm_w
          * jax.nn.silu(z[rows[c], sls[h]]) for i, (c, h) in enumerate(pairs)]
    y = jnp.concatenate([jnp.concatenate(ys[c * NH:(c + 1) * NH], axis=1) for c in range(n_ch)], axis=0)
    return y, jnp.concatenate(s_hs, axis=0)


def rg_pre(xh8, x, conv_w, conv_b, gw_r, gw_i, gate_b, lam):
    xc = causal_conv(xh8, x, conv_w) + conv_b
    r = jax.nn.sigmoid(mm_nn(xc, gw_r) + gate_b[0:1])
    i = jax.nn.sigmoid(mm_nn(xc, gw_i) + gate_b[1:2])
    log_a = -RG_C * r * jax.nn.softplus(-lam)
    a = jnp.exp(log_a)
    th = jnp.tanh(log_a)
    one_minus_a2 = -2.0 * th / (1.0 - th)
    b = jnp.sqrt(one_minus_a2) * (i * xc)
    return a, b


def _scan_rows(a_ref, b_ref, out_ref, h0, n_rows, reverse):
    n_groups = n_rows // 8
    width = a_ref.shape[1]
    row = lax.broadcasted_iota(jnp.int32, (8, width), 0)

    def body(j, h):
        g = (n_groups - 1 - j) if reverse else j
        r0 = pl.multiple_of(g * 8, 8)
        av = a_ref[pl.ds(r0, 8), :]
        bv = b_ref[pl.ds(r0, 8), :]
        for d in (1, 2, 4):
            sh = (8 - d) if reverse else d
            a_s = pltpu.roll(av, sh, 0)
            b_s = pltpu.roll(bv, sh, 0)
            valid = (row < 8 - d) if reverse else (row >= d)
            bv = jnp.where(valid, av * b_s + bv, bv)
            av = jnp.where(valid, av * a_s, av)
        hv = av * h + bv
        out_ref[pl.ds(r0, 8), :] = hv
        return hv[0:1, :] if reverse else hv[7:8, :]

    return lax.fori_loop(0, n_groups, body, h0)


def _norm_rows(xv):
    r = lax.rsqrt(jnp.mean(xv * xv, axis=-1, keepdims=True) + EPS)
    return xv * r, r


def inproj(x, nw, w):
    s, ts = x.shape[0], 256

    def body(x_ref, nw_ref, w_ref, o_ref):
        xn, _ = _norm_rows(x_ref[...])
        hn = xn * nw_ref[...]
        o_ref[...] = jnp.dot(hn.astype(bf16), w_ref[...], preferred_element_type=f32)

    return pl.pallas_call(
        body, grid=(s // ts,),
        in_specs=[pl.BlockSpec((ts, D), lambda i: (i, 0)), pl.BlockSpec((1, D), lambda i: (0, 0)),
                  pl.BlockSpec((D, DP), lambda i: (0, 0))],
        out_specs=pl.BlockSpec((ts, DP), lambda i: (i, 0)),
        out_shape=jax.ShapeDtypeStruct((s, DP), f32), name="inproj", compiler_params=_cp(56))(x, nw, w)


def outproj(x, yr, ym, yg, wo):
    s, ts = x.shape[0], 256

    def body(x_ref, yr_ref, ym_ref, yg_ref, wo_ref, o_ref):
        acc = x_ref[...]
        for j, y_ref in enumerate((yr_ref, ym_ref, yg_ref)):
            acc = acc + jnp.dot(y_ref[...].astype(bf16), wo_ref[j * SEG:(j + 1) * SEG, :], preferred_element_type=f32)
        o_ref[...] = acc

    yspec = pl.BlockSpec((ts, SEG), lambda i: (i, 0))
    return pl.pallas_call(
        body, grid=(s // ts,),
        in_specs=[pl.BlockSpec((ts, D), lambda i: (i, 0)), yspec, yspec, yspec,
                  pl.BlockSpec((3 * SEG, D), lambda i: (0, 0))],
        out_specs=pl.BlockSpec((ts, D), lambda i: (i, 0)),
        out_shape=jax.ShapeDtypeStruct((s, D), f32), name="outproj", compiler_params=_cp(40))(x, yr, ym, yg, wo)


def head(x, fw, tgt):
    s, ts = x.shape[0], 256

    def body(x_ref, fw_ref, t_ref, dx_ref, loss_ref, dfw_ref):
        @pl.when(pl.program_id(0) == 0)
        def _():
            loss_ref[...] = jnp.zeros_like(loss_ref)
            dfw_ref[...] = jnp.zeros_like(dfw_ref)

        xn, r = _norm_rows(x_ref[...])
        fw_v = fw_ref[...]
        err = xn * fw_v - t_ref[...]
        loss_ref[...] += 0.5 * jnp.sum(jnp.mean(err * err, axis=-1, keepdims=True))
        dy = err * (1.0 / D)
        dfw_ref[0:1, :] += jnp.sum(dy * xn, axis=0, keepdims=True)
        dxn = dy * fw_v
        dx_ref[...] = r * (dxn - xn * jnp.mean(dxn * xn, axis=-1, keepdims=True))

    tile = pl.BlockSpec((ts, D), lambda i: (i, 0))
    return pl.pallas_call(
        body, grid=(s // ts,),
        in_specs=[tile, pl.BlockSpec((1, D), lambda i: (0, 0)), tile],
        out_specs=[tile, pl.BlockSpec((8, 128), lambda i: (0, 0)), pl.BlockSpec((8, D), lambda i: (0, 0))],
        out_shape=[jax.ShapeDtypeStruct((s, D), f32), jax.ShapeDtypeStruct((8, 128), f32),
                   jax.ShapeDtypeStruct((8, D), f32)], name="head")(x, fw, tgt)


def outproj_bwd(dx, yr, ym, yg, wo):
    s, ts = dx.shape[0], 256

    def body(dx_ref, yr_ref, ym_ref, yg_ref, wo_ref, dyr_ref, dym_ref, dyg_ref, gwo_ref):
        @pl.when(pl.program_id(0) == 0)
        def _():
            gwo_ref[...] = jnp.zeros_like(gwo_ref)

        dxb = dx_ref[...].astype(bf16)
        for j, (y_ref, dy_ref) in enumerate(((yr_ref, dyr_ref), (ym_ref, dym_ref), (yg_ref, dyg_ref))):
            rows = slice(j * SEG, (j + 1) * SEG)
            dy_ref[...] = lax.dot_general(dxb, wo_ref[rows, :], (((1,), (1,)), ((), ())), preferred_element_type=f32)
            gwo_ref[rows, :] += lax.dot_general(y_ref[...].astype(bf16), dxb, (((0,), (0,)), ((), ())),
                                                preferred_element_type=f32)

    yspec = pl.BlockSpec((ts, SEG), lambda i: (i, 0))
    wspec = pl.BlockSpec((3 * SEG, D), lambda i: (0, 0))
    ysh = jax.ShapeDtypeStruct((s, SEG), f32)
    return pl.pallas_call(
        body, grid=(s // ts,),
        in_specs=[pl.BlockSpec((ts, D), lambda i: (i, 0)), yspec, yspec, yspec, wspec],
        out_specs=[yspec, yspec, yspec, wspec],
        out_shape=[ysh, ysh, ysh, jax.ShapeDtypeStruct((3 * SEG, D), f32)],
        name="outproj_bwd", compiler_params=_cp(48))(dx, yr, ym, yg, wo)


def inproj_bwd_x(x, nw, w, dxo, d_rg, d_ml, d_gd, d_sa, d_sb):
    s, ts = x.shape[0], 256
    widths = (d_rg.shape[1], d_ml.shape[1], d_gd.shape[1], HD)

    def body(x_ref, nw_ref, w_ref, dxo_ref, rg_ref, ml_ref, gd_ref, sa_ref, sb_ref, dx_ref, dnw_ref):
        @pl.when(pl.program_id(0) == 0)
        def _():
            dnw_ref[...] = jnp.zeros_like(dnw_ref)

        xn, r = _norm_rows(x_ref[...])
        pieces = (rg_ref[...], ml_ref[...], gd_ref[...], sa_ref[...] + sb_ref[...])
        dhn = jnp.zeros((ts, D), f32)
        c0 = 0
        for piece, wd in zip(pieces, widths):
            dhn = dhn + lax.dot_general(piece.astype(bf16), w_ref[:, c0:c0 + wd], (((1,), (1,)), ((), ())),
                                        preferred_element_type=f32)
            c0 += wd
        dnw_ref[0:1, :] += jnp.sum(dhn * xn, axis=0, keepdims=True)
        dxn = dhn * nw_ref[...]
        dx_ref[...] = dxo_ref[...] + r * (dxn - xn * jnp.mean(dxn * xn, axis=-1, keepdims=True))

    tile = pl.BlockSpec((ts, D), lambda i: (i, 0))
    return pl.pallas_call(
        body, grid=(s // ts,),
        in_specs=[tile, pl.BlockSpec((1, D), lambda i: (0, 0)), pl.BlockSpec((D, DP), lambda i: (0, 0)), tile]
        + [pl.BlockSpec((ts, wd), lambda i: (i, 0)) for wd in widths] + [pl.BlockSpec((ts, HD), lambda i: (i, 0))],
        out_specs=[tile, pl.BlockSpec((8, D), lambda i: (0, 0))],
        out_shape=[jax.ShapeDtypeStruct((s, D), f32), jax.ShapeDtypeStruct((8, D), f32)],
        name="inproj_bwd_x", compiler_params=_cp(56))(x, nw, w, dxo, d_rg, d_ml, d_gd, d_sa, d_sb)


def wgrad(x, nw, dps, name):
    s, ts = x.shape[0], 512
    c = dps[0].shape[1]
    ct = min(c, SEG)
    n_dp = len(dps)

    def body(*refs):
        x_ref, nw_ref = refs[0], refs[1]
        dp_refs = refs[2:2 + n_dp]
        o_ref = refs[2 + n_dp]

        @pl.when(pl.program_id(1) == 0)
        def _():
            o_ref[...] = jnp.zeros_like(o_ref)

        xn, _ = _norm_rows(x_ref[...])
        hn = (xn * nw_ref[...]).astype(bf16)
        dp = dp_refs[0][...]
        for extra in dp_refs[1:]:
            dp = dp + extra[...]
        o_ref[...] += lax.dot_general(hn, dp.astype(bf16), (((0,), (0,)), ((), ())), preferred_element_type=f32)

    return pl.pallas_call(
        body, grid=(c // ct, s // ts),
        in_specs=[pl.BlockSpec((ts, D), lambda j, i: (i, 0)), pl.BlockSpec((1, D), lambda j, i: (0, 0))]
        + [pl.BlockSpec((ts, ct), lambda j, i: (i, j)) for _ in dps],
        out_specs=pl.BlockSpec((D, ct), lambda j, i: (0, j)),
        out_shape=jax.ShapeDtypeStruct((D, c), f32), name=name, compiler_params=_cp(40))(x, nw, *dps)


def _seg_spec(rows, seg, n_tiles=None):
    if n_tiles is None:
        return pl.BlockSpec((rows, SEG), lambda i: (i, seg))
    return pl.BlockSpec((rows, SEG), lambda i: (n_tiles - 1 - i, seg))


def _halo_spec(rows, seg, n_tiles=None):
    per = rows // 8
    if n_tiles is None:
        return pl.BlockSpec((8, SEG), lambda i: (jnp.maximum(i * per - 1, 0), seg))
    return pl.BlockSpec((8, SEG), lambda i: (jnp.maximum((n_tiles - 1 - i) * per - 1, 0), seg))


def _const_spec(shape):
    return pl.BlockSpec(shape, lambda i: tuple(0 for _ in shape))


def rglru_fwd(proj, conv_w, conv_b, gw_r, gw_i, gate_b, lam):
    s = proj.shape[0]
    tr = RG_TILE

    def body(xh_ref, x_ref, z_ref, cw_ref, cb_ref, gr_ref, gi_ref, gb_ref, lam_ref, y_ref, h_ref, a_s, b_s, hc):
        first = pl.program_id(0) == 0

        @pl.when(first)
        def _():
            hc[...] = jnp.zeros_like(hc)

        xh = jnp.where(first, 0.0, xh_ref[...])
        a, b = rg_pre(xh, x_ref[...], cw_ref[...], cb_ref[...], gr_ref[...], gi_ref[...], gb_ref[...], lam_ref[...])
        a_s[...] = a
        b_s[...] = b
        hc[0:1, :] = _scan_rows(a_s, b_s, h_ref, hc[0:1, :], tr, False)
        y_ref[...] = h_ref[...] * jax.nn.silu(z_ref[...])

    out = pl.BlockSpec((tr, SEG), lambda i: (i, 0))
    return pl.pallas_call(
        body, grid=(s // tr,),
        in_specs=[_halo_spec(tr, 0), _seg_spec(tr, 0), _seg_spec(tr, 1), _const_spec((4, SEG)), _const_spec((1, SEG)),
                  _const_spec((SEG, SEG)), _const_spec((SEG, SEG)), _const_spec((2, SEG)), _const_spec((1, SEG))],
        out_specs=[out, out],
        out_shape=[jax.ShapeDtypeStruct((s, SEG), f32), jax.ShapeDtypeStruct((s, SEG), f32)],
        scratch_shapes=[pltpu.VMEM((tr, SEG), f32), pltpu.VMEM((tr, SEG), f32), pltpu.VMEM((8, SEG), f32)],
        name="rglru_fwd", compiler_params=_cp(40))(proj, proj, proj, conv_w, conv_b, gw_r, gw_i, gate_b, lam)


def rglru_bwd(proj, hs, dy, conv_w, conv_b, gw_r, gw_i, gate_b, lam):
    s = proj.shape[0]
    tr = RG_TILE
    nt = s // tr

    def body(xh_ref, x_ref, z_ref, hh_ref, h_ref, dy_ref, cw_ref, cb_ref, gr_ref, gi_ref, gb_ref, lam_ref,
             dp_ref, dcw_ref, dcb_ref, dgr_ref, dgi_ref, dgb_ref, dlam_ref,
             an_s, g_s, dh_s, a_first, dh_first, dhalo):
        i = pl.program_id(0)
        first_tile = i == nt - 1

        @pl.when(i == 0)
        def _():
            for ref in (dcw_ref, dcb_ref, dgr_ref, dgi_ref, dgb_ref, dlam_ref, a_first, dh_first, dhalo):
                ref[...] = jnp.zeros_like(ref)

        xh = jnp.where(first_tile, 0.0, xh_ref[...])
        params = (cw_ref[...], cb_ref[...], gr_ref[...], gi_ref[...], gb_ref[...], lam_ref[...])
        (a, _), vjp = jax.vjp(rg_pre, xh, x_ref[...], *params)
        zv = z_ref[...]
        hv = h_ref[...]
        dyv = dy_ref[...]
        sig = jax.nn.sigmoid(zv)
        g_s[...] = dyv * (zv * sig)
        dp_ref[:, SEG:2 * SEG] = dyv * hv * (sig * (1.0 + zv * (1.0 - sig)))
        ridx = lax.broadcasted_iota(jnp.int32, (tr, SEG), 0)
        an_s[...] = jnp.where(ridx == tr - 1, jnp.broadcast_to(a_first[0:1, :], (tr, SEG)), pltpu.roll(a, tr - 1, 0))
        _scan_rows(an_s, g_s, dh_s, dh_first[0:1, :], tr, True)
        dh = dh_s[...]
        h_prev_last = jnp.where(first_tile, 0.0, hh_ref[...])[7:8, :]
        h_prev = pltpu.roll(hv, 1, 0)
        h_prev = jnp.where(ridx == 0, jnp.broadcast_to(h_prev_last, (tr, SEG)), h_prev)
        dxh, dx, dcw, dcb, dgr, dgi, dgb, dlam = vjp((dh * h_prev, dh))
        dp_ref[:, 0:SEG] = dx
        dp_ref[tr - 8:tr, 0:SEG] += dhalo[...]
        dhalo[...] = dxh
        a_first[0:1, :] = a[0:1, :]
        dh_first[0:1, :] = dh[0:1, :]
        dcw_ref[...] += dcw
        dcb_ref[...] += dcb
        dgr_ref[...] += dgr
        dgi_ref[...] += dgi
        dgb_ref[...] += dgb
        dlam_ref[...] += dlam

    pspecs = [_const_spec((4, SEG)), _const_spec((1, SEG)), _const_spec((SEG, SEG)), _const_spec((SEG, SEG)),
              _const_spec((2, SEG)), _const_spec((1, SEG))]
    pshapes = [jax.ShapeDtypeStruct(sh, f32) for sh in ((4, SEG), (1, SEG), (SEG, SEG), (SEG, SEG), (2, SEG), (1, SEG))]
    tile = pl.BlockSpec((tr, SEG), lambda i: (nt - 1 - i, 0))
    return pl.pallas_call(
        body, grid=(nt,),
        in_specs=[_halo_spec(tr, 0, nt), _seg_spec(tr, 0, nt), _seg_spec(tr, 1, nt),
                  pl.BlockSpec((8, SEG), lambda i: (jnp.maximum((nt - 1 - i) * (tr // 8) - 1, 0), 0)), tile, tile] + pspecs,
        out_specs=[pl.BlockSpec((tr, 2 * SEG), lambda i: (nt - 1 - i, 0))] + pspecs,
        out_shape=[jax.ShapeDtypeStruct((s, 2 * SEG), f32)] + pshapes,
        scratch_shapes=[pltpu.VMEM((tr, SEG), f32), pltpu.VMEM((tr, SEG), f32), pltpu.VMEM((tr, SEG), f32),
                        pltpu.VMEM((8, SEG), f32), pltpu.VMEM((8, SEG), f32), pltpu.VMEM((8, SEG), f32)],
        name="rglru_bwd", compiler_params=_cp(48))(proj, proj, proj, hs, hs, dy, conv_w, conv_b, gw_r, gw_i, gate_b, lam)


ML_SEGS = (2, 3, 4, 5, 6)
SMALL_BLK = SMALL0 // HD


def _cut_refs(refs, n_in, n_xin, n_out, n_land, n_scratch):
    bounds = [0, n_in, n_in + n_xin, n_in + n_xin + n_out, n_in + n_xin + n_out + n_land,
              n_in + n_xin + n_out + n_land + n_scratch, len(refs)]
    return [refs[a:b] for a, b in zip(bounds[:-1], bounds[1:])]


def mlstm_fwd(proj, bias_row, norm_w, xf=None):
    s = proj.shape[0]
    nc = s // LR
    n_in, n_out = 8, 4
    x_ops, x_in_specs, x_out_shapes, x_out_specs, aliases, x_sems, n_xin, n_land = _carried(xf, n_in, n_out)

    def body(*refs):
        ins, x_refs, outs, land_refs, scratch, sem_refs = _cut_refs(refs, n_in, n_xin, n_out, n_land, 3)
        q_ref, k_ref, v_ref, o_ref, z_ref, sm_ref, b_ref, nw_ref = ins
        y_ref, cs_ref, ns_ref, ms_ref = outs
        c_s, n_s, m_s = scratch
        i = pl.program_id(0)
        _carry_open(xf, i, nc, x_refs, land_refs, sem_refs)

        @pl.when(i == 0)
        def _():
            c_s[...] = jnp.zeros_like(c_s)
            n_s[...] = jnp.zeros_like(n_s)
            m_s[...] = jnp.zeros_like(m_s)

        cs_ref[0] = c_s[...]
        ns_ref[0] = n_s[...]
        ms_ref[0] = m_s[...]
        y, c2, n2, m2 = ml_chunk(q_ref[...], k_ref[...], v_ref[...], o_ref[...], z_ref[...], sm_ref[...],
                                 b_ref[...], nw_ref[...], c_s[...], n_s[...], m_s[...])
        y_ref[...] = y
        c_s[...] = c2
        n_s[...] = n2
        m_s[...] = m2
        _carry_close(xf, i, nc, x_refs, land_refs, sem_refs)

    return pl.pallas_call(
        body, grid=(nc,),
        in_specs=[_seg_spec(LR, sg) for sg in ML_SEGS]
        + [pl.BlockSpec((LR, HD), lambda i: (i, SMALL_BLK)), _const_spec((1, HD)), _const_spec((1, SEG))] + x_in_specs,
        out_specs=[pl.BlockSpec((LR, SEG), lambda i: (i, 0)), pl.BlockSpec((1, SEG, HD), lambda i: (i, 0, 0)),
                   pl.BlockSpec((1, 8, HD), lambda i: (i, 0, 0)), pl.BlockSpec((1, 8, HD), lambda i: (i, 0, 0))] + x_out_specs,
        out_shape=[jax.ShapeDtypeStruct((s, SEG), f32), jax.ShapeDtypeStruct((nc, SEG, HD), f32),
                   jax.ShapeDtypeStruct((nc, 8, HD), f32), jax.ShapeDtypeStruct((nc, 8, HD), f32)] + x_out_shapes,
        scratch_shapes=[pltpu.VMEM((SEG, HD), f32), pltpu.VMEM((8, HD), f32), pltpu.VMEM((8, HD), f32)] + x_sems,
        input_output_aliases=aliases, name="mlstm_fwd" if xf is None else "mlstm_fwd_carrying")(
            proj, proj, proj, proj, proj, proj, bias_row, norm_w, *x_ops)


def mlstm_bwd(proj, dy, cs, ns, ms, bias_row, norm_w, xf=None):
    s = proj.shape[0]
    nc = s // LR
    n_in, n_out = 12, 4
    x_ops, x_in_specs, x_out_shapes, x_out_specs, aliases, x_sems, n_xin, n_land = _carried(xf, n_in, n_out)

    def body(*refs):
        ins, x_refs, outs, land_refs, scratch, sem_refs = _cut_refs(refs, n_in, n_xin, n_out, n_land, 3)
        q_ref, k_ref, v_ref, o_ref, z_ref, sm_ref, dy_ref, cs_ref, ns_ref, ms_ref, b_ref, nw_ref = ins
        dp_ref, dsm_ref, db_ref, dnw_ref = outs
        dc_s, dn_s, dm_s = scratch
        i = pl.program_id(0)
        _carry_open(xf, i, nc, x_refs, land_refs, sem_refs)

        @pl.when(i == 0)
        def _():
            for ref in (db_ref, dnw_ref, dc_s, dn_s, dm_s):
                ref[...] = jnp.zeros_like(ref)

        _, vjp = jax.vjp(ml_chunk, q_ref[...], k_ref[...], v_ref[...], o_ref[...], z_ref[...], sm_ref[...],
                         b_ref[...], nw_ref[...], cs_ref[0], ns_ref[0], ms_ref[0])
        dq, dk, dv, do, dz, dsm, db, dnw, dc, dn, dm = vjp((dy_ref[...], dc_s[...], dn_s[...], dm_s[...]))
        for j, val in enumerate((dq, dk, dv, do, dz)):
            dp_ref[:, j * SEG:(j + 1) * SEG] = val
        dsm_ref[...] = dsm
        db_ref[0:1, :] += db
        dnw_ref[0:1, :] += dnw
        dc_s[...] = dc
        dn_s[...] = dn
        dm_s[...] = dm
        _carry_close(xf, i, nc, x_refs, land_refs, sem_refs)

    rev3 = lambda i: (nc - 1 - i, 0, 0)
    return pl.pallas_call(
        body, grid=(nc,),
        in_specs=[_seg_spec(LR, sg, nc) for sg in ML_SEGS]
        + [pl.BlockSpec((LR, HD), lambda i: (nc - 1 - i, SMALL_BLK)), pl.BlockSpec((LR, SEG), lambda i: (nc - 1 - i, 0)),
           pl.BlockSpec((1, SEG, HD), rev3), pl.BlockSpec((1, 8, HD), rev3), pl.BlockSpec((1, 8, HD), rev3),
           _const_spec((1, HD)), _const_spec((1, SEG))] + x_in_specs,
        out_specs=[pl.BlockSpec((LR, 5 * SEG), lambda i: (nc - 1 - i, 0)), pl.BlockSpec((LR, HD), lambda i: (nc - 1 - i, 0)),
                   _const_spec((8, HD)), _const_spec((8, SEG))] + x_out_specs,
        out_shape=[jax.ShapeDtypeStruct((s, 5 * SEG), f32), jax.ShapeDtypeStruct((s, HD), f32),
                   jax.ShapeDtypeStruct((8, HD), f32), jax.ShapeDtypeStruct((8, SEG), f32)] + x_out_shapes,
        scratch_shapes=[pltpu.VMEM((SEG, HD), f32), pltpu.VMEM((8, HD), f32), pltpu.VMEM((8, HD), f32)] + x_sems,
        input_output_aliases=aliases, name="mlstm_bwd" if xf is None else "mlstm_bwd_carrying", compiler_params=_cp(48))(
            proj, proj, proj, proj, proj, proj, dy, cs, ns, ms, bias_row, norm_w, *x_ops)


GD_SEGS = (7, 8, 9)


def _carried(xf, n_in, n_out):
    operands, out_shapes, aliases, sems, _, n_xin, n_land = _carry_plumb(xf, n_in, n_out)
    return operands, [_ANY] * n_xin, out_shapes, [_ANY] * n_land, aliases, sems, n_xin, n_land


def _carry_open(xf, i, n_steps, x_refs, land_refs, sem_refs):
    if xf is None:
        return
    srcs = x_refs[:len(xf.srcs)]

    @pl.when(i == 0)
    def _():
        _carry_start(xf, srcs, land_refs, sem_refs)

    @pl.when(i == n_steps // 2)
    def _():
        _carry_middle(xf, srcs, land_refs, sem_refs)


def _carry_close(xf, i, n_steps, x_refs, land_refs, sem_refs):
    if xf is None:
        return

    @pl.when(i == n_steps - 1)
    def _():
        _carry_finish(xf, x_refs[:len(xf.srcs)], land_refs, sem_refs)


def gdn_fwd(proj, conv_w, alog_row, dt_row, norm_w, xf=None):
    s = proj.shape[0]
    nc = s // LR
    n_in, n_out = 12, 2
    x_ops, x_in_specs, x_out_shapes, x_out_specs, aliases, x_sems, n_xin, n_land = _carried(xf, n_in, n_out)

    def body(*refs):
        qh_ref, q_ref, kh_ref, k_ref, vh_ref, v_ref, z_ref, sm_ref, cw_ref, al_ref, dt_ref, nw_ref = refs[:n_in]
        x_refs = refs[n_in:n_in + n_xin]
        y_ref, ss_ref = refs[n_in + n_xin:n_in + n_xin + n_out]
        land_refs = refs[n_in + n_xin + n_out:n_in + n_xin + n_out + n_land]
        st_s = refs[n_in + n_xin + n_out + n_land]
        sem_refs = refs[n_in + n_xin + n_out + n_land + 1:]
        i = pl.program_id(0)
        first = i == 0
        _carry_open(xf, i, nc, x_refs, land_refs, sem_refs)

        @pl.when(first)
        def _():
            st_s[...] = jnp.zeros_like(st_s)

        ss_ref[0] = st_s[...]
        halo = [jnp.where(first, 0.0, r[...]) for r in (qh_ref, kh_ref, vh_ref)]
        y, st2 = gd_chunk(halo[0], q_ref[...], halo[1], k_ref[...], halo[2], v_ref[...], z_ref[...], sm_ref[...],
                          cw_ref[...], al_ref[...], dt_ref[...], nw_ref[...], st_s[...])
        y_ref[...] = y
        st_s[...] = st2
        _carry_close(xf, i, nc, x_refs, land_refs, sem_refs)

    qkv_specs = []
    for sg in GD_SEGS:
        qkv_specs += [_halo_spec(LR, sg), _seg_spec(LR, sg)]
    return pl.pallas_call(
        body, grid=(nc,),
        in_specs=qkv_specs + [_seg_spec(LR, 10), pl.BlockSpec((LR, HD), lambda i: (i, SMALL_BLK)),
                              _const_spec((4, 3 * SEG)), _const_spec((1, HD)), _const_spec((1, HD)), _const_spec((1, HD))]
        + x_in_specs,
        out_specs=[pl.BlockSpec((LR, SEG), lambda i: (i, 0)), pl.BlockSpec((1, SEG, HD), lambda i: (i, 0, 0))] + x_out_specs,
        out_shape=[jax.ShapeDtypeStruct((s, SEG), f32), jax.ShapeDtypeStruct((nc, SEG, HD), f32)] + x_out_shapes,
        scratch_shapes=[pltpu.VMEM((SEG, HD), f32)] + x_sems, input_output_aliases=aliases,
        name="gdn_fwd" if xf is None else "gdn_fwd_carrying")(
            proj, proj, proj, proj, proj, proj, proj, proj, conv_w, alog_row, dt_row, norm_w, *x_ops)


def gdn_bwd(proj, dy, ss, conv_w, alog_row, dt_row, norm_w, xf=None):
    s = proj.shape[0]
    nc = s // LR
    n_in, n_out = 14, 6
    x_ops, x_in_specs, x_out_shapes, x_out_specs, aliases, x_sems, n_xin, n_land = _carried(xf, n_in, n_out)

    def body(*refs):
        (qh_ref, q_ref, kh_ref, k_ref, vh_ref, v_ref, z_ref, sm_ref, dy_ref, ss_ref,
         cw_ref, al_ref, dt_ref, nw_ref) = refs[:n_in]
        x_refs = refs[n_in:n_in + n_xin]
        dp_ref, dsm_ref, dcw_ref, dal_ref, ddt_ref, dnw_ref = refs[n_in + n_xin:n_in + n_xin + n_out]
        land_refs = refs[n_in + n_xin + n_out:n_in + n_xin + n_out + n_land]
        dst_s, dhalo = refs[n_in + n_xin + n_out + n_land:n_in + n_xin + n_out + n_land + 2]
        sem_refs = refs[n_in + n_xin + n_out + n_land + 2:]
        i = pl.program_id(0)
        first_chunk = i == nc - 1
        _carry_open(xf, i, nc, x_refs, land_refs, sem_refs)

        @pl.when(i == 0)
        def _():
            for ref in (dcw_ref, dal_ref, ddt_ref, dnw_ref, dst_s, dhalo):
                ref[...] = jnp.zeros_like(ref)

        halo = [jnp.where(first_chunk, 0.0, r[...]) for r in (qh_ref, kh_ref, vh_ref)]
        _, vjp = jax.vjp(gd_chunk, halo[0], q_ref[...], halo[1], k_ref[...], halo[2], v_ref[...], z_ref[...], sm_ref[...],
                         cw_ref[...], al_ref[...], dt_ref[...], nw_ref[...], ss_ref[0])
        dqh, dq, dkh, dk, dvh, dv, dz, dsm, dcw, dal, ddt, dnw, dst = vjp((dy_ref[...], dst_s[...]))
        for j, val in enumerate((dq, dk, dv, dz)):
            dp_ref[:, j * SEG:(j + 1) * SEG] = val
        dp_ref[LR - 8:LR, 0:3 * SEG] += dhalo[...]
        for j, val in enumerate((dqh, dkh, dvh)):
            dhalo[:, j * SEG:(j + 1) * SEG] = val
        dsm_ref[...] = dsm
        dcw_ref[...] += dcw
        dal_ref[0:1, :] += dal
        ddt_ref[0:1, :] += ddt
        dnw_ref[0:1, :] += dnw
        dst_s[...] = dst
        _carry_close(xf, i, nc, x_refs, land_refs, sem_refs)

    qkv_specs = []
    for sg in GD_SEGS:
        qkv_specs += [_halo_spec(LR, sg, nc), _seg_spec(LR, sg, nc)]
    return pl.pallas_call(
        body, grid=(nc,),
        in_specs=qkv_specs + [_seg_spec(LR, 10, nc), pl.BlockSpec((LR, HD), lambda i: (nc - 1 - i, SMALL_BLK)),
                              pl.BlockSpec((LR, SEG), lambda i: (nc - 1 - i, 0)),
                              pl.BlockSpec((1, SEG, HD), lambda i: (nc - 1 - i, 0, 0)),
                              _const_spec((4, 3 * SEG)), _const_spec((1, HD)), _const_spec((1, HD)), _const_spec((1, HD))]
        + x_in_specs,
        out_specs=[pl.BlockSpec((LR, 4 * SEG), lambda i: (nc - 1 - i, 0)), pl.BlockSpec((LR, HD), lambda i: (nc - 1 - i, 0)),
                   _const_spec((4, 3 * SEG)), _const_spec((8, HD)), _const_spec((8, HD)), _const_spec((8, HD))] + x_out_specs,
        out_shape=[jax.ShapeDtypeStruct((s, 4 * SEG), f32), jax.ShapeDtypeStruct((s, HD), f32),
                   jax.ShapeDtypeStruct((4, 3 * SEG), f32), jax.ShapeDtypeStruct((8, HD), f32),
                   jax.ShapeDtypeStruct((8, HD), f32), jax.ShapeDtypeStruct((8, HD), f32)] + x_out_shapes,
        scratch_shapes=[pltpu.VMEM((SEG, HD), f32), pltpu.VMEM((8, 3 * SEG), f32)] + x_sems, input_output_aliases=aliases,
        name="gdn_bwd" if xf is None else "gdn_bwd_carrying", compiler_params=_cp(48))(
            proj, proj, proj, proj, proj, proj, proj, proj, dy, ss, conv_w, alog_row, dt_row, norm_w, *x_ops)


def _my_place():
    return lax.axis_index("x"), lax.axis_index("y"), lax.axis_index("c")


def _slot(p):
    return 4 * p[0] + 2 * p[1] + p[2]


def _peer(me, j):
    flips = ((j >> 2) & 1, (j >> 1) & 1, j & 1)
    return tuple((1 - v) if fl else v for v, fl in zip(me, flips))


_ANY = pl.BlockSpec(memory_space=pl.ANY)


class Scatter(NamedTuple):
    srcs: list
    lands: list
    land_of: list
    layer: list
    span: list


class Gather2(NamedTuple):
    srcs: list
    lands: list


def _carry_plumb(cx, n_in, n_out):
    if cx is None:
        return [], [], {}, [], 0, 0, 0
    n_src = len(cx.srcs)
    if isinstance(cx, Gather2):
        operands, aliases = list(cx.srcs), {}
    else:
        operands = list(cx.srcs) + list(cx.lands)
        aliases = {n_in + n_src + li: n_out + li for li in range(len(cx.lands))}
    out_shapes = [jax.ShapeDtypeStruct(ld.shape, ld.dtype) for ld in cx.lands]
    sems = [pltpu.SemaphoreType.DMA((n_src, N_DEV - 1)), pltpu.SemaphoreType.DMA((n_src, N_DEV - 1)),
            pltpu.SemaphoreType.DMA((n_src,))]
    return operands, out_shapes, aliases, sems, n_src, len(operands), len(cx.lands)


def _scatter_copies(sc, src_refs, land_refs, send_sems, recv_sems, local_sems):
    me = _my_place()
    mine = _slot(me)
    local, remote = [], []
    for a, src_ref in enumerate(src_refs):
        lo, hi = sc.span[a]
        rows = pl.ds(lo, hi - lo)
        dst = land_refs[sc.land_of[a]].at[mine, sc.layer[a], rows]
        local.append(pltpu.make_async_copy(src_ref.at[mine, rows], dst, local_sems.at[a]))
        for j in range(1, N_DEV):
            peer = _peer(me, j)
            remote.append(pltpu.make_async_remote_copy(
                src_ref=src_ref.at[_slot(peer), rows], dst_ref=dst, send_sem=send_sems.at[a, j - 1],
                recv_sem=recv_sems.at[a, j - 1], device_id=peer, device_id_type=MESH))
    return local, remote


def _gather2_copy(land, sems, a, k, block_of, to, src=None):
    dst = land.at[_slot(block_of)]
    return pltpu.make_async_remote_copy(src_ref=dst if src is None else src, dst_ref=dst, send_sem=sems[0].at[a, k],
                                        recv_sem=sems[1].at[a, k], device_id=to, device_id_type=MESH)


def _gather2_places():
    x, y, c = _my_place()
    return (x, y, c), (x, y, 1 - c), [(1 - x, y), (x, 1 - y), (1 - x, 1 - y)], c


def _carry_start(cx, src_refs, land_refs, sems):
    if isinstance(cx, Scatter):
        local, remote = _scatter_copies(cx, src_refs, land_refs, *sems)
        for cp in local + remote:
            cp.start()
        return
    me, sib, chips, c = _gather2_places()
    for a, (src, land) in enumerate(zip(src_refs, land_refs)):
        pltpu.make_async_copy(src, land.at[_slot(me)], sems[2].at[a]).start()
        _gather2_copy(land, sems, a, 0, me, sib, src=src).start()
        for j, chip in enumerate(chips):
            _gather2_copy(land, sems, a, 1 + j, me, (*chip, c), src=src).start()


def _carry_middle(cx, src_refs, land_refs, sems):
    if isinstance(cx, Scatter):
        return
    me, sib, chips, c = _gather2_places()
    for a, land in enumerate(land_refs):
        for j, chip in enumerate(chips):
            _gather2_copy(land, sems, a, 1 + j, (*chip, c), me).wait_recv()
            _gather2_copy(land, sems, a, 4 + j, (*chip, c), sib).start()


def _carry_finish(cx, src_refs, land_refs, sems):
    if isinstance(cx, Scatter):
        local, remote = _scatter_copies(cx, src_refs, land_refs, *sems)
        for cp in remote:
            cp.wait()
        for cp in local:
            cp.wait()
        return
    me, sib, chips, c = _gather2_places()
    for a, (src, land) in enumerate(zip(src_refs, land_refs)):
        _gather2_copy(land, sems, a, 0, sib, me).wait_recv()
        for j, chip in enumerate(chips):
            _gather2_copy(land, sems, a, 4 + j, (*chip, 1 - c), me).wait_recv()
        _gather2_copy(land, sems, a, 0, me, sib, src=src).wait_send()
        for j, chip in enumerate(chips):
            _gather2_copy(land, sems, a, 1 + j, me, (*chip, c), src=src).wait_send()
            _gather2_copy(land, sems, a, 4 + j, (*chip, c), sib).wait_send()
        pltpu.make_async_copy(src, land.at[_slot(me)], sems[2].at[a]).wait()


def exchange(cx, name):
    operands, out_shapes, aliases, sems, n_src, n_xin, n_land = _carry_plumb(cx, 0, 0)

    def body(*refs):
        src_refs, land_refs, sem_refs = refs[:n_src], refs[n_xin:n_xin + n_land], refs[n_xin + n_land:]
        _carry_start(cx, src_refs, land_refs, sem_refs)
        _carry_middle(cx, src_refs, land_refs, sem_refs)
        _carry_finish(cx, src_refs, land_refs, sem_refs)

    return pl.pallas_call(body, in_specs=[_ANY] * n_xin, out_specs=[_ANY] * n_land, out_shape=out_shapes,
                          scratch_shapes=sems, input_output_aliases=aliases, name=name)(*operands)


def small_allreduce(packed, name):
    r = packed.shape[0]
    rc = r // N_DEV

    def body(in_ref, out_ref, recv_buf, send1, recv1, send2, recv2):
        me = _my_place()
        mine = _slot(me)
        my_rows = pl.ds(pl.multiple_of(mine * rc, 8), rc)
        first, second = [], []
        for j in range(1, N_DEV):
            peer = _peer(me, j)
            peer_rows = pl.ds(pl.multiple_of(_slot(peer) * rc, 8), rc)
            first.append(pltpu.make_async_remote_copy(
                src_ref=in_ref.at[peer_rows], dst_ref=recv_buf.at[mine], send_sem=send1.at[j - 1], recv_sem=recv1.at[j - 1],
                device_id=peer, device_id_type=MESH))
            second.append(pltpu.make_async_remote_copy(
                src_ref=out_ref.at[my_rows], dst_ref=out_ref.at[my_rows], send_sem=send2.at[j - 1], recv_sem=recv2.at[j - 1],
                device_id=peer, device_id_type=MESH))
        for cp in first:
            cp.start()
        recv_buf[mine] = in_ref[my_rows]
        for cp in first:
            cp.wait()
        acc = recv_buf[0]
        for k in range(1, N_DEV):
            acc = acc + recv_buf[k]
        out_ref[my_rows] = acc
        for cp in second:
            cp.start()
        for cp in second:
            cp.wait()

    vmem = pl.BlockSpec(memory_space=pltpu.VMEM)
    return pl.pallas_call(
        body, in_specs=[vmem], out_specs=vmem, out_shape=jax.ShapeDtypeStruct((r, 128), f32),
        scratch_shapes=[pltpu.VMEM((N_DEV, rc, 128), f32)] + [pltpu.SemaphoreType.DMA((N_DEV - 1,))] * 4,
        name=name, compiler_params=_cp(32))(packed)


def _adamw(w, g, m, v):
    m = ADAM_B1 * m + (1.0 - ADAM_B1) * g
    v = ADAM_B2 * v + (1.0 - ADAM_B2) * (g * g)
    m_hat = m / (1.0 - ADAM_B1 ** ADAM_STEP)
    v_hat = v / (1.0 - ADAM_B2 ** ADAM_STEP)
    delta = -ADAM_LR * (m_hat / (jnp.sqrt(v_hat) + ADAM_EPS) + ADAM_WD * w)
    return delta, m, v


def adam_slots(slots, w, m, v, rows, name):
    _, depth, r, c = slots.shape

    def body(s_ref, w_ref, m_ref, v_ref, g_ref, d_ref, m2_ref, v2_ref):
        g = s_ref[0, 0].astype(f32)
        for k in range(1, N_DEV):
            g = g + s_ref[k, 0].astype(f32)
        d, m2, v2 = _adamw(w_ref[0], g, m_ref[0], v_ref[0])
        g_ref[0] = g
        d_ref[0] = d
        m2_ref[0] = m2
        v2_ref[0] = v2

    blk = pl.BlockSpec((1, rows, c), lambda l, i: (l, i, 0))
    sh = jax.ShapeDtypeStruct((depth, r, c), f32)
    return pl.pallas_call(
        body, grid=(depth, r // rows),
        in_specs=[pl.BlockSpec((N_DEV, 1, rows, c), lambda l, i: (0, l, i, 0)), blk, blk, blk],
        out_specs=[blk] * 4, out_shape=[sh] * 4, name=name, compiler_params=_cp(40))(slots, w, m, v)


def adam_flat(g, w, m, v, name):
    def body(g_ref, w_ref, m_ref, v_ref, d_ref, m2_ref, v2_ref):
        d, m2, v2 = _adamw(w_ref[...], g_ref[...], m_ref[...], v_ref[...])
        d_ref[...] = d
        m2_ref[...] = m2
        v2_ref[...] = v2

    sh = jax.ShapeDtypeStruct(g.shape, f32)
    return pl.pallas_call(body, out_shape=[sh] * 3, name=name)(g, w, m, v)


def _rows_of(shape):
    n = 1
    for dim in shape:
        n *= dim
    return n, -(-n // (8 * 128)) * 8


def _pack(arrs):
    parts = []
    for a in arrs:
        n, rows = _rows_of(a.shape)
        parts.append(jnp.pad(a.reshape(-1).astype(f32), (0, rows * 128 - n)).reshape(rows, 128))
    return jnp.concatenate(parts, axis=0)


def _unpack(packed, shapes):
    out, row = [], 0
    for sh in shapes:
        n, rows = _rows_of(sh)
        out.append(packed[row:row + rows].reshape(-1)[:n].reshape(sh))
        row += rows
    return out


def _block_diag(gw):
    eye = jnp.eye(8, dtype=gw.dtype)
    return (gw[:, :, None, :] * eye[:, None, :, None]).reshape(SEG, SEG)


def _diag_blocks(dense):
    eye = jnp.eye(8, dtype=dense.dtype)
    return (dense.reshape(8, 64, 8, 64) * eye[:, None, :, None]).sum(axis=2)


def _lane_row(vals, first_lane):
    return jnp.zeros((1, HD), f32).at[0, first_lane:first_lane + NH].set(vals)


def kernel(x, norm_w, w_in, rg_conv_w, rg_conv_b, rg_gate_w, rg_gate_b, rg_lambda, ml_gate_b, ml_norm_w, gd_conv_w, gd_a_log, gd_dt_bias, gd_norm_w, w_out, final_norm_w, loss_target, m_norm_w, m_w_in, m_rg_conv_w, m_rg_conv_b, m_rg_gate_w, m_rg_gate_b, m_rg_lambda, m_ml_gate_b, m_ml_norm_w, m_gd_conv_w, m_gd_a_log, m_gd_dt_bias, m_gd_norm_w, m_w_out, m_final_norm_w, v_norm_w, v_w_in, v_rg_conv_w, v_rg_conv_b, v_rg_gate_w, v_rg_gate_b, v_rg_lambda, v_ml_gate_b, v_ml_norm_w, v_gd_conv_w, v_gd_a_log, v_gd_dt_bias, v_gd_norm_w, v_w_out, v_final_norm_w):
    s = x.shape[1]
    xs = x.reshape(s, D)
    tgt = loss_target.reshape(s, D)
    me = 4 * lax.axis_index("x") + 2 * lax.axis_index("y") + lax.axis_index("c")

    comm = MeshComm(w_in, w_out, [rg_conv_w, rg_gate_b, gd_conv_w])
    rg_conv_full, rg_gb_full, gd_conv_full = comm.small_weights
    loss_part, dx, d_fw, g_small = local_step(
        xs, tgt, comm, rg_conv_full, rg_gb_full, gd_conv_full, norm_w, rg_conv_b, rg_gate_w, rg_lambda,
        ml_gate_b, ml_norm_w, gd_a_log, gd_dt_bias, gd_norm_w, final_norm_w)
    given_w = dict(norm_w=norm_w, rg_conv_w=rg_conv_w, rg_conv_b=rg_conv_b, rg_gate_w=rg_gate_w, rg_gate_b=rg_gate_b,
                   rg_lambda=rg_lambda, ml_gate_b=ml_gate_b, ml_norm_w=ml_norm_w, gd_conv_w=gd_conv_w, gd_a_log=gd_a_log,
                   gd_dt_bias=gd_dt_bias, gd_norm_w=gd_norm_w, final_norm_w=final_norm_w, w_in=w_in, w_out=w_out)
    given_m = dict(norm_w=m_norm_w, rg_conv_w=m_rg_conv_w, rg_conv_b=m_rg_conv_b, rg_gate_w=m_rg_gate_w, rg_gate_b=m_rg_gate_b,
                   rg_lambda=m_rg_lambda, ml_gate_b=m_ml_gate_b, ml_norm_w=m_ml_norm_w, gd_conv_w=m_gd_conv_w,
                   gd_a_log=m_gd_a_log, gd_dt_bias=m_gd_dt_bias, gd_norm_w=m_gd_norm_w, final_norm_w=m_final_norm_w,
                   w_in=m_w_in, w_out=m_w_out)
    given_v = dict(norm_w=v_norm_w, rg_conv_w=v_rg_conv_w, rg_conv_b=v_rg_conv_b, rg_gate_w=v_rg_gate_w, rg_gate_b=v_rg_gate_b,
                   rg_lambda=v_rg_lambda, ml_gate_b=v_ml_gate_b, ml_norm_w=v_ml_norm_w, gd_conv_w=v_gd_conv_w,
                   gd_a_log=v_gd_a_log, gd_dt_bias=v_gd_dt_bias, gd_norm_w=v_gd_norm_w, final_norm_w=v_final_norm_w,
                   w_in=v_w_in, w_out=v_w_out)
    return finish_step(loss_part, dx, d_fw, g_small, comm, s, me, given_w, given_m, given_v)


def _gathered_pieces():
    per = D_IN // N_DEV
    pieces = []
    for lo, hi in ((0, 3584), (3592, 5640), (3584, 3592), (5640, 5648)):
        col = lo
        while col < hi:
            k = col // per
            end = min(hi, (k + 1) * per)
            pieces.append((k, col - k * per, end - k * per))
            col = end
    return pieces


class MeshComm:
    GWI_SPLIT = 256

    def __init__(self, w_in, w_out, small_shards):
        per = D_IN // N_DEV
        self.wi_sh = [w_in[l].astype(bf16) for l in range(DEPTH)]
        self.wo_sh = [w_out[l].astype(bf16) for l in range(DEPTH)]
        self.wi_land = jax.ShapeDtypeStruct((N_DEV, D, per), bf16)
        self.wo_land = jax.ShapeDtypeStruct((N_DEV, 3 * SEG // N_DEV, D), bf16)
        packed = _pack(small_shards)
        first = Gather2([self.wi_sh[0], self.wo_sh[0], packed],
                        [self.wi_land, self.wo_land, jax.ShapeDtypeStruct((N_DEV,) + packed.shape, f32)])
        wi_g, wo_g, sm_g = exchange(first, "gather_first")
        self.wi_g, self.wo_g = {0: wi_g}, {0: wo_g}
        shapes = [a.shape for a in small_shards]
        parts = [_unpack(sm_g[k], shapes) for k in range(N_DEV)]
        self.small_weights = [jnp.concatenate([p[j] for p in parts], axis=-1) for j in range(len(small_shards))]
        self.gwi_land = lax.empty((N_DEV, DEPTH, D, per), bf16)
        self.gwo_land = lax.empty((N_DEV, DEPTH, 3 * SEG // N_DEV, D), bf16)
        self.gwi_slots = {}
        self.gwo_slots = {}

    def weights(self, l):
        cols = [self.wi_g[l][k, :, a:b] for k, a, b in _gathered_pieces()] + [jnp.zeros((D, DP - D_IN), bf16)]
        return jnp.concatenate(cols, axis=1), self.wo_g[l].reshape(3 * SEG, D)

    def fwd_carry(self, l, host):
        if l + 1 >= DEPTH:
            return None
        if host == "mlstm":
            return Gather2([self.wo_sh[l + 1]], [self.wo_land])
        return Gather2([self.wi_sh[l + 1]], [self.wi_land])

    def fwd_landed(self, l, host, landed):
        (self.wo_g if host == "mlstm" else self.wi_g)[l + 1] = landed[0]

    def own_w_out_grad(self, l, g_wo):
        self.gwo_slots[l] = g_wo.reshape(N_DEV, 3 * SEG // N_DEV, D).astype(bf16)

    def bwd_carry(self, l, host):
        rows_o = 3 * SEG // N_DEV
        srcs, land_of, layer, span = [], [], [], []
        if l + 1 < DEPTH:
            if host == "mlstm":
                srcs += [self.gwo_slots[l + 1], self.gwi_slots[l + 1]]
                land_of, layer, span = [1, 0], [l + 1, l + 1], [(0, rows_o), (0, self.GWI_SPLIT)]
            else:
                srcs, land_of, layer, span = [self.gwi_slots[l + 1]], [0], [l + 1], [(self.GWI_SPLIT, D)]
        if l == 0 and host == "mlstm":
            srcs, land_of, layer, span = srcs + [self.gwo_slots[0]], land_of + [1], layer + [0], span + [(0, rows_o)]
        if not srcs:
            return None
        return Scatter(srcs, [self.gwi_land, self.gwo_land], land_of, layer, span)

    def bwd_landed(self, landed):
        self.gwi_land, self.gwo_land = landed

    def grads_ready(self, l, pieces):
        per = D_IN // N_DEV
        g_wi = jnp.concatenate(pieces, axis=1)
        self.gwi_slots[l] = jnp.stack([g_wi[:, k * per:(k + 1) * per] for k in range(N_DEV)]).astype(bf16)

    def finish(self):
        last = Scatter([self.gwi_slots[0]], [self.gwi_land, self.gwo_land], [0], [0], [(0, D)])
        return exchange(last, "scatter_last")


def local_step(xs, tgt, comm, rg_conv_full, rg_gb_full, gd_conv_full, norm_w, rg_conv_b, rg_gate_w,
               rg_lambda, ml_gate_b, ml_norm_w, gd_a_log, gd_dt_bias, gd_norm_w, final_norm_w):
    acts = []
    for l in range(DEPTH):
        nw = norm_w[l].reshape(1, D)
        w_in_l, w_out_l = comm.weights(l)
        proj = inproj(xs, nw, w_in_l)
        rg_p = (rg_conv_full[l], rg_conv_b[l].reshape(1, SEG), _block_diag(rg_gate_w[l, 0]), _block_diag(rg_gate_w[l, 1]),
                rg_gb_full[l], rg_lambda[l].reshape(1, SEG))
        y_rg, hs = rglru_fwd(proj, *rg_p)
        ml_p = (jnp.zeros((1, HD), f32).at[0, 0:2 * NH].set(ml_gate_b[l].reshape(-1)), ml_norm_w[l].reshape(1, SEG))
        xf = comm.fwd_carry(l, "mlstm")
        y_ml, cs, ns, ms, *landed = mlstm_fwd(proj, *ml_p, xf=xf)
        if xf is not None:
            comm.fwd_landed(l, "mlstm", landed)
        gd_p = (gd_conv_full[l], _lane_row(gd_a_log[l], 2 * NH), _lane_row(gd_dt_bias[l], 2 * NH), gd_norm_w[l].reshape(1, HD))
        xf = comm.fwd_carry(l, "gdn")
        y_gd, ss, *landed = gdn_fwd(proj, *gd_p, xf=xf)
        if xf is not None:
            comm.fwd_landed(l, "gdn", landed)
        acts.append((xs, nw, proj, w_in_l, w_out_l, rg_p, y_rg, hs, ml_p, y_ml, cs, ns, ms, gd_p, y_gd, ss))
        xs = outproj(xs, y_rg, y_ml, y_gd, w_out_l)

    dx, loss_part, d_fw = head(xs, final_norm_w.reshape(1, D), tgt)

    g_small = {k: [None] * DEPTH for k in ("norm_w", "rg_conv_w", "rg_conv_b", "rg_gate_w", "rg_gate_b", "rg_lambda",
                                           "ml_gate_b", "ml_norm_w", "gd_conv_w", "gd_a_log", "gd_dt_bias", "gd_norm_w")}
    for l in reversed(range(DEPTH)):
        x_l, nw, proj, w_in_l, w_out_l, rg_p, y_rg, hs, ml_p, y_ml, cs, ns, ms, gd_p, y_gd, ss = acts[l]
        dy_rg, dy_ml, dy_gd, g_wo = outproj_bwd(dx, y_rg, y_ml, y_gd, w_out_l)
        d_rg, d_cw, d_cb, d_gr, d_gi, d_gb, d_lam = rglru_bwd(proj, hs, dy_rg, *rg_p)
        comm.own_w_out_grad(l, g_wo)
        xf = comm.bwd_carry(l, "mlstm")
        d_ml, d_sm_ml, d_bias, d_mnw, *landed = mlstm_bwd(proj, dy_ml, cs, ns, ms, *ml_p, xf=xf)
        if xf is not None:
            comm.bwd_landed(landed)
        xf = comm.bwd_carry(l, "gdn")
        d_gd, d_sm_gd, d_gcw, d_al, d_dt, d_gnw, *landed = gdn_bwd(proj, dy_gd, ss, *gd_p, xf=xf)
        if xf is not None:
            comm.bwd_landed(landed)
        dx, d_nw = inproj_bwd_x(x_l, nw, w_in_l, dx, d_rg, d_ml, d_gd, d_sm_ml, d_sm_gd)
        gw_rg = wgrad(x_l, nw, [d_rg], "wgrad_rg")
        gw_ml = wgrad(x_l, nw, [d_ml], "wgrad_ml")
        gw_gd = wgrad(x_l, nw, [d_gd], "wgrad_gd")
        gw_sm = wgrad(x_l, nw, [d_sm_ml, d_sm_gd], "wgrad_small")
        comm.grads_ready(l, [gw_rg, gw_ml, gw_sm[:, 0:2 * NH], gw_gd, gw_sm[:, 2 * NH:4 * NH]])
        g_small["norm_w"][l] = d_nw[0]
        g_small["rg_conv_w"][l] = d_cw
        g_small["rg_conv_b"][l] = d_cb[0]
        g_small["rg_gate_w"][l] = jnp.stack([_diag_blocks(d_gr), _diag_blocks(d_gi)])
        g_small["rg_gate_b"][l] = d_gb
        g_small["rg_lambda"][l] = d_lam[0]
        g_small["ml_gate_b"][l] = d_bias[0, 0:2 * NH].reshape(2, NH)
        g_small["ml_norm_w"][l] = d_mnw[0]
        g_small["gd_conv_w"][l] = d_gcw
        g_small["gd_a_log"][l] = d_al[0, 2 * NH:3 * NH]
        g_small["gd_dt_bias"][l] = d_dt[0, 2 * NH:3 * NH]
        g_small["gd_norm_w"][l] = d_gnw[0]
    return loss_part, dx, d_fw, g_small


def finish_step(loss_part, dx, d_fw, g_small, comm, s, me, given_w, given_m, given_v):
    small_names = ["norm_w", "rg_conv_w", "rg_conv_b", "rg_gate_w", "rg_gate_b", "rg_lambda", "ml_gate_b", "ml_norm_w",
                   "gd_conv_w", "gd_a_log", "gd_dt_bias", "gd_norm_w"]
    small_list = [loss_part[0, 0:1], d_fw[0]] + [jnp.stack(g_small[k]) for k in small_names]
    small_shapes = [a.shape for a in small_list]
    packed = _pack(small_list)
    packed = jnp.pad(packed, ((0, -packed.shape[0] % (8 * N_DEV)), (0, 0)))
    g_all = _unpack(small_allreduce(packed, "small_allreduce"), small_shapes)
    gwi_r, gwo_r = comm.finish()

    g_wi, d_wi, m_wi, v_wi = adam_slots(gwi_r, given_w["w_in"], given_m["w_in"], given_v["w_in"], 256, "adam_w_in")
    g_wo, d_wo, m_wo, v_wo = adam_slots(gwo_r, given_w["w_out"], given_m["w_out"], given_v["w_out"], 192, "adam_w_out")
    loss = g_all[0][0]
    grads = {"final_norm_w": g_all[1]}
    for k, g in zip(small_names, g_all[2:]):
        grads[k] = g
    for k, width in (("rg_conv_w", 64), ("rg_gate_b", 64), ("gd_conv_w", 192)):
        grads[k] = lax.dynamic_slice_in_dim(grads[k], me * width, width, axis=2)
    names = small_names + ["final_norm_w"]
    shapes = [given_w[k].shape for k in names]
    d_p, m_p, v_p = adam_flat(_pack([grads[k] for k in names]), _pack([given_w[k] for k in names]),
                              _pack([given_m[k] for k in names]), _pack([given_v[k] for k in names]), "adam_small")
    deltas = dict(zip(names, _unpack(d_p, shapes)))
    new_m = dict(zip(names, _unpack(m_p, shapes)))
    new_v = dict(zip(names, _unpack(v_p, shapes)))
    grads["w_in"], deltas["w_in"], new_m["w_in"], new_v["w_in"] = g_wi, d_wi, m_wi, v_wi
    grads["w_out"], deltas["w_out"], new_m["w_out"], new_v["w_out"] = g_wo, d_wo, m_wo, v_wo

    order = ["norm_w", "w_in", "rg_conv_w", "rg_conv_b", "rg_gate_w", "rg_gate_b", "rg_lambda", "ml_gate_b", "ml_norm_w",
             "gd_conv_w", "gd_a_log", "gd_dt_bias", "gd_norm_w", "w_out", "final_norm_w"]
    return (loss, dx.reshape(1, s, D), *[grads[k] for k in order], *[deltas[k] for k in order],
            *[new_m[k] for k in order], *[new_v[k] for k in order])
```

```python
import functools
from typing import NamedTuple

import jax
import jax.numpy as jnp
from jax import lax
from jax.experimental import pallas as pl
from jax.experimental.pallas import tpu as pltpu

f32 = jnp.float32
bf16 = jnp.bfloat16
MESH = pl.DeviceIdType.MESH

N_DEV = 8
D = 1024
DEPTH = 4
EPS = 1e-6
SEG = 512
HD = 128
NH = 4
LC = 64
RUN = 4
LR = RUN * LC
D_IN = 5648
DP = 5760
SMALL0 = 5632
RG_TILE = 256
RG_C = 8.0

ADAM_LR = 0.001
ADAM_B1 = 0.9
ADAM_B2 = 0.999
ADAM_EPS = 1e-08
ADAM_WD = 0.01
ADAM_STEP = 10


def _cp(vmem_mb):
    return pltpu.CompilerParams(vmem_limit_bytes=vmem_mb * 2 ** 20)


def _dot(a, b, ca, cb):
    return lax.dot_general(a.astype(bf16), b.astype(bf16), (((ca,), (cb,)), ((), ())), preferred_element_type=f32)


@jax.custom_vjp
def mm_nn(a, b):
    return _dot(a, b, 1, 0)


@jax.custom_vjp
def mm_nt(a, b):
    return _dot(a, b, 1, 1)


@jax.custom_vjp
def mm_tn(a, b):
    return _dot(a, b, 0, 0)


mm_nn.defvjp(lambda a, b: (mm_nn(a, b), (a, b)), lambda r, g: (mm_nt(g, r[1]), mm_tn(r[0], g)))
mm_nt.defvjp(lambda a, b: (mm_nt(a, b), (a, b)), lambda r, g: (mm_nn(g, r[1]), mm_tn(g, r[0])))
mm_tn.defvjp(lambda a, b: (mm_tn(a, b), (a, b)), lambda r, g: (mm_nt(r[1], g), mm_nn(r[0], g)))


def _split(x):
    hi = x.astype(bf16)
    return hi, (x - hi.astype(f32)).astype(bf16)


def dot3(a, b):
    ah, al = _split(a)
    bh, bl = _split(b)
    d = functools.partial(jnp.dot, preferred_element_type=f32)
    return d(ah, bh) + (d(al, bh) + d(ah, bl))


def _tri_sum(x, reverse):
    n = x.shape[0]
    r = lax.broadcasted_iota(jnp.int32, (n, 3 * n), 0)
    c = lax.broadcasted_iota(jnp.int32, (n, 3 * n), 1) & (n - 1)
    ones = ((c >= r) if reverse else (c <= r)).astype(bf16)
    hi = x.astype(bf16)
    rest = x - hi.astype(f32)
    mid = rest.astype(bf16)
    lo = (rest - mid.astype(f32)).astype(bf16)
    return jnp.dot(ones, jnp.concatenate([hi, mid, lo], axis=0), preferred_element_type=f32)


@jax.custom_vjp
def cumsum_rows(x):
    return _tri_sum(x, False)


@jax.custom_vjp
def rev_cumsum_rows(x):
    return _tri_sum(x, True)


cumsum_rows.defvjp(lambda x: (cumsum_rows(x), None), lambda _, g: (rev_cumsum_rows(g),))
rev_cumsum_rows.defvjp(lambda x: (rev_cumsum_rows(x), None), lambda _, g: (cumsum_rows(g),))


def _tri(n, strict=False):
    r = lax.broadcasted_iota(jnp.int32, (n, n), 0)
    c = lax.broadcasted_iota(jnp.int32, (n, n), 1)
    return (r > c) if strict else (r >= c)


def _lane_col(v, j):
    lane = lax.broadcasted_iota(jnp.int32, v.shape, 1)
    return jnp.sum(jnp.where(lane == j, v, 0.0), axis=1, keepdims=True)


def _rows_from(pieces, rows, width):
    ridx = lax.broadcasted_iota(jnp.int32, (rows, width), 0)
    out = jnp.zeros((rows, width), f32)
    for h, p in enumerate(pieces):
        out = out + jnp.where(ridx == h, jnp.broadcast_to(p, (rows, width)), 0.0)
    return out


def causal_conv(halo8, x, w4):
    t = x.shape[0]
    xe = jnp.concatenate([halo8, x], axis=0)
    y = xe[5:5 + t] * w4[0:1]
    for k in range(1, 4):
        y = y + xe[5 + k:5 + k + t] * w4[k:k + 1]
    return y


def ml_chunk(q, k, v, o_pre, z, small, bias_row, norm_w, C, n, m):
    n_ch = q.shape[0] // LC
    lane = lax.broadcasted_iota(jnp.int32, small.shape, 1)
    pre = small + bias_row
    lg = jnp.where(lane < NH, pre, jnp.where(lane < 2 * NH, jax.nn.log_sigmoid(pre), 0.0))
    rows = [slice(c * LC, (c + 1) * LC) for c in range(n_ch)]
    lgs = [lg[r] for r in rows]
    bcs = [cumsum_rows(x) for x in lgs]
    lg_ts = [x.T for x in lgs]
    bc_ts = [x.T for x in bcs]
    causal = _tri(LC)
    pairs = [(c, h) for c in range(n_ch) for h in range(NH)]
    idx = range(len(pairs))
    sls = [slice(h * HD, (h + 1) * HD) for h in range(NH)]
    qs = [q[rows[c], sls[h]] * (HD ** -0.5) for c, h in pairs]
    ks = [k[rows[c], sls[h]] for c, h in pairs]
    vs = [v[rows[c], sls[h]] for c, h in pairs]
    li_cols = [_lane_col(lgs[c], h) for c, h in pairs]
    b_cols = [_lane_col(bcs[c], NH + h) for c, h in pairs]
    dms = [jnp.where(causal, b_cols[i] - bc_ts[c][NH + h:NH + h + 1, :] + lg_ts[c][h:h + 1, :], -jnp.inf)
           for i, (c, h) in enumerate(pairs)]
    dm_maxs = [jnp.max(dms[i], axis=-1, keepdims=True) for i in idx]
    gs = [b_cols[i][LC - 1:LC, :] for i in idx]
    wss = [gs[i] - b_cols[i] + li_cols[i] for i in idx]
    ws_maxs = [jnp.max(wss[i], axis=0, keepdims=True) for i in idx]
    qks = [mm_nt(qs[i], ks[i]) for i in idx]
    m_in = [None] * len(pairs)
    m_out = [None] * len(pairs)
    for h in range(NH):
        cur = m[h:h + 1, 0:1]
        for c in range(n_ch):
            i = c * NH + h
            m_in[i] = cur
            cur = jnp.maximum(gs[i] + cur, ws_maxs[i])
            m_out[i] = cur
    m_inters = [b_cols[i] + m_in[i] for i in idx]
    m_ts = [jnp.maximum(m_inters[i], dm_maxs[i]) for i in idx]
    ss = [qks[i] * jnp.exp(dms[i] - m_ts[i]) for i in idx]
    scs = [jnp.exp(m_inters[i] - m_ts[i]) for i in idx]
    decs = [jnp.exp(gs[i] + m_in[i] - m_out[i]) for i in idx]
    kws = [jnp.exp(wss[i] - m_out[i]) * ks[i] for i in idx]
    c_adds = [mm_tn(kws[i], vs[i]) for i in idx]
    n_adds = [jnp.sum(kws[i], axis=0, keepdims=True) for i in idx]
    svs = [mm_nn(ss[i], vs[i]) for i in idx]
    s_sums = [jnp.sum(ss[i], axis=-1, keepdims=True) for i in idx]
    c_hs = [C[sl, :] for sl in sls]
    n_hs = [n[h:h + 1, :] for h in range(NH)]
    hhs = [None] * len(pairs)
    for c in range(n_ch):
        for h in range(NH):
            i = c * NH + h
            num = svs[i] + scs[i] * mm_nn(qs[i], c_hs[h])
            den = s_sums[i] + scs[i] * jnp.sum(qs[i] * n_hs[h], axis=-1, keepdims=True)
            hhs[i] = num / jnp.maximum(jnp.abs(den), jnp.exp(-m_ts[i]))
        c_hs = [decs[c * NH + h] * c_hs[h] + c_adds[c * NH + h] for h in range(NH)]
        n_hs = [decs[c * NH + h] * n_hs[h] + n_adds[c * NH + h] for h in range(NH)]
    ys = [hhs[i] * lax.rsqrt(jnp.mean(hhs[i] * hhs[i], axis=-1, keepdims=True) + EPS) * norm_w[:, sls[h]]
          * jax.nn.sigmoid(o_pre[rows[c], sls[h]]) * jax.nn.silu(z[rows[c], sls[h]]) for i, (c, h) in enumerate(pairs)]
    y = jnp.concatenate([jnp.concatenate(ys[c * NH:(c + 1) * NH], axis=1) for c in range(n_ch)], axis=0)
    last = (n_ch - 1) * NH
    m_rows = [jnp.broadcast_to(m_out[last + h], (1, HD)) for h in range(NH)]
    return y, jnp.concatenate(c_hs, axis=0), _rows_from(n_hs, 8, HD), _rows_from(m_rows, 8, HD)


@jax.custom_vjp
def _unit_lower_inverses(mats):
    eye = (lax.broadcasted_iota(jnp.int32, (LC, LC), 0) == lax.broadcasted_iota(jnp.int32, (LC, LC), 1)).astype(f32)
    ps = [-m for m in mats]
    ts = [eye + p for p in ps]
    for _ in range(5):
        ps = [dot3(p, p) for p in ps]
        ts = [t + dot3(t, p) for t, p in zip(ts, ps)]
    return tuple(ts)


def _unit_lower_inverses_fwd(mats):
    ts = _unit_lower_inverses(mats)
    return ts, ts


def _unit_lower_inverses_bwd(ts, gs):
    tts = [t.T for t in ts]
    mid = [dot3(tt, g) for tt, g in zip(tts, gs)]
    return (tuple(-dot3(m, tt) for m, tt in zip(mid, tts)),)


_unit_lower_inverses.defvjp(_unit_lower_inverses_fwd, _unit_lower_inverses_bwd)


def gd_chunk(qh8, q, kh8, k, vh8, v, z, small, conv_w, alog_row, dt_row, norm_w, st):
    n_ch = q.shape[0] // LC
    lane = lax.broadcasted_iota(jnp.int32, small.shape, 1)
    is_a = (lane >= 2 * NH) & (lane < 3 * NH)
    g_all = jnp.where(is_a, -jnp.exp(alog_row) * jax.nn.softplus(small + dt_row), 0.0)
    beta_all = jax.nn.sigmoid(small)
    qc = jax.nn.silu(causal_conv(qh8, q, conv_w[:, 0:SEG]))
    kc = jax.nn.silu(causal_conv(kh8, k, conv_w[:, SEG:2 * SEG]))
    vc = jax.nn.silu(causal_conv(vh8, v, conv_w[:, 2 * SEG:3 * SEG]))
    incl = _tri(LC)
    strict = _tri(LC, strict=True)
    rows = [slice(c * LC, (c + 1) * LC) for c in range(n_ch)]
    gc_alls = [cumsum_rows(g_all[r]) for r in rows]
    gc_ts = [g.T for g in gc_alls]
    pairs = [(c, h) for c in range(n_ch) for h in range(NH)]
    idx = range(len(pairs))
    sls = [slice(h * HD, (h + 1) * HD) for h in range(NH)]
    qs = [qc[rows[c], sls[h]] for c, h in pairs]
    ks = [kc[rows[c], sls[h]] for c, h in pairs]
    vs = [vc[rows[c], sls[h]] for c, h in pairs]
    qs = [x * lax.rsqrt(jnp.sum(x * x, axis=-1, keepdims=True) + EPS) * (HD ** -0.5) for x in qs]
    ks = [x * lax.rsqrt(jnp.sum(x * x, axis=-1, keepdims=True) + EPS) for x in ks]
    betas = [_lane_col(beta_all[rows[c]], 3 * NH + h) for c, h in pairs]
    gcs = [_lane_col(gc_alls[c], 2 * NH + h) for c, h in pairs]
    gams = [jnp.exp(jnp.where(incl, gcs[i] - gc_ts[c][2 * NH + h:2 * NH + h + 1, :], -jnp.inf))
            for i, (c, h) in enumerate(pairs)]
    kbs = [ks[i] * betas[i] for i in idx]
    mats = tuple(jnp.where(strict, mm_nt(kbs[i], ks[i]) * gams[i], 0.0) for i in idx)
    aqks = [mm_nt(qs[i], ks[i]) * gams[i] for i in idx]
    t_invs = _unit_lower_inverses(mats)
    us = [mm_nn(t_invs[i], vs[i] * betas[i]) for i in idx]
    ws = [mm_nn(t_invs[i], kbs[i] * jnp.exp(gcs[i])) for i in idx]
    g_lasts = [gcs[i][LC - 1:LC, :] for i in idx]
    q_decs = [qs[i] * jnp.exp(gcs[i]) for i in idx]
    k_decs = [ks[i] * jnp.exp(g_lasts[i] - gcs[i]) for i in idx]
    e_lasts = [jnp.exp(g_lasts[i]) for i in idx]
    s_hs = [st[sl, :] for sl in sls]
    os_ = [None] * len(pairs)
    for c in range(n_ch):
        ids = [c * NH + h for h in range(NH)]
        v_news = [us[i] - mm_nn(ws[i], s_hs[h]) for h, i in enumerate(ids)]
        for h, i in enumerate(ids):
            os_[i] = mm_nn(q_decs[i], s_hs[h]) + mm_nn(aqks[i], v_news[h])
        s_hs = [s_hs[h] * e_lasts[i] + mm_tn(k_decs[i], v_news[h]) for h, i in enumerate(ids)]
    ys = [os_[i] * lax.rsqrt(jnp.mean(os_[i] * os_[i], axis=-1, keepdims=True) + EPS) * norm_w
          * jax.nn.silu(z[rows[c], sls[h]]) for i, (c, h) in enumerate(pairs)]
    y = jnp.concatenate([jnp.concatenate(ys[c * NH:(c + 1) * NH], axis=1) for c in range(n_ch)], axis=0)
    return y, jnp.concatenate(s_hs, axis=0)


def rg_pre(xh8, x, conv_w, conv_b, gw_r, gw_i, gate_b, lam):
    xc = causal_conv(xh8, x, conv_w) + conv_b
    r = jax.nn.sigmoid(mm_nn(xc, gw_r) + gate_b[0:1])
    i = jax.nn.sigmoid(mm_nn(xc, gw_i) + gate_b[1:2])
    log_a = -RG_C * r * jax.nn.softplus(-lam)
    a = jnp.exp(log_a)
    th = jnp.tanh(log_a)
    one_minus_a2 = -2.0 * th / (1.0 - th)
    b = jnp.sqrt(one_minus_a2) * (i * xc)
    return a, b


def _scan_rows(a_ref, b_ref, out_ref, h0, n_rows, reverse):
    n_groups = n_rows // 8
    width = a_ref.shape[1]
    row = lax.broadcasted_iota(jnp.int32, (8, width), 0)

    def body(j, h):
        g = (n_groups - 1 - j) if reverse else j
        r0 = pl.multiple_of(g * 8, 8)
        av = a_ref[pl.ds(r0, 8), :]
        bv = b_ref[pl.ds(r0, 8), :]
        for d in (1, 2, 4):
            sh = (8 - d) if reverse else d
            a_s = pltpu.roll(av, sh, 0)
            b_s = pltpu.roll(bv, sh, 0)
            valid = (row < 8 - d) if reverse else (row >= d)
            bv = jnp.where(valid, av * b_s + bv, bv)
            av = jnp.where(valid, av * a_s, av)
        hv = av * h + bv
        out_ref[pl.ds(r0, 8), :] = hv
        return hv[0:1, :] if reverse else hv[7:8, :]

    return lax.fori_loop(0, n_groups, body, h0)


def _norm_rows(xv):
    r = lax.rsqrt(jnp.mean(xv * xv, axis=-1, keepdims=True) + EPS)
    return xv * r, r


def inproj(x, nw, w, xf=None):
    s, ts = x.shape[0], 256
    nt = s // ts
    n_in, n_out = 3, 2
    x_ops, x_in_specs, x_out_shapes, x_out_specs, aliases, x_sems, n_xin, n_land = _carried(xf, n_in, n_out)

    def body(*refs):
        (x_ref, nw_ref, w_ref), x_refs, (o_ref, ht_ref), land_refs, _, sem_refs = _cut_refs(refs, n_in, n_xin, n_out, n_land, 0)
        i = pl.program_id(0)
        _carry_open(xf, i, nt, x_refs, land_refs, sem_refs)
        xn, _ = _norm_rows(x_ref[...])
        hn = xn * nw_ref[...]
        o_ref[...] = jnp.dot(hn.astype(bf16), w_ref[...], preferred_element_type=f32)
        ht_ref[...] = hn.T.astype(bf16)
        _carry_close(xf, i, nt, x_refs, land_refs, sem_refs)

    return pl.pallas_call(
        body, grid=(nt,),
        in_specs=[pl.BlockSpec((ts, D), lambda i: (i, 0)), pl.BlockSpec((1, D), lambda i: (0, 0)),
                  pl.BlockSpec((D, DP), lambda i: (0, 0))] + x_in_specs,
        out_specs=[pl.BlockSpec((ts, DP), lambda i: (i, 0)), pl.BlockSpec((D, ts), lambda i: (0, i))] + x_out_specs,
        out_shape=[jax.ShapeDtypeStruct((s, DP), f32), jax.ShapeDtypeStruct((D, s), bf16)] + x_out_shapes,
        scratch_shapes=x_sems, input_output_aliases=aliases,
        name="inproj" if xf is None else "inproj_carrying", compiler_params=_cp(56))(x, nw, w, *x_ops)


def outproj(x, yr, ym, yg, wo):
    s, ts = x.shape[0], 256

    def body(x_ref, yr_ref, ym_ref, yg_ref, wo_ref, o_ref):
        acc = x_ref[...]
        for j, y_ref in enumerate((yr_ref, ym_ref, yg_ref)):
            acc = acc + jnp.dot(y_ref[...].astype(bf16), wo_ref[j * SEG:(j + 1) * SEG, :], preferred_element_type=f32)
        o_ref[...] = acc

    yspec = pl.BlockSpec((ts, SEG), lambda i: (i, 0))
    return pl.pallas_call(
        body, grid=(s // ts,),
        in_specs=[pl.BlockSpec((ts, D), lambda i: (i, 0)), yspec, yspec, yspec,
                  pl.BlockSpec((3 * SEG, D), lambda i: (0, 0))],
        out_specs=pl.BlockSpec((ts, D), lambda i: (i, 0)),
        out_shape=jax.ShapeDtypeStruct((s, D), f32), name="outproj", compiler_params=_cp(40))(x, yr, ym, yg, wo)


def head(x, fw, tgt):
    s, ts = x.shape[0], 256

    def body(x_ref, fw_ref, t_ref, dx_ref, loss_ref, dfw_ref):
        @pl.when(pl.program_id(0) == 0)
        def _():
            loss_ref[...] = jnp.zeros_like(loss_ref)
            dfw_ref[...] = jnp.zeros_like(dfw_ref)

        xn, r = _norm_rows(x_ref[...])
        fw_v = fw_ref[...]
        err = xn * fw_v - t_ref[...]
        loss_ref[...] += 0.5 * jnp.sum(jnp.mean(err * err, axis=-1, keepdims=True))
        dy = err * (1.0 / D)
        dfw_ref[0:1, :] += jnp.sum(dy * xn, axis=0, keepdims=True)
        dxn = dy * fw_v
        dx_ref[...] = r * (dxn - xn * jnp.mean(dxn * xn, axis=-1, keepdims=True))

    tile = pl.BlockSpec((ts, D), lambda i: (i, 0))
    return pl.pallas_call(
        body, grid=(s // ts,),
        in_specs=[tile, pl.BlockSpec((1, D), lambda i: (0, 0)), tile],
        out_specs=[tile, pl.BlockSpec((8, 128), lambda i: (0, 0)), pl.BlockSpec((8, D), lambda i: (0, 0))],
        out_shape=[jax.ShapeDtypeStruct((s, D), f32), jax.ShapeDtypeStruct((8, 128), f32),
                   jax.ShapeDtypeStruct((8, D), f32)], name="head")(x, fw, tgt)


def outproj_bwd(dx, yr, ym, yg, wo):
    s, ts = dx.shape[0], 256

    def body(dx_ref, yr_ref, ym_ref, yg_ref, wo_ref, dyr_ref, dym_ref, dyg_ref, gwo_ref):
        @pl.when(pl.program_id(0) == 0)
        def _():
            gwo_ref[...] = jnp.zeros_like(gwo_ref)

        dxb = dx_ref[...].astype(bf16)
        for j, (y_ref, dy_ref) in enumerate(((yr_ref, dyr_ref), (ym_ref, dym_ref), (yg_ref, dyg_ref))):
            rows = slice(j * SEG, (j + 1) * SEG)
            dy_ref[...] = lax.dot_general(dxb, wo_ref[rows, :], (((1,), (1,)), ((), ())), preferred_element_type=f32)
            gwo_ref[rows, :] += lax.dot_general(y_ref[...].astype(bf16), dxb, (((0,), (0,)), ((), ())),
                                                preferred_element_type=f32)

    yspec = pl.BlockSpec((ts, SEG), lambda i: (i, 0))
    wspec = pl.BlockSpec((3 * SEG, D), lambda i: (0, 0))
    ysh = jax.ShapeDtypeStruct((s, SEG), f32)
    return pl.pallas_call(
        body, grid=(s // ts,),
        in_specs=[pl.BlockSpec((ts, D), lambda i: (i, 0)), yspec, yspec, yspec, wspec],
        out_specs=[yspec, yspec, yspec, wspec],
        out_shape=[ysh, ysh, ysh, jax.ShapeDtypeStruct((3 * SEG, D), f32)],
        name="outproj_bwd", compiler_params=_cp(48))(dx, yr, ym, yg, wo)


def inproj_bwd_x(x, nw, w, dxo, d_rg, d_ml, d_gd, d_sa, d_sb):
    s, ts = x.shape[0], 256
    widths = (d_rg.shape[1], d_ml.shape[1], d_gd.shape[1], HD)

    def body(x_ref, nw_ref, w_ref, dxo_ref, rg_ref, ml_ref, gd_ref, sa_ref, sb_ref, dx_ref, dnw_ref):
        @pl.when(pl.program_id(0) == 0)
        def _():
            dnw_ref[...] = jnp.zeros_like(dnw_ref)

        xn, r = _norm_rows(x_ref[...])
        pieces = (rg_ref[...], ml_ref[...], gd_ref[...], sa_ref[...] + sb_ref[...])
        dhn = jnp.zeros((ts, D), f32)
        c0 = 0
        for piece, wd in zip(pieces, widths):
            dhn = dhn + lax.dot_general(piece.astype(bf16), w_ref[:, c0:c0 + wd], (((1,), (1,)), ((), ())),
                                        preferred_element_type=f32)
            c0 += wd
        dnw_ref[0:1, :] += jnp.sum(dhn * xn, axis=0, keepdims=True)
        dxn = dhn * nw_ref[...]
        dx_ref[...] = dxo_ref[...] + r * (dxn - xn * jnp.mean(dxn * xn, axis=-1, keepdims=True))

    tile = pl.BlockSpec((ts, D), lambda i: (i, 0))
    return pl.pallas_call(
        body, grid=(s // ts,),
        in_specs=[tile, pl.BlockSpec((1, D), lambda i: (0, 0)), pl.BlockSpec((D, DP), lambda i: (0, 0)), tile]
        + [pl.BlockSpec((ts, wd), lambda i: (i, 0)) for wd in widths] + [pl.BlockSpec((ts, HD), lambda i: (i, 0))],
        out_specs=[tile, pl.BlockSpec((8, D), lambda i: (0, 0))],
        out_shape=[jax.ShapeDtypeStruct((s, D), f32), jax.ShapeDtypeStruct((8, D), f32)],
        name="inproj_bwd_x", compiler_params=_cp(56))(x, nw, w, dxo, d_rg, d_ml, d_gd, d_sa, d_sb)


def wgrad(hn_t, dps, name):
    s, ts = hn_t.shape[1], 512
    c = dps[0].shape[1]
    ct = min(c, SEG)
    n_dp = len(dps)

    def body(*refs):
        ht_ref = refs[0]
        dp_refs = refs[1:1 + n_dp]
        o_ref = refs[1 + n_dp]

        @pl.when(pl.program_id(1) == 0)
        def _():
            o_ref[...] = jnp.zeros_like(o_ref)

        dp = dp_refs[0][...]
        for extra in dp_refs[1:]:
            dp = dp + extra[...]
        o_ref[...] += jnp.dot(ht_ref[...], dp.astype(bf16), preferred_element_type=f32)

    return pl.pallas_call(
        body, grid=(c // ct, s // ts),
        in_specs=[pl.BlockSpec((D, ts), lambda j, i: (0, i))] + [pl.BlockSpec((ts, ct), lambda j, i: (i, j)) for _ in dps],
        out_specs=pl.BlockSpec((D, ct), lambda j, i: (0, j)),
        out_shape=jax.ShapeDtypeStruct((D, c), f32), name=name, compiler_params=_cp(40))(hn_t, *dps)


def _seg_spec(rows, seg, n_tiles=None):
    if n_tiles is None:
        return pl.BlockSpec((rows, SEG), lambda i: (i, seg))
    return pl.BlockSpec((rows, SEG), lambda i: (n_tiles - 1 - i, seg))


def _halo_spec(rows, seg, n_tiles=None):
    per = rows // 8
    if n_tiles is None:
        return pl.BlockSpec((8, SEG), lambda i: (jnp.maximum(i * per - 1, 0), seg))
    return pl.BlockSpec((8, SEG), lambda i: (jnp.maximum((n_tiles - 1 - i) * per - 1, 0), seg))


def _const_spec(shape):
    return pl.BlockSpec(shape, lambda i: tuple(0 for _ in shape))


def rglru_fwd(proj, conv_w, conv_b, gw_r, gw_i, gate_b, lam):
    s = proj.shape[0]
    tr = RG_TILE

    def body(xh_ref, x_ref, z_ref, cw_ref, cb_ref, gr_ref, gi_ref, gb_ref, lam_ref, y_ref, h_ref, a_s, b_s, hc):
        first = pl.program_id(0) == 0

        @pl.when(first)
        def _():
            hc[...] = jnp.zeros_like(hc)

        xh = jnp.where(first, 0.0, xh_ref[...])
        a, b = rg_pre(xh, x_ref[...], cw_ref[...], cb_ref[...], gr_ref[...], gi_ref[...], gb_ref[...], lam_ref[...])
        a_s[...] = a
        b_s[...] = b
        hc[0:1, :] = _scan_rows(a_s, b_s, h_ref, hc[0:1, :], tr, False)
        y_ref[...] = (h_ref[...] * jax.nn.silu(z_ref[...])).astype(bf16)

    out = pl.BlockSpec((tr, SEG), lambda i: (i, 0))
    return pl.pallas_call(
        body, grid=(s // tr,),
        in_specs=[_halo_spec(tr, 0), _seg_spec(tr, 0), _seg_spec(tr, 1), _const_spec((4, SEG)), _const_spec((1, SEG)),
                  _const_spec((SEG, SEG)), _const_spec((SEG, SEG)), _const_spec((2, SEG)), _const_spec((1, SEG))],
        out_specs=[out, out],
        out_shape=[jax.ShapeDtypeStruct((s, SEG), bf16), jax.ShapeDtypeStruct((s, SEG), f32)],
        scratch_shapes=[pltpu.VMEM((tr, SEG), f32), pltpu.VMEM((tr, SEG), f32), pltpu.VMEM((8, SEG), f32)],
        name="rglru_fwd", compiler_params=_cp(40))(proj, proj, proj, conv_w, conv_b, gw_r, gw_i, gate_b, lam)


def rglru_bwd(proj, hs, dy, conv_w, conv_b, gw_r, gw_i, gate_b, lam):
    s = proj.shape[0]
    tr = RG_TILE
    nt = s // tr

    def body(xh_ref, x_ref, z_ref, hh_ref, h_ref, dy_ref, cw_ref, cb_ref, gr_ref, gi_ref, gb_ref, lam_ref,
             dp_ref, dcw_ref, dcb_ref, dgr_ref, dgi_ref, dgb_ref, dlam_ref,
             an_s, g_s, dh_s, a_first, dh_first, dhalo):
        i = pl.program_id(0)
        first_tile = i == nt - 1

        @pl.when(i == 0)
        def _():
            for ref in (dcw_ref, dcb_ref, dgr_ref, dgi_ref, dgb_ref, dlam_ref, a_first, dh_first, dhalo):
                ref[...] = jnp.zeros_like(ref)

        xh = jnp.where(first_tile, 0.0, xh_ref[...])
        params = (cw_ref[...], cb_ref[...], gr_ref[...], gi_ref[...], gb_ref[...], lam_ref[...])
        (a, _), vjp = jax.vjp(rg_pre, xh, x_ref[...], *params)
        zv = z_ref[...]
        hv = h_ref[...]
        dyv = dy_ref[...]
        sig = jax.nn.sigmoid(zv)
        g_s[...] = dyv * (zv * sig)
        dp_ref[:, SEG:2 * SEG] = (dyv * hv * (sig * (1.0 + zv * (1.0 - sig)))).astype(bf16)
        ridx = lax.broadcasted_iota(jnp.int32, (tr, SEG), 0)
        an_s[...] = jnp.where(ridx == tr - 1, jnp.broadcast_to(a_first[0:1, :], (tr, SEG)), pltpu.roll(a, tr - 1, 0))
        _scan_rows(an_s, g_s, dh_s, dh_first[0:1, :], tr, True)
        dh = dh_s[...]
        h_prev_last = jnp.where(first_tile, 0.0, hh_ref[...])[7:8, :]
        h_prev = pltpu.roll(hv, 1, 0)
        h_prev = jnp.where(ridx == 0, jnp.broadcast_to(h_prev_last, (tr, SEG)), h_prev)
        dxh, dx, dcw, dcb, dgr, dgi, dgb, dlam = vjp((dh * h_prev, dh))
        dx = dx + jnp.concatenate([jnp.zeros((tr - 8, SEG), f32), dhalo[...]], axis=0)
        dp_ref[:, 0:SEG] = dx.astype(bf16)
        dhalo[...] = dxh
        a_first[0:1, :] = a[0:1, :]
        dh_first[0:1, :] = dh[0:1, :]
        dcw_ref[...] += dcw
        dcb_ref[...] += dcb
        dgr_ref[...] += dgr
        dgi_ref[...] += dgi
        dgb_ref[...] += dgb
        dlam_ref[...] += dlam

    pspecs = [_const_spec((4, SEG)), _const_spec((1, SEG)), _const_spec((SEG, SEG)), _const_spec((SEG, SEG)),
              _const_spec((2, SEG)), _const_spec((1, SEG))]
    pshapes = [jax.ShapeDtypeStruct(sh, f32) for sh in ((4, SEG), (1, SEG), (SEG, SEG), (SEG, SEG), (2, SEG), (1, SEG))]
    tile = pl.BlockSpec((tr, SEG), lambda i: (nt - 1 - i, 0))
    return pl.pallas_call(
        body, grid=(nt,),
        in_specs=[_halo_spec(tr, 0, nt), _seg_spec(tr, 0, nt), _seg_spec(tr, 1, nt),
                  pl.BlockSpec((8, SEG), lambda i: (jnp.maximum((nt - 1 - i) * (tr // 8) - 1, 0), 0)), tile, tile] + pspecs,
        out_specs=[pl.BlockSpec((tr, 2 * SEG), lambda i: (nt - 1 - i, 0))] + pspecs,
        out_shape=[jax.ShapeDtypeStruct((s, 2 * SEG), bf16)] + pshapes,
        scratch_shapes=[pltpu.VMEM((tr, SEG), f32), pltpu.VMEM((tr, SEG), f32), pltpu.VMEM((tr, SEG), f32),
                        pltpu.VMEM((8, SEG), f32), pltpu.VMEM((8, SEG), f32), pltpu.VMEM((8, SEG), f32)],
        name="rglru_bwd", compiler_params=_cp(48))(proj, proj, proj, hs, hs, dy, conv_w, conv_b, gw_r, gw_i, gate_b, lam)


ML_SEGS = (2, 3, 4, 5, 6)
SMALL_BLK = SMALL0 // HD


def _cut_refs(refs, n_in, n_xin, n_out, n_land, n_scratch):
    bounds = [0, n_in, n_in + n_xin, n_in + n_xin + n_out, n_in + n_xin + n_out + n_land,
              n_in + n_xin + n_out + n_land + n_scratch, len(refs)]
    return [refs[a:b] for a, b in zip(bounds[:-1], bounds[1:])]


def mlstm_fwd(proj, bias_row, norm_w, xf=None):
    s = proj.shape[0]
    nc = s // LR
    n_in, n_out = 8, 4
    x_ops, x_in_specs, x_out_shapes, x_out_specs, aliases, x_sems, n_xin, n_land = _carried(xf, n_in, n_out)

    def body(*refs):
        ins, x_refs, outs, land_refs, scratch, sem_refs = _cut_refs(refs, n_in, n_xin, n_out, n_land, 3)
        q_ref, k_ref, v_ref, o_ref, z_ref, sm_ref, b_ref, nw_ref = ins
        y_ref, cs_ref, ns_ref, ms_ref = outs
        c_s, n_s, m_s = scratch
        i = pl.program_id(0)
        _carry_open(xf, i, nc, x_refs, land_refs, sem_refs)

        @pl.when(i == 0)
        def _():
            c_s[...] = jnp.zeros_like(c_s)
            n_s[...] = jnp.zeros_like(n_s)
            m_s[...] = jnp.zeros_like(m_s)

        cs_ref[0] = c_s[...]
        ns_ref[0] = n_s[...]
        ms_ref[0] = m_s[...]
        y, c2, n2, m2 = ml_chunk(q_ref[...], k_ref[...], v_ref[...], o_ref[...], z_ref[...], sm_ref[...],
                                 b_ref[...], nw_ref[...], c_s[...], n_s[...], m_s[...])
        y_ref[...] = y.astype(bf16)
        c_s[...] = c2
        n_s[...] = n2
        m_s[...] = m2
        _carry_close(xf, i, nc, x_refs, land_refs, sem_refs)

    return pl.pallas_call(
        body, grid=(nc,),
        in_specs=[_seg_spec(LR, sg) for sg in ML_SEGS]
        + [pl.BlockSpec((LR, HD), lambda i: (i, SMALL_BLK)), _const_spec((1, HD)), _const_spec((1, SEG))] + x_in_specs,
        out_specs=[pl.BlockSpec((LR, SEG), lambda i: (i, 0)), pl.BlockSpec((1, SEG, HD), lambda i: (i, 0, 0)),
                   pl.BlockSpec((1, 8, HD), lambda i: (i, 0, 0)), pl.BlockSpec((1, 8, HD), lambda i: (i, 0, 0))] + x_out_specs,
        out_shape=[jax.ShapeDtypeStruct((s, SEG), bf16), jax.ShapeDtypeStruct((nc, SEG, HD), f32),
                   jax.ShapeDtypeStruct((nc, 8, HD), f32), jax.ShapeDtypeStruct((nc, 8, HD), f32)] + x_out_shapes,
        scratch_shapes=[pltpu.VMEM((SEG, HD), f32), pltpu.VMEM((8, HD), f32), pltpu.VMEM((8, HD), f32)] + x_sems,
        input_output_aliases=aliases, name="mlstm_fwd" if xf is None else "mlstm_fwd_carrying")(
            proj, proj, proj, proj, proj, proj, bias_row, norm_w, *x_ops)


def mlstm_bwd(proj, dy, cs, ns, ms, bias_row, norm_w, xf=None):
    s = proj.shape[0]
    nc = s // LR
    n_in, n_out = 12, 4
    x_ops, x_in_specs, x_out_shapes, x_out_specs, aliases, x_sems, n_xin, n_land = _carried(xf, n_in, n_out)

    def body(*refs):
        ins, x_refs, outs, land_refs, scratch, sem_refs = _cut_refs(refs, n_in, n_xin, n_out, n_land, 3)
        q_ref, k_ref, v_ref, o_ref, z_ref, sm_ref, dy_ref, cs_ref, ns_ref, ms_ref, b_ref, nw_ref = ins
        dp_ref, dsm_ref, db_ref, dnw_ref = outs
        dc_s, dn_s, dm_s = scratch
        i = pl.program_id(0)
        _carry_open(xf, i, nc, x_refs, land_refs, sem_refs)

        @pl.when(i == 0)
        def _():
            for ref in (db_ref, dnw_ref, dc_s, dn_s, dm_s):
                ref[...] = jnp.zeros_like(ref)

        _, vjp = jax.vjp(ml_chunk, q_ref[...], k_ref[...], v_ref[...], o_ref[...], z_ref[...], sm_ref[...],
                         b_ref[...], nw_ref[...], cs_ref[0], ns_ref[0], ms_ref[0])
        dq, dk, dv, do, dz, dsm, db, dnw, dc, dn, dm = vjp((dy_ref[...], dc_s[...], dn_s[...], dm_s[...]))
        for j, val in enumerate((dq, dk, dv, do, dz)):
            dp_ref[:, j * SEG:(j + 1) * SEG] = val.astype(bf16)
        dsm_ref[...] = dsm
        db_ref[0:1, :] += db
        dnw_ref[0:1, :] += dnw
        dc_s[...] = dc
        dn_s[...] = dn
        dm_s[...] = dm
        _carry_close(xf, i, nc, x_refs, land_refs, sem_refs)

    rev3 = lambda i: (nc - 1 - i, 0, 0)
    return pl.pallas_call(
        body, grid=(nc,),
        in_specs=[_seg_spec(LR, sg, nc) for sg in ML_SEGS]
        + [pl.BlockSpec((LR, HD), lambda i: (nc - 1 - i, SMALL_BLK)), pl.BlockSpec((LR, SEG), lambda i: (nc - 1 - i, 0)),
           pl.BlockSpec((1, SEG, HD), rev3), pl.BlockSpec((1, 8, HD), rev3), pl.BlockSpec((1, 8, HD), rev3),
           _const_spec((1, HD)), _const_spec((1, SEG))] + x_in_specs,
        out_specs=[pl.BlockSpec((LR, 5 * SEG), lambda i: (nc - 1 - i, 0)), pl.BlockSpec((LR, HD), lambda i: (nc - 1 - i, 0)),
                   _const_spec((8, HD)), _const_spec((8, SEG))] + x_out_specs,
        out_shape=[jax.ShapeDtypeStruct((s, 5 * SEG), bf16), jax.ShapeDtypeStruct((s, HD), f32),
                   jax.ShapeDtypeStruct((8, HD), f32), jax.ShapeDtypeStruct((8, SEG), f32)] + x_out_shapes,
        scratch_shapes=[pltpu.VMEM((SEG, HD), f32), pltpu.VMEM((8, HD), f32), pltpu.VMEM((8, HD), f32)] + x_sems,
        input_output_aliases=aliases, name="mlstm_bwd" if xf is None else "mlstm_bwd_carrying", compiler_params=_cp(48))(
            proj, proj, proj, proj, proj, proj, dy, cs, ns, ms, bias_row, norm_w, *x_ops)


GD_SEGS = (7, 8, 9)


def _carried(xf, n_in, n_out):
    operands, out_shapes, aliases, sems, _, n_xin, n_land = _carry_plumb(xf, n_in, n_out)
    return operands, [_ANY] * n_xin, out_shapes, [_ANY] * n_land, aliases, sems, n_xin, n_land


def _carry_open(xf, i, n_steps, x_refs, land_refs, sem_refs):
    if xf is None:
        return
    srcs = x_refs[:len(xf.srcs)]

    @pl.when(i == 0)
    def _():
        _carry_start(xf, srcs, land_refs, sem_refs)

    @pl.when(i == n_steps // 2)
    def _():
        _carry_middle(xf, srcs, land_refs, sem_refs)


def _carry_close(xf, i, n_steps, x_refs, land_refs, sem_refs):
    if xf is None:
        return

    @pl.when(i == n_steps - 1)
    def _():
        _carry_finish(xf, x_refs[:len(xf.srcs)], land_refs, sem_refs)


def gdn_fwd(proj, conv_w, alog_row, dt_row, norm_w, xf=None):
    s = proj.shape[0]
    nc = s // LR
    n_in, n_out = 12, 2
    x_ops, x_in_specs, x_out_shapes, x_out_specs, aliases, x_sems, n_xin, n_land = _carried(xf, n_in, n_out)

    def body(*refs):
        qh_ref, q_ref, kh_ref, k_ref, vh_ref, v_ref, z_ref, sm_ref, cw_ref, al_ref, dt_ref, nw_ref = refs[:n_in]
        x_refs = refs[n_in:n_in + n_xin]
        y_ref, ss_ref = refs[n_in + n_xin:n_in + n_xin + n_out]
        land_refs = refs[n_in + n_xin + n_out:n_in + n_xin + n_out + n_land]
        st_s = refs[n_in + n_xin + n_out + n_land]
        sem_refs = refs[n_in + n_xin + n_out + n_land + 1:]
        i = pl.program_id(0)
        first = i == 0
        _carry_open(xf, i, nc, x_refs, land_refs, sem_refs)

        @pl.when(first)
        def _():
            st_s[...] = jnp.zeros_like(st_s)

        ss_ref[0] = st_s[...]
        halo = [jnp.where(first, 0.0, r[...]) for r in (qh_ref, kh_ref, vh_ref)]
        y, st2 = gd_chunk(halo[0], q_ref[...], halo[1], k_ref[...], halo[2], v_ref[...], z_ref[...], sm_ref[...],
                          cw_ref[...], al_ref[...], dt_ref[...], nw_ref[...], st_s[...])
        y_ref[...] = y.astype(bf16)
        st_s[...] = st2
        _carry_close(xf, i, nc, x_refs, land_refs, sem_refs)

    qkv_specs = []
    for sg in GD_SEGS:
        qkv_specs += [_halo_spec(LR, sg), _seg_spec(LR, sg)]
    return pl.pallas_call(
        body, grid=(nc,),
        in_specs=qkv_specs + [_seg_spec(LR, 10), pl.BlockSpec((LR, HD), lambda i: (i, SMALL_BLK)),
                              _const_spec((4, 3 * SEG)), _const_spec((1, HD)), _const_spec((1, HD)), _const_spec((1, HD))]
        + x_in_specs,
        out_specs=[pl.BlockSpec((LR, SEG), lambda i: (i, 0)), pl.BlockSpec((1, SEG, HD), lambda i: (i, 0, 0))] + x_out_specs,
        out_shape=[jax.ShapeDtypeStruct((s, SEG), bf16), jax.ShapeDtypeStruct((nc, SEG, HD), f32)] + x_out_shapes,
        scratch_shapes=[pltpu.VMEM((SEG, HD), f32)] + x_sems, input_output_aliases=aliases,
        name="gdn_fwd" if xf is None else "gdn_fwd_carrying")(
            proj, proj, proj, proj, proj, proj, proj, proj, conv_w, alog_row, dt_row, norm_w, *x_ops)


def gdn_bwd(proj, dy, ss, conv_w, alog_row, dt_row, norm_w, xf=None):
    s = proj.shape[0]
    nc = s // LR
    n_in, n_out = 14, 6
    x_ops, x_in_specs, x_out_shapes, x_out_specs, aliases, x_sems, n_xin, n_land = _carried(xf, n_in, n_out)

    def body(*refs):
        (qh_ref, q_ref, kh_ref, k_ref, vh_ref, v_ref, z_ref, sm_ref, dy_ref, ss_ref,
         cw_ref, al_ref, dt_ref, nw_ref) = refs[:n_in]
        x_refs = refs[n_in:n_in + n_xin]
        dp_ref, dsm_ref, dcw_ref, dal_ref, ddt_ref, dnw_ref = refs[n_in + n_xin:n_in + n_xin + n_out]
        land_refs = refs[n_in + n_xin + n_out:n_in + n_xin + n_out + n_land]
        dst_s, dhalo = refs[n_in + n_xin + n_out + n_land:n_in + n_xin + n_out + n_land + 2]
        sem_refs = refs[n_in + n_xin + n_out + n_land + 2:]
        i = pl.program_id(0)
        first_chunk = i == nc - 1
        _carry_open(xf, i, nc, x_refs, land_refs, sem_refs)

        @pl.when(i == 0)
        def _():
            for ref in (dcw_ref, dal_ref, ddt_ref, dnw_ref, dst_s, dhalo):
                ref[...] = jnp.zeros_like(ref)

        halo = [jnp.where(first_chunk, 0.0, r[...]) for r in (qh_ref, kh_ref, vh_ref)]
        _, vjp = jax.vjp(gd_chunk, halo[0], q_ref[...], halo[1], k_ref[...], halo[2], v_ref[...], z_ref[...], sm_ref[...],
                         cw_ref[...], al_ref[...], dt_ref[...], nw_ref[...], ss_ref[0])
        dqh, dq, dkh, dk, dvh, dv, dz, dsm, dcw, dal, ddt, dnw, dst = vjp((dy_ref[...], dst_s[...]))
        for j, val in enumerate((dq, dk, dv)):
            val = val + jnp.concatenate([jnp.zeros((LR - 8, SEG), f32), dhalo[:, j * SEG:(j + 1) * SEG]], axis=0)
            dp_ref[:, j * SEG:(j + 1) * SEG] = val.astype(bf16)
        dp_ref[:, 3 * SEG:4 * SEG] = dz.astype(bf16)
        for j, val in enumerate((dqh, dkh, dvh)):
            dhalo[:, j * SEG:(j + 1) * SEG] = val
        dsm_ref[...] = dsm
        dcw_ref[...] += dcw
        dal_ref[0:1, :] += dal
        ddt_ref[0:1, :] += ddt
        dnw_ref[0:1, :] += dnw
        dst_s[...] = dst
        _carry_close(xf, i, nc, x_refs, land_refs, sem_refs)

    qkv_specs = []
    for sg in GD_SEGS:
        qkv_specs += [_halo_spec(LR, sg, nc), _seg_spec(LR, sg, nc)]
    return pl.pallas_call(
        body, grid=(nc,),
        in_specs=qkv_specs + [_seg_spec(LR, 10, nc), pl.BlockSpec((LR, HD), lambda i: (nc - 1 - i, SMALL_BLK)),
                              pl.BlockSpec((LR, SEG), lambda i: (nc - 1 - i, 0)),
                              pl.BlockSpec((1, SEG, HD), lambda i: (nc - 1 - i, 0, 0)),
                              _const_spec((4, 3 * SEG)), _const_spec((1, HD)), _const_spec((1, HD)), _const_spec((1, HD))]
        + x_in_specs,
        out_specs=[pl.BlockSpec((LR, 4 * SEG), lambda i: (nc - 1 - i, 0)), pl.BlockSpec((LR, HD), lambda i: (nc - 1 - i, 0)),
                   _const_spec((4, 3 * SEG)), _const_spec((8, HD)), _const_spec((8, HD)), _const_spec((8, HD))] + x_out_specs,
        out_shape=[jax.ShapeDtypeStruct((s, 4 * SEG), bf16), jax.ShapeDtypeStruct((s, HD), f32),
                   jax.ShapeDtypeStruct((4, 3 * SEG), f32), jax.ShapeDtypeStruct((8, HD), f32),
                   jax.ShapeDtypeStruct((8, HD), f32), jax.ShapeDtypeStruct((8, HD), f32)] + x_out_shapes,
        scratch_shapes=[pltpu.VMEM((SEG, HD), f32), pltpu.VMEM((8, 3 * SEG), f32)] + x_sems, input_output_aliases=aliases,
        name="gdn_bwd" if xf is None else "gdn_bwd_carrying", compiler_params=_cp(48))(
            proj, proj, proj, proj, proj, proj, proj, proj, dy, ss, conv_w, alog_row, dt_row, norm_w, *x_ops)


def _my_place():
    return lax.axis_index("x"), lax.axis_index("y"), lax.axis_index("c")


def _slot(p):
    return 4 * p[0] + 2 * p[1] + p[2]


def _peer(me, j):
    flips = ((j >> 2) & 1, (j >> 1) & 1, j & 1)
    return tuple((1 - v) if fl else v for v, fl in zip(me, flips))


_ANY = pl.BlockSpec(memory_space=pl.ANY)


class Scatter(NamedTuple):
    srcs: list
    lands: list
    land_of: list
    layer: list
    span: list


class Gather2(NamedTuple):
    srcs: list
    lands: list
    span: list


def _carry_plumb(cx, n_in, n_out):
    if cx is None:
        return [], [], {}, [], 0, 0, 0
    n_src = len(cx.srcs)
    passed = [li for li, ld in enumerate(cx.lands) if not isinstance(ld, jax.ShapeDtypeStruct)]
    operands = list(cx.srcs) + [cx.lands[li] for li in passed]
    aliases = {n_in + n_src + k: n_out + li for k, li in enumerate(passed)}
    out_shapes = [jax.ShapeDtypeStruct(ld.shape, ld.dtype) for ld in cx.lands]
    sems = [pltpu.SemaphoreType.DMA((n_src, N_DEV - 1)), pltpu.SemaphoreType.DMA((n_src, N_DEV - 1)),
            pltpu.SemaphoreType.DMA((n_src,))]
    return operands, out_shapes, aliases, sems, n_src, len(operands), len(cx.lands)


def _scatter_copies(sc, src_refs, land_refs, send_sems, recv_sems, local_sems):
    me = _my_place()
    mine = _slot(me)
    local, remote = [], []
    for a, src_ref in enumerate(src_refs):
        lo, hi = sc.span[a]
        rows = pl.ds(lo, hi - lo)
        dst = land_refs[sc.land_of[a]].at[mine, sc.layer[a], rows]
        local.append(pltpu.make_async_copy(src_ref.at[mine, rows], dst, local_sems.at[a]))
        for j in range(1, N_DEV):
            peer = _peer(me, j)
            remote.append(pltpu.make_async_remote_copy(
                src_ref=src_ref.at[_slot(peer), rows], dst_ref=dst, send_sem=send_sems.at[a, j - 1],
                recv_sem=recv_sems.at[a, j - 1], device_id=peer, device_id_type=MESH))
    return local, remote


def _gather2_copy(land, sems, a, k, block_of, to, rows, src=None):
    dst = land.at[_slot(block_of), rows]
    return pltpu.make_async_remote_copy(src_ref=dst if src is None else src, dst_ref=dst, send_sem=sems[0].at[a, k],
                                        recv_sem=sems[1].at[a, k], device_id=to, device_id_type=MESH)


def _gather2_places():
    x, y, c = _my_place()
    return (x, y, c), (x, y, 1 - c), [(1 - x, y), (x, 1 - y), (1 - x, 1 - y)], c


def _carry_start(cx, src_refs, land_refs, sems):
    if isinstance(cx, Scatter):
        local, remote = _scatter_copies(cx, src_refs, land_refs, *sems)
        for cp in local + remote:
            cp.start()
        return
    me, sib, chips, c = _gather2_places()
    for a, (src_ref, land) in enumerate(zip(src_refs, land_refs)):
        rows = pl.ds(cx.span[a][0], cx.span[a][1] - cx.span[a][0])
        src = src_ref.at[rows]
        pltpu.make_async_copy(src, land.at[_slot(me), rows], sems[2].at[a]).start()
        _gather2_copy(land, sems, a, 0, me, sib, rows, src=src).start()
        for j, chip in enumerate(chips):
            _gather2_copy(land, sems, a, 1 + j, me, (*chip, c), rows, src=src).start()


def _carry_middle(cx, src_refs, land_refs, sems):
    if isinstance(cx, Scatter):
        return
    me, sib, chips, c = _gather2_places()
    for a, land in enumerate(land_refs):
        rows = pl.ds(cx.span[a][0], cx.span[a][1] - cx.span[a][0])
        for j, chip in enumerate(chips):
            _gather2_copy(land, sems, a, 1 + j, (*chip, c), me, rows).wait_recv()
            _gather2_copy(land, sems, a, 4 + j, (*chip, c), sib, rows).start()


def _carry_finish(cx, src_refs, land_refs, sems):
    if isinstance(cx, Scatter):
        local, remote = _scatter_copies(cx, src_refs, land_refs, *sems)
        for cp in remote:
            cp.wait()
        for cp in local:
            cp.wait()
        return
    me, sib, chips, c = _gather2_places()
    for a, (src_ref, land) in enumerate(zip(src_refs, land_refs)):
        rows = pl.ds(cx.span[a][0], cx.span[a][1] - cx.span[a][0])
        src = src_ref.at[rows]
        _gather2_copy(land, sems, a, 0, sib, me, rows).wait_recv()
        for j, chip in enumerate(chips):
            _gather2_copy(land, sems, a, 4 + j, (*chip, 1 - c), me, rows).wait_recv()
        _gather2_copy(land, sems, a, 0, me, sib, rows, src=src).wait_send()
        for j, chip in enumerate(chips):
            _gather2_copy(land, sems, a, 1 + j, me, (*chip, c), rows, src=src).wait_send()
            _gather2_copy(land, sems, a, 4 + j, (*chip, c), sib, rows).wait_send()
        pltpu.make_async_copy(src, land.at[_slot(me), rows], sems[2].at[a]).wait()


def exchange(cx, name):
    operands, out_shapes, aliases, sems, n_src, n_xin, n_land = _carry_plumb(cx, 0, 0)

    def body(*refs):
        src_refs, land_refs, sem_refs = refs[:n_src], refs[n_xin:n_xin + n_land], refs[n_xin + n_land:]
        _carry_start(cx, src_refs, land_refs, sem_refs)
        _carry_middle(cx, src_refs, land_refs, sem_refs)
        _carry_finish(cx, src_refs, land_refs, sem_refs)

    return pl.pallas_call(body, in_specs=[_ANY] * n_xin, out_specs=[_ANY] * n_land, out_shape=out_shapes,
                          scratch_shapes=sems, input_output_aliases=aliases, name=name)(*operands)


def small_allreduce(packed, name):
    r = packed.shape[0]
    rc = r // N_DEV

    def body(in_ref, out_ref, recv_buf, send1, recv1, send2, recv2):
        me = _my_place()
        mine = _slot(me)
        my_rows = pl.ds(pl.multiple_of(mine * rc, 8), rc)
        first, second = [], []
        for j in range(1, N_DEV):
            peer = _peer(me, j)
            peer_rows = pl.ds(pl.multiple_of(_slot(peer) * rc, 8), rc)
            first.append(pltpu.make_async_remote_copy(
                src_ref=in_ref.at[peer_rows], dst_ref=recv_buf.at[mine], send_sem=send1.at[j - 1], recv_sem=recv1.at[j - 1],
                device_id=peer, device_id_type=MESH))
            second.append(pltpu.make_async_remote_copy(
                src_ref=out_ref.at[my_rows], dst_ref=out_ref.at[my_rows], send_sem=send2.at[j - 1], recv_sem=recv2.at[j - 1],
                device_id=peer, device_id_type=MESH))
        for cp in first:
            cp.start()
        recv_buf[mine] = in_ref[my_rows]
        for cp in first:
            cp.wait()
        acc = recv_buf[0]
        for k in range(1, N_DEV):
            acc = acc + recv_buf[k]
        out_ref[my_rows] = acc
        for cp in second:
            cp.start()
        for cp in second:
            cp.wait()

    vmem = pl.BlockSpec(memory_space=pltpu.VMEM)
    return pl.pallas_call(
        body, in_specs=[vmem], out_specs=vmem, out_shape=jax.ShapeDtypeStruct((r, 128), f32),
        scratch_shapes=[pltpu.VMEM((N_DEV, rc, 128), f32)] + [pltpu.SemaphoreType.DMA((N_DEV - 1,))] * 4,
        name=name, compiler_params=_cp(32))(packed)


def _adamw(w, g, m, v):
    m = ADAM_B1 * m + (1.0 - ADAM_B1) * g
    v = ADAM_B2 * v + (1.0 - ADAM_B2) * (g * g)
    m_hat = m / (1.0 - ADAM_B1 ** ADAM_STEP)
    v_hat = v / (1.0 - ADAM_B2 ** ADAM_STEP)
    delta = -ADAM_LR * (m_hat / (jnp.sqrt(v_hat) + ADAM_EPS) + ADAM_WD * w)
    return delta, m, v


def adam_slots(slots, w, m, v, rows, name):
    _, depth, r, c = slots.shape

    def body(s_ref, w_ref, m_ref, v_ref, g_ref, d_ref, m2_ref, v2_ref):
        g = s_ref[0, 0].astype(f32)
        for k in range(1, N_DEV):
            g = g + s_ref[k, 0].astype(f32)
        d, m2, v2 = _adamw(w_ref[0], g, m_ref[0], v_ref[0])
        g_ref[0] = g
        d_ref[0] = d
        m2_ref[0] = m2
        v2_ref[0] = v2

    blk = pl.BlockSpec((1, rows, c), lambda l, i: (l, i, 0))
    sh = jax.ShapeDtypeStruct((depth, r, c), f32)
    return pl.pallas_call(
        body, grid=(depth, r // rows),
        in_specs=[pl.BlockSpec((N_DEV, 1, rows, c), lambda l, i: (0, l, i, 0)), blk, blk, blk],
        out_specs=[blk] * 4, out_shape=[sh] * 4, name=name, compiler_params=_cp(40))(slots, w, m, v)


def adam_flat(g, w, m, v, name):
    def body(g_ref, w_ref, m_ref, v_ref, d_ref, m2_ref, v2_ref):
        d, m2, v2 = _adamw(w_ref[...], g_ref[...], m_ref[...], v_ref[...])
        d_ref[...] = d
        m2_ref[...] = m2
        v2_ref[...] = v2

    sh = jax.ShapeDtypeStruct(g.shape, f32)
    return pl.pallas_call(body, out_shape=[sh] * 3, name=name)(g, w, m, v)


def _rows_of(shape):
    n = 1
    for dim in shape:
        n *= dim
    return n, -(-n // (8 * 128)) * 8


def _pack(arrs):
    parts = []
    for a in arrs:
        n, rows = _rows_of(a.shape)
        parts.append(jnp.pad(a.reshape(-1).astype(f32), (0, rows * 128 - n)).reshape(rows, 128))
    return jnp.concatenate(parts, axis=0)


def _unpack(packed, shapes):
    out, row = [], 0
    for sh in shapes:
        n, rows = _rows_of(sh)
        out.append(packed[row:row + rows].reshape(-1)[:n].reshape(sh))
        row += rows
    return out


def _block_diag(gw):
    eye = jnp.eye(8, dtype=gw.dtype)
    return (gw[:, :, None, :] * eye[:, None, :, None]).reshape(SEG, SEG)


def _diag_blocks(dense):
    eye = jnp.eye(8, dtype=dense.dtype)
    return (dense.reshape(8, 64, 8, 64) * eye[:, None, :, None]).sum(axis=2)


def _lane_row(vals, first_lane):
    return jnp.zeros((1, HD), f32).at[0, first_lane:first_lane + NH].set(vals)


def kernel(x, norm_w, w_in, rg_conv_w, rg_conv_b, rg_gate_w, rg_gate_b, rg_lambda, ml_gate_b, ml_norm_w, gd_conv_w, gd_a_log, gd_dt_bias, gd_norm_w, w_out, final_norm_w, loss_target, m_norm_w, m_w_in, m_rg_conv_w, m_rg_conv_b, m_rg_gate_w, m_rg_gate_b, m_rg_lambda, m_ml_gate_b, m_ml_norm_w, m_gd_conv_w, m_gd_a_log, m_gd_dt_bias, m_gd_norm_w, m_w_out, m_final_norm_w, v_norm_w, v_w_in, v_rg_conv_w, v_rg_conv_b, v_rg_gate_w, v_rg_gate_b, v_rg_lambda, v_ml_gate_b, v_ml_norm_w, v_gd_conv_w, v_gd_a_log, v_gd_dt_bias, v_gd_norm_w, v_w_out, v_final_norm_w):
    s = x.shape[1]
    xs = x.reshape(s, D)
    tgt = loss_target.reshape(s, D)
    me = 4 * lax.axis_index("x") + 2 * lax.axis_index("y") + lax.axis_index("c")

    comm = MeshComm(w_in, w_out, [rg_conv_w, rg_gate_b, gd_conv_w])
    rg_conv_full, rg_gb_full, gd_conv_full = comm.small_weights
    loss_part, dx, d_fw, g_small = local_step(
        xs, tgt, comm, rg_conv_full, rg_gb_full, gd_conv_full, norm_w, rg_conv_b, rg_gate_w, rg_lambda,
        ml_gate_b, ml_norm_w, gd_a_log, gd_dt_bias, gd_norm_w, final_norm_w)
    given_w = dict(norm_w=norm_w, rg_conv_w=rg_conv_w, rg_conv_b=rg_conv_b, rg_gate_w=rg_gate_w, rg_gate_b=rg_gate_b,
                   rg_lambda=rg_lambda, ml_gate_b=ml_gate_b, ml_norm_w=ml_norm_w, gd_conv_w=gd_conv_w, gd_a_log=gd_a_log,
                   gd_dt_bias=gd_dt_bias, gd_norm_w=gd_norm_w, final_norm_w=final_norm_w, w_in=w_in, w_out=w_out)
    given_m = dict(norm_w=m_norm_w, rg_conv_w=m_rg_conv_w, rg_conv_b=m_rg_conv_b, rg_gate_w=m_rg_gate_w, rg_gate_b=m_rg_gate_b,
                   rg_lambda=m_rg_lambda, ml_gate_b=m_ml_gate_b, ml_norm_w=m_ml_norm_w, gd_conv_w=m_gd_conv_w,
                   gd_a_log=m_gd_a_log, gd_dt_bias=m_gd_dt_bias, gd_norm_w=m_gd_norm_w, final_norm_w=m_final_norm_w,
                   w_in=m_w_in, w_out=m_w_out)
    given_v = dict(norm_w=v_norm_w, rg_conv_w=v_rg_conv_w, rg_conv_b=v_rg_conv_b, rg_gate_w=v_rg_gate_w, rg_gate_b=v_rg_gate_b,
                   rg_lambda=v_rg_lambda, ml_gate_b=v_ml_gate_b, ml_norm_w=v_ml_norm_w, gd_conv_w=v_gd_conv_w,
                   gd_a_log=v_gd_a_log, gd_dt_bias=v_gd_dt_bias, gd_norm_w=v_gd_norm_w, final_norm_w=v_final_norm_w,
                   w_in=v_w_in, w_out=v_w_out)
    return finish_step(loss_part, dx, d_fw, g_small, comm, s, me, given_w, given_m, given_v)


def _gathered_pieces():
    per = D_IN // N_DEV
    pieces = []
    for lo, hi in ((0, 3584), (3592, 5640), (3584, 3592), (5640, 5648)):
        col = lo
        while col < hi:
            k = col // per
            end = min(hi, (k + 1) * per)
            pieces.append((k, col - k * per, end - k * per))
            col = end
    return pieces


class MeshComm:
    GWI_SPLIT = 256
    WI_SPLIT = 384

    def __init__(self, w_in, w_out, small_shards):
        per = D_IN // N_DEV
        self.wi_sh = [w_in[l].astype(bf16) for l in range(DEPTH)]
        self.wo_sh = [w_out[l].astype(bf16) for l in range(DEPTH)]
        self.wi_land = jax.ShapeDtypeStruct((N_DEV, D, per), bf16)
        self.wo_land = jax.ShapeDtypeStruct((N_DEV, 3 * SEG // N_DEV, D), bf16)
        packed = _pack(small_shards)
        first = Gather2([self.wi_sh[0], self.wo_sh[0], packed],
                        [self.wi_land, self.wo_land, jax.ShapeDtypeStruct((N_DEV,) + packed.shape, f32)],
                        [(0, D), (0, 3 * SEG // N_DEV), (0, packed.shape[0])])
        wi_g, wo_g, sm_g = exchange(first, "gather_first")
        self.wi_g, self.wo_g = {0: wi_g}, {0: wo_g}
        shapes = [a.shape for a in small_shards]
        parts = [_unpack(sm_g[k], shapes) for k in range(N_DEV)]
        self.small_weights = [jnp.concatenate([p[j] for p in parts], axis=-1) for j in range(len(small_shards))]
        self.gwi_land = lax.empty((N_DEV, DEPTH, D, per), bf16)
        self.gwo_land = lax.empty((N_DEV, DEPTH, 3 * SEG // N_DEV, D), bf16)
        self.gwi_slots = {}
        self.gwo_slots = {}

    def weights(self, l):
        cols = [self.wi_g[l][k, :, a:b] for k, a, b in _gathered_pieces()] + [jnp.zeros((D, DP - D_IN), bf16)]
        return jnp.concatenate(cols, axis=1), self.wo_g[l].reshape(3 * SEG, D)

    def fwd_carry(self, l, host):
        if l + 1 >= DEPTH:
            return None
        if host == "mlstm":
            return Gather2([self.wo_sh[l + 1]], [self.wo_land], [(0, 3 * SEG // N_DEV)])
        if host == "inproj":
            return Gather2([self.wi_sh[l + 1]], [self.wi_land], [(0, self.WI_SPLIT)])
        return Gather2([self.wi_sh[l + 1]], [self.wi_g[l + 1]], [(self.WI_SPLIT, D)])

    def fwd_landed(self, l, host, landed):
        (self.wo_g if host == "mlstm" else self.wi_g)[l + 1] = landed[0]

    def own_w_out_grad(self, l, g_wo):
        self.gwo_slots[l] = g_wo.reshape(N_DEV, 3 * SEG // N_DEV, D).astype(bf16)

    def bwd_carry(self, l, host):
        rows_o = 3 * SEG // N_DEV
        srcs, land_of, layer, span = [], [], [], []
        if l + 1 < DEPTH:
            if host == "mlstm":
                srcs += [self.gwo_slots[l + 1], self.gwi_slots[l + 1]]
                land_of, layer, span = [1, 0], [l + 1, l + 1], [(0, rows_o), (0, self.GWI_SPLIT)]
            else:
                srcs, land_of, layer, span = [self.gwi_slots[l + 1]], [0], [l + 1], [(self.GWI_SPLIT, D)]
        if l == 0 and host == "mlstm":
            srcs, land_of, layer, span = srcs + [self.gwo_slots[0]], land_of + [1], layer + [0], span + [(0, rows_o)]
        if not srcs:
            return None
        return Scatter(srcs, [self.gwi_land, self.gwo_land], land_of, layer, span)

    def bwd_landed(self, landed):
        self.gwi_land, self.gwo_land = landed

    def grads_ready(self, l, pieces):
        per = D_IN // N_DEV
        g_wi = jnp.concatenate(pieces, axis=1)
        self.gwi_slots[l] = jnp.stack([g_wi[:, k * per:(k + 1) * per] for k in range(N_DEV)]).astype(bf16)

    def finish(self):
        last = Scatter([self.gwi_slots[0]], [self.gwi_land, self.gwo_land], [0], [0], [(0, D)])
        return exchange(last, "scatter_last")


def local_step(xs, tgt, comm, rg_conv_full, rg_gb_full, gd_conv_full, norm_w, rg_conv_b, rg_gate_w,
               rg_lambda, ml_gate_b, ml_norm_w, gd_a_log, gd_dt_bias, gd_norm_w, final_norm_w):
    acts = []
    for l in range(DEPTH):
        nw = norm_w[l].reshape(1, D)
        w_in_l, w_out_l = comm.weights(l)
        xf = comm.fwd_carry(l, "inproj")
        proj, hn_t, *landed = inproj(xs, nw, w_in_l, xf=xf)
        if xf is not None:
            comm.fwd_landed(l, "inproj", landed)
        rg_p = (rg_conv_full[l], rg_conv_b[l].reshape(1, SEG), _block_diag(rg_gate_w[l, 0]), _block_diag(rg_gate_w[l, 1]),
                rg_gb_full[l], rg_lambda[l].reshape(1, SEG))
        y_rg, hs = rglru_fwd(proj, *rg_p)
        ml_p = (jnp.zeros((1, HD), f32).at[0, 0:2 * NH].set(ml_gate_b[l].reshape(-1)), ml_norm_w[l].reshape(1, SEG))
        xf = comm.fwd_carry(l, "mlstm")
        y_ml, cs, ns, ms, *landed = mlstm_fwd(proj, *ml_p, xf=xf)
        if xf is not None:
            comm.fwd_landed(l, "mlstm", landed)
        gd_p = (gd_conv_full[l], _lane_row(gd_a_log[l], 2 * NH), _lane_row(gd_dt_bias[l], 2 * NH), gd_norm_w[l].reshape(1, HD))
        xf = comm.fwd_carry(l, "gdn")
        y_gd, ss, *landed = gdn_fwd(proj, *gd_p, xf=xf)
        if xf is not None:
            comm.fwd_landed(l, "gdn", landed)
        acts.append((xs, nw, proj, hn_t, w_in_l, w_out_l, rg_p, y_rg, hs, ml_p, y_ml, cs, ns, ms, gd_p, y_gd, ss))
        xs = outproj(xs, y_rg, y_ml, y_gd, w_out_l)

    dx, loss_part, d_fw = head(xs, final_norm_w.reshape(1, D), tgt)

    g_small = {k: [None] * DEPTH for k in ("norm_w", "rg_conv_w", "rg_conv_b", "rg_gate_w", "rg_gate_b", "rg_lambda",
                                           "ml_gate_b", "ml_norm_w", "gd_conv_w", "gd_a_log", "gd_dt_bias", "gd_norm_w")}
    for l in reversed(range(DEPTH)):
        x_l, nw, proj, hn_t, w_in_l, w_out_l, rg_p, y_rg, hs, ml_p, y_ml, cs, ns, ms, gd_p, y_gd, ss = acts[l]
        dy_rg, dy_ml, dy_gd, g_wo = outproj_bwd(dx, y_rg, y_ml, y_gd, w_out_l)
        d_rg, d_cw, d_cb, d_gr, d_gi, d_gb, d_lam = rglru_bwd(proj, hs, dy_rg, *rg_p)
        comm.own_w_out_grad(l, g_wo)
        xf = comm.bwd_carry(l, "mlstm")
        d_ml, d_sm_ml, d_bias, d_mnw, *landed = mlstm_bwd(proj, dy_ml, cs, ns, ms, *ml_p, xf=xf)
        if xf is not None:
            comm.bwd_landed(landed)
        xf = comm.bwd_carry(l, "gdn")
        d_gd, d_sm_gd, d_gcw, d_al, d_dt, d_gnw, *landed = gdn_bwd(proj, dy_gd, ss, *gd_p, xf=xf)
        if xf is not None:
            comm.bwd_landed(landed)
        dx, d_nw = inproj_bwd_x(x_l, nw, w_in_l, dx, d_rg, d_ml, d_gd, d_sm_ml, d_sm_gd)
        gw_rg = wgrad(hn_t, [d_rg], "wgrad_rg")
        gw_ml = wgrad(hn_t, [d_ml], "wgrad_ml")
        gw_gd = wgrad(hn_t, [d_gd], "wgrad_gd")
        gw_sm = wgrad(hn_t, [d_sm_ml, d_sm_gd], "wgrad_small")
        comm.grads_ready(l, [gw_rg, gw_ml, gw_sm[:, 0:2 * NH], gw_gd, gw_sm[:, 2 * NH:4 * NH]])
        g_small["norm_w"][l] = d_nw[0]
        g_small["rg_conv_w"][l] = d_cw
        g_small["rg_conv_b"][l] = d_cb[0]
        g_small["rg_gate_w"][l] = jnp.stack([_diag_blocks(d_gr), _diag_blocks(d_gi)])
        g_small["rg_gate_b"][l] = d_gb
        g_small["rg_lambda"][l] = d_lam[0]
        g_small["ml_gate_b"][l] = d_bias[0, 0:2 * NH].reshape(2, NH)
        g_small["ml_norm_w"][l] = d_mnw[0]
        g_small["gd_conv_w"][l] = d_gcw
        g_small["gd_a_log"][l] = d_al[0, 2 * NH:3 * NH]
        g_small["gd_dt_bias"][l] = d_dt[0, 2 * NH:3 * NH]
        g_small["gd_norm_w"][l] = d_gnw[0]
    return loss_part, dx, d_fw, g_small


def finish_step(loss_part, dx, d_fw, g_small, comm, s, me, given_w, given_m, given_v):
    small_names = ["norm_w", "rg_conv_w", "rg_conv_b", "rg_gate_w", "rg_gate_b", "rg_lambda", "ml_gate_b", "ml_norm_w",
                   "gd_conv_w", "gd_a_log", "gd_dt_bias", "gd_norm_w"]
    small_list = [loss_part[0, 0:1], d_fw[0]] + [jnp.stack(g_small[k]) for k in small_names]
    small_shapes = [a.shape for a in small_list]
    packed = _pack(small_list)
    packed = jnp.pad(packed, ((0, -packed.shape[0] % (8 * N_DEV)), (0, 0)))
    g_all = _unpack(small_allreduce(packed, "small_allreduce"), small_shapes)
    gwi_r, gwo_r = comm.finish()

    g_wi, d_wi, m_wi, v_wi = adam_slots(gwi_r, given_w["w_in"], given_m["w_in"], given_v["w_in"], 256, "adam_w_in")
    g_wo, d_wo, m_wo, v_wo = adam_slots(gwo_r, given_w["w_out"], given_m["w_out"], given_v["w_out"], 192, "adam_w_out")
    loss = g_all[0][0]
    grads = {"final_norm_w": g_all[1]}
    for k, g in zip(small_names, g_all[2:]):
        grads[k] = g
    for k, width in (("rg_conv_w", 64), ("rg_gate_b", 64), ("gd_conv_w", 192)):
        grads[k] = lax.dynamic_slice_in_dim(grads[k], me * width, width, axis=2)
    names = small_names + ["final_norm_w"]
    shapes = [given_w[k].shape for k in names]
    d_p, m_p, v_p = adam_flat(_pack([grads[k] for k in names]), _pack([given_w[k] for k in names]),
                              _pack([given_m[k] for k in names]), _pack([given_v[k] for k in names]), "adam_small")
    deltas = dict(zip(names, _unpack(d_p, shapes)))
    new_m = dict(zip(names, _unpack(m_p, shapes)))
    new_v = dict(zip(names, _unpack(v_p, shapes)))
    grads["w_in"], deltas["w_in"], new_m["w_in"], new_v["w_in"] = g_wi, d_wi, m_wi, v_wi
    grads["w_out"], deltas["w_out"], new_m["w_out"], new_v["w_out"] = g_wo, d_wo, m_wo, v_wo

    order = ["norm_w", "w_in", "rg_conv_w", "rg_conv_b", "rg_gate_w", "rg_gate_b", "rg_lambda", "ml_gate_b", "ml_norm_w",
             "gd_conv_w", "gd_a_log", "gd_dt_bias", "gd_norm_w", "w_out", "final_norm_w"]
    return (loss, dx.reshape(1, s, D), *[grads[k] for k in order], *[deltas[k] for k in order],
            *[new_m[k] for k in order], *[new_v[k] for k in order])
```

```python
import functools
from typing import NamedTuple

import jax
import jax.numpy as jnp
from jax import lax
from jax.experimental import pallas as pl
from jax.experimental.pallas import tpu as pltpu

f32 = jnp.float32
bf16 = jnp.bfloat16
MESH = pl.DeviceIdType.MESH

N_DEV = 8
D = 1024
DEPTH = 4
EPS = 1e-6
SEG = 512
HD = 128
NH = 4
LC = 64
RUN = 4
LR = RUN * LC
D_IN = 5648
DP = 5760
SMALL0 = 5632
RG_TILE = 256
RG_C = 8.0

ADAM_LR = 0.001
ADAM_B1 = 0.9
ADAM_B2 = 0.999
ADAM_EPS = 1e-08
ADAM_WD = 0.01
ADAM_STEP = 10


def _cp(vmem_mb):
    return pltpu.CompilerParams(vmem_limit_bytes=vmem_mb * 2 ** 20)


def _dot(a, b, ca, cb):
    return lax.dot_general(a.astype(bf16), b.astype(bf16), (((ca,), (cb,)), ((), ())), preferred_element_type=f32)


@jax.custom_vjp
def mm_nn(a, b):
    return _dot(a, b, 1, 0)


@jax.custom_vjp
def mm_nt(a, b):
    return _dot(a, b, 1, 1)


@jax.custom_vjp
def mm_tn(a, b):
    return _dot(a, b, 0, 0)


mm_nn.defvjp(lambda a, b: (mm_nn(a, b), (a, b)), lambda r, g: (mm_nt(g, r[1]), mm_tn(r[0], g)))
mm_nt.defvjp(lambda a, b: (mm_nt(a, b), (a, b)), lambda r, g: (mm_nn(g, r[1]), mm_tn(g, r[0])))
mm_tn.defvjp(lambda a, b: (mm_tn(a, b), (a, b)), lambda r, g: (mm_nt(r[1], g), mm_nn(r[0], g)))


def _split(x):
    hi = x.astype(bf16)
    return hi, (x - hi.astype(f32)).astype(bf16)


def dot3(a, b):
    ah, al = _split(a)
    bh, bl = _split(b)
    d = functools.partial(jnp.dot, preferred_element_type=f32)
    return d(ah, bh) + (d(al, bh) + d(ah, bl))


def _tri_sum(x, reverse):
    n = x.shape[0]
    r = lax.broadcasted_iota(jnp.int32, (n, 3 * n), 0)
    c = lax.broadcasted_iota(jnp.int32, (n, 3 * n), 1) & (n - 1)
    ones = ((c >= r) if reverse else (c <= r)).astype(bf16)
    hi = x.astype(bf16)
    rest = x - hi.astype(f32)
    mid = rest.astype(bf16)
    lo = (rest - mid.astype(f32)).astype(bf16)
    return jnp.dot(ones, jnp.concatenate([hi, mid, lo], axis=0), preferred_element_type=f32)


@jax.custom_vjp
def cumsum_rows(x):
    return _tri_sum(x, False)


@jax.custom_vjp
def rev_cumsum_rows(x):
    return _tri_sum(x, True)


cumsum_rows.defvjp(lambda x: (cumsum_rows(x), None), lambda _, g: (rev_cumsum_rows(g),))
rev_cumsum_rows.defvjp(lambda x: (rev_cumsum_rows(x), None), lambda _, g: (cumsum_rows(g),))


def _tri(n, strict=False):
    r = lax.broadcasted_iota(jnp.int32, (n, n), 0)
    c = lax.broadcasted_iota(jnp.int32, (n, n), 1)
    return (r > c) if strict else (r >= c)


def _lane_col(v, j):
    lane = lax.broadcasted_iota(jnp.int32, v.shape, 1)
    return jnp.sum(jnp.where(lane == j, v, 0.0), axis=1, keepdims=True)


def _rows_from(pieces, rows, width):
    ridx = lax.broadcasted_iota(jnp.int32, (rows, width), 0)
    out = jnp.zeros((rows, width), f32)
    for h, p in enumerate(pieces):
        out = out + jnp.where(ridx == h, jnp.broadcast_to(p, (rows, width)), 0.0)
    return out


def causal_conv(halo8, x, w4):
    t = x.shape[0]
    xe = jnp.concatenate([halo8, x], axis=0)
    y = xe[5:5 + t] * w4[0:1]
    for k in range(1, 4):
        y = y + xe[5 + k:5 + k + t] * w4[k:k + 1]
    return y


def ml_chunk(q, k, v, o_pre, z, small, bias_row, norm_w, C, n, m):
    n_ch = q.shape[0] // LC
    lane = lax.broadcasted_iota(jnp.int32, small.shape, 1)
    pre = small + bias_row
    lg = jnp.where(lane < NH, pre, jnp.where(lane < 2 * NH, jax.nn.log_sigmoid(pre), 0.0))
    rows = [slice(c * LC, (c + 1) * LC) for c in range(n_ch)]
    lgs = [lg[r] for r in rows]
    bcs = [cumsum_rows(x) for x in lgs]
    lg_ts = [x.T for x in lgs]
    bc_ts = [x.T for x in bcs]
    causal = _tri(LC)
    pairs = [(c, h) for c in range(n_ch) for h in range(NH)]
    idx = range(len(pairs))
    sls = [slice(h * HD, (h + 1) * HD) for h in range(NH)]
    qs = [q[rows[c], sls[h]] * (HD ** -0.5) for c, h in pairs]
    ks = [k[rows[c], sls[h]] for c, h in pairs]
    vs = [v[rows[c], sls[h]] for c, h in pairs]
    li_cols = [_lane_col(lgs[c], h) for c, h in pairs]
    b_cols = [_lane_col(bcs[c], NH + h) for c, h in pairs]
    dms = [jnp.where(causal, b_cols[i] - bc_ts[c][NH + h:NH + h + 1, :] + lg_ts[c][h:h + 1, :], -jnp.inf)
           for i, (c, h) in enumerate(pairs)]
    dm_maxs = [jnp.max(dms[i], axis=-1, keepdims=True) for i in idx]
    gs = [b_cols[i][LC - 1:LC, :] for i in idx]
    wss = [gs[i] - b_cols[i] + li_cols[i] for i in idx]
    ws_maxs = [jnp.max(wss[i], axis=0, keepdims=True) for i in idx]
    qks = [mm_nt(qs[i], ks[i]) for i in idx]
    m_in = [None] * len(pairs)
    m_out = [None] * len(pairs)
    for h in range(NH):
        cur = m[h:h + 1, 0:1]
        for c in range(n_ch):
            i = c * NH + h
            m_in[i] = cur
            cur = jnp.maximum(gs[i] + cur, ws_maxs[i])
            m_out[i] = cur
    m_inters = [b_cols[i] + m_in[i] for i in idx]
    m_ts = [jnp.maximum(m_inters[i], dm_maxs[i]) for i in idx]
    ss = [qks[i] * jnp.exp(dms[i] - m_ts[i]) for i in idx]
    scs = [jnp.exp(m_inters[i] - m_ts[i]) for i in idx]
    decs = [jnp.exp(gs[i] + m_in[i] - m_out[i]) for i in idx]
    kws = [jnp.exp(wss[i] - m_out[i]) * ks[i] for i in idx]
    c_adds = [mm_tn(kws[i], vs[i]) for i in idx]
    n_adds = [jnp.sum(kws[i], axis=0, keepdims=True) for i in idx]
    svs = [mm_nn(ss[i], vs[i]) for i in idx]
    s_sums = [jnp.sum(ss[i], axis=-1, keepdims=True) for i in idx]
    c_hs = [C[sl, :] for sl in sls]
    n_hs = [n[h:h + 1, :] for h in range(NH)]
    hhs = [None] * len(pairs)
    for c in range(n_ch):
        for h in range(NH):
            i = c * NH + h
            num = svs[i] + scs[i] * mm_nn(qs[i], c_hs[h])
            den = s_sums[i] + scs[i] * jnp.sum(qs[i] * n_hs[h], axis=-1, keepdims=True)
            hhs[i] = num / jnp.maximum(jnp.abs(den), jnp.exp(-m_ts[i]))
        c_hs = [decs[c * NH + h] * c_hs[h] + c_adds[c * NH + h] for h in range(NH)]
        n_hs = [decs[c * NH + h] * n_hs[h] + n_adds[c * NH + h] for h in range(NH)]
    ys = [hhs[i] * lax.rsqrt(jnp.mean(hhs[i] * hhs[i], axis=-1, keepdims=True) + EPS) * norm_w[:, sls[h]]
          * jax.nn.sigmoid(o_pre[rows[c], sls[h]]) * jax.nn.silu(z[rows[c], sls[h]]) for i, (c, h) in enumerate(pairs)]
    y = jnp.concatenate([jnp.concatenate(ys[c * NH:(c + 1) * NH], axis=1) for c in range(n_ch)], axis=0)
    last = (n_ch - 1) * NH
    m_rows = [jnp.broadcast_to(m_out[last + h], (1, HD)) for h in range(NH)]
    return y, jnp.concatenate(c_hs, axis=0), _rows_from(n_hs, 8, HD), _rows_from(m_rows, 8, HD)


@jax.custom_vjp
def _unit_lower_inverses(mats):
    eye = (lax.broadcasted_iota(jnp.int32, (LC, LC), 0) == lax.broadcasted_iota(jnp.int32, (LC, LC), 1)).astype(f32)
    ps = [-m for m in mats]
    ts = [eye + p for p in ps]
    for _ in range(5):
        ps = [dot3(p, p) for p in ps]
        ts = [t + dot3(t, p) for t, p in zip(ts, ps)]
    return tuple(ts)


def _unit_lower_inverses_fwd(mats):
    ts = _unit_lower_inverses(mats)
    return ts, ts


def _unit_lower_inverses_bwd(ts, gs):
    tts = [t.T for t in ts]
    mid = [dot3(tt, g) for tt, g in zip(tts, gs)]
    return (tuple(-dot3(m, tt) for m, tt in zip(mid, tts)),)


_unit_lower_inverses.defvjp(_unit_lower_inverses_fwd, _unit_lower_inverses_bwd)


def gd_chunk(qh8, q, kh8, k, vh8, v, z, small, conv_w, alog_row, dt_row, norm_w, st):
    n_ch = q.shape[0] // LC
    lane = lax.broadcasted_iota(jnp.int32, small.shape, 1)
    is_a = (lane >= 2 * NH) & (lane < 3 * NH)
    g_all = jnp.where(is_a, -jnp.exp(alog_row) * jax.nn.softplus(small + dt_row), 0.0)
    beta_all = jax.nn.sigmoid(small)
    qc = jax.nn.silu(causal_conv(qh8, q, conv_w[:, 0:SEG]))
    kc = jax.nn.silu(causal_conv(kh8, k, conv_w[:, SEG:2 * SEG]))
    vc = jax.nn.silu(causal_conv(vh8, v, conv_w[:, 2 * SEG:3 * SEG]))
    incl = _tri(LC)
    strict = _tri(LC, strict=True)
    rows = [slice(c * LC, (c + 1) * LC) for c in range(n_ch)]
    gc_alls = [cumsum_rows(g_all[r]) for r in rows]
    gc_ts = [g.T for g in gc_alls]
    pairs = [(c, h) for c in range(n_ch) for h in range(NH)]
    idx = range(len(pairs))
    sls = [slice(h * HD, (h + 1) * HD) for h in range(NH)]
    qs = [qc[rows[c], sls[h]] for c, h in pairs]
    ks = [kc[rows[c], sls[h]] for c, h in pairs]
    vs = [vc[rows[c], sls[h]] for c, h in pairs]
    qs = [x * lax.rsqrt(jnp.sum(x * x, axis=-1, keepdims=True) + EPS) * (HD ** -0.5) for x in qs]
    ks = [x * lax.rsqrt(jnp.sum(x * x, axis=-1, keepdims=True) + EPS) for x in ks]
    betas = [_lane_col(beta_all[rows[c]], 3 * NH + h) for c, h in pairs]
    gcs = [_lane_col(gc_alls[c], 2 * NH + h) for c, h in pairs]
    gams = [jnp.exp(jnp.where(incl, gcs[i] - gc_ts[c][2 * NH + h:2 * NH + h + 1, :], -jnp.inf))
            for i, (c, h) in enumerate(pairs)]
    kbs = [ks[i] * betas[i] for i in idx]
    mats = tuple(jnp.where(strict, mm_nt(kbs[i], ks[i]) * gams[i], 0.0) for i in idx)
    aqks = [mm_nt(qs[i], ks[i]) * gams[i] for i in idx]
    t_invs = _unit_lower_inverses(mats)
    us = [mm_nn(t_invs[i], vs[i] * betas[i]) for i in idx]
    ws = [mm_nn(t_invs[i], kbs[i] * jnp.exp(gcs[i])) for i in idx]
    g_lasts = [gcs[i][LC - 1:LC, :] for i in idx]
    q_decs = [qs[i] * jnp.exp(gcs[i]) for i in idx]
    k_decs = [ks[i] * jnp.exp(g_lasts[i] - gcs[i]) for i in idx]
    e_lasts = [jnp.exp(g_lasts[i]) for i in idx]
    s_hs = [st[sl, :] for sl in sls]
    os_ = [None] * len(pairs)
    for c in range(n_ch):
        ids = [c * NH + h for h in range(NH)]
        v_news = [us[i] - mm_nn(ws[i], s_hs[h]) for h, i in enumerate(ids)]
        for h, i in enumerate(ids):
            os_[i] = mm_nn(q_decs[i], s_hs[h]) + mm_nn(aqks[i], v_news[h])
        s_hs = [s_hs[h] * e_lasts[i] + mm_tn(k_decs[i], v_news[h]) for h, i in enumerate(ids)]
    ys = [os_[i] * lax.rsqrt(jnp.mean(os_[i] * os_[i], axis=-1, keepdims=True) + EPS) * norm_w
          * jax.nn.silu(z[rows[c], sls[h]]) for i, (c, h) in enumerate(pairs)]
    y = jnp.concatenate([jnp.concatenate(ys[c * NH:(c + 1) * NH], axis=1) for c in range(n_ch)], axis=0)
    return y, jnp.concatenate(s_hs, axis=0)


def rg_pre(xh8, x, conv_w, conv_b, gw_r, gw_i, gate_b, lam):
    xc = causal_conv(xh8, x, conv_w) + conv_b
    r = jax.nn.sigmoid(mm_nn(xc, gw_r) + gate_b[0:1])
    i = jax.nn.sigmoid(mm_nn(xc, gw_i) + gate_b[1:2])
    log_a = -RG_C * r * jax.nn.softplus(-lam)
    a = jnp.exp(log_a)
    th = jnp.tanh(log_a)
    one_minus_a2 = -2.0 * th / (1.0 - th)
    b = jnp.sqrt(one_minus_a2) * (i * xc)
    return a, b


def _scan_rows(a_ref, b_ref, out_ref, h0, n_rows, reverse):
    n_groups = n_rows // 8
    width = a_ref.shape[1]
    row = lax.broadcasted_iota(jnp.int32, (8, width), 0)

    def body(j, h):
        g = (n_groups - 1 - j) if reverse else j
        r0 = pl.multiple_of(g * 8, 8)
        av = a_ref[pl.ds(r0, 8), :]
        bv = b_ref[pl.ds(r0, 8), :]
        for d in (1, 2, 4):
            sh = (8 - d) if reverse else d
            a_s = pltpu.roll(av, sh, 0)
            b_s = pltpu.roll(bv, sh, 0)
            valid = (row < 8 - d) if reverse else (row >= d)
            bv = jnp.where(valid, av * b_s + bv, bv)
            av = jnp.where(valid, av * a_s, av)
        hv = av * h + bv
        out_ref[pl.ds(r0, 8), :] = hv
        return hv[0:1, :] if reverse else hv[7:8, :]

    return lax.fori_loop(0, n_groups, body, h0)


def _norm_rows(xv):
    r = lax.rsqrt(jnp.mean(xv * xv, axis=-1, keepdims=True) + EPS)
    return xv * r, r


def inproj(x, nw, w, xf=None):
    s, ts = x.shape[0], 256
    nt = s // ts
    n_in, n_out = 3, 2
    x_ops, x_in_specs, x_out_shapes, x_out_specs, aliases, x_sems, n_xin, n_land = _carried(xf, n_in, n_out)

    def body(*refs):
        (x_ref, nw_ref, w_ref), x_refs, (o_ref, ht_ref), land_refs, _, sem_refs = _cut_refs(refs, n_in, n_xin, n_out, n_land, 0)
        i = pl.program_id(0)
        _carry_open(xf, i, nt, x_refs, land_refs, sem_refs)
        xn, _ = _norm_rows(x_ref[...])
        hn = xn * nw_ref[...]
        o_ref[...] = jnp.dot(hn.astype(bf16), w_ref[...], preferred_element_type=f32)
        ht_ref[...] = hn.T.astype(bf16)
        _carry_close(xf, i, nt, x_refs, land_refs, sem_refs)

    return pl.pallas_call(
        body, grid=(nt,),
        in_specs=[pl.BlockSpec((ts, D), lambda i: (i, 0)), pl.BlockSpec((1, D), lambda i: (0, 0)),
                  pl.BlockSpec((D, DP), lambda i: (0, 0))] + x_in_specs,
        out_specs=[pl.BlockSpec((ts, DP), lambda i: (i, 0)), pl.BlockSpec((D, ts), lambda i: (0, i))] + x_out_specs,
        out_shape=[jax.ShapeDtypeStruct((s, DP), f32), jax.ShapeDtypeStruct((D, s), bf16)] + x_out_shapes,
        scratch_shapes=x_sems, input_output_aliases=aliases,
        name="inproj" if xf is None else "inproj_carrying", compiler_params=_cp(56))(x, nw, w, *x_ops)


def outproj(x, yr, ym, yg, wo):
    s, ts = x.shape[0], 256

    def body(x_ref, yr_ref, ym_ref, yg_ref, wo_ref, o_ref):
        acc = x_ref[...]
        for j, y_ref in enumerate((yr_ref, ym_ref, yg_ref)):
            acc = acc + jnp.dot(y_ref[...].astype(bf16), wo_ref[j * SEG:(j + 1) * SEG, :], preferred_element_type=f32)
        o_ref[...] = acc

    yspec = pl.BlockSpec((ts, SEG), lambda i: (i, 0))
    return pl.pallas_call(
        body, grid=(s // ts,),
        in_specs=[pl.BlockSpec((ts, D), lambda i: (i, 0)), yspec, yspec, yspec,
                  pl.BlockSpec((3 * SEG, D), lambda i: (0, 0))],
        out_specs=pl.BlockSpec((ts, D), lambda i: (i, 0)),
        out_shape=jax.ShapeDtypeStruct((s, D), f32), name="outproj", compiler_params=_cp(40))(x, yr, ym, yg, wo)


def head(x, fw, tgt):
    s, ts = x.shape[0], 256

    def body(x_ref, fw_ref, t_ref, dx_ref, loss_ref, dfw_ref):
        @pl.when(pl.program_id(0) == 0)
        def _():
            loss_ref[...] = jnp.zeros_like(loss_ref)
            dfw_ref[...] = jnp.zeros_like(dfw_ref)

        xn, r = _norm_rows(x_ref[...])
        fw_v = fw_ref[...]
        err = xn * fw_v - t_ref[...]
        loss_ref[...] += 0.5 * jnp.sum(jnp.mean(err * err, axis=-1, keepdims=True))
        dy = err * (1.0 / D)
        dfw_ref[0:1, :] += jnp.sum(dy * xn, axis=0, keepdims=True)
        dxn = dy * fw_v
        dx_ref[...] = r * (dxn - xn * jnp.mean(dxn * xn, axis=-1, keepdims=True))

    tile = pl.BlockSpec((ts, D), lambda i: (i, 0))
    return pl.pallas_call(
        body, grid=(s // ts,),
        in_specs=[tile, pl.BlockSpec((1, D), lambda i: (0, 0)), tile],
        out_specs=[tile, pl.BlockSpec((8, 128), lambda i: (0, 0)), pl.BlockSpec((8, D), lambda i: (0, 0))],
        out_shape=[jax.ShapeDtypeStruct((s, D), f32), jax.ShapeDtypeStruct((8, 128), f32),
                   jax.ShapeDtypeStruct((8, D), f32)], name="head")(x, fw, tgt)


def outproj_bwd(dx, yr, ym, yg, wo):
    s, ts = dx.shape[0], 256

    def body(dx_ref, yr_ref, ym_ref, yg_ref, wo_ref, dyr_ref, dym_ref, dyg_ref, gwo_ref):
        @pl.when(pl.program_id(0) == 0)
        def _():
            gwo_ref[...] = jnp.zeros_like(gwo_ref)

        dxb = dx_ref[...].astype(bf16)
        for j, (y_ref, dy_ref) in enumerate(((yr_ref, dyr_ref), (ym_ref, dym_ref), (yg_ref, dyg_ref))):
            rows = slice(j * SEG, (j + 1) * SEG)
            dy_ref[...] = lax.dot_general(dxb, wo_ref[rows, :], (((1,), (1,)), ((), ())), preferred_element_type=f32)
            gwo_ref[rows, :] += lax.dot_general(y_ref[...].astype(bf16), dxb, (((0,), (0,)), ((), ())),
                                                preferred_element_type=f32)

    yspec = pl.BlockSpec((ts, SEG), lambda i: (i, 0))
    wspec = pl.BlockSpec((3 * SEG, D), lambda i: (0, 0))
    ysh = jax.ShapeDtypeStruct((s, SEG), f32)
    return pl.pallas_call(
        body, grid=(s // ts,),
        in_specs=[pl.BlockSpec((ts, D), lambda i: (i, 0)), yspec, yspec, yspec, wspec],
        out_specs=[yspec, yspec, yspec, wspec],
        out_shape=[ysh, ysh, ysh, jax.ShapeDtypeStruct((3 * SEG, D), f32)],
        name="outproj_bwd", compiler_params=_cp(48))(dx, yr, ym, yg, wo)


def inproj_bwd_x(x, nw, w, dxo, d_rg, d_ml, d_gd, d_sa, d_sb):
    s, ts = x.shape[0], 256
    widths = (d_rg.shape[1], d_ml.shape[1], d_gd.shape[1], HD)

    def body(x_ref, nw_ref, w_ref, dxo_ref, rg_ref, ml_ref, gd_ref, sa_ref, sb_ref, dx_ref, dnw_ref):
        @pl.when(pl.program_id(0) == 0)
        def _():
            dnw_ref[...] = jnp.zeros_like(dnw_ref)

        xn, r = _norm_rows(x_ref[...])
        pieces = (rg_ref[...], ml_ref[...], gd_ref[...], sa_ref[...] + sb_ref[...])
        dhn = jnp.zeros((ts, D), f32)
        c0 = 0
        for piece, wd in zip(pieces, widths):
            dhn = dhn + lax.dot_general(piece.astype(bf16), w_ref[:, c0:c0 + wd], (((1,), (1,)), ((), ())),
                                        preferred_element_type=f32)
            c0 += wd
        dnw_ref[0:1, :] += jnp.sum(dhn * xn, axis=0, keepdims=True)
        dxn = dhn * nw_ref[...]
        dx_ref[...] = dxo_ref[...] + r * (dxn - xn * jnp.mean(dxn * xn, axis=-1, keepdims=True))

    tile = pl.BlockSpec((ts, D), lambda i: (i, 0))
    return pl.pallas_call(
        body, grid=(s // ts,),
        in_specs=[tile, pl.BlockSpec((1, D), lambda i: (0, 0)), pl.BlockSpec((D, DP), lambda i: (0, 0)), tile]
        + [pl.BlockSpec((ts, wd), lambda i: (i, 0)) for wd in widths] + [pl.BlockSpec((ts, HD), lambda i: (i, 0))],
        out_specs=[tile, pl.BlockSpec((8, D), lambda i: (0, 0))],
        out_shape=[jax.ShapeDtypeStruct((s, D), f32), jax.ShapeDtypeStruct((8, D), f32)],
        name="inproj_bwd_x", compiler_params=_cp(56))(x, nw, w, dxo, d_rg, d_ml, d_gd, d_sa, d_sb)


def wgrad(hn_t, dps, name):
    s, ts = hn_t.shape[1], 512
    c = dps[0].shape[1]
    ct = min(c, SEG)
    n_dp = len(dps)

    def body(*refs):
        ht_ref = refs[0]
        dp_refs = refs[1:1 + n_dp]
        o_ref = refs[1 + n_dp]

        @pl.when(pl.program_id(1) == 0)
        def _():
            o_ref[...] = jnp.zeros_like(o_ref)

        dp = dp_refs[0][...]
        for extra in dp_refs[1:]:
            dp = dp + extra[...]
        o_ref[...] += jnp.dot(ht_ref[...], dp.astype(bf16), preferred_element_type=f32)

    return pl.pallas_call(
        body, grid=(c // ct, s // ts),
        in_specs=[pl.BlockSpec((D, ts), lambda j, i: (0, i))] + [pl.BlockSpec((ts, ct), lambda j, i: (i, j)) for _ in dps],
        out_specs=pl.BlockSpec((D, ct), lambda j, i: (0, j)),
        out_shape=jax.ShapeDtypeStruct((D, c), f32), name=name, compiler_params=_cp(40))(hn_t, *dps)


def _seg_spec(rows, seg, n_tiles=None):
    if n_tiles is None:
        return pl.BlockSpec((rows, SEG), lambda i: (i, seg))
    return pl.BlockSpec((rows, SEG), lambda i: (n_tiles - 1 - i, seg))


def _halo_spec(rows, seg, n_tiles=None):
    per = rows // 8
    if n_tiles is None:
        return pl.BlockSpec((8, SEG), lambda i: (jnp.maximum(i * per - 1, 0), seg))
    return pl.BlockSpec((8, SEG), lambda i: (jnp.maximum((n_tiles - 1 - i) * per - 1, 0), seg))


def _const_spec(shape):
    return pl.BlockSpec(shape, lambda i: tuple(0 for _ in shape))


def rglru_fwd(proj, conv_w, conv_b, gw_r, gw_i, gate_b, lam):
    s = proj.shape[0]
    tr = RG_TILE

    def body(xh_ref, x_ref, z_ref, cw_ref, cb_ref, gr_ref, gi_ref, gb_ref, lam_ref, y_ref, h_ref, a_s, b_s, hc):
        first = pl.program_id(0) == 0

        @pl.when(first)
        def _():
            hc[...] = jnp.zeros_like(hc)

        xh = jnp.where(first, 0.0, xh_ref[...])
        a, b = rg_pre(xh, x_ref[...], cw_ref[...], cb_ref[...], gr_ref[...], gi_ref[...], gb_ref[...], lam_ref[...])
        a_s[...] = a
        b_s[...] = b
        hc[0:1, :] = _scan_rows(a_s, b_s, h_ref, hc[0:1, :], tr, False)
        y_ref[...] = (h_ref[...] * jax.nn.silu(z_ref[...])).astype(bf16)

    out = pl.BlockSpec((tr, SEG), lambda i: (i, 0))
    return pl.pallas_call(
        body, grid=(s // tr,),
        in_specs=[_halo_spec(tr, 0), _seg_spec(tr, 0), _seg_spec(tr, 1), _const_spec((4, SEG)), _const_spec((1, SEG)),
                  _const_spec((SEG, SEG)), _const_spec((SEG, SEG)), _const_spec((2, SEG)), _const_spec((1, SEG))],
        out_specs=[out, out],
        out_shape=[jax.ShapeDtypeStruct((s, SEG), bf16), jax.ShapeDtypeStruct((s, SEG), f32)],
        scratch_shapes=[pltpu.VMEM((tr, SEG), f32), pltpu.VMEM((tr, SEG), f32), pltpu.VMEM((8, SEG), f32)],
        name="rglru_fwd", compiler_params=_cp(40))(proj, proj, proj, conv_w, conv_b, gw_r, gw_i, gate_b, lam)


def rglru_bwd(proj, hs, dy, conv_w, conv_b, gw_r, gw_i, gate_b, lam):
    s = proj.shape[0]
    tr = RG_TILE
    nt = s // tr

    def body(xh_ref, x_ref, z_ref, hh_ref, h_ref, dy_ref, cw_ref, cb_ref, gr_ref, gi_ref, gb_ref, lam_ref,
             dp_ref, dcw_ref, dcb_ref, dgr_ref, dgi_ref, dgb_ref, dlam_ref,
             an_s, g_s, dh_s, a_first, dh_first, dhalo):
        i = pl.program_id(0)
        first_tile = i == nt - 1

        @pl.when(i == 0)
        def _():
            for ref in (dcw_ref, dcb_ref, dgr_ref, dgi_ref, dgb_ref, dlam_ref, a_first, dh_first, dhalo):
                ref[...] = jnp.zeros_like(ref)

        xh = jnp.where(first_tile, 0.0, xh_ref[...])
        params = (cw_ref[...], cb_ref[...], gr_ref[...], gi_ref[...], gb_ref[...], lam_ref[...])
        (a, _), vjp = jax.vjp(rg_pre, xh, x_ref[...], *params)
        zv = z_ref[...]
        hv = h_ref[...]
        dyv = dy_ref[...]
        sig = jax.nn.sigmoid(zv)
        g_s[...] = dyv * (zv * sig)
        dp_ref[:, SEG:2 * SEG] = (dyv * hv * (sig * (1.0 + zv * (1.0 - sig)))).astype(bf16)
        ridx = lax.broadcasted_iota(jnp.int32, (tr, SEG), 0)
        an_s[...] = jnp.where(ridx == tr - 1, jnp.broadcast_to(a_first[0:1, :], (tr, SEG)), pltpu.roll(a, tr - 1, 0))
        _scan_rows(an_s, g_s, dh_s, dh_first[0:1, :], tr, True)
        dh = dh_s[...]
        h_prev_last = jnp.where(first_tile, 0.0, hh_ref[...])[7:8, :]
        h_prev = pltpu.roll(hv, 1, 0)
        h_prev = jnp.where(ridx == 0, jnp.broadcast_to(h_prev_last, (tr, SEG)), h_prev)
        dxh, dx, dcw, dcb, dgr, dgi, dgb, dlam = vjp((dh * h_prev, dh))
        dx = dx + jnp.concatenate([jnp.zeros((tr - 8, SEG), f32), dhalo[...]], axis=0)
        dp_ref[:, 0:SEG] = dx.astype(bf16)
        dhalo[...] = dxh
        a_first[0:1, :] = a[0:1, :]
        dh_first[0:1, :] = dh[0:1, :]
        dcw_ref[...] += dcw
        dcb_ref[...] += dcb
        dgr_ref[...] += dgr
        dgi_ref[...] += dgi
        dgb_ref[...] += dgb
        dlam_ref[...] += dlam

    pspecs = [_const_spec((4, SEG)), _const_spec((1, SEG)), _const_spec((SEG, SEG)), _const_spec((SEG, SEG)),
              _const_spec((2, SEG)), _const_spec((1, SEG))]
    pshapes = [jax.ShapeDtypeStruct(sh, f32) for sh in ((4, SEG), (1, SEG), (SEG, SEG), (SEG, SEG), (2, SEG), (1, SEG))]
    tile = pl.BlockSpec((tr, SEG), lambda i: (nt - 1 - i, 0))
    return pl.pallas_call(
        body, grid=(nt,),
        in_specs=[_halo_spec(tr, 0, nt), _seg_spec(tr, 0, nt), _seg_spec(tr, 1, nt),
                  pl.BlockSpec((8, SEG), lambda i: (jnp.maximum((nt - 1 - i) * (tr // 8) - 1, 0), 0)), tile, tile] + pspecs,
        out_specs=[pl.BlockSpec((tr, 2 * SEG), lambda i: (nt - 1 - i, 0))] + pspecs,
        out_shape=[jax.ShapeDtypeStruct((s, 2 * SEG), bf16)] + pshapes,
        scratch_shapes=[pltpu.VMEM((tr, SEG), f32), pltpu.VMEM((tr, SEG), f32), pltpu.VMEM((tr, SEG), f32),
                        pltpu.VMEM((8, SEG), f32), pltpu.VMEM((8, SEG), f32), pltpu.VMEM((8, SEG), f32)],
        name="rglru_bwd", compiler_params=_cp(48))(proj, proj, proj, hs, hs, dy, conv_w, conv_b, gw_r, gw_i, gate_b, lam)


ML_SEGS = (2, 3, 4, 5, 6)
SMALL_BLK = SMALL0 // HD


def _cut_refs(refs, n_in, n_xin, n_out, n_land, n_scratch):
    bounds = [0, n_in, n_in + n_xin, n_in + n_xin + n_out, n_in + n_xin + n_out + n_land,
              n_in + n_xin + n_out + n_land + n_scratch, len(refs)]
    return [refs[a:b] for a, b in zip(bounds[:-1], bounds[1:])]


def mlstm_fwd(proj, bias_row, norm_w, xf=None):
    s = proj.shape[0]
    nc = s // LR
    n_in, n_out = 8, 4
    x_ops, x_in_specs, x_out_shapes, x_out_specs, aliases, x_sems, n_xin, n_land = _carried(xf, n_in, n_out)

    def body(*refs):
        ins, x_refs, outs, land_refs, scratch, sem_refs = _cut_refs(refs, n_in, n_xin, n_out, n_land, 3)
        q_ref, k_ref, v_ref, o_ref, z_ref, sm_ref, b_ref, nw_ref = ins
        y_ref, cs_ref, ns_ref, ms_ref = outs
        c_s, n_s, m_s = scratch
        i = pl.program_id(0)
        _carry_open(xf, i, nc, x_refs, land_refs, sem_refs)

        @pl.when(i == 0)
        def _():
            c_s[...] = jnp.zeros_like(c_s)
            n_s[...] = jnp.zeros_like(n_s)
            m_s[...] = jnp.zeros_like(m_s)

        cs_ref[0] = c_s[...]
        ns_ref[0] = n_s[...]
        ms_ref[0] = m_s[...]
        y, c2, n2, m2 = ml_chunk(q_ref[...], k_ref[...], v_ref[...], o_ref[...], z_ref[...], sm_ref[...],
                                 b_ref[...], nw_ref[...], c_s[...], n_s[...], m_s[...])
        y_ref[...] = y.astype(bf16)
        c_s[...] = c2
        n_s[...] = n2
        m_s[...] = m2
        _carry_close(xf, i, nc, x_refs, land_refs, sem_refs)

    return pl.pallas_call(
        body, grid=(nc,),
        in_specs=[_seg_spec(LR, sg) for sg in ML_SEGS]
        + [pl.BlockSpec((LR, HD), lambda i: (i, SMALL_BLK)), _const_spec((1, HD)), _const_spec((1, SEG))] + x_in_specs,
        out_specs=[pl.BlockSpec((LR, SEG), lambda i: (i, 0)), pl.BlockSpec((1, SEG, HD), lambda i: (i, 0, 0)),
                   pl.BlockSpec((1, 8, HD), lambda i: (i, 0, 0)), pl.BlockSpec((1, 8, HD), lambda i: (i, 0, 0))] + x_out_specs,
        out_shape=[jax.ShapeDtypeStruct((s, SEG), bf16), jax.ShapeDtypeStruct((nc, SEG, HD), f32),
                   jax.ShapeDtypeStruct((nc, 8, HD), f32), jax.ShapeDtypeStruct((nc, 8, HD), f32)] + x_out_shapes,
        scratch_shapes=[pltpu.VMEM((SEG, HD), f32), pltpu.VMEM((8, HD), f32), pltpu.VMEM((8, HD), f32)] + x_sems,
        input_output_aliases=aliases, name="mlstm_fwd" if xf is None else "mlstm_fwd_carrying")(
            proj, proj, proj, proj, proj, proj, bias_row, norm_w, *x_ops)


def mlstm_bwd(proj, dy, cs, ns, ms, bias_row, norm_w, xf=None):
    s = proj.shape[0]
    nc = s // LR
    n_in, n_out = 12, 4
    x_ops, x_in_specs, x_out_shapes, x_out_specs, aliases, x_sems, n_xin, n_land = _carried(xf, n_in, n_out)

    def body(*refs):
        ins, x_refs, outs, land_refs, scratch, sem_refs = _cut_refs(refs, n_in, n_xin, n_out, n_land, 3)
        q_ref, k_ref, v_ref, o_ref, z_ref, sm_ref, dy_ref, cs_ref, ns_ref, ms_ref, b_ref, nw_ref = ins
        dp_ref, dsm_ref, db_ref, dnw_ref = outs
        dc_s, dn_s, dm_s = scratch
        i = pl.program_id(0)
        _carry_open(xf, i, nc, x_refs, land_refs, sem_refs)

        @pl.when(i == 0)
        def _():
            for ref in (db_ref, dnw_ref, dc_s, dn_s, dm_s):
                ref[...] = jnp.zeros_like(ref)

        _, vjp = jax.vjp(ml_chunk, q_ref[...], k_ref[...], v_ref[...], o_ref[...], z_ref[...], sm_ref[...],
                         b_ref[...], nw_ref[...], cs_ref[0], ns_ref[0], ms_ref[0])
        dq, dk, dv, do, dz, dsm, db, dnw, dc, dn, dm = vjp((dy_ref[...], dc_s[...], dn_s[...], dm_s[...]))
        for j, val in enumerate((dq, dk, dv, do, dz)):
            dp_ref[:, j * SEG:(j + 1) * SEG] = val.astype(bf16)
        dsm_ref[...] = dsm
        db_ref[0:1, :] += db
        dnw_ref[0:1, :] += dnw
        dc_s[...] = dc
        dn_s[...] = dn
        dm_s[...] = dm
        _carry_close(xf, i, nc, x_refs, land_refs, sem_refs)

    rev3 = lambda i: (nc - 1 - i, 0, 0)
    return pl.pallas_call(
        body, grid=(nc,),
        in_specs=[_seg_spec(LR, sg, nc) for sg in ML_SEGS]
        + [pl.BlockSpec((LR, HD), lambda i: (nc - 1 - i, SMALL_BLK)), pl.BlockSpec((LR, SEG), lambda i: (nc - 1 - i, 0)),
           pl.BlockSpec((1, SEG, HD), rev3), pl.BlockSpec((1, 8, HD), rev3), pl.BlockSpec((1, 8, HD), rev3),
           _const_spec((1, HD)), _const_spec((1, SEG))] + x_in_specs,
        out_specs=[pl.BlockSpec((LR, 5 * SEG), lambda i: (nc - 1 - i, 0)), pl.BlockSpec((LR, HD), lambda i: (nc - 1 - i, 0)),
                   _const_spec((8, HD)), _const_spec((8, SEG))] + x_out_specs,
        out_shape=[jax.ShapeDtypeStruct((s, 5 * SEG), bf16), jax.ShapeDtypeStruct((s, HD), f32),
                   jax.ShapeDtypeStruct((8, HD), f32), jax.ShapeDtypeStruct((8, SEG), f32)] + x_out_shapes,
        scratch_shapes=[pltpu.VMEM((SEG, HD), f32), pltpu.VMEM((8, HD), f32), pltpu.VMEM((8, HD), f32)] + x_sems,
        input_output_aliases=aliases, name="mlstm_bwd" if xf is None else "mlstm_bwd_carrying", compiler_params=_cp(48))(
            proj, proj, proj, proj, proj, proj, dy, cs, ns, ms, bias_row, norm_w, *x_ops)


GD_SEGS = (7, 8, 9)


def _carried(xf, n_in, n_out):
    operands, out_shapes, aliases, sems, _, n_xin, n_land = _carry_plumb(xf, n_in, n_out)
    return operands, [_ANY] * n_xin, out_shapes, [_ANY] * n_land, aliases, sems, n_xin, n_land


def _carry_open(xf, i, n_steps, x_refs, land_refs, sem_refs):
    if xf is None:
        return
    srcs = x_refs[:len(xf.srcs)]

    @pl.when(i == 0)
    def _():
        _carry_start(xf, srcs, land_refs, sem_refs)

    @pl.when(i == max(n_steps - 2, 0))
    def _():
        _carry_middle(xf, srcs, land_refs, sem_refs)


def _carry_close(xf, i, n_steps, x_refs, land_refs, sem_refs):
    if xf is None:
        return

    @pl.when(i == n_steps - 1)
    def _():
        _carry_finish(xf, x_refs[:len(xf.srcs)], land_refs, sem_refs)


def gdn_fwd(proj, conv_w, alog_row, dt_row, norm_w, xf=None):
    s = proj.shape[0]
    nc = s // LR
    n_in, n_out = 12, 2
    x_ops, x_in_specs, x_out_shapes, x_out_specs, aliases, x_sems, n_xin, n_land = _carried(xf, n_in, n_out)

    def body(*refs):
        qh_ref, q_ref, kh_ref, k_ref, vh_ref, v_ref, z_ref, sm_ref, cw_ref, al_ref, dt_ref, nw_ref = refs[:n_in]
        x_refs = refs[n_in:n_in + n_xin]
        y_ref, ss_ref = refs[n_in + n_xin:n_in + n_xin + n_out]
        land_refs = refs[n_in + n_xin + n_out:n_in + n_xin + n_out + n_land]
        st_s = refs[n_in + n_xin + n_out + n_land]
        sem_refs = refs[n_in + n_xin + n_out + n_land + 1:]
        i = pl.program_id(0)
        first = i == 0
        _carry_open(xf, i, nc, x_refs, land_refs, sem_refs)

        @pl.when(first)
        def _():
            st_s[...] = jnp.zeros_like(st_s)

        ss_ref[0] = st_s[...]
        halo = [jnp.where(first, 0.0, r[...]) for r in (qh_ref, kh_ref, vh_ref)]
        y, st2 = gd_chunk(halo[0], q_ref[...], halo[1], k_ref[...], halo[2], v_ref[...], z_ref[...], sm_ref[...],
                          cw_ref[...], al_ref[...], dt_ref[...], nw_ref[...], st_s[...])
        y_ref[...] = y.astype(bf16)
        st_s[...] = st2
        _carry_close(xf, i, nc, x_refs, land_refs, sem_refs)

    qkv_specs = []
    for sg in GD_SEGS:
        qkv_specs += [_halo_spec(LR, sg), _seg_spec(LR, sg)]
    return pl.pallas_call(
        body, grid=(nc,),
        in_specs=qkv_specs + [_seg_spec(LR, 10), pl.BlockSpec((LR, HD), lambda i: (i, SMALL_BLK)),
                              _const_spec((4, 3 * SEG)), _const_spec((1, HD)), _const_spec((1, HD)), _const_spec((1, HD))]
        + x_in_specs,
        out_specs=[pl.BlockSpec((LR, SEG), lambda i: (i, 0)), pl.BlockSpec((1, SEG, HD), lambda i: (i, 0, 0))] + x_out_specs,
        out_shape=[jax.ShapeDtypeStruct((s, SEG), bf16), jax.ShapeDtypeStruct((nc, SEG, HD), f32)] + x_out_shapes,
        scratch_shapes=[pltpu.VMEM((SEG, HD), f32)] + x_sems, input_output_aliases=aliases,
        name="gdn_fwd" if xf is None else "gdn_fwd_carrying")(
            proj, proj, proj, proj, proj, proj, proj, proj, conv_w, alog_row, dt_row, norm_w, *x_ops)


def gdn_bwd(proj, dy, ss, conv_w, alog_row, dt_row, norm_w, xf=None):
    s = proj.shape[0]
    nc = s // LR
    n_in, n_out = 14, 6
    x_ops, x_in_specs, x_out_shapes, x_out_specs, aliases, x_sems, n_xin, n_land = _carried(xf, n_in, n_out)

    def body(*refs):
        (qh_ref, q_ref, kh_ref, k_ref, vh_ref, v_ref, z_ref, sm_ref, dy_ref, ss_ref,
         cw_ref, al_ref, dt_ref, nw_ref) = refs[:n_in]
        x_refs = refs[n_in:n_in + n_xin]
        dp_ref, dsm_ref, dcw_ref, dal_ref, ddt_ref, dnw_ref = refs[n_in + n_xin:n_in + n_xin + n_out]
        land_refs = refs[n_in + n_xin + n_out:n_in + n_xin + n_out + n_land]
        dst_s, dhalo = refs[n_in + n_xin + n_out + n_land:n_in + n_xin + n_out + n_land + 2]
        sem_refs = refs[n_in + n_xin + n_out + n_land + 2:]
        i = pl.program_id(0)
        first_chunk = i == nc - 1
        _carry_open(xf, i, nc, x_refs, land_refs, sem_refs)

        @pl.when(i == 0)
        def _():
            for ref in (dcw_ref, dal_ref, ddt_ref, dnw_ref, dst_s, dhalo):
                ref[...] = jnp.zeros_like(ref)

        halo = [jnp.where(first_chunk, 0.0, r[...]) for r in (qh_ref, kh_ref, vh_ref)]
        _, vjp = jax.vjp(gd_chunk, halo[0], q_ref[...], halo[1], k_ref[...], halo[2], v_ref[...], z_ref[...], sm_ref[...],
                         cw_ref[...], al_ref[...], dt_ref[...], nw_ref[...], ss_ref[0])
        dqh, dq, dkh, dk, dvh, dv, dz, dsm, dcw, dal, ddt, dnw, dst = vjp((dy_ref[...], dst_s[...]))
        for j, val in enumerate((dq, dk, dv)):
            val = val + jnp.concatenate([jnp.zeros((LR - 8, SEG), f32), dhalo[:, j * SEG:(j + 1) * SEG]], axis=0)
            dp_ref[:, j * SEG:(j + 1) * SEG] = val.astype(bf16)
        dp_ref[:, 3 * SEG:4 * SEG] = dz.astype(bf16)
        for j, val in enumerate((dqh, dkh, dvh)):
            dhalo[:, j * SEG:(j + 1) * SEG] = val
        dsm_ref[...] = dsm
        dcw_ref[...] += dcw
        dal_ref[0:1, :] += dal
        ddt_ref[0:1, :] += ddt
        dnw_ref[0:1, :] += dnw
        dst_s[...] = dst
        _carry_close(xf, i, nc, x_refs, land_refs, sem_refs)

    qkv_specs = []
    for sg in GD_SEGS:
        qkv_specs += [_halo_spec(LR, sg, nc), _seg_spec(LR, sg, nc)]
    return pl.pallas_call(
        body, grid=(nc,),
        in_specs=qkv_specs + [_seg_spec(LR, 10, nc), pl.BlockSpec((LR, HD), lambda i: (nc - 1 - i, SMALL_BLK)),
                              pl.BlockSpec((LR, SEG), lambda i: (nc - 1 - i, 0)),
                              pl.BlockSpec((1, SEG, HD), lambda i: (nc - 1 - i, 0, 0)),
                              _const_spec((4, 3 * SEG)), _const_spec((1, HD)), _const_spec((1, HD)), _const_spec((1, HD))]
        + x_in_specs,
        out_specs=[pl.BlockSpec((LR, 4 * SEG), lambda i: (nc - 1 - i, 0)), pl.BlockSpec((LR, HD), lambda i: (nc - 1 - i, 0)),
                   _const_spec((4, 3 * SEG)), _const_spec((8, HD)), _const_spec((8, HD)), _const_spec((8, HD))] + x_out_specs,
        out_shape=[jax.ShapeDtypeStruct((s, 4 * SEG), bf16), jax.ShapeDtypeStruct((s, HD), f32),
                   jax.ShapeDtypeStruct((4, 3 * SEG), f32), jax.ShapeDtypeStruct((8, HD), f32),
                   jax.ShapeDtypeStruct((8, HD), f32), jax.ShapeDtypeStruct((8, HD), f32)] + x_out_shapes,
        scratch_shapes=[pltpu.VMEM((SEG, HD), f32), pltpu.VMEM((8, 3 * SEG), f32)] + x_sems, input_output_aliases=aliases,
        name="gdn_bwd" if xf is None else "gdn_bwd_carrying", compiler_params=_cp(48))(
            proj, proj, proj, proj, proj, proj, proj, proj, dy, ss, conv_w, alog_row, dt_row, norm_w, *x_ops)


def _my_place():
    return lax.axis_index("x"), lax.axis_index("y"), lax.axis_index("c")


def _slot(p):
    return 4 * p[0] + 2 * p[1] + p[2]


def _peer(me, j):
    flips = ((j >> 2) & 1, (j >> 1) & 1, j & 1)
    return tuple((1 - v) if fl else v for v, fl in zip(me, flips))


_ANY = pl.BlockSpec(memory_space=pl.ANY)


class Scatter(NamedTuple):
    srcs: list
    lands: list
    land_of: list
    layer: list
    span: list


class Gather2(NamedTuple):
    srcs: list
    lands: list
    span: list


def _carry_plumb(cx, n_in, n_out):
    if cx is None:
        return [], [], {}, [], 0, 0, 0
    n_src = len(cx.srcs)
    passed = [li for li, ld in enumerate(cx.lands) if not isinstance(ld, jax.ShapeDtypeStruct)]
    operands = list(cx.srcs) + [cx.lands[li] for li in passed]
    aliases = {n_in + n_src + k: n_out + li for k, li in enumerate(passed)}
    out_shapes = [jax.ShapeDtypeStruct(ld.shape, ld.dtype) for ld in cx.lands]
    sems = [pltpu.SemaphoreType.DMA((n_src, N_DEV - 1)), pltpu.SemaphoreType.DMA((n_src, N_DEV - 1)),
            pltpu.SemaphoreType.DMA((n_src,))]
    return operands, out_shapes, aliases, sems, n_src, len(operands), len(cx.lands)


def _scatter_copies(sc, src_refs, land_refs, send_sems, recv_sems, local_sems):
    me = _my_place()
    mine = _slot(me)
    local, remote = [], []
    for a, src_ref in enumerate(src_refs):
        lo, hi = sc.span[a]
        rows = pl.ds(lo, hi - lo)
        dst = land_refs[sc.land_of[a]].at[mine, sc.layer[a], rows]
        local.append(pltpu.make_async_copy(src_ref.at[mine, rows], dst, local_sems.at[a]))
        for j in range(1, N_DEV):
            peer = _peer(me, j)
            remote.append(pltpu.make_async_remote_copy(
                src_ref=src_ref.at[_slot(peer), rows], dst_ref=dst, send_sem=send_sems.at[a, j - 1],
                recv_sem=recv_sems.at[a, j - 1], device_id=peer, device_id_type=MESH))
    return local, remote


def _gather2_copy(land, sems, a, k, block_of, to, rows, src=None):
    dst = land.at[_slot(block_of), rows]
    return pltpu.make_async_remote_copy(src_ref=dst if src is None else src, dst_ref=dst, send_sem=sems[0].at[a, k],
                                        recv_sem=sems[1].at[a, k], device_id=to, device_id_type=MESH)


def _gather2_places():
    x, y, c = _my_place()
    return (x, y, c), (x, y, 1 - c), [(1 - x, y), (x, 1 - y), (1 - x, 1 - y)], c


def _carry_start(cx, src_refs, land_refs, sems):
    if isinstance(cx, Scatter):
        local, remote = _scatter_copies(cx, src_refs, land_refs, *sems)
        for cp in local + remote:
            cp.start()
        return
    me, sib, chips, c = _gather2_places()
    for a, (src_ref, land) in enumerate(zip(src_refs, land_refs)):
        rows = pl.ds(cx.span[a][0], cx.span[a][1] - cx.span[a][0])
        src = src_ref.at[rows]
        pltpu.make_async_copy(src, land.at[_slot(me), rows], sems[2].at[a]).start()
        _gather2_copy(land, sems, a, 0, me, sib, rows, src=src).start()
        for j, chip in enumerate(chips):
            _gather2_copy(land, sems, a, 1 + j, me, (*chip, c), rows, src=src).start()


def _carry_middle(cx, src_refs, land_refs, sems):
    if isinstance(cx, Scatter):
        return
    me, sib, chips, c = _gather2_places()
    for a, land in enumerate(land_refs):
        rows = pl.ds(cx.span[a][0], cx.span[a][1] - cx.span[a][0])
        for j, chip in enumerate(chips):
            _gather2_copy(land, sems, a, 1 + j, (*chip, c), me, rows).wait_recv()
            _gather2_copy(land, sems, a, 4 + j, (*chip, c), sib, rows).start()


def _carry_finish(cx, src_refs, land_refs, sems):
    if isinstance(cx, Scatter):
        local, remote = _scatter_copies(cx, src_refs, land_refs, *sems)
        for cp in remote:
            cp.wait()
        for cp in local:
            cp.wait()
        return
    me, sib, chips, c = _gather2_places()
    for a, (src_ref, land) in enumerate(zip(src_refs, land_refs)):
        rows = pl.ds(cx.span[a][0], cx.span[a][1] - cx.span[a][0])
        src = src_ref.at[rows]
        _gather2_copy(land, sems, a, 0, sib, me, rows).wait_recv()
        for j, chip in enumerate(chips):
            _gather2_copy(land, sems, a, 4 + j, (*chip, 1 - c), me, rows).wait_recv()
        _gather2_copy(land, sems, a, 0, me, sib, rows, src=src).wait_send()
        for j, chip in enumerate(chips):
            _gather2_copy(land, sems, a, 1 + j, me, (*chip, c), rows, src=src).wait_send()
            _gather2_copy(land, sems, a, 4 + j, (*chip, c), sib, rows).wait_send()
        pltpu.make_async_copy(src, land.at[_slot(me), rows], sems[2].at[a]).wait()


def exchange(cx, name):
    operands, out_shapes, aliases, sems, n_src, n_xin, n_land = _carry_plumb(cx, 0, 0)

    def body(*refs):
        src_refs, land_refs, sem_refs = refs[:n_src], refs[n_xin:n_xin + n_land], refs[n_xin + n_land:]
        _carry_start(cx, src_refs, land_refs, sem_refs)
        _carry_middle(cx, src_refs, land_refs, sem_refs)
        _carry_finish(cx, src_refs, land_refs, sem_refs)

    return pl.pallas_call(body, in_specs=[_ANY] * n_xin, out_specs=[_ANY] * n_land, out_shape=out_shapes,
                          scratch_shapes=sems, input_output_aliases=aliases, name=name)(*operands)


def small_allreduce(packed, name, xf=None):
    r = packed.shape[0]
    rc = r // N_DEV
    x_ops, x_in_specs, x_out_shapes, x_out_specs, aliases, x_sems, n_xin, n_land = _carried(xf, 1, 1)

    def body(*refs):
        (in_ref,), x_refs, (out_ref,), land_refs, scratch, sem_refs = _cut_refs(refs, 1, n_xin, 1, n_land, 5)
        recv_buf, send1, recv1, send2, recv2 = scratch
        if xf is not None:
            _carry_start(xf, x_refs[:len(xf.srcs)], land_refs, sem_refs)
        me = _my_place()
        mine = _slot(me)
        my_rows = pl.ds(pl.multiple_of(mine * rc, 8), rc)
        first, second = [], []
        for j in range(1, N_DEV):
            peer = _peer(me, j)
            peer_rows = pl.ds(pl.multiple_of(_slot(peer) * rc, 8), rc)
            first.append(pltpu.make_async_remote_copy(
                src_ref=in_ref.at[peer_rows], dst_ref=recv_buf.at[mine], send_sem=send1.at[j - 1], recv_sem=recv1.at[j - 1],
                device_id=peer, device_id_type=MESH))
            second.append(pltpu.make_async_remote_copy(
                src_ref=out_ref.at[my_rows], dst_ref=out_ref.at[my_rows], send_sem=send2.at[j - 1], recv_sem=recv2.at[j - 1],
                device_id=peer, device_id_type=MESH))
        for cp in first:
            cp.start()
        recv_buf[mine] = in_ref[my_rows]
        for cp in first:
            cp.wait()
        acc = recv_buf[0]
        for k in range(1, N_DEV):
            acc = acc + recv_buf[k]
        out_ref[my_rows] = acc
        for cp in second:
            cp.start()
        for cp in second:
            cp.wait()
        if xf is not None:
            _carry_finish(xf, x_refs[:len(xf.srcs)], land_refs, sem_refs)

    vmem = pl.BlockSpec(memory_space=pltpu.VMEM)
    return pl.pallas_call(
        body, in_specs=[vmem] + x_in_specs, out_specs=[vmem] + x_out_specs,
        out_shape=[jax.ShapeDtypeStruct((r, 128), f32)] + x_out_shapes,
        scratch_shapes=[pltpu.VMEM((N_DEV, rc, 128), f32)] + [pltpu.SemaphoreType.DMA((N_DEV - 1,))] * 4 + x_sems,
        input_output_aliases=aliases, name=name, compiler_params=_cp(32))(packed, *x_ops)


def _adamw(w, g, m, v):
    m = ADAM_B1 * m + (1.0 - ADAM_B1) * g
    v = ADAM_B2 * v + (1.0 - ADAM_B2) * (g * g)
    m_hat = m / (1.0 - ADAM_B1 ** ADAM_STEP)
    v_hat = v / (1.0 - ADAM_B2 ** ADAM_STEP)
    delta = -ADAM_LR * (m_hat / (jnp.sqrt(v_hat) + ADAM_EPS) + ADAM_WD * w)
    return delta, m, v


def adam_slots(slots, w, m, v, rows, name):
    _, depth, r, c = slots.shape

    def body(s_ref, w_ref, m_ref, v_ref, g_ref, d_ref, m2_ref, v2_ref):
        g = s_ref[0, 0].astype(f32)
        for k in range(1, N_DEV):
            g = g + s_ref[k, 0].astype(f32)
        d, m2, v2 = _adamw(w_ref[0], g, m_ref[0], v_ref[0])
        g_ref[0] = g
        d_ref[0] = d
        m2_ref[0] = m2
        v2_ref[0] = v2

    blk = pl.BlockSpec((1, rows, c), lambda l, i: (l, i, 0))
    sh = jax.ShapeDtypeStruct((depth, r, c), f32)
    return pl.pallas_call(
        body, grid=(depth, r // rows),
        in_specs=[pl.BlockSpec((N_DEV, 1, rows, c), lambda l, i: (0, l, i, 0)), blk, blk, blk],
        out_specs=[blk] * 4, out_shape=[sh] * 4, name=name, compiler_params=_cp(40))(slots, w, m, v)


def adam_flat(g, w, m, v, name):
    def body(g_ref, w_ref, m_ref, v_ref, d_ref, m2_ref, v2_ref):
        d, m2, v2 = _adamw(w_ref[...], g_ref[...], m_ref[...], v_ref[...])
        d_ref[...] = d
        m2_ref[...] = m2
        v2_ref[...] = v2

    sh = jax.ShapeDtypeStruct(g.shape, f32)
    return pl.pallas_call(body, out_shape=[sh] * 3, name=name)(g, w, m, v)


def _rows_of(shape):
    n = 1
    for dim in shape:
        n *= dim
    return n, -(-n // (8 * 128)) * 8


def _pack(arrs):
    parts = []
    for a in arrs:
        n, rows = _rows_of(a.shape)
        parts.append(jnp.pad(a.reshape(-1).astype(f32), (0, rows * 128 - n)).reshape(rows, 128))
    return jnp.concatenate(parts, axis=0)


def _unpack(packed, shapes):
    out, row = [], 0
    for sh in shapes:
        n, rows = _rows_of(sh)
        out.append(packed[row:row + rows].reshape(-1)[:n].reshape(sh))
        row += rows
    return out


def _block_diag(gw):
    eye = jnp.eye(8, dtype=gw.dtype)
    return (gw[:, :, None, :] * eye[:, None, :, None]).reshape(SEG, SEG)


def _diag_blocks(dense):
    eye = jnp.eye(8, dtype=dense.dtype)
    return (dense.reshape(8, 64, 8, 64) * eye[:, None, :, None]).sum(axis=2)


def _lane_row(vals, first_lane):
    return jnp.zeros((1, HD), f32).at[0, first_lane:first_lane + NH].set(vals)


def kernel(x, norm_w, w_in, rg_conv_w, rg_conv_b, rg_gate_w, rg_gate_b, rg_lambda, ml_gate_b, ml_norm_w, gd_conv_w, gd_a_log, gd_dt_bias, gd_norm_w, w_out, final_norm_w, loss_target, m_norm_w, m_w_in, m_rg_conv_w, m_rg_conv_b, m_rg_gate_w, m_rg_gate_b, m_rg_lambda, m_ml_gate_b, m_ml_norm_w, m_gd_conv_w, m_gd_a_log, m_gd_dt_bias, m_gd_norm_w, m_w_out, m_final_norm_w, v_norm_w, v_w_in, v_rg_conv_w, v_rg_conv_b, v_rg_gate_w, v_rg_gate_b, v_rg_lambda, v_ml_gate_b, v_ml_norm_w, v_gd_conv_w, v_gd_a_log, v_gd_dt_bias, v_gd_norm_w, v_w_out, v_final_norm_w):
    s = x.shape[1]
    xs = x.reshape(s, D)
    tgt = loss_target.reshape(s, D)
    me = 4 * lax.axis_index("x") + 2 * lax.axis_index("y") + lax.axis_index("c")

    comm = MeshComm(w_in, w_out, [rg_conv_w, rg_gate_b, gd_conv_w])
    rg_conv_full, rg_gb_full, gd_conv_full = comm.small_weights
    loss_part, dx, d_fw, g_small = local_step(
        xs, tgt, comm, rg_conv_full, rg_gb_full, gd_conv_full, norm_w, rg_conv_b, rg_gate_w, rg_lambda,
        ml_gate_b, ml_norm_w, gd_a_log, gd_dt_bias, gd_norm_w, final_norm_w)
    given_w = dict(norm_w=norm_w, rg_conv_w=rg_conv_w, rg_conv_b=rg_conv_b, rg_gate_w=rg_gate_w, rg_gate_b=rg_gate_b,
                   rg_lambda=rg_lambda, ml_gate_b=ml_gate_b, ml_norm_w=ml_norm_w, gd_conv_w=gd_conv_w, gd_a_log=gd_a_log,
                   gd_dt_bias=gd_dt_bias, gd_norm_w=gd_norm_w, final_norm_w=final_norm_w, w_in=w_in, w_out=w_out)
    given_m = dict(norm_w=m_norm_w, rg_conv_w=m_rg_conv_w, rg_conv_b=m_rg_conv_b, rg_gate_w=m_rg_gate_w, rg_gate_b=m_rg_gate_b,
                   rg_lambda=m_rg_lambda, ml_gate_b=m_ml_gate_b, ml_norm_w=m_ml_norm_w, gd_conv_w=m_gd_conv_w,
                   gd_a_log=m_gd_a_log, gd_dt_bias=m_gd_dt_bias, gd_norm_w=m_gd_norm_w, final_norm_w=m_final_norm_w,
                   w_in=m_w_in, w_out=m_w_out)
    given_v = dict(norm_w=v_norm_w, rg_conv_w=v_rg_conv_w, rg_conv_b=v_rg_conv_b, rg_gate_w=v_rg_gate_w, rg_gate_b=v_rg_gate_b,
                   rg_lambda=v_rg_lambda, ml_gate_b=v_ml_gate_b, ml_norm_w=v_ml_norm_w, gd_conv_w=v_gd_conv_w,
                   gd_a_log=v_gd_a_log, gd_dt_bias=v_gd_dt_bias, gd_norm_w=v_gd_norm_w, final_norm_w=v_final_norm_w,
                   w_in=v_w_in, w_out=v_w_out)
    return finish_step(loss_part, dx, d_fw, g_small, comm, s, me, given_w, given_m, given_v)


def _gathered_pieces():
    per = D_IN // N_DEV
    pieces = []
    for lo, hi in ((0, 3584), (3592, 5640), (3584, 3592), (5640, 5648)):
        col = lo
        while col < hi:
            k = col // per
            end = min(hi, (k + 1) * per)
            pieces.append((k, col - k * per, end - k * per))
            col = end
    return pieces


class MeshComm:
    GWI_SPLIT = 256
    WI_SPLIT = 384

    def __init__(self, w_in, w_out, small_shards):
        per = D_IN // N_DEV
        self.wi_sh = [w_in[l].astype(bf16) for l in range(DEPTH)]
        self.wo_sh = [w_out[l].astype(bf16) for l in range(DEPTH)]
        self.wi_land = jax.ShapeDtypeStruct((N_DEV, D, per), bf16)
        self.wo_land = jax.ShapeDtypeStruct((N_DEV, 3 * SEG // N_DEV, D), bf16)
        packed = _pack(small_shards)
        first = Gather2([self.wi_sh[0], self.wo_sh[0], packed],
                        [self.wi_land, self.wo_land, jax.ShapeDtypeStruct((N_DEV,) + packed.shape, f32)],
                        [(0, D), (0, 3 * SEG // N_DEV), (0, packed.shape[0])])
        wi_g, wo_g, sm_g = exchange(first, "gather_first")
        self.wi_g, self.wo_g = {0: wi_g}, {0: wo_g}
        shapes = [a.shape for a in small_shards]
        parts = [_unpack(sm_g[k], shapes) for k in range(N_DEV)]
        self.small_weights = [jnp.concatenate([p[j] for p in parts], axis=-1) for j in range(len(small_shards))]
        self.gwi_land = lax.empty((N_DEV, DEPTH, D, per), bf16)
        self.gwo_land = lax.empty((N_DEV, DEPTH, 3 * SEG // N_DEV, D), bf16)
        self.gwi_slots = {}
        self.gwo_slots = {}

    def weights(self, l):
        cols = [self.wi_g[l][k, :, a:b] for k, a, b in _gathered_pieces()] + [jnp.zeros((D, DP - D_IN), bf16)]
        return jnp.concatenate(cols, axis=1), self.wo_g[l].reshape(3 * SEG, D)

    def fwd_carry(self, l, host):
        if l + 1 >= DEPTH:
            return None
        if host == "mlstm":
            return Gather2([self.wo_sh[l + 1]], [self.wo_land], [(0, 3 * SEG // N_DEV)])
        if host == "inproj":
            return Gather2([self.wi_sh[l + 1]], [self.wi_land], [(0, self.WI_SPLIT)])
        return Gather2([self.wi_sh[l + 1]], [self.wi_g[l + 1]], [(self.WI_SPLIT, D)])

    def fwd_landed(self, l, host, landed):
        (self.wo_g if host == "mlstm" else self.wi_g)[l + 1] = landed[0]

    def own_w_out_grad(self, l, g_wo):
        self.gwo_slots[l] = g_wo.reshape(N_DEV, 3 * SEG // N_DEV, D).astype(bf16)

    def bwd_carry(self, l, host):
        rows_o = 3 * SEG // N_DEV
        srcs, land_of, layer, span = [], [], [], []
        if l + 1 < DEPTH:
            if host == "mlstm":
                srcs += [self.gwo_slots[l + 1], self.gwi_slots[l + 1]]
                land_of, layer, span = [1, 0], [l + 1, l + 1], [(0, rows_o), (0, self.GWI_SPLIT)]
            else:
                srcs, land_of, layer, span = [self.gwi_slots[l + 1]], [0], [l + 1], [(self.GWI_SPLIT, D)]
        if l == 0 and host == "mlstm":
            srcs, land_of, layer, span = srcs + [self.gwo_slots[0]], land_of + [1], layer + [0], span + [(0, rows_o)]
        if not srcs:
            return None
        return Scatter(srcs, [self.gwi_land, self.gwo_land], land_of, layer, span)

    def bwd_landed(self, landed):
        self.gwi_land, self.gwo_land = landed

    def grads_ready(self, l, pieces):
        per = D_IN // N_DEV
        g_wi = jnp.concatenate(pieces, axis=1)
        self.gwi_slots[l] = jnp.stack([g_wi[:, k * per:(k + 1) * per] for k in range(N_DEV)]).astype(bf16)

    def last_carry(self):
        return Scatter([self.gwi_slots[0]], [self.gwi_land, self.gwo_land], [0], [0], [(0, D)])


def local_step(xs, tgt, comm, rg_conv_full, rg_gb_full, gd_conv_full, norm_w, rg_conv_b, rg_gate_w,
               rg_lambda, ml_gate_b, ml_norm_w, gd_a_log, gd_dt_bias, gd_norm_w, final_norm_w):
    acts = []
    for l in range(DEPTH):
        nw = norm_w[l].reshape(1, D)
        w_in_l, w_out_l = comm.weights(l)
        xf = comm.fwd_carry(l, "inproj")
        proj, hn_t, *landed = inproj(xs, nw, w_in_l, xf=xf)
        if xf is not None:
            comm.fwd_landed(l, "inproj", landed)
        rg_p = (rg_conv_full[l], rg_conv_b[l].reshape(1, SEG), _block_diag(rg_gate_w[l, 0]), _block_diag(rg_gate_w[l, 1]),
                rg_gb_full[l], rg_lambda[l].reshape(1, SEG))
        y_rg, hs = rglru_fwd(proj, *rg_p)
        ml_p = (jnp.zeros((1, HD), f32).at[0, 0:2 * NH].set(ml_gate_b[l].reshape(-1)), ml_norm_w[l].reshape(1, SEG))
        xf = comm.fwd_carry(l, "mlstm")
        y_ml, cs, ns, ms, *landed = mlstm_fwd(proj, *ml_p, xf=xf)
        if xf is not None:
            comm.fwd_landed(l, "mlstm", landed)
        gd_p = (gd_conv_full[l], _lane_row(gd_a_log[l], 2 * NH), _lane_row(gd_dt_bias[l], 2 * NH), gd_norm_w[l].reshape(1, HD))
        xf = comm.fwd_carry(l, "gdn")
        y_gd, ss, *landed = gdn_fwd(proj, *gd_p, xf=xf)
        if xf is not None:
            comm.fwd_landed(l, "gdn", landed)
        acts.append((xs, nw, proj, hn_t, w_in_l, w_out_l, rg_p, y_rg, hs, ml_p, y_ml, cs, ns, ms, gd_p, y_gd, ss))
        xs = outproj(xs, y_rg, y_ml, y_gd, w_out_l)

    dx, loss_part, d_fw = head(xs, final_norm_w.reshape(1, D), tgt)

    g_small = {k: [None] * DEPTH for k in ("norm_w", "rg_conv_w", "rg_conv_b", "rg_gate_w", "rg_gate_b", "rg_lambda",
                                           "ml_gate_b", "ml_norm_w", "gd_conv_w", "gd_a_log", "gd_dt_bias", "gd_norm_w")}
    for l in reversed(range(DEPTH)):
        x_l, nw, proj, hn_t, w_in_l, w_out_l, rg_p, y_rg, hs, ml_p, y_ml, cs, ns, ms, gd_p, y_gd, ss = acts[l]
        dy_rg, dy_ml, dy_gd, g_wo = outproj_bwd(dx, y_rg, y_ml, y_gd, w_out_l)
        d_rg, d_cw, d_cb, d_gr, d_gi, d_gb, d_lam = rglru_bwd(proj, hs, dy_rg, *rg_p)
        comm.own_w_out_grad(l, g_wo)
        xf = comm.bwd_carry(l, "mlstm")
        d_ml, d_sm_ml, d_bias, d_mnw, *landed = mlstm_bwd(proj, dy_ml, cs, ns, ms, *ml_p, xf=xf)
        if xf is not None:
            comm.bwd_landed(landed)
        xf = comm.bwd_carry(l, "gdn")
        d_gd, d_sm_gd, d_gcw, d_al, d_dt, d_gnw, *landed = gdn_bwd(proj, dy_gd, ss, *gd_p, xf=xf)
        if xf is not None:
            comm.bwd_landed(landed)
        dx, d_nw = inproj_bwd_x(x_l, nw, w_in_l, dx, d_rg, d_ml, d_gd, d_sm_ml, d_sm_gd)
        gw_rg = wgrad(hn_t, [d_rg], "wgrad_rg")
        gw_ml = wgrad(hn_t, [d_ml], "wgrad_ml")
        gw_gd = wgrad(hn_t, [d_gd], "wgrad_gd")
        gw_sm = wgrad(hn_t, [d_sm_ml, d_sm_gd], "wgrad_small")
        comm.grads_ready(l, [gw_rg, gw_ml, gw_sm[:, 0:2 * NH], gw_gd, gw_sm[:, 2 * NH:4 * NH]])
        g_small["norm_w"][l] = d_nw[0]
        g_small["rg_conv_w"][l] = d_cw
        g_small["rg_conv_b"][l] = d_cb[0]
        g_small["rg_gate_w"][l] = jnp.stack([_diag_blocks(d_gr), _diag_blocks(d_gi)])
        g_small["rg_gate_b"][l] = d_gb
        g_small["rg_lambda"][l] = d_lam[0]
        g_small["ml_gate_b"][l] = d_bias[0, 0:2 * NH].reshape(2, NH)
        g_small["ml_norm_w"][l] = d_mnw[0]
        g_small["gd_conv_w"][l] = d_gcw
        g_small["gd_a_log"][l] = d_al[0, 2 * NH:3 * NH]
        g_small["gd_dt_bias"][l] = d_dt[0, 2 * NH:3 * NH]
        g_small["gd_norm_w"][l] = d_gnw[0]
    return loss_part, dx, d_fw, g_small


def finish_step(loss_part, dx, d_fw, g_small, comm, s, me, given_w, given_m, given_v):
    small_names = ["norm_w", "rg_conv_w", "rg_conv_b", "rg_gate_w", "rg_gate_b", "rg_lambda", "ml_gate_b", "ml_norm_w",
                   "gd_conv_w", "gd_a_log", "gd_dt_bias", "gd_norm_w"]
    small_list = [loss_part[0, 0:1], d_fw[0]] + [jnp.stack(g_small[k]) for k in small_names]
    small_shapes = [a.shape for a in small_list]
    packed = _pack(small_list)
    packed = jnp.pad(packed, ((0, -packed.shape[0] % (8 * N_DEV)), (0, 0)))
    summed, gwi_r, gwo_r = small_allreduce(packed, "last_exchange", xf=comm.last_carry())
    g_all = _unpack(summed, small_shapes)

    g_wi, d_wi, m_wi, v_wi = adam_slots(gwi_r, given_w["w_in"], given_m["w_in"], given_v["w_in"], 256, "adam_w_in")
    g_wo, d_wo, m_wo, v_wo = adam_slots(gwo_r, given_w["w_out"], given_m["w_out"], given_v["w_out"], 192, "adam_w_out")
    loss = g_all[0][0]
    grads = {"final_norm_w": g_all[1]}
    for k, g in zip(small_names, g_all[2:]):
        grads[k] = g
    for k, width in (("rg_conv_w", 64), ("rg_gate_b", 64), ("gd_conv_w", 192)):
        grads[k] = lax.dynamic_slice_in_dim(grads[k], me * width, width, axis=2)
    names = small_names + ["final_norm_w"]
    shapes = [given_w[k].shape for k in names]
    d_p, m_p, v_p = adam_flat(_pack([grads[k] for k in names]), _pack([given_w[k] for k in names]),
                              _pack([given_m[k] for k in names]), _pack([given_v[k] for k in names]), "adam_small")
    deltas = dict(zip(names, _unpack(d_p, shapes)))
    new_m = dict(zip(names, _unpack(m_p, shapes)))
    new_v = dict(zip(names, _unpack(v_p, shapes)))
    grads["w_in"], deltas["w_in"], new_m["w_in"], new_v["w_in"] = g_wi, d_wi, m_wi, v_wi
    grads["w_out"], deltas["w_out"], new_m["w_out"], new_v["w_out"] = g_wo, d_wo, m_wo, v_wo

    order = ["norm_w", "w_in", "rg_conv_w", "rg_conv_b", "rg_gate_w", "rg_gate_b", "rg_lambda", "ml_gate_b", "ml_norm_w",
             "gd_conv_w", "gd_a_log", "gd_dt_bias", "gd_norm_w", "w_out", "final_norm_w"]
    return (loss, dx.reshape(1, s, D), *[grads[k] for k in order], *[deltas[k] for k in order],
            *[new_m[k] for k in order], *[new_v[k] for k in order])
```

```python
import functools
from typing import NamedTuple

import jax
import jax.numpy as jnp
from jax import lax
from jax.experimental import pallas as pl
from jax.experimental.pallas import tpu as pltpu

f32 = jnp.float32
bf16 = jnp.bfloat16
MESH = pl.DeviceIdType.MESH

N_DEV = 8
D = 1024
DEPTH = 4
EPS = 1e-6
SEG = 512
HD = 128
NH = 4
LC = 64
RUN = 4
LR = RUN * LC
D_IN = 5648
DP = 5760
SMALL0 = 5632
RG_TILE = 256
RG_C = 8.0

ADAM_LR = 0.001
ADAM_B1 = 0.9
ADAM_B2 = 0.999
ADAM_EPS = 1e-08
ADAM_WD = 0.01
ADAM_STEP = 10


def _cp(vmem_mb):
    return pltpu.CompilerParams(vmem_limit_bytes=vmem_mb * 2 ** 20)


def _dot(a, b, ca, cb):
    return lax.dot_general(a.astype(bf16), b.astype(bf16), (((ca,), (cb,)), ((), ())), preferred_element_type=f32)


@jax.custom_vjp
def mm_nn(a, b):
    return _dot(a, b, 1, 0)


@jax.custom_vjp
def mm_nt(a, b):
    return _dot(a, b, 1, 1)


@jax.custom_vjp
def mm_tn(a, b):
    return _dot(a, b, 0, 0)


mm_nn.defvjp(lambda a, b: (mm_nn(a, b), (a, b)), lambda r, g: (mm_nt(g, r[1]), mm_tn(r[0], g)))
mm_nt.defvjp(lambda a, b: (mm_nt(a, b), (a, b)), lambda r, g: (mm_nn(g, r[1]), mm_tn(g, r[0])))
mm_tn.defvjp(lambda a, b: (mm_tn(a, b), (a, b)), lambda r, g: (mm_nt(r[1], g), mm_nn(r[0], g)))


def _split(x):
    hi = x.astype(bf16)
    return hi, (x - hi.astype(f32)).astype(bf16)


def dot3(a, b):
    ah, al = _split(a)
    bh, bl = _split(b)
    d = functools.partial(jnp.dot, preferred_element_type=f32)
    return d(ah, bh) + (d(al, bh) + d(ah, bl))


def _tri_sum(x, reverse):
    n = x.shape[0]
    r = lax.broadcasted_iota(jnp.int32, (n, 3 * n), 0)
    c = lax.broadcasted_iota(jnp.int32, (n, 3 * n), 1) & (n - 1)
    ones = ((c >= r) if reverse else (c <= r)).astype(bf16)
    hi = x.astype(bf16)
    rest = x - hi.astype(f32)
    mid = rest.astype(bf16)
    lo = (rest - mid.astype(f32)).astype(bf16)
    return jnp.dot(ones, jnp.concatenate([hi, mid, lo], axis=0), preferred_element_type=f32)


@jax.custom_vjp
def cumsum_rows(x):
    return _tri_sum(x, False)


@jax.custom_vjp
def rev_cumsum_rows(x):
    return _tri_sum(x, True)


cumsum_rows.defvjp(lambda x: (cumsum_rows(x), None), lambda _, g: (rev_cumsum_rows(g),))
rev_cumsum_rows.defvjp(lambda x: (rev_cumsum_rows(x), None), lambda _, g: (cumsum_rows(g),))


def _tri(n, strict=False):
    r = lax.broadcasted_iota(jnp.int32, (n, n), 0)
    c = lax.broadcasted_iota(jnp.int32, (n, n), 1)
    return (r > c) if strict else (r >= c)


def _lane_col(v, j):
    lane = lax.broadcasted_iota(jnp.int32, v.shape, 1)
    return jnp.sum(jnp.where(lane == j, v, 0.0), axis=1, keepdims=True)


def _rows_from(pieces, rows, width):
    ridx = lax.broadcasted_iota(jnp.int32, (rows, width), 0)
    out = jnp.zeros((rows, width), f32)
    for h, p in enumerate(pieces):
        out = out + jnp.where(ridx == h, jnp.broadcast_to(p, (rows, width)), 0.0)
    return out


def causal_conv(halo8, x, w4):
    t = x.shape[0]
    xe = jnp.concatenate([halo8, x], axis=0)
    y = xe[5:5 + t] * w4[0:1]
    for k in range(1, 4):
        y = y + xe[5 + k:5 + k + t] * w4[k:k + 1]
    return y


def ml_chunk(q, k, v, o_pre, z, small, bias_row, norm_w, C, n, m):
    n_ch = q.shape[0] // LC
    lane = lax.broadcasted_iota(jnp.int32, small.shape, 1)
    pre = small + bias_row
    lg = jnp.where(lane < NH, pre, jnp.where(lane < 2 * NH, jax.nn.log_sigmoid(pre), 0.0))
    rows = [slice(c * LC, (c + 1) * LC) for c in range(n_ch)]
    lgs = [lg[r] for r in rows]
    bcs = [cumsum_rows(x) for x in lgs]
    lg_ts = [x.T for x in lgs]
    bc_ts = [x.T for x in bcs]
    causal = _tri(LC)
    pairs = [(c, h) for c in range(n_ch) for h in range(NH)]
    idx = range(len(pairs))
    sls = [slice(h * HD, (h + 1) * HD) for h in range(NH)]
    qs = [q[rows[c], sls[h]] * (HD ** -0.5) for c, h in pairs]
    ks = [k[rows[c], sls[h]] for c, h in pairs]
    vs = [v[rows[c], sls[h]] for c, h in pairs]
    li_cols = [_lane_col(lgs[c], h) for c, h in pairs]
    b_cols = [_lane_col(bcs[c], NH + h) for c, h in pairs]
    dms = [jnp.where(causal, b_cols[i] - bc_ts[c][NH + h:NH + h + 1, :] + lg_ts[c][h:h + 1, :], -jnp.inf)
           for i, (c, h) in enumerate(pairs)]
    dm_maxs = [jnp.max(dms[i], axis=-1, keepdims=True) for i in idx]
    gs = [b_cols[i][LC - 1:LC, :] for i in idx]
    wss = [gs[i] - b_cols[i] + li_cols[i] for i in idx]
    ws_maxs = [jnp.max(wss[i], axis=0, keepdims=True) for i in idx]
    qks = [mm_nt(qs[i], ks[i]) for i in idx]
    m_in = [None] * len(pairs)
    m_out = [None] * len(pairs)
    for h in range(NH):
        cur = m[h:h + 1, 0:1]
        for c in range(n_ch):
            i = c * NH + h
            m_in[i] = cur
            cur = jnp.maximum(gs[i] + cur, ws_maxs[i])
            m_out[i] = cur
    m_inters = [b_cols[i] + m_in[i] for i in idx]
    m_ts = [jnp.maximum(m_inters[i], dm_maxs[i]) for i in idx]
    ss = [qks[i] * jnp.exp(dms[i] - m_ts[i]) for i in idx]
    scs = [jnp.exp(m_inters[i] - m_ts[i]) for i in idx]
    decs = [jnp.exp(gs[i] + m_in[i] - m_out[i]) for i in idx]
    kws = [jnp.exp(wss[i] - m_out[i]) * ks[i] for i in idx]
    c_adds = [mm_tn(kws[i], vs[i]) for i in idx]
    n_adds = [jnp.sum(kws[i], axis=0, keepdims=True) for i in idx]
    svs = [mm_nn(ss[i], vs[i]) for i in idx]
    s_sums = [jnp.sum(ss[i], axis=-1, keepdims=True) for i in idx]
    c_hs = [C[sl, :] for sl in sls]
    n_hs = [n[h:h + 1, :] for h in range(NH)]
    hhs = [None] * len(pairs)
    for c in range(n_ch):
        for h in range(NH):
            i = c * NH + h
            num = svs[i] + scs[i] * mm_nn(qs[i], c_hs[h])
            den = s_sums[i] + scs[i] * jnp.sum(qs[i] * n_hs[h], axis=-1, keepdims=True)
            hhs[i] = num / jnp.maximum(jnp.abs(den), jnp.exp(-m_ts[i]))
        c_hs = [decs[c * NH + h] * c_hs[h] + c_adds[c * NH + h] for h in range(NH)]
        n_hs = [decs[c * NH + h] * n_hs[h] + n_adds[c * NH + h] for h in range(NH)]
    ys = [hhs[i] * lax.rsqrt(jnp.mean(hhs[i] * hhs[i], axis=-1, keepdims=True) + EPS) * norm_w[:, sls[h]]
          * jax.nn.sigmoid(o_pre[rows[c], sls[h]]) * jax.nn.silu(z[rows[c], sls[h]]) for i, (c, h) in enumerate(pairs)]
    y = jnp.concatenate([jnp.concatenate(ys[c * NH:(c + 1) * NH], axis=1) for c in range(n_ch)], axis=0)
    last = (n_ch - 1) * NH
    m_rows = [jnp.broadcast_to(m_out[last + h], (1, HD)) for h in range(NH)]
    return y, jnp.concatenate(c_hs, axis=0), _rows_from(n_hs, 8, HD), _rows_from(m_rows, 8, HD)


@jax.custom_vjp
def _unit_lower_inverses(mats):
    eye = (lax.broadcasted_iota(jnp.int32, (LC, LC), 0) == lax.broadcasted_iota(jnp.int32, (LC, LC), 1)).astype(f32)
    ps = [-m for m in mats]
    ts = [eye + p for p in ps]
    for _ in range(5):
        ps = [dot3(p, p) for p in ps]
        ts = [t + dot3(t, p) for t, p in zip(ts, ps)]
    return tuple(ts)


def _unit_lower_inverses_fwd(mats):
    ts = _unit_lower_inverses(mats)
    return ts, ts


def _unit_lower_inverses_bwd(ts, gs):
    tts = [t.T for t in ts]
    mid = [dot3(tt, g) for tt, g in zip(tts, gs)]
    return (tuple(-dot3(m, tt) for m, tt in zip(mid, tts)),)


_unit_lower_inverses.defvjp(_unit_lower_inverses_fwd, _unit_lower_inverses_bwd)


def gd_chunk(qh8, q, kh8, k, vh8, v, z, small, conv_w, alog_row, dt_row, norm_w, st):
    n_ch = q.shape[0] // LC
    lane = lax.broadcasted_iota(jnp.int32, small.shape, 1)
    is_a = (lane >= 2 * NH) & (lane < 3 * NH)
    g_all = jnp.where(is_a, -jnp.exp(alog_row) * jax.nn.softplus(small + dt_row), 0.0)
    beta_all = jax.nn.sigmoid(small)
    qc = jax.nn.silu(causal_conv(qh8, q, conv_w[:, 0:SEG]))
    kc = jax.nn.silu(causal_conv(kh8, k, conv_w[:, SEG:2 * SEG]))
    vc = jax.nn.silu(causal_conv(vh8, v, conv_w[:, 2 * SEG:3 * SEG]))
    incl = _tri(LC)
    strict = _tri(LC, strict=True)
    rows = [slice(c * LC, (c + 1) * LC) for c in range(n_ch)]
    gc_alls = [cumsum_rows(g_all[r]) for r in rows]
    gc_ts = [g.T for g in gc_alls]
    pairs = [(c, h) for c in range(n_ch) for h in range(NH)]
    idx = range(len(pairs))
    sls = [slice(h * HD, (h + 1) * HD) for h in range(NH)]
    qs = [qc[rows[c], sls[h]] for c, h in pairs]
    ks = [kc[rows[c], sls[h]] for c, h in pairs]
    vs = [vc[rows[c], sls[h]] for c, h in pairs]
    qs = [x * lax.rsqrt(jnp.sum(x * x, axis=-1, keepdims=True) + EPS) * (HD ** -0.5) for x in qs]
    ks = [x * lax.rsqrt(jnp.sum(x * x, axis=-1, keepdims=True) + EPS) for x in ks]
    betas = [_lane_col(beta_all[rows[c]], 3 * NH + h) for c, h in pairs]
    gcs = [_lane_col(gc_alls[c], 2 * NH + h) for c, h in pairs]
    gams = [jnp.exp(jnp.where(incl, gcs[i] - gc_ts[c][2 * NH + h:2 * NH + h + 1, :], -jnp.inf))
            for i, (c, h) in enumerate(pairs)]
    kbs = [ks[i] * betas[i] for i in idx]
    mats = tuple(jnp.where(strict, mm_nt(kbs[i], ks[i]) * gams[i], 0.0) for i in idx)
    aqks = [mm_nt(qs[i], ks[i]) * gams[i] for i in idx]
    t_invs = _unit_lower_inverses(mats)
    us = [mm_nn(t_invs[i], vs[i] * betas[i]) for i in idx]
    ws = [mm_nn(t_invs[i], kbs[i] * jnp.exp(gcs[i])) for i in idx]
    g_lasts = [gcs[i][LC - 1:LC, :] for i in idx]
    q_decs = [qs[i] * jnp.exp(gcs[i]) for i in idx]
    k_decs = [ks[i] * jnp.exp(g_lasts[i] - gcs[i]) for i in idx]
    e_lasts = [jnp.exp(g_lasts[i]) for i in idx]
    s_hs = [st[sl, :] for sl in sls]
    os_ = [None] * len(pairs)
    for c in range(n_ch):
        ids = [c * NH + h for h in range(NH)]
        v_news = [us[i] - mm_nn(ws[i], s_hs[h]) for h, i in enumerate(ids)]
        for h, i in enumerate(ids):
            os_[i] = mm_nn(q_decs[i], s_hs[h]) + mm_nn(aqks[i], v_news[h])
        s_hs = [s_hs[h] * e_lasts[i] + mm_tn(k_decs[i], v_news[h]) for h, i in enumerate(ids)]
    ys = [os_[i] * lax.rsqrt(jnp.mean(os_[i] * os_[i], axis=-1, keepdims=True) + EPS) * norm_w
          * jax.nn.silu(z[rows[c], sls[h]]) for i, (c, h) in enumerate(pairs)]
    y = jnp.concatenate([jnp.concatenate(ys[c * NH:(c + 1) * NH], axis=1) for c in range(n_ch)], axis=0)
    return y, jnp.concatenate(s_hs, axis=0)


def rg_pre(xh8, x, conv_w, conv_b, gw_r, gw_i, gate_b, lam):
    xc = causal_conv(xh8, x, conv_w) + conv_b
    r = jax.nn.sigmoid(mm_nn(xc, gw_r) + gate_b[0:1])
    i = jax.nn.sigmoid(mm_nn(xc, gw_i) + gate_b[1:2])
    log_a = -RG_C * r * jax.nn.softplus(-lam)
    a = jnp.exp(log_a)
    th = jnp.tanh(log_a)
    one_minus_a2 = -2.0 * th / (1.0 - th)
    b = jnp.sqrt(one_minus_a2) * (i * xc)
    return a, b


def _scan_rows(a_ref, b_ref, out_ref, h0, n_rows, reverse):
    n_groups = n_rows // 8
    width = a_ref.shape[1]
    row = lax.broadcasted_iota(jnp.int32, (8, width), 0)

    def body(j, h):
        g = (n_groups - 1 - j) if reverse else j
        r0 = pl.multiple_of(g * 8, 8)
        av = a_ref[pl.ds(r0, 8), :]
        bv = b_ref[pl.ds(r0, 8), :]
        for d in (1, 2, 4):
            sh = (8 - d) if reverse else d
            a_s = pltpu.roll(av, sh, 0)
            b_s = pltpu.roll(bv, sh, 0)
            valid = (row < 8 - d) if reverse else (row >= d)
            bv = jnp.where(valid, av * b_s + bv, bv)
            av = jnp.where(valid, av * a_s, av)
        hv = av * h + bv
        out_ref[pl.ds(r0, 8), :] = hv
        return hv[0:1, :] if reverse else hv[7:8, :]

    return lax.fori_loop(0, n_groups, body, h0)


def _norm_rows(xv):
    r = lax.rsqrt(jnp.mean(xv * xv, axis=-1, keepdims=True) + EPS)
    return xv * r, r


def inproj(x, nw, w, xf=None):
    s, ts = x.shape[0], 256
    nt = s // ts
    n_in, n_out = 3, 2
    x_ops, x_in_specs, x_out_shapes, x_out_specs, aliases, x_sems, n_xin, n_land = _carried(xf, n_in, n_out)

    def body(*refs):
        (x_ref, nw_ref, w_ref), x_refs, (o_ref, ht_ref), land_refs, _, sem_refs = _cut_refs(refs, n_in, n_xin, n_out, n_land, 0)
        i = pl.program_id(0)
        _carry_open(xf, i, nt, x_refs, land_refs, sem_refs)
        xn, _ = _norm_rows(x_ref[...])
        hn = xn * nw_ref[...]
        o_ref[...] = jnp.dot(hn.astype(bf16), w_ref[...], preferred_element_type=f32)
        ht_ref[...] = hn.T.astype(bf16)
        _carry_close(xf, i, nt, x_refs, land_refs, sem_refs)

    return pl.pallas_call(
        body, grid=(nt,),
        in_specs=[pl.BlockSpec((ts, D), lambda i: (i, 0)), pl.BlockSpec((1, D), lambda i: (0, 0)),
                  pl.BlockSpec((D, DP), lambda i: (0, 0))] + x_in_specs,
        out_specs=[pl.BlockSpec((ts, DP), lambda i: (i, 0)), pl.BlockSpec((D, ts), lambda i: (0, i))] + x_out_specs,
        out_shape=[jax.ShapeDtypeStruct((s, DP), f32), jax.ShapeDtypeStruct((D, s), bf16)] + x_out_shapes,
        scratch_shapes=x_sems, input_output_aliases=aliases,
        name="inproj" if xf is None else "inproj_carrying", compiler_params=_cp(56))(x, nw, w, *x_ops)


def outproj(x, yr, ym, yg, wo):
    s, ts = x.shape[0], 256

    def body(x_ref, yr_ref, ym_ref, yg_ref, wo_ref, o_ref):
        acc = x_ref[...]
        for j, y_ref in enumerate((yr_ref, ym_ref, yg_ref)):
            acc = acc + jnp.dot(y_ref[...].astype(bf16), wo_ref[j * SEG:(j + 1) * SEG, :], preferred_element_type=f32)
        o_ref[...] = acc

    yspec = pl.BlockSpec((ts, SEG), lambda i: (i, 0))
    return pl.pallas_call(
        body, grid=(s // ts,),
        in_specs=[pl.BlockSpec((ts, D), lambda i: (i, 0)), yspec, yspec, yspec,
                  pl.BlockSpec((3 * SEG, D), lambda i: (0, 0))],
        out_specs=pl.BlockSpec((ts, D), lambda i: (i, 0)),
        out_shape=jax.ShapeDtypeStruct((s, D), f32), name="outproj", compiler_params=_cp(40))(x, yr, ym, yg, wo)


def head(x, fw, tgt):
    s, ts = x.shape[0], 256

    def body(x_ref, fw_ref, t_ref, dx_ref, loss_ref, dfw_ref):
        @pl.when(pl.program_id(0) == 0)
        def _():
            loss_ref[...] = jnp.zeros_like(loss_ref)
            dfw_ref[...] = jnp.zeros_like(dfw_ref)

        xn, r = _norm_rows(x_ref[...])
        fw_v = fw_ref[...]
        err = xn * fw_v - t_ref[...]
        loss_ref[...] += 0.5 * jnp.sum(jnp.mean(err * err, axis=-1, keepdims=True))
        dy = err * (1.0 / D)
        dfw_ref[0:1, :] += jnp.sum(dy * xn, axis=0, keepdims=True)
        dxn = dy * fw_v
        dx_ref[...] = r * (dxn - xn * jnp.mean(dxn * xn, axis=-1, keepdims=True))

    tile = pl.BlockSpec((ts, D), lambda i: (i, 0))
    return pl.pallas_call(
        body, grid=(s // ts,),
        in_specs=[tile, pl.BlockSpec((1, D), lambda i: (0, 0)), tile],
        out_specs=[tile, pl.BlockSpec((8, 128), lambda i: (0, 0)), pl.BlockSpec((8, D), lambda i: (0, 0))],
        out_shape=[jax.ShapeDtypeStruct((s, D), f32), jax.ShapeDtypeStruct((8, 128), f32),
                   jax.ShapeDtypeStruct((8, D), f32)], name="head")(x, fw, tgt)


def outproj_bwd(dx, yr, ym, yg, wo):
    s, ts = dx.shape[0], 256

    def body(dx_ref, yr_ref, ym_ref, yg_ref, wo_ref, dyr_ref, dym_ref, dyg_ref, gwo_ref):
        @pl.when(pl.program_id(0) == 0)
        def _():
            gwo_ref[...] = jnp.zeros_like(gwo_ref)

        dxb = dx_ref[...].astype(bf16)
        for j, (y_ref, dy_ref) in enumerate(((yr_ref, dyr_ref), (ym_ref, dym_ref), (yg_ref, dyg_ref))):
            rows = slice(j * SEG, (j + 1) * SEG)
            dy_ref[...] = lax.dot_general(dxb, wo_ref[rows, :], (((1,), (1,)), ((), ())), preferred_element_type=f32)
            gwo_ref[rows, :] += lax.dot_general(y_ref[...].astype(bf16), dxb, (((0,), (0,)), ((), ())),
                                                preferred_element_type=f32)

    yspec = pl.BlockSpec((ts, SEG), lambda i: (i, 0))
    wspec = pl.BlockSpec((3 * SEG, D), lambda i: (0, 0))
    ysh = jax.ShapeDtypeStruct((s, SEG), f32)
    return pl.pallas_call(
        body, grid=(s // ts,),
        in_specs=[pl.BlockSpec((ts, D), lambda i: (i, 0)), yspec, yspec, yspec, wspec],
        out_specs=[yspec, yspec, yspec, wspec],
        out_shape=[ysh, ysh, ysh, jax.ShapeDtypeStruct((3 * SEG, D), f32)],
        name="outproj_bwd", compiler_params=_cp(48))(dx, yr, ym, yg, wo)


def inproj_bwd_x(x, nw, w, dxo, d_rg, d_ml, d_gd, d_sa, d_sb):
    s, ts = x.shape[0], 256
    widths = (d_rg.shape[1], d_ml.shape[1], d_gd.shape[1], HD)

    def body(x_ref, nw_ref, w_ref, dxo_ref, rg_ref, ml_ref, gd_ref, sa_ref, sb_ref, dx_ref, dnw_ref):
        @pl.when(pl.program_id(0) == 0)
        def _():
            dnw_ref[...] = jnp.zeros_like(dnw_ref)

        xn, r = _norm_rows(x_ref[...])
        pieces = (rg_ref[...], ml_ref[...], gd_ref[...], sa_ref[...] + sb_ref[...])
        dhn = jnp.zeros((ts, D), f32)
        c0 = 0
        for piece, wd in zip(pieces, widths):
            dhn = dhn + lax.dot_general(piece.astype(bf16), w_ref[:, c0:c0 + wd], (((1,), (1,)), ((), ())),
                                        preferred_element_type=f32)
            c0 += wd
        dnw_ref[0:1, :] += jnp.sum(dhn * xn, axis=0, keepdims=True)
        dxn = dhn * nw_ref[...]
        dx_ref[...] = dxo_ref[...] + r * (dxn - xn * jnp.mean(dxn * xn, axis=-1, keepdims=True))

    tile = pl.BlockSpec((ts, D), lambda i: (i, 0))
    return pl.pallas_call(
        body, grid=(s // ts,),
        in_specs=[tile, pl.BlockSpec((1, D), lambda i: (0, 0)), pl.BlockSpec((D, DP), lambda i: (0, 0)), tile]
        + [pl.BlockSpec((ts, wd), lambda i: (i, 0)) for wd in widths] + [pl.BlockSpec((ts, HD), lambda i: (i, 0))],
        out_specs=[tile, pl.BlockSpec((8, D), lambda i: (0, 0))],
        out_shape=[jax.ShapeDtypeStruct((s, D), f32), jax.ShapeDtypeStruct((8, D), f32)],
        name="inproj_bwd_x", compiler_params=_cp(56))(x, nw, w, dxo, d_rg, d_ml, d_gd, d_sa, d_sb)


def wgrad(hn_t, dps, name):
    s = hn_t.shape[1]
    c = dps[0].shape[1]
    ct = min(c, SEG)
    n_dp = len(dps)

    def body(*refs):
        ht_ref = refs[0]
        dp_refs = refs[1:1 + n_dp]
        o_ref = refs[1 + n_dp]
        dp = dp_refs[0][...]
        for extra in dp_refs[1:]:
            dp = dp + extra[...]
        o_ref[...] = jnp.dot(ht_ref[...], dp.astype(bf16), preferred_element_type=f32)

    return pl.pallas_call(
        body, grid=(c // ct,),
        in_specs=[pl.BlockSpec((D, s), lambda j: (0, 0))] + [pl.BlockSpec((s, ct), lambda j: (0, j)) for _ in dps],
        out_specs=pl.BlockSpec((D, ct), lambda j: (0, j)),
        out_shape=jax.ShapeDtypeStruct((D, c), f32), name=name, compiler_params=_cp(40))(hn_t, *dps)


def _seg_spec(rows, seg, n_tiles=None):
    if n_tiles is None:
        return pl.BlockSpec((rows, SEG), lambda i: (i, seg))
    return pl.BlockSpec((rows, SEG), lambda i: (n_tiles - 1 - i, seg))


def _halo_spec(rows, seg, n_tiles=None):
    per = rows // 8
    if n_tiles is None:
        return pl.BlockSpec((8, SEG), lambda i: (jnp.maximum(i * per - 1, 0), seg))
    return pl.BlockSpec((8, SEG), lambda i: (jnp.maximum((n_tiles - 1 - i) * per - 1, 0), seg))


def _const_spec(shape):
    return pl.BlockSpec(shape, lambda i: tuple(0 for _ in shape))


def rglru_fwd(proj, conv_w, conv_b, gw_r, gw_i, gate_b, lam):
    s = proj.shape[0]
    tr = RG_TILE

    def body(xh_ref, x_ref, z_ref, cw_ref, cb_ref, gr_ref, gi_ref, gb_ref, lam_ref, y_ref, h_ref, a_s, b_s, hc):
        first = pl.program_id(0) == 0

        @pl.when(first)
        def _():
            hc[...] = jnp.zeros_like(hc)

        xh = jnp.where(first, 0.0, xh_ref[...])
        a, b = rg_pre(xh, x_ref[...], cw_ref[...], cb_ref[...], gr_ref[...], gi_ref[...], gb_ref[...], lam_ref[...])
        a_s[...] = a
        b_s[...] = b
        hc[0:1, :] = _scan_rows(a_s, b_s, h_ref, hc[0:1, :], tr, False)
        y_ref[...] = (h_ref[...] * jax.nn.silu(z_ref[...])).astype(bf16)

    out = pl.BlockSpec((tr, SEG), lambda i: (i, 0))
    return pl.pallas_call(
        body, grid=(s // tr,),
        in_specs=[_halo_spec(tr, 0), _seg_spec(tr, 0), _seg_spec(tr, 1), _const_spec((4, SEG)), _const_spec((1, SEG)),
                  _const_spec((SEG, SEG)), _const_spec((SEG, SEG)), _const_spec((2, SEG)), _const_spec((1, SEG))],
        out_specs=[out, out],
        out_shape=[jax.ShapeDtypeStruct((s, SEG), bf16), jax.ShapeDtypeStruct((s, SEG), f32)],
        scratch_shapes=[pltpu.VMEM((tr, SEG), f32), pltpu.VMEM((tr, SEG), f32), pltpu.VMEM((8, SEG), f32)],
        name="rglru_fwd", compiler_params=_cp(40))(proj, proj, proj, conv_w, conv_b, gw_r, gw_i, gate_b, lam)


def rglru_bwd(proj, hs, dy, conv_w, conv_b, gw_r, gw_i, gate_b, lam):
    s = proj.shape[0]
    tr = RG_TILE
    nt = s // tr

    def body(xh_ref, x_ref, z_ref, hh_ref, h_ref, dy_ref, cw_ref, cb_ref, gr_ref, gi_ref, gb_ref, lam_ref,
             dp_ref, dcw_ref, dcb_ref, dgr_ref, dgi_ref, dgb_ref, dlam_ref,
             an_s, g_s, dh_s, a_first, dh_first, dhalo):
        i = pl.program_id(0)
        first_tile = i == nt - 1

        @pl.when(i == 0)
        def _():
            for ref in (dcw_ref, dcb_ref, dgr_ref, dgi_ref, dgb_ref, dlam_ref, a_first, dh_first, dhalo):
                ref[...] = jnp.zeros_like(ref)

        xh = jnp.where(first_tile, 0.0, xh_ref[...])
        params = (cw_ref[...], cb_ref[...], gr_ref[...], gi_ref[...], gb_ref[...], lam_ref[...])
        (a, _), vjp = jax.vjp(rg_pre, xh, x_ref[...], *params)
        zv = z_ref[...]
        hv = h_ref[...]
        dyv = dy_ref[...]
        sig = jax.nn.sigmoid(zv)
        g_s[...] = dyv * (zv * sig)
        dp_ref[:, SEG:2 * SEG] = (dyv * hv * (sig * (1.0 + zv * (1.0 - sig)))).astype(bf16)
        ridx = lax.broadcasted_iota(jnp.int32, (tr, SEG), 0)
        an_s[...] = jnp.where(ridx == tr - 1, jnp.broadcast_to(a_first[0:1, :], (tr, SEG)), pltpu.roll(a, tr - 1, 0))
        _scan_rows(an_s, g_s, dh_s, dh_first[0:1, :], tr, True)
        dh = dh_s[...]
        h_prev_last = jnp.where(first_tile, 0.0, hh_ref[...])[7:8, :]
        h_prev = pltpu.roll(hv, 1, 0)
        h_prev = jnp.where(ridx == 0, jnp.broadcast_to(h_prev_last, (tr, SEG)), h_prev)
        dxh, dx, dcw, dcb, dgr, dgi, dgb, dlam = vjp((dh * h_prev, dh))
        dx = dx + jnp.concatenate([jnp.zeros((tr - 8, SEG), f32), dhalo[...]], axis=0)
        dp_ref[:, 0:SEG] = dx.astype(bf16)
        dhalo[...] = dxh
        a_first[0:1, :] = a[0:1, :]
        dh_first[0:1, :] = dh[0:1, :]
        dcw_ref[...] += dcw
        dcb_ref[...] += dcb
        dgr_ref[...] += dgr
        dgi_ref[...] += dgi
        dgb_ref[...] += dgb
        dlam_ref[...] += dlam

    pspecs = [_const_spec((4, SEG)), _const_spec((1, SEG)), _const_spec((SEG, SEG)), _const_spec((SEG, SEG)),
              _const_spec((2, SEG)), _const_spec((1, SEG))]
    pshapes = [jax.ShapeDtypeStruct(sh, f32) for sh in ((4, SEG), (1, SEG), (SEG, SEG), (SEG, SEG), (2, SEG), (1, SEG))]
    tile = pl.BlockSpec((tr, SEG), lambda i: (nt - 1 - i, 0))
    return pl.pallas_call(
        body, grid=(nt,),
        in_specs=[_halo_spec(tr, 0, nt), _seg_spec(tr, 0, nt), _seg_spec(tr, 1, nt),
                  pl.BlockSpec((8, SEG), lambda i: (jnp.maximum((nt - 1 - i) * (tr // 8) - 1, 0), 0)), tile, tile] + pspecs,
        out_specs=[pl.BlockSpec((tr, 2 * SEG), lambda i: (nt - 1 - i, 0))] + pspecs,
        out_shape=[jax.ShapeDtypeStruct((s, 2 * SEG), bf16)] + pshapes,
        scratch_shapes=[pltpu.VMEM((tr, SEG), f32), pltpu.VMEM((tr, SEG), f32), pltpu.VMEM((tr, SEG), f32),
                        pltpu.VMEM((8, SEG), f32), pltpu.VMEM((8, SEG), f32), pltpu.VMEM((8, SEG), f32)],
        name="rglru_bwd", compiler_params=_cp(48))(proj, proj, proj, hs, hs, dy, conv_w, conv_b, gw_r, gw_i, gate_b, lam)


ML_SEGS = (2, 3, 4, 5, 6)
SMALL_BLK = SMALL0 // HD


def _cut_refs(refs, n_in, n_xin, n_out, n_land, n_scratch):
    bounds = [0, n_in, n_in + n_xin, n_in + n_xin + n_out, n_in + n_xin + n_out + n_land,
              n_in + n_xin + n_out + n_land + n_scratch, len(refs)]
    return [refs[a:b] for a, b in zip(bounds[:-1], bounds[1:])]


def mlstm_fwd(proj, bias_row, norm_w, xf=None):
    s = proj.shape[0]
    nc = s // LR
    n_in, n_out = 8, 4
    x_ops, x_in_specs, x_out_shapes, x_out_specs, aliases, x_sems, n_xin, n_land = _carried(xf, n_in, n_out)

    def body(*refs):
        ins, x_refs, outs, land_refs, scratch, sem_refs = _cut_refs(refs, n_in, n_xin, n_out, n_land, 3)
        q_ref, k_ref, v_ref, o_ref, z_ref, sm_ref, b_ref, nw_ref = ins
        y_ref, cs_ref, ns_ref, ms_ref = outs
        c_s, n_s, m_s = scratch
        i = pl.program_id(0)
        _carry_open(xf, i, nc, x_refs, land_refs, sem_refs)

        @pl.when(i == 0)
        def _():
            c_s[...] = jnp.zeros_like(c_s)
            n_s[...] = jnp.zeros_like(n_s)
            m_s[...] = jnp.zeros_like(m_s)

        cs_ref[0] = c_s[...]
        ns_ref[0] = n_s[...]
        ms_ref[0] = m_s[...]
        y, c2, n2, m2 = ml_chunk(q_ref[...], k_ref[...], v_ref[...], o_ref[...], z_ref[...], sm_ref[...],
                                 b_ref[...], nw_ref[...], c_s[...], n_s[...], m_s[...])
        y_ref[...] = y.astype(bf16)
        c_s[...] = c2
        n_s[...] = n2
        m_s[...] = m2
        _carry_close(xf, i, nc, x_refs, land_refs, sem_refs)

    return pl.pallas_call(
        body, grid=(nc,),
        in_specs=[_seg_spec(LR, sg) for sg in ML_SEGS]
        + [pl.BlockSpec((LR, HD), lambda i: (i, SMALL_BLK)), _const_spec((1, HD)), _const_spec((1, SEG))] + x_in_specs,
        out_specs=[pl.BlockSpec((LR, SEG), lambda i: (i, 0)), pl.BlockSpec((1, SEG, HD), lambda i: (i, 0, 0)),
                   pl.BlockSpec((1, 8, HD), lambda i: (i, 0, 0)), pl.BlockSpec((1, 8, HD), lambda i: (i, 0, 0))] + x_out_specs,
        out_shape=[jax.ShapeDtypeStruct((s, SEG), bf16), jax.ShapeDtypeStruct((nc, SEG, HD), f32),
                   jax.ShapeDtypeStruct((nc, 8, HD), f32), jax.ShapeDtypeStruct((nc, 8, HD), f32)] + x_out_shapes,
        scratch_shapes=[pltpu.VMEM((SEG, HD), f32), pltpu.VMEM((8, HD), f32), pltpu.VMEM((8, HD), f32)] + x_sems,
        input_output_aliases=aliases, name="mlstm_fwd" if xf is None else "mlstm_fwd_carrying")(
            proj, proj, proj, proj, proj, proj, bias_row, norm_w, *x_ops)


def mlstm_bwd(proj, dy, cs, ns, ms, bias_row, norm_w, xf=None):
    s = proj.shape[0]
    nc = s // LR
    n_in, n_out = 12, 4
    x_ops, x_in_specs, x_out_shapes, x_out_specs, aliases, x_sems, n_xin, n_land = _carried(xf, n_in, n_out)

    def body(*refs):
        ins, x_refs, outs, land_refs, scratch, sem_refs = _cut_refs(refs, n_in, n_xin, n_out, n_land, 3)
        q_ref, k_ref, v_ref, o_ref, z_ref, sm_ref, dy_ref, cs_ref, ns_ref, ms_ref, b_ref, nw_ref = ins
        dp_ref, dsm_ref, db_ref, dnw_ref = outs
        dc_s, dn_s, dm_s = scratch
        i = pl.program_id(0)
        _carry_open(xf, i, nc, x_refs, land_refs, sem_refs)

        @pl.when(i == 0)
        def _():
            for ref in (db_ref, dnw_ref, dc_s, dn_s, dm_s):
                ref[...] = jnp.zeros_like(ref)

        _, vjp = jax.vjp(ml_chunk, q_ref[...], k_ref[...], v_ref[...], o_ref[...], z_ref[...], sm_ref[...],
                         b_ref[...], nw_ref[...], cs_ref[0], ns_ref[0], ms_ref[0])
        dq, dk, dv, do, dz, dsm, db, dnw, dc, dn, dm = vjp((dy_ref[...], dc_s[...], dn_s[...], dm_s[...]))
        for j, val in enumerate((dq, dk, dv, do, dz)):
            dp_ref[:, j * SEG:(j + 1) * SEG] = val.astype(bf16)
        dsm_ref[...] = dsm
        db_ref[0:1, :] += db
        dnw_ref[0:1, :] += dnw
        dc_s[...] = dc
        dn_s[...] = dn
        dm_s[...] = dm
        _carry_close(xf, i, nc, x_refs, land_refs, sem_refs)

    rev3 = lambda i: (nc - 1 - i, 0, 0)
    return pl.pallas_call(
        body, grid=(nc,),
        in_specs=[_seg_spec(LR, sg, nc) for sg in ML_SEGS]
        + [pl.BlockSpec((LR, HD), lambda i: (nc - 1 - i, SMALL_BLK)), pl.BlockSpec((LR, SEG), lambda i: (nc - 1 - i, 0)),
           pl.BlockSpec((1, SEG, HD), rev3), pl.BlockSpec((1, 8, HD), rev3), pl.BlockSpec((1, 8, HD), rev3),
           _const_spec((1, HD)), _const_spec((1, SEG))] + x_in_specs,
        out_specs=[pl.BlockSpec((LR, 5 * SEG), lambda i: (nc - 1 - i, 0)), pl.BlockSpec((LR, HD), lambda i: (nc - 1 - i, 0)),
                   _const_spec((8, HD)), _const_spec((8, SEG))] + x_out_specs,
        out_shape=[jax.ShapeDtypeStruct((s, 5 * SEG), bf16), jax.ShapeDtypeStruct((s, HD), f32),
                   jax.ShapeDtypeStruct((8, HD), f32), jax.ShapeDtypeStruct((8, SEG), f32)] + x_out_shapes,
        scratch_shapes=[pltpu.VMEM((SEG, HD), f32), pltpu.VMEM((8, HD), f32), pltpu.VMEM((8, HD), f32)] + x_sems,
        input_output_aliases=aliases, name="mlstm_bwd" if xf is None else "mlstm_bwd_carrying", compiler_params=_cp(48))(
            proj, proj, proj, proj, proj, proj, dy, cs, ns, ms, bias_row, norm_w, *x_ops)


GD_SEGS = (7, 8, 9)


def _carried(xf, n_in, n_out):
    operands, out_shapes, aliases, sems, _, n_xin, n_land = _carry_plumb(xf, n_in, n_out)
    return operands, [_ANY] * n_xin, out_shapes, [_ANY] * n_land, aliases, sems, n_xin, n_land


def _carry_open(xf, i, n_steps, x_refs, land_refs, sem_refs):
    if xf is None:
        return
    srcs = x_refs[:len(xf.srcs)]

    @pl.when(i == 0)
    def _():
        _carry_start(xf, srcs, land_refs, sem_refs)

    @pl.when(i == max(n_steps - 2, 0))
    def _():
        _carry_middle(xf, srcs, land_refs, sem_refs)


def _carry_close(xf, i, n_steps, x_refs, land_refs, sem_refs):
    if xf is None:
        return

    @pl.when(i == n_steps - 1)
    def _():
        _carry_finish(xf, x_refs[:len(xf.srcs)], land_refs, sem_refs)


def gdn_fwd(proj, conv_w, alog_row, dt_row, norm_w, xf=None):
    s = proj.shape[0]
    nc = s // LR
    n_in, n_out = 12, 2
    x_ops, x_in_specs, x_out_shapes, x_out_specs, aliases, x_sems, n_xin, n_land = _carried(xf, n_in, n_out)

    def body(*refs):
        qh_ref, q_ref, kh_ref, k_ref, vh_ref, v_ref, z_ref, sm_ref, cw_ref, al_ref, dt_ref, nw_ref = refs[:n_in]
        x_refs = refs[n_in:n_in + n_xin]
        y_ref, ss_ref = refs[n_in + n_xin:n_in + n_xin + n_out]
        land_refs = refs[n_in + n_xin + n_out:n_in + n_xin + n_out + n_land]
        st_s = refs[n_in + n_xin + n_out + n_land]
        sem_refs = refs[n_in + n_xin + n_out + n_land + 1:]
        i = pl.program_id(0)
        first = i == 0
        _carry_open(xf, i, nc, x_refs, land_refs, sem_refs)

        @pl.when(first)
        def _():
            st_s[...] = jnp.zeros_like(st_s)

        ss_ref[0] = st_s[...]
        halo = [jnp.where(first, 0.0, r[...]) for r in (qh_ref, kh_ref, vh_ref)]
        y, st2 = gd_chunk(halo[0], q_ref[...], halo[1], k_ref[...], halo[2], v_ref[...], z_ref[...], sm_ref[...],
                          cw_ref[...], al_ref[...], dt_ref[...], nw_ref[...], st_s[...])
        y_ref[...] = y.astype(bf16)
        st_s[...] = st2
        _carry_close(xf, i, nc, x_refs, land_refs, sem_refs)

    qkv_specs = []
    for sg in GD_SEGS:
        qkv_specs += [_halo_spec(LR, sg), _seg_spec(LR, sg)]
    return pl.pallas_call(
        body, grid=(nc,),
        in_specs=qkv_specs + [_seg_spec(LR, 10), pl.BlockSpec((LR, HD), lambda i: (i, SMALL_BLK)),
                              _const_spec((4, 3 * SEG)), _const_spec((1, HD)), _const_spec((1, HD)), _const_spec((1, HD))]
        + x_in_specs,
        out_specs=[pl.BlockSpec((LR, SEG), lambda i: (i, 0)), pl.BlockSpec((1, SEG, HD), lambda i: (i, 0, 0))] + x_out_specs,
        out_shape=[jax.ShapeDtypeStruct((s, SEG), bf16), jax.ShapeDtypeStruct((nc, SEG, HD), f32)] + x_out_shapes,
        scratch_shapes=[pltpu.VMEM((SEG, HD), f32)] + x_sems, input_output_aliases=aliases,
        name="gdn_fwd" if xf is None else "gdn_fwd_carrying")(
            proj, proj, proj, proj, proj, proj, proj, proj, conv_w, alog_row, dt_row, norm_w, *x_ops)


def gdn_bwd(proj, dy, ss, conv_w, alog_row, dt_row, norm_w, xf=None):
    s = proj.shape[0]
    nc = s // LR
    n_in, n_out = 14, 6
    x_ops, x_in_specs, x_out_shapes, x_out_specs, aliases, x_sems, n_xin, n_land = _carried(xf, n_in, n_out)

    def body(*refs):
        (qh_ref, q_ref, kh_ref, k_ref, vh_ref, v_ref, z_ref, sm_ref, dy_ref, ss_ref,
         cw_ref, al_ref, dt_ref, nw_ref) = refs[:n_in]
        x_refs = refs[n_in:n_in + n_xin]
        dp_ref, dsm_ref, dcw_ref, dal_ref, ddt_ref, dnw_ref = refs[n_in + n_xin:n_in + n_xin + n_out]
        land_refs = refs[n_in + n_xin + n_out:n_in + n_xin + n_out + n_land]
        dst_s, dhalo = refs[n_in + n_xin + n_out + n_land:n_in + n_xin + n_out + n_land + 2]
        sem_refs = refs[n_in + n_xin + n_out + n_land + 2:]
        i = pl.program_id(0)
        first_chunk = i == nc - 1
        _carry_open(xf, i, nc, x_refs, land_refs, sem_refs)

        @pl.when(i == 0)
        def _():
            for ref in (dcw_ref, dal_ref, ddt_ref, dnw_ref, dst_s, dhalo):
                ref[...] = jnp.zeros_like(ref)

        halo = [jnp.where(first_chunk, 0.0, r[...]) for r in (qh_ref, kh_ref, vh_ref)]
        _, vjp = jax.vjp(gd_chunk, halo[0], q_ref[...], halo[1], k_ref[...], halo[2], v_ref[...], z_ref[...], sm_ref[...],
                         cw_ref[...], al_ref[...], dt_ref[...], nw_ref[...], ss_ref[0])
        dqh, dq, dkh, dk, dvh, dv, dz, dsm, dcw, dal, ddt, dnw, dst = vjp((dy_ref[...], dst_s[...]))
        for j, val in enumerate((dq, dk, dv)):
            val = val + jnp.concatenate([jnp.zeros((LR - 8, SEG), f32), dhalo[:, j * SEG:(j + 1) * SEG]], axis=0)
            dp_ref[:, j * SEG:(j + 1) * SEG] = val.astype(bf16)
        dp_ref[:, 3 * SEG:4 * SEG] = dz.astype(bf16)
        for j, val in enumerate((dqh, dkh, dvh)):
            dhalo[:, j * SEG:(j + 1) * SEG] = val
        dsm_ref[...] = dsm
        dcw_ref[...] += dcw
        dal_ref[0:1, :] += dal
        ddt_ref[0:1, :] += ddt
        dnw_ref[0:1, :] += dnw
        dst_s[...] = dst
        _carry_close(xf, i, nc, x_refs, land_refs, sem_refs)

    qkv_specs = []
    for sg in GD_SEGS:
        qkv_specs += [_halo_spec(LR, sg, nc), _seg_spec(LR, sg, nc)]
    return pl.pallas_call(
        body, grid=(nc,),
        in_specs=qkv_specs + [_seg_spec(LR, 10, nc), pl.BlockSpec((LR, HD), lambda i: (nc - 1 - i, SMALL_BLK)),
                              pl.BlockSpec((LR, SEG), lambda i: (nc - 1 - i, 0)),
                              pl.BlockSpec((1, SEG, HD), lambda i: (nc - 1 - i, 0, 0)),
                              _const_spec((4, 3 * SEG)), _const_spec((1, HD)), _const_spec((1, HD)), _const_spec((1, HD))]
        + x_in_specs,
        out_specs=[pl.BlockSpec((LR, 4 * SEG), lambda i: (nc - 1 - i, 0)), pl.BlockSpec((LR, HD), lambda i: (nc - 1 - i, 0)),
                   _const_spec((4, 3 * SEG)), _const_spec((8, HD)), _const_spec((8, HD)), _const_spec((8, HD))] + x_out_specs,
        out_shape=[jax.ShapeDtypeStruct((s, 4 * SEG), bf16), jax.ShapeDtypeStruct((s, HD), f32),
                   jax.ShapeDtypeStruct((4, 3 * SEG), f32), jax.ShapeDtypeStruct((8, HD), f32),
                   jax.ShapeDtypeStruct((8, HD), f32), jax.ShapeDtypeStruct((8, HD), f32)] + x_out_shapes,
        scratch_shapes=[pltpu.VMEM((SEG, HD), f32), pltpu.VMEM((8, 3 * SEG), f32)] + x_sems, input_output_aliases=aliases,
        name="gdn_bwd" if xf is None else "gdn_bwd_carrying", compiler_params=_cp(48))(
            proj, proj, proj, proj, proj, proj, proj, proj, dy, ss, conv_w, alog_row, dt_row, norm_w, *x_ops)


def _my_place():
    return lax.axis_index("x"), lax.axis_index("y"), lax.axis_index("c")


def _slot(p):
    return 4 * p[0] + 2 * p[1] + p[2]


def _peer(me, j):
    flips = ((j >> 2) & 1, (j >> 1) & 1, j & 1)
    return tuple((1 - v) if fl else v for v, fl in zip(me, flips))


_ANY = pl.BlockSpec(memory_space=pl.ANY)


class Scatter(NamedTuple):
    srcs: list
    lands: list
    land_of: list
    layer: list
    span: list


class Gather2(NamedTuple):
    srcs: list
    lands: list
    span: list


def _carry_plumb(cx, n_in, n_out):
    if cx is None:
        return [], [], {}, [], 0, 0, 0
    n_src = len(cx.srcs)
    passed = [li for li, ld in enumerate(cx.lands) if not isinstance(ld, jax.ShapeDtypeStruct)]
    operands = list(cx.srcs) + [cx.lands[li] for li in passed]
    aliases = {n_in + n_src + k: n_out + li for k, li in enumerate(passed)}
    out_shapes = [jax.ShapeDtypeStruct(ld.shape, ld.dtype) for ld in cx.lands]
    sems = [pltpu.SemaphoreType.DMA((n_src, N_DEV - 1)), pltpu.SemaphoreType.DMA((n_src, N_DEV - 1)),
            pltpu.SemaphoreType.DMA((n_src,))]
    return operands, out_shapes, aliases, sems, n_src, len(operands), len(cx.lands)


def _scatter_copies(sc, src_refs, land_refs, send_sems, recv_sems, local_sems):
    me = _my_place()
    mine = _slot(me)
    local, remote = [], []
    for a, src_ref in enumerate(src_refs):
        lo, hi = sc.span[a]
        rows = pl.ds(lo, hi - lo)
        dst = land_refs[sc.land_of[a]].at[mine, sc.layer[a], rows]
        local.append(pltpu.make_async_copy(src_ref.at[mine, rows], dst, local_sems.at[a]))
        for j in range(1, N_DEV):
            peer = _peer(me, j)
            remote.append(pltpu.make_async_remote_copy(
                src_ref=src_ref.at[_slot(peer), rows], dst_ref=dst, send_sem=send_sems.at[a, j - 1],
                recv_sem=recv_sems.at[a, j - 1], device_id=peer, device_id_type=MESH))
    return local, remote


def _gather2_copy(land, sems, a, k, block_of, to, rows, src=None):
    dst = land.at[_slot(block_of), rows]
    return pltpu.make_async_remote_copy(src_ref=dst if src is None else src, dst_ref=dst, send_sem=sems[0].at[a, k],
                                        recv_sem=sems[1].at[a, k], device_id=to, device_id_type=MESH)


def _gather2_places():
    x, y, c = _my_place()
    return (x, y, c), (x, y, 1 - c), [(1 - x, y), (x, 1 - y), (1 - x, 1 - y)], c


def _carry_start(cx, src_refs, land_refs, sems):
    if isinstance(cx, Scatter):
        local, remote = _scatter_copies(cx, src_refs, land_refs, *sems)
        for cp in local + remote:
            cp.start()
        return
    me, sib, chips, c = _gather2_places()
    for a, (src_ref, land) in enumerate(zip(src_refs, land_refs)):
        rows = pl.ds(cx.span[a][0], cx.span[a][1] - cx.span[a][0])
        src = src_ref.at[rows]
        pltpu.make_async_copy(src, land.at[_slot(me), rows], sems[2].at[a]).start()
        _gather2_copy(land, sems, a, 0, me, sib, rows, src=src).start()
        for j, chip in enumerate(chips):
            _gather2_copy(land, sems, a, 1 + j, me, (*chip, c), rows, src=src).start()


def _carry_middle(cx, src_refs, land_refs, sems):
    if isinstance(cx, Scatter):
        return
    me, sib, chips, c = _gather2_places()
    for a, land in enumerate(land_refs):
        rows = pl.ds(cx.span[a][0], cx.span[a][1] - cx.span[a][0])
        for j, chip in enumerate(chips):
            _gather2_copy(land, sems, a, 1 + j, (*chip, c), me, rows).wait_recv()
            _gather2_copy(land, sems, a, 4 + j, (*chip, c), sib, rows).start()


def _carry_finish(cx, src_refs, land_refs, sems):
    if isinstance(cx, Scatter):
        local, remote = _scatter_copies(cx, src_refs, land_refs, *sems)
        for cp in remote:
            cp.wait()
        for cp in local:
            cp.wait()
        return
    me, sib, chips, c = _gather2_places()
    for a, (src_ref, land) in enumerate(zip(src_refs, land_refs)):
        rows = pl.ds(cx.span[a][0], cx.span[a][1] - cx.span[a][0])
        src = src_ref.at[rows]
        _gather2_copy(land, sems, a, 0, sib, me, rows).wait_recv()
        for j, chip in enumerate(chips):
            _gather2_copy(land, sems, a, 4 + j, (*chip, 1 - c), me, rows).wait_recv()
        _gather2_copy(land, sems, a, 0, me, sib, rows, src=src).wait_send()
        for j, chip in enumerate(chips):
            _gather2_copy(land, sems, a, 1 + j, me, (*chip, c), rows, src=src).wait_send()
            _gather2_copy(land, sems, a, 4 + j, (*chip, c), sib, rows).wait_send()
        pltpu.make_async_copy(src, land.at[_slot(me), rows], sems[2].at[a]).wait()


def exchange(cx, name):
    operands, out_shapes, aliases, sems, n_src, n_xin, n_land = _carry_plumb(cx, 0, 0)

    def body(*refs):
        src_refs, land_refs, sem_refs = refs[:n_src], refs[n_xin:n_xin + n_land], refs[n_xin + n_land:]
        _carry_start(cx, src_refs, land_refs, sem_refs)
        _carry_middle(cx, src_refs, land_refs, sem_refs)
        _carry_finish(cx, src_refs, land_refs, sem_refs)

    return pl.pallas_call(body, in_specs=[_ANY] * n_xin, out_specs=[_ANY] * n_land, out_shape=out_shapes,
                          scratch_shapes=sems, input_output_aliases=aliases, name=name)(*operands)


def small_allreduce(packed, name, xf=None):
    r = packed.shape[0]
    rc = r // N_DEV
    x_ops, x_in_specs, x_out_shapes, x_out_specs, aliases, x_sems, n_xin, n_land = _carried(xf, 1, 1)

    def body(*refs):
        (in_ref,), x_refs, (out_ref,), land_refs, scratch, sem_refs = _cut_refs(refs, 1, n_xin, 1, n_land, 5)
        recv_buf, send1, recv1, send2, recv2 = scratch
        if xf is not None:
            _carry_start(xf, x_refs[:len(xf.srcs)], land_refs, sem_refs)
        me = _my_place()
        mine = _slot(me)
        my_rows = pl.ds(pl.multiple_of(mine * rc, 8), rc)
        first, second = [], []
        for j in range(1, N_DEV):
            peer = _peer(me, j)
            peer_rows = pl.ds(pl.multiple_of(_slot(peer) * rc, 8), rc)
            first.append(pltpu.make_async_remote_copy(
                src_ref=in_ref.at[peer_rows], dst_ref=recv_buf.at[mine], send_sem=send1.at[j - 1], recv_sem=recv1.at[j - 1],
                device_id=peer, device_id_type=MESH))
            second.append(pltpu.make_async_remote_copy(
                src_ref=out_ref.at[my_rows], dst_ref=out_ref.at[my_rows], send_sem=send2.at[j - 1], recv_sem=recv2.at[j - 1],
                device_id=peer, device_id_type=MESH))
        for cp in first:
            cp.start()
        recv_buf[mine] = in_ref[my_rows]
        for cp in first:
            cp.wait()
        acc = recv_buf[0]
        for k in range(1, N_DEV):
            acc = acc + recv_buf[k]
        out_ref[my_rows] = acc
        for cp in second:
            cp.start()
        for cp in second:
            cp.wait()
        if xf is not None:
            _carry_finish(xf, x_refs[:len(xf.srcs)], land_refs, sem_refs)

    vmem = pl.BlockSpec(memory_space=pltpu.VMEM)
    return pl.pallas_call(
        body, in_specs=[vmem] + x_in_specs, out_specs=[vmem] + x_out_specs,
        out_shape=[jax.ShapeDtypeStruct((r, 128), f32)] + x_out_shapes,
        scratch_shapes=[pltpu.VMEM((N_DEV, rc, 128), f32)] + [pltpu.SemaphoreType.DMA((N_DEV - 1,))] * 4 + x_sems,
        input_output_aliases=aliases, name=name, compiler_params=_cp(32))(packed, *x_ops)


def _adamw(w, g, m, v):
    m = ADAM_B1 * m + (1.0 - ADAM_B1) * g
    v = ADAM_B2 * v + (1.0 - ADAM_B2) * (g * g)
    m_hat = m / (1.0 - ADAM_B1 ** ADAM_STEP)
    v_hat = v / (1.0 - ADAM_B2 ** ADAM_STEP)
    delta = -ADAM_LR * (m_hat / (jnp.sqrt(v_hat) + ADAM_EPS) + ADAM_WD * w)
    return delta, m, v


def adam_slots(slots, w, m, v, rows, name):
    _, depth, r, c = slots.shape

    def body(s_ref, w_ref, m_ref, v_ref, g_ref, d_ref, m2_ref, v2_ref):
        g = s_ref[0, 0].astype(f32)
        for k in range(1, N_DEV):
            g = g + s_ref[k, 0].astype(f32)
        d, m2, v2 = _adamw(w_ref[0], g, m_ref[0], v_ref[0])
        g_ref[0] = g
        d_ref[0] = d
        m2_ref[0] = m2
        v2_ref[0] = v2

    blk = pl.BlockSpec((1, rows, c), lambda l, i: (l, i, 0))
    sh = jax.ShapeDtypeStruct((depth, r, c), f32)
    return pl.pallas_call(
        body, grid=(depth, r // rows),
        in_specs=[pl.BlockSpec((N_DEV, 1, rows, c), lambda l, i: (0, l, i, 0)), blk, blk, blk],
        out_specs=[blk] * 4, out_shape=[sh] * 4, name=name, compiler_params=_cp(40))(slots, w, m, v)


def adam_flat(g, w, m, v, name):
    def body(g_ref, w_ref, m_ref, v_ref, d_ref, m2_ref, v2_ref):
        d, m2, v2 = _adamw(w_ref[...], g_ref[...], m_ref[...], v_ref[...])
        d_ref[...] = d
        m2_ref[...] = m2
        v2_ref[...] = v2

    sh = jax.ShapeDtypeStruct(g.shape, f32)
    return pl.pallas_call(body, out_shape=[sh] * 3, name=name)(g, w, m, v)


def _rows_of(shape):
    n = 1
    for dim in shape:
        n *= dim
    return n, -(-n // (8 * 128)) * 8


def _pack(arrs):
    parts = []
    for a in arrs:
        n, rows = _rows_of(a.shape)
        parts.append(jnp.pad(a.reshape(-1).astype(f32), (0, rows * 128 - n)).reshape(rows, 128))
    return jnp.concatenate(parts, axis=0)


def _unpack(packed, shapes):
    out, row = [], 0
    for sh in shapes:
        n, rows = _rows_of(sh)
        out.append(packed[row:row + rows].reshape(-1)[:n].reshape(sh))
        row += rows
    return out


def _block_diag(gw):
    eye = jnp.eye(8, dtype=gw.dtype)
    return (gw[:, :, None, :] * eye[:, None, :, None]).reshape(SEG, SEG)


def _diag_blocks(dense):
    eye = jnp.eye(8, dtype=dense.dtype)
    return (dense.reshape(8, 64, 8, 64) * eye[:, None, :, None]).sum(axis=2)


def _lane_row(vals, first_lane):
    return jnp.zeros((1, HD), f32).at[0, first_lane:first_lane + NH].set(vals)


def kernel(x, norm_w, w_in, rg_conv_w, rg_conv_b, rg_gate_w, rg_gate_b, rg_lambda, ml_gate_b, ml_norm_w, gd_conv_w, gd_a_log, gd_dt_bias, gd_norm_w, w_out, final_norm_w, loss_target, m_norm_w, m_w_in, m_rg_conv_w, m_rg_conv_b, m_rg_gate_w, m_rg_gate_b, m_rg_lambda, m_ml_gate_b, m_ml_norm_w, m_gd_conv_w, m_gd_a_log, m_gd_dt_bias, m_gd_norm_w, m_w_out, m_final_norm_w, v_norm_w, v_w_in, v_rg_conv_w, v_rg_conv_b, v_rg_gate_w, v_rg_gate_b, v_rg_lambda, v_ml_gate_b, v_ml_norm_w, v_gd_conv_w, v_gd_a_log, v_gd_dt_bias, v_gd_norm_w, v_w_out, v_final_norm_w):
    s = x.shape[1]
    xs = x.reshape(s, D)
    tgt = loss_target.reshape(s, D)
    me = 4 * lax.axis_index("x") + 2 * lax.axis_index("y") + lax.axis_index("c")

    comm = MeshComm(w_in, w_out, [rg_conv_w, rg_gate_b, gd_conv_w])
    rg_conv_full, rg_gb_full, gd_conv_full = comm.small_weights
    loss_part, dx, d_fw, g_small = local_step(
        xs, tgt, comm, rg_conv_full, rg_gb_full, gd_conv_full, norm_w, rg_conv_b, rg_gate_w, rg_lambda,
        ml_gate_b, ml_norm_w, gd_a_log, gd_dt_bias, gd_norm_w, final_norm_w)
    given_w = dict(norm_w=norm_w, rg_conv_w=rg_conv_w, rg_conv_b=rg_conv_b, rg_gate_w=rg_gate_w, rg_gate_b=rg_gate_b,
                   rg_lambda=rg_lambda, ml_gate_b=ml_gate_b, ml_norm_w=ml_norm_w, gd_conv_w=gd_conv_w, gd_a_log=gd_a_log,
                   gd_dt_bias=gd_dt_bias, gd_norm_w=gd_norm_w, final_norm_w=final_norm_w, w_in=w_in, w_out=w_out)
    given_m = dict(norm_w=m_norm_w, rg_conv_w=m_rg_conv_w, rg_conv_b=m_rg_conv_b, rg_gate_w=m_rg_gate_w, rg_gate_b=m_rg_gate_b,
                   rg_lambda=m_rg_lambda, ml_gate_b=m_ml_gate_b, ml_norm_w=m_ml_norm_w, gd_conv_w=m_gd_conv_w,
                   gd_a_log=m_gd_a_log, gd_dt_bias=m_gd_dt_bias, gd_norm_w=m_gd_norm_w, final_norm_w=m_final_norm_w,
                   w_in=m_w_in, w_out=m_w_out)
    given_v = dict(norm_w=v_norm_w, rg_conv_w=v_rg_conv_w, rg_conv_b=v_rg_conv_b, rg_gate_w=v_rg_gate_w, rg_gate_b=v_rg_gate_b,
                   rg_lambda=v_rg_lambda, ml_gate_b=v_ml_gate_b, ml_norm_w=v_ml_norm_w, gd_conv_w=v_gd_conv_w,
                   gd_a_log=v_gd_a_log, gd_dt_bias=v_gd_dt_bias, gd_norm_w=v_gd_norm_w, final_norm_w=v_final_norm_w,
                   w_in=v_w_in, w_out=v_w_out)
    return finish_step(loss_part, dx, d_fw, g_small, comm, s, me, given_w, given_m, given_v)


def _gathered_pieces():
    per = D_IN // N_DEV
    pieces = []
    for lo, hi in ((0, 3584), (3592, 5640), (3584, 3592), (5640, 5648)):
        col = lo
        while col < hi:
            k = col // per
            end = min(hi, (k + 1) * per)
            pieces.append((k, col - k * per, end - k * per))
            col = end
    return pieces


class MeshComm:
    GWI_SPLIT = 256
    WI_SPLIT = 384

    def __init__(self, w_in, w_out, small_shards):
        per = D_IN // N_DEV
        self.wi_sh = [w_in[l].astype(bf16) for l in range(DEPTH)]
        self.wo_sh = [w_out[l].astype(bf16) for l in range(DEPTH)]
        self.wi_land = jax.ShapeDtypeStruct((N_DEV, D, per), bf16)
        self.wo_land = jax.ShapeDtypeStruct((N_DEV, 3 * SEG // N_DEV, D), bf16)
        packed = _pack(small_shards)
        first = Gather2([self.wi_sh[0], self.wo_sh[0], packed],
                        [self.wi_land, self.wo_land, jax.ShapeDtypeStruct((N_DEV,) + packed.shape, f32)],
                        [(0, D), (0, 3 * SEG // N_DEV), (0, packed.shape[0])])
        wi_g, wo_g, sm_g = exchange(first, "gather_first")
        self.wi_g, self.wo_g = {0: wi_g}, {0: wo_g}
        shapes = [a.shape for a in small_shards]
        parts = [_unpack(sm_g[k], shapes) for k in range(N_DEV)]
        self.small_weights = [jnp.concatenate([p[j] for p in parts], axis=-1) for j in range(len(small_shards))]
        self.gwi_land = lax.empty((N_DEV, DEPTH, D, per), bf16)
        self.gwo_land = lax.empty((N_DEV, DEPTH, 3 * SEG // N_DEV, D), bf16)
        self.gwi_slots = {}
        self.gwo_slots = {}

    def weights(self, l):
        cols = [self.wi_g[l][k, :, a:b] for k, a, b in _gathered_pieces()] + [jnp.zeros((D, DP - D_IN), bf16)]
        return jnp.concatenate(cols, axis=1), self.wo_g[l].reshape(3 * SEG, D)

    def fwd_carry(self, l, host):
        if l + 1 >= DEPTH:
            return None
        if host == "mlstm":
            return Gather2([self.wo_sh[l + 1]], [self.wo_land], [(0, 3 * SEG // N_DEV)])
        if host == "inproj":
            return Gather2([self.wi_sh[l + 1]], [self.wi_land], [(0, self.WI_SPLIT)])
        return Gather2([self.wi_sh[l + 1]], [self.wi_g[l + 1]], [(self.WI_SPLIT, D)])

    def fwd_landed(self, l, host, landed):
        (self.wo_g if host == "mlstm" else self.wi_g)[l + 1] = landed[0]

    def own_w_out_grad(self, l, g_wo):
        self.gwo_slots[l] = g_wo.reshape(N_DEV, 3 * SEG // N_DEV, D).astype(bf16)

    def bwd_carry(self, l, host):
        rows_o = 3 * SEG // N_DEV
        srcs, land_of, layer, span = [], [], [], []
        if l + 1 < DEPTH:
            if host == "mlstm":
                srcs += [self.gwo_slots[l + 1], self.gwi_slots[l + 1]]
                land_of, layer, span = [1, 0], [l + 1, l + 1], [(0, rows_o), (0, self.GWI_SPLIT)]
            else:
                srcs, land_of, layer, span = [self.gwi_slots[l + 1]], [0], [l + 1], [(self.GWI_SPLIT, D)]
        if l == 0 and host == "mlstm":
            srcs, land_of, layer, span = srcs + [self.gwo_slots[0]], land_of + [1], layer + [0], span + [(0, rows_o)]
        if not srcs:
            return None
        return Scatter(srcs, [self.gwi_land, self.gwo_land], land_of, layer, span)

    def bwd_landed(self, landed):
        self.gwi_land, self.gwo_land = landed

    def grads_ready(self, l, pieces):
        per = D_IN // N_DEV
        g_wi = jnp.concatenate(pieces, axis=1)
        self.gwi_slots[l] = jnp.stack([g_wi[:, k * per:(k + 1) * per] for k in range(N_DEV)]).astype(bf16)

    def last_carry(self):
        return Scatter([self.gwi_slots[0]], [self.gwi_land, self.gwo_land], [0], [0], [(0, D)])


def local_step(xs, tgt, comm, rg_conv_full, rg_gb_full, gd_conv_full, norm_w, rg_conv_b, rg_gate_w,
               rg_lambda, ml_gate_b, ml_norm_w, gd_a_log, gd_dt_bias, gd_norm_w, final_norm_w):
    acts = []
    for l in range(DEPTH):
        nw = norm_w[l].reshape(1, D)
        w_in_l, w_out_l = comm.weights(l)
        xf = comm.fwd_carry(l, "inproj")
        proj, hn_t, *landed = inproj(xs, nw, w_in_l, xf=xf)
        if xf is not None:
            comm.fwd_landed(l, "inproj", landed)
        rg_p = (rg_conv_full[l], rg_conv_b[l].reshape(1, SEG), _block_diag(rg_gate_w[l, 0]), _block_diag(rg_gate_w[l, 1]),
                rg_gb_full[l], rg_lambda[l].reshape(1, SEG))
        y_rg, hs = rglru_fwd(proj, *rg_p)
        ml_p = (jnp.zeros((1, HD), f32).at[0, 0:2 * NH].set(ml_gate_b[l].reshape(-1)), ml_norm_w[l].reshape(1, SEG))
        xf = comm.fwd_carry(l, "mlstm")
        y_ml, cs, ns, ms, *landed = mlstm_fwd(proj, *ml_p, xf=xf)
        if xf is not None:
            comm.fwd_landed(l, "mlstm", landed)
        gd_p = (gd_conv_full[l], _lane_row(gd_a_log[l], 2 * NH), _lane_row(gd_dt_bias[l], 2 * NH), gd_norm_w[l].reshape(1, HD))
        xf = comm.fwd_carry(l, "gdn")
        y_gd, ss, *landed = gdn_fwd(proj, *gd_p, xf=xf)
        if xf is not None:
            comm.fwd_landed(l, "gdn", landed)
        acts.append((xs, nw, proj, hn_t, w_in_l, w_out_l, rg_p, y_rg, hs, ml_p, y_ml, cs, ns, ms, gd_p, y_gd, ss))
        xs = outproj(xs, y_rg, y_ml, y_gd, w_out_l)

    dx, loss_part, d_fw = head(xs, final_norm_w.reshape(1, D), tgt)

    g_small = {k: [None] * DEPTH for k in ("norm_w", "rg_conv_w", "rg_conv_b", "rg_gate_w", "rg_gate_b", "rg_lambda",
                                           "ml_gate_b", "ml_norm_w", "gd_conv_w", "gd_a_log", "gd_dt_bias", "gd_norm_w")}
    for l in reversed(range(DEPTH)):
        x_l, nw, proj, hn_t, w_in_l, w_out_l, rg_p, y_rg, hs, ml_p, y_ml, cs, ns, ms, gd_p, y_gd, ss = acts[l]
        dy_rg, dy_ml, dy_gd, g_wo = outproj_bwd(dx, y_rg, y_ml, y_gd, w_out_l)
        d_rg, d_cw, d_cb, d_gr, d_gi, d_gb, d_lam = rglru_bwd(proj, hs, dy_rg, *rg_p)
        comm.own_w_out_grad(l, g_wo)
        xf = comm.bwd_carry(l, "mlstm")
        d_ml, d_sm_ml, d_bias, d_mnw, *landed = mlstm_bwd(proj, dy_ml, cs, ns, ms, *ml_p, xf=xf)
        if xf is not None:
            comm.bwd_landed(landed)
        xf = comm.bwd_carry(l, "gdn")
        d_gd, d_sm_gd, d_gcw, d_al, d_dt, d_gnw, *landed = gdn_bwd(proj, dy_gd, ss, *gd_p, xf=xf)
        if xf is not None:
            comm.bwd_landed(landed)
        dx, d_nw = inproj_bwd_x(x_l, nw, w_in_l, dx, d_rg, d_ml, d_gd, d_sm_ml, d_sm_gd)
        gw_rg = wgrad(hn_t, [d_rg], "wgrad_rg")
        gw_ml = wgrad(hn_t, [d_ml], "wgrad_ml")
        gw_gd = wgrad(hn_t, [d_gd], "wgrad_gd")
        gw_sm = wgrad(hn_t, [d_sm_ml, d_sm_gd], "wgrad_small")
        comm.grads_ready(l, [gw_rg, gw_ml, gw_sm[:, 0:2 * NH], gw_gd, gw_sm[:, 2 * NH:4 * NH]])
        g_small["norm_w"][l] = d_nw[0]
        g_small["rg_conv_w"][l] = d_cw
        g_small["rg_conv_b"][l] = d_cb[0]
        g_small["rg_gate_w"][l] = jnp.stack([_diag_blocks(d_gr), _diag_blocks(d_gi)])
        g_small["rg_gate_b"][l] = d_gb
        g_small["rg_lambda"][l] = d_lam[0]
        g_small["ml_gate_b"][l] = d_bias[0, 0:2 * NH].reshape(2, NH)
        g_small["ml_norm_w"][l] = d_mnw[0]
        g_small["gd_conv_w"][l] = d_gcw
        g_small["gd_a_log"][l] = d_al[0, 2 * NH:3 * NH]
        g_small["gd_dt_bias"][l] = d_dt[0, 2 * NH:3 * NH]
        g_small["gd_norm_w"][l] = d_gnw[0]
    return loss_part, dx, d_fw, g_small


def finish_step(loss_part, dx, d_fw, g_small, comm, s, me, given_w, given_m, given_v):
    small_names = ["norm_w", "rg_conv_w", "rg_conv_b", "rg_gate_w", "rg_gate_b", "rg_lambda", "ml_gate_b", "ml_norm_w",
                   "gd_conv_w", "gd_a_log", "gd_dt_bias", "gd_norm_w"]
    small_list = [loss_part[0, 0:1], d_fw[0]] + [jnp.stack(g_small[k]) for k in small_names]
    small_shapes = [a.shape for a in small_list]
    packed = _pack(small_list)
    packed = jnp.pad(packed, ((0, -packed.shape[0] % (8 * N_DEV)), (0, 0)))
    summed, gwi_r, gwo_r = small_allreduce(packed, "last_exchange", xf=comm.last_carry())
    g_all = _unpack(summed, small_shapes)

    g_wi, d_wi, m_wi, v_wi = adam_slots(gwi_r, given_w["w_in"], given_m["w_in"], given_v["w_in"], 256, "adam_w_in")
    g_wo, d_wo, m_wo, v_wo = adam_slots(gwo_r, given_w["w_out"], given_m["w_out"], given_v["w_out"], 192, "adam_w_out")
    loss = g_all[0][0]
    grads = {"final_norm_w": g_all[1]}
    for k, g in zip(small_names, g_all[2:]):
        grads[k] = g
    for k, width in (("rg_conv_w", 64), ("rg_gate_b", 64), ("gd_conv_w", 192)):
        grads[k] = lax.dynamic_slice_in_dim(grads[k], me * width, width, axis=2)
    names = small_names + ["final_norm_w"]
    shapes = [given_w[k].shape for k in names]
    d_p, m_p, v_p = adam_flat(_pack([grads[k] for k in names]), _pack([given_w[k] for k in names]),
                              _pack([given_m[k] for k in names]), _pack([given_v[k] for k in names]), "adam_small")
    deltas = dict(zip(names, _unpack(d_p, shapes)))
    new_m = dict(zip(names, _unpack(m_p, shapes)))
    new_v = dict(zip(names, _unpack(v_p, shapes)))
    grads["w_in"], deltas["w_in"], new_m["w_in"], new_v["w_in"] = g_wi, d_wi, m_wi, v_wi
    grads["w_out"], deltas["w_out"], new_m["w_out"], new_v["w_out"] = g_wo, d_wo, m_wo, v_wo

    order = ["norm_w", "w_in", "rg_conv_w", "rg_conv_b", "rg_gate_w", "rg_gate_b", "rg_lambda", "ml_gate_b", "ml_norm_w",
             "gd_conv_w", "gd_a_log", "gd_dt_bias", "gd_norm_w", "w_out", "final_norm_w"]
    return (loss, dx.reshape(1, s, D), *[grads[k] for k in order], *[deltas[k] for k in order],
            *[new_m[k] for k in order], *[new_v[k] for k in order])
```

```python
import functools
from typing import NamedTuple

import jax
import jax.numpy as jnp
from jax import lax
from jax.experimental import pallas as pl
from jax.experimental.pallas import tpu as pltpu

f32 = jnp.float32
bf16 = jnp.bfloat16
MESH = pl.DeviceIdType.MESH

N_DEV = 8
D = 1024
DEPTH = 4
EPS = 1e-6
SEG = 512
HD = 128
NH = 4
LC = 64
RUN = 4
LR = RUN * LC
D_IN = 5648
DP = 5760
SMALL0 = 5632
RG_TILE = 256
RG_C = 8.0

ADAM_LR = 0.001
ADAM_B1 = 0.9
ADAM_B2 = 0.999
ADAM_EPS = 1e-08
ADAM_WD = 0.01
ADAM_STEP = 10


def _cp(vmem_mb):
    return pltpu.CompilerParams(vmem_limit_bytes=vmem_mb * 2 ** 20)


def _dot(a, b, ca, cb):
    return lax.dot_general(a.astype(bf16), b.astype(bf16), (((ca,), (cb,)), ((), ())), preferred_element_type=f32)


@jax.custom_vjp
def mm_nn(a, b):
    return _dot(a, b, 1, 0)


@jax.custom_vjp
def mm_nt(a, b):
    return _dot(a, b, 1, 1)


@jax.custom_vjp
def mm_tn(a, b):
    return _dot(a, b, 0, 0)


mm_nn.defvjp(lambda a, b: (mm_nn(a, b), (a, b)), lambda r, g: (mm_nt(g, r[1]), mm_tn(r[0], g)))
mm_nt.defvjp(lambda a, b: (mm_nt(a, b), (a, b)), lambda r, g: (mm_nn(g, r[1]), mm_tn(g, r[0])))
mm_tn.defvjp(lambda a, b: (mm_tn(a, b), (a, b)), lambda r, g: (mm_nt(r[1], g), mm_nn(r[0], g)))


def _split(x):
    hi = x.astype(bf16)
    return hi, (x - hi.astype(f32)).astype(bf16)


def dot3(a, b):
    ah, al = _split(a)
    bh, bl = _split(b)
    d = functools.partial(jnp.dot, preferred_element_type=f32)
    return d(ah, bh) + (d(al, bh) + d(ah, bl))


def _tri_sum(x, reverse):
    n = x.shape[0]
    r = lax.broadcasted_iota(jnp.int32, (n, 3 * n), 0)
    c = lax.broadcasted_iota(jnp.int32, (n, 3 * n), 1) & (n - 1)
    ones = ((c >= r) if reverse else (c <= r)).astype(bf16)
    hi = x.astype(bf16)
    rest = x - hi.astype(f32)
    mid = rest.astype(bf16)
    lo = (rest - mid.astype(f32)).astype(bf16)
    return jnp.dot(ones, jnp.concatenate([hi, mid, lo], axis=0), preferred_element_type=f32)


@jax.custom_vjp
def cumsum_rows(x):
    return _tri_sum(x, False)


@jax.custom_vjp
def rev_cumsum_rows(x):
    return _tri_sum(x, True)


cumsum_rows.defvjp(lambda x: (cumsum_rows(x), None), lambda _, g: (rev_cumsum_rows(g),))
rev_cumsum_rows.defvjp(lambda x: (rev_cumsum_rows(x), None), lambda _, g: (cumsum_rows(g),))


def _tri(n, strict=False):
    r = lax.broadcasted_iota(jnp.int32, (n, n), 0)
    c = lax.broadcasted_iota(jnp.int32, (n, n), 1)
    return (r > c) if strict else (r >= c)


def _lane_col(v, j):
    lane = lax.broadcasted_iota(jnp.int32, v.shape, 1)
    return jnp.sum(jnp.where(lane == j, v, 0.0), axis=1, keepdims=True)


def _rows_from(pieces, rows, width):
    ridx = lax.broadcasted_iota(jnp.int32, (rows, width), 0)
    out = jnp.zeros((rows, width), f32)
    for h, p in enumerate(pieces):
        out = out + jnp.where(ridx == h, jnp.broadcast_to(p, (rows, width)), 0.0)
    return out


def causal_conv(halo8, x, w4):
    t = x.shape[0]
    xe = jnp.concatenate([halo8, x], axis=0)
    y = xe[5:5 + t] * w4[0:1]
    for k in range(1, 4):
        y = y + xe[5 + k:5 + k + t] * w4[k:k + 1]
    return y


def ml_chunk(q, k, v, o_pre, z, small, bias_row, norm_w, C, n, m):
    n_ch = q.shape[0] // LC
    lane = lax.broadcasted_iota(jnp.int32, small.shape, 1)
    pre = small + bias_row
    lg = jnp.where(lane < NH, pre, jnp.where(lane < 2 * NH, jax.nn.log_sigmoid(pre), 0.0))
    rows = [slice(c * LC, (c + 1) * LC) for c in range(n_ch)]
    lgs = [lg[r] for r in rows]
    bcs = [cumsum_rows(x) for x in lgs]
    lg_ts = [x.T for x in lgs]
    bc_ts = [x.T for x in bcs]
    causal = _tri(LC)
    pairs = [(c, h) for c in range(n_ch) for h in range(NH)]
    idx = range(len(pairs))
    sls = [slice(h * HD, (h + 1) * HD) for h in range(NH)]
    qs = [q[rows[c], sls[h]] * (HD ** -0.5) for c, h in pairs]
    ks = [k[rows[c], sls[h]] for c, h in pairs]
    vs = [v[rows[c], sls[h]] for c, h in pairs]
    li_cols = [_lane_col(lgs[c], h) for c, h in pairs]
    b_cols = [_lane_col(bcs[c], NH + h) for c, h in pairs]
    dms = [jnp.where(causal, b_cols[i] - bc_ts[c][NH + h:NH + h + 1, :] + lg_ts[c][h:h + 1, :], -jnp.inf)
           for i, (c, h) in enumerate(pairs)]
    dm_maxs = [jnp.max(dms[i], axis=-1, keepdims=True) for i in idx]
    gs = [b_cols[i][LC - 1:LC, :] for i in idx]
    wss = [gs[i] - b_cols[i] + li_cols[i] for i in idx]
    ws_maxs = [jnp.max(wss[i], axis=0, keepdims=True) for i in idx]
    qks = [mm_nt(qs[i], ks[i]) for i in idx]
    m_in = [None] * len(pairs)
    m_out = [None] * len(pairs)
    for h in range(NH):
        cur = m[h:h + 1, 0:1]
        for c in range(n_ch):
            i = c * NH + h
            m_in[i] = cur
            cur = jnp.maximum(gs[i] + cur, ws_maxs[i])
            m_out[i] = cur
    m_inters = [b_cols[i] + m_in[i] for i in idx]
    m_ts = [jnp.maximum(m_inters[i], dm_maxs[i]) for i in idx]
    ss = [qks[i] * jnp.exp(dms[i] - m_ts[i]) for i in idx]
    scs = [jnp.exp(m_inters[i] - m_ts[i]) for i in idx]
    decs = [jnp.exp(gs[i] + m_in[i] - m_out[i]) for i in idx]
    kws = [jnp.exp(wss[i] - m_out[i]) * ks[i] for i in idx]
    c_adds = [mm_tn(kws[i], vs[i]) for i in idx]
    n_adds = [jnp.sum(kws[i], axis=0, keepdims=True) for i in idx]
    svs = [mm_nn(ss[i], vs[i]) for i in idx]
    s_sums = [jnp.sum(ss[i], axis=-1, keepdims=True) for i in idx]
    c_hs = [C[sl, :] for sl in sls]
    n_hs = [n[h:h + 1, :] for h in range(NH)]
    hhs = [None] * len(pairs)
    for c in range(n_ch):
        for h in range(NH):
            i = c * NH + h
            num = svs[i] + scs[i] * mm_nn(qs[i], c_hs[h])
            den = s_sums[i] + scs[i] * jnp.sum(qs[i] * n_hs[h], axis=-1, keepdims=True)
            hhs[i] = num / jnp.maximum(jnp.abs(den), jnp.exp(-m_ts[i]))
        c_hs = [decs[c * NH + h] * c_hs[h] + c_adds[c * NH + h] for h in range(NH)]
        n_hs = [decs[c * NH + h] * n_hs[h] + n_adds[c * NH + h] for h in range(NH)]
    ys = [hhs[i] * lax.rsqrt(jnp.mean(hhs[i] * hhs[i], axis=-1, keepdims=True) + EPS) * norm_w[:, sls[h]]
          * jax.nn.sigmoid(o_pre[rows[c], sls[h]]) * jax.nn.silu(z[rows[c], sls[h]]) for i, (c, h) in enumerate(pairs)]
    y = jnp.concatenate([jnp.concatenate(ys[c * NH:(c + 1) * NH], axis=1) for c in range(n_ch)], axis=0)
    last = (n_ch - 1) * NH
    m_rows = [jnp.broadcast_to(m_out[last + h], (1, HD)) for h in range(NH)]
    return y, jnp.concatenate(c_hs, axis=0), _rows_from(n_hs, 8, HD), _rows_from(m_rows, 8, HD)


@jax.custom_vjp
def _unit_lower_inverses(mats):
    eye = (lax.broadcasted_iota(jnp.int32, (LC, LC), 0) == lax.broadcasted_iota(jnp.int32, (LC, LC), 1)).astype(f32)
    ps = [-m for m in mats]
    ts = [eye + p for p in ps]
    for _ in range(5):
        ps = [dot3(p, p) for p in ps]
        ts = [t + dot3(t, p) for t, p in zip(ts, ps)]
    return tuple(ts)


def _unit_lower_inverses_fwd(mats):
    ts = _unit_lower_inverses(mats)
    return ts, ts


def _unit_lower_inverses_bwd(ts, gs):
    tts = [t.T for t in ts]
    mid = [dot3(tt, g) for tt, g in zip(tts, gs)]
    return (tuple(-dot3(m, tt) for m, tt in zip(mid, tts)),)


_unit_lower_inverses.defvjp(_unit_lower_inverses_fwd, _unit_lower_inverses_bwd)


@jax.custom_vjp
def _known_inverses(mats, ts):
    return tuple(ts)


_known_inverses.defvjp(lambda mats, ts: (tuple(ts), tuple(ts)),
                       lambda ts, gs: (_unit_lower_inverses_bwd(ts, gs)[0], tuple(jnp.zeros_like(t) for t in ts)))


def gd_chunk(qh8, q, kh8, k, vh8, v, z, small, conv_w, alog_row, dt_row, norm_w, st, known_t=None):
    n_ch = q.shape[0] // LC
    lane = lax.broadcasted_iota(jnp.int32, small.shape, 1)
    is_a = (lane >= 2 * NH) & (lane < 3 * NH)
    g_all = jnp.where(is_a, -jnp.exp(alog_row) * jax.nn.softplus(small + dt_row), 0.0)
    beta_all = jax.nn.sigmoid(small)
    qc = jax.nn.silu(causal_conv(qh8, q, conv_w[:, 0:SEG]))
    kc = jax.nn.silu(causal_conv(kh8, k, conv_w[:, SEG:2 * SEG]))
    vc = jax.nn.silu(causal_conv(vh8, v, conv_w[:, 2 * SEG:3 * SEG]))
    incl = _tri(LC)
    strict = _tri(LC, strict=True)
    rows = [slice(c * LC, (c + 1) * LC) for c in range(n_ch)]
    gc_alls = [cumsum_rows(g_all[r]) for r in rows]
    gc_ts = [g.T for g in gc_alls]
    pairs = [(c, h) for c in range(n_ch) for h in range(NH)]
    idx = range(len(pairs))
    sls = [slice(h * HD, (h + 1) * HD) for h in range(NH)]
    qs = [qc[rows[c], sls[h]] for c, h in pairs]
    ks = [kc[rows[c], sls[h]] for c, h in pairs]
    vs = [vc[rows[c], sls[h]] for c, h in pairs]
    qs = [x * lax.rsqrt(jnp.sum(x * x, axis=-1, keepdims=True) + EPS) * (HD ** -0.5) for x in qs]
    ks = [x * lax.rsqrt(jnp.sum(x * x, axis=-1, keepdims=True) + EPS) for x in ks]
    betas = [_lane_col(beta_all[rows[c]], 3 * NH + h) for c, h in pairs]
    gcs = [_lane_col(gc_alls[c], 2 * NH + h) for c, h in pairs]
    gams = [jnp.exp(jnp.where(incl, gcs[i] - gc_ts[c][2 * NH + h:2 * NH + h + 1, :], -jnp.inf))
            for i, (c, h) in enumerate(pairs)]
    kbs = [ks[i] * betas[i] for i in idx]
    mats = tuple(jnp.where(strict, mm_nt(kbs[i], ks[i]) * gams[i], 0.0) for i in idx)
    aqks = [mm_nt(qs[i], ks[i]) * gams[i] for i in idx]
    if known_t is None:
        t_invs = _unit_lower_inverses(mats)
    else:
        t_invs = _known_inverses(mats, tuple(known_t[i] for i in idx))
    us = [mm_nn(t_invs[i], vs[i] * betas[i]) for i in idx]
    ws = [mm_nn(t_invs[i], kbs[i] * jnp.exp(gcs[i])) for i in idx]
    g_lasts = [gcs[i][LC - 1:LC, :] for i in idx]
    q_decs = [qs[i] * jnp.exp(gcs[i]) for i in idx]
    k_decs = [ks[i] * jnp.exp(g_lasts[i] - gcs[i]) for i in idx]
    e_lasts = [jnp.exp(g_lasts[i]) for i in idx]
    s_hs = [st[sl, :] for sl in sls]
    os_ = [None] * len(pairs)
    for c in range(n_ch):
        ids = [c * NH + h for h in range(NH)]
        v_news = [us[i] - mm_nn(ws[i], s_hs[h]) for h, i in enumerate(ids)]
        for h, i in enumerate(ids):
            os_[i] = mm_nn(q_decs[i], s_hs[h]) + mm_nn(aqks[i], v_news[h])
        s_hs = [s_hs[h] * e_lasts[i] + mm_tn(k_decs[i], v_news[h]) for h, i in enumerate(ids)]
    ys = [os_[i] * lax.rsqrt(jnp.mean(os_[i] * os_[i], axis=-1, keepdims=True) + EPS) * norm_w
          * jax.nn.silu(z[rows[c], sls[h]]) for i, (c, h) in enumerate(pairs)]
    y = jnp.concatenate([jnp.concatenate(ys[c * NH:(c + 1) * NH], axis=1) for c in range(n_ch)], axis=0)
    if known_t is None:
        return y, jnp.concatenate(s_hs, axis=0), jnp.stack(t_invs)
    return y, jnp.concatenate(s_hs, axis=0)


def rg_pre(xh8, x, conv_w, conv_b, gw_r, gw_i, gate_b, lam):
    xc = causal_conv(xh8, x, conv_w) + conv_b
    r = jax.nn.sigmoid(mm_nn(xc, gw_r) + gate_b[0:1])
    i = jax.nn.sigmoid(mm_nn(xc, gw_i) + gate_b[1:2])
    log_a = -RG_C * r * jax.nn.softplus(-lam)
    a = jnp.exp(log_a)
    th = jnp.tanh(log_a)
    one_minus_a2 = -2.0 * th / (1.0 - th)
    b = jnp.sqrt(one_minus_a2) * (i * xc)
    return a, b


def _scan_rows(a_ref, b_ref, out_ref, h0, n_rows, reverse):
    n_groups = n_rows // 8
    width = a_ref.shape[1]
    row = lax.broadcasted_iota(jnp.int32, (8, width), 0)

    def body(j, h):
        g = (n_groups - 1 - j) if reverse else j
        r0 = pl.multiple_of(g * 8, 8)
        av = a_ref[pl.ds(r0, 8), :]
        bv = b_ref[pl.ds(r0, 8), :]
        for d in (1, 2, 4):
            sh = (8 - d) if reverse else d
            a_s = pltpu.roll(av, sh, 0)
            b_s = pltpu.roll(bv, sh, 0)
            valid = (row < 8 - d) if reverse else (row >= d)
            bv = jnp.where(valid, av * b_s + bv, bv)
            av = jnp.where(valid, av * a_s, av)
        hv = av * h + bv
        out_ref[pl.ds(r0, 8), :] = hv
        return hv[0:1, :] if reverse else hv[7:8, :]

    return lax.fori_loop(0, n_groups, body, h0)


def _norm_rows(xv):
    r = lax.rsqrt(jnp.mean(xv * xv, axis=-1, keepdims=True) + EPS)
    return xv * r, r


def inproj(x, nw, w, xf=None):
    s, ts = x.shape[0], 256
    nt = s // ts
    n_in, n_out = 3, 2
    x_ops, x_in_specs, x_out_shapes, x_out_specs, aliases, x_sems, n_xin, n_land = _carried(xf, n_in, n_out)

    def body(*refs):
        (x_ref, nw_ref, w_ref), x_refs, (o_ref, ht_ref), land_refs, _, sem_refs = _cut_refs(refs, n_in, n_xin, n_out, n_land, 0)
        i = pl.program_id(0)
        _carry_open(xf, i, nt, x_refs, land_refs, sem_refs)
        xn, _ = _norm_rows(x_ref[...])
        hn = xn * nw_ref[...]
        o_ref[...] = jnp.dot(hn.astype(bf16), w_ref[...], preferred_element_type=f32)
        ht_ref[...] = hn.T.astype(bf16)
        _carry_close(xf, i, nt, x_refs, land_refs, sem_refs)

    return pl.pallas_call(
        body, grid=(nt,),
        in_specs=[pl.BlockSpec((ts, D), lambda i: (i, 0)), pl.BlockSpec((1, D), lambda i: (0, 0)),
                  pl.BlockSpec((D, DP), lambda i: (0, 0))] + x_in_specs,
        out_specs=[pl.BlockSpec((ts, DP), lambda i: (i, 0)), pl.BlockSpec((D, ts), lambda i: (0, i))] + x_out_specs,
        out_shape=[jax.ShapeDtypeStruct((s, DP), f32), jax.ShapeDtypeStruct((D, s), bf16)] + x_out_shapes,
        scratch_shapes=x_sems, input_output_aliases=aliases,
        name="inproj" if xf is None else "inproj_carrying", compiler_params=_cp(56))(x, nw, w, *x_ops)


def outproj(x, yr, ym, yg, wo):
    s, ts = x.shape[0], 256

    def body(x_ref, yr_ref, ym_ref, yg_ref, wo_ref, o_ref):
        acc = x_ref[...]
        for j, y_ref in enumerate((yr_ref, ym_ref, yg_ref)):
            acc = acc + jnp.dot(y_ref[...].astype(bf16), wo_ref[j * SEG:(j + 1) * SEG, :], preferred_element_type=f32)
        o_ref[...] = acc

    yspec = pl.BlockSpec((ts, SEG), lambda i: (i, 0))
    return pl.pallas_call(
        body, grid=(s // ts,),
        in_specs=[pl.BlockSpec((ts, D), lambda i: (i, 0)), yspec, yspec, yspec,
                  pl.BlockSpec((3 * SEG, D), lambda i: (0, 0))],
        out_specs=pl.BlockSpec((ts, D), lambda i: (i, 0)),
        out_shape=jax.ShapeDtypeStruct((s, D), f32), name="outproj", compiler_params=_cp(40))(x, yr, ym, yg, wo)


def head(x, fw, tgt):
    s, ts = x.shape[0], 256

    def body(x_ref, fw_ref, t_ref, dx_ref, loss_ref, dfw_ref):
        @pl.when(pl.program_id(0) == 0)
        def _():
            loss_ref[...] = jnp.zeros_like(loss_ref)
            dfw_ref[...] = jnp.zeros_like(dfw_ref)

        xn, r = _norm_rows(x_ref[...])
        fw_v = fw_ref[...]
        err = xn * fw_v - t_ref[...]
        loss_ref[...] += 0.5 * jnp.sum(jnp.mean(err * err, axis=-1, keepdims=True))
        dy = err * (1.0 / D)
        dfw_ref[0:1, :] += jnp.sum(dy * xn, axis=0, keepdims=True)
        dxn = dy * fw_v
        dx_ref[...] = r * (dxn - xn * jnp.mean(dxn * xn, axis=-1, keepdims=True))

    tile = pl.BlockSpec((ts, D), lambda i: (i, 0))
    return pl.pallas_call(
        body, grid=(s // ts,),
        in_specs=[tile, pl.BlockSpec((1, D), lambda i: (0, 0)), tile],
        out_specs=[tile, pl.BlockSpec((8, 128), lambda i: (0, 0)), pl.BlockSpec((8, D), lambda i: (0, 0))],
        out_shape=[jax.ShapeDtypeStruct((s, D), f32), jax.ShapeDtypeStruct((8, 128), f32),
                   jax.ShapeDtypeStruct((8, D), f32)], name="head")(x, fw, tgt)


def outproj_bwd(dx, yr, ym, yg, wo):
    s, ts = dx.shape[0], 256

    def body(dx_ref, yr_ref, ym_ref, yg_ref, wo_ref, dyr_ref, dym_ref, dyg_ref, gwo_ref):
        @pl.when(pl.program_id(0) == 0)
        def _():
            gwo_ref[...] = jnp.zeros_like(gwo_ref)

        dxb = dx_ref[...].astype(bf16)
        for j, (y_ref, dy_ref) in enumerate(((yr_ref, dyr_ref), (ym_ref, dym_ref), (yg_ref, dyg_ref))):
            rows = slice(j * SEG, (j + 1) * SEG)
            dy_ref[...] = lax.dot_general(dxb, wo_ref[rows, :], (((1,), (1,)), ((), ())), preferred_element_type=f32)
            gwo_ref[rows, :] += lax.dot_general(y_ref[...].astype(bf16), dxb, (((0,), (0,)), ((), ())),
                                                preferred_element_type=f32)

    yspec = pl.BlockSpec((ts, SEG), lambda i: (i, 0))
    wspec = pl.BlockSpec((3 * SEG, D), lambda i: (0, 0))
    ysh = jax.ShapeDtypeStruct((s, SEG), f32)
    return pl.pallas_call(
        body, grid=(s // ts,),
        in_specs=[pl.BlockSpec((ts, D), lambda i: (i, 0)), yspec, yspec, yspec, wspec],
        out_specs=[yspec, yspec, yspec, wspec],
        out_shape=[ysh, ysh, ysh, jax.ShapeDtypeStruct((3 * SEG, D), f32)],
        name="outproj_bwd", compiler_params=_cp(48))(dx, yr, ym, yg, wo)


def inproj_bwd_x(x, nw, w, dxo, d_rg, d_ml, d_gd, d_sa, d_sb):
    s, ts = x.shape[0], 256
    widths = (d_rg.shape[1], d_ml.shape[1], d_gd.shape[1], HD)

    def body(x_ref, nw_ref, w_ref, dxo_ref, rg_ref, ml_ref, gd_ref, sa_ref, sb_ref, dx_ref, dnw_ref):
        @pl.when(pl.program_id(0) == 0)
        def _():
            dnw_ref[...] = jnp.zeros_like(dnw_ref)

        xn, r = _norm_rows(x_ref[...])
        pieces = (rg_ref[...], ml_ref[...], gd_ref[...], sa_ref[...] + sb_ref[...])
        dhn = jnp.zeros((ts, D), f32)
        c0 = 0
        for piece, wd in zip(pieces, widths):
            dhn = dhn + lax.dot_general(piece.astype(bf16), w_ref[:, c0:c0 + wd], (((1,), (1,)), ((), ())),
                                        preferred_element_type=f32)
            c0 += wd
        dnw_ref[0:1, :] += jnp.sum(dhn * xn, axis=0, keepdims=True)
        dxn = dhn * nw_ref[...]
        dx_ref[...] = dxo_ref[...] + r * (dxn - xn * jnp.mean(dxn * xn, axis=-1, keepdims=True))

    tile = pl.BlockSpec((ts, D), lambda i: (i, 0))
    return pl.pallas_call(
        body, grid=(s // ts,),
        in_specs=[tile, pl.BlockSpec((1, D), lambda i: (0, 0)), pl.BlockSpec((D, DP), lambda i: (0, 0)), tile]
        + [pl.BlockSpec((ts, wd), lambda i: (i, 0)) for wd in widths] + [pl.BlockSpec((ts, HD), lambda i: (i, 0))],
        out_specs=[tile, pl.BlockSpec((8, D), lambda i: (0, 0))],
        out_shape=[jax.ShapeDtypeStruct((s, D), f32), jax.ShapeDtypeStruct((8, D), f32)],
        name="inproj_bwd_x", compiler_params=_cp(56))(x, nw, w, dxo, d_rg, d_ml, d_gd, d_sa, d_sb)


def wgrad(hn_t, dps, name):
    s = hn_t.shape[1]
    c = dps[0].shape[1]
    ct = min(c, SEG)
    n_dp = len(dps)

    def body(*refs):
        ht_ref = refs[0]
        dp_refs = refs[1:1 + n_dp]
        o_ref = refs[1 + n_dp]
        dp = dp_refs[0][...]
        for extra in dp_refs[1:]:
            dp = dp + extra[...]
        o_ref[...] = jnp.dot(ht_ref[...], dp.astype(bf16), preferred_element_type=f32)

    return pl.pallas_call(
        body, grid=(c // ct,),
        in_specs=[pl.BlockSpec((D, s), lambda j: (0, 0))] + [pl.BlockSpec((s, ct), lambda j: (0, j)) for _ in dps],
        out_specs=pl.BlockSpec((D, ct), lambda j: (0, j)),
        out_shape=jax.ShapeDtypeStruct((D, c), f32), name=name, compiler_params=_cp(40))(hn_t, *dps)


def _seg_spec(rows, seg, n_tiles=None):
    if n_tiles is None:
        return pl.BlockSpec((rows, SEG), lambda i: (i, seg))
    return pl.BlockSpec((rows, SEG), lambda i: (n_tiles - 1 - i, seg))


def _halo_spec(rows, seg, n_tiles=None):
    per = rows // 8
    if n_tiles is None:
        return pl.BlockSpec((8, SEG), lambda i: (jnp.maximum(i * per - 1, 0), seg))
    return pl.BlockSpec((8, SEG), lambda i: (jnp.maximum((n_tiles - 1 - i) * per - 1, 0), seg))


def _const_spec(shape):
    return pl.BlockSpec(shape, lambda i: tuple(0 for _ in shape))


def rglru_fwd(proj, conv_w, conv_b, gw_r, gw_i, gate_b, lam):
    s = proj.shape[0]
    tr = RG_TILE

    def body(xh_ref, x_ref, z_ref, cw_ref, cb_ref, gr_ref, gi_ref, gb_ref, lam_ref, y_ref, h_ref, a_s, b_s, hc):
        first = pl.program_id(0) == 0

        @pl.when(first)
        def _():
            hc[...] = jnp.zeros_like(hc)

        xh = jnp.where(first, 0.0, xh_ref[...])
        a, b = rg_pre(xh, x_ref[...], cw_ref[...], cb_ref[...], gr_ref[...], gi_ref[...], gb_ref[...], lam_ref[...])
        a_s[...] = a
        b_s[...] = b
        hc[0:1, :] = _scan_rows(a_s, b_s, h_ref, hc[0:1, :], tr, False)
        y_ref[...] = (h_ref[...] * jax.nn.silu(z_ref[...])).astype(bf16)

    out = pl.BlockSpec((tr, SEG), lambda i: (i, 0))
    return pl.pallas_call(
        body, grid=(s // tr,),
        in_specs=[_halo_spec(tr, 0), _seg_spec(tr, 0), _seg_spec(tr, 1), _const_spec((4, SEG)), _const_spec((1, SEG)),
                  _const_spec((SEG, SEG)), _const_spec((SEG, SEG)), _const_spec((2, SEG)), _const_spec((1, SEG))],
        out_specs=[out, out],
        out_shape=[jax.ShapeDtypeStruct((s, SEG), bf16), jax.ShapeDtypeStruct((s, SEG), f32)],
        scratch_shapes=[pltpu.VMEM((tr, SEG), f32), pltpu.VMEM((tr, SEG), f32), pltpu.VMEM((8, SEG), f32)],
        name="rglru_fwd", compiler_params=_cp(40))(proj, proj, proj, conv_w, conv_b, gw_r, gw_i, gate_b, lam)


def rglru_bwd(proj, hs, dy, conv_w, conv_b, gw_r, gw_i, gate_b, lam):
    s = proj.shape[0]
    tr = RG_TILE
    nt = s // tr

    def body(xh_ref, x_ref, z_ref, hh_ref, h_ref, dy_ref, cw_ref, cb_ref, gr_ref, gi_ref, gb_ref, lam_ref,
             dp_ref, dcw_ref, dcb_ref, dgr_ref, dgi_ref, dgb_ref, dlam_ref,
             an_s, g_s, dh_s, a_first, dh_first, dhalo):
        i = pl.program_id(0)
        first_tile = i == nt - 1

        @pl.when(i == 0)
        def _():
            for ref in (dcw_ref, dcb_ref, dgr_ref, dgi_ref, dgb_ref, dlam_ref, a_first, dh_first, dhalo):
                ref[...] = jnp.zeros_like(ref)

        xh = jnp.where(first_tile, 0.0, xh_ref[...])
        params = (cw_ref[...], cb_ref[...], gr_ref[...], gi_ref[...], gb_ref[...], lam_ref[...])
        (a, _), vjp = jax.vjp(rg_pre, xh, x_ref[...], *params)
        zv = z_ref[...]
        hv = h_ref[...]
        dyv = dy_ref[...]
        sig = jax.nn.sigmoid(zv)
        g_s[...] = dyv * (zv * sig)
        dp_ref[:, SEG:2 * SEG] = (dyv * hv * (sig * (1.0 + zv * (1.0 - sig)))).astype(bf16)
        ridx = lax.broadcasted_iota(jnp.int32, (tr, SEG), 0)
        an_s[...] = jnp.where(ridx == tr - 1, jnp.broadcast_to(a_first[0:1, :], (tr, SEG)), pltpu.roll(a, tr - 1, 0))
        _scan_rows(an_s, g_s, dh_s, dh_first[0:1, :], tr, True)
        dh = dh_s[...]
        h_prev_last = jnp.where(first_tile, 0.0, hh_ref[...])[7:8, :]
        h_prev = pltpu.roll(hv, 1, 0)
        h_prev = jnp.where(ridx == 0, jnp.broadcast_to(h_prev_last, (tr, SEG)), h_prev)
        dxh, dx, dcw, dcb, dgr, dgi, dgb, dlam = vjp((dh * h_prev, dh))
        dx = dx + jnp.concatenate([jnp.zeros((tr - 8, SEG), f32), dhalo[...]], axis=0)
        dp_ref[:, 0:SEG] = dx.astype(bf16)
        dhalo[...] = dxh
        a_first[0:1, :] = a[0:1, :]
        dh_first[0:1, :] = dh[0:1, :]
        dcw_ref[...] += dcw
        dcb_ref[...] += dcb
        dgr_ref[...] += dgr
        dgi_ref[...] += dgi
        dgb_ref[...] += dgb
        dlam_ref[...] += dlam

    pspecs = [_const_spec((4, SEG)), _const_spec((1, SEG)), _const_spec((SEG, SEG)), _const_spec((SEG, SEG)),
              _const_spec((2, SEG)), _const_spec((1, SEG))]
    pshapes = [jax.ShapeDtypeStruct(sh, f32) for sh in ((4, SEG), (1, SEG), (SEG, SEG), (SEG, SEG), (2, SEG), (1, SEG))]
    tile = pl.BlockSpec((tr, SEG), lambda i: (nt - 1 - i, 0))
    return pl.pallas_call(
        body, grid=(nt,),
        in_specs=[_halo_spec(tr, 0, nt), _seg_spec(tr, 0, nt), _seg_spec(tr, 1, nt),
                  pl.BlockSpec((8, SEG), lambda i: (jnp.maximum((nt - 1 - i) * (tr // 8) - 1, 0), 0)), tile, tile] + pspecs,
        out_specs=[pl.BlockSpec((tr, 2 * SEG), lambda i: (nt - 1 - i, 0))] + pspecs,
        out_shape=[jax.ShapeDtypeStruct((s, 2 * SEG), bf16)] + pshapes,
        scratch_shapes=[pltpu.VMEM((tr, SEG), f32), pltpu.VMEM((tr, SEG), f32), pltpu.VMEM((tr, SEG), f32),
                        pltpu.VMEM((8, SEG), f32), pltpu.VMEM((8, SEG), f32), pltpu.VMEM((8, SEG), f32)],
        name="rglru_bwd", compiler_params=_cp(48))(proj, proj, proj, hs, hs, dy, conv_w, conv_b, gw_r, gw_i, gate_b, lam)


ML_SEGS = (2, 3, 4, 5, 6)
SMALL_BLK = SMALL0 // HD


def _cut_refs(refs, n_in, n_xin, n_out, n_land, n_scratch):
    bounds = [0, n_in, n_in + n_xin, n_in + n_xin + n_out, n_in + n_xin + n_out + n_land,
              n_in + n_xin + n_out + n_land + n_scratch, len(refs)]
    return [refs[a:b] for a, b in zip(bounds[:-1], bounds[1:])]


def mlstm_fwd(proj, bias_row, norm_w, xf=None):
    s = proj.shape[0]
    nc = s // LR
    n_in, n_out = 8, 4
    x_ops, x_in_specs, x_out_shapes, x_out_specs, aliases, x_sems, n_xin, n_land = _carried(xf, n_in, n_out)

    def body(*refs):
        ins, x_refs, outs, land_refs, scratch, sem_refs = _cut_refs(refs, n_in, n_xin, n_out, n_land, 3)
        q_ref, k_ref, v_ref, o_ref, z_ref, sm_ref, b_ref, nw_ref = ins
        y_ref, cs_ref, ns_ref, ms_ref = outs
        c_s, n_s, m_s = scratch
        i = pl.program_id(0)
        _carry_open(xf, i, nc, x_refs, land_refs, sem_refs)

        @pl.when(i == 0)
        def _():
            c_s[...] = jnp.zeros_like(c_s)
            n_s[...] = jnp.zeros_like(n_s)
            m_s[...] = jnp.zeros_like(m_s)

        cs_ref[0] = c_s[...]
        ns_ref[0] = n_s[...]
        ms_ref[0] = m_s[...]
        y, c2, n2, m2 = ml_chunk(q_ref[...], k_ref[...], v_ref[...], o_ref[...], z_ref[...], sm_ref[...],
                                 b_ref[...], nw_ref[...], c_s[...], n_s[...], m_s[...])
        y_ref[...] = y.astype(bf16)
        c_s[...] = c2
        n_s[...] = n2
        m_s[...] = m2
        _carry_close(xf, i, nc, x_refs, land_refs, sem_refs)

    return pl.pallas_call(
        body, grid=(nc,),
        in_specs=[_seg_spec(LR, sg) for sg in ML_SEGS]
        + [pl.BlockSpec((LR, HD), lambda i: (i, SMALL_BLK)), _const_spec((1, HD)), _const_spec((1, SEG))] + x_in_specs,
        out_specs=[pl.BlockSpec((LR, SEG), lambda i: (i, 0)), pl.BlockSpec((1, SEG, HD), lambda i: (i, 0, 0)),
                   pl.BlockSpec((1, 8, HD), lambda i: (i, 0, 0)), pl.BlockSpec((1, 8, HD), lambda i: (i, 0, 0))] + x_out_specs,
        out_shape=[jax.ShapeDtypeStruct((s, SEG), bf16), jax.ShapeDtypeStruct((nc, SEG, HD), f32),
                   jax.ShapeDtypeStruct((nc, 8, HD), f32), jax.ShapeDtypeStruct((nc, 8, HD), f32)] + x_out_shapes,
        scratch_shapes=[pltpu.VMEM((SEG, HD), f32), pltpu.VMEM((8, HD), f32), pltpu.VMEM((8, HD), f32)] + x_sems,
        input_output_aliases=aliases, name="mlstm_fwd" if xf is None else "mlstm_fwd_carrying")(
            proj, proj, proj, proj, proj, proj, bias_row, norm_w, *x_ops)


def mlstm_bwd(proj, dy, cs, ns, ms, bias_row, norm_w, xf=None):
    s = proj.shape[0]
    nc = s // LR
    n_in, n_out = 12, 4
    x_ops, x_in_specs, x_out_shapes, x_out_specs, aliases, x_sems, n_xin, n_land = _carried(xf, n_in, n_out)

    def body(*refs):
        ins, x_refs, outs, land_refs, scratch, sem_refs = _cut_refs(refs, n_in, n_xin, n_out, n_land, 3)
        q_ref, k_ref, v_ref, o_ref, z_ref, sm_ref, dy_ref, cs_ref, ns_ref, ms_ref, b_ref, nw_ref = ins
        dp_ref, dsm_ref, db_ref, dnw_ref = outs
        dc_s, dn_s, dm_s = scratch
        i = pl.program_id(0)
        _carry_open(xf, i, nc, x_refs, land_refs, sem_refs)

        @pl.when(i == 0)
        def _():
            for ref in (db_ref, dnw_ref, dc_s, dn_s, dm_s):
                ref[...] = jnp.zeros_like(ref)

        _, vjp = jax.vjp(ml_chunk, q_ref[...], k_ref[...], v_ref[...], o_ref[...], z_ref[...], sm_ref[...],
                         b_ref[...], nw_ref[...], cs_ref[0], ns_ref[0], ms_ref[0])
        dq, dk, dv, do, dz, dsm, db, dnw, dc, dn, dm = vjp((dy_ref[...], dc_s[...], dn_s[...], dm_s[...]))
        for j, val in enumerate((dq, dk, dv, do, dz)):
            dp_ref[:, j * SEG:(j + 1) * SEG] = val.astype(bf16)
        dsm_ref[...] = dsm
        db_ref[0:1, :] += db
        dnw_ref[0:1, :] += dnw
        dc_s[...] = dc
        dn_s[...] = dn
        dm_s[...] = dm
        _carry_close(xf, i, nc, x_refs, land_refs, sem_refs)

    rev3 = lambda i: (nc - 1 - i, 0, 0)
    return pl.pallas_call(
        body, grid=(nc,),
        in_specs=[_seg_spec(LR, sg, nc) for sg in ML_SEGS]
        + [pl.BlockSpec((LR, HD), lambda i: (nc - 1 - i, SMALL_BLK)), pl.BlockSpec((LR, SEG), lambda i: (nc - 1 - i, 0)),
           pl.BlockSpec((1, SEG, HD), rev3), pl.BlockSpec((1, 8, HD), rev3), pl.BlockSpec((1, 8, HD), rev3),
           _const_spec((1, HD)), _const_spec((1, SEG))] + x_in_specs,
        out_specs=[pl.BlockSpec((LR, 5 * SEG), lambda i: (nc - 1 - i, 0)), pl.BlockSpec((LR, HD), lambda i: (nc - 1 - i, 0)),
                   _const_spec((8, HD)), _const_spec((8, SEG))] + x_out_specs,
        out_shape=[jax.ShapeDtypeStruct((s, 5 * SEG), bf16), jax.ShapeDtypeStruct((s, HD), f32),
                   jax.ShapeDtypeStruct((8, HD), f32), jax.ShapeDtypeStruct((8, SEG), f32)] + x_out_shapes,
        scratch_shapes=[pltpu.VMEM((SEG, HD), f32), pltpu.VMEM((8, HD), f32), pltpu.VMEM((8, HD), f32)] + x_sems,
        input_output_aliases=aliases, name="mlstm_bwd" if xf is None else "mlstm_bwd_carrying", compiler_params=_cp(48))(
            proj, proj, proj, proj, proj, proj, dy, cs, ns, ms, bias_row, norm_w, *x_ops)


GD_SEGS = (7, 8, 9)


def _carried(xf, n_in, n_out):
    operands, out_shapes, aliases, sems, _, n_xin, n_land = _carry_plumb(xf, n_in, n_out)
    return operands, [_ANY] * n_xin, out_shapes, [_ANY] * n_land, aliases, sems, n_xin, n_land


def _carry_open(xf, i, n_steps, x_refs, land_refs, sem_refs):
    if xf is None:
        return
    srcs = x_refs[:len(xf.srcs)]

    @pl.when(i == 0)
    def _():
        _carry_start(xf, srcs, land_refs, sem_refs)

    @pl.when(i == max(n_steps - 2, 0))
    def _():
        _carry_middle(xf, srcs, land_refs, sem_refs)


def _carry_close(xf, i, n_steps, x_refs, land_refs, sem_refs):
    if xf is None:
        return

    @pl.when(i == n_steps - 1)
    def _():
        _carry_finish(xf, x_refs[:len(xf.srcs)], land_refs, sem_refs)


def gdn_fwd(proj, conv_w, alog_row, dt_row, norm_w, xf=None):
    s = proj.shape[0]
    nc = s // LR
    n_in, n_out = 12, 3
    x_ops, x_in_specs, x_out_shapes, x_out_specs, aliases, x_sems, n_xin, n_land = _carried(xf, n_in, n_out)

    def body(*refs):
        qh_ref, q_ref, kh_ref, k_ref, vh_ref, v_ref, z_ref, sm_ref, cw_ref, al_ref, dt_ref, nw_ref = refs[:n_in]
        x_refs = refs[n_in:n_in + n_xin]
        y_ref, ss_ref, ti_ref = refs[n_in + n_xin:n_in + n_xin + n_out]
        land_refs = refs[n_in + n_xin + n_out:n_in + n_xin + n_out + n_land]
        st_s = refs[n_in + n_xin + n_out + n_land]
        sem_refs = refs[n_in + n_xin + n_out + n_land + 1:]
        i = pl.program_id(0)
        first = i == 0
        _carry_open(xf, i, nc, x_refs, land_refs, sem_refs)

        @pl.when(first)
        def _():
            st_s[...] = jnp.zeros_like(st_s)

        ss_ref[0] = st_s[...]
        halo = [jnp.where(first, 0.0, r[...]) for r in (qh_ref, kh_ref, vh_ref)]
        y, st2, t_invs = gd_chunk(halo[0], q_ref[...], halo[1], k_ref[...], halo[2], v_ref[...], z_ref[...], sm_ref[...],
                                  cw_ref[...], al_ref[...], dt_ref[...], nw_ref[...], st_s[...])
        y_ref[...] = y.astype(bf16)
        ti_ref[...] = t_invs
        st_s[...] = st2
        _carry_close(xf, i, nc, x_refs, land_refs, sem_refs)

    qkv_specs = []
    for sg in GD_SEGS:
        qkv_specs += [_halo_spec(LR, sg), _seg_spec(LR, sg)]
    return pl.pallas_call(
        body, grid=(nc,),
        in_specs=qkv_specs + [_seg_spec(LR, 10), pl.BlockSpec((LR, HD), lambda i: (i, SMALL_BLK)),
                              _const_spec((4, 3 * SEG)), _const_spec((1, HD)), _const_spec((1, HD)), _const_spec((1, HD))]
        + x_in_specs,
        out_specs=[pl.BlockSpec((LR, SEG), lambda i: (i, 0)), pl.BlockSpec((1, SEG, HD), lambda i: (i, 0, 0)),
                   pl.BlockSpec((RUN * NH, LC, LC), lambda i: (i, 0, 0))] + x_out_specs,
        out_shape=[jax.ShapeDtypeStruct((s, SEG), bf16), jax.ShapeDtypeStruct((nc, SEG, HD), f32),
                   jax.ShapeDtypeStruct((s // LC * NH, LC, LC), f32)] + x_out_shapes,
        scratch_shapes=[pltpu.VMEM((SEG, HD), f32)] + x_sems, input_output_aliases=aliases,
        name="gdn_fwd" if xf is None else "gdn_fwd_carrying")(
            proj, proj, proj, proj, proj, proj, proj, proj, conv_w, alog_row, dt_row, norm_w, *x_ops)


def gdn_bwd(proj, dy, ss, t_invs, conv_w, alog_row, dt_row, norm_w, xf=None):
    s = proj.shape[0]
    nc = s // LR
    n_in, n_out = 15, 6
    x_ops, x_in_specs, x_out_shapes, x_out_specs, aliases, x_sems, n_xin, n_land = _carried(xf, n_in, n_out)

    def body(*refs):
        (qh_ref, q_ref, kh_ref, k_ref, vh_ref, v_ref, z_ref, sm_ref, dy_ref, ss_ref, ti_ref,
         cw_ref, al_ref, dt_ref, nw_ref) = refs[:n_in]
        x_refs = refs[n_in:n_in + n_xin]
        dp_ref, dsm_ref, dcw_ref, dal_ref, ddt_ref, dnw_ref = refs[n_in + n_xin:n_in + n_xin + n_out]
        land_refs = refs[n_in + n_xin + n_out:n_in + n_xin + n_out + n_land]
        dst_s, dhalo = refs[n_in + n_xin + n_out + n_land:n_in + n_xin + n_out + n_land + 2]
        sem_refs = refs[n_in + n_xin + n_out + n_land + 2:]
        i = pl.program_id(0)
        first_chunk = i == nc - 1
        _carry_open(xf, i, nc, x_refs, land_refs, sem_refs)

        @pl.when(i == 0)
        def _():
            for ref in (dcw_ref, dal_ref, ddt_ref, dnw_ref, dst_s, dhalo):
                ref[...] = jnp.zeros_like(ref)

        halo = [jnp.where(first_chunk, 0.0, r[...]) for r in (qh_ref, kh_ref, vh_ref)]
        with_known = functools.partial(gd_chunk, known_t=ti_ref[...])
        _, vjp = jax.vjp(with_known, halo[0], q_ref[...], halo[1], k_ref[...], halo[2], v_ref[...], z_ref[...], sm_ref[...],
                         cw_ref[...], al_ref[...], dt_ref[...], nw_ref[...], ss_ref[0])
        dqh, dq, dkh, dk, dvh, dv, dz, dsm, dcw, dal, ddt, dnw, dst = vjp((dy_ref[...], dst_s[...]))
        for j, val in enumerate((dq, dk, dv)):
            val = val + jnp.concatenate([jnp.zeros((LR - 8, SEG), f32), dhalo[:, j * SEG:(j + 1) * SEG]], axis=0)
            dp_ref[:, j * SEG:(j + 1) * SEG] = val.astype(bf16)
        dp_ref[:, 3 * SEG:4 * SEG] = dz.astype(bf16)
        for j, val in enumerate((dqh, dkh, dvh)):
            dhalo[:, j * SEG:(j + 1) * SEG] = val
        dsm_ref[...] = dsm
        dcw_ref[...] += dcw
        dal_ref[0:1, :] += dal
        ddt_ref[0:1, :] += ddt
        dnw_ref[0:1, :] += dnw
        dst_s[...] = dst
        _carry_close(xf, i, nc, x_refs, land_refs, sem_refs)

    qkv_specs = []
    for sg in GD_SEGS:
        qkv_specs += [_halo_spec(LR, sg, nc), _seg_spec(LR, sg, nc)]
    return pl.pallas_call(
        body, grid=(nc,),
        in_specs=qkv_specs + [_seg_spec(LR, 10, nc), pl.BlockSpec((LR, HD), lambda i: (nc - 1 - i, SMALL_BLK)),
                              pl.BlockSpec((LR, SEG), lambda i: (nc - 1 - i, 0)),
                              pl.BlockSpec((1, SEG, HD), lambda i: (nc - 1 - i, 0, 0)),
                              pl.BlockSpec((RUN * NH, LC, LC), lambda i: (nc - 1 - i, 0, 0)),
                              _const_spec((4, 3 * SEG)), _const_spec((1, HD)), _const_spec((1, HD)), _const_spec((1, HD))]
        + x_in_specs,
        out_specs=[pl.BlockSpec((LR, 4 * SEG), lambda i: (nc - 1 - i, 0)), pl.BlockSpec((LR, HD), lambda i: (nc - 1 - i, 0)),
                   _const_spec((4, 3 * SEG)), _const_spec((8, HD)), _const_spec((8, HD)), _const_spec((8, HD))] + x_out_specs,
        out_shape=[jax.ShapeDtypeStruct((s, 4 * SEG), bf16), jax.ShapeDtypeStruct((s, HD), f32),
                   jax.ShapeDtypeStruct((4, 3 * SEG), f32), jax.ShapeDtypeStruct((8, HD), f32),
                   jax.ShapeDtypeStruct((8, HD), f32), jax.ShapeDtypeStruct((8, HD), f32)] + x_out_shapes,
        scratch_shapes=[pltpu.VMEM((SEG, HD), f32), pltpu.VMEM((8, 3 * SEG), f32)] + x_sems, input_output_aliases=aliases,
        name="gdn_bwd" if xf is None else "gdn_bwd_carrying", compiler_params=_cp(48))(
            proj, proj, proj, proj, proj, proj, proj, proj, dy, ss, t_invs, conv_w, alog_row, dt_row, norm_w, *x_ops)


def _my_place():
    return lax.axis_index("x"), lax.axis_index("y"), lax.axis_index("c")


def _slot(p):
    return 4 * p[0] + 2 * p[1] + p[2]


def _peer(me, j):
    flips = ((j >> 2) & 1, (j >> 1) & 1, j & 1)
    return tuple((1 - v) if fl else v for v, fl in zip(me, flips))


_ANY = pl.BlockSpec(memory_space=pl.ANY)


class Scatter(NamedTuple):
    srcs: list
    lands: list
    land_of: list
    layer: list
    span: list


class Gather2(NamedTuple):
    srcs: list
    lands: list
    span: list


def _carry_plumb(cx, n_in, n_out):
    if cx is None:
        return [], [], {}, [], 0, 0, 0
    n_src = len(cx.srcs)
    passed = [li for li, ld in enumerate(cx.lands) if not isinstance(ld, jax.ShapeDtypeStruct)]
    operands = list(cx.srcs) + [cx.lands[li] for li in passed]
    aliases = {n_in + n_src + k: n_out + li for k, li in enumerate(passed)}
    out_shapes = [jax.ShapeDtypeStruct(ld.shape, ld.dtype) for ld in cx.lands]
    sems = [pltpu.SemaphoreType.DMA((n_src, N_DEV - 1)), pltpu.SemaphoreType.DMA((n_src, N_DEV - 1)),
            pltpu.SemaphoreType.DMA((n_src,))]
    return operands, out_shapes, aliases, sems, n_src, len(operands), len(cx.lands)


def _scatter_copies(sc, src_refs, land_refs, send_sems, recv_sems, local_sems):
    me = _my_place()
    mine = _slot(me)
    local, remote = [], []
    for a, src_ref in enumerate(src_refs):
        lo, hi = sc.span[a]
        rows = pl.ds(lo, hi - lo)
        dst = land_refs[sc.land_of[a]].at[mine, sc.layer[a], rows]
        local.append(pltpu.make_async_copy(src_ref.at[mine, rows], dst, local_sems.at[a]))
        for j in range(1, N_DEV):
            peer = _peer(me, j)
            remote.append(pltpu.make_async_remote_copy(
                src_ref=src_ref.at[_slot(peer), rows], dst_ref=dst, send_sem=send_sems.at[a, j - 1],
                recv_sem=recv_sems.at[a, j - 1], device_id=peer, device_id_type=MESH))
    return local, remote


def _gather2_copy(land, sems, a, k, block_of, to, rows, src=None):
    dst = land.at[_slot(block_of), rows]
    return pltpu.make_async_remote_copy(src_ref=dst if src is None else src, dst_ref=dst, send_sem=sems[0].at[a, k],
                                        recv_sem=sems[1].at[a, k], device_id=to, device_id_type=MESH)


def _gather2_places():
    x, y, c = _my_place()
    return (x, y, c), (x, y, 1 - c), [(1 - x, y), (x, 1 - y), (1 - x, 1 - y)], c


def _carry_start(cx, src_refs, land_refs, sems):
    if isinstance(cx, Scatter):
        local, remote = _scatter_copies(cx, src_refs, land_refs, *sems)
        for cp in local + remote:
            cp.start()
        return
    me, sib, chips, c = _gather2_places()
    for a, (src_ref, land) in enumerate(zip(src_refs, land_refs)):
        rows = pl.ds(cx.span[a][0], cx.span[a][1] - cx.span[a][0])
        src = src_ref.at[rows]
        pltpu.make_async_copy(src, land.at[_slot(me), rows], sems[2].at[a]).start()
        _gather2_copy(land, sems, a, 0, me, sib, rows, src=src).start()
        for j, chip in enumerate(chips):
            _gather2_copy(land, sems, a, 1 + j, me, (*chip, c), rows, src=src).start()


def _carry_middle(cx, src_refs, land_refs, sems):
    if isinstance(cx, Scatter):
        return
    me, sib, chips, c = _gather2_places()
    for a, land in enumerate(land_refs):
        rows = pl.ds(cx.span[a][0], cx.span[a][1] - cx.span[a][0])
        for j, chip in enumerate(chips):
            _gather2_copy(land, sems, a, 1 + j, (*chip, c), me, rows).wait_recv()
            _gather2_copy(land, sems, a, 4 + j, (*chip, c), sib, rows).start()


def _carry_finish(cx, src_refs, land_refs, sems):
    if isinstance(cx, Scatter):
        local, remote = _scatter_copies(cx, src_refs, land_refs, *sems)
        for cp in remote:
            cp.wait()
        for cp in local:
            cp.wait()
        return
    me, sib, chips, c = _gather2_places()
    for a, (src_ref, land) in enumerate(zip(src_refs, land_refs)):
        rows = pl.ds(cx.span[a][0], cx.span[a][1] - cx.span[a][0])
        src = src_ref.at[rows]
        _gather2_copy(land, sems, a, 0, sib, me, rows).wait_recv()
        for j, chip in enumerate(chips):
            _gather2_copy(land, sems, a, 4 + j, (*chip, 1 - c), me, rows).wait_recv()
        _gather2_copy(land, sems, a, 0, me, sib, rows, src=src).wait_send()
        for j, chip in enumerate(chips):
            _gather2_copy(land, sems, a, 1 + j, me, (*chip, c), rows, src=src).wait_send()
            _gather2_copy(land, sems, a, 4 + j, (*chip, c), sib, rows).wait_send()
        pltpu.make_async_copy(src, land.at[_slot(me), rows], sems[2].at[a]).wait()


def exchange(cx, name):
    operands, out_shapes, aliases, sems, n_src, n_xin, n_land = _carry_plumb(cx, 0, 0)

    def body(*refs):
        src_refs, land_refs, sem_refs = refs[:n_src], refs[n_xin:n_xin + n_land], refs[n_xin + n_land:]
        _carry_start(cx, src_refs, land_refs, sem_refs)
        _carry_middle(cx, src_refs, land_refs, sem_refs)
        _carry_finish(cx, src_refs, land_refs, sem_refs)

    return pl.pallas_call(body, in_specs=[_ANY] * n_xin, out_specs=[_ANY] * n_land, out_shape=out_shapes,
                          scratch_shapes=sems, input_output_aliases=aliases, name=name)(*operands)


def small_allreduce(packed, name, xf=None):
    r = packed.shape[0]
    rc = r // N_DEV
    x_ops, x_in_specs, x_out_shapes, x_out_specs, aliases, x_sems, n_xin, n_land = _carried(xf, 1, 1)

    def body(*refs):
        (in_ref,), x_refs, (out_ref,), land_refs, scratch, sem_refs = _cut_refs(refs, 1, n_xin, 1, n_land, 5)
        recv_buf, send1, recv1, send2, recv2 = scratch
        if xf is not None:
            _carry_start(xf, x_refs[:len(xf.srcs)], land_refs, sem_refs)
        me = _my_place()
        mine = _slot(me)
        my_rows = pl.ds(pl.multiple_of(mine * rc, 8), rc)
        first, second = [], []
        for j in range(1, N_DEV):
            peer = _peer(me, j)
            peer_rows = pl.ds(pl.multiple_of(_slot(peer) * rc, 8), rc)
            first.append(pltpu.make_async_remote_copy(
                src_ref=in_ref.at[peer_rows], dst_ref=recv_buf.at[mine], send_sem=send1.at[j - 1], recv_sem=recv1.at[j - 1],
                device_id=peer, device_id_type=MESH))
            second.append(pltpu.make_async_remote_copy(
                src_ref=out_ref.at[my_rows], dst_ref=out_ref.at[my_rows], send_sem=send2.at[j - 1], recv_sem=recv2.at[j - 1],
                device_id=peer, device_id_type=MESH))
        for cp in first:
            cp.start()
        recv_buf[mine] = in_ref[my_rows]
        for cp in first:
            cp.wait()
        acc = recv_buf[0]
        for k in range(1, N_DEV):
            acc = acc + recv_buf[k]
        out_ref[my_rows] = acc
        for cp in second:
            cp.start()
        for cp in second:
            cp.wait()
        if xf is not None:
            _carry_finish(xf, x_refs[:len(xf.srcs)], land_refs, sem_refs)

    vmem = pl.BlockSpec(memory_space=pltpu.VMEM)
    return pl.pallas_call(
        body, in_specs=[vmem] + x_in_specs, out_specs=[vmem] + x_out_specs,
        out_shape=[jax.ShapeDtypeStruct((r, 128), f32)] + x_out_shapes,
        scratch_shapes=[pltpu.VMEM((N_DEV, rc, 128), f32)] + [pltpu.SemaphoreType.DMA((N_DEV - 1,))] * 4 + x_sems,
        input_output_aliases=aliases, name=name, compiler_params=_cp(32))(packed, *x_ops)


def _adamw(w, g, m, v):
    m = ADAM_B1 * m + (1.0 - ADAM_B1) * g
    v = ADAM_B2 * v + (1.0 - ADAM_B2) * (g * g)
    m_hat = m / (1.0 - ADAM_B1 ** ADAM_STEP)
    v_hat = v / (1.0 - ADAM_B2 ** ADAM_STEP)
    delta = -ADAM_LR * (m_hat / (jnp.sqrt(v_hat) + ADAM_EPS) + ADAM_WD * w)
    return delta, m, v


def adam_slots(slots, w, m, v, rows, name):
    _, depth, r, c = slots.shape

    def body(s_ref, w_ref, m_ref, v_ref, g_ref, d_ref, m2_ref, v2_ref):
        g = s_ref[0, 0].astype(f32)
        for k in range(1, N_DEV):
            g = g + s_ref[k, 0].astype(f32)
        d, m2, v2 = _adamw(w_ref[0], g, m_ref[0], v_ref[0])
        g_ref[0] = g
        d_ref[0] = d
        m2_ref[0] = m2
        v2_ref[0] = v2

    blk = pl.BlockSpec((1, rows, c), lambda l, i: (l, i, 0))
    sh = jax.ShapeDtypeStruct((depth, r, c), f32)
    return pl.pallas_call(
        body, grid=(depth, r // rows),
        in_specs=[pl.BlockSpec((N_DEV, 1, rows, c), lambda l, i: (0, l, i, 0)), blk, blk, blk],
        out_specs=[blk] * 4, out_shape=[sh] * 4, name=name, compiler_params=_cp(40))(slots, w, m, v)


def adam_flat(g, w, m, v, name):
    def body(g_ref, w_ref, m_ref, v_ref, d_ref, m2_ref, v2_ref):
        d, m2, v2 = _adamw(w_ref[...], g_ref[...], m_ref[...], v_ref[...])
        d_ref[...] = d
        m2_ref[...] = m2
        v2_ref[...] = v2

    sh = jax.ShapeDtypeStruct(g.shape, f32)
    return pl.pallas_call(body, out_shape=[sh] * 3, name=name)(g, w, m, v)


def _rows_of(shape):
    n = 1
    for dim in shape:
        n *= dim
    return n, -(-n // (8 * 128)) * 8


def _pack(arrs):
    parts = []
    for a in arrs:
        n, rows = _rows_of(a.shape)
        parts.append(jnp.pad(a.reshape(-1).astype(f32), (0, rows * 128 - n)).reshape(rows, 128))
    return jnp.concatenate(parts, axis=0)


def _unpack(packed, shapes):
    out, row = [], 0
    for sh in shapes:
        n, rows = _rows_of(sh)
        out.append(packed[row:row + rows].reshape(-1)[:n].reshape(sh))
        row += rows
    return out


def _block_diag(gw):
    eye = jnp.eye(8, dtype=gw.dtype)
    return (gw[:, :, None, :] * eye[:, None, :, None]).reshape(SEG, SEG)


def _diag_blocks(dense):
    eye = jnp.eye(8, dtype=dense.dtype)
    return (dense.reshape(8, 64, 8, 64) * eye[:, None, :, None]).sum(axis=2)


def _lane_row(vals, first_lane):
    return jnp.zeros((1, HD), f32).at[0, first_lane:first_lane + NH].set(vals)


def kernel(x, norm_w, w_in, rg_conv_w, rg_conv_b, rg_gate_w, rg_gate_b, rg_lambda, ml_gate_b, ml_norm_w, gd_conv_w, gd_a_log, gd_dt_bias, gd_norm_w, w_out, final_norm_w, loss_target, m_norm_w, m_w_in, m_rg_conv_w, m_rg_conv_b, m_rg_gate_w, m_rg_gate_b, m_rg_lambda, m_ml_gate_b, m_ml_norm_w, m_gd_conv_w, m_gd_a_log, m_gd_dt_bias, m_gd_norm_w, m_w_out, m_final_norm_w, v_norm_w, v_w_in, v_rg_conv_w, v_rg_conv_b, v_rg_gate_w, v_rg_gate_b, v_rg_lambda, v_ml_gate_b, v_ml_norm_w, v_gd_conv_w, v_gd_a_log, v_gd_dt_bias, v_gd_norm_w, v_w_out, v_final_norm_w):
    s = x.shape[1]
    xs = x.reshape(s, D)
    tgt = loss_target.reshape(s, D)
    me = 4 * lax.axis_index("x") + 2 * lax.axis_index("y") + lax.axis_index("c")

    comm = MeshComm(w_in, w_out, [rg_conv_w, rg_gate_b, gd_conv_w])
    rg_conv_full, rg_gb_full, gd_conv_full = comm.small_weights
    loss_part, dx, d_fw, g_small = local_step(
        xs, tgt, comm, rg_conv_full, rg_gb_full, gd_conv_full, norm_w, rg_conv_b, rg_gate_w, rg_lambda,
        ml_gate_b, ml_norm_w, gd_a_log, gd_dt_bias, gd_norm_w, final_norm_w)
    given_w = dict(norm_w=norm_w, rg_conv_w=rg_conv_w, rg_conv_b=rg_conv_b, rg_gate_w=rg_gate_w, rg_gate_b=rg_gate_b,
                   rg_lambda=rg_lambda, ml_gate_b=ml_gate_b, ml_norm_w=ml_norm_w, gd_conv_w=gd_conv_w, gd_a_log=gd_a_log,
                   gd_dt_bias=gd_dt_bias, gd_norm_w=gd_norm_w, final_norm_w=final_norm_w, w_in=w_in, w_out=w_out)
    given_m = dict(norm_w=m_norm_w, rg_conv_w=m_rg_conv_w, rg_conv_b=m_rg_conv_b, rg_gate_w=m_rg_gate_w, rg_gate_b=m_rg_gate_b,
                   rg_lambda=m_rg_lambda, ml_gate_b=m_ml_gate_b, ml_norm_w=m_ml_norm_w, gd_conv_w=m_gd_conv_w,
                   gd_a_log=m_gd_a_log, gd_dt_bias=m_gd_dt_bias, gd_norm_w=m_gd_norm_w, final_norm_w=m_final_norm_w,
                   w_in=m_w_in, w_out=m_w_out)
    given_v = dict(norm_w=v_norm_w, rg_conv_w=v_rg_conv_w, rg_conv_b=v_rg_conv_b, rg_gate_w=v_rg_gate_w, rg_gate_b=v_rg_gate_b,
                   rg_lambda=v_rg_lambda, ml_gate_b=v_ml_gate_b, ml_norm_w=v_ml_norm_w, gd_conv_w=v_gd_conv_w,
                   gd_a_log=v_gd_a_log, gd_dt_bias=v_gd_dt_bias, gd_norm_w=v_gd_norm_w, final_norm_w=v_final_norm_w,
                   w_in=v_w_in, w_out=v_w_out)
    return finish_step(loss_part, dx, d_fw, g_small, comm, s, me, given_w, given_m, given_v)


def _gathered_pieces():
    per = D_IN // N_DEV
    pieces = []
    for lo, hi in ((0, 3584), (3592, 5640), (3584, 3592), (5640, 5648)):
        col = lo
        while col < hi:
            k = col // per
            end = min(hi, (k + 1) * per)
            pieces.append((k, col - k * per, end - k * per))
            col = end
    return pieces


class MeshComm:
    GWI_SPLIT = 256
    WI_SPLIT = 384

    def __init__(self, w_in, w_out, small_shards):
        per = D_IN // N_DEV
        self.wi_sh = [w_in[l].astype(bf16) for l in range(DEPTH)]
        self.wo_sh = [w_out[l].astype(bf16) for l in range(DEPTH)]
        self.wi_land = jax.ShapeDtypeStruct((N_DEV, D, per), bf16)
        self.wo_land = jax.ShapeDtypeStruct((N_DEV, 3 * SEG // N_DEV, D), bf16)
        packed = _pack(small_shards)
        first = Gather2([self.wi_sh[0], self.wo_sh[0], packed],
                        [self.wi_land, self.wo_land, jax.ShapeDtypeStruct((N_DEV,) + packed.shape, f32)],
                        [(0, D), (0, 3 * SEG // N_DEV), (0, packed.shape[0])])
        wi_g, wo_g, sm_g = exchange(first, "gather_first")
        self.wi_g, self.wo_g = {0: wi_g}, {0: wo_g}
        shapes = [a.shape for a in small_shards]
        parts = [_unpack(sm_g[k], shapes) for k in range(N_DEV)]
        self.small_weights = [jnp.concatenate([p[j] for p in parts], axis=-1) for j in range(len(small_shards))]
        self.gwi_land = lax.empty((N_DEV, DEPTH, D, per), bf16)
        self.gwo_land = lax.empty((N_DEV, DEPTH, 3 * SEG // N_DEV, D), bf16)
        self.gwi_slots = {}
        self.gwo_slots = {}

    def weights(self, l):
        cols = [self.wi_g[l][k, :, a:b] for k, a, b in _gathered_pieces()] + [jnp.zeros((D, DP - D_IN), bf16)]
        return jnp.concatenate(cols, axis=1), self.wo_g[l].reshape(3 * SEG, D)

    def fwd_carry(self, l, host):
        if l + 1 >= DEPTH:
            return None
        if host == "mlstm":
            return Gather2([self.wo_sh[l + 1]], [self.wo_land], [(0, 3 * SEG // N_DEV)])
        if host == "inproj":
            return Gather2([self.wi_sh[l + 1]], [self.wi_land], [(0, self.WI_SPLIT)])
        return Gather2([self.wi_sh[l + 1]], [self.wi_g[l + 1]], [(self.WI_SPLIT, D)])

    def fwd_landed(self, l, host, landed):
        (self.wo_g if host == "mlstm" else self.wi_g)[l + 1] = landed[0]

    def own_w_out_grad(self, l, g_wo):
        self.gwo_slots[l] = g_wo.reshape(N_DEV, 3 * SEG // N_DEV, D).astype(bf16)

    def bwd_carry(self, l, host):
        rows_o = 3 * SEG // N_DEV
        srcs, land_of, layer, span = [], [], [], []
        if l + 1 < DEPTH:
            if host == "mlstm":
                srcs += [self.gwo_slots[l + 1], self.gwi_slots[l + 1]]
                land_of, layer, span = [1, 0], [l + 1, l + 1], [(0, rows_o), (0, self.GWI_SPLIT)]
            else:
                srcs, land_of, layer, span = [self.gwi_slots[l + 1]], [0], [l + 1], [(self.GWI_SPLIT, D)]
        if l == 0 and host == "mlstm":
            srcs, land_of, layer, span = srcs + [self.gwo_slots[0]], land_of + [1], layer + [0], span + [(0, rows_o)]
        if not srcs:
            return None
        return Scatter(srcs, [self.gwi_land, self.gwo_land], land_of, layer, span)

    def bwd_landed(self, landed):
        self.gwi_land, self.gwo_land = landed

    def grads_ready(self, l, pieces):
        per = D_IN // N_DEV
        g_wi = jnp.concatenate(pieces, axis=1)
        self.gwi_slots[l] = jnp.stack([g_wi[:, k * per:(k + 1) * per] for k in range(N_DEV)]).astype(bf16)

    def last_carry(self):
        return Scatter([self.gwi_slots[0]], [self.gwi_land, self.gwo_land], [0], [0], [(0, D)])


def local_step(xs, tgt, comm, rg_conv_full, rg_gb_full, gd_conv_full, norm_w, rg_conv_b, rg_gate_w,
               rg_lambda, ml_gate_b, ml_norm_w, gd_a_log, gd_dt_bias, gd_norm_w, final_norm_w):
    acts = []
    for l in range(DEPTH):
        nw = norm_w[l].reshape(1, D)
        w_in_l, w_out_l = comm.weights(l)
        xf = comm.fwd_carry(l, "inproj")
        proj, hn_t, *landed = inproj(xs, nw, w_in_l, xf=xf)
        if xf is not None:
            comm.fwd_landed(l, "inproj", landed)
        rg_p = (rg_conv_full[l], rg_conv_b[l].reshape(1, SEG), _block_diag(rg_gate_w[l, 0]), _block_diag(rg_gate_w[l, 1]),
                rg_gb_full[l], rg_lambda[l].reshape(1, SEG))
        y_rg, hs = rglru_fwd(proj, *rg_p)
        ml_p = (jnp.zeros((1, HD), f32).at[0, 0:2 * NH].set(ml_gate_b[l].reshape(-1)), ml_norm_w[l].reshape(1, SEG))
        xf = comm.fwd_carry(l, "mlstm")
        y_ml, cs, ns, ms, *landed = mlstm_fwd(proj, *ml_p, xf=xf)
        if xf is not None:
            comm.fwd_landed(l, "mlstm", landed)
        gd_p = (gd_conv_full[l], _lane_row(gd_a_log[l], 2 * NH), _lane_row(gd_dt_bias[l], 2 * NH), gd_norm_w[l].reshape(1, HD))
        xf = comm.fwd_carry(l, "gdn")
        y_gd, ss, t_invs, *landed = gdn_fwd(proj, *gd_p, xf=xf)
        if xf is not None:
            comm.fwd_landed(l, "gdn", landed)
        acts.append((xs, nw, proj, hn_t, w_in_l, w_out_l, rg_p, y_rg, hs, ml_p, y_ml, cs, ns, ms, gd_p, y_gd, ss, t_invs))
        xs = outproj(xs, y_rg, y_ml, y_gd, w_out_l)

    dx, loss_part, d_fw = head(xs, final_norm_w.reshape(1, D), tgt)

    g_small = {k: [None] * DEPTH for k in ("norm_w", "rg_conv_w", "rg_conv_b", "rg_gate_w", "rg_gate_b", "rg_lambda",
                                           "ml_gate_b", "ml_norm_w", "gd_conv_w", "gd_a_log", "gd_dt_bias", "gd_norm_w")}
    for l in reversed(range(DEPTH)):
        x_l, nw, proj, hn_t, w_in_l, w_out_l, rg_p, y_rg, hs, ml_p, y_ml, cs, ns, ms, gd_p, y_gd, ss, t_invs = acts[l]
        dy_rg, dy_ml, dy_gd, g_wo = outproj_bwd(dx, y_rg, y_ml, y_gd, w_out_l)
        d_rg, d_cw, d_cb, d_gr, d_gi, d_gb, d_lam = rglru_bwd(proj, hs, dy_rg, *rg_p)
        comm.own_w_out_grad(l, g_wo)
        xf = comm.bwd_carry(l, "mlstm")
        d_ml, d_sm_ml, d_bias, d_mnw, *landed = mlstm_bwd(proj, dy_ml, cs, ns, ms, *ml_p, xf=xf)
        if xf is not None:
            comm.bwd_landed(landed)
        xf = comm.bwd_carry(l, "gdn")
        d_gd, d_sm_gd, d_gcw, d_al, d_dt, d_gnw, *landed = gdn_bwd(proj, dy_gd, ss, t_invs, *gd_p, xf=xf)
        if xf is not None:
            comm.bwd_landed(landed)
        dx, d_nw = inproj_bwd_x(x_l, nw, w_in_l, dx, d_rg, d_ml, d_gd, d_sm_ml, d_sm_gd)
        gw_rg = wgrad(hn_t, [d_rg], "wgrad_rg")
        gw_ml = wgrad(hn_t, [d_ml], "wgrad_ml")
        gw_gd = wgrad(hn_t, [d_gd], "wgrad_gd")
        gw_sm = wgrad(hn_t, [d_sm_ml, d_sm_gd], "wgrad_small")
        comm.grads_ready(l, [gw_rg, gw_ml, gw_sm[:, 0:2 * NH], gw_gd, gw_sm[:, 2 * NH:4 * NH]])
        g_small["norm_w"][l] = d_nw[0]
        g_small["rg_conv_w"][l] = d_cw
        g_small["rg_conv_b"][l] = d_cb[0]
        g_small["rg_gate_w"][l] = jnp.stack([_diag_blocks(d_gr), _diag_blocks(d_gi)])
        g_small["rg_gate_b"][l] = d_gb
        g_small["rg_lambda"][l] = d_lam[0]
        g_small["ml_gate_b"][l] = d_bias[0, 0:2 * NH].reshape(2, NH)
        g_small["ml_norm_w"][l] = d_mnw[0]
        g_small["gd_conv_w"][l] = d_gcw
        g_small["gd_a_log"][l] = d_al[0, 2 * NH:3 * NH]
        g_small["gd_dt_bias"][l] = d_dt[0, 2 * NH:3 * NH]
        g_small["gd_norm_w"][l] = d_gnw[0]
    return loss_part, dx, d_fw, g_small


def finish_step(loss_part, dx, d_fw, g_small, comm, s, me, given_w, given_m, given_v):
    small_names = ["norm_w", "rg_conv_w", "rg_conv_b", "rg_gate_w", "rg_gate_b", "rg_lambda", "ml_gate_b", "ml_norm_w",
                   "gd_conv_w", "gd_a_log", "gd_dt_bias", "gd_norm_w"]
    small_list = [loss_part[0, 0:1], d_fw[0]] + [jnp.stack(g_small[k]) for k in small_names]
    small_shapes = [a.shape for a in small_list]
    packed = _pack(small_list)
    packed = jnp.pad(packed, ((0, -packed.shape[0] % (8 * N_DEV)), (0, 0)))
    summed, gwi_r, gwo_r = small_allreduce(packed, "last_exchange", xf=comm.last_carry())
    g_all = _unpack(summed, small_shapes)

    g_wi, d_wi, m_wi, v_wi = adam_slots(gwi_r, given_w["w_in"], given_m["w_in"], given_v["w_in"], 256, "adam_w_in")
    g_wo, d_wo, m_wo, v_wo = adam_slots(gwo_r, given_w["w_out"], given_m["w_out"], given_v["w_out"], 192, "adam_w_out")
    loss = g_all[0][0]
    grads = {"final_norm_w": g_all[1]}
    for k, g in zip(small_names, g_all[2:]):
        grads[k] = g
    for k, width in (("rg_conv_w", 64), ("rg_gate_b", 64), ("gd_conv_w", 192)):
        grads[k] = lax.dynamic_slice_in_dim(grads[k], me * width, width, axis=2)
    names = small_names + ["final_norm_w"]
    shapes = [given_w[k].shape for k in names]
    d_p, m_p, v_p = adam_flat(_pack([grads[k] for k in names]), _pack([given_w[k] for k in names]),
                              _pack([given_m[k] for k in names]), _pack([given_v[k] for k in names]), "adam_small")
    deltas = dict(zip(names, _unpack(d_p, shapes)))
    new_m = dict(zip(names, _unpack(m_p, shapes)))
    new_v = dict(zip(names, _unpack(v_p, shapes)))
    grads["w_in"], deltas["w_in"], new_m["w_in"], new_v["w_in"] = g_wi, d_wi, m_wi, v_wi
    grads["w_out"], deltas["w_out"], new_m["w_out"], new_v["w_out"] = g_wo, d_wo, m_wo, v_wo

    order = ["norm_w", "w_in", "rg_conv_w", "rg_conv_b", "rg_gate_w", "rg_gate_b", "rg_lambda", "ml_gate_b", "ml_norm_w",
             "gd_conv_w", "gd_a_log", "gd_dt_bias", "gd_norm_w", "w_out", "final_norm_w"]
    return (loss, dx.reshape(1, s, D), *[grads[k] for k in order], *[deltas[k] for k in order],
            *[new_m[k] for k in order], *[new_v[k] for k in order])
```

```python
import functools
from typing import NamedTuple

import jax
import jax.numpy as jnp
from jax import lax
from jax.experimental import pallas as pl
from jax.experimental.pallas import tpu as pltpu

f32 = jnp.float32
bf16 = jnp.bfloat16
MESH = pl.DeviceIdType.MESH

N_DEV = 8
D = 1024
DEPTH = 4
EPS = 1e-6
SEG = 512
HD = 128
NH = 4
LC = 64
RUN = 4
LR = RUN * LC
D_IN = 5648
DP = 5760
SMALL0 = 5632
RG_TILE = 256
RG_C = 8.0

ADAM_LR = 0.001
ADAM_B1 = 0.9
ADAM_B2 = 0.999
ADAM_EPS = 1e-08
ADAM_WD = 0.01
ADAM_STEP = 10


def _cp(vmem_mb):
    return pltpu.CompilerParams(vmem_limit_bytes=vmem_mb * 2 ** 20)


def _dot(a, b, ca, cb):
    return lax.dot_general(a.astype(bf16), b.astype(bf16), (((ca,), (cb,)), ((), ())), preferred_element_type=f32)


@jax.custom_vjp
def mm_nn(a, b):
    return _dot(a, b, 1, 0)


@jax.custom_vjp
def mm_nt(a, b):
    return _dot(a, b, 1, 1)


@jax.custom_vjp
def mm_tn(a, b):
    return _dot(a, b, 0, 0)


mm_nn.defvjp(lambda a, b: (mm_nn(a, b), (a, b)), lambda r, g: (mm_nt(g, r[1]), mm_tn(r[0], g)))
mm_nt.defvjp(lambda a, b: (mm_nt(a, b), (a, b)), lambda r, g: (mm_nn(g, r[1]), mm_tn(g, r[0])))
mm_tn.defvjp(lambda a, b: (mm_tn(a, b), (a, b)), lambda r, g: (mm_nt(r[1], g), mm_nn(r[0], g)))


def _split(x):
    hi = x.astype(bf16)
    return hi, (x - hi.astype(f32)).astype(bf16)


def dot3(a, b):
    ah, al = _split(a)
    bh, bl = _split(b)
    d = functools.partial(jnp.dot, preferred_element_type=f32)
    return d(ah, bh) + (d(al, bh) + d(ah, bl))


def _tri_sum(x, reverse):
    n = x.shape[0]
    r = lax.broadcasted_iota(jnp.int32, (n, 3 * n), 0)
    c = lax.broadcasted_iota(jnp.int32, (n, 3 * n), 1) & (n - 1)
    ones = ((c >= r) if reverse else (c <= r)).astype(bf16)
    hi = x.astype(bf16)
    rest = x - hi.astype(f32)
    mid = rest.astype(bf16)
    lo = (rest - mid.astype(f32)).astype(bf16)
    return jnp.dot(ones, jnp.concatenate([hi, mid, lo], axis=0), preferred_element_type=f32)


@jax.custom_vjp
def cumsum_rows(x):
    return _tri_sum(x, False)


@jax.custom_vjp
def rev_cumsum_rows(x):
    return _tri_sum(x, True)


cumsum_rows.defvjp(lambda x: (cumsum_rows(x), None), lambda _, g: (rev_cumsum_rows(g),))
rev_cumsum_rows.defvjp(lambda x: (rev_cumsum_rows(x), None), lambda _, g: (cumsum_rows(g),))


def _tri(n, strict=False):
    r = lax.broadcasted_iota(jnp.int32, (n, n), 0)
    c = lax.broadcasted_iota(jnp.int32, (n, n), 1)
    return (r > c) if strict else (r >= c)


def _lane_col(v, j):
    lane = lax.broadcasted_iota(jnp.int32, v.shape, 1)
    return jnp.sum(jnp.where(lane == j, v, 0.0), axis=1, keepdims=True)


def _rows_from(pieces, rows, width):
    ridx = lax.broadcasted_iota(jnp.int32, (rows, width), 0)
    out = jnp.zeros((rows, width), f32)
    for h, p in enumerate(pieces):
        out = out + jnp.where(ridx == h, jnp.broadcast_to(p, (rows, width)), 0.0)
    return out


def causal_conv(halo8, x, w4):
    t = x.shape[0]
    xe = jnp.concatenate([halo8, x], axis=0)
    y = xe[5:5 + t] * w4[0:1]
    for k in range(1, 4):
        y = y + xe[5 + k:5 + k + t] * w4[k:k + 1]
    return y


def ml_chunk(q, k, v, o_pre, z, small, bias_row, norm_w, C, n, m):
    n_ch = q.shape[0] // LC
    lane = lax.broadcasted_iota(jnp.int32, small.shape, 1)
    pre = small + bias_row
    lg = jnp.where(lane < NH, pre, jnp.where(lane < 2 * NH, jax.nn.log_sigmoid(pre), 0.0))
    rows = [slice(c * LC, (c + 1) * LC) for c in range(n_ch)]
    lgs = [lg[r] for r in rows]
    bcs = [cumsum_rows(x) for x in lgs]
    lg_ts = [x.T for x in lgs]
    bc_ts = [x.T for x in bcs]
    causal = _tri(LC)
    pairs = [(c, h) for c in range(n_ch) for h in range(NH)]
    idx = range(len(pairs))
    sls = [slice(h * HD, (h + 1) * HD) for h in range(NH)]
    qs = [q[rows[c], sls[h]] * (HD ** -0.5) for c, h in pairs]
    ks = [k[rows[c], sls[h]] for c, h in pairs]
    vs = [v[rows[c], sls[h]] for c, h in pairs]
    li_cols = [_lane_col(lgs[c], h) for c, h in pairs]
    b_cols = [_lane_col(bcs[c], NH + h) for c, h in pairs]
    dms = [jnp.where(causal, b_cols[i] - bc_ts[c][NH + h:NH + h + 1, :] + lg_ts[c][h:h + 1, :], -jnp.inf)
           for i, (c, h) in enumerate(pairs)]
    dm_maxs = [jnp.max(dms[i], axis=-1, keepdims=True) for i in idx]
    gs = [b_cols[i][LC - 1:LC, :] for i in idx]
    wss = [gs[i] - b_cols[i] + li_cols[i] for i in idx]
    ws_maxs = [jnp.max(wss[i], axis=0, keepdims=True) for i in idx]
    qks = [mm_nt(qs[i], ks[i]) for i in idx]
    m_in = [None] * len(pairs)
    m_out = [None] * len(pairs)
    for h in range(NH):
        cur = m[h:h + 1, 0:1]
        for c in range(n_ch):
            i = c * NH + h
            m_in[i] = cur
            cur = jnp.maximum(gs[i] + cur, ws_maxs[i])
            m_out[i] = cur
    m_inters = [b_cols[i] + m_in[i] for i in idx]
    m_ts = [jnp.maximum(m_inters[i], dm_maxs[i]) for i in idx]
    ss = [qks[i] * jnp.exp(dms[i] - m_ts[i]) for i in idx]
    scs = [jnp.exp(m_inters[i] - m_ts[i]) for i in idx]
    decs = [jnp.exp(gs[i] + m_in[i] - m_out[i]) for i in idx]
    kws = [jnp.exp(wss[i] - m_out[i]) * ks[i] for i in idx]
    c_adds = [mm_tn(kws[i], vs[i]) for i in idx]
    n_adds = [jnp.sum(kws[i], axis=0, keepdims=True) for i in idx]
    svs = [mm_nn(ss[i], vs[i]) for i in idx]
    s_sums = [jnp.sum(ss[i], axis=-1, keepdims=True) for i in idx]
    c_hs = [C[sl, :] for sl in sls]
    n_hs = [n[h:h + 1, :] for h in range(NH)]
    hhs = [None] * len(pairs)
    for c in range(n_ch):
        for h in range(NH):
            i = c * NH + h
            num = svs[i] + scs[i] * mm_nn(qs[i], c_hs[h])
            den = s_sums[i] + scs[i] * jnp.sum(qs[i] * n_hs[h], axis=-1, keepdims=True)
            hhs[i] = num / jnp.maximum(jnp.abs(den), jnp.exp(-m_ts[i]))
        c_hs = [decs[c * NH + h] * c_hs[h] + c_adds[c * NH + h] for h in range(NH)]
        n_hs = [decs[c * NH + h] * n_hs[h] + n_adds[c * NH + h] for h in range(NH)]
    ys = [hhs[i] * lax.rsqrt(jnp.mean(hhs[i] * hhs[i], axis=-1, keepdims=True) + EPS) * norm_w[:, sls[h]]
          * jax.nn.sigmoid(o_pre[rows[c], sls[h]]) * jax.nn.silu(z[rows[c], sls[h]]) for i, (c, h) in enumerate(pairs)]
    y = jnp.concatenate([jnp.concatenate(ys[c * NH:(c + 1) * NH], axis=1) for c in range(n_ch)], axis=0)
    last = (n_ch - 1) * NH
    m_rows = [jnp.broadcast_to(m_out[last + h], (1, HD)) for h in range(NH)]
    return y, jnp.concatenate(c_hs, axis=0), _rows_from(n_hs, 8, HD), _rows_from(m_rows, 8, HD)


@jax.custom_vjp
def _unit_lower_inverses(mats):
    eye = (lax.broadcasted_iota(jnp.int32, (LC, LC), 0) == lax.broadcasted_iota(jnp.int32, (LC, LC), 1)).astype(f32)
    ps = [-m for m in mats]
    ts = [eye + p for p in ps]
    for _ in range(5):
        ps = [dot3(p, p) for p in ps]
        ts = [t + dot3(t, p) for t, p in zip(ts, ps)]
    return tuple(ts)


def _unit_lower_inverses_fwd(mats):
    ts = _unit_lower_inverses(mats)
    return ts, ts


def _unit_lower_inverses_bwd(ts, gs):
    tts = [t.T for t in ts]
    mid = [dot3(tt, g) for tt, g in zip(tts, gs)]
    return (tuple(-dot3(m, tt) for m, tt in zip(mid, tts)),)


_unit_lower_inverses.defvjp(_unit_lower_inverses_fwd, _unit_lower_inverses_bwd)


@jax.custom_vjp
def _known_inverses(mats, ts):
    return tuple(ts)


_known_inverses.defvjp(lambda mats, ts: (tuple(ts), tuple(ts)),
                       lambda ts, gs: (_unit_lower_inverses_bwd(ts, gs)[0], tuple(jnp.zeros_like(t) for t in ts)))


def gd_chunk(qh8, q, kh8, k, vh8, v, z, small, conv_w, alog_row, dt_row, norm_w, st, known_t=None):
    n_ch = q.shape[0] // LC
    lane = lax.broadcasted_iota(jnp.int32, small.shape, 1)
    is_a = (lane >= 2 * NH) & (lane < 3 * NH)
    g_all = jnp.where(is_a, -jnp.exp(alog_row) * jax.nn.softplus(small + dt_row), 0.0)
    beta_all = jax.nn.sigmoid(small)
    qc = jax.nn.silu(causal_conv(qh8, q, conv_w[:, 0:SEG]))
    kc = jax.nn.silu(causal_conv(kh8, k, conv_w[:, SEG:2 * SEG]))
    vc = jax.nn.silu(causal_conv(vh8, v, conv_w[:, 2 * SEG:3 * SEG]))
    incl = _tri(LC)
    strict = _tri(LC, strict=True)
    rows = [slice(c * LC, (c + 1) * LC) for c in range(n_ch)]
    gc_alls = [cumsum_rows(g_all[r]) for r in rows]
    gc_ts = [g.T for g in gc_alls]
    pairs = [(c, h) for c in range(n_ch) for h in range(NH)]
    idx = range(len(pairs))
    sls = [slice(h * HD, (h + 1) * HD) for h in range(NH)]
    qs = [qc[rows[c], sls[h]] for c, h in pairs]
    ks = [kc[rows[c], sls[h]] for c, h in pairs]
    vs = [vc[rows[c], sls[h]] for c, h in pairs]
    qs = [x * lax.rsqrt(jnp.sum(x * x, axis=-1, keepdims=True) + EPS) * (HD ** -0.5) for x in qs]
    ks = [x * lax.rsqrt(jnp.sum(x * x, axis=-1, keepdims=True) + EPS) for x in ks]
    betas = [_lane_col(beta_all[rows[c]], 3 * NH + h) for c, h in pairs]
    gcs = [_lane_col(gc_alls[c], 2 * NH + h) for c, h in pairs]
    gams = [jnp.exp(jnp.where(incl, gcs[i] - gc_ts[c][2 * NH + h:2 * NH + h + 1, :], -jnp.inf))
            for i, (c, h) in enumerate(pairs)]
    kbs = [ks[i] * betas[i] for i in idx]
    mats = tuple(jnp.where(strict, mm_nt(kbs[i], ks[i]) * gams[i], 0.0) for i in idx)
    aqks = [mm_nt(qs[i], ks[i]) * gams[i] for i in idx]
    if known_t is None:
        t_invs = _unit_lower_inverses(mats)
    else:
        t_invs = _known_inverses(mats, tuple(known_t[i] for i in idx))
    us = [mm_nn(t_invs[i], vs[i] * betas[i]) for i in idx]
    ws = [mm_nn(t_invs[i], kbs[i] * jnp.exp(gcs[i])) for i in idx]
    g_lasts = [gcs[i][LC - 1:LC, :] for i in idx]
    q_decs = [qs[i] * jnp.exp(gcs[i]) for i in idx]
    k_decs = [ks[i] * jnp.exp(g_lasts[i] - gcs[i]) for i in idx]
    e_lasts = [jnp.exp(g_lasts[i]) for i in idx]
    s_hs = [st[sl, :] for sl in sls]
    os_ = [None] * len(pairs)
    for c in range(n_ch):
        ids = [c * NH + h for h in range(NH)]
        v_news = [us[i] - mm_nn(ws[i], s_hs[h]) for h, i in enumerate(ids)]
        for h, i in enumerate(ids):
            os_[i] = mm_nn(q_decs[i], s_hs[h]) + mm_nn(aqks[i], v_news[h])
        s_hs = [s_hs[h] * e_lasts[i] + mm_tn(k_decs[i], v_news[h]) for h, i in enumerate(ids)]
    ys = [os_[i] * lax.rsqrt(jnp.mean(os_[i] * os_[i], axis=-1, keepdims=True) + EPS) * norm_w
          * jax.nn.silu(z[rows[c], sls[h]]) for i, (c, h) in enumerate(pairs)]
    y = jnp.concatenate([jnp.concatenate(ys[c * NH:(c + 1) * NH], axis=1) for c in range(n_ch)], axis=0)
    if known_t is None:
        return y, jnp.concatenate(s_hs, axis=0), jnp.stack(t_invs)
    return y, jnp.concatenate(s_hs, axis=0)


def rg_pre(xh8, x, conv_w, conv_b, gw_r, gw_i, gate_b, lam):
    xc = causal_conv(xh8, x, conv_w) + conv_b
    r = jax.nn.sigmoid(mm_nn(xc, gw_r) + gate_b[0:1])
    i = jax.nn.sigmoid(mm_nn(xc, gw_i) + gate_b[1:2])
    log_a = -RG_C * r * jax.nn.softplus(-lam)
    a = jnp.exp(log_a)
    th = jnp.tanh(log_a)
    one_minus_a2 = -2.0 * th / (1.0 - th)
    b = jnp.sqrt(one_minus_a2) * (i * xc)
    return a, b


def _scan_rows(a_ref, b_ref, out_ref, h0, n_rows, reverse):
    n_groups = n_rows // 8
    width = a_ref.shape[1]
    row = lax.broadcasted_iota(jnp.int32, (8, width), 0)

    def body(j, h):
        g = (n_groups - 1 - j) if reverse else j
        r0 = pl.multiple_of(g * 8, 8)
        av = a_ref[pl.ds(r0, 8), :]
        bv = b_ref[pl.ds(r0, 8), :]
        for d in (1, 2, 4):
            sh = (8 - d) if reverse else d
            a_s = pltpu.roll(av, sh, 0)
            b_s = pltpu.roll(bv, sh, 0)
            valid = (row < 8 - d) if reverse else (row >= d)
            bv = jnp.where(valid, av * b_s + bv, bv)
            av = jnp.where(valid, av * a_s, av)
        hv = av * h + bv
        out_ref[pl.ds(r0, 8), :] = hv
        return hv[0:1, :] if reverse else hv[7:8, :]

    return lax.fori_loop(0, n_groups, body, h0)


def _norm_rows(xv):
    r = lax.rsqrt(jnp.mean(xv * xv, axis=-1, keepdims=True) + EPS)
    return xv * r, r


def inproj(x, nw, w, xf=None):
    s, ts = x.shape[0], 256
    nt = s // ts
    n_in, n_out = 3, 2
    x_ops, x_in_specs, x_out_shapes, x_out_specs, aliases, x_sems, n_xin, n_land = _carried(xf, n_in, n_out)

    def body(*refs):
        (x_ref, nw_ref, w_ref), x_refs, (o_ref, ht_ref), land_refs, _, sem_refs = _cut_refs(refs, n_in, n_xin, n_out, n_land, 0)
        i = pl.program_id(0)
        _carry_open(xf, i, nt, x_refs, land_refs, sem_refs)
        xn, _ = _norm_rows(x_ref[...])
        hn = xn * nw_ref[...]
        o_ref[...] = jnp.dot(hn.astype(bf16), w_ref[...], preferred_element_type=f32)
        ht_ref[...] = hn.T.astype(bf16)
        _carry_close(xf, i, nt, x_refs, land_refs, sem_refs)

    return pl.pallas_call(
        body, grid=(nt,),
        in_specs=[pl.BlockSpec((ts, D), lambda i: (i, 0)), pl.BlockSpec((1, D), lambda i: (0, 0)),
                  pl.BlockSpec((D, DP), lambda i: (0, 0))] + x_in_specs,
        out_specs=[pl.BlockSpec((ts, DP), lambda i: (i, 0)), pl.BlockSpec((D, ts), lambda i: (0, i))] + x_out_specs,
        out_shape=[jax.ShapeDtypeStruct((s, DP), f32), jax.ShapeDtypeStruct((D, s), bf16)] + x_out_shapes,
        scratch_shapes=x_sems, input_output_aliases=aliases,
        name="inproj" if xf is None else "inproj_carrying", compiler_params=_cp(56))(x, nw, w, *x_ops)


def outproj(x, yr, ym, yg, wo):
    s, ts = x.shape[0], 256

    def body(x_ref, yr_ref, ym_ref, yg_ref, wo_ref, o_ref):
        acc = x_ref[...]
        for j, y_ref in enumerate((yr_ref, ym_ref, yg_ref)):
            acc = acc + jnp.dot(y_ref[...].astype(bf16), wo_ref[j * SEG:(j + 1) * SEG, :], preferred_element_type=f32)
        o_ref[...] = acc

    yspec = pl.BlockSpec((ts, SEG), lambda i: (i, 0))
    return pl.pallas_call(
        body, grid=(s // ts,),
        in_specs=[pl.BlockSpec((ts, D), lambda i: (i, 0)), yspec, yspec, yspec,
                  pl.BlockSpec((3 * SEG, D), lambda i: (0, 0))],
        out_specs=pl.BlockSpec((ts, D), lambda i: (i, 0)),
        out_shape=jax.ShapeDtypeStruct((s, D), f32), name="outproj", compiler_params=_cp(40))(x, yr, ym, yg, wo)


def head(x, fw, tgt):
    s, ts = x.shape[0], 256

    def body(x_ref, fw_ref, t_ref, dx_ref, loss_ref, dfw_ref):
        @pl.when(pl.program_id(0) == 0)
        def _():
            loss_ref[...] = jnp.zeros_like(loss_ref)
            dfw_ref[...] = jnp.zeros_like(dfw_ref)

        xn, r = _norm_rows(x_ref[...])
        fw_v = fw_ref[...]
        err = xn * fw_v - t_ref[...]
        loss_ref[...] += 0.5 * jnp.sum(jnp.mean(err * err, axis=-1, keepdims=True))
        dy = err * (1.0 / D)
        dfw_ref[0:1, :] += jnp.sum(dy * xn, axis=0, keepdims=True)
        dxn = dy * fw_v
        dx_ref[...] = r * (dxn - xn * jnp.mean(dxn * xn, axis=-1, keepdims=True))

    tile = pl.BlockSpec((ts, D), lambda i: (i, 0))
    return pl.pallas_call(
        body, grid=(s // ts,),
        in_specs=[tile, pl.BlockSpec((1, D), lambda i: (0, 0)), tile],
        out_specs=[tile, pl.BlockSpec((8, 128), lambda i: (0, 0)), pl.BlockSpec((8, D), lambda i: (0, 0))],
        out_shape=[jax.ShapeDtypeStruct((s, D), f32), jax.ShapeDtypeStruct((8, 128), f32),
                   jax.ShapeDtypeStruct((8, D), f32)], name="head")(x, fw, tgt)


def outproj_bwd(dx, yr, ym, yg, wo):
    s, ts = dx.shape[0], 256

    def body(dx_ref, yr_ref, ym_ref, yg_ref, wo_ref, dyr_ref, dym_ref, dyg_ref, gwo_ref):
        @pl.when(pl.program_id(0) == 0)
        def _():
            gwo_ref[...] = jnp.zeros_like(gwo_ref)

        dxb = dx_ref[...].astype(bf16)
        for j, (y_ref, dy_ref) in enumerate(((yr_ref, dyr_ref), (ym_ref, dym_ref), (yg_ref, dyg_ref))):
            rows = slice(j * SEG, (j + 1) * SEG)
            dy_ref[...] = lax.dot_general(dxb, wo_ref[rows, :], (((1,), (1,)), ((), ())), preferred_element_type=f32)
            gwo_ref[rows, :] += lax.dot_general(y_ref[...].astype(bf16), dxb, (((0,), (0,)), ((), ())),
                                                preferred_element_type=f32)

    yspec = pl.BlockSpec((ts, SEG), lambda i: (i, 0))
    wspec = pl.BlockSpec((3 * SEG, D), lambda i: (0, 0))
    ysh = jax.ShapeDtypeStruct((s, SEG), f32)
    return pl.pallas_call(
        body, grid=(s // ts,),
        in_specs=[pl.BlockSpec((ts, D), lambda i: (i, 0)), yspec, yspec, yspec, wspec],
        out_specs=[yspec, yspec, yspec, wspec],
        out_shape=[ysh, ysh, ysh, jax.ShapeDtypeStruct((3 * SEG, D), f32)],
        name="outproj_bwd", compiler_params=_cp(48))(dx, yr, ym, yg, wo)


def inproj_bwd_x(x, nw, w, dxo, d_rg, d_ml, d_gd, d_sa, d_sb):
    s, ts = x.shape[0], 256
    widths = (d_rg.shape[1], d_ml.shape[1], d_gd.shape[1], HD)

    def body(x_ref, nw_ref, w_ref, dxo_ref, rg_ref, ml_ref, gd_ref, sa_ref, sb_ref, dx_ref, dnw_ref):
        @pl.when(pl.program_id(0) == 0)
        def _():
            dnw_ref[...] = jnp.zeros_like(dnw_ref)

        xn, r = _norm_rows(x_ref[...])
        pieces = (rg_ref[...], ml_ref[...], gd_ref[...], sa_ref[...] + sb_ref[...])
        dhn = jnp.zeros((ts, D), f32)
        c0 = 0
        for piece, wd in zip(pieces, widths):
            dhn = dhn + lax.dot_general(piece.astype(bf16), w_ref[:, c0:c0 + wd], (((1,), (1,)), ((), ())),
                                        preferred_element_type=f32)
            c0 += wd
        dnw_ref[0:1, :] += jnp.sum(dhn * xn, axis=0, keepdims=True)
        dxn = dhn * nw_ref[...]
        dx_ref[...] = dxo_ref[...] + r * (dxn - xn * jnp.mean(dxn * xn, axis=-1, keepdims=True))

    tile = pl.BlockSpec((ts, D), lambda i: (i, 0))
    return pl.pallas_call(
        body, grid=(s // ts,),
        in_specs=[tile, pl.BlockSpec((1, D), lambda i: (0, 0)), pl.BlockSpec((D, DP), lambda i: (0, 0)), tile]
        + [pl.BlockSpec((ts, wd), lambda i: (i, 0)) for wd in widths] + [pl.BlockSpec((ts, HD), lambda i: (i, 0))],
        out_specs=[tile, pl.BlockSpec((8, D), lambda i: (0, 0))],
        out_shape=[jax.ShapeDtypeStruct((s, D), f32), jax.ShapeDtypeStruct((8, D), f32)],
        name="inproj_bwd_x", compiler_params=_cp(56))(x, nw, w, dxo, d_rg, d_ml, d_gd, d_sa, d_sb)


def wgrad(hn_t, dps, name):
    s = hn_t.shape[1]
    c = dps[0].shape[1]
    ct = min(c, SEG)
    n_dp = len(dps)

    def body(*refs):
        ht_ref = refs[0]
        dp_refs = refs[1:1 + n_dp]
        o_ref = refs[1 + n_dp]
        dp = dp_refs[0][...]
        for extra in dp_refs[1:]:
            dp = dp + extra[...]
        o_ref[...] = jnp.dot(ht_ref[...], dp.astype(bf16), preferred_element_type=f32)

    return pl.pallas_call(
        body, grid=(c // ct,),
        in_specs=[pl.BlockSpec((D, s), lambda j: (0, 0))] + [pl.BlockSpec((s, ct), lambda j: (0, j)) for _ in dps],
        out_specs=pl.BlockSpec((D, ct), lambda j: (0, j)),
        out_shape=jax.ShapeDtypeStruct((D, c), f32), name=name, compiler_params=_cp(40))(hn_t, *dps)


def _seg_spec(rows, seg, n_tiles=None):
    if n_tiles is None:
        return pl.BlockSpec((rows, SEG), lambda i: (i, seg))
    return pl.BlockSpec((rows, SEG), lambda i: (n_tiles - 1 - i, seg))


def _halo_spec(rows, seg, n_tiles=None):
    per = rows // 8
    if n_tiles is None:
        return pl.BlockSpec((8, SEG), lambda i: (jnp.maximum(i * per - 1, 0), seg))
    return pl.BlockSpec((8, SEG), lambda i: (jnp.maximum((n_tiles - 1 - i) * per - 1, 0), seg))


def _const_spec(shape):
    return pl.BlockSpec(shape, lambda i: tuple(0 for _ in shape))


def rglru_fwd(proj, conv_w, conv_b, gw_r, gw_i, gate_b, lam):
    s = proj.shape[0]
    tr = RG_TILE

    def body(xh_ref, x_ref, z_ref, cw_ref, cb_ref, gr_ref, gi_ref, gb_ref, lam_ref, y_ref, h_ref, a_s, b_s, hc):
        first = pl.program_id(0) == 0

        @pl.when(first)
        def _():
            hc[...] = jnp.zeros_like(hc)

        xh = jnp.where(first, 0.0, xh_ref[...])
        a, b = rg_pre(xh, x_ref[...], cw_ref[...], cb_ref[...], gr_ref[...], gi_ref[...], gb_ref[...], lam_ref[...])
        a_s[...] = a
        b_s[...] = b
        hc[0:1, :] = _scan_rows(a_s, b_s, h_ref, hc[0:1, :], tr, False)
        y_ref[...] = (h_ref[...] * jax.nn.silu(z_ref[...])).astype(bf16)

    out = pl.BlockSpec((tr, SEG), lambda i: (i, 0))
    return pl.pallas_call(
        body, grid=(s // tr,),
        in_specs=[_halo_spec(tr, 0), _seg_spec(tr, 0), _seg_spec(tr, 1), _const_spec((4, SEG)), _const_spec((1, SEG)),
                  _const_spec((SEG, SEG)), _const_spec((SEG, SEG)), _const_spec((2, SEG)), _const_spec((1, SEG))],
        out_specs=[out, out],
        out_shape=[jax.ShapeDtypeStruct((s, SEG), bf16), jax.ShapeDtypeStruct((s, SEG), f32)],
        scratch_shapes=[pltpu.VMEM((tr, SEG), f32), pltpu.VMEM((tr, SEG), f32), pltpu.VMEM((8, SEG), f32)],
        name="rglru_fwd", compiler_params=_cp(40))(proj, proj, proj, conv_w, conv_b, gw_r, gw_i, gate_b, lam)


def rglru_bwd(proj, hs, dy, conv_w, conv_b, gw_r, gw_i, gate_b, lam, xf=None):
    s = proj.shape[0]
    tr = RG_TILE
    nt = s // tr
    n_in, n_out = 12, 7
    x_ops, x_in_specs, x_out_shapes, x_out_specs, aliases, x_sems, n_xin, n_land = _carried(xf, n_in, n_out)

    def body(*refs):
        ins, x_refs, outs, land_refs, scratch, sem_refs = _cut_refs(refs, n_in, n_xin, n_out, n_land, 6)
        xh_ref, x_ref, z_ref, hh_ref, h_ref, dy_ref, cw_ref, cb_ref, gr_ref, gi_ref, gb_ref, lam_ref = ins
        dp_ref, dcw_ref, dcb_ref, dgr_ref, dgi_ref, dgb_ref, dlam_ref = outs
        an_s, g_s, dh_s, a_first, dh_first, dhalo = scratch
        i = pl.program_id(0)
        first_tile = i == nt - 1
        _carry_open(xf, i, nt, x_refs, land_refs, sem_refs)

        @pl.when(i == 0)
        def _():
            for ref in (dcw_ref, dcb_ref, dgr_ref, dgi_ref, dgb_ref, dlam_ref, a_first, dh_first, dhalo):
                ref[...] = jnp.zeros_like(ref)

        xh = jnp.where(first_tile, 0.0, xh_ref[...])
        params = (cw_ref[...], cb_ref[...], gr_ref[...], gi_ref[...], gb_ref[...], lam_ref[...])
        (a, _), vjp = jax.vjp(rg_pre, xh, x_ref[...], *params)
        zv = z_ref[...]
        hv = h_ref[...]
        dyv = dy_ref[...]
        sig = jax.nn.sigmoid(zv)
        g_s[...] = dyv * (zv * sig)
        dp_ref[:, SEG:2 * SEG] = (dyv * hv * (sig * (1.0 + zv * (1.0 - sig)))).astype(bf16)
        ridx = lax.broadcasted_iota(jnp.int32, (tr, SEG), 0)
        an_s[...] = jnp.where(ridx == tr - 1, jnp.broadcast_to(a_first[0:1, :], (tr, SEG)), pltpu.roll(a, tr - 1, 0))
        _scan_rows(an_s, g_s, dh_s, dh_first[0:1, :], tr, True)
        dh = dh_s[...]
        h_prev_last = jnp.where(first_tile, 0.0, hh_ref[...])[7:8, :]
        h_prev = pltpu.roll(hv, 1, 0)
        h_prev = jnp.where(ridx == 0, jnp.broadcast_to(h_prev_last, (tr, SEG)), h_prev)
        dxh, dx, dcw, dcb, dgr, dgi, dgb, dlam = vjp((dh * h_prev, dh))
        dx = dx + jnp.concatenate([jnp.zeros((tr - 8, SEG), f32), dhalo[...]], axis=0)
        dp_ref[:, 0:SEG] = dx.astype(bf16)
        dhalo[...] = dxh
        a_first[0:1, :] = a[0:1, :]
        dh_first[0:1, :] = dh[0:1, :]
        dcw_ref[...] += dcw
        dcb_ref[...] += dcb
        dgr_ref[...] += dgr
        dgi_ref[...] += dgi
        dgb_ref[...] += dgb
        dlam_ref[...] += dlam
        _carry_close(xf, i, nt, x_refs, land_refs, sem_refs)

    pspecs = [_const_spec((4, SEG)), _const_spec((1, SEG)), _const_spec((SEG, SEG)), _const_spec((SEG, SEG)),
              _const_spec((2, SEG)), _const_spec((1, SEG))]
    pshapes = [jax.ShapeDtypeStruct(sh, f32) for sh in ((4, SEG), (1, SEG), (SEG, SEG), (SEG, SEG), (2, SEG), (1, SEG))]
    tile = pl.BlockSpec((tr, SEG), lambda i: (nt - 1 - i, 0))
    return pl.pallas_call(
        body, grid=(nt,),
        in_specs=[_halo_spec(tr, 0, nt), _seg_spec(tr, 0, nt), _seg_spec(tr, 1, nt),
                  pl.BlockSpec((8, SEG), lambda i: (jnp.maximum((nt - 1 - i) * (tr // 8) - 1, 0), 0)), tile, tile] + pspecs
        + x_in_specs,
        out_specs=[pl.BlockSpec((tr, 2 * SEG), lambda i: (nt - 1 - i, 0))] + pspecs + x_out_specs,
        out_shape=[jax.ShapeDtypeStruct((s, 2 * SEG), bf16)] + pshapes + x_out_shapes,
        scratch_shapes=[pltpu.VMEM((tr, SEG), f32), pltpu.VMEM((tr, SEG), f32), pltpu.VMEM((tr, SEG), f32),
                        pltpu.VMEM((8, SEG), f32), pltpu.VMEM((8, SEG), f32), pltpu.VMEM((8, SEG), f32)] + x_sems,
        input_output_aliases=aliases, name="rglru_bwd" if xf is None else "rglru_bwd_carrying", compiler_params=_cp(48))(
            proj, proj, proj, hs, hs, dy, conv_w, conv_b, gw_r, gw_i, gate_b, lam, *x_ops)


ML_SEGS = (2, 3, 4, 5, 6)
SMALL_BLK = SMALL0 // HD


def _cut_refs(refs, n_in, n_xin, n_out, n_land, n_scratch):
    bounds = [0, n_in, n_in + n_xin, n_in + n_xin + n_out, n_in + n_xin + n_out + n_land,
              n_in + n_xin + n_out + n_land + n_scratch, len(refs)]
    return [refs[a:b] for a, b in zip(bounds[:-1], bounds[1:])]


def mlstm_fwd(proj, bias_row, norm_w, xf=None):
    s = proj.shape[0]
    nc = s // LR
    n_in, n_out = 8, 4
    x_ops, x_in_specs, x_out_shapes, x_out_specs, aliases, x_sems, n_xin, n_land = _carried(xf, n_in, n_out)

    def body(*refs):
        ins, x_refs, outs, land_refs, scratch, sem_refs = _cut_refs(refs, n_in, n_xin, n_out, n_land, 3)
        q_ref, k_ref, v_ref, o_ref, z_ref, sm_ref, b_ref, nw_ref = ins
        y_ref, cs_ref, ns_ref, ms_ref = outs
        c_s, n_s, m_s = scratch
        i = pl.program_id(0)
        _carry_open(xf, i, nc, x_refs, land_refs, sem_refs)

        @pl.when(i == 0)
        def _():
            c_s[...] = jnp.zeros_like(c_s)
            n_s[...] = jnp.zeros_like(n_s)
            m_s[...] = jnp.zeros_like(m_s)

        cs_ref[0] = c_s[...]
        ns_ref[0] = n_s[...]
        ms_ref[0] = m_s[...]
        y, c2, n2, m2 = ml_chunk(q_ref[...], k_ref[...], v_ref[...], o_ref[...], z_ref[...], sm_ref[...],
                                 b_ref[...], nw_ref[...], c_s[...], n_s[...], m_s[...])
        y_ref[...] = y.astype(bf16)
        c_s[...] = c2
        n_s[...] = n2
        m_s[...] = m2
        _carry_close(xf, i, nc, x_refs, land_refs, sem_refs)

    return pl.pallas_call(
        body, grid=(nc,),
        in_specs=[_seg_spec(LR, sg) for sg in ML_SEGS]
        + [pl.BlockSpec((LR, HD), lambda i: (i, SMALL_BLK)), _const_spec((1, HD)), _const_spec((1, SEG))] + x_in_specs,
        out_specs=[pl.BlockSpec((LR, SEG), lambda i: (i, 0)), pl.BlockSpec((1, SEG, HD), lambda i: (i, 0, 0)),
                   pl.BlockSpec((1, 8, HD), lambda i: (i, 0, 0)), pl.BlockSpec((1, 8, HD), lambda i: (i, 0, 0))] + x_out_specs,
        out_shape=[jax.ShapeDtypeStruct((s, SEG), bf16), jax.ShapeDtypeStruct((nc, SEG, HD), f32),
                   jax.ShapeDtypeStruct((nc, 8, HD), f32), jax.ShapeDtypeStruct((nc, 8, HD), f32)] + x_out_shapes,
        scratch_shapes=[pltpu.VMEM((SEG, HD), f32), pltpu.VMEM((8, HD), f32), pltpu.VMEM((8, HD), f32)] + x_sems,
        input_output_aliases=aliases, name="mlstm_fwd" if xf is None else "mlstm_fwd_carrying")(
            proj, proj, proj, proj, proj, proj, bias_row, norm_w, *x_ops)


def mlstm_bwd(proj, dy, cs, ns, ms, bias_row, norm_w, xf=None):
    s = proj.shape[0]
    nc = s // LR
    n_in, n_out = 12, 4
    x_ops, x_in_specs, x_out_shapes, x_out_specs, aliases, x_sems, n_xin, n_land = _carried(xf, n_in, n_out)

    def body(*refs):
        ins, x_refs, outs, land_refs, scratch, sem_refs = _cut_refs(refs, n_in, n_xin, n_out, n_land, 3)
        q_ref, k_ref, v_ref, o_ref, z_ref, sm_ref, dy_ref, cs_ref, ns_ref, ms_ref, b_ref, nw_ref = ins
        dp_ref, dsm_ref, db_ref, dnw_ref = outs
        dc_s, dn_s, dm_s = scratch
        i = pl.program_id(0)
        _carry_open(xf, i, nc, x_refs, land_refs, sem_refs)

        @pl.when(i == 0)
        def _():
            for ref in (db_ref, dnw_ref, dc_s, dn_s, dm_s):
                ref[...] = jnp.zeros_like(ref)

        _, vjp = jax.vjp(ml_chunk, q_ref[...], k_ref[...], v_ref[...], o_ref[...], z_ref[...], sm_ref[...],
                         b_ref[...], nw_ref[...], cs_ref[0], ns_ref[0], ms_ref[0])
        dq, dk, dv, do, dz, dsm, db, dnw, dc, dn, dm = vjp((dy_ref[...], dc_s[...], dn_s[...], dm_s[...]))
        for j, val in enumerate((dq, dk, dv, do, dz)):
            dp_ref[:, j * SEG:(j + 1) * SEG] = val.astype(bf16)
        dsm_ref[...] = dsm
        db_ref[0:1, :] += db
        dnw_ref[0:1, :] += dnw
        dc_s[...] = dc
        dn_s[...] = dn
        dm_s[...] = dm
        _carry_close(xf, i, nc, x_refs, land_refs, sem_refs)

    rev3 = lambda i: (nc - 1 - i, 0, 0)
    return pl.pallas_call(
        body, grid=(nc,),
        in_specs=[_seg_spec(LR, sg, nc) for sg in ML_SEGS]
        + [pl.BlockSpec((LR, HD), lambda i: (nc - 1 - i, SMALL_BLK)), pl.BlockSpec((LR, SEG), lambda i: (nc - 1 - i, 0)),
           pl.BlockSpec((1, SEG, HD), rev3), pl.BlockSpec((1, 8, HD), rev3), pl.BlockSpec((1, 8, HD), rev3),
           _const_spec((1, HD)), _const_spec((1, SEG))] + x_in_specs,
        out_specs=[pl.BlockSpec((LR, 5 * SEG), lambda i: (nc - 1 - i, 0)), pl.BlockSpec((LR, HD), lambda i: (nc - 1 - i, 0)),
                   _const_spec((8, HD)), _const_spec((8, SEG))] + x_out_specs,
        out_shape=[jax.ShapeDtypeStruct((s, 5 * SEG), bf16), jax.ShapeDtypeStruct((s, HD), f32),
                   jax.ShapeDtypeStruct((8, HD), f32), jax.ShapeDtypeStruct((8, SEG), f32)] + x_out_shapes,
        scratch_shapes=[pltpu.VMEM((SEG, HD), f32), pltpu.VMEM((8, HD), f32), pltpu.VMEM((8, HD), f32)] + x_sems,
        input_output_aliases=aliases, name="mlstm_bwd" if xf is None else "mlstm_bwd_carrying", compiler_params=_cp(48))(
            proj, proj, proj, proj, proj, proj, dy, cs, ns, ms, bias_row, norm_w, *x_ops)


GD_SEGS = (7, 8, 9)


def _carried(xf, n_in, n_out):
    operands, out_shapes, aliases, sems, _, n_xin, n_land = _carry_plumb(xf, n_in, n_out)
    return operands, [_ANY] * n_xin, out_shapes, [_ANY] * n_land, aliases, sems, n_xin, n_land


def _carry_open(xf, i, n_steps, x_refs, land_refs, sem_refs):
    if xf is None:
        return
    srcs = x_refs[:len(xf.srcs)]

    @pl.when(i == 0)
    def _():
        _carry_start(xf, srcs, land_refs, sem_refs)

    @pl.when(i == max(n_steps - 2, 0))
    def _():
        _carry_middle(xf, srcs, land_refs, sem_refs)


def _carry_close(xf, i, n_steps, x_refs, land_refs, sem_refs):
    if xf is None:
        return

    @pl.when(i == n_steps - 1)
    def _():
        _carry_finish(xf, x_refs[:len(xf.srcs)], land_refs, sem_refs)


def gdn_fwd(proj, conv_w, alog_row, dt_row, norm_w, xf=None):
    s = proj.shape[0]
    nc = s // LR
    n_in, n_out = 12, 3
    x_ops, x_in_specs, x_out_shapes, x_out_specs, aliases, x_sems, n_xin, n_land = _carried(xf, n_in, n_out)

    def body(*refs):
        qh_ref, q_ref, kh_ref, k_ref, vh_ref, v_ref, z_ref, sm_ref, cw_ref, al_ref, dt_ref, nw_ref = refs[:n_in]
        x_refs = refs[n_in:n_in + n_xin]
        y_ref, ss_ref, ti_ref = refs[n_in + n_xin:n_in + n_xin + n_out]
        land_refs = refs[n_in + n_xin + n_out:n_in + n_xin + n_out + n_land]
        st_s = refs[n_in + n_xin + n_out + n_land]
        sem_refs = refs[n_in + n_xin + n_out + n_land + 1:]
        i = pl.program_id(0)
        first = i == 0
        _carry_open(xf, i, nc, x_refs, land_refs, sem_refs)

        @pl.when(first)
        def _():
            st_s[...] = jnp.zeros_like(st_s)

        ss_ref[0] = st_s[...]
        halo = [jnp.where(first, 0.0, r[...]) for r in (qh_ref, kh_ref, vh_ref)]
        y, st2, t_invs = gd_chunk(halo[0], q_ref[...], halo[1], k_ref[...], halo[2], v_ref[...], z_ref[...], sm_ref[...],
                                  cw_ref[...], al_ref[...], dt_ref[...], nw_ref[...], st_s[...])
        y_ref[...] = y.astype(bf16)
        ti_ref[...] = t_invs
        st_s[...] = st2
        _carry_close(xf, i, nc, x_refs, land_refs, sem_refs)

    qkv_specs = []
    for sg in GD_SEGS:
        qkv_specs += [_halo_spec(LR, sg), _seg_spec(LR, sg)]
    return pl.pallas_call(
        body, grid=(nc,),
        in_specs=qkv_specs + [_seg_spec(LR, 10), pl.BlockSpec((LR, HD), lambda i: (i, SMALL_BLK)),
                              _const_spec((4, 3 * SEG)), _const_spec((1, HD)), _const_spec((1, HD)), _const_spec((1, HD))]
        + x_in_specs,
        out_specs=[pl.BlockSpec((LR, SEG), lambda i: (i, 0)), pl.BlockSpec((1, SEG, HD), lambda i: (i, 0, 0)),
                   pl.BlockSpec((RUN * NH, LC, LC), lambda i: (i, 0, 0))] + x_out_specs,
        out_shape=[jax.ShapeDtypeStruct((s, SEG), bf16), jax.ShapeDtypeStruct((nc, SEG, HD), f32),
                   jax.ShapeDtypeStruct((s // LC * NH, LC, LC), f32)] + x_out_shapes,
        scratch_shapes=[pltpu.VMEM((SEG, HD), f32)] + x_sems, input_output_aliases=aliases,
        name="gdn_fwd" if xf is None else "gdn_fwd_carrying")(
            proj, proj, proj, proj, proj, proj, proj, proj, conv_w, alog_row, dt_row, norm_w, *x_ops)


def gdn_bwd(proj, dy, ss, t_invs, conv_w, alog_row, dt_row, norm_w, xf=None):
    s = proj.shape[0]
    nc = s // LR
    n_in, n_out = 15, 6
    x_ops, x_in_specs, x_out_shapes, x_out_specs, aliases, x_sems, n_xin, n_land = _carried(xf, n_in, n_out)

    def body(*refs):
        (qh_ref, q_ref, kh_ref, k_ref, vh_ref, v_ref, z_ref, sm_ref, dy_ref, ss_ref, ti_ref,
         cw_ref, al_ref, dt_ref, nw_ref) = refs[:n_in]
        x_refs = refs[n_in:n_in + n_xin]
        dp_ref, dsm_ref, dcw_ref, dal_ref, ddt_ref, dnw_ref = refs[n_in + n_xin:n_in + n_xin + n_out]
        land_refs = refs[n_in + n_xin + n_out:n_in + n_xin + n_out + n_land]
        dst_s, dhalo = refs[n_in + n_xin + n_out + n_land:n_in + n_xin + n_out + n_land + 2]
        sem_refs = refs[n_in + n_xin + n_out + n_land + 2:]
        i = pl.program_id(0)
        first_chunk = i == nc - 1
        _carry_open(xf, i, nc, x_refs, land_refs, sem_refs)

        @pl.when(i == 0)
        def _():
            for ref in (dcw_ref, dal_ref, ddt_ref, dnw_ref, dst_s, dhalo):
                ref[...] = jnp.zeros_like(ref)

        halo = [jnp.where(first_chunk, 0.0, r[...]) for r in (qh_ref, kh_ref, vh_ref)]
        with_known = functools.partial(gd_chunk, known_t=ti_ref[...])
        _, vjp = jax.vjp(with_known, halo[0], q_ref[...], halo[1], k_ref[...], halo[2], v_ref[...], z_ref[...], sm_ref[...],
                         cw_ref[...], al_ref[...], dt_ref[...], nw_ref[...], ss_ref[0])
        dqh, dq, dkh, dk, dvh, dv, dz, dsm, dcw, dal, ddt, dnw, dst = vjp((dy_ref[...], dst_s[...]))
        for j, val in enumerate((dq, dk, dv)):
            val = val + jnp.concatenate([jnp.zeros((LR - 8, SEG), f32), dhalo[:, j * SEG:(j + 1) * SEG]], axis=0)
            dp_ref[:, j * SEG:(j + 1) * SEG] = val.astype(bf16)
        dp_ref[:, 3 * SEG:4 * SEG] = dz.astype(bf16)
        for j, val in enumerate((dqh, dkh, dvh)):
            dhalo[:, j * SEG:(j + 1) * SEG] = val
        dsm_ref[...] = dsm
        dcw_ref[...] += dcw
        dal_ref[0:1, :] += dal
        ddt_ref[0:1, :] += ddt
        dnw_ref[0:1, :] += dnw
        dst_s[...] = dst
        _carry_close(xf, i, nc, x_refs, land_refs, sem_refs)

    qkv_specs = []
    for sg in GD_SEGS:
        qkv_specs += [_halo_spec(LR, sg, nc), _seg_spec(LR, sg, nc)]
    return pl.pallas_call(
        body, grid=(nc,),
        in_specs=qkv_specs + [_seg_spec(LR, 10, nc), pl.BlockSpec((LR, HD), lambda i: (nc - 1 - i, SMALL_BLK)),
                              pl.BlockSpec((LR, SEG), lambda i: (nc - 1 - i, 0)),
                              pl.BlockSpec((1, SEG, HD), lambda i: (nc - 1 - i, 0, 0)),
                              pl.BlockSpec((RUN * NH, LC, LC), lambda i: (nc - 1 - i, 0, 0)),
                              _const_spec((4, 3 * SEG)), _const_spec((1, HD)), _const_spec((1, HD)), _const_spec((1, HD))]
        + x_in_specs,
        out_specs=[pl.BlockSpec((LR, 4 * SEG), lambda i: (nc - 1 - i, 0)), pl.BlockSpec((LR, HD), lambda i: (nc - 1 - i, 0)),
                   _const_spec((4, 3 * SEG)), _const_spec((8, HD)), _const_spec((8, HD)), _const_spec((8, HD))] + x_out_specs,
        out_shape=[jax.ShapeDtypeStruct((s, 4 * SEG), bf16), jax.ShapeDtypeStruct((s, HD), f32),
                   jax.ShapeDtypeStruct((4, 3 * SEG), f32), jax.ShapeDtypeStruct((8, HD), f32),
                   jax.ShapeDtypeStruct((8, HD), f32), jax.ShapeDtypeStruct((8, HD), f32)] + x_out_shapes,
        scratch_shapes=[pltpu.VMEM((SEG, HD), f32), pltpu.VMEM((8, 3 * SEG), f32)] + x_sems, input_output_aliases=aliases,
        name="gdn_bwd" if xf is None else "gdn_bwd_carrying", compiler_params=_cp(48))(
            proj, proj, proj, proj, proj, proj, proj, proj, dy, ss, t_invs, conv_w, alog_row, dt_row, norm_w, *x_ops)


def _my_place():
    return lax.axis_index("x"), lax.axis_index("y"), lax.axis_index("c")


def _slot(p):
    return 4 * p[0] + 2 * p[1] + p[2]


def _peer(me, j):
    flips = ((j >> 2) & 1, (j >> 1) & 1, j & 1)
    return tuple((1 - v) if fl else v for v, fl in zip(me, flips))


_ANY = pl.BlockSpec(memory_space=pl.ANY)


class Scatter(NamedTuple):
    srcs: list
    lands: list
    land_of: list
    layer: list
    span: list


class Gather2(NamedTuple):
    srcs: list
    lands: list
    span: list


def _carry_plumb(cx, n_in, n_out):
    if cx is None:
        return [], [], {}, [], 0, 0, 0
    n_src = len(cx.srcs)
    passed = [li for li, ld in enumerate(cx.lands) if not isinstance(ld, jax.ShapeDtypeStruct)]
    operands = list(cx.srcs) + [cx.lands[li] for li in passed]
    aliases = {n_in + n_src + k: n_out + li for k, li in enumerate(passed)}
    out_shapes = [jax.ShapeDtypeStruct(ld.shape, ld.dtype) for ld in cx.lands]
    sems = [pltpu.SemaphoreType.DMA((n_src, N_DEV - 1)), pltpu.SemaphoreType.DMA((n_src, N_DEV - 1)),
            pltpu.SemaphoreType.DMA((n_src,))]
    return operands, out_shapes, aliases, sems, n_src, len(operands), len(cx.lands)


def _scatter_copies(sc, src_refs, land_refs, send_sems, recv_sems, local_sems):
    me = _my_place()
    mine = _slot(me)
    local, remote = [], []
    for a, src_ref in enumerate(src_refs):
        lo, hi = sc.span[a]
        rows = pl.ds(lo, hi - lo)
        dst = land_refs[sc.land_of[a]].at[mine, sc.layer[a], rows]
        local.append(pltpu.make_async_copy(src_ref.at[mine, rows], dst, local_sems.at[a]))
        for j in range(1, N_DEV):
            peer = _peer(me, j)
            remote.append(pltpu.make_async_remote_copy(
                src_ref=src_ref.at[_slot(peer), rows], dst_ref=dst, send_sem=send_sems.at[a, j - 1],
                recv_sem=recv_sems.at[a, j - 1], device_id=peer, device_id_type=MESH))
    return local, remote


def _gather2_copy(land, sems, a, k, block_of, to, rows, src=None):
    dst = land.at[_slot(block_of), rows]
    return pltpu.make_async_remote_copy(src_ref=dst if src is None else src, dst_ref=dst, send_sem=sems[0].at[a, k],
                                        recv_sem=sems[1].at[a, k], device_id=to, device_id_type=MESH)


def _gather2_places():
    x, y, c = _my_place()
    return (x, y, c), (x, y, 1 - c), [(1 - x, y), (x, 1 - y), (1 - x, 1 - y)], c


def _carry_start(cx, src_refs, land_refs, sems):
    if isinstance(cx, Scatter):
        local, remote = _scatter_copies(cx, src_refs, land_refs, *sems)
        for cp in local + remote:
            cp.start()
        return
    me, sib, chips, c = _gather2_places()
    for a, (src_ref, land) in enumerate(zip(src_refs, land_refs)):
        rows = pl.ds(cx.span[a][0], cx.span[a][1] - cx.span[a][0])
        src = src_ref.at[rows]
        pltpu.make_async_copy(src, land.at[_slot(me), rows], sems[2].at[a]).start()
        _gather2_copy(land, sems, a, 0, me, sib, rows, src=src).start()
        for j, chip in enumerate(chips):
            _gather2_copy(land, sems, a, 1 + j, me, (*chip, c), rows, src=src).start()


def _carry_middle(cx, src_refs, land_refs, sems):
    if isinstance(cx, Scatter):
        return
    me, sib, chips, c = _gather2_places()
    for a, land in enumerate(land_refs):
        rows = pl.ds(cx.span[a][0], cx.span[a][1] - cx.span[a][0])
        for j, chip in enumerate(chips):
            _gather2_copy(land, sems, a, 1 + j, (*chip, c), me, rows).wait_recv()
            _gather2_copy(land, sems, a, 4 + j, (*chip, c), sib, rows).start()


def _carry_finish(cx, src_refs, land_refs, sems):
    if isinstance(cx, Scatter):
        local, remote = _scatter_copies(cx, src_refs, land_refs, *sems)
        for cp in remote:
            cp.wait()
        for cp in local:
            cp.wait()
        return
    me, sib, chips, c = _gather2_places()
    for a, (src_ref, land) in enumerate(zip(src_refs, land_refs)):
        rows = pl.ds(cx.span[a][0], cx.span[a][1] - cx.span[a][0])
        src = src_ref.at[rows]
        _gather2_copy(land, sems, a, 0, sib, me, rows).wait_recv()
        for j, chip in enumerate(chips):
            _gather2_copy(land, sems, a, 4 + j, (*chip, 1 - c), me, rows).wait_recv()
        _gather2_copy(land, sems, a, 0, me, sib, rows, src=src).wait_send()
        for j, chip in enumerate(chips):
            _gather2_copy(land, sems, a, 1 + j, me, (*chip, c), rows, src=src).wait_send()
            _gather2_copy(land, sems, a, 4 + j, (*chip, c), sib, rows).wait_send()
        pltpu.make_async_copy(src, land.at[_slot(me), rows], sems[2].at[a]).wait()


def exchange(cx, name):
    operands, out_shapes, aliases, sems, n_src, n_xin, n_land = _carry_plumb(cx, 0, 0)

    def body(*refs):
        src_refs, land_refs, sem_refs = refs[:n_src], refs[n_xin:n_xin + n_land], refs[n_xin + n_land:]
        _carry_start(cx, src_refs, land_refs, sem_refs)
        _carry_middle(cx, src_refs, land_refs, sem_refs)
        _carry_finish(cx, src_refs, land_refs, sem_refs)

    return pl.pallas_call(body, in_specs=[_ANY] * n_xin, out_specs=[_ANY] * n_land, out_shape=out_shapes,
                          scratch_shapes=sems, input_output_aliases=aliases, name=name)(*operands)


def small_allreduce(packed, name, xf=None):
    r = packed.shape[0]
    rc = r // N_DEV
    x_ops, x_in_specs, x_out_shapes, x_out_specs, aliases, x_sems, n_xin, n_land = _carried(xf, 1, 1)

    def body(*refs):
        (in_ref,), x_refs, (out_ref,), land_refs, scratch, sem_refs = _cut_refs(refs, 1, n_xin, 1, n_land, 5)
        recv_buf, send1, recv1, send2, recv2 = scratch
        if xf is not None:
            _carry_start(xf, x_refs[:len(xf.srcs)], land_refs, sem_refs)
        me = _my_place()
        mine = _slot(me)
        my_rows = pl.ds(pl.multiple_of(mine * rc, 8), rc)
        first, second = [], []
        for j in range(1, N_DEV):
            peer = _peer(me, j)
            peer_rows = pl.ds(pl.multiple_of(_slot(peer) * rc, 8), rc)
            first.append(pltpu.make_async_remote_copy(
                src_ref=in_ref.at[peer_rows], dst_ref=recv_buf.at[mine], send_sem=send1.at[j - 1], recv_sem=recv1.at[j - 1],
                device_id=peer, device_id_type=MESH))
            second.append(pltpu.make_async_remote_copy(
                src_ref=out_ref.at[my_rows], dst_ref=out_ref.at[my_rows], send_sem=send2.at[j - 1], recv_sem=recv2.at[j - 1],
                device_id=peer, device_id_type=MESH))
        for cp in first:
            cp.start()
        recv_buf[mine] = in_ref[my_rows]
        for cp in first:
            cp.wait()
        acc = recv_buf[0]
        for k in range(1, N_DEV):
            acc = acc + recv_buf[k]
        out_ref[my_rows] = acc
        for cp in second:
            cp.start()
        for cp in second:
            cp.wait()
        if xf is not None:
            _carry_finish(xf, x_refs[:len(xf.srcs)], land_refs, sem_refs)

    vmem = pl.BlockSpec(memory_space=pltpu.VMEM)
    return pl.pallas_call(
        body, in_specs=[vmem] + x_in_specs, out_specs=[vmem] + x_out_specs,
        out_shape=[jax.ShapeDtypeStruct((r, 128), f32)] + x_out_shapes,
        scratch_shapes=[pltpu.VMEM((N_DEV, rc, 128), f32)] + [pltpu.SemaphoreType.DMA((N_DEV - 1,))] * 4 + x_sems,
        input_output_aliases=aliases, name=name, compiler_params=_cp(32))(packed, *x_ops)


def _adamw(w, g, m, v):
    m = ADAM_B1 * m + (1.0 - ADAM_B1) * g
    v = ADAM_B2 * v + (1.0 - ADAM_B2) * (g * g)
    m_hat = m / (1.0 - ADAM_B1 ** ADAM_STEP)
    v_hat = v / (1.0 - ADAM_B2 ** ADAM_STEP)
    delta = -ADAM_LR * (m_hat / (jnp.sqrt(v_hat) + ADAM_EPS) + ADAM_WD * w)
    return delta, m, v


def adam_slots(slots, w, m, v, rows, name):
    _, depth, r, c = slots.shape

    def body(s_ref, w_ref, m_ref, v_ref, g_ref, d_ref, m2_ref, v2_ref):
        g = s_ref[0, 0].astype(f32)
        for k in range(1, N_DEV):
            g = g + s_ref[k, 0].astype(f32)
        d, m2, v2 = _adamw(w_ref[0], g, m_ref[0], v_ref[0])
        g_ref[0] = g
        d_ref[0] = d
        m2_ref[0] = m2
        v2_ref[0] = v2

    blk = pl.BlockSpec((1, rows, c), lambda l, i: (l, i, 0))
    sh = jax.ShapeDtypeStruct((depth, r, c), f32)
    return pl.pallas_call(
        body, grid=(depth, r // rows),
        in_specs=[pl.BlockSpec((N_DEV, 1, rows, c), lambda l, i: (0, l, i, 0)), blk, blk, blk],
        out_specs=[blk] * 4, out_shape=[sh] * 4, name=name, compiler_params=_cp(40))(slots, w, m, v)


def adam_flat(g, w, m, v, name):
    def body(g_ref, w_ref, m_ref, v_ref, d_ref, m2_ref, v2_ref):
        d, m2, v2 = _adamw(w_ref[...], g_ref[...], m_ref[...], v_ref[...])
        d_ref[...] = d
        m2_ref[...] = m2
        v2_ref[...] = v2

    sh = jax.ShapeDtypeStruct(g.shape, f32)
    return pl.pallas_call(body, out_shape=[sh] * 3, name=name)(g, w, m, v)


def _rows_of(shape):
    n = 1
    for dim in shape:
        n *= dim
    return n, -(-n // (8 * 128)) * 8


def _pack(arrs):
    parts = []
    for a in arrs:
        n, rows = _rows_of(a.shape)
        parts.append(jnp.pad(a.reshape(-1).astype(f32), (0, rows * 128 - n)).reshape(rows, 128))
    return jnp.concatenate(parts, axis=0)


def _unpack(packed, shapes):
    out, row = [], 0
    for sh in shapes:
        n, rows = _rows_of(sh)
        out.append(packed[row:row + rows].reshape(-1)[:n].reshape(sh))
        row += rows
    return out


def _block_diag(gw):
    eye = jnp.eye(8, dtype=gw.dtype)
    return (gw[:, :, None, :] * eye[:, None, :, None]).reshape(SEG, SEG)


def _diag_blocks(dense):
    eye = jnp.eye(8, dtype=dense.dtype)
    return (dense.reshape(8, 64, 8, 64) * eye[:, None, :, None]).sum(axis=2)


def _lane_row(vals, first_lane):
    return jnp.zeros((1, HD), f32).at[0, first_lane:first_lane + NH].set(vals)


def kernel(x, norm_w, w_in, rg_conv_w, rg_conv_b, rg_gate_w, rg_gate_b, rg_lambda, ml_gate_b, ml_norm_w, gd_conv_w, gd_a_log, gd_dt_bias, gd_norm_w, w_out, final_norm_w, loss_target, m_norm_w, m_w_in, m_rg_conv_w, m_rg_conv_b, m_rg_gate_w, m_rg_gate_b, m_rg_lambda, m_ml_gate_b, m_ml_norm_w, m_gd_conv_w, m_gd_a_log, m_gd_dt_bias, m_gd_norm_w, m_w_out, m_final_norm_w, v_norm_w, v_w_in, v_rg_conv_w, v_rg_conv_b, v_rg_gate_w, v_rg_gate_b, v_rg_lambda, v_ml_gate_b, v_ml_norm_w, v_gd_conv_w, v_gd_a_log, v_gd_dt_bias, v_gd_norm_w, v_w_out, v_final_norm_w):
    s = x.shape[1]
    xs = x.reshape(s, D)
    tgt = loss_target.reshape(s, D)
    me = 4 * lax.axis_index("x") + 2 * lax.axis_index("y") + lax.axis_index("c")

    comm = MeshComm(w_in, w_out, [rg_conv_w, rg_gate_b, gd_conv_w])
    rg_conv_full, rg_gb_full, gd_conv_full = comm.small_weights
    loss_part, dx, d_fw, g_small = local_step(
        xs, tgt, comm, rg_conv_full, rg_gb_full, gd_conv_full, norm_w, rg_conv_b, rg_gate_w, rg_lambda,
        ml_gate_b, ml_norm_w, gd_a_log, gd_dt_bias, gd_norm_w, final_norm_w)
    given_w = dict(norm_w=norm_w, rg_conv_w=rg_conv_w, rg_conv_b=rg_conv_b, rg_gate_w=rg_gate_w, rg_gate_b=rg_gate_b,
                   rg_lambda=rg_lambda, ml_gate_b=ml_gate_b, ml_norm_w=ml_norm_w, gd_conv_w=gd_conv_w, gd_a_log=gd_a_log,
                   gd_dt_bias=gd_dt_bias, gd_norm_w=gd_norm_w, final_norm_w=final_norm_w, w_in=w_in, w_out=w_out)
    given_m = dict(norm_w=m_norm_w, rg_conv_w=m_rg_conv_w, rg_conv_b=m_rg_conv_b, rg_gate_w=m_rg_gate_w, rg_gate_b=m_rg_gate_b,
                   rg_lambda=m_rg_lambda, ml_gate_b=m_ml_gate_b, ml_norm_w=m_ml_norm_w, gd_conv_w=m_gd_conv_w,
                   gd_a_log=m_gd_a_log, gd_dt_bias=m_gd_dt_bias, gd_norm_w=m_gd_norm_w, final_norm_w=m_final_norm_w,
                   w_in=m_w_in, w_out=m_w_out)
    given_v = dict(norm_w=v_norm_w, rg_conv_w=v_rg_conv_w, rg_conv_b=v_rg_conv_b, rg_gate_w=v_rg_gate_w, rg_gate_b=v_rg_gate_b,
                   rg_lambda=v_rg_lambda, ml_gate_b=v_ml_gate_b, ml_norm_w=v_ml_norm_w, gd_conv_w=v_gd_conv_w,
                   gd_a_log=v_gd_a_log, gd_dt_bias=v_gd_dt_bias, gd_norm_w=v_gd_norm_w, final_norm_w=v_final_norm_w,
                   w_in=v_w_in, w_out=v_w_out)
    return finish_step(loss_part, dx, d_fw, g_small, comm, s, me, given_w, given_m, given_v)


def _gathered_pieces():
    per = D_IN // N_DEV
    pieces = []
    for lo, hi in ((0, 3584), (3592, 5640), (3584, 3592), (5640, 5648)):
        col = lo
        while col < hi:
            k = col // per
            end = min(hi, (k + 1) * per)
            pieces.append((k, col - k * per, end - k * per))
            col = end
    return pieces


class MeshComm:
    GWI_SPLIT = 448
    WI_SPLIT = 384

    def __init__(self, w_in, w_out, small_shards):
        per = D_IN // N_DEV
        self.wi_sh = [w_in[l].astype(bf16) for l in range(DEPTH)]
        self.wo_sh = [w_out[l].astype(bf16) for l in range(DEPTH)]
        self.wi_land = jax.ShapeDtypeStruct((N_DEV, D, per), bf16)
        self.wo_land = jax.ShapeDtypeStruct((N_DEV, 3 * SEG // N_DEV, D), bf16)
        packed = _pack(small_shards)
        first = Gather2([self.wi_sh[0], self.wo_sh[0], packed],
                        [self.wi_land, self.wo_land, jax.ShapeDtypeStruct((N_DEV,) + packed.shape, f32)],
                        [(0, D), (0, 3 * SEG // N_DEV), (0, packed.shape[0])])
        wi_g, wo_g, sm_g = exchange(first, "gather_first")
        self.wi_g, self.wo_g = {0: wi_g}, {0: wo_g}
        shapes = [a.shape for a in small_shards]
        parts = [_unpack(sm_g[k], shapes) for k in range(N_DEV)]
        self.small_weights = [jnp.concatenate([p[j] for p in parts], axis=-1) for j in range(len(small_shards))]
        self.gwi_land = lax.empty((N_DEV, DEPTH, D, per), bf16)
        self.gwo_land = lax.empty((N_DEV, DEPTH, 3 * SEG // N_DEV, D), bf16)
        self.gwi_slots = {}
        self.gwo_slots = {}

    def weights(self, l):
        cols = [self.wi_g[l][k, :, a:b] for k, a, b in _gathered_pieces()] + [jnp.zeros((D, DP - D_IN), bf16)]
        return jnp.concatenate(cols, axis=1), self.wo_g[l].reshape(3 * SEG, D)

    def fwd_carry(self, l, host):
        if l + 1 >= DEPTH:
            return None
        if host == "mlstm":
            return Gather2([self.wo_sh[l + 1]], [self.wo_land], [(0, 3 * SEG // N_DEV)])
        if host == "inproj":
            return Gather2([self.wi_sh[l + 1]], [self.wi_land], [(0, self.WI_SPLIT)])
        return Gather2([self.wi_sh[l + 1]], [self.wi_g[l + 1]], [(self.WI_SPLIT, D)])

    def fwd_landed(self, l, host, landed):
        (self.wo_g if host == "mlstm" else self.wi_g)[l + 1] = landed[0]

    def own_w_out_grad(self, l, g_wo):
        self.gwo_slots[l] = g_wo.reshape(N_DEV, 3 * SEG // N_DEV, D).astype(bf16)

    def bwd_carry(self, l, host):
        rows_o = 3 * SEG // N_DEV
        srcs, land_of, layer, span = [], [], [], []
        if l + 1 < DEPTH:
            if host == "rglru":
                srcs, land_of, layer, span = [self.gwo_slots[l + 1]], [1], [l + 1], [(0, rows_o)]
            elif host == "mlstm":
                srcs, land_of, layer, span = [self.gwi_slots[l + 1]], [0], [l + 1], [(0, self.GWI_SPLIT)]
            else:
                srcs, land_of, layer, span = [self.gwi_slots[l + 1]], [0], [l + 1], [(self.GWI_SPLIT, D)]
        if l == 0 and host == "mlstm":
            srcs, land_of, layer, span = srcs + [self.gwo_slots[0]], land_of + [1], layer + [0], span + [(0, rows_o)]
        if not srcs:
            return None
        return Scatter(srcs, [self.gwi_land, self.gwo_land], land_of, layer, span)

    def bwd_landed(self, landed):
        self.gwi_land, self.gwo_land = landed

    def grads_ready(self, l, pieces):
        per = D_IN // N_DEV
        g_wi = jnp.concatenate(pieces, axis=1)
        self.gwi_slots[l] = jnp.stack([g_wi[:, k * per:(k + 1) * per] for k in range(N_DEV)]).astype(bf16)

    def last_carry(self):
        return Scatter([self.gwi_slots[0]], [self.gwi_land, self.gwo_land], [0], [0], [(0, D)])


def local_step(xs, tgt, comm, rg_conv_full, rg_gb_full, gd_conv_full, norm_w, rg_conv_b, rg_gate_w,
               rg_lambda, ml_gate_b, ml_norm_w, gd_a_log, gd_dt_bias, gd_norm_w, final_norm_w):
    acts = []
    for l in range(DEPTH):
        nw = norm_w[l].reshape(1, D)
        w_in_l, w_out_l = comm.weights(l)
        xf = comm.fwd_carry(l, "inproj")
        proj, hn_t, *landed = inproj(xs, nw, w_in_l, xf=xf)
        if xf is not None:
            comm.fwd_landed(l, "inproj", landed)
        rg_p = (rg_conv_full[l], rg_conv_b[l].reshape(1, SEG), _block_diag(rg_gate_w[l, 0]), _block_diag(rg_gate_w[l, 1]),
                rg_gb_full[l], rg_lambda[l].reshape(1, SEG))
        y_rg, hs = rglru_fwd(proj, *rg_p)
        ml_p = (jnp.zeros((1, HD), f32).at[0, 0:2 * NH].set(ml_gate_b[l].reshape(-1)), ml_norm_w[l].reshape(1, SEG))
        xf = comm.fwd_carry(l, "mlstm")
        y_ml, cs, ns, ms, *landed = mlstm_fwd(proj, *ml_p, xf=xf)
        if xf is not None:
            comm.fwd_landed(l, "mlstm", landed)
        gd_p = (gd_conv_full[l], _lane_row(gd_a_log[l], 2 * NH), _lane_row(gd_dt_bias[l], 2 * NH), gd_norm_w[l].reshape(1, HD))
        xf = comm.fwd_carry(l, "gdn")
        y_gd, ss, t_invs, *landed = gdn_fwd(proj, *gd_p, xf=xf)
        if xf is not None:
            comm.fwd_landed(l, "gdn", landed)
        acts.append((xs, nw, proj, hn_t, w_in_l, w_out_l, rg_p, y_rg, hs, ml_p, y_ml, cs, ns, ms, gd_p, y_gd, ss, t_invs))
        xs = outproj(xs, y_rg, y_ml, y_gd, w_out_l)

    dx, loss_part, d_fw = head(xs, final_norm_w.reshape(1, D), tgt)

    g_small = {k: [None] * DEPTH for k in ("norm_w", "rg_conv_w", "rg_conv_b", "rg_gate_w", "rg_gate_b", "rg_lambda",
                                           "ml_gate_b", "ml_norm_w", "gd_conv_w", "gd_a_log", "gd_dt_bias", "gd_norm_w")}
    for l in reversed(range(DEPTH)):
        x_l, nw, proj, hn_t, w_in_l, w_out_l, rg_p, y_rg, hs, ml_p, y_ml, cs, ns, ms, gd_p, y_gd, ss, t_invs = acts[l]
        dy_rg, dy_ml, dy_gd, g_wo = outproj_bwd(dx, y_rg, y_ml, y_gd, w_out_l)
        comm.own_w_out_grad(l, g_wo)
        xf = comm.bwd_carry(l, "rglru")
        d_rg, d_cw, d_cb, d_gr, d_gi, d_gb, d_lam, *landed = rglru_bwd(proj, hs, dy_rg, *rg_p, xf=xf)
        if xf is not None:
            comm.bwd_landed(landed)
        xf = comm.bwd_carry(l, "mlstm")
        d_ml, d_sm_ml, d_bias, d_mnw, *landed = mlstm_bwd(proj, dy_ml, cs, ns, ms, *ml_p, xf=xf)
        if xf is not None:
            comm.bwd_landed(landed)
        xf = comm.bwd_carry(l, "gdn")
        d_gd, d_sm_gd, d_gcw, d_al, d_dt, d_gnw, *landed = gdn_bwd(proj, dy_gd, ss, t_invs, *gd_p, xf=xf)
        if xf is not None:
            comm.bwd_landed(landed)
        dx, d_nw = inproj_bwd_x(x_l, nw, w_in_l, dx, d_rg, d_ml, d_gd, d_sm_ml, d_sm_gd)
        gw_rg = wgrad(hn_t, [d_rg], "wgrad_rg")
        gw_ml = wgrad(hn_t, [d_ml], "wgrad_ml")
        gw_gd = wgrad(hn_t, [d_gd], "wgrad_gd")
        gw_sm = wgrad(hn_t, [d_sm_ml, d_sm_gd], "wgrad_small")
        comm.grads_ready(l, [gw_rg, gw_ml, gw_sm[:, 0:2 * NH], gw_gd, gw_sm[:, 2 * NH:4 * NH]])
        g_small["norm_w"][l] = d_nw[0]
        g_small["rg_conv_w"][l] = d_cw
        g_small["rg_conv_b"][l] = d_cb[0]
        g_small["rg_gate_w"][l] = jnp.stack([_diag_blocks(d_gr), _diag_blocks(d_gi)])
        g_small["rg_gate_b"][l] = d_gb
        g_small["rg_lambda"][l] = d_lam[0]
        g_small["ml_gate_b"][l] = d_bias[0, 0:2 * NH].reshape(2, NH)
        g_small["ml_norm_w"][l] = d_mnw[0]
        g_small["gd_conv_w"][l] = d_gcw
        g_small["gd_a_log"][l] = d_al[0, 2 * NH:3 * NH]
        g_small["gd_dt_bias"][l] = d_dt[0, 2 * NH:3 * NH]
        g_small["gd_norm_w"][l] = d_gnw[0]
    return loss_part, dx, d_fw, g_small


def finish_step(loss_part, dx, d_fw, g_small, comm, s, me, given_w, given_m, given_v):
    small_names = ["norm_w", "rg_conv_w", "rg_conv_b", "rg_gate_w", "rg_gate_b", "rg_lambda", "ml_gate_b", "ml_norm_w",
                   "gd_conv_w", "gd_a_log", "gd_dt_bias", "gd_norm_w"]
    small_list = [loss_part[0, 0:1], d_fw[0]] + [jnp.stack(g_small[k]) for k in small_names]
    small_shapes = [a.shape for a in small_list]
    packed = _pack(small_list)
    packed = jnp.pad(packed, ((0, -packed.shape[0] % (8 * N_DEV)), (0, 0)))
    summed, gwi_r, gwo_r = small_allreduce(packed, "last_exchange", xf=comm.last_carry())
    g_all = _unpack(summed, small_shapes)

    g_wi, d_wi, m_wi, v_wi = adam_slots(gwi_r, given_w["w_in"], given_m["w_in"], given_v["w_in"], 256, "adam_w_in")
    g_wo, d_wo, m_wo, v_wo = adam_slots(gwo_r, given_w["w_out"], given_m["w_out"], given_v["w_out"], 192, "adam_w_out")
    loss = g_all[0][0]
    grads = {"final_norm_w": g_all[1]}
    for k, g in zip(small_names, g_all[2:]):
        grads[k] = g
    for k, width in (("rg_conv_w", 64), ("rg_gate_b", 64), ("gd_conv_w", 192)):
        grads[k] = lax.dynamic_slice_in_dim(grads[k], me * width, width, axis=2)
    names = small_names + ["final_norm_w"]
    shapes = [given_w[k].shape for k in names]
    d_p, m_p, v_p = adam_flat(_pack([grads[k] for k in names]), _pack([given_w[k] for k in names]),
                              _pack([given_m[k] for k in names]), _pack([given_v[k] for k in names]), "adam_small")
    deltas = dict(zip(names, _unpack(d_p, shapes)))
    new_m = dict(zip(names, _unpack(m_p, shapes)))
    new_v = dict(zip(names, _unpack(v_p, shapes)))
    grads["w_in"], deltas["w_in"], new_m["w_in"], new_v["w_in"] = g_wi, d_wi, m_wi, v_wi
    grads["w_out"], deltas["w_out"], new_m["w_out"], new_v["w_out"] = g_wo, d_wo, m_wo, v_wo

    order = ["norm_w", "w_in", "rg_conv_w", "rg_conv_b", "rg_gate_w", "rg_gate_b", "rg_lambda", "ml_gate_b", "ml_norm_w",
             "gd_conv_w", "gd_a_log", "gd_dt_bias", "gd_norm_w", "w_out", "final_norm_w"]
    return (loss, dx.reshape(1, s, D), *[grads[k] for k in order], *[deltas[k] for k in order],
            *[new_m[k] for k in order], *[new_v[k] for k in order])
```

```python
import functools
from typing import NamedTuple

import jax
import jax.numpy as jnp
from jax import lax
from jax.experimental import pallas as pl
from jax.experimental.pallas import tpu as pltpu

f32 = jnp.float32
bf16 = jnp.bfloat16
MESH = pl.DeviceIdType.MESH

N_DEV = 8
D = 1024
DEPTH = 4
EPS = 1e-6
SEG = 512
HD = 128
NH = 4
LC = 64
RUN = 4
LR = RUN * LC
D_IN = 5648
DP = 5760
SMALL0 = 5632
RG_TILE = 256
RG_C = 8.0

ADAM_LR = 0.001
ADAM_B1 = 0.9
ADAM_B2 = 0.999
ADAM_EPS = 1e-08
ADAM_WD = 0.01
ADAM_STEP = 10


def _cp(vmem_mb):
    return pltpu.CompilerParams(vmem_limit_bytes=vmem_mb * 2 ** 20)


def _dot(a, b, ca, cb):
    return lax.dot_general(a.astype(bf16), b.astype(bf16), (((ca,), (cb,)), ((), ())), preferred_element_type=f32)


@jax.custom_vjp
def mm_nn(a, b):
    return _dot(a, b, 1, 0)


@jax.custom_vjp
def mm_nt(a, b):
    return _dot(a, b, 1, 1)


@jax.custom_vjp
def mm_tn(a, b):
    return _dot(a, b, 0, 0)


mm_nn.defvjp(lambda a, b: (mm_nn(a, b), (a, b)), lambda r, g: (mm_nt(g, r[1]), mm_tn(r[0], g)))
mm_nt.defvjp(lambda a, b: (mm_nt(a, b), (a, b)), lambda r, g: (mm_nn(g, r[1]), mm_tn(g, r[0])))
mm_tn.defvjp(lambda a, b: (mm_tn(a, b), (a, b)), lambda r, g: (mm_nt(r[1], g), mm_nn(r[0], g)))


def _split(x):
    hi = x.astype(bf16)
    return hi, (x - hi.astype(f32)).astype(bf16)


def dot3(a, b):
    ah, al = _split(a)
    bh, bl = _split(b)
    d = functools.partial(jnp.dot, preferred_element_type=f32)
    return d(ah, bh) + (d(al, bh) + d(ah, bl))


def _tri_sum(x, reverse):
    n = x.shape[0]
    r = lax.broadcasted_iota(jnp.int32, (n, 3 * n), 0)
    c = lax.broadcasted_iota(jnp.int32, (n, 3 * n), 1) & (n - 1)
    ones = ((c >= r) if reverse else (c <= r)).astype(bf16)
    hi = x.astype(bf16)
    rest = x - hi.astype(f32)
    mid = rest.astype(bf16)
    lo = (rest - mid.astype(f32)).astype(bf16)
    return jnp.dot(ones, jnp.concatenate([hi, mid, lo], axis=0), preferred_element_type=f32)


@jax.custom_vjp
def cumsum_rows(x):
    return _tri_sum(x, False)


@jax.custom_vjp
def rev_cumsum_rows(x):
    return _tri_sum(x, True)


cumsum_rows.defvjp(lambda x: (cumsum_rows(x), None), lambda _, g: (rev_cumsum_rows(g),))
rev_cumsum_rows.defvjp(lambda x: (rev_cumsum_rows(x), None), lambda _, g: (cumsum_rows(g),))


def _tri(n, strict=False):
    r = lax.broadcasted_iota(jnp.int32, (n, n), 0)
    c = lax.broadcasted_iota(jnp.int32, (n, n), 1)
    return (r > c) if strict else (r >= c)


def _lane_col(v, j):
    lane = lax.broadcasted_iota(jnp.int32, v.shape, 1)
    return jnp.sum(jnp.where(lane == j, v, 0.0), axis=1, keepdims=True)


def _rows_from(pieces, rows, width):
    ridx = lax.broadcasted_iota(jnp.int32, (rows, width), 0)
    out = jnp.zeros((rows, width), f32)
    for h, p in enumerate(pieces):
        out = out + jnp.where(ridx == h, jnp.broadcast_to(p, (rows, width)), 0.0)
    return out


def causal_conv(halo8, x, w4):
    t = x.shape[0]
    xe = jnp.concatenate([halo8, x], axis=0)
    y = xe[5:5 + t] * w4[0:1]
    for k in range(1, 4):
        y = y + xe[5 + k:5 + k + t] * w4[k:k + 1]
    return y


def ml_chunk(q, k, v, o_pre, z, small, bias_row, norm_w, C, n, m):
    n_ch = q.shape[0] // LC
    lane = lax.broadcasted_iota(jnp.int32, small.shape, 1)
    pre = small + bias_row
    lg = jnp.where(lane < NH, pre, jnp.where(lane < 2 * NH, jax.nn.log_sigmoid(pre), 0.0))
    rows = [slice(c * LC, (c + 1) * LC) for c in range(n_ch)]
    lgs = [lg[r] for r in rows]
    bcs = [cumsum_rows(x) for x in lgs]
    lg_ts = [x.T for x in lgs]
    bc_ts = [x.T for x in bcs]
    causal = _tri(LC)
    pairs = [(c, h) for c in range(n_ch) for h in range(NH)]
    idx = range(len(pairs))
    sls = [slice(h * HD, (h + 1) * HD) for h in range(NH)]
    qs = [q[rows[c], sls[h]] * (HD ** -0.5) for c, h in pairs]
    ks = [k[rows[c], sls[h]] for c, h in pairs]
    vs = [v[rows[c], sls[h]] for c, h in pairs]
    li_cols = [_lane_col(lgs[c], h) for c, h in pairs]
    b_cols = [_lane_col(bcs[c], NH + h) for c, h in pairs]
    dms = [jnp.where(causal, b_cols[i] - bc_ts[c][NH + h:NH + h + 1, :] + lg_ts[c][h:h + 1, :], -jnp.inf)
           for i, (c, h) in enumerate(pairs)]
    dm_maxs = [jnp.max(dms[i], axis=-1, keepdims=True) for i in idx]
    gs = [b_cols[i][LC - 1:LC, :] for i in idx]
    wss = [gs[i] - b_cols[i] + li_cols[i] for i in idx]
    ws_maxs = [jnp.max(wss[i], axis=0, keepdims=True) for i in idx]
    qks = [mm_nt(qs[i], ks[i]) for i in idx]
    m_in = [None] * len(pairs)
    m_out = [None] * len(pairs)
    for h in range(NH):
        cur = m[h:h + 1, 0:1]
        for c in range(n_ch):
            i = c * NH + h
            m_in[i] = cur
            cur = jnp.maximum(gs[i] + cur, ws_maxs[i])
            m_out[i] = cur
    m_inters = [b_cols[i] + m_in[i] for i in idx]
    m_ts = [jnp.maximum(m_inters[i], dm_maxs[i]) for i in idx]
    ss = [qks[i] * jnp.exp(dms[i] - m_ts[i]) for i in idx]
    scs = [jnp.exp(m_inters[i] - m_ts[i]) for i in idx]
    decs = [jnp.exp(gs[i] + m_in[i] - m_out[i]) for i in idx]
    kws = [jnp.exp(wss[i] - m_out[i]) * ks[i] for i in idx]
    c_adds = [mm_tn(kws[i], vs[i]) for i in idx]
    n_adds = [jnp.sum(kws[i], axis=0, keepdims=True) for i in idx]
    svs = [mm_nn(ss[i], vs[i]) for i in idx]
    s_sums = [jnp.sum(ss[i], axis=-1, keepdims=True) for i in idx]
    c_hs = [C[sl, :] for sl in sls]
    n_hs = [n[h:h + 1, :] for h in range(NH)]
    hhs = [None] * len(pairs)
    for c in range(n_ch):
        for h in range(NH):
            i = c * NH + h
            num = svs[i] + scs[i] * mm_nn(qs[i], c_hs[h])
            den = s_sums[i] + scs[i] * jnp.sum(qs[i] * n_hs[h], axis=-1, keepdims=True)
            hhs[i] = num / jnp.maximum(jnp.abs(den), jnp.exp(-m_ts[i]))
        c_hs = [decs[c * NH + h] * c_hs[h] + c_adds[c * NH + h] for h in range(NH)]
        n_hs = [decs[c * NH + h] * n_hs[h] + n_adds[c * NH + h] for h in range(NH)]
    ys = [hhs[i] * lax.rsqrt(jnp.mean(hhs[i] * hhs[i], axis=-1, keepdims=True) + EPS) * norm_w[:, sls[h]]
          * jax.nn.sigmoid(o_pre[rows[c], sls[h]]) * jax.nn.silu(z[rows[c], sls[h]]) for i, (c, h) in enumerate(pairs)]
    y = jnp.concatenate([jnp.concatenate(ys[c * NH:(c + 1) * NH], axis=1) for c in range(n_ch)], axis=0)
    last = (n_ch - 1) * NH
    m_rows = [jnp.broadcast_to(m_out[last + h], (1, HD)) for h in range(NH)]
    return y, jnp.concatenate(c_hs, axis=0), _rows_from(n_hs, 8, HD), _rows_from(m_rows, 8, HD)


@jax.custom_vjp
def _unit_lower_inverses(mats):
    eye = (lax.broadcasted_iota(jnp.int32, (LC, LC), 0) == lax.broadcasted_iota(jnp.int32, (LC, LC), 1)).astype(f32)
    ps = [-m for m in mats]
    ts = [eye + p for p in ps]
    for _ in range(5):
        ps = [dot3(p, p) for p in ps]
        ts = [t + dot3(t, p) for t, p in zip(ts, ps)]
    return tuple(ts)


def _unit_lower_inverses_fwd(mats):
    ts = _unit_lower_inverses(mats)
    return ts, ts


def _unit_lower_inverses_bwd(ts, gs):
    tts = [t.T for t in ts]
    mid = [dot3(tt, g) for tt, g in zip(tts, gs)]
    return (tuple(-dot3(m, tt) for m, tt in zip(mid, tts)),)


_unit_lower_inverses.defvjp(_unit_lower_inverses_fwd, _unit_lower_inverses_bwd)


@jax.custom_vjp
def _known_inverses(mats, ts):
    return tuple(ts)


_known_inverses.defvjp(lambda mats, ts: (tuple(ts), tuple(ts)),
                       lambda ts, gs: (_unit_lower_inverses_bwd(ts, gs)[0], tuple(jnp.zeros_like(t) for t in ts)))


def gd_chunk(qh8, q, kh8, k, vh8, v, z, small, conv_w, alog_row, dt_row, norm_w, st, known_t=None):
    n_ch = q.shape[0] // LC
    lane = lax.broadcasted_iota(jnp.int32, small.shape, 1)
    is_a = (lane >= 2 * NH) & (lane < 3 * NH)
    g_all = jnp.where(is_a, -jnp.exp(alog_row) * jax.nn.softplus(small + dt_row), 0.0)
    beta_all = jax.nn.sigmoid(small)
    qc = jax.nn.silu(causal_conv(qh8, q, conv_w[:, 0:SEG]))
    kc = jax.nn.silu(causal_conv(kh8, k, conv_w[:, SEG:2 * SEG]))
    vc = jax.nn.silu(causal_conv(vh8, v, conv_w[:, 2 * SEG:3 * SEG]))
    incl = _tri(LC)
    strict = _tri(LC, strict=True)
    rows = [slice(c * LC, (c + 1) * LC) for c in range(n_ch)]
    gc_alls = [cumsum_rows(g_all[r]) for r in rows]
    gc_ts = [g.T for g in gc_alls]
    pairs = [(c, h) for c in range(n_ch) for h in range(NH)]
    idx = range(len(pairs))
    sls = [slice(h * HD, (h + 1) * HD) for h in range(NH)]
    qs = [qc[rows[c], sls[h]] for c, h in pairs]
    ks = [kc[rows[c], sls[h]] for c, h in pairs]
    vs = [vc[rows[c], sls[h]] for c, h in pairs]
    qs = [x * lax.rsqrt(jnp.sum(x * x, axis=-1, keepdims=True) + EPS) * (HD ** -0.5) for x in qs]
    ks = [x * lax.rsqrt(jnp.sum(x * x, axis=-1, keepdims=True) + EPS) for x in ks]
    betas = [_lane_col(beta_all[rows[c]], 3 * NH + h) for c, h in pairs]
    gcs = [_lane_col(gc_alls[c], 2 * NH + h) for c, h in pairs]
    gams = [jnp.exp(jnp.where(incl, gcs[i] - gc_ts[c][2 * NH + h:2 * NH + h + 1, :], -jnp.inf))
            for i, (c, h) in enumerate(pairs)]
    kbs = [ks[i] * betas[i] for i in idx]
    mats = tuple(jnp.where(strict, mm_nt(kbs[i], ks[i]) * gams[i], 0.0) for i in idx)
    aqks = [mm_nt(qs[i], ks[i]) * gams[i] for i in idx]
    if known_t is None:
        t_invs = _unit_lower_inverses(mats)
    else:
        t_invs = _known_inverses(mats, tuple(known_t[i] for i in idx))
    us = [mm_nn(t_invs[i], vs[i] * betas[i]) for i in idx]
    ws = [mm_nn(t_invs[i], kbs[i] * jnp.exp(gcs[i])) for i in idx]
    g_lasts = [gcs[i][LC - 1:LC, :] for i in idx]
    q_decs = [qs[i] * jnp.exp(gcs[i]) for i in idx]
    k_decs = [ks[i] * jnp.exp(g_lasts[i] - gcs[i]) for i in idx]
    e_lasts = [jnp.exp(g_lasts[i]) for i in idx]
    s_hs = [st[sl, :] for sl in sls]
    os_ = [None] * len(pairs)
    for c in range(n_ch):
        ids = [c * NH + h for h in range(NH)]
        v_news = [us[i] - mm_nn(ws[i], s_hs[h]) for h, i in enumerate(ids)]
        for h, i in enumerate(ids):
            os_[i] = mm_nn(q_decs[i], s_hs[h]) + mm_nn(aqks[i], v_news[h])
        s_hs = [s_hs[h] * e_lasts[i] + mm_tn(k_decs[i], v_news[h]) for h, i in enumerate(ids)]
    ys = [os_[i] * lax.rsqrt(jnp.mean(os_[i] * os_[i], axis=-1, keepdims=True) + EPS) * norm_w
          * jax.nn.silu(z[rows[c], sls[h]]) for i, (c, h) in enumerate(pairs)]
    y = jnp.concatenate([jnp.concatenate(ys[c * NH:(c + 1) * NH], axis=1) for c in range(n_ch)], axis=0)
    if known_t is None:
        return y, jnp.concatenate(s_hs, axis=0), jnp.stack(t_invs)
    return y, jnp.concatenate(s_hs, axis=0)


def rg_pre(xh8, x, conv_w, conv_b, gw_r, gw_i, gate_b, lam):
    xc = causal_conv(xh8, x, conv_w) + conv_b
    r = jax.nn.sigmoid(mm_nn(xc, gw_r) + gate_b[0:1])
    i = jax.nn.sigmoid(mm_nn(xc, gw_i) + gate_b[1:2])
    log_a = -RG_C * r * jax.nn.softplus(-lam)
    a = jnp.exp(log_a)
    th = jnp.tanh(log_a)
    one_minus_a2 = -2.0 * th / (1.0 - th)
    b = jnp.sqrt(one_minus_a2) * (i * xc)
    return a, b


def _scan_rows(a_ref, b_ref, out_ref, h0, n_rows, reverse):
    n_groups = n_rows // 8
    width = a_ref.shape[1]
    row = lax.broadcasted_iota(jnp.int32, (8, width), 0)

    def body(j, h):
        g = (n_groups - 1 - j) if reverse else j
        r0 = pl.multiple_of(g * 8, 8)
        av = a_ref[pl.ds(r0, 8), :]
        bv = b_ref[pl.ds(r0, 8), :]
        for d in (1, 2, 4):
            sh = (8 - d) if reverse else d
            a_s = pltpu.roll(av, sh, 0)
            b_s = pltpu.roll(bv, sh, 0)
            valid = (row < 8 - d) if reverse else (row >= d)
            bv = jnp.where(valid, av * b_s + bv, bv)
            av = jnp.where(valid, av * a_s, av)
        hv = av * h + bv
        out_ref[pl.ds(r0, 8), :] = hv
        return hv[0:1, :] if reverse else hv[7:8, :]

    return lax.fori_loop(0, n_groups, body, h0)


def _norm_rows(xv):
    r = lax.rsqrt(jnp.mean(xv * xv, axis=-1, keepdims=True) + EPS)
    return xv * r, r


def inproj(x, nw, w, xf=None):
    s, ts = x.shape[0], 256
    nt = s // ts
    n_in, n_out = 3, 2
    x_ops, x_in_specs, x_out_shapes, x_out_specs, aliases, x_sems, n_xin, n_land = _carried(xf, n_in, n_out)

    def body(*refs):
        (x_ref, nw_ref, w_ref), x_refs, (o_ref, ht_ref), land_refs, _, sem_refs = _cut_refs(refs, n_in, n_xin, n_out, n_land, 0)
        i = pl.program_id(0)
        _carry_open(xf, i, nt, x_refs, land_refs, sem_refs)
        xn, _ = _norm_rows(x_ref[...])
        hn = xn * nw_ref[...]
        o_ref[...] = jnp.dot(hn.astype(bf16), w_ref[...], preferred_element_type=f32)
        ht_ref[...] = hn.T.astype(bf16)
        _carry_close(xf, i, nt, x_refs, land_refs, sem_refs)

    return pl.pallas_call(
        body, grid=(nt,),
        in_specs=[pl.BlockSpec((ts, D), lambda i: (i, 0)), pl.BlockSpec((1, D), lambda i: (0, 0)),
                  pl.BlockSpec((D, DP), lambda i: (0, 0))] + x_in_specs,
        out_specs=[pl.BlockSpec((ts, DP), lambda i: (i, 0)), pl.BlockSpec((D, ts), lambda i: (0, i))] + x_out_specs,
        out_shape=[jax.ShapeDtypeStruct((s, DP), f32), jax.ShapeDtypeStruct((D, s), bf16)] + x_out_shapes,
        scratch_shapes=x_sems, input_output_aliases=aliases,
        name="inproj" if xf is None else "inproj_carrying", compiler_params=_cp(56))(x, nw, w, *x_ops)


def outproj(x, yr, ym, yg, wo):
    s, ts = x.shape[0], 256

    def body(x_ref, yr_ref, ym_ref, yg_ref, wo_ref, o_ref):
        acc = x_ref[...]
        for j, y_ref in enumerate((yr_ref, ym_ref, yg_ref)):
            acc = acc + jnp.dot(y_ref[...].astype(bf16), wo_ref[j * SEG:(j + 1) * SEG, :], preferred_element_type=f32)
        o_ref[...] = acc

    yspec = pl.BlockSpec((ts, SEG), lambda i: (i, 0))
    return pl.pallas_call(
        body, grid=(s // ts,),
        in_specs=[pl.BlockSpec((ts, D), lambda i: (i, 0)), yspec, yspec, yspec,
                  pl.BlockSpec((3 * SEG, D), lambda i: (0, 0))],
        out_specs=pl.BlockSpec((ts, D), lambda i: (i, 0)),
        out_shape=jax.ShapeDtypeStruct((s, D), f32), name="outproj", compiler_params=_cp(40))(x, yr, ym, yg, wo)


def head(x, fw, tgt):
    s, ts = x.shape[0], 256

    def body(x_ref, fw_ref, t_ref, dx_ref, loss_ref, dfw_ref):
        @pl.when(pl.program_id(0) == 0)
        def _():
            loss_ref[...] = jnp.zeros_like(loss_ref)
            dfw_ref[...] = jnp.zeros_like(dfw_ref)

        xn, r = _norm_rows(x_ref[...])
        fw_v = fw_ref[...]
        err = xn * fw_v - t_ref[...]
        loss_ref[...] += 0.5 * jnp.sum(jnp.mean(err * err, axis=-1, keepdims=True))
        dy = err * (1.0 / D)
        dfw_ref[0:1, :] += jnp.sum(dy * xn, axis=0, keepdims=True)
        dxn = dy * fw_v
        dx_ref[...] = r * (dxn - xn * jnp.mean(dxn * xn, axis=-1, keepdims=True))

    tile = pl.BlockSpec((ts, D), lambda i: (i, 0))
    return pl.pallas_call(
        body, grid=(s // ts,),
        in_specs=[tile, pl.BlockSpec((1, D), lambda i: (0, 0)), tile],
        out_specs=[tile, pl.BlockSpec((8, 128), lambda i: (0, 0)), pl.BlockSpec((8, D), lambda i: (0, 0))],
        out_shape=[jax.ShapeDtypeStruct((s, D), f32), jax.ShapeDtypeStruct((8, 128), f32),
                   jax.ShapeDtypeStruct((8, D), f32)], name="head")(x, fw, tgt)


def outproj_bwd(dx, yr, ym, yg, wo):
    s, ts = dx.shape[0], 256

    def body(dx_ref, yr_ref, ym_ref, yg_ref, wo_ref, dyr_ref, dym_ref, dyg_ref, gwo_ref):
        @pl.when(pl.program_id(0) == 0)
        def _():
            gwo_ref[...] = jnp.zeros_like(gwo_ref)

        dxb = dx_ref[...].astype(bf16)
        for j, (y_ref, dy_ref) in enumerate(((yr_ref, dyr_ref), (ym_ref, dym_ref), (yg_ref, dyg_ref))):
            rows = slice(j * SEG, (j + 1) * SEG)
            dy_ref[...] = lax.dot_general(dxb, wo_ref[rows, :], (((1,), (1,)), ((), ())), preferred_element_type=f32)
            gwo_ref[rows, :] += lax.dot_general(y_ref[...].astype(bf16), dxb, (((0,), (0,)), ((), ())),
                                                preferred_element_type=f32)

    yspec = pl.BlockSpec((ts, SEG), lambda i: (i, 0))
    wspec = pl.BlockSpec((3 * SEG, D), lambda i: (0, 0))
    ysh = jax.ShapeDtypeStruct((s, SEG), f32)
    return pl.pallas_call(
        body, grid=(s // ts,),
        in_specs=[pl.BlockSpec((ts, D), lambda i: (i, 0)), yspec, yspec, yspec, wspec],
        out_specs=[yspec, yspec, yspec, wspec],
        out_shape=[ysh, ysh, ysh, jax.ShapeDtypeStruct((3 * SEG, D), f32)],
        name="outproj_bwd", compiler_params=_cp(48))(dx, yr, ym, yg, wo)


def inproj_bwd_x(x, nw, w, dxo, d_rg, d_ml, d_gd, d_sa, d_sb):
    s, ts = x.shape[0], 256
    widths = (d_rg.shape[1], d_ml.shape[1], d_gd.shape[1], HD)

    def body(x_ref, nw_ref, w_ref, dxo_ref, rg_ref, ml_ref, gd_ref, sa_ref, sb_ref, dx_ref, dnw_ref):
        @pl.when(pl.program_id(0) == 0)
        def _():
            dnw_ref[...] = jnp.zeros_like(dnw_ref)

        xn, r = _norm_rows(x_ref[...])
        pieces = (rg_ref[...], ml_ref[...], gd_ref[...], sa_ref[...] + sb_ref[...])
        dhn = jnp.zeros((ts, D), f32)
        c0 = 0
        for piece, wd in zip(pieces, widths):
            dhn = dhn + lax.dot_general(piece.astype(bf16), w_ref[:, c0:c0 + wd], (((1,), (1,)), ((), ())),
                                        preferred_element_type=f32)
            c0 += wd
        dnw_ref[0:1, :] += jnp.sum(dhn * xn, axis=0, keepdims=True)
        dxn = dhn * nw_ref[...]
        dx_ref[...] = dxo_ref[...] + r * (dxn - xn * jnp.mean(dxn * xn, axis=-1, keepdims=True))

    tile = pl.BlockSpec((ts, D), lambda i: (i, 0))
    return pl.pallas_call(
        body, grid=(s // ts,),
        in_specs=[tile, pl.BlockSpec((1, D), lambda i: (0, 0)), pl.BlockSpec((D, DP), lambda i: (0, 0)), tile]
        + [pl.BlockSpec((ts, wd), lambda i: (i, 0)) for wd in widths] + [pl.BlockSpec((ts, HD), lambda i: (i, 0))],
        out_specs=[tile, pl.BlockSpec((8, D), lambda i: (0, 0))],
        out_shape=[jax.ShapeDtypeStruct((s, D), f32), jax.ShapeDtypeStruct((8, D), f32)],
        name="inproj_bwd_x", compiler_params=_cp(56))(x, nw, w, dxo, d_rg, d_ml, d_gd, d_sa, d_sb)


def wgrad(hn_t, dps, name):
    s = hn_t.shape[1]
    c = dps[0].shape[1]
    ct = min(c, SEG)
    n_dp = len(dps)

    def body(*refs):
        ht_ref = refs[0]
        dp_refs = refs[1:1 + n_dp]
        o_ref = refs[1 + n_dp]
        dp = dp_refs[0][...]
        for extra in dp_refs[1:]:
            dp = dp + extra[...]
        o_ref[...] = jnp.dot(ht_ref[...], dp.astype(bf16), preferred_element_type=f32).astype(bf16)

    return pl.pallas_call(
        body, grid=(c // ct,),
        in_specs=[pl.BlockSpec((D, s), lambda j: (0, 0))] + [pl.BlockSpec((s, ct), lambda j: (0, j)) for _ in dps],
        out_specs=pl.BlockSpec((D, ct), lambda j: (0, j)),
        out_shape=jax.ShapeDtypeStruct((D, c), bf16), name=name, compiler_params=_cp(40))(hn_t, *dps)


def _seg_spec(rows, seg, n_tiles=None):
    if n_tiles is None:
        return pl.BlockSpec((rows, SEG), lambda i: (i, seg))
    return pl.BlockSpec((rows, SEG), lambda i: (n_tiles - 1 - i, seg))


def _halo_spec(rows, seg, n_tiles=None):
    per = rows // 8
    if n_tiles is None:
        return pl.BlockSpec((8, SEG), lambda i: (jnp.maximum(i * per - 1, 0), seg))
    return pl.BlockSpec((8, SEG), lambda i: (jnp.maximum((n_tiles - 1 - i) * per - 1, 0), seg))


def _const_spec(shape):
    return pl.BlockSpec(shape, lambda i: tuple(0 for _ in shape))


def rglru_fwd(proj, conv_w, conv_b, gw_r, gw_i, gate_b, lam):
    s = proj.shape[0]
    tr = RG_TILE

    def body(xh_ref, x_ref, z_ref, cw_ref, cb_ref, gr_ref, gi_ref, gb_ref, lam_ref, y_ref, h_ref, a_s, b_s, hc):
        first = pl.program_id(0) == 0

        @pl.when(first)
        def _():
            hc[...] = jnp.zeros_like(hc)

        xh = jnp.where(first, 0.0, xh_ref[...])
        a, b = rg_pre(xh, x_ref[...], cw_ref[...], cb_ref[...], gr_ref[...], gi_ref[...], gb_ref[...], lam_ref[...])
        a_s[...] = a
        b_s[...] = b
        hc[0:1, :] = _scan_rows(a_s, b_s, h_ref, hc[0:1, :], tr, False)
        y_ref[...] = (h_ref[...] * jax.nn.silu(z_ref[...])).astype(bf16)

    out = pl.BlockSpec((tr, SEG), lambda i: (i, 0))
    return pl.pallas_call(
        body, grid=(s // tr,),
        in_specs=[_halo_spec(tr, 0), _seg_spec(tr, 0), _seg_spec(tr, 1), _const_spec((4, SEG)), _const_spec((1, SEG)),
                  _const_spec((SEG, SEG)), _const_spec((SEG, SEG)), _const_spec((2, SEG)), _const_spec((1, SEG))],
        out_specs=[out, out],
        out_shape=[jax.ShapeDtypeStruct((s, SEG), bf16), jax.ShapeDtypeStruct((s, SEG), f32)],
        scratch_shapes=[pltpu.VMEM((tr, SEG), f32), pltpu.VMEM((tr, SEG), f32), pltpu.VMEM((8, SEG), f32)],
        name="rglru_fwd", compiler_params=_cp(40))(proj, proj, proj, conv_w, conv_b, gw_r, gw_i, gate_b, lam)


def rglru_bwd(proj, hs, dy, conv_w, conv_b, gw_r, gw_i, gate_b, lam, xf=None):
    s = proj.shape[0]
    tr = RG_TILE
    nt = s // tr
    n_in, n_out = 12, 7
    x_ops, x_in_specs, x_out_shapes, x_out_specs, aliases, x_sems, n_xin, n_land = _carried(xf, n_in, n_out)

    def body(*refs):
        ins, x_refs, outs, land_refs, scratch, sem_refs = _cut_refs(refs, n_in, n_xin, n_out, n_land, 6)
        xh_ref, x_ref, z_ref, hh_ref, h_ref, dy_ref, cw_ref, cb_ref, gr_ref, gi_ref, gb_ref, lam_ref = ins
        dp_ref, dcw_ref, dcb_ref, dgr_ref, dgi_ref, dgb_ref, dlam_ref = outs
        an_s, g_s, dh_s, a_first, dh_first, dhalo = scratch
        i = pl.program_id(0)
        first_tile = i == nt - 1
        _carry_open(xf, i, nt, x_refs, land_refs, sem_refs)

        @pl.when(i == 0)
        def _():
            for ref in (dcw_ref, dcb_ref, dgr_ref, dgi_ref, dgb_ref, dlam_ref, a_first, dh_first, dhalo):
                ref[...] = jnp.zeros_like(ref)

        xh = jnp.where(first_tile, 0.0, xh_ref[...])
        params = (cw_ref[...], cb_ref[...], gr_ref[...], gi_ref[...], gb_ref[...], lam_ref[...])
        (a, _), vjp = jax.vjp(rg_pre, xh, x_ref[...], *params)
        zv = z_ref[...]
        hv = h_ref[...]
        dyv = dy_ref[...]
        sig = jax.nn.sigmoid(zv)
        g_s[...] = dyv * (zv * sig)
        dp_ref[:, SEG:2 * SEG] = (dyv * hv * (sig * (1.0 + zv * (1.0 - sig)))).astype(bf16)
        ridx = lax.broadcasted_iota(jnp.int32, (tr, SEG), 0)
        an_s[...] = jnp.where(ridx == tr - 1, jnp.broadcast_to(a_first[0:1, :], (tr, SEG)), pltpu.roll(a, tr - 1, 0))
        _scan_rows(an_s, g_s, dh_s, dh_first[0:1, :], tr, True)
        dh = dh_s[...]
        h_prev_last = jnp.where(first_tile, 0.0, hh_ref[...])[7:8, :]
        h_prev = pltpu.roll(hv, 1, 0)
        h_prev = jnp.where(ridx == 0, jnp.broadcast_to(h_prev_last, (tr, SEG)), h_prev)
        dxh, dx, dcw, dcb, dgr, dgi, dgb, dlam = vjp((dh * h_prev, dh))
        dx = dx + jnp.concatenate([jnp.zeros((tr - 8, SEG), f32), dhalo[...]], axis=0)
        dp_ref[:, 0:SEG] = dx.astype(bf16)
        dhalo[...] = dxh
        a_first[0:1, :] = a[0:1, :]
        dh_first[0:1, :] = dh[0:1, :]
        dcw_ref[...] += dcw
        dcb_ref[...] += dcb
        dgr_ref[...] += dgr
        dgi_ref[...] += dgi
        dgb_ref[...] += dgb
        dlam_ref[...] += dlam
        _carry_close(xf, i, nt, x_refs, land_refs, sem_refs)

    pspecs = [_const_spec((4, SEG)), _const_spec((1, SEG)), _const_spec((SEG, SEG)), _const_spec((SEG, SEG)),
              _const_spec((2, SEG)), _const_spec((1, SEG))]
    pshapes = [jax.ShapeDtypeStruct(sh, f32) for sh in ((4, SEG), (1, SEG), (SEG, SEG), (SEG, SEG), (2, SEG), (1, SEG))]
    tile = pl.BlockSpec((tr, SEG), lambda i: (nt - 1 - i, 0))
    return pl.pallas_call(
        body, grid=(nt,),
        in_specs=[_halo_spec(tr, 0, nt), _seg_spec(tr, 0, nt), _seg_spec(tr, 1, nt),
                  pl.BlockSpec((8, SEG), lambda i: (jnp.maximum((nt - 1 - i) * (tr // 8) - 1, 0), 0)), tile, tile] + pspecs
        + x_in_specs,
        out_specs=[pl.BlockSpec((tr, 2 * SEG), lambda i: (nt - 1 - i, 0))] + pspecs + x_out_specs,
        out_shape=[jax.ShapeDtypeStruct((s, 2 * SEG), bf16)] + pshapes + x_out_shapes,
        scratch_shapes=[pltpu.VMEM((tr, SEG), f32), pltpu.VMEM((tr, SEG), f32), pltpu.VMEM((tr, SEG), f32),
                        pltpu.VMEM((8, SEG), f32), pltpu.VMEM((8, SEG), f32), pltpu.VMEM((8, SEG), f32)] + x_sems,
        input_output_aliases=aliases, name="rglru_bwd" if xf is None else "rglru_bwd_carrying", compiler_params=_cp(48))(
            proj, proj, proj, hs, hs, dy, conv_w, conv_b, gw_r, gw_i, gate_b, lam, *x_ops)


ML_SEGS = (2, 3, 4, 5, 6)
SMALL_BLK = SMALL0 // HD


def _cut_refs(refs, n_in, n_xin, n_out, n_land, n_scratch):
    bounds = [0, n_in, n_in + n_xin, n_in + n_xin + n_out, n_in + n_xin + n_out + n_land,
              n_in + n_xin + n_out + n_land + n_scratch, len(refs)]
    return [refs[a:b] for a, b in zip(bounds[:-1], bounds[1:])]


def mlstm_fwd(proj, bias_row, norm_w, xf=None):
    s = proj.shape[0]
    nc = s // LR
    n_in, n_out = 8, 4
    x_ops, x_in_specs, x_out_shapes, x_out_specs, aliases, x_sems, n_xin, n_land = _carried(xf, n_in, n_out)

    def body(*refs):
        ins, x_refs, outs, land_refs, scratch, sem_refs = _cut_refs(refs, n_in, n_xin, n_out, n_land, 3)
        q_ref, k_ref, v_ref, o_ref, z_ref, sm_ref, b_ref, nw_ref = ins
        y_ref, cs_ref, ns_ref, ms_ref = outs
        c_s, n_s, m_s = scratch
        i = pl.program_id(0)
        _carry_open(xf, i, nc, x_refs, land_refs, sem_refs)

        @pl.when(i == 0)
        def _():
            c_s[...] = jnp.zeros_like(c_s)
            n_s[...] = jnp.zeros_like(n_s)
            m_s[...] = jnp.zeros_like(m_s)

        cs_ref[0] = c_s[...]
        ns_ref[0] = n_s[...]
        ms_ref[0] = m_s[...]
        y, c2, n2, m2 = ml_chunk(q_ref[...], k_ref[...], v_ref[...], o_ref[...], z_ref[...], sm_ref[...],
                                 b_ref[...], nw_ref[...], c_s[...], n_s[...], m_s[...])
        y_ref[...] = y.astype(bf16)
        c_s[...] = c2
        n_s[...] = n2
        m_s[...] = m2
        _carry_close(xf, i, nc, x_refs, land_refs, sem_refs)

    return pl.pallas_call(
        body, grid=(nc,),
        in_specs=[_seg_spec(LR, sg) for sg in ML_SEGS]
        + [pl.BlockSpec((LR, HD), lambda i: (i, SMALL_BLK)), _const_spec((1, HD)), _const_spec((1, SEG))] + x_in_specs,
        out_specs=[pl.BlockSpec((LR, SEG), lambda i: (i, 0)), pl.BlockSpec((1, SEG, HD), lambda i: (i, 0, 0)),
                   pl.BlockSpec((1, 8, HD), lambda i: (i, 0, 0)), pl.BlockSpec((1, 8, HD), lambda i: (i, 0, 0))] + x_out_specs,
        out_shape=[jax.ShapeDtypeStruct((s, SEG), bf16), jax.ShapeDtypeStruct((nc, SEG, HD), f32),
                   jax.ShapeDtypeStruct((nc, 8, HD), f32), jax.ShapeDtypeStruct((nc, 8, HD), f32)] + x_out_shapes,
        scratch_shapes=[pltpu.VMEM((SEG, HD), f32), pltpu.VMEM((8, HD), f32), pltpu.VMEM((8, HD), f32)] + x_sems,
        input_output_aliases=aliases, name="mlstm_fwd" if xf is None else "mlstm_fwd_carrying")(
            proj, proj, proj, proj, proj, proj, bias_row, norm_w, *x_ops)


def mlstm_bwd(proj, dy, cs, ns, ms, bias_row, norm_w, xf=None):
    s = proj.shape[0]
    nc = s // LR
    n_in, n_out = 12, 4
    x_ops, x_in_specs, x_out_shapes, x_out_specs, aliases, x_sems, n_xin, n_land = _carried(xf, n_in, n_out)

    def body(*refs):
        ins, x_refs, outs, land_refs, scratch, sem_refs = _cut_refs(refs, n_in, n_xin, n_out, n_land, 3)
        q_ref, k_ref, v_ref, o_ref, z_ref, sm_ref, dy_ref, cs_ref, ns_ref, ms_ref, b_ref, nw_ref = ins
        dp_ref, dsm_ref, db_ref, dnw_ref = outs
        dc_s, dn_s, dm_s = scratch
        i = pl.program_id(0)
        _carry_open(xf, i, nc, x_refs, land_refs, sem_refs)

        @pl.when(i == 0)
        def _():
            for ref in (db_ref, dnw_ref, dc_s, dn_s, dm_s):
                ref[...] = jnp.zeros_like(ref)

        _, vjp = jax.vjp(ml_chunk, q_ref[...], k_ref[...], v_ref[...], o_ref[...], z_ref[...], sm_ref[...],
                         b_ref[...], nw_ref[...], cs_ref[0], ns_ref[0], ms_ref[0])
        dq, dk, dv, do, dz, dsm, db, dnw, dc, dn, dm = vjp((dy_ref[...], dc_s[...], dn_s[...], dm_s[...]))
        for j, val in enumerate((dq, dk, dv, do, dz)):
            dp_ref[:, j * SEG:(j + 1) * SEG] = val.astype(bf16)
        dsm_ref[...] = dsm
        db_ref[0:1, :] += db
        dnw_ref[0:1, :] += dnw
        dc_s[...] = dc
        dn_s[...] = dn
        dm_s[...] = dm
        _carry_close(xf, i, nc, x_refs, land_refs, sem_refs)

    rev3 = lambda i: (nc - 1 - i, 0, 0)
    return pl.pallas_call(
        body, grid=(nc,),
        in_specs=[_seg_spec(LR, sg, nc) for sg in ML_SEGS]
        + [pl.BlockSpec((LR, HD), lambda i: (nc - 1 - i, SMALL_BLK)), pl.BlockSpec((LR, SEG), lambda i: (nc - 1 - i, 0)),
           pl.BlockSpec((1, SEG, HD), rev3), pl.BlockSpec((1, 8, HD), rev3), pl.BlockSpec((1, 8, HD), rev3),
           _const_spec((1, HD)), _const_spec((1, SEG))] + x_in_specs,
        out_specs=[pl.BlockSpec((LR, 5 * SEG), lambda i: (nc - 1 - i, 0)), pl.BlockSpec((LR, HD), lambda i: (nc - 1 - i, 0)),
                   _const_spec((8, HD)), _const_spec((8, SEG))] + x_out_specs,
        out_shape=[jax.ShapeDtypeStruct((s, 5 * SEG), bf16), jax.ShapeDtypeStruct((s, HD), f32),
                   jax.ShapeDtypeStruct((8, HD), f32), jax.ShapeDtypeStruct((8, SEG), f32)] + x_out_shapes,
        scratch_shapes=[pltpu.VMEM((SEG, HD), f32), pltpu.VMEM((8, HD), f32), pltpu.VMEM((8, HD), f32)] + x_sems,
        input_output_aliases=aliases, name="mlstm_bwd" if xf is None else "mlstm_bwd_carrying", compiler_params=_cp(48))(
            proj, proj, proj, proj, proj, proj, dy, cs, ns, ms, bias_row, norm_w, *x_ops)


GD_SEGS = (7, 8, 9)


def _carried(xf, n_in, n_out):
    operands, out_shapes, aliases, sems, _, n_xin, n_land = _carry_plumb(xf, n_in, n_out)
    return operands, [_ANY] * n_xin, out_shapes, [_ANY] * n_land, aliases, sems, n_xin, n_land


def _carry_open(xf, i, n_steps, x_refs, land_refs, sem_refs):
    if xf is None:
        return
    srcs = x_refs[:len(xf.srcs)]

    @pl.when(i == 0)
    def _():
        _carry_start(xf, srcs, land_refs, sem_refs)

    @pl.when(i == max(n_steps - 2, 0))
    def _():
        _carry_middle(xf, srcs, land_refs, sem_refs)


def _carry_close(xf, i, n_steps, x_refs, land_refs, sem_refs):
    if xf is None:
        return

    @pl.when(i == n_steps - 1)
    def _():
        _carry_finish(xf, x_refs[:len(xf.srcs)], land_refs, sem_refs)


def gdn_fwd(proj, conv_w, alog_row, dt_row, norm_w, xf=None):
    s = proj.shape[0]
    nc = s // LR
    n_in, n_out = 12, 3
    x_ops, x_in_specs, x_out_shapes, x_out_specs, aliases, x_sems, n_xin, n_land = _carried(xf, n_in, n_out)

    def body(*refs):
        qh_ref, q_ref, kh_ref, k_ref, vh_ref, v_ref, z_ref, sm_ref, cw_ref, al_ref, dt_ref, nw_ref = refs[:n_in]
        x_refs = refs[n_in:n_in + n_xin]
        y_ref, ss_ref, ti_ref = refs[n_in + n_xin:n_in + n_xin + n_out]
        land_refs = refs[n_in + n_xin + n_out:n_in + n_xin + n_out + n_land]
        st_s = refs[n_in + n_xin + n_out + n_land]
        sem_refs = refs[n_in + n_xin + n_out + n_land + 1:]
        i = pl.program_id(0)
        first = i == 0
        _carry_open(xf, i, nc, x_refs, land_refs, sem_refs)

        @pl.when(first)
        def _():
            st_s[...] = jnp.zeros_like(st_s)

        ss_ref[0] = st_s[...]
        halo = [jnp.where(first, 0.0, r[...]) for r in (qh_ref, kh_ref, vh_ref)]
        y, st2, t_invs = gd_chunk(halo[0], q_ref[...], halo[1], k_ref[...], halo[2], v_ref[...], z_ref[...], sm_ref[...],
                                  cw_ref[...], al_ref[...], dt_ref[...], nw_ref[...], st_s[...])
        y_ref[...] = y.astype(bf16)
        ti_ref[...] = t_invs
        st_s[...] = st2
        _carry_close(xf, i, nc, x_refs, land_refs, sem_refs)

    qkv_specs = []
    for sg in GD_SEGS:
        qkv_specs += [_halo_spec(LR, sg), _seg_spec(LR, sg)]
    return pl.pallas_call(
        body, grid=(nc,),
        in_specs=qkv_specs + [_seg_spec(LR, 10), pl.BlockSpec((LR, HD), lambda i: (i, SMALL_BLK)),
                              _const_spec((4, 3 * SEG)), _const_spec((1, HD)), _const_spec((1, HD)), _const_spec((1, HD))]
        + x_in_specs,
        out_specs=[pl.BlockSpec((LR, SEG), lambda i: (i, 0)), pl.BlockSpec((1, SEG, HD), lambda i: (i, 0, 0)),
                   pl.BlockSpec((RUN * NH, LC, LC), lambda i: (i, 0, 0))] + x_out_specs,
        out_shape=[jax.ShapeDtypeStruct((s, SEG), bf16), jax.ShapeDtypeStruct((nc, SEG, HD), f32),
                   jax.ShapeDtypeStruct((s // LC * NH, LC, LC), f32)] + x_out_shapes,
        scratch_shapes=[pltpu.VMEM((SEG, HD), f32)] + x_sems, input_output_aliases=aliases,
        name="gdn_fwd" if xf is None else "gdn_fwd_carrying")(
            proj, proj, proj, proj, proj, proj, proj, proj, conv_w, alog_row, dt_row, norm_w, *x_ops)


def gdn_bwd(proj, dy, ss, t_invs, conv_w, alog_row, dt_row, norm_w, xf=None):
    s = proj.shape[0]
    nc = s // LR
    n_in, n_out = 15, 6
    x_ops, x_in_specs, x_out_shapes, x_out_specs, aliases, x_sems, n_xin, n_land = _carried(xf, n_in, n_out)

    def body(*refs):
        (qh_ref, q_ref, kh_ref, k_ref, vh_ref, v_ref, z_ref, sm_ref, dy_ref, ss_ref, ti_ref,
         cw_ref, al_ref, dt_ref, nw_ref) = refs[:n_in]
        x_refs = refs[n_in:n_in + n_xin]
        dp_ref, dsm_ref, dcw_ref, dal_ref, ddt_ref, dnw_ref = refs[n_in + n_xin:n_in + n_xin + n_out]
        land_refs = refs[n_in + n_xin + n_out:n_in + n_xin + n_out + n_land]
        dst_s, dhalo = refs[n_in + n_xin + n_out + n_land:n_in + n_xin + n_out + n_land + 2]
        sem_refs = refs[n_in + n_xin + n_out + n_land + 2:]
        i = pl.program_id(0)
        first_chunk = i == nc - 1
        _carry_open(xf, i, nc, x_refs, land_refs, sem_refs)

        @pl.when(i == 0)
        def _():
            for ref in (dcw_ref, dal_ref, ddt_ref, dnw_ref, dst_s, dhalo):
                ref[...] = jnp.zeros_like(ref)

        halo = [jnp.where(first_chunk, 0.0, r[...]) for r in (qh_ref, kh_ref, vh_ref)]
        with_known = functools.partial(gd_chunk, known_t=ti_ref[...])
        _, vjp = jax.vjp(with_known, halo[0], q_ref[...], halo[1], k_ref[...], halo[2], v_ref[...], z_ref[...], sm_ref[...],
                         cw_ref[...], al_ref[...], dt_ref[...], nw_ref[...], ss_ref[0])
        dqh, dq, dkh, dk, dvh, dv, dz, dsm, dcw, dal, ddt, dnw, dst = vjp((dy_ref[...], dst_s[...]))
        for j, val in enumerate((dq, dk, dv)):
            val = val + jnp.concatenate([jnp.zeros((LR - 8, SEG), f32), dhalo[:, j * SEG:(j + 1) * SEG]], axis=0)
            dp_ref[:, j * SEG:(j + 1) * SEG] = val.astype(bf16)
        dp_ref[:, 3 * SEG:4 * SEG] = dz.astype(bf16)
        for j, val in enumerate((dqh, dkh, dvh)):
            dhalo[:, j * SEG:(j + 1) * SEG] = val
        dsm_ref[...] = dsm
        dcw_ref[...] += dcw
        dal_ref[0:1, :] += dal
        ddt_ref[0:1, :] += ddt
        dnw_ref[0:1, :] += dnw
        dst_s[...] = dst
        _carry_close(xf, i, nc, x_refs, land_refs, sem_refs)

    qkv_specs = []
    for sg in GD_SEGS:
        qkv_specs += [_halo_spec(LR, sg, nc), _seg_spec(LR, sg, nc)]
    return pl.pallas_call(
        body, grid=(nc,),
        in_specs=qkv_specs + [_seg_spec(LR, 10, nc), pl.BlockSpec((LR, HD), lambda i: (nc - 1 - i, SMALL_BLK)),
                              pl.BlockSpec((LR, SEG), lambda i: (nc - 1 - i, 0)),
                              pl.BlockSpec((1, SEG, HD), lambda i: (nc - 1 - i, 0, 0)),
                              pl.BlockSpec((RUN * NH, LC, LC), lambda i: (nc - 1 - i, 0, 0)),
                              _const_spec((4, 3 * SEG)), _const_spec((1, HD)), _const_spec((1, HD)), _const_spec((1, HD))]
        + x_in_specs,
        out_specs=[pl.BlockSpec((LR, 4 * SEG), lambda i: (nc - 1 - i, 0)), pl.BlockSpec((LR, HD), lambda i: (nc - 1 - i, 0)),
                   _const_spec((4, 3 * SEG)), _const_spec((8, HD)), _const_spec((8, HD)), _const_spec((8, HD))] + x_out_specs,
        out_shape=[jax.ShapeDtypeStruct((s, 4 * SEG), bf16), jax.ShapeDtypeStruct((s, HD), f32),
                   jax.ShapeDtypeStruct((4, 3 * SEG), f32), jax.ShapeDtypeStruct((8, HD), f32),
                   jax.ShapeDtypeStruct((8, HD), f32), jax.ShapeDtypeStruct((8, HD), f32)] + x_out_shapes,
        scratch_shapes=[pltpu.VMEM((SEG, HD), f32), pltpu.VMEM((8, 3 * SEG), f32)] + x_sems, input_output_aliases=aliases,
        name="gdn_bwd" if xf is None else "gdn_bwd_carrying", compiler_params=_cp(48))(
            proj, proj, proj, proj, proj, proj, proj, proj, dy, ss, t_invs, conv_w, alog_row, dt_row, norm_w, *x_ops)


def _my_place():
    return lax.axis_index("x"), lax.axis_index("y"), lax.axis_index("c")


def _slot(p):
    return 4 * p[0] + 2 * p[1] + p[2]


def _peer(me, j):
    flips = ((j >> 2) & 1, (j >> 1) & 1, j & 1)
    return tuple((1 - v) if fl else v for v, fl in zip(me, flips))


_ANY = pl.BlockSpec(memory_space=pl.ANY)


class Scatter(NamedTuple):
    srcs: list
    lands: list
    land_of: list
    layer: list
    span: list


class Gather2(NamedTuple):
    srcs: list
    lands: list
    span: list


def _carry_plumb(cx, n_in, n_out):
    if cx is None:
        return [], [], {}, [], 0, 0, 0
    n_src = len(cx.srcs)
    passed = [li for li, ld in enumerate(cx.lands) if not isinstance(ld, jax.ShapeDtypeStruct)]
    operands = list(cx.srcs) + [cx.lands[li] for li in passed]
    aliases = {n_in + n_src + k: n_out + li for k, li in enumerate(passed)}
    out_shapes = [jax.ShapeDtypeStruct(ld.shape, ld.dtype) for ld in cx.lands]
    sems = [pltpu.SemaphoreType.DMA((n_src, N_DEV - 1)), pltpu.SemaphoreType.DMA((n_src, N_DEV - 1)),
            pltpu.SemaphoreType.DMA((n_src,))]
    return operands, out_shapes, aliases, sems, n_src, len(operands), len(cx.lands)


def _scatter_copies(sc, src_refs, land_refs, send_sems, recv_sems, local_sems):
    me = _my_place()
    mine = _slot(me)
    local, remote = [], []
    for a, src_ref in enumerate(src_refs):
        lo, hi = sc.span[a]
        rows = pl.ds(lo, hi - lo)
        dst = land_refs[sc.land_of[a]].at[mine, sc.layer[a], rows]
        local.append(pltpu.make_async_copy(src_ref.at[mine, rows], dst, local_sems.at[a]))
        for j in range(1, N_DEV):
            peer = _peer(me, j)
            remote.append(pltpu.make_async_remote_copy(
                src_ref=src_ref.at[_slot(peer), rows], dst_ref=dst, send_sem=send_sems.at[a, j - 1],
                recv_sem=recv_sems.at[a, j - 1], device_id=peer, device_id_type=MESH))
    return local, remote


def _gather2_copy(land, sems, a, k, block_of, to, rows, src=None):
    dst = land.at[_slot(block_of), rows]
    return pltpu.make_async_remote_copy(src_ref=dst if src is None else src, dst_ref=dst, send_sem=sems[0].at[a, k],
                                        recv_sem=sems[1].at[a, k], device_id=to, device_id_type=MESH)


def _gather2_places():
    x, y, c = _my_place()
    return (x, y, c), (x, y, 1 - c), [(1 - x, y), (x, 1 - y), (1 - x, 1 - y)], c


def _carry_start(cx, src_refs, land_refs, sems):
    if isinstance(cx, Scatter):
        local, remote = _scatter_copies(cx, src_refs, land_refs, *sems)
        for cp in local + remote:
            cp.start()
        return
    me, sib, chips, c = _gather2_places()
    for a, (src_ref, land) in enumerate(zip(src_refs, land_refs)):
        rows = pl.ds(cx.span[a][0], cx.span[a][1] - cx.span[a][0])
        src = src_ref.at[rows]
        pltpu.make_async_copy(src, land.at[_slot(me), rows], sems[2].at[a]).start()
        _gather2_copy(land, sems, a, 0, me, sib, rows, src=src).start()
        for j, chip in enumerate(chips):
            _gather2_copy(land, sems, a, 1 + j, me, (*chip, c), rows, src=src).start()


def _carry_middle(cx, src_refs, land_refs, sems):
    if isinstance(cx, Scatter):
        return
    me, sib, chips, c = _gather2_places()
    for a, land in enumerate(land_refs):
        rows = pl.ds(cx.span[a][0], cx.span[a][1] - cx.span[a][0])
        for j, chip in enumerate(chips):
            _gather2_copy(land, sems, a, 1 + j, (*chip, c), me, rows).wait_recv()
            _gather2_copy(land, sems, a, 4 + j, (*chip, c), sib, rows).start()


def _carry_finish(cx, src_refs, land_refs, sems):
    if isinstance(cx, Scatter):
        local, remote = _scatter_copies(cx, src_refs, land_refs, *sems)
        for cp in remote:
            cp.wait()
        for cp in local:
            cp.wait()
        return
    me, sib, chips, c = _gather2_places()
    for a, (src_ref, land) in enumerate(zip(src_refs, land_refs)):
        rows = pl.ds(cx.span[a][0], cx.span[a][1] - cx.span[a][0])
        src = src_ref.at[rows]
        _gather2_copy(land, sems, a, 0, sib, me, rows).wait_recv()
        for j, chip in enumerate(chips):
            _gather2_copy(land, sems, a, 4 + j, (*chip, 1 - c), me, rows).wait_recv()
        _gather2_copy(land, sems, a, 0, me, sib, rows, src=src).wait_send()
        for j, chip in enumerate(chips):
            _gather2_copy(land, sems, a, 1 + j, me, (*chip, c), rows, src=src).wait_send()
            _gather2_copy(land, sems, a, 4 + j, (*chip, c), sib, rows).wait_send()
        pltpu.make_async_copy(src, land.at[_slot(me), rows], sems[2].at[a]).wait()


def exchange(cx, name):
    operands, out_shapes, aliases, sems, n_src, n_xin, n_land = _carry_plumb(cx, 0, 0)

    def body(*refs):
        src_refs, land_refs, sem_refs = refs[:n_src], refs[n_xin:n_xin + n_land], refs[n_xin + n_land:]
        _carry_start(cx, src_refs, land_refs, sem_refs)
        _carry_middle(cx, src_refs, land_refs, sem_refs)
        _carry_finish(cx, src_refs, land_refs, sem_refs)

    return pl.pallas_call(body, in_specs=[_ANY] * n_xin, out_specs=[_ANY] * n_land, out_shape=out_shapes,
                          scratch_shapes=sems, input_output_aliases=aliases, name=name)(*operands)


def small_allreduce(packed, name, xf=None):
    r = packed.shape[0]
    rc = r // N_DEV
    x_ops, x_in_specs, x_out_shapes, x_out_specs, aliases, x_sems, n_xin, n_land = _carried(xf, 1, 1)

    def body(*refs):
        (in_ref,), x_refs, (out_ref,), land_refs, scratch, sem_refs = _cut_refs(refs, 1, n_xin, 1, n_land, 5)
        recv_buf, send1, recv1, send2, recv2 = scratch
        if xf is not None:
            _carry_start(xf, x_refs[:len(xf.srcs)], land_refs, sem_refs)
        me = _my_place()
        mine = _slot(me)
        my_rows = pl.ds(pl.multiple_of(mine * rc, 8), rc)
        first, second = [], []
        for j in range(1, N_DEV):
            peer = _peer(me, j)
            peer_rows = pl.ds(pl.multiple_of(_slot(peer) * rc, 8), rc)
            first.append(pltpu.make_async_remote_copy(
                src_ref=in_ref.at[peer_rows], dst_ref=recv_buf.at[mine], send_sem=send1.at[j - 1], recv_sem=recv1.at[j - 1],
                device_id=peer, device_id_type=MESH))
            second.append(pltpu.make_async_remote_copy(
                src_ref=out_ref.at[my_rows], dst_ref=out_ref.at[my_rows], send_sem=send2.at[j - 1], recv_sem=recv2.at[j - 1],
                device_id=peer, device_id_type=MESH))
        for cp in first:
            cp.start()
        recv_buf[mine] = in_ref[my_rows]
        for cp in first:
            cp.wait()
        acc = recv_buf[0]
        for k in range(1, N_DEV):
            acc = acc + recv_buf[k]
        out_ref[my_rows] = acc
        for cp in second:
            cp.start()
        for cp in second:
            cp.wait()
        if xf is not None:
            _carry_finish(xf, x_refs[:len(xf.srcs)], land_refs, sem_refs)

    vmem = pl.BlockSpec(memory_space=pltpu.VMEM)
    return pl.pallas_call(
        body, in_specs=[vmem] + x_in_specs, out_specs=[vmem] + x_out_specs,
        out_shape=[jax.ShapeDtypeStruct((r, 128), f32)] + x_out_shapes,
        scratch_shapes=[pltpu.VMEM((N_DEV, rc, 128), f32)] + [pltpu.SemaphoreType.DMA((N_DEV - 1,))] * 4 + x_sems,
        input_output_aliases=aliases, name=name, compiler_params=_cp(32))(packed, *x_ops)


def _adamw(w, g, m, v):
    m = ADAM_B1 * m + (1.0 - ADAM_B1) * g
    v = ADAM_B2 * v + (1.0 - ADAM_B2) * (g * g)
    m_hat = m / (1.0 - ADAM_B1 ** ADAM_STEP)
    v_hat = v / (1.0 - ADAM_B2 ** ADAM_STEP)
    delta = -ADAM_LR * (m_hat / (jnp.sqrt(v_hat) + ADAM_EPS) + ADAM_WD * w)
    return delta, m, v


def adam_slots(slots, w, m, v, rows, name):
    _, depth, r, c = slots.shape

    def body(s_ref, w_ref, m_ref, v_ref, g_ref, d_ref, m2_ref, v2_ref):
        g = s_ref[0, 0].astype(f32)
        for k in range(1, N_DEV):
            g = g + s_ref[k, 0].astype(f32)
        d, m2, v2 = _adamw(w_ref[0], g, m_ref[0], v_ref[0])
        g_ref[0] = g
        d_ref[0] = d
        m2_ref[0] = m2
        v2_ref[0] = v2

    blk = pl.BlockSpec((1, rows, c), lambda l, i: (l, i, 0))
    sh = jax.ShapeDtypeStruct((depth, r, c), f32)
    return pl.pallas_call(
        body, grid=(depth, r // rows),
        in_specs=[pl.BlockSpec((N_DEV, 1, rows, c), lambda l, i: (0, l, i, 0)), blk, blk, blk],
        out_specs=[blk] * 4, out_shape=[sh] * 4, name=name, compiler_params=_cp(40))(slots, w, m, v)


def adam_flat(g, w, m, v, name):
    def body(g_ref, w_ref, m_ref, v_ref, d_ref, m2_ref, v2_ref):
        d, m2, v2 = _adamw(w_ref[...], g_ref[...], m_ref[...], v_ref[...])
        d_ref[...] = d
        m2_ref[...] = m2
        v2_ref[...] = v2

    sh = jax.ShapeDtypeStruct(g.shape, f32)
    return pl.pallas_call(body, out_shape=[sh] * 3, name=name)(g, w, m, v)


def _rows_of(shape):
    n = 1
    for dim in shape:
        n *= dim
    return n, -(-n // (8 * 128)) * 8


def _pack(arrs):
    parts = []
    for a in arrs:
        n, rows = _rows_of(a.shape)
        parts.append(jnp.pad(a.reshape(-1).astype(f32), (0, rows * 128 - n)).reshape(rows, 128))
    return jnp.concatenate(parts, axis=0)


def _unpack(packed, shapes):
    out, row = [], 0
    for sh in shapes:
        n, rows = _rows_of(sh)
        out.append(packed[row:row + rows].reshape(-1)[:n].reshape(sh))
        row += rows
    return out


def _block_diag(gw):
    eye = jnp.eye(8, dtype=gw.dtype)
    return (gw[:, :, None, :] * eye[:, None, :, None]).reshape(SEG, SEG)


def _diag_blocks(dense):
    eye = jnp.eye(8, dtype=dense.dtype)
    return (dense.reshape(8, 64, 8, 64) * eye[:, None, :, None]).sum(axis=2)


def _lane_row(vals, first_lane):
    return jnp.zeros((1, HD), f32).at[0, first_lane:first_lane + NH].set(vals)


def kernel(x, norm_w, w_in, rg_conv_w, rg_conv_b, rg_gate_w, rg_gate_b, rg_lambda, ml_gate_b, ml_norm_w, gd_conv_w, gd_a_log, gd_dt_bias, gd_norm_w, w_out, final_norm_w, loss_target, m_norm_w, m_w_in, m_rg_conv_w, m_rg_conv_b, m_rg_gate_w, m_rg_gate_b, m_rg_lambda, m_ml_gate_b, m_ml_norm_w, m_gd_conv_w, m_gd_a_log, m_gd_dt_bias, m_gd_norm_w, m_w_out, m_final_norm_w, v_norm_w, v_w_in, v_rg_conv_w, v_rg_conv_b, v_rg_gate_w, v_rg_gate_b, v_rg_lambda, v_ml_gate_b, v_ml_norm_w, v_gd_conv_w, v_gd_a_log, v_gd_dt_bias, v_gd_norm_w, v_w_out, v_final_norm_w):
    s = x.shape[1]
    xs = x.reshape(s, D)
    tgt = loss_target.reshape(s, D)
    me = 4 * lax.axis_index("x") + 2 * lax.axis_index("y") + lax.axis_index("c")

    comm = MeshComm(w_in, w_out, [rg_conv_w, rg_gate_b, gd_conv_w])
    rg_conv_full, rg_gb_full, gd_conv_full = comm.small_weights
    loss_part, dx, d_fw, g_small = local_step(
        xs, tgt, comm, rg_conv_full, rg_gb_full, gd_conv_full, norm_w, rg_conv_b, rg_gate_w, rg_lambda,
        ml_gate_b, ml_norm_w, gd_a_log, gd_dt_bias, gd_norm_w, final_norm_w)
    given_w = dict(norm_w=norm_w, rg_conv_w=rg_conv_w, rg_conv_b=rg_conv_b, rg_gate_w=rg_gate_w, rg_gate_b=rg_gate_b,
                   rg_lambda=rg_lambda, ml_gate_b=ml_gate_b, ml_norm_w=ml_norm_w, gd_conv_w=gd_conv_w, gd_a_log=gd_a_log,
                   gd_dt_bias=gd_dt_bias, gd_norm_w=gd_norm_w, final_norm_w=final_norm_w, w_in=w_in, w_out=w_out)
    given_m = dict(norm_w=m_norm_w, rg_conv_w=m_rg_conv_w, rg_conv_b=m_rg_conv_b, rg_gate_w=m_rg_gate_w, rg_gate_b=m_rg_gate_b,
                   rg_lambda=m_rg_lambda, ml_gate_b=m_ml_gate_b, ml_norm_w=m_ml_norm_w, gd_conv_w=m_gd_conv_w,
                   gd_a_log=m_gd_a_log, gd_dt_bias=m_gd_dt_bias, gd_norm_w=m_gd_norm_w, final_norm_w=m_final_norm_w,
                   w_in=m_w_in, w_out=m_w_out)
    given_v = dict(norm_w=v_norm_w, rg_conv_w=v_rg_conv_w, rg_conv_b=v_rg_conv_b, rg_gate_w=v_rg_gate_w, rg_gate_b=v_rg_gate_b,
                   rg_lambda=v_rg_lambda, ml_gate_b=v_ml_gate_b, ml_norm_w=v_ml_norm_w, gd_conv_w=v_gd_conv_w,
                   gd_a_log=v_gd_a_log, gd_dt_bias=v_gd_dt_bias, gd_norm_w=v_gd_norm_w, final_norm_w=v_final_norm_w,
                   w_in=v_w_in, w_out=v_w_out)
    return finish_step(loss_part, dx, d_fw, g_small, comm, s, me, given_w, given_m, given_v)


def _gathered_pieces():
    per = D_IN // N_DEV
    pieces = []
    for lo, hi in ((0, 3584), (3592, 5640), (3584, 3592), (5640, 5648)):
        col = lo
        while col < hi:
            k = col // per
            end = min(hi, (k + 1) * per)
            pieces.append((k, col - k * per, end - k * per))
            col = end
    return pieces


class MeshComm:
    GWI_SPLIT = 448
    WI_SPLIT = 384

    def __init__(self, w_in, w_out, small_shards):
        per = D_IN // N_DEV
        self.wi_sh = [w_in[l].astype(bf16) for l in range(DEPTH)]
        self.wo_sh = [w_out[l].astype(bf16) for l in range(DEPTH)]
        self.wi_land = jax.ShapeDtypeStruct((N_DEV, D, per), bf16)
        self.wo_land = jax.ShapeDtypeStruct((N_DEV, 3 * SEG // N_DEV, D), bf16)
        packed = _pack(small_shards)
        first = Gather2([self.wi_sh[0], self.wo_sh[0], packed],
                        [self.wi_land, self.wo_land, jax.ShapeDtypeStruct((N_DEV,) + packed.shape, f32)],
                        [(0, D), (0, 3 * SEG // N_DEV), (0, packed.shape[0])])
        wi_g, wo_g, sm_g = exchange(first, "gather_first")
        self.wi_g, self.wo_g = {0: wi_g}, {0: wo_g}
        shapes = [a.shape for a in small_shards]
        parts = [_unpack(sm_g[k], shapes) for k in range(N_DEV)]
        self.small_weights = [jnp.concatenate([p[j] for p in parts], axis=-1) for j in range(len(small_shards))]
        self.gwi_land = lax.empty((N_DEV, DEPTH, D, per), bf16)
        self.gwo_land = lax.empty((N_DEV, DEPTH, 3 * SEG // N_DEV, D), bf16)
        self.gwi_slots = {}
        self.gwo_slots = {}

    def weights(self, l):
        cols = [self.wi_g[l][k, :, a:b] for k, a, b in _gathered_pieces()] + [jnp.zeros((D, DP - D_IN), bf16)]
        return jnp.concatenate(cols, axis=1), self.wo_g[l].reshape(3 * SEG, D)

    def fwd_carry(self, l, host):
        if l + 1 >= DEPTH:
            return None
        if host == "mlstm":
            return Gather2([self.wo_sh[l + 1]], [self.wo_land], [(0, 3 * SEG // N_DEV)])
        if host == "inproj":
            return Gather2([self.wi_sh[l + 1]], [self.wi_land], [(0, self.WI_SPLIT)])
        return Gather2([self.wi_sh[l + 1]], [self.wi_g[l + 1]], [(self.WI_SPLIT, D)])

    def fwd_landed(self, l, host, landed):
        (self.wo_g if host == "mlstm" else self.wi_g)[l + 1] = landed[0]

    def own_w_out_grad(self, l, g_wo):
        self.gwo_slots[l] = g_wo.reshape(N_DEV, 3 * SEG // N_DEV, D).astype(bf16)

    def bwd_carry(self, l, host):
        rows_o = 3 * SEG // N_DEV
        srcs, land_of, layer, span = [], [], [], []
        if l + 1 < DEPTH:
            if host == "rglru":
                srcs, land_of, layer, span = [self.gwo_slots[l + 1]], [1], [l + 1], [(0, rows_o)]
            elif host == "mlstm":
                srcs, land_of, layer, span = [self.gwi_slots[l + 1]], [0], [l + 1], [(0, self.GWI_SPLIT)]
            else:
                srcs, land_of, layer, span = [self.gwi_slots[l + 1]], [0], [l + 1], [(self.GWI_SPLIT, D)]
        if l == 0 and host == "mlstm":
            srcs, land_of, layer, span = srcs + [self.gwo_slots[0]], land_of + [1], layer + [0], span + [(0, rows_o)]
        if not srcs:
            return None
        return Scatter(srcs, [self.gwi_land, self.gwo_land], land_of, layer, span)

    def bwd_landed(self, landed):
        self.gwi_land, self.gwo_land = landed

    def grads_ready(self, l, pieces):
        per = D_IN // N_DEV
        g_wi = jnp.concatenate(pieces, axis=1)
        self.gwi_slots[l] = jnp.stack([g_wi[:, k * per:(k + 1) * per] for k in range(N_DEV)]).astype(bf16)

    def last_carry(self):
        return Scatter([self.gwi_slots[0]], [self.gwi_land, self.gwo_land], [0], [0], [(0, D)])


def local_step(xs, tgt, comm, rg_conv_full, rg_gb_full, gd_conv_full, norm_w, rg_conv_b, rg_gate_w,
               rg_lambda, ml_gate_b, ml_norm_w, gd_a_log, gd_dt_bias, gd_norm_w, final_norm_w):
    acts = []
    for l in range(DEPTH):
        nw = norm_w[l].reshape(1, D)
        w_in_l, w_out_l = comm.weights(l)
        xf = comm.fwd_carry(l, "inproj")
        proj, hn_t, *landed = inproj(xs, nw, w_in_l, xf=xf)
        if xf is not None:
            comm.fwd_landed(l, "inproj", landed)
        rg_p = (rg_conv_full[l], rg_conv_b[l].reshape(1, SEG), _block_diag(rg_gate_w[l, 0]), _block_diag(rg_gate_w[l, 1]),
                rg_gb_full[l], rg_lambda[l].reshape(1, SEG))
        y_rg, hs = rglru_fwd(proj, *rg_p)
        ml_p = (jnp.zeros((1, HD), f32).at[0, 0:2 * NH].set(ml_gate_b[l].reshape(-1)), ml_norm_w[l].reshape(1, SEG))
        xf = comm.fwd_carry(l, "mlstm")
        y_ml, cs, ns, ms, *landed = mlstm_fwd(proj, *ml_p, xf=xf)
        if xf is not None:
            comm.fwd_landed(l, "mlstm", landed)
        gd_p = (gd_conv_full[l], _lane_row(gd_a_log[l], 2 * NH), _lane_row(gd_dt_bias[l], 2 * NH), gd_norm_w[l].reshape(1, HD))
        xf = comm.fwd_carry(l, "gdn")
        y_gd, ss, t_invs, *landed = gdn_fwd(proj, *gd_p, xf=xf)
        if xf is not None:
            comm.fwd_landed(l, "gdn", landed)
        acts.append((xs, nw, proj, hn_t, w_in_l, w_out_l, rg_p, y_rg, hs, ml_p, y_ml, cs, ns, ms, gd_p, y_gd, ss, t_invs))
        xs = outproj(xs, y_rg, y_ml, y_gd, w_out_l)

    dx, loss_part, d_fw = head(xs, final_norm_w.reshape(1, D), tgt)

    g_small = {k: [None] * DEPTH for k in ("norm_w", "rg_conv_w", "rg_conv_b", "rg_gate_w", "rg_gate_b", "rg_lambda",
                                           "ml_gate_b", "ml_norm_w", "gd_conv_w", "gd_a_log", "gd_dt_bias", "gd_norm_w")}
    for l in reversed(range(DEPTH)):
        x_l, nw, proj, hn_t, w_in_l, w_out_l, rg_p, y_rg, hs, ml_p, y_ml, cs, ns, ms, gd_p, y_gd, ss, t_invs = acts[l]
        dy_rg, dy_ml, dy_gd, g_wo = outproj_bwd(dx, y_rg, y_ml, y_gd, w_out_l)
        comm.own_w_out_grad(l, g_wo)
        xf = comm.bwd_carry(l, "rglru")
        d_rg, d_cw, d_cb, d_gr, d_gi, d_gb, d_lam, *landed = rglru_bwd(proj, hs, dy_rg, *rg_p, xf=xf)
        if xf is not None:
            comm.bwd_landed(landed)
        xf = comm.bwd_carry(l, "mlstm")
        d_ml, d_sm_ml, d_bias, d_mnw, *landed = mlstm_bwd(proj, dy_ml, cs, ns, ms, *ml_p, xf=xf)
        if xf is not None:
            comm.bwd_landed(landed)
        xf = comm.bwd_carry(l, "gdn")
        d_gd, d_sm_gd, d_gcw, d_al, d_dt, d_gnw, *landed = gdn_bwd(proj, dy_gd, ss, t_invs, *gd_p, xf=xf)
        if xf is not None:
            comm.bwd_landed(landed)
        dx, d_nw = inproj_bwd_x(x_l, nw, w_in_l, dx, d_rg, d_ml, d_gd, d_sm_ml, d_sm_gd)
        gw_rg = wgrad(hn_t, [d_rg], "wgrad_rg")
        gw_ml = wgrad(hn_t, [d_ml], "wgrad_ml")
        gw_gd = wgrad(hn_t, [d_gd], "wgrad_gd")
        gw_sm = wgrad(hn_t, [d_sm_ml, d_sm_gd], "wgrad_small")
        comm.grads_ready(l, [gw_rg, gw_ml, gw_sm[:, 0:2 * NH], gw_gd, gw_sm[:, 2 * NH:4 * NH]])
        g_small["norm_w"][l] = d_nw[0]
        g_small["rg_conv_w"][l] = d_cw
        g_small["rg_conv_b"][l] = d_cb[0]
        g_small["rg_gate_w"][l] = jnp.stack([_diag_blocks(d_gr), _diag_blocks(d_gi)])
        g_small["rg_gate_b"][l] = d_gb
        g_small["rg_lambda"][l] = d_lam[0]
        g_small["ml_gate_b"][l] = d_bias[0, 0:2 * NH].reshape(2, NH)
        g_small["ml_norm_w"][l] = d_mnw[0]
        g_small["gd_conv_w"][l] = d_gcw
        g_small["gd_a_log"][l] = d_al[0, 2 * NH:3 * NH]
        g_small["gd_dt_bias"][l] = d_dt[0, 2 * NH:3 * NH]
        g_small["gd_norm_w"][l] = d_gnw[0]
    return loss_part, dx, d_fw, g_small


def finish_step(loss_part, dx, d_fw, g_small, comm, s, me, given_w, given_m, given_v):
    small_names = ["norm_w", "rg_conv_w", "rg_conv_b", "rg_gate_w", "rg_gate_b", "rg_lambda", "ml_gate_b", "ml_norm_w",
                   "gd_conv_w", "gd_a_log", "gd_dt_bias", "gd_norm_w"]
    small_list = [loss_part[0, 0:1], d_fw[0]] + [jnp.stack(g_small[k]) for k in small_names]
    small_shapes = [a.shape for a in small_list]
    packed = _pack(small_list)
    packed = jnp.pad(packed, ((0, -packed.shape[0] % (8 * N_DEV)), (0, 0)))
    summed, gwi_r, gwo_r = small_allreduce(packed, "last_exchange", xf=comm.last_carry())
    g_all = _unpack(summed, small_shapes)

    g_wi, d_wi, m_wi, v_wi = adam_slots(gwi_r, given_w["w_in"], given_m["w_in"], given_v["w_in"], 256, "adam_w_in")
    g_wo, d_wo, m_wo, v_wo = adam_slots(gwo_r, given_w["w_out"], given_m["w_out"], given_v["w_out"], 192, "adam_w_out")
    loss = g_all[0][0]
    grads = {"final_norm_w": g_all[1]}
    for k, g in zip(small_names, g_all[2:]):
        grads[k] = g
    for k, width in (("rg_conv_w", 64), ("rg_gate_b", 64), ("gd_conv_w", 192)):
        grads[k] = lax.dynamic_slice_in_dim(grads[k], me * width, width, axis=2)
    names = small_names + ["final_norm_w"]
    shapes = [given_w[k].shape for k in names]
    d_p, m_p, v_p = adam_flat(_pack([grads[k] for k in names]), _pack([given_w[k] for k in names]),
                              _pack([given_m[k] for k in names]), _pack([given_v[k] for k in names]), "adam_small")
    deltas = dict(zip(names, _unpack(d_p, shapes)))
    new_m = dict(zip(names, _unpack(m_p, shapes)))
    new_v = dict(zip(names, _unpack(v_p, shapes)))
    grads["w_in"], deltas["w_in"], new_m["w_in"], new_v["w_in"] = g_wi, d_wi, m_wi, v_wi
    grads["w_out"], deltas["w_out"], new_m["w_out"], new_v["w_out"] = g_wo, d_wo, m_wo, v_wo

    order = ["norm_w", "w_in", "rg_conv_w", "rg_conv_b", "rg_gate_w", "rg_gate_b", "rg_lambda", "ml_gate_b", "ml_norm_w",
             "gd_conv_w", "gd_a_log", "gd_dt_bias", "gd_norm_w", "w_out", "final_norm_w"]
    return (loss, dx.reshape(1, s, D), *[grads[k] for k in order], *[deltas[k] for k in order],
            *[new_m[k] for k in order], *[new_v[k] for k in order])
```

```python
import functools
from typing import NamedTuple

import jax
import jax.numpy as jnp
from jax import lax
from jax.experimental import pallas as pl
from jax.experimental.pallas import tpu as pltpu

f32 = jnp.float32
bf16 = jnp.bfloat16
MESH = pl.DeviceIdType.MESH

N_DEV = 8
D = 1024
DEPTH = 4
EPS = 1e-6
SEG = 512
HD = 128
NH = 4
LC = 64
RUN = 4
LR = RUN * LC
D_IN = 5648
DP = 5760
SMALL0 = 5632
RG_TILE = 256
RG_C = 8.0

ADAM_LR = 0.001
ADAM_B1 = 0.9
ADAM_B2 = 0.999
ADAM_EPS = 1e-08
ADAM_WD = 0.01
ADAM_STEP = 10


def _cp(vmem_mb):
    return pltpu.CompilerParams(vmem_limit_bytes=vmem_mb * 2 ** 20)


def _dot(a, b, ca, cb):
    return lax.dot_general(a.astype(bf16), b.astype(bf16), (((ca,), (cb,)), ((), ())), preferred_element_type=f32)


@jax.custom_vjp
def mm_nn(a, b):
    return _dot(a, b, 1, 0)


@jax.custom_vjp
def mm_nt(a, b):
    return _dot(a, b, 1, 1)


@jax.custom_vjp
def mm_tn(a, b):
    return _dot(a, b, 0, 0)


mm_nn.defvjp(lambda a, b: (mm_nn(a, b), (a, b)), lambda r, g: (mm_nt(g, r[1]), mm_tn(r[0], g)))
mm_nt.defvjp(lambda a, b: (mm_nt(a, b), (a, b)), lambda r, g: (mm_nn(g, r[1]), mm_tn(g, r[0])))
mm_tn.defvjp(lambda a, b: (mm_tn(a, b), (a, b)), lambda r, g: (mm_nt(r[1], g), mm_nn(r[0], g)))


def _split(x):
    hi = x.astype(bf16)
    return hi, (x - hi.astype(f32)).astype(bf16)


def dot3(a, b):
    ah, al = _split(a)
    bh, bl = _split(b)
    d = functools.partial(jnp.dot, preferred_element_type=f32)
    return d(ah, bh) + (d(al, bh) + d(ah, bl))


def _tri_sum(x, reverse):
    n = x.shape[0]
    r = lax.broadcasted_iota(jnp.int32, (n, 3 * n), 0)
    c = lax.broadcasted_iota(jnp.int32, (n, 3 * n), 1) & (n - 1)
    ones = ((c >= r) if reverse else (c <= r)).astype(bf16)
    hi = x.astype(bf16)
    rest = x - hi.astype(f32)
    mid = rest.astype(bf16)
    lo = (rest - mid.astype(f32)).astype(bf16)
    return jnp.dot(ones, jnp.concatenate([hi, mid, lo], axis=0), preferred_element_type=f32)


@jax.custom_vjp
def cumsum_rows(x):
    return _tri_sum(x, False)


@jax.custom_vjp
def rev_cumsum_rows(x):
    return _tri_sum(x, True)


cumsum_rows.defvjp(lambda x: (cumsum_rows(x), None), lambda _, g: (rev_cumsum_rows(g),))
rev_cumsum_rows.defvjp(lambda x: (rev_cumsum_rows(x), None), lambda _, g: (cumsum_rows(g),))


def _tri(n, strict=False):
    r = lax.broadcasted_iota(jnp.int32, (n, n), 0)
    c = lax.broadcasted_iota(jnp.int32, (n, n), 1)
    return (r > c) if strict else (r >= c)


def _lane_col(v, j):
    lane = lax.broadcasted_iota(jnp.int32, v.shape, 1)
    return jnp.sum(jnp.where(lane == j, v, 0.0), axis=1, keepdims=True)


def _rows_from(pieces, rows, width):
    ridx = lax.broadcasted_iota(jnp.int32, (rows, width), 0)
    out = jnp.zeros((rows, width), f32)
    for h, p in enumerate(pieces):
        out = out + jnp.where(ridx == h, jnp.broadcast_to(p, (rows, width)), 0.0)
    return out


def causal_conv(halo8, x, w4):
    t = x.shape[0]
    xe = jnp.concatenate([halo8, x], axis=0)
    y = xe[5:5 + t] * w4[0:1]
    for k in range(1, 4):
        y = y + xe[5 + k:5 + k + t] * w4[k:k + 1]
    return y


def ml_chunk(q, k, v, o_pre, z, small, bias_row, norm_w, C, n, m):
    n_ch = q.shape[0] // LC
    lane = lax.broadcasted_iota(jnp.int32, small.shape, 1)
    pre = small + bias_row
    lg = jnp.where(lane < NH, pre, jnp.where(lane < 2 * NH, jax.nn.log_sigmoid(pre), 0.0))
    rows = [slice(c * LC, (c + 1) * LC) for c in range(n_ch)]
    lgs = [lg[r] for r in rows]
    bcs = [cumsum_rows(x) for x in lgs]
    lg_ts = [x.T for x in lgs]
    bc_ts = [x.T for x in bcs]
    causal = _tri(LC)
    pairs = [(c, h) for c in range(n_ch) for h in range(NH)]
    idx = range(len(pairs))
    sls = [slice(h * HD, (h + 1) * HD) for h in range(NH)]
    qs = [q[rows[c], sls[h]] * (HD ** -0.5) for c, h in pairs]
    ks = [k[rows[c], sls[h]] for c, h in pairs]
    vs = [v[rows[c], sls[h]] for c, h in pairs]
    li_cols = [_lane_col(lgs[c], h) for c, h in pairs]
    b_cols = [_lane_col(bcs[c], NH + h) for c, h in pairs]
    dms = [jnp.where(causal, b_cols[i] - bc_ts[c][NH + h:NH + h + 1, :] + lg_ts[c][h:h + 1, :], -jnp.inf)
           for i, (c, h) in enumerate(pairs)]
    dm_maxs = [jnp.max(dms[i], axis=-1, keepdims=True) for i in idx]
    gs = [b_cols[i][LC - 1:LC, :] for i in idx]
    wss = [gs[i] - b_cols[i] + li_cols[i] for i in idx]
    ws_maxs = [jnp.max(wss[i], axis=0, keepdims=True) for i in idx]
    qks = [mm_nt(qs[i], ks[i]) for i in idx]
    m_in = [None] * len(pairs)
    m_out = [None] * len(pairs)
    for h in range(NH):
        cur = m[h:h + 1, 0:1]
        for c in range(n_ch):
            i = c * NH + h
            m_in[i] = cur
            cur = jnp.maximum(gs[i] + cur, ws_maxs[i])
            m_out[i] = cur
    m_inters = [b_cols[i] + m_in[i] for i in idx]
    m_ts = [jnp.maximum(m_inters[i], dm_maxs[i]) for i in idx]
    ss = [qks[i] * jnp.exp(dms[i] - m_ts[i]) for i in idx]
    scs = [jnp.exp(m_inters[i] - m_ts[i]) for i in idx]
    decs = [jnp.exp(gs[i] + m_in[i] - m_out[i]) for i in idx]
    kws = [jnp.exp(wss[i] - m_out[i]) * ks[i] for i in idx]
    c_adds = [mm_tn(kws[i], vs[i]) for i in idx]
    n_adds = [jnp.sum(kws[i], axis=0, keepdims=True) for i in idx]
    svs = [mm_nn(ss[i], vs[i]) for i in idx]
    s_sums = [jnp.sum(ss[i], axis=-1, keepdims=True) for i in idx]
    c_hs = [C[sl, :] for sl in sls]
    n_hs = [n[h:h + 1, :] for h in range(NH)]
    hhs = [None] * len(pairs)
    for c in range(n_ch):
        for h in range(NH):
            i = c * NH + h
            num = svs[i] + scs[i] * mm_nn(qs[i], c_hs[h])
            den = s_sums[i] + scs[i] * jnp.sum(qs[i] * n_hs[h], axis=-1, keepdims=True)
            hhs[i] = num / jnp.maximum(jnp.abs(den), jnp.exp(-m_ts[i]))
        c_hs = [decs[c * NH + h] * c_hs[h] + c_adds[c * NH + h] for h in range(NH)]
        n_hs = [decs[c * NH + h] * n_hs[h] + n_adds[c * NH + h] for h in range(NH)]
    ys = [hhs[i] * lax.rsqrt(jnp.mean(hhs[i] * hhs[i], axis=-1, keepdims=True) + EPS) * norm_w[:, sls[h]]
          * jax.nn.sigmoid(o_pre[rows[c], sls[h]]) * jax.nn.silu(z[rows[c], sls[h]]) for i, (c, h) in enumerate(pairs)]
    y = jnp.concatenate([jnp.concatenate(ys[c * NH:(c + 1) * NH], axis=1) for c in range(n_ch)], axis=0)
    last = (n_ch - 1) * NH
    m_rows = [jnp.broadcast_to(m_out[last + h], (1, HD)) for h in range(NH)]
    return y, jnp.concatenate(c_hs, axis=0), _rows_from(n_hs, 8, HD), _rows_from(m_rows, 8, HD)


@jax.custom_vjp
def _unit_lower_inverses(mats):
    eye = (lax.broadcasted_iota(jnp.int32, (LC, LC), 0) == lax.broadcasted_iota(jnp.int32, (LC, LC), 1)).astype(f32)
    ps = [-m for m in mats]
    ts = [eye + p for p in ps]
    for _ in range(5):
        ps = [dot3(p, p) for p in ps]
        ts = [t + dot3(t, p) for t, p in zip(ts, ps)]
    return tuple(ts)


def _unit_lower_inverses_fwd(mats):
    ts = _unit_lower_inverses(mats)
    return ts, ts


def _unit_lower_inverses_bwd(ts, gs):
    tts = [t.T for t in ts]
    mid = [dot3(tt, g) for tt, g in zip(tts, gs)]
    return (tuple(-dot3(m, tt) for m, tt in zip(mid, tts)),)


_unit_lower_inverses.defvjp(_unit_lower_inverses_fwd, _unit_lower_inverses_bwd)


@jax.custom_vjp
def _known_inverses(mats, ts):
    return tuple(ts)


_known_inverses.defvjp(lambda mats, ts: (tuple(ts), tuple(ts)),
                       lambda ts, gs: (_unit_lower_inverses_bwd(ts, gs)[0], tuple(jnp.zeros_like(t) for t in ts)))


def gd_chunk(qh8, q, kh8, k, vh8, v, z, small, conv_w, alog_row, dt_row, norm_w, st, known_t=None):
    n_ch = q.shape[0] // LC
    lane = lax.broadcasted_iota(jnp.int32, small.shape, 1)
    is_a = (lane >= 2 * NH) & (lane < 3 * NH)
    g_all = jnp.where(is_a, -jnp.exp(alog_row) * jax.nn.softplus(small + dt_row), 0.0)
    beta_all = jax.nn.sigmoid(small)
    qc = jax.nn.silu(causal_conv(qh8, q, conv_w[:, 0:SEG]))
    kc = jax.nn.silu(causal_conv(kh8, k, conv_w[:, SEG:2 * SEG]))
    vc = jax.nn.silu(causal_conv(vh8, v, conv_w[:, 2 * SEG:3 * SEG]))
    incl = _tri(LC)
    strict = _tri(LC, strict=True)
    rows = [slice(c * LC, (c + 1) * LC) for c in range(n_ch)]
    gc_alls = [cumsum_rows(g_all[r]) for r in rows]
    gc_ts = [g.T for g in gc_alls]
    pairs = [(c, h) for c in range(n_ch) for h in range(NH)]
    idx = range(len(pairs))
    sls = [slice(h * HD, (h + 1) * HD) for h in range(NH)]
    qs = [qc[rows[c], sls[h]] for c, h in pairs]
    ks = [kc[rows[c], sls[h]] for c, h in pairs]
    vs = [vc[rows[c], sls[h]] for c, h in pairs]
    qs = [x * lax.rsqrt(jnp.sum(x * x, axis=-1, keepdims=True) + EPS) * (HD ** -0.5) for x in qs]
    ks = [x * lax.rsqrt(jnp.sum(x * x, axis=-1, keepdims=True) + EPS) for x in ks]
    betas = [_lane_col(beta_all[rows[c]], 3 * NH + h) for c, h in pairs]
    gcs = [_lane_col(gc_alls[c], 2 * NH + h) for c, h in pairs]
    gams = [jnp.exp(jnp.where(incl, gcs[i] - gc_ts[c][2 * NH + h:2 * NH + h + 1, :], -jnp.inf))
            for i, (c, h) in enumerate(pairs)]
    kbs = [ks[i] * betas[i] for i in idx]
    mats = tuple(jnp.where(strict, mm_nt(kbs[i], ks[i]) * gams[i], 0.0) for i in idx)
    aqks = [mm_nt(qs[i], ks[i]) * gams[i] for i in idx]
    if known_t is None:
        t_invs = _unit_lower_inverses(mats)
    else:
        t_invs = _known_inverses(mats, tuple(known_t[i] for i in idx))
    us = [mm_nn(t_invs[i], vs[i] * betas[i]) for i in idx]
    ws = [mm_nn(t_invs[i], kbs[i] * jnp.exp(gcs[i])) for i in idx]
    g_lasts = [gcs[i][LC - 1:LC, :] for i in idx]
    q_decs = [qs[i] * jnp.exp(gcs[i]) for i in idx]
    k_decs = [ks[i] * jnp.exp(g_lasts[i] - gcs[i]) for i in idx]
    e_lasts = [jnp.exp(g_lasts[i]) for i in idx]
    s_hs = [st[sl, :] for sl in sls]
    os_ = [None] * len(pairs)
    for c in range(n_ch):
        ids = [c * NH + h for h in range(NH)]
        v_news = [us[i] - mm_nn(ws[i], s_hs[h]) for h, i in enumerate(ids)]
        for h, i in enumerate(ids):
            os_[i] = mm_nn(q_decs[i], s_hs[h]) + mm_nn(aqks[i], v_news[h])
        s_hs = [s_hs[h] * e_lasts[i] + mm_tn(k_decs[i], v_news[h]) for h, i in enumerate(ids)]
    ys = [os_[i] * lax.rsqrt(jnp.mean(os_[i] * os_[i], axis=-1, keepdims=True) + EPS) * norm_w
          * jax.nn.silu(z[rows[c], sls[h]]) for i, (c, h) in enumerate(pairs)]
    y = jnp.concatenate([jnp.concatenate(ys[c * NH:(c + 1) * NH], axis=1) for c in range(n_ch)], axis=0)
    if known_t is None:
        return y, jnp.concatenate(s_hs, axis=0), jnp.stack(t_invs)
    return y, jnp.concatenate(s_hs, axis=0)


def rg_pre(xh8, x, conv_w, conv_b, gw_r, gw_i, gate_b, lam):
    xc = causal_conv(xh8, x, conv_w) + conv_b
    r = jax.nn.sigmoid(mm_nn(xc, gw_r) + gate_b[0:1])
    i = jax.nn.sigmoid(mm_nn(xc, gw_i) + gate_b[1:2])
    log_a = -RG_C * r * jax.nn.softplus(-lam)
    a = jnp.exp(log_a)
    th = jnp.tanh(log_a)
    one_minus_a2 = -2.0 * th / (1.0 - th)
    b = jnp.sqrt(one_minus_a2) * (i * xc)
    return a, b


def _scan_rows(a_ref, b_ref, out_ref, h0, n_rows, reverse):
    n_groups = n_rows // 8
    width = a_ref.shape[1]
    row = lax.broadcasted_iota(jnp.int32, (8, width), 0)

    def body(j, h):
        g = (n_groups - 1 - j) if reverse else j
        r0 = pl.multiple_of(g * 8, 8)
        av = a_ref[pl.ds(r0, 8), :]
        bv = b_ref[pl.ds(r0, 8), :]
        for d in (1, 2, 4):
            sh = (8 - d) if reverse else d
            a_s = pltpu.roll(av, sh, 0)
            b_s = pltpu.roll(bv, sh, 0)
            valid = (row < 8 - d) if reverse else (row >= d)
            bv = jnp.where(valid, av * b_s + bv, bv)
            av = jnp.where(valid, av * a_s, av)
        hv = av * h + bv
        out_ref[pl.ds(r0, 8), :] = hv
        return hv[0:1, :] if reverse else hv[7:8, :]

    return lax.fori_loop(0, n_groups, body, h0)


def _norm_rows(xv):
    r = lax.rsqrt(jnp.mean(xv * xv, axis=-1, keepdims=True) + EPS)
    return xv * r, r


def inproj(x, nw, w, xf=None):
    s, ts = x.shape[0], 256
    nt = s // ts
    n_in, n_out = 3, 2
    x_ops, x_in_specs, x_out_shapes, x_out_specs, aliases, x_sems, n_xin, n_land = _carried(xf, n_in, n_out)

    def body(*refs):
        (x_ref, nw_ref, w_ref), x_refs, (o_ref, ht_ref), land_refs, _, sem_refs = _cut_refs(refs, n_in, n_xin, n_out, n_land, 0)
        i = pl.program_id(0)
        _carry_open(xf, i, nt, x_refs, land_refs, sem_refs)
        xn, _ = _norm_rows(x_ref[...])
        hn = xn * nw_ref[...]
        o_ref[...] = jnp.dot(hn.astype(bf16), w_ref[...], preferred_element_type=f32)
        ht_ref[...] = hn.T.astype(bf16)
        _carry_close(xf, i, nt, x_refs, land_refs, sem_refs)

    return pl.pallas_call(
        body, grid=(nt,),
        in_specs=[pl.BlockSpec((ts, D), lambda i: (i, 0)), pl.BlockSpec((1, D), lambda i: (0, 0)),
                  pl.BlockSpec((D, DP), lambda i: (0, 0))] + x_in_specs,
        out_specs=[pl.BlockSpec((ts, DP), lambda i: (i, 0)), pl.BlockSpec((D, ts), lambda i: (0, i))] + x_out_specs,
        out_shape=[jax.ShapeDtypeStruct((s, DP), f32), jax.ShapeDtypeStruct((D, s), bf16)] + x_out_shapes,
        scratch_shapes=x_sems, input_output_aliases=aliases,
        name="inproj" if xf is None else "inproj_carrying", compiler_params=_cp(56))(x, nw, w, *x_ops)


def outproj(x, yr, ym, yg, wo):
    s, ts = x.shape[0], 256

    def body(x_ref, yr_ref, ym_ref, yg_ref, wo_ref, o_ref):
        acc = x_ref[...]
        for j, y_ref in enumerate((yr_ref, ym_ref, yg_ref)):
            acc = acc + jnp.dot(y_ref[...].astype(bf16), wo_ref[j * SEG:(j + 1) * SEG, :], preferred_element_type=f32)
        o_ref[...] = acc

    yspec = pl.BlockSpec((ts, SEG), lambda i: (i, 0))
    return pl.pallas_call(
        body, grid=(s // ts,),
        in_specs=[pl.BlockSpec((ts, D), lambda i: (i, 0)), yspec, yspec, yspec,
                  pl.BlockSpec((3 * SEG, D), lambda i: (0, 0))],
        out_specs=pl.BlockSpec((ts, D), lambda i: (i, 0)),
        out_shape=jax.ShapeDtypeStruct((s, D), f32), name="outproj", compiler_params=_cp(40))(x, yr, ym, yg, wo)


def head(x, fw, tgt):
    s, ts = x.shape[0], 256

    def body(x_ref, fw_ref, t_ref, dx_ref, loss_ref, dfw_ref):
        @pl.when(pl.program_id(0) == 0)
        def _():
            loss_ref[...] = jnp.zeros_like(loss_ref)
            dfw_ref[...] = jnp.zeros_like(dfw_ref)

        xn, r = _norm_rows(x_ref[...])
        fw_v = fw_ref[...]
        err = xn * fw_v - t_ref[...]
        loss_ref[...] += 0.5 * jnp.sum(jnp.mean(err * err, axis=-1, keepdims=True))
        dy = err * (1.0 / D)
        dfw_ref[0:1, :] += jnp.sum(dy * xn, axis=0, keepdims=True)
        dxn = dy * fw_v
        dx_ref[...] = r * (dxn - xn * jnp.mean(dxn * xn, axis=-1, keepdims=True))

    tile = pl.BlockSpec((ts, D), lambda i: (i, 0))
    return pl.pallas_call(
        body, grid=(s // ts,),
        in_specs=[tile, pl.BlockSpec((1, D), lambda i: (0, 0)), tile],
        out_specs=[tile, pl.BlockSpec((8, 128), lambda i: (0, 0)), pl.BlockSpec((8, D), lambda i: (0, 0))],
        out_shape=[jax.ShapeDtypeStruct((s, D), f32), jax.ShapeDtypeStruct((8, 128), f32),
                   jax.ShapeDtypeStruct((8, D), f32)], name="head")(x, fw, tgt)


def outproj_bwd(dx, yr, ym, yg, wo):
    s, ts = dx.shape[0], 256

    def body(dx_ref, yr_ref, ym_ref, yg_ref, wo_ref, dyr_ref, dym_ref, dyg_ref, gwo_ref):
        @pl.when(pl.program_id(0) == 0)
        def _():
            gwo_ref[...] = jnp.zeros_like(gwo_ref)

        dxb = dx_ref[...].astype(bf16)
        for j, (y_ref, dy_ref) in enumerate(((yr_ref, dyr_ref), (ym_ref, dym_ref), (yg_ref, dyg_ref))):
            rows = slice(j * SEG, (j + 1) * SEG)
            dy_ref[...] = lax.dot_general(dxb, wo_ref[rows, :], (((1,), (1,)), ((), ())), preferred_element_type=f32)
            gwo_ref[rows, :] += lax.dot_general(y_ref[...].astype(bf16), dxb, (((0,), (0,)), ((), ())),
                                                preferred_element_type=f32)

    yspec = pl.BlockSpec((ts, SEG), lambda i: (i, 0))
    wspec = pl.BlockSpec((3 * SEG, D), lambda i: (0, 0))
    ysh = jax.ShapeDtypeStruct((s, SEG), f32)
    return pl.pallas_call(
        body, grid=(s // ts,),
        in_specs=[pl.BlockSpec((ts, D), lambda i: (i, 0)), yspec, yspec, yspec, wspec],
        out_specs=[yspec, yspec, yspec, wspec],
        out_shape=[ysh, ysh, ysh, jax.ShapeDtypeStruct((3 * SEG, D), f32)],
        name="outproj_bwd", compiler_params=_cp(48))(dx, yr, ym, yg, wo)


def inproj_bwd_x(x, nw, w, dxo, d_rg, d_ml, d_gd, d_sa, d_sb):
    s, ts = x.shape[0], 256
    widths = (d_rg.shape[1], d_ml.shape[1], d_gd.shape[1], HD)

    def body(x_ref, nw_ref, w_ref, dxo_ref, rg_ref, ml_ref, gd_ref, sa_ref, sb_ref, dx_ref, dnw_ref):
        @pl.when(pl.program_id(0) == 0)
        def _():
            dnw_ref[...] = jnp.zeros_like(dnw_ref)

        xn, r = _norm_rows(x_ref[...])
        pieces = (rg_ref[...], ml_ref[...], gd_ref[...], sa_ref[...] + sb_ref[...])
        dhn = jnp.zeros((ts, D), f32)
        c0 = 0
        for piece, wd in zip(pieces, widths):
            dhn = dhn + lax.dot_general(piece.astype(bf16), w_ref[:, c0:c0 + wd], (((1,), (1,)), ((), ())),
                                        preferred_element_type=f32)
            c0 += wd
        dnw_ref[0:1, :] += jnp.sum(dhn * xn, axis=0, keepdims=True)
        dxn = dhn * nw_ref[...]
        dx_ref[...] = dxo_ref[...] + r * (dxn - xn * jnp.mean(dxn * xn, axis=-1, keepdims=True))

    tile = pl.BlockSpec((ts, D), lambda i: (i, 0))
    return pl.pallas_call(
        body, grid=(s // ts,),
        in_specs=[tile, pl.BlockSpec((1, D), lambda i: (0, 0)), pl.BlockSpec((D, DP), lambda i: (0, 0)), tile]
        + [pl.BlockSpec((ts, wd), lambda i: (i, 0)) for wd in widths] + [pl.BlockSpec((ts, HD), lambda i: (i, 0))],
        out_specs=[tile, pl.BlockSpec((8, D), lambda i: (0, 0))],
        out_shape=[jax.ShapeDtypeStruct((s, D), f32), jax.ShapeDtypeStruct((8, D), f32)],
        name="inproj_bwd_x", compiler_params=_cp(56))(x, nw, w, dxo, d_rg, d_ml, d_gd, d_sa, d_sb)


def wgrad(hn_t, dps, name):
    s = hn_t.shape[1]
    c = dps[0].shape[1]
    ct = min(c, SEG)
    n_dp = len(dps)

    def body(*refs):
        ht_ref = refs[0]
        dp_refs = refs[1:1 + n_dp]
        o_ref = refs[1 + n_dp]
        dp = dp_refs[0][...]
        for extra in dp_refs[1:]:
            dp = dp + extra[...]
        o_ref[...] = jnp.dot(ht_ref[...], dp.astype(bf16), preferred_element_type=f32).astype(bf16)

    return pl.pallas_call(
        body, grid=(c // ct,),
        in_specs=[pl.BlockSpec((D, s), lambda j: (0, 0))] + [pl.BlockSpec((s, ct), lambda j: (0, j)) for _ in dps],
        out_specs=pl.BlockSpec((D, ct), lambda j: (0, j)),
        out_shape=jax.ShapeDtypeStruct((D, c), bf16), name=name, compiler_params=_cp(40))(hn_t, *dps)


def _seg_spec(rows, seg, n_tiles=None):
    if n_tiles is None:
        return pl.BlockSpec((rows, SEG), lambda i: (i, seg))
    return pl.BlockSpec((rows, SEG), lambda i: (n_tiles - 1 - i, seg))


def _halo_spec(rows, seg, n_tiles=None):
    per = rows // 8
    if n_tiles is None:
        return pl.BlockSpec((8, SEG), lambda i: (jnp.maximum(i * per - 1, 0), seg))
    return pl.BlockSpec((8, SEG), lambda i: (jnp.maximum((n_tiles - 1 - i) * per - 1, 0), seg))


def _const_spec(shape):
    return pl.BlockSpec(shape, lambda i: tuple(0 for _ in shape))


def rglru_fwd(proj, conv_w, conv_b, gw_r, gw_i, gate_b, lam):
    s = proj.shape[0]
    tr = RG_TILE

    def body(xh_ref, x_ref, z_ref, cw_ref, cb_ref, gr_ref, gi_ref, gb_ref, lam_ref, y_ref, h_ref, a_s, b_s, hc):
        first = pl.program_id(0) == 0

        @pl.when(first)
        def _():
            hc[...] = jnp.zeros_like(hc)

        xh = jnp.where(first, 0.0, xh_ref[...])
        a, b = rg_pre(xh, x_ref[...], cw_ref[...], cb_ref[...], gr_ref[...], gi_ref[...], gb_ref[...], lam_ref[...])
        a_s[...] = a
        b_s[...] = b
        hc[0:1, :] = _scan_rows(a_s, b_s, h_ref, hc[0:1, :], tr, False)
        y_ref[...] = (h_ref[...] * jax.nn.silu(z_ref[...])).astype(bf16)

    out = pl.BlockSpec((tr, SEG), lambda i: (i, 0))
    return pl.pallas_call(
        body, grid=(s // tr,),
        in_specs=[_halo_spec(tr, 0), _seg_spec(tr, 0), _seg_spec(tr, 1), _const_spec((4, SEG)), _const_spec((1, SEG)),
                  _const_spec((SEG, SEG)), _const_spec((SEG, SEG)), _const_spec((2, SEG)), _const_spec((1, SEG))],
        out_specs=[out, out],
        out_shape=[jax.ShapeDtypeStruct((s, SEG), bf16), jax.ShapeDtypeStruct((s, SEG), f32)],
        scratch_shapes=[pltpu.VMEM((tr, SEG), f32), pltpu.VMEM((tr, SEG), f32), pltpu.VMEM((8, SEG), f32)],
        name="rglru_fwd", compiler_params=_cp(40))(proj, proj, proj, conv_w, conv_b, gw_r, gw_i, gate_b, lam)


def rglru_bwd(proj, hs, dy, conv_w, conv_b, gw_r, gw_i, gate_b, lam, xf=None):
    s = proj.shape[0]
    tr = RG_TILE
    nt = s // tr
    n_in, n_out = 12, 7
    x_ops, x_in_specs, x_out_shapes, x_out_specs, aliases, x_sems, n_xin, n_land = _carried(xf, n_in, n_out)

    def body(*refs):
        ins, x_refs, outs, land_refs, scratch, sem_refs = _cut_refs(refs, n_in, n_xin, n_out, n_land, 6)
        xh_ref, x_ref, z_ref, hh_ref, h_ref, dy_ref, cw_ref, cb_ref, gr_ref, gi_ref, gb_ref, lam_ref = ins
        dp_ref, dcw_ref, dcb_ref, dgr_ref, dgi_ref, dgb_ref, dlam_ref = outs
        an_s, g_s, dh_s, a_first, dh_first, dhalo = scratch
        i = pl.program_id(0)
        first_tile = i == nt - 1
        _carry_open(xf, i, nt, x_refs, land_refs, sem_refs)

        @pl.when(i == 0)
        def _():
            for ref in (dcw_ref, dcb_ref, dgr_ref, dgi_ref, dgb_ref, dlam_ref, a_first, dh_first, dhalo):
                ref[...] = jnp.zeros_like(ref)

        xh = jnp.where(first_tile, 0.0, xh_ref[...])
        params = (cw_ref[...], cb_ref[...], gr_ref[...], gi_ref[...], gb_ref[...], lam_ref[...])
        (a, _), vjp = jax.vjp(rg_pre, xh, x_ref[...], *params)
        zv = z_ref[...]
        hv = h_ref[...]
        dyv = dy_ref[...]
        sig = jax.nn.sigmoid(zv)
        g_s[...] = dyv * (zv * sig)
        dp_ref[:, SEG:2 * SEG] = (dyv * hv * (sig * (1.0 + zv * (1.0 - sig)))).astype(bf16)
        ridx = lax.broadcasted_iota(jnp.int32, (tr, SEG), 0)
        an_s[...] = jnp.where(ridx == tr - 1, jnp.broadcast_to(a_first[0:1, :], (tr, SEG)), pltpu.roll(a, tr - 1, 0))
        _scan_rows(an_s, g_s, dh_s, dh_first[0:1, :], tr, True)
        dh = dh_s[...]
        h_prev_last = jnp.where(first_tile, 0.0, hh_ref[...])[7:8, :]
        h_prev = pltpu.roll(hv, 1, 0)
        h_prev = jnp.where(ridx == 0, jnp.broadcast_to(h_prev_last, (tr, SEG)), h_prev)
        dxh, dx, dcw, dcb, dgr, dgi, dgb, dlam = vjp((dh * h_prev, dh))
        dx = dx + jnp.concatenate([jnp.zeros((tr - 8, SEG), f32), dhalo[...]], axis=0)
        dp_ref[:, 0:SEG] = dx.astype(bf16)
        dhalo[...] = dxh
        a_first[0:1, :] = a[0:1, :]
        dh_first[0:1, :] = dh[0:1, :]
        dcw_ref[...] += dcw
        dcb_ref[...] += dcb
        dgr_ref[...] += dgr
        dgi_ref[...] += dgi
        dgb_ref[...] += dgb
        dlam_ref[...] += dlam
        _carry_close(xf, i, nt, x_refs, land_refs, sem_refs)

    pspecs = [_const_spec((4, SEG)), _const_spec((1, SEG)), _const_spec((SEG, SEG)), _const_spec((SEG, SEG)),
              _const_spec((2, SEG)), _const_spec((1, SEG))]
    pshapes = [jax.ShapeDtypeStruct(sh, f32) for sh in ((4, SEG), (1, SEG), (SEG, SEG), (SEG, SEG), (2, SEG), (1, SEG))]
    tile = pl.BlockSpec((tr, SEG), lambda i: (nt - 1 - i, 0))
    return pl.pallas_call(
        body, grid=(nt,),
        in_specs=[_halo_spec(tr, 0, nt), _seg_spec(tr, 0, nt), _seg_spec(tr, 1, nt),
                  pl.BlockSpec((8, SEG), lambda i: (jnp.maximum((nt - 1 - i) * (tr // 8) - 1, 0), 0)), tile, tile] + pspecs
        + x_in_specs,
        out_specs=[pl.BlockSpec((tr, 2 * SEG), lambda i: (nt - 1 - i, 0))] + pspecs + x_out_specs,
        out_shape=[jax.ShapeDtypeStruct((s, 2 * SEG), bf16)] + pshapes + x_out_shapes,
        scratch_shapes=[pltpu.VMEM((tr, SEG), f32), pltpu.VMEM((tr, SEG), f32), pltpu.VMEM((tr, SEG), f32),
                        pltpu.VMEM((8, SEG), f32), pltpu.VMEM((8, SEG), f32), pltpu.VMEM((8, SEG), f32)] + x_sems,
        input_output_aliases=aliases, name="rglru_bwd" if xf is None else "rglru_bwd_carrying", compiler_params=_cp(48))(
            proj, proj, proj, hs, hs, dy, conv_w, conv_b, gw_r, gw_i, gate_b, lam, *x_ops)


ML_SEGS = (2, 3, 4, 5, 6)
SMALL_BLK = SMALL0 // HD


def _cut_refs(refs, n_in, n_xin, n_out, n_land, n_scratch):
    bounds = [0, n_in, n_in + n_xin, n_in + n_xin + n_out, n_in + n_xin + n_out + n_land,
              n_in + n_xin + n_out + n_land + n_scratch, len(refs)]
    return [refs[a:b] for a, b in zip(bounds[:-1], bounds[1:])]


def mlstm_fwd(proj, bias_row, norm_w, xf=None):
    s = proj.shape[0]
    nc = s // LR
    n_in, n_out = 8, 4
    x_ops, x_in_specs, x_out_shapes, x_out_specs, aliases, x_sems, n_xin, n_land = _carried(xf, n_in, n_out)

    def body(*refs):
        ins, x_refs, outs, land_refs, scratch, sem_refs = _cut_refs(refs, n_in, n_xin, n_out, n_land, 3)
        q_ref, k_ref, v_ref, o_ref, z_ref, sm_ref, b_ref, nw_ref = ins
        y_ref, cs_ref, ns_ref, ms_ref = outs
        c_s, n_s, m_s = scratch
        i = pl.program_id(0)
        _carry_open(xf, i, nc, x_refs, land_refs, sem_refs)

        @pl.when(i == 0)
        def _():
            c_s[...] = jnp.zeros_like(c_s)
            n_s[...] = jnp.zeros_like(n_s)
            m_s[...] = jnp.zeros_like(m_s)

        cs_ref[0] = c_s[...]
        ns_ref[0] = n_s[...]
        ms_ref[0] = m_s[...]
        y, c2, n2, m2 = ml_chunk(q_ref[...], k_ref[...], v_ref[...], o_ref[...], z_ref[...], sm_ref[...],
                                 b_ref[...], nw_ref[...], c_s[...], n_s[...], m_s[...])
        y_ref[...] = y.astype(bf16)
        c_s[...] = c2
        n_s[...] = n2
        m_s[...] = m2
        _carry_close(xf, i, nc, x_refs, land_refs, sem_refs)

    return pl.pallas_call(
        body, grid=(nc,),
        in_specs=[_seg_spec(LR, sg) for sg in ML_SEGS]
        + [pl.BlockSpec((LR, HD), lambda i: (i, SMALL_BLK)), _const_spec((1, HD)), _const_spec((1, SEG))] + x_in_specs,
        out_specs=[pl.BlockSpec((LR, SEG), lambda i: (i, 0)), pl.BlockSpec((1, SEG, HD), lambda i: (i, 0, 0)),
                   pl.BlockSpec((1, 8, HD), lambda i: (i, 0, 0)), pl.BlockSpec((1, 8, HD), lambda i: (i, 0, 0))] + x_out_specs,
        out_shape=[jax.ShapeDtypeStruct((s, SEG), bf16), jax.ShapeDtypeStruct((nc, SEG, HD), f32),
                   jax.ShapeDtypeStruct((nc, 8, HD), f32), jax.ShapeDtypeStruct((nc, 8, HD), f32)] + x_out_shapes,
        scratch_shapes=[pltpu.VMEM((SEG, HD), f32), pltpu.VMEM((8, HD), f32), pltpu.VMEM((8, HD), f32)] + x_sems,
        input_output_aliases=aliases, name="mlstm_fwd" if xf is None else "mlstm_fwd_carrying")(
            proj, proj, proj, proj, proj, proj, bias_row, norm_w, *x_ops)


def mlstm_bwd(proj, dy, cs, ns, ms, bias_row, norm_w, xf=None):
    s = proj.shape[0]
    nc = s // LR
    n_in, n_out = 12, 4
    x_ops, x_in_specs, x_out_shapes, x_out_specs, aliases, x_sems, n_xin, n_land = _carried(xf, n_in, n_out)

    def body(*refs):
        ins, x_refs, outs, land_refs, scratch, sem_refs = _cut_refs(refs, n_in, n_xin, n_out, n_land, 3)
        q_ref, k_ref, v_ref, o_ref, z_ref, sm_ref, dy_ref, cs_ref, ns_ref, ms_ref, b_ref, nw_ref = ins
        dp_ref, dsm_ref, db_ref, dnw_ref = outs
        dc_s, dn_s, dm_s = scratch
        i = pl.program_id(0)
        _carry_open(xf, i, nc, x_refs, land_refs, sem_refs)

        @pl.when(i == 0)
        def _():
            for ref in (db_ref, dnw_ref, dc_s, dn_s, dm_s):
                ref[...] = jnp.zeros_like(ref)

        _, vjp = jax.vjp(ml_chunk, q_ref[...], k_ref[...], v_ref[...], o_ref[...], z_ref[...], sm_ref[...],
                         b_ref[...], nw_ref[...], cs_ref[0], ns_ref[0], ms_ref[0])
        dq, dk, dv, do, dz, dsm, db, dnw, dc, dn, dm = vjp((dy_ref[...], dc_s[...], dn_s[...], dm_s[...]))
        for j, val in enumerate((dq, dk, dv, do, dz)):
            dp_ref[:, j * SEG:(j + 1) * SEG] = val.astype(bf16)
        dsm_ref[...] = dsm
        db_ref[0:1, :] += db
        dnw_ref[0:1, :] += dnw
        dc_s[...] = dc
        dn_s[...] = dn
        dm_s[...] = dm
        _carry_close(xf, i, nc, x_refs, land_refs, sem_refs)

    rev3 = lambda i: (nc - 1 - i, 0, 0)
    return pl.pallas_call(
        body, grid=(nc,),
        in_specs=[_seg_spec(LR, sg, nc) for sg in ML_SEGS]
        + [pl.BlockSpec((LR, HD), lambda i: (nc - 1 - i, SMALL_BLK)), pl.BlockSpec((LR, SEG), lambda i: (nc - 1 - i, 0)),
           pl.BlockSpec((1, SEG, HD), rev3), pl.BlockSpec((1, 8, HD), rev3), pl.BlockSpec((1, 8, HD), rev3),
           _const_spec((1, HD)), _const_spec((1, SEG))] + x_in_specs,
        out_specs=[pl.BlockSpec((LR, 5 * SEG), lambda i: (nc - 1 - i, 0)), pl.BlockSpec((LR, HD), lambda i: (nc - 1 - i, 0)),
                   _const_spec((8, HD)), _const_spec((8, SEG))] + x_out_specs,
        out_shape=[jax.ShapeDtypeStruct((s, 5 * SEG), bf16), jax.ShapeDtypeStruct((s, HD), f32),
                   jax.ShapeDtypeStruct((8, HD), f32), jax.ShapeDtypeStruct((8, SEG), f32)] + x_out_shapes,
        scratch_shapes=[pltpu.VMEM((SEG, HD), f32), pltpu.VMEM((8, HD), f32), pltpu.VMEM((8, HD), f32)] + x_sems,
        input_output_aliases=aliases, name="mlstm_bwd" if xf is None else "mlstm_bwd_carrying", compiler_params=_cp(48))(
            proj, proj, proj, proj, proj, proj, dy, cs, ns, ms, bias_row, norm_w, *x_ops)


GD_SEGS = (7, 8, 9)


def _carried(xf, n_in, n_out):
    operands, out_shapes, aliases, sems, _, n_xin, n_land = _carry_plumb(xf, n_in, n_out)
    return operands, [_ANY] * n_xin, out_shapes, [_ANY] * n_land, aliases, sems, n_xin, n_land


def _carry_open(xf, i, n_steps, x_refs, land_refs, sem_refs):
    if xf is None:
        return
    srcs = x_refs[:len(xf.srcs)]

    @pl.when(i == 0)
    def _():
        _carry_start(xf, srcs, land_refs, sem_refs)

    @pl.when(i == max(n_steps - 2, 0))
    def _():
        _carry_middle(xf, srcs, land_refs, sem_refs)


def _carry_close(xf, i, n_steps, x_refs, land_refs, sem_refs):
    if xf is None:
        return

    @pl.when(i == n_steps - 1)
    def _():
        _carry_finish(xf, x_refs[:len(xf.srcs)], land_refs, sem_refs)


def gdn_fwd(proj, conv_w, alog_row, dt_row, norm_w, xf=None):
    s = proj.shape[0]
    nc = s // LR
    n_in, n_out = 12, 3
    x_ops, x_in_specs, x_out_shapes, x_out_specs, aliases, x_sems, n_xin, n_land = _carried(xf, n_in, n_out)

    def body(*refs):
        qh_ref, q_ref, kh_ref, k_ref, vh_ref, v_ref, z_ref, sm_ref, cw_ref, al_ref, dt_ref, nw_ref = refs[:n_in]
        x_refs = refs[n_in:n_in + n_xin]
        y_ref, ss_ref, ti_ref = refs[n_in + n_xin:n_in + n_xin + n_out]
        land_refs = refs[n_in + n_xin + n_out:n_in + n_xin + n_out + n_land]
        st_s = refs[n_in + n_xin + n_out + n_land]
        sem_refs = refs[n_in + n_xin + n_out + n_land + 1:]
        i = pl.program_id(0)
        first = i == 0
        _carry_open(xf, i, nc, x_refs, land_refs, sem_refs)

        @pl.when(first)
        def _():
            st_s[...] = jnp.zeros_like(st_s)

        ss_ref[0] = st_s[...]
        halo = [jnp.where(first, 0.0, r[...]) for r in (qh_ref, kh_ref, vh_ref)]
        y, st2, t_invs = gd_chunk(halo[0], q_ref[...], halo[1], k_ref[...], halo[2], v_ref[...], z_ref[...], sm_ref[...],
                                  cw_ref[...], al_ref[...], dt_ref[...], nw_ref[...], st_s[...])
        y_ref[...] = y.astype(bf16)
        ti_ref[...] = t_invs
        st_s[...] = st2
        _carry_close(xf, i, nc, x_refs, land_refs, sem_refs)

    qkv_specs = []
    for sg in GD_SEGS:
        qkv_specs += [_halo_spec(LR, sg), _seg_spec(LR, sg)]
    return pl.pallas_call(
        body, grid=(nc,),
        in_specs=qkv_specs + [_seg_spec(LR, 10), pl.BlockSpec((LR, HD), lambda i: (i, SMALL_BLK)),
                              _const_spec((4, 3 * SEG)), _const_spec((1, HD)), _const_spec((1, HD)), _const_spec((1, HD))]
        + x_in_specs,
        out_specs=[pl.BlockSpec((LR, SEG), lambda i: (i, 0)), pl.BlockSpec((1, SEG, HD), lambda i: (i, 0, 0)),
                   pl.BlockSpec((RUN * NH, LC, LC), lambda i: (i, 0, 0))] + x_out_specs,
        out_shape=[jax.ShapeDtypeStruct((s, SEG), bf16), jax.ShapeDtypeStruct((nc, SEG, HD), f32),
                   jax.ShapeDtypeStruct((s // LC * NH, LC, LC), f32)] + x_out_shapes,
        scratch_shapes=[pltpu.VMEM((SEG, HD), f32)] + x_sems, input_output_aliases=aliases,
        name="gdn_fwd" if xf is None else "gdn_fwd_carrying")(
            proj, proj, proj, proj, proj, proj, proj, proj, conv_w, alog_row, dt_row, norm_w, *x_ops)


def gdn_bwd(proj, dy, ss, t_invs, conv_w, alog_row, dt_row, norm_w, xf=None):
    s = proj.shape[0]
    nc = s // LR
    n_in, n_out = 15, 6
    x_ops, x_in_specs, x_out_shapes, x_out_specs, aliases, x_sems, n_xin, n_land = _carried(xf, n_in, n_out)

    def body(*refs):
        (qh_ref, q_ref, kh_ref, k_ref, vh_ref, v_ref, z_ref, sm_ref, dy_ref, ss_ref, ti_ref,
         cw_ref, al_ref, dt_ref, nw_ref) = refs[:n_in]
        x_refs = refs[n_in:n_in + n_xin]
        dp_ref, dsm_ref, dcw_ref, dal_ref, ddt_ref, dnw_ref = refs[n_in + n_xin:n_in + n_xin + n_out]
        land_refs = refs[n_in + n_xin + n_out:n_in + n_xin + n_out + n_land]
        dst_s, dhalo = refs[n_in + n_xin + n_out + n_land:n_in + n_xin + n_out + n_land + 2]
        sem_refs = refs[n_in + n_xin + n_out + n_land + 2:]
        i = pl.program_id(0)
        first_chunk = i == nc - 1
        _carry_open(xf, i, nc, x_refs, land_refs, sem_refs)

        @pl.when(i == 0)
        def _():
            for ref in (dcw_ref, dal_ref, ddt_ref, dnw_ref, dst_s, dhalo):
                ref[...] = jnp.zeros_like(ref)

        halo = [jnp.where(first_chunk, 0.0, r[...]) for r in (qh_ref, kh_ref, vh_ref)]
        with_known = functools.partial(gd_chunk, known_t=ti_ref[...])
        _, vjp = jax.vjp(with_known, halo[0], q_ref[...], halo[1], k_ref[...], halo[2], v_ref[...], z_ref[...], sm_ref[...],
                         cw_ref[...], al_ref[...], dt_ref[...], nw_ref[...], ss_ref[0])
        dqh, dq, dkh, dk, dvh, dv, dz, dsm, dcw, dal, ddt, dnw, dst = vjp((dy_ref[...], dst_s[...]))
        for j, val in enumerate((dq, dk, dv)):
            val = val + jnp.concatenate([jnp.zeros((LR - 8, SEG), f32), dhalo[:, j * SEG:(j + 1) * SEG]], axis=0)
            dp_ref[:, j * SEG:(j + 1) * SEG] = val.astype(bf16)
        dp_ref[:, 3 * SEG:4 * SEG] = dz.astype(bf16)
        for j, val in enumerate((dqh, dkh, dvh)):
            dhalo[:, j * SEG:(j + 1) * SEG] = val
        dsm_ref[...] = dsm
        dcw_ref[...] += dcw
        dal_ref[0:1, :] += dal
        ddt_ref[0:1, :] += ddt
        dnw_ref[0:1, :] += dnw
        dst_s[...] = dst
        _carry_close(xf, i, nc, x_refs, land_refs, sem_refs)

    qkv_specs = []
    for sg in GD_SEGS:
        qkv_specs += [_halo_spec(LR, sg, nc), _seg_spec(LR, sg, nc)]
    return pl.pallas_call(
        body, grid=(nc,),
        in_specs=qkv_specs + [_seg_spec(LR, 10, nc), pl.BlockSpec((LR, HD), lambda i: (nc - 1 - i, SMALL_BLK)),
                              pl.BlockSpec((LR, SEG), lambda i: (nc - 1 - i, 0)),
                              pl.BlockSpec((1, SEG, HD), lambda i: (nc - 1 - i, 0, 0)),
                              pl.BlockSpec((RUN * NH, LC, LC), lambda i: (nc - 1 - i, 0, 0)),
                              _const_spec((4, 3 * SEG)), _const_spec((1, HD)), _const_spec((1, HD)), _const_spec((1, HD))]
        + x_in_specs,
        out_specs=[pl.BlockSpec((LR, 4 * SEG), lambda i: (nc - 1 - i, 0)), pl.BlockSpec((LR, HD), lambda i: (nc - 1 - i, 0)),
                   _const_spec((4, 3 * SEG)), _const_spec((8, HD)), _const_spec((8, HD)), _const_spec((8, HD))] + x_out_specs,
        out_shape=[jax.ShapeDtypeStruct((s, 4 * SEG), bf16), jax.ShapeDtypeStruct((s, HD), f32),
                   jax.ShapeDtypeStruct((4, 3 * SEG), f32), jax.ShapeDtypeStruct((8, HD), f32),
                   jax.ShapeDtypeStruct((8, HD), f32), jax.ShapeDtypeStruct((8, HD), f32)] + x_out_shapes,
        scratch_shapes=[pltpu.VMEM((SEG, HD), f32), pltpu.VMEM((8, 3 * SEG), f32)] + x_sems, input_output_aliases=aliases,
        name="gdn_bwd" if xf is None else "gdn_bwd_carrying", compiler_params=_cp(48))(
            proj, proj, proj, proj, proj, proj, proj, proj, dy, ss, t_invs, conv_w, alog_row, dt_row, norm_w, *x_ops)


def _my_place():
    return lax.axis_index("x"), lax.axis_index("y"), lax.axis_index("c")


def _slot(p):
    return 4 * p[0] + 2 * p[1] + p[2]


def _peer(me, j):
    flips = ((j >> 2) & 1, (j >> 1) & 1, j & 1)
    return tuple((1 - v) if fl else v for v, fl in zip(me, flips))


_ANY = pl.BlockSpec(memory_space=pl.ANY)


class Scatter(NamedTuple):
    srcs: list
    lands: list
    land_of: list
    layer: list
    span: list


class Gather2(NamedTuple):
    srcs: list
    lands: list
    span: list


def _carry_plumb(cx, n_in, n_out):
    if cx is None:
        return [], [], {}, [], 0, 0, 0
    n_src = len(cx.srcs)
    passed = [li for li, ld in enumerate(cx.lands) if not isinstance(ld, jax.ShapeDtypeStruct)]
    operands = list(cx.srcs) + [cx.lands[li] for li in passed]
    aliases = {n_in + n_src + k: n_out + li for k, li in enumerate(passed)}
    out_shapes = [jax.ShapeDtypeStruct(ld.shape, ld.dtype) for ld in cx.lands]
    sems = [pltpu.SemaphoreType.DMA((n_src, N_DEV - 1)), pltpu.SemaphoreType.DMA((n_src, N_DEV - 1)),
            pltpu.SemaphoreType.DMA((n_src,))]
    return operands, out_shapes, aliases, sems, n_src, len(operands), len(cx.lands)


def _scatter_copies(sc, src_refs, land_refs, send_sems, recv_sems, local_sems):
    me = _my_place()
    mine = _slot(me)
    local, remote = [], []
    for a, src_ref in enumerate(src_refs):
        lo, hi = sc.span[a]
        rows = pl.ds(lo, hi - lo)
        dst = land_refs[sc.land_of[a]].at[mine, sc.layer[a], rows]
        local.append(pltpu.make_async_copy(src_ref.at[mine, rows], dst, local_sems.at[a]))
        for j in range(1, N_DEV):
            peer = _peer(me, j)
            remote.append(pltpu.make_async_remote_copy(
                src_ref=src_ref.at[_slot(peer), rows], dst_ref=dst, send_sem=send_sems.at[a, j - 1],
                recv_sem=recv_sems.at[a, j - 1], device_id=peer, device_id_type=MESH))
    return local, remote


def _gather2_copy(land, sems, a, k, block_of, to, rows, src=None):
    dst = land.at[_slot(block_of), rows]
    return pltpu.make_async_remote_copy(src_ref=dst if src is None else src, dst_ref=dst, send_sem=sems[0].at[a, k],
                                        recv_sem=sems[1].at[a, k], device_id=to, device_id_type=MESH)


def _gather2_places():
    x, y, c = _my_place()
    return (x, y, c), (x, y, 1 - c), [(1 - x, y), (x, 1 - y), (1 - x, 1 - y)], c


def _carry_start(cx, src_refs, land_refs, sems):
    if isinstance(cx, Scatter):
        local, remote = _scatter_copies(cx, src_refs, land_refs, *sems)
        for cp in local + remote:
            cp.start()
        return
    me, sib, chips, c = _gather2_places()
    for a, (src_ref, land) in enumerate(zip(src_refs, land_refs)):
        rows = pl.ds(cx.span[a][0], cx.span[a][1] - cx.span[a][0])
        src = src_ref.at[rows]
        pltpu.make_async_copy(src, land.at[_slot(me), rows], sems[2].at[a]).start()
        _gather2_copy(land, sems, a, 0, me, sib, rows, src=src).start()
        for j, chip in enumerate(chips):
            _gather2_copy(land, sems, a, 1 + j, me, (*chip, c), rows, src=src).start()


def _carry_middle(cx, src_refs, land_refs, sems):
    if isinstance(cx, Scatter):
        return
    me, sib, chips, c = _gather2_places()
    for a, land in enumerate(land_refs):
        rows = pl.ds(cx.span[a][0], cx.span[a][1] - cx.span[a][0])
        for j, chip in enumerate(chips):
            _gather2_copy(land, sems, a, 1 + j, (*chip, c), me, rows).wait_recv()
            _gather2_copy(land, sems, a, 4 + j, (*chip, c), sib, rows).start()


def _carry_finish(cx, src_refs, land_refs, sems):
    if isinstance(cx, Scatter):
        local, remote = _scatter_copies(cx, src_refs, land_refs, *sems)
        for cp in remote:
            cp.wait()
        for cp in local:
            cp.wait()
        return
    me, sib, chips, c = _gather2_places()
    for a, (src_ref, land) in enumerate(zip(src_refs, land_refs)):
        rows = pl.ds(cx.span[a][0], cx.span[a][1] - cx.span[a][0])
        src = src_ref.at[rows]
        _gather2_copy(land, sems, a, 0, sib, me, rows).wait_recv()
        for j, chip in enumerate(chips):
            _gather2_copy(land, sems, a, 4 + j, (*chip, 1 - c), me, rows).wait_recv()
        _gather2_copy(land, sems, a, 0, me, sib, rows, src=src).wait_send()
        for j, chip in enumerate(chips):
            _gather2_copy(land, sems, a, 1 + j, me, (*chip, c), rows, src=src).wait_send()
            _gather2_copy(land, sems, a, 4 + j, (*chip, c), sib, rows).wait_send()
        pltpu.make_async_copy(src, land.at[_slot(me), rows], sems[2].at[a]).wait()


def exchange(cx, name):
    operands, out_shapes, aliases, sems, n_src, n_xin, n_land = _carry_plumb(cx, 0, 0)

    def body(*refs):
        src_refs, land_refs, sem_refs = refs[:n_src], refs[n_xin:n_xin + n_land], refs[n_xin + n_land:]
        _carry_start(cx, src_refs, land_refs, sem_refs)
        _carry_middle(cx, src_refs, land_refs, sem_refs)
        _carry_finish(cx, src_refs, land_refs, sem_refs)

    return pl.pallas_call(body, in_specs=[_ANY] * n_xin, out_specs=[_ANY] * n_land, out_shape=out_shapes,
                          scratch_shapes=sems, input_output_aliases=aliases, name=name)(*operands)


def small_allreduce(packed, name, xf=None):
    r = packed.shape[0]
    rc = r // N_DEV
    x_ops, x_in_specs, x_out_shapes, x_out_specs, aliases, x_sems, n_xin, n_land = _carried(xf, 1, 1)

    def body(*refs):
        (in_ref,), x_refs, (out_ref,), land_refs, scratch, sem_refs = _cut_refs(refs, 1, n_xin, 1, n_land, 5)
        recv_buf, send1, recv1, send2, recv2 = scratch
        if xf is not None:
            _carry_start(xf, x_refs[:len(xf.srcs)], land_refs, sem_refs)
        me = _my_place()
        mine = _slot(me)
        my_rows = pl.ds(pl.multiple_of(mine * rc, 8), rc)
        first, second = [], []
        for j in range(1, N_DEV):
            peer = _peer(me, j)
            peer_rows = pl.ds(pl.multiple_of(_slot(peer) * rc, 8), rc)
            first.append(pltpu.make_async_remote_copy(
                src_ref=in_ref.at[peer_rows], dst_ref=recv_buf.at[mine], send_sem=send1.at[j - 1], recv_sem=recv1.at[j - 1],
                device_id=peer, device_id_type=MESH))
            second.append(pltpu.make_async_remote_copy(
                src_ref=out_ref.at[my_rows], dst_ref=out_ref.at[my_rows], send_sem=send2.at[j - 1], recv_sem=recv2.at[j - 1],
                device_id=peer, device_id_type=MESH))
        for cp in first:
            cp.start()
        recv_buf[mine] = in_ref[my_rows]
        for cp in first:
            cp.wait()
        acc = recv_buf[0]
        for k in range(1, N_DEV):
            acc = acc + recv_buf[k]
        out_ref[my_rows] = acc
        for cp in second:
            cp.start()
        for cp in second:
            cp.wait()
        if xf is not None:
            _carry_finish(xf, x_refs[:len(xf.srcs)], land_refs, sem_refs)

    vmem = pl.BlockSpec(memory_space=pltpu.VMEM)
    return pl.pallas_call(
        body, in_specs=[vmem] + x_in_specs, out_specs=[vmem] + x_out_specs,
        out_shape=[jax.ShapeDtypeStruct((r, 128), f32)] + x_out_shapes,
        scratch_shapes=[pltpu.VMEM((N_DEV, rc, 128), f32)] + [pltpu.SemaphoreType.DMA((N_DEV - 1,))] * 4 + x_sems,
        input_output_aliases=aliases, name=name, compiler_params=_cp(32))(packed, *x_ops)


def _adamw(w, g, m, v):
    m = ADAM_B1 * m + (1.0 - ADAM_B1) * g
    v = ADAM_B2 * v + (1.0 - ADAM_B2) * (g * g)
    m_hat = m / (1.0 - ADAM_B1 ** ADAM_STEP)
    v_hat = v / (1.0 - ADAM_B2 ** ADAM_STEP)
    delta = -ADAM_LR * (m_hat / (jnp.sqrt(v_hat) + ADAM_EPS) + ADAM_WD * w)
    return delta, m, v


def adam_slots(slots, w, m, v, rows, name):
    _, depth, r, c = slots.shape

    def body(s_ref, w_ref, m_ref, v_ref, g_ref, d_ref, m2_ref, v2_ref):
        g = s_ref[0, 0].astype(f32)
        for k in range(1, N_DEV):
            g = g + s_ref[k, 0].astype(f32)
        d, m2, v2 = _adamw(w_ref[0], g, m_ref[0], v_ref[0])
        g_ref[0] = g
        d_ref[0] = d
        m2_ref[0] = m2
        v2_ref[0] = v2

    blk = pl.BlockSpec((1, rows, c), lambda l, i: (l, i, 0))
    sh = jax.ShapeDtypeStruct((depth, r, c), f32)
    return pl.pallas_call(
        body, grid=(depth, r // rows),
        in_specs=[pl.BlockSpec((N_DEV, 1, rows, c), lambda l, i: (0, l, i, 0)), blk, blk, blk],
        out_specs=[blk] * 4, out_shape=[sh] * 4, name=name, compiler_params=_cp(40))(slots, w, m, v)


def adam_flat(g, w, m, v, name):
    def body(g_ref, w_ref, m_ref, v_ref, d_ref, m2_ref, v2_ref):
        d, m2, v2 = _adamw(w_ref[...], g_ref[...], m_ref[...], v_ref[...])
        d_ref[...] = d
        m2_ref[...] = m2
        v2_ref[...] = v2

    sh = jax.ShapeDtypeStruct(g.shape, f32)
    return pl.pallas_call(body, out_shape=[sh] * 3, name=name)(g, w, m, v)


def _rows_of(shape):
    n = 1
    for dim in shape:
        n *= dim
    return n, -(-n // (8 * 128)) * 8


def _pack(arrs):
    parts = []
    for a in arrs:
        n, rows = _rows_of(a.shape)
        parts.append(jnp.pad(a.reshape(-1).astype(f32), (0, rows * 128 - n)).reshape(rows, 128))
    return jnp.concatenate(parts, axis=0)


def _unpack(packed, shapes):
    out, row = [], 0
    for sh in shapes:
        n, rows = _rows_of(sh)
        out.append(packed[row:row + rows].reshape(-1)[:n].reshape(sh))
        row += rows
    return out


def _block_diag(gw):
    eye = jnp.eye(8, dtype=gw.dtype)
    return (gw[:, :, None, :] * eye[:, None, :, None]).reshape(SEG, SEG)


def _diag_blocks(dense):
    eye = jnp.eye(8, dtype=dense.dtype)
    return (dense.reshape(8, 64, 8, 64) * eye[:, None, :, None]).sum(axis=2)


def _lane_row(vals, first_lane):
    return jnp.zeros((1, HD), f32).at[0, first_lane:first_lane + NH].set(vals)


def kernel(x, norm_w, w_in, rg_conv_w, rg_conv_b, rg_gate_w, rg_gate_b, rg_lambda, ml_gate_b, ml_norm_w, gd_conv_w, gd_a_log, gd_dt_bias, gd_norm_w, w_out, final_norm_w, loss_target, m_norm_w, m_w_in, m_rg_conv_w, m_rg_conv_b, m_rg_gate_w, m_rg_gate_b, m_rg_lambda, m_ml_gate_b, m_ml_norm_w, m_gd_conv_w, m_gd_a_log, m_gd_dt_bias, m_gd_norm_w, m_w_out, m_final_norm_w, v_norm_w, v_w_in, v_rg_conv_w, v_rg_conv_b, v_rg_gate_w, v_rg_gate_b, v_rg_lambda, v_ml_gate_b, v_ml_norm_w, v_gd_conv_w, v_gd_a_log, v_gd_dt_bias, v_gd_norm_w, v_w_out, v_final_norm_w):
    s = x.shape[1]
    xs = x.reshape(s, D)
    tgt = loss_target.reshape(s, D)
    me = 4 * lax.axis_index("x") + 2 * lax.axis_index("y") + lax.axis_index("c")

    comm = MeshComm(w_in, w_out, [rg_conv_w, rg_gate_b, gd_conv_w])
    rg_conv_full, rg_gb_full, gd_conv_full = comm.small_weights
    loss_part, dx, d_fw, g_small = local_step(
        xs, tgt, comm, rg_conv_full, rg_gb_full, gd_conv_full, norm_w, rg_conv_b, rg_gate_w, rg_lambda,
        ml_gate_b, ml_norm_w, gd_a_log, gd_dt_bias, gd_norm_w, final_norm_w)
    given_w = dict(norm_w=norm_w, rg_conv_w=rg_conv_w, rg_conv_b=rg_conv_b, rg_gate_w=rg_gate_w, rg_gate_b=rg_gate_b,
                   rg_lambda=rg_lambda, ml_gate_b=ml_gate_b, ml_norm_w=ml_norm_w, gd_conv_w=gd_conv_w, gd_a_log=gd_a_log,
                   gd_dt_bias=gd_dt_bias, gd_norm_w=gd_norm_w, final_norm_w=final_norm_w, w_in=w_in, w_out=w_out)
    given_m = dict(norm_w=m_norm_w, rg_conv_w=m_rg_conv_w, rg_conv_b=m_rg_conv_b, rg_gate_w=m_rg_gate_w, rg_gate_b=m_rg_gate_b,
                   rg_lambda=m_rg_lambda, ml_gate_b=m_ml_gate_b, ml_norm_w=m_ml_norm_w, gd_conv_w=m_gd_conv_w,
                   gd_a_log=m_gd_a_log, gd_dt_bias=m_gd_dt_bias, gd_norm_w=m_gd_norm_w, final_norm_w=m_final_norm_w,
                   w_in=m_w_in, w_out=m_w_out)
    given_v = dict(norm_w=v_norm_w, rg_conv_w=v_rg_conv_w, rg_conv_b=v_rg_conv_b, rg_gate_w=v_rg_gate_w, rg_gate_b=v_rg_gate_b,
                   rg_lambda=v_rg_lambda, ml_gate_b=v_ml_gate_b, ml_norm_w=v_ml_norm_w, gd_conv_w=v_gd_conv_w,
                   gd_a_log=v_gd_a_log, gd_dt_bias=v_gd_dt_bias, gd_norm_w=v_gd_norm_w, final_norm_w=v_final_norm_w,
                   w_in=v_w_in, w_out=v_w_out)
    return finish_step(loss_part, dx, d_fw, g_small, comm, s, me, given_w, given_m, given_v)


def _gathered_pieces():
    per = D_IN // N_DEV
    pieces = []
    for lo, hi in ((0, 3584), (3592, 5640), (3584, 3592), (5640, 5648)):
        col = lo
        while col < hi:
            k = col // per
            end = min(hi, (k + 1) * per)
            pieces.append((k, col - k * per, end - k * per))
            col = end
    return pieces


def regroup_w_in(wi_g):
    rows = 256
    pieces = _gathered_pieces()

    def body(x_ref, o_ref):
        cols = [x_ref[k, :, a:b] for k, a, b in pieces] + [jnp.zeros((rows, DP - D_IN), wi_g.dtype)]
        o_ref[...] = jnp.concatenate(cols, axis=1)

    return pl.pallas_call(
        body, grid=(D // rows,), in_specs=[pl.BlockSpec((N_DEV, rows, D_IN // N_DEV), lambda i: (0, i, 0))],
        out_specs=pl.BlockSpec((rows, DP), lambda i: (i, 0)), out_shape=jax.ShapeDtypeStruct((D, DP), wi_g.dtype),
        name="regroup_w_in")(wi_g)


class MeshComm:
    GWI_SPLIT = 448
    WI_SPLIT = 384

    def __init__(self, w_in, w_out, small_shards):
        per = D_IN // N_DEV
        self.wi_sh = [w_in[l].astype(bf16) for l in range(DEPTH)]
        self.wo_sh = [w_out[l].astype(bf16) for l in range(DEPTH)]
        self.wi_land = jax.ShapeDtypeStruct((N_DEV, D, per), bf16)
        self.wo_land = jax.ShapeDtypeStruct((N_DEV, 3 * SEG // N_DEV, D), bf16)
        packed = _pack(small_shards)
        first = Gather2([self.wi_sh[0], self.wo_sh[0], packed],
                        [self.wi_land, self.wo_land, jax.ShapeDtypeStruct((N_DEV,) + packed.shape, f32)],
                        [(0, D), (0, 3 * SEG // N_DEV), (0, packed.shape[0])])
        wi_g, wo_g, sm_g = exchange(first, "gather_first")
        self.wi_g, self.wo_g = {0: wi_g}, {0: wo_g}
        shapes = [a.shape for a in small_shards]
        parts = [_unpack(sm_g[k], shapes) for k in range(N_DEV)]
        self.small_weights = [jnp.concatenate([p[j] for p in parts], axis=-1) for j in range(len(small_shards))]
        self.gwi_land = lax.empty((N_DEV, DEPTH, D, per), bf16)
        self.gwo_land = lax.empty((N_DEV, DEPTH, 3 * SEG // N_DEV, D), bf16)
        self.gwi_slots = {}
        self.gwo_slots = {}

    def weights(self, l):
        return regroup_w_in(self.wi_g[l]), self.wo_g[l].reshape(3 * SEG, D)

    def fwd_carry(self, l, host):
        if l + 1 >= DEPTH:
            return None
        if host == "mlstm":
            return Gather2([self.wo_sh[l + 1]], [self.wo_land], [(0, 3 * SEG // N_DEV)])
        if host == "inproj":
            return Gather2([self.wi_sh[l + 1]], [self.wi_land], [(0, self.WI_SPLIT)])
        return Gather2([self.wi_sh[l + 1]], [self.wi_g[l + 1]], [(self.WI_SPLIT, D)])

    def fwd_landed(self, l, host, landed):
        (self.wo_g if host == "mlstm" else self.wi_g)[l + 1] = landed[0]

    def own_w_out_grad(self, l, g_wo):
        self.gwo_slots[l] = g_wo.reshape(N_DEV, 3 * SEG // N_DEV, D).astype(bf16)

    def bwd_carry(self, l, host):
        rows_o = 3 * SEG // N_DEV
        srcs, land_of, layer, span = [], [], [], []
        if l + 1 < DEPTH:
            if host == "rglru":
                srcs, land_of, layer, span = [self.gwo_slots[l + 1]], [1], [l + 1], [(0, rows_o)]
            elif host == "mlstm":
                srcs, land_of, layer, span = [self.gwi_slots[l + 1]], [0], [l + 1], [(0, self.GWI_SPLIT)]
            else:
                srcs, land_of, layer, span = [self.gwi_slots[l + 1]], [0], [l + 1], [(self.GWI_SPLIT, D)]
        if l == 0 and host == "mlstm":
            srcs, land_of, layer, span = srcs + [self.gwo_slots[0]], land_of + [1], layer + [0], span + [(0, rows_o)]
        if not srcs:
            return None
        return Scatter(srcs, [self.gwi_land, self.gwo_land], land_of, layer, span)

    def bwd_landed(self, landed):
        self.gwi_land, self.gwo_land = landed

    def grads_ready(self, l, pieces):
        per = D_IN // N_DEV
        g_wi = jnp.concatenate(pieces, axis=1)
        self.gwi_slots[l] = jnp.stack([g_wi[:, k * per:(k + 1) * per] for k in range(N_DEV)]).astype(bf16)

    def last_carry(self):
        return Scatter([self.gwi_slots[0]], [self.gwi_land, self.gwo_land], [0], [0], [(0, D)])


def local_step(xs, tgt, comm, rg_conv_full, rg_gb_full, gd_conv_full, norm_w, rg_conv_b, rg_gate_w,
               rg_lambda, ml_gate_b, ml_norm_w, gd_a_log, gd_dt_bias, gd_norm_w, final_norm_w):
    acts = []
    for l in range(DEPTH):
        nw = norm_w[l].reshape(1, D)
        w_in_l, w_out_l = comm.weights(l)
        xf = comm.fwd_carry(l, "inproj")
        proj, hn_t, *landed = inproj(xs, nw, w_in_l, xf=xf)
        if xf is not None:
            comm.fwd_landed(l, "inproj", landed)
        rg_p = (rg_conv_full[l], rg_conv_b[l].reshape(1, SEG), _block_diag(rg_gate_w[l, 0]), _block_diag(rg_gate_w[l, 1]),
                rg_gb_full[l], rg_lambda[l].reshape(1, SEG))
        y_rg, hs = rglru_fwd(proj, *rg_p)
        ml_p = (jnp.zeros((1, HD), f32).at[0, 0:2 * NH].set(ml_gate_b[l].reshape(-1)), ml_norm_w[l].reshape(1, SEG))
        xf = comm.fwd_carry(l, "mlstm")
        y_ml, cs, ns, ms, *landed = mlstm_fwd(proj, *ml_p, xf=xf)
        if xf is not None:
            comm.fwd_landed(l, "mlstm", landed)
        gd_p = (gd_conv_full[l], _lane_row(gd_a_log[l], 2 * NH), _lane_row(gd_dt_bias[l], 2 * NH), gd_norm_w[l].reshape(1, HD))
        xf = comm.fwd_carry(l, "gdn")
        y_gd, ss, t_invs, *landed = gdn_fwd(proj, *gd_p, xf=xf)
        if xf is not None:
            comm.fwd_landed(l, "gdn", landed)
        acts.append((xs, nw, proj, hn_t, w_in_l, w_out_l, rg_p, y_rg, hs, ml_p, y_ml, cs, ns, ms, gd_p, y_gd, ss, t_invs))
        xs = outproj(xs, y_rg, y_ml, y_gd, w_out_l)

    dx, loss_part, d_fw = head(xs, final_norm_w.reshape(1, D), tgt)

    g_small = {k: [None] * DEPTH for k in ("norm_w", "rg_conv_w", "rg_conv_b", "rg_gate_w", "rg_gate_b", "rg_lambda",
                                           "ml_gate_b", "ml_norm_w", "gd_conv_w", "gd_a_log", "gd_dt_bias", "gd_norm_w")}
    for l in reversed(range(DEPTH)):
        x_l, nw, proj, hn_t, w_in_l, w_out_l, rg_p, y_rg, hs, ml_p, y_ml, cs, ns, ms, gd_p, y_gd, ss, t_invs = acts[l]
        dy_rg, dy_ml, dy_gd, g_wo = outproj_bwd(dx, y_rg, y_ml, y_gd, w_out_l)
        comm.own_w_out_grad(l, g_wo)
        xf = comm.bwd_carry(l, "rglru")
        d_rg, d_cw, d_cb, d_gr, d_gi, d_gb, d_lam, *landed = rglru_bwd(proj, hs, dy_rg, *rg_p, xf=xf)
        if xf is not None:
            comm.bwd_landed(landed)
        xf = comm.bwd_carry(l, "mlstm")
        d_ml, d_sm_ml, d_bias, d_mnw, *landed = mlstm_bwd(proj, dy_ml, cs, ns, ms, *ml_p, xf=xf)
        if xf is not None:
            comm.bwd_landed(landed)
        xf = comm.bwd_carry(l, "gdn")
        d_gd, d_sm_gd, d_gcw, d_al, d_dt, d_gnw, *landed = gdn_bwd(proj, dy_gd, ss, t_invs, *gd_p, xf=xf)
        if xf is not None:
            comm.bwd_landed(landed)
        dx, d_nw = inproj_bwd_x(x_l, nw, w_in_l, dx, d_rg, d_ml, d_gd, d_sm_ml, d_sm_gd)
        gw_rg = wgrad(hn_t, [d_rg], "wgrad_rg")
        gw_ml = wgrad(hn_t, [d_ml], "wgrad_ml")
        gw_gd = wgrad(hn_t, [d_gd], "wgrad_gd")
        gw_sm = wgrad(hn_t, [d_sm_ml, d_sm_gd], "wgrad_small")
        comm.grads_ready(l, [gw_rg, gw_ml, gw_sm[:, 0:2 * NH], gw_gd, gw_sm[:, 2 * NH:4 * NH]])
        g_small["norm_w"][l] = d_nw[0]
        g_small["rg_conv_w"][l] = d_cw
        g_small["rg_conv_b"][l] = d_cb[0]
        g_small["rg_gate_w"][l] = jnp.stack([_diag_blocks(d_gr), _diag_blocks(d_gi)])
        g_small["rg_gate_b"][l] = d_gb
        g_small["rg_lambda"][l] = d_lam[0]
        g_small["ml_gate_b"][l] = d_bias[0, 0:2 * NH].reshape(2, NH)
        g_small["ml_norm_w"][l] = d_mnw[0]
        g_small["gd_conv_w"][l] = d_gcw
        g_small["gd_a_log"][l] = d_al[0, 2 * NH:3 * NH]
        g_small["gd_dt_bias"][l] = d_dt[0, 2 * NH:3 * NH]
        g_small["gd_norm_w"][l] = d_gnw[0]
    return loss_part, dx, d_fw, g_small


def finish_step(loss_part, dx, d_fw, g_small, comm, s, me, given_w, given_m, given_v):
    small_names = ["norm_w", "rg_conv_w", "rg_conv_b", "rg_gate_w", "rg_gate_b", "rg_lambda", "ml_gate_b", "ml_norm_w",
                   "gd_conv_w", "gd_a_log", "gd_dt_bias", "gd_norm_w"]
    small_list = [loss_part[0, 0:1], d_fw[0]] + [jnp.stack(g_small[k]) for k in small_names]
    small_shapes = [a.shape for a in small_list]
    packed = _pack(small_list)
    packed = jnp.pad(packed, ((0, -packed.shape[0] % (8 * N_DEV)), (0, 0)))
    summed, gwi_r, gwo_r = small_allreduce(packed, "last_exchange", xf=comm.last_carry())
    g_all = _unpack(summed, small_shapes)

    g_wi, d_wi, m_wi, v_wi = adam_slots(gwi_r, given_w["w_in"], given_m["w_in"], given_v["w_in"], 256, "adam_w_in")
    g_wo, d_wo, m_wo, v_wo = adam_slots(gwo_r, given_w["w_out"], given_m["w_out"], given_v["w_out"], 192, "adam_w_out")
    loss = g_all[0][0]
    grads = {"final_norm_w": g_all[1]}
    for k, g in zip(small_names, g_all[2:]):
        grads[k] = g
    for k, width in (("rg_conv_w", 64), ("rg_gate_b", 64), ("gd_conv_w", 192)):
        grads[k] = lax.dynamic_slice_in_dim(grads[k], me * width, width, axis=2)
    names = small_names + ["final_norm_w"]
    shapes = [given_w[k].shape for k in names]
    d_p, m_p, v_p = adam_flat(_pack([grads[k] for k in names]), _pack([given_w[k] for k in names]),
                              _pack([given_m[k] for k in names]), _pack([given_v[k] for k in names]), "adam_small")
    deltas = dict(zip(names, _unpack(d_p, shapes)))
    new_m = dict(zip(names, _unpack(m_p, shapes)))
    new_v = dict(zip(names, _unpack(v_p, shapes)))
    grads["w_in"], deltas["w_in"], new_m["w_in"], new_v["w_in"] = g_wi, d_wi, m_wi, v_wi
    grads["w_out"], deltas["w_out"], new_m["w_out"], new_v["w_out"] = g_wo, d_wo, m_wo, v_wo

    order = ["norm_w", "w_in", "rg_conv_w", "rg_conv_b", "rg_gate_w", "rg_gate_b", "rg_lambda", "ml_gate_b", "ml_norm_w",
             "gd_conv_w", "gd_a_log", "gd_dt_bias", "gd_norm_w", "w_out", "final_norm_w"]
    return (loss, dx.reshape(1, s, D), *[grads[k] for k in order], *[deltas[k] for k in order],
            *[new_m[k] for k in order], *[new_v[k] for k in order])
```

```python
import functools
from typing import NamedTuple

import jax
import jax.numpy as jnp
from jax import lax
from jax.experimental import pallas as pl
from jax.experimental.pallas import tpu as pltpu

f32 = jnp.float32
bf16 = jnp.bfloat16
MESH = pl.DeviceIdType.MESH

N_DEV = 8
D = 1024
DEPTH = 4
EPS = 1e-6
SEG = 512
HD = 128
NH = 4
LC = 64
RUN = 4
LR = RUN * LC
D_IN = 5648
DP = 5760
SMALL0 = 5632
RG_TILE = 256
RG_C = 8.0

ADAM_LR = 0.001
ADAM_B1 = 0.9
ADAM_B2 = 0.999
ADAM_EPS = 1e-08
ADAM_WD = 0.01
ADAM_STEP = 10


def _cp(vmem_mb):
    return pltpu.CompilerParams(vmem_limit_bytes=vmem_mb * 2 ** 20)


def _dot(a, b, ca, cb):
    return lax.dot_general(a.astype(bf16), b.astype(bf16), (((ca,), (cb,)), ((), ())), preferred_element_type=f32)


@jax.custom_vjp
def mm_nn(a, b):
    return _dot(a, b, 1, 0)


@jax.custom_vjp
def mm_nt(a, b):
    return _dot(a, b, 1, 1)


@jax.custom_vjp
def mm_tn(a, b):
    return _dot(a, b, 0, 0)


mm_nn.defvjp(lambda a, b: (mm_nn(a, b), (a, b)), lambda r, g: (mm_nt(g, r[1]), mm_tn(r[0], g)))
mm_nt.defvjp(lambda a, b: (mm_nt(a, b), (a, b)), lambda r, g: (mm_nn(g, r[1]), mm_tn(g, r[0])))
mm_tn.defvjp(lambda a, b: (mm_tn(a, b), (a, b)), lambda r, g: (mm_nt(r[1], g), mm_nn(r[0], g)))


def _split(x):
    hi = x.astype(bf16)
    return hi, (x - hi.astype(f32)).astype(bf16)


def dot3(a, b):
    ah, al = _split(a)
    bh, bl = _split(b)
    d = functools.partial(jnp.dot, preferred_element_type=f32)
    return d(ah, bh) + (d(al, bh) + d(ah, bl))


def _tri_sum(x, reverse):
    n = x.shape[0]
    r = lax.broadcasted_iota(jnp.int32, (n, 3 * n), 0)
    c = lax.broadcasted_iota(jnp.int32, (n, 3 * n), 1) & (n - 1)
    ones = ((c >= r) if reverse else (c <= r)).astype(bf16)
    hi = x.astype(bf16)
    rest = x - hi.astype(f32)
    mid = rest.astype(bf16)
    lo = (rest - mid.astype(f32)).astype(bf16)
    return jnp.dot(ones, jnp.concatenate([hi, mid, lo], axis=0), preferred_element_type=f32)


@jax.custom_vjp
def cumsum_rows(x):
    return _tri_sum(x, False)


@jax.custom_vjp
def rev_cumsum_rows(x):
    return _tri_sum(x, True)


cumsum_rows.defvjp(lambda x: (cumsum_rows(x), None), lambda _, g: (rev_cumsum_rows(g),))
rev_cumsum_rows.defvjp(lambda x: (rev_cumsum_rows(x), None), lambda _, g: (cumsum_rows(g),))


def _tri(n, strict=False):
    r = lax.broadcasted_iota(jnp.int32, (n, n), 0)
    c = lax.broadcasted_iota(jnp.int32, (n, n), 1)
    return (r > c) if strict else (r >= c)


def _lane_col(v, j):
    lane = lax.broadcasted_iota(jnp.int32, v.shape, 1)
    return jnp.sum(jnp.where(lane == j, v, 0.0), axis=1, keepdims=True)


def _rows_from(pieces, rows, width):
    ridx = lax.broadcasted_iota(jnp.int32, (rows, width), 0)
    out = jnp.zeros((rows, width), f32)
    for h, p in enumerate(pieces):
        out = out + jnp.where(ridx == h, jnp.broadcast_to(p, (rows, width)), 0.0)
    return out


def causal_conv(halo8, x, w4):
    t = x.shape[0]
    xe = jnp.concatenate([halo8, x], axis=0)
    y = xe[5:5 + t] * w4[0:1]
    for k in range(1, 4):
        y = y + xe[5 + k:5 + k + t] * w4[k:k + 1]
    return y


def ml_chunk(q, k, v, o_pre, z, small, bias_row, norm_w, C, n, m):
    n_ch = q.shape[0] // LC
    lane = lax.broadcasted_iota(jnp.int32, small.shape, 1)
    pre = small + bias_row
    lg = jnp.where(lane < NH, pre, jnp.where(lane < 2 * NH, jax.nn.log_sigmoid(pre), 0.0))
    rows = [slice(c * LC, (c + 1) * LC) for c in range(n_ch)]
    lgs = [lg[r] for r in rows]
    bcs = [cumsum_rows(x) for x in lgs]
    lg_ts = [x.T for x in lgs]
    bc_ts = [x.T for x in bcs]
    causal = _tri(LC)
    pairs = [(c, h) for c in range(n_ch) for h in range(NH)]
    idx = range(len(pairs))
    sls = [slice(h * HD, (h + 1) * HD) for h in range(NH)]
    qs = [q[rows[c], sls[h]] * (HD ** -0.5) for c, h in pairs]
    ks = [k[rows[c], sls[h]] for c, h in pairs]
    vs = [v[rows[c], sls[h]] for c, h in pairs]
    li_cols = [_lane_col(lgs[c], h) for c, h in pairs]
    b_cols = [_lane_col(bcs[c], NH + h) for c, h in pairs]
    dms = [jnp.where(causal, b_cols[i] - bc_ts[c][NH + h:NH + h + 1, :] + lg_ts[c][h:h + 1, :], -jnp.inf)
           for i, (c, h) in enumerate(pairs)]
    dm_maxs = [jnp.max(dms[i], axis=-1, keepdims=True) for i in idx]
    gs = [b_cols[i][LC - 1:LC, :] for i in idx]
    wss = [gs[i] - b_cols[i] + li_cols[i] for i in idx]
    ws_maxs = [jnp.max(wss[i], axis=0, keepdims=True) for i in idx]
    qks = [mm_nt(qs[i], ks[i]) for i in idx]
    m_in = [None] * len(pairs)
    m_out = [None] * len(pairs)
    for h in range(NH):
        cur = m[h:h + 1, 0:1]
        for c in range(n_ch):
            i = c * NH + h
            m_in[i] = cur
            cur = jnp.maximum(gs[i] + cur, ws_maxs[i])
            m_out[i] = cur
    m_inters = [b_cols[i] + m_in[i] for i in idx]
    m_ts = [jnp.maximum(m_inters[i], dm_maxs[i]) for i in idx]
    ss = [qks[i] * jnp.exp(dms[i] - m_ts[i]) for i in idx]
    scs = [jnp.exp(m_inters[i] - m_ts[i]) for i in idx]
    decs = [jnp.exp(gs[i] + m_in[i] - m_out[i]) for i in idx]
    kws = [jnp.exp(wss[i] - m_out[i]) * ks[i] for i in idx]
    c_adds = [mm_tn(kws[i], vs[i]) for i in idx]
    n_adds = [jnp.sum(kws[i], axis=0, keepdims=True) for i in idx]
    svs = [mm_nn(ss[i], vs[i]) for i in idx]
    s_sums = [jnp.sum(ss[i], axis=-1, keepdims=True) for i in idx]
    c_hs = [C[sl, :] for sl in sls]
    n_hs = [n[h:h + 1, :] for h in range(NH)]
    hhs = [None] * len(pairs)
    for c in range(n_ch):
        for h in range(NH):
            i = c * NH + h
            num = svs[i] + scs[i] * mm_nn(qs[i], c_hs[h])
            den = s_sums[i] + scs[i] * jnp.sum(qs[i] * n_hs[h], axis=-1, keepdims=True)
            hhs[i] = num / jnp.maximum(jnp.abs(den), jnp.exp(-m_ts[i]))
        c_hs = [decs[c * NH + h] * c_hs[h] + c_adds[c * NH + h] for h in range(NH)]
        n_hs = [decs[c * NH + h] * n_hs[h] + n_adds[c * NH + h] for h in range(NH)]
    ys = [hhs[i] * lax.rsqrt(jnp.mean(hhs[i] * hhs[i], axis=-1, keepdims=True) + EPS) * norm_w[:, sls[h]]
          * jax.nn.sigmoid(o_pre[rows[c], sls[h]]) * jax.nn.silu(z[rows[c], sls[h]]) for i, (c, h) in enumerate(pairs)]
    y = jnp.concatenate([jnp.concatenate(ys[c * NH:(c + 1) * NH], axis=1) for c in range(n_ch)], axis=0)
    last = (n_ch - 1) * NH
    m_rows = [jnp.broadcast_to(m_out[last + h], (1, HD)) for h in range(NH)]
    return y, jnp.concatenate(c_hs, axis=0), _rows_from(n_hs, 8, HD), _rows_from(m_rows, 8, HD)


@jax.custom_vjp
def _unit_lower_inverses(mats):
    eye = (lax.broadcasted_iota(jnp.int32, (LC, LC), 0) == lax.broadcasted_iota(jnp.int32, (LC, LC), 1)).astype(f32)
    ps = [-m for m in mats]
    ts = [eye + p for p in ps]
    for _ in range(5):
        ps = [dot3(p, p) for p in ps]
        ts = [t + dot3(t, p) for t, p in zip(ts, ps)]
    return tuple(ts)


def _unit_lower_inverses_fwd(mats):
    ts = _unit_lower_inverses(mats)
    return ts, ts


def _unit_lower_inverses_bwd(ts, gs):
    tts = [t.T for t in ts]
    mid = [dot3(tt, g) for tt, g in zip(tts, gs)]
    return (tuple(-dot3(m, tt) for m, tt in zip(mid, tts)),)


_unit_lower_inverses.defvjp(_unit_lower_inverses_fwd, _unit_lower_inverses_bwd)


@jax.custom_vjp
def _known_inverses(mats, ts):
    return tuple(ts)


_known_inverses.defvjp(lambda mats, ts: (tuple(ts), tuple(ts)),
                       lambda ts, gs: (_unit_lower_inverses_bwd(ts, gs)[0], tuple(jnp.zeros_like(t) for t in ts)))


def gd_chunk(qh8, q, kh8, k, vh8, v, z, small, conv_w, alog_row, dt_row, norm_w, st, known_t=None):
    n_ch = q.shape[0] // LC
    lane = lax.broadcasted_iota(jnp.int32, small.shape, 1)
    is_a = (lane >= 2 * NH) & (lane < 3 * NH)
    g_all = jnp.where(is_a, -jnp.exp(alog_row) * jax.nn.softplus(small + dt_row), 0.0)
    beta_all = jax.nn.sigmoid(small)
    qc = jax.nn.silu(causal_conv(qh8, q, conv_w[:, 0:SEG]))
    kc = jax.nn.silu(causal_conv(kh8, k, conv_w[:, SEG:2 * SEG]))
    vc = jax.nn.silu(causal_conv(vh8, v, conv_w[:, 2 * SEG:3 * SEG]))
    incl = _tri(LC)
    strict = _tri(LC, strict=True)
    rows = [slice(c * LC, (c + 1) * LC) for c in range(n_ch)]
    gc_alls = [cumsum_rows(g_all[r]) for r in rows]
    gc_ts = [g.T for g in gc_alls]
    pairs = [(c, h) for c in range(n_ch) for h in range(NH)]
    idx = range(len(pairs))
    sls = [slice(h * HD, (h + 1) * HD) for h in range(NH)]
    qs = [qc[rows[c], sls[h]] for c, h in pairs]
    ks = [kc[rows[c], sls[h]] for c, h in pairs]
    vs = [vc[rows[c], sls[h]] for c, h in pairs]
    qs = [x * lax.rsqrt(jnp.sum(x * x, axis=-1, keepdims=True) + EPS) * (HD ** -0.5) for x in qs]
    ks = [x * lax.rsqrt(jnp.sum(x * x, axis=-1, keepdims=True) + EPS) for x in ks]
    betas = [_lane_col(beta_all[rows[c]], 3 * NH + h) for c, h in pairs]
    gcs = [_lane_col(gc_alls[c], 2 * NH + h) for c, h in pairs]
    gams = [jnp.exp(jnp.where(incl, gcs[i] - gc_ts[c][2 * NH + h:2 * NH + h + 1, :], -jnp.inf))
            for i, (c, h) in enumerate(pairs)]
    kbs = [ks[i] * betas[i] for i in idx]
    mats = tuple(jnp.where(strict, mm_nt(kbs[i], ks[i]) * gams[i], 0.0) for i in idx)
    aqks = [mm_nt(qs[i], ks[i]) * gams[i] for i in idx]
    if known_t is None:
        t_invs = _unit_lower_inverses(mats)
    else:
        t_invs = _known_inverses(mats, tuple(known_t[i] for i in idx))
    us = [mm_nn(t_invs[i], vs[i] * betas[i]) for i in idx]
    ws = [mm_nn(t_invs[i], kbs[i] * jnp.exp(gcs[i])) for i in idx]
    g_lasts = [gcs[i][LC - 1:LC, :] for i in idx]
    q_decs = [qs[i] * jnp.exp(gcs[i]) for i in idx]
    k_decs = [ks[i] * jnp.exp(g_lasts[i] - gcs[i]) for i in idx]
    e_lasts = [jnp.exp(g_lasts[i]) for i in idx]
    s_hs = [st[sl, :] for sl in sls]
    os_ = [None] * len(pairs)
    for c in range(n_ch):
        ids = [c * NH + h for h in range(NH)]
        v_news = [us[i] - mm_nn(ws[i], s_hs[h]) for h, i in enumerate(ids)]
        for h, i in enumerate(ids):
            os_[i] = mm_nn(q_decs[i], s_hs[h]) + mm_nn(aqks[i], v_news[h])
        s_hs = [s_hs[h] * e_lasts[i] + mm_tn(k_decs[i], v_news[h]) for h, i in enumerate(ids)]
    ys = [os_[i] * lax.rsqrt(jnp.mean(os_[i] * os_[i], axis=-1, keepdims=True) + EPS) * norm_w
          * jax.nn.silu(z[rows[c], sls[h]]) for i, (c, h) in enumerate(pairs)]
    y = jnp.concatenate([jnp.concatenate(ys[c * NH:(c + 1) * NH], axis=1) for c in range(n_ch)], axis=0)
    if known_t is None:
        return y, jnp.concatenate(s_hs, axis=0), jnp.stack(t_invs)
    return y, jnp.concatenate(s_hs, axis=0)


def rg_pre(xh8, x, conv_w, conv_b, gw_r, gw_i, gate_b, lam):
    xc = causal_conv(xh8, x, conv_w) + conv_b
    r = jax.nn.sigmoid(mm_nn(xc, gw_r) + gate_b[0:1])
    i = jax.nn.sigmoid(mm_nn(xc, gw_i) + gate_b[1:2])
    log_a = -RG_C * r * jax.nn.softplus(-lam)
    a = jnp.exp(log_a)
    th = jnp.tanh(log_a)
    one_minus_a2 = -2.0 * th / (1.0 - th)
    b = jnp.sqrt(one_minus_a2) * (i * xc)
    return a, b


def _scan_rows(a_ref, b_ref, out_ref, h0, n_rows, reverse):
    n_groups = n_rows // 8
    width = a_ref.shape[1]
    row = lax.broadcasted_iota(jnp.int32, (8, width), 0)

    def body(j, h):
        g = (n_groups - 1 - j) if reverse else j
        r0 = pl.multiple_of(g * 8, 8)
        av = a_ref[pl.ds(r0, 8), :]
        bv = b_ref[pl.ds(r0, 8), :]
        for d in (1, 2, 4):
            sh = (8 - d) if reverse else d
            a_s = pltpu.roll(av, sh, 0)
            b_s = pltpu.roll(bv, sh, 0)
            valid = (row < 8 - d) if reverse else (row >= d)
            bv = jnp.where(valid, av * b_s + bv, bv)
            av = jnp.where(valid, av * a_s, av)
        hv = av * h + bv
        out_ref[pl.ds(r0, 8), :] = hv
        return hv[0:1, :] if reverse else hv[7:8, :]

    return lax.fori_loop(0, n_groups, body, h0)


def _norm_rows(xv):
    r = lax.rsqrt(jnp.mean(xv * xv, axis=-1, keepdims=True) + EPS)
    return xv * r, r


def inproj(x, nw, w, xf=None):
    s, ts = x.shape[0], 256
    nt = s // ts
    n_in, n_out = 3, 2
    x_ops, x_in_specs, x_out_shapes, x_out_specs, aliases, x_sems, n_xin, n_land = _carried(xf, n_in, n_out)

    def body(*refs):
        (x_ref, nw_ref, w_ref), x_refs, (o_ref, ht_ref), land_refs, _, sem_refs = _cut_refs(refs, n_in, n_xin, n_out, n_land, 0)
        i = pl.program_id(0)
        _carry_open(xf, i, nt, x_refs, land_refs, sem_refs)
        xn, _ = _norm_rows(x_ref[...])
        hn = xn * nw_ref[...]
        o_ref[...] = jnp.dot(hn.astype(bf16), w_ref[...], preferred_element_type=f32)
        ht_ref[...] = hn.T.astype(bf16)
        _carry_close(xf, i, nt, x_refs, land_refs, sem_refs)

    return pl.pallas_call(
        body, grid=(nt,),
        in_specs=[pl.BlockSpec((ts, D), lambda i: (i, 0)), pl.BlockSpec((1, D), lambda i: (0, 0)),
                  pl.BlockSpec((D, DP), lambda i: (0, 0))] + x_in_specs,
        out_specs=[pl.BlockSpec((ts, DP), lambda i: (i, 0)), pl.BlockSpec((D, ts), lambda i: (0, i))] + x_out_specs,
        out_shape=[jax.ShapeDtypeStruct((s, DP), f32), jax.ShapeDtypeStruct((D, s), bf16)] + x_out_shapes,
        scratch_shapes=x_sems, input_output_aliases=aliases,
        name="inproj" if xf is None else "inproj_carrying", compiler_params=_cp(56))(x, nw, w, *x_ops)


def outproj(x, yr, ym, yg, wo):
    s, ts = x.shape[0], 256

    def body(x_ref, yr_ref, ym_ref, yg_ref, wo_ref, o_ref):
        acc = x_ref[...]
        for j, y_ref in enumerate((yr_ref, ym_ref, yg_ref)):
            acc = acc + jnp.dot(y_ref[...].astype(bf16), wo_ref[j * SEG:(j + 1) * SEG, :], preferred_element_type=f32)
        o_ref[...] = acc

    yspec = pl.BlockSpec((ts, SEG), lambda i: (i, 0))
    return pl.pallas_call(
        body, grid=(s // ts,),
        in_specs=[pl.BlockSpec((ts, D), lambda i: (i, 0)), yspec, yspec, yspec,
                  pl.BlockSpec((3 * SEG, D), lambda i: (0, 0))],
        out_specs=pl.BlockSpec((ts, D), lambda i: (i, 0)),
        out_shape=jax.ShapeDtypeStruct((s, D), f32), name="outproj", compiler_params=_cp(40))(x, yr, ym, yg, wo)


def head(x, fw, tgt):
    s, ts = x.shape[0], 256

    def body(x_ref, fw_ref, t_ref, dx_ref, loss_ref, dfw_ref):
        @pl.when(pl.program_id(0) == 0)
        def _():
            loss_ref[...] = jnp.zeros_like(loss_ref)
            dfw_ref[...] = jnp.zeros_like(dfw_ref)

        xn, r = _norm_rows(x_ref[...])
        fw_v = fw_ref[...]
        err = xn * fw_v - t_ref[...]
        loss_ref[...] += 0.5 * jnp.sum(jnp.mean(err * err, axis=-1, keepdims=True))
        dy = err * (1.0 / D)
        dfw_ref[0:1, :] += jnp.sum(dy * xn, axis=0, keepdims=True)
        dxn = dy * fw_v
        dx_ref[...] = r * (dxn - xn * jnp.mean(dxn * xn, axis=-1, keepdims=True))

    tile = pl.BlockSpec((ts, D), lambda i: (i, 0))
    return pl.pallas_call(
        body, grid=(s // ts,),
        in_specs=[tile, pl.BlockSpec((1, D), lambda i: (0, 0)), tile],
        out_specs=[tile, pl.BlockSpec((8, 128), lambda i: (0, 0)), pl.BlockSpec((8, D), lambda i: (0, 0))],
        out_shape=[jax.ShapeDtypeStruct((s, D), f32), jax.ShapeDtypeStruct((8, 128), f32),
                   jax.ShapeDtypeStruct((8, D), f32)], name="head")(x, fw, tgt)


def outproj_bwd(dx, yr, ym, yg, wo):
    s, ts = dx.shape[0], 256

    def body(dx_ref, yr_ref, ym_ref, yg_ref, wo_ref, dyr_ref, dym_ref, dyg_ref, gwo_ref):
        @pl.when(pl.program_id(0) == 0)
        def _():
            gwo_ref[...] = jnp.zeros_like(gwo_ref)

        dxb = dx_ref[...].astype(bf16)
        for j, (y_ref, dy_ref) in enumerate(((yr_ref, dyr_ref), (ym_ref, dym_ref), (yg_ref, dyg_ref))):
            rows = slice(j * SEG, (j + 1) * SEG)
            dy_ref[...] = lax.dot_general(dxb, wo_ref[rows, :], (((1,), (1,)), ((), ())), preferred_element_type=f32)
            gwo_ref[rows, :] += lax.dot_general(y_ref[...].astype(bf16), dxb, (((0,), (0,)), ((), ())),
                                                preferred_element_type=f32)

    yspec = pl.BlockSpec((ts, SEG), lambda i: (i, 0))
    wspec = pl.BlockSpec((3 * SEG, D), lambda i: (0, 0))
    ysh = jax.ShapeDtypeStruct((s, SEG), f32)
    return pl.pallas_call(
        body, grid=(s // ts,),
        in_specs=[pl.BlockSpec((ts, D), lambda i: (i, 0)), yspec, yspec, yspec, wspec],
        out_specs=[yspec, yspec, yspec, wspec],
        out_shape=[ysh, ysh, ysh, jax.ShapeDtypeStruct((3 * SEG, D), f32)],
        name="outproj_bwd", compiler_params=_cp(48))(dx, yr, ym, yg, wo)


def inproj_bwd_x(x, nw, w, dxo, d_rg, d_ml, d_gd, d_sa, d_sb):
    s, ts = x.shape[0], 256
    widths = (d_rg.shape[1], d_ml.shape[1], d_gd.shape[1], HD)

    def body(x_ref, nw_ref, w_ref, dxo_ref, rg_ref, ml_ref, gd_ref, sa_ref, sb_ref, dx_ref, dnw_ref):
        @pl.when(pl.program_id(0) == 0)
        def _():
            dnw_ref[...] = jnp.zeros_like(dnw_ref)

        xn, r = _norm_rows(x_ref[...])
        pieces = (rg_ref[...], ml_ref[...], gd_ref[...], sa_ref[...] + sb_ref[...])
        dhn = jnp.zeros((ts, D), f32)
        c0 = 0
        for piece, wd in zip(pieces, widths):
            dhn = dhn + lax.dot_general(piece.astype(bf16), w_ref[:, c0:c0 + wd], (((1,), (1,)), ((), ())),
                                        preferred_element_type=f32)
            c0 += wd
        dnw_ref[0:1, :] += jnp.sum(dhn * xn, axis=0, keepdims=True)
        dxn = dhn * nw_ref[...]
        dx_ref[...] = dxo_ref[...] + r * (dxn - xn * jnp.mean(dxn * xn, axis=-1, keepdims=True))

    tile = pl.BlockSpec((ts, D), lambda i: (i, 0))
    return pl.pallas_call(
        body, grid=(s // ts,),
        in_specs=[tile, pl.BlockSpec((1, D), lambda i: (0, 0)), pl.BlockSpec((D, DP), lambda i: (0, 0)), tile]
        + [pl.BlockSpec((ts, wd), lambda i: (i, 0)) for wd in widths] + [pl.BlockSpec((ts, HD), lambda i: (i, 0))],
        out_specs=[tile, pl.BlockSpec((8, D), lambda i: (0, 0))],
        out_shape=[jax.ShapeDtypeStruct((s, D), f32), jax.ShapeDtypeStruct((8, D), f32)],
        name="inproj_bwd_x", compiler_params=_cp(56))(x, nw, w, dxo, d_rg, d_ml, d_gd, d_sa, d_sb)


def wgrad(hn_t, dps, name):
    s = hn_t.shape[1]
    c = dps[0].shape[1]
    ct = min(c, SEG)
    n_dp = len(dps)

    def body(*refs):
        ht_ref = refs[0]
        dp_refs = refs[1:1 + n_dp]
        o_ref = refs[1 + n_dp]
        dp = dp_refs[0][...]
        for extra in dp_refs[1:]:
            dp = dp + extra[...]
        o_ref[...] = jnp.dot(ht_ref[...], dp.astype(bf16), preferred_element_type=f32).astype(bf16)

    return pl.pallas_call(
        body, grid=(c // ct,),
        in_specs=[pl.BlockSpec((D, s), lambda j: (0, 0))] + [pl.BlockSpec((s, ct), lambda j: (0, j)) for _ in dps],
        out_specs=pl.BlockSpec((D, ct), lambda j: (0, j)),
        out_shape=jax.ShapeDtypeStruct((D, c), bf16), name=name, compiler_params=_cp(40))(hn_t, *dps)


def _seg_spec(rows, seg, n_tiles=None):
    if n_tiles is None:
        return pl.BlockSpec((rows, SEG), lambda i: (i, seg))
    return pl.BlockSpec((rows, SEG), lambda i: (n_tiles - 1 - i, seg))


def _halo_spec(rows, seg, n_tiles=None):
    per = rows // 8
    if n_tiles is None:
        return pl.BlockSpec((8, SEG), lambda i: (jnp.maximum(i * per - 1, 0), seg))
    return pl.BlockSpec((8, SEG), lambda i: (jnp.maximum((n_tiles - 1 - i) * per - 1, 0), seg))


def _const_spec(shape):
    return pl.BlockSpec(shape, lambda i: tuple(0 for _ in shape))


def rglru_fwd(proj, conv_w, conv_b, gw_r, gw_i, gate_b, lam):
    s = proj.shape[0]
    tr = RG_TILE

    def body(xh_ref, x_ref, z_ref, cw_ref, cb_ref, gr_ref, gi_ref, gb_ref, lam_ref, y_ref, h_ref, a_s, b_s, hc):
        first = pl.program_id(0) == 0

        @pl.when(first)
        def _():
            hc[...] = jnp.zeros_like(hc)

        xh = jnp.where(first, 0.0, xh_ref[...])
        a, b = rg_pre(xh, x_ref[...], cw_ref[...], cb_ref[...], gr_ref[...], gi_ref[...], gb_ref[...], lam_ref[...])
        a_s[...] = a
        b_s[...] = b
        hc[0:1, :] = _scan_rows(a_s, b_s, h_ref, hc[0:1, :], tr, False)
        y_ref[...] = (h_ref[...] * jax.nn.silu(z_ref[...])).astype(bf16)

    out = pl.BlockSpec((tr, SEG), lambda i: (i, 0))
    return pl.pallas_call(
        body, grid=(s // tr,),
        in_specs=[_halo_spec(tr, 0), _seg_spec(tr, 0), _seg_spec(tr, 1), _const_spec((4, SEG)), _const_spec((1, SEG)),
                  _const_spec((SEG, SEG)), _const_spec((SEG, SEG)), _const_spec((2, SEG)), _const_spec((1, SEG))],
        out_specs=[out, out],
        out_shape=[jax.ShapeDtypeStruct((s, SEG), bf16), jax.ShapeDtypeStruct((s, SEG), f32)],
        scratch_shapes=[pltpu.VMEM((tr, SEG), f32), pltpu.VMEM((tr, SEG), f32), pltpu.VMEM((8, SEG), f32)],
        name="rglru_fwd", compiler_params=_cp(40))(proj, proj, proj, conv_w, conv_b, gw_r, gw_i, gate_b, lam)


def rglru_bwd(proj, hs, dy, conv_w, conv_b, gw_r, gw_i, gate_b, lam, xf=None):
    s = proj.shape[0]
    tr = RG_TILE
    nt = s // tr
    n_in, n_out = 12, 7
    x_ops, x_in_specs, x_out_shapes, x_out_specs, aliases, x_sems, n_xin, n_land = _carried(xf, n_in, n_out)

    def body(*refs):
        ins, x_refs, outs, land_refs, scratch, sem_refs = _cut_refs(refs, n_in, n_xin, n_out, n_land, 6)
        xh_ref, x_ref, z_ref, hh_ref, h_ref, dy_ref, cw_ref, cb_ref, gr_ref, gi_ref, gb_ref, lam_ref = ins
        dp_ref, dcw_ref, dcb_ref, dgr_ref, dgi_ref, dgb_ref, dlam_ref = outs
        an_s, g_s, dh_s, a_first, dh_first, dhalo = scratch
        i = pl.program_id(0)
        first_tile = i == nt - 1
        _carry_open(xf, i, nt, x_refs, land_refs, sem_refs)

        @pl.when(i == 0)
        def _():
            for ref in (dcw_ref, dcb_ref, dgr_ref, dgi_ref, dgb_ref, dlam_ref, a_first, dh_first, dhalo):
                ref[...] = jnp.zeros_like(ref)

        xh = jnp.where(first_tile, 0.0, xh_ref[...])
        params = (cw_ref[...], cb_ref[...], gr_ref[...], gi_ref[...], gb_ref[...], lam_ref[...])
        (a, _), vjp = jax.vjp(rg_pre, xh, x_ref[...], *params)
        zv = z_ref[...]
        hv = h_ref[...]
        dyv = dy_ref[...]
        sig = jax.nn.sigmoid(zv)
        g_s[...] = dyv * (zv * sig)
        dp_ref[:, SEG:2 * SEG] = (dyv * hv * (sig * (1.0 + zv * (1.0 - sig)))).astype(bf16)
        ridx = lax.broadcasted_iota(jnp.int32, (tr, SEG), 0)
        an_s[...] = jnp.where(ridx == tr - 1, jnp.broadcast_to(a_first[0:1, :], (tr, SEG)), pltpu.roll(a, tr - 1, 0))
        _scan_rows(an_s, g_s, dh_s, dh_first[0:1, :], tr, True)
        dh = dh_s[...]
        h_prev_last = jnp.where(first_tile, 0.0, hh_ref[...])[7:8, :]
        h_prev = pltpu.roll(hv, 1, 0)
        h_prev = jnp.where(ridx == 0, jnp.broadcast_to(h_prev_last, (tr, SEG)), h_prev)
        dxh, dx, dcw, dcb, dgr, dgi, dgb, dlam = vjp((dh * h_prev, dh))
        dx = dx + jnp.concatenate([jnp.zeros((tr - 8, SEG), f32), dhalo[...]], axis=0)
        dp_ref[:, 0:SEG] = dx.astype(bf16)
        dhalo[...] = dxh
        a_first[0:1, :] = a[0:1, :]
        dh_first[0:1, :] = dh[0:1, :]
        dcw_ref[...] += dcw
        dcb_ref[...] += dcb
        dgr_ref[...] += dgr
        dgi_ref[...] += dgi
        dgb_ref[...] += dgb
        dlam_ref[...] += dlam
        _carry_close(xf, i, nt, x_refs, land_refs, sem_refs)

    pspecs = [_const_spec((4, SEG)), _const_spec((1, SEG)), _const_spec((SEG, SEG)), _const_spec((SEG, SEG)),
              _const_spec((2, SEG)), _const_spec((1, SEG))]
    pshapes = [jax.ShapeDtypeStruct(sh, f32) for sh in ((4, SEG), (1, SEG), (SEG, SEG), (SEG, SEG), (2, SEG), (1, SEG))]
    tile = pl.BlockSpec((tr, SEG), lambda i: (nt - 1 - i, 0))
    return pl.pallas_call(
        body, grid=(nt,),
        in_specs=[_halo_spec(tr, 0, nt), _seg_spec(tr, 0, nt), _seg_spec(tr, 1, nt),
                  pl.BlockSpec((8, SEG), lambda i: (jnp.maximum((nt - 1 - i) * (tr // 8) - 1, 0), 0)), tile, tile] + pspecs
        + x_in_specs,
        out_specs=[pl.BlockSpec((tr, 2 * SEG), lambda i: (nt - 1 - i, 0))] + pspecs + x_out_specs,
        out_shape=[jax.ShapeDtypeStruct((s, 2 * SEG), bf16)] + pshapes + x_out_shapes,
        scratch_shapes=[pltpu.VMEM((tr, SEG), f32), pltpu.VMEM((tr, SEG), f32), pltpu.VMEM((tr, SEG), f32),
                        pltpu.VMEM((8, SEG), f32), pltpu.VMEM((8, SEG), f32), pltpu.VMEM((8, SEG), f32)] + x_sems,
        input_output_aliases=aliases, name="rglru_bwd" if xf is None else "rglru_bwd_carrying", compiler_params=_cp(48))(
            proj, proj, proj, hs, hs, dy, conv_w, conv_b, gw_r, gw_i, gate_b, lam, *x_ops)


ML_SEGS = (2, 3, 4, 5, 6)
SMALL_BLK = SMALL0 // HD


def _cut_refs(refs, n_in, n_xin, n_out, n_land, n_scratch):
    bounds = [0, n_in, n_in + n_xin, n_in + n_xin + n_out, n_in + n_xin + n_out + n_land,
              n_in + n_xin + n_out + n_land + n_scratch, len(refs)]
    return [refs[a:b] for a, b in zip(bounds[:-1], bounds[1:])]


def mlstm_fwd(proj, bias_row, norm_w, xf=None):
    s = proj.shape[0]
    nc = s // LR
    n_in, n_out = 8, 4
    x_ops, x_in_specs, x_out_shapes, x_out_specs, aliases, x_sems, n_xin, n_land = _carried(xf, n_in, n_out)

    def body(*refs):
        ins, x_refs, outs, land_refs, scratch, sem_refs = _cut_refs(refs, n_in, n_xin, n_out, n_land, 3)
        q_ref, k_ref, v_ref, o_ref, z_ref, sm_ref, b_ref, nw_ref = ins
        y_ref, cs_ref, ns_ref, ms_ref = outs
        c_s, n_s, m_s = scratch
        i = pl.program_id(0)
        _carry_open(xf, i, nc, x_refs, land_refs, sem_refs)

        @pl.when(i == 0)
        def _():
            c_s[...] = jnp.zeros_like(c_s)
            n_s[...] = jnp.zeros_like(n_s)
            m_s[...] = jnp.zeros_like(m_s)

        cs_ref[0] = c_s[...]
        ns_ref[0] = n_s[...]
        ms_ref[0] = m_s[...]
        y, c2, n2, m2 = ml_chunk(q_ref[...], k_ref[...], v_ref[...], o_ref[...], z_ref[...], sm_ref[...],
                                 b_ref[...], nw_ref[...], c_s[...], n_s[...], m_s[...])
        y_ref[...] = y.astype(bf16)
        c_s[...] = c2
        n_s[...] = n2
        m_s[...] = m2
        _carry_close(xf, i, nc, x_refs, land_refs, sem_refs)

    return pl.pallas_call(
        body, grid=(nc,),
        in_specs=[_seg_spec(LR, sg) for sg in ML_SEGS]
        + [pl.BlockSpec((LR, HD), lambda i: (i, SMALL_BLK)), _const_spec((1, HD)), _const_spec((1, SEG))] + x_in_specs,
        out_specs=[pl.BlockSpec((LR, SEG), lambda i: (i, 0)), pl.BlockSpec((1, SEG, HD), lambda i: (i, 0, 0)),
                   pl.BlockSpec((1, 8, HD), lambda i: (i, 0, 0)), pl.BlockSpec((1, 8, HD), lambda i: (i, 0, 0))] + x_out_specs,
        out_shape=[jax.ShapeDtypeStruct((s, SEG), bf16), jax.ShapeDtypeStruct((nc, SEG, HD), f32),
                   jax.ShapeDtypeStruct((nc, 8, HD), f32), jax.ShapeDtypeStruct((nc, 8, HD), f32)] + x_out_shapes,
        scratch_shapes=[pltpu.VMEM((SEG, HD), f32), pltpu.VMEM((8, HD), f32), pltpu.VMEM((8, HD), f32)] + x_sems,
        input_output_aliases=aliases, name="mlstm_fwd" if xf is None else "mlstm_fwd_carrying")(
            proj, proj, proj, proj, proj, proj, bias_row, norm_w, *x_ops)


def mlstm_bwd(proj, dy, cs, ns, ms, bias_row, norm_w, xf=None):
    s = proj.shape[0]
    nc = s // LR
    n_in, n_out = 12, 4
    x_ops, x_in_specs, x_out_shapes, x_out_specs, aliases, x_sems, n_xin, n_land = _carried(xf, n_in, n_out)

    def body(*refs):
        ins, x_refs, outs, land_refs, scratch, sem_refs = _cut_refs(refs, n_in, n_xin, n_out, n_land, 3)
        q_ref, k_ref, v_ref, o_ref, z_ref, sm_ref, dy_ref, cs_ref, ns_ref, ms_ref, b_ref, nw_ref = ins
        dp_ref, dsm_ref, db_ref, dnw_ref = outs
        dc_s, dn_s, dm_s = scratch
        i = pl.program_id(0)
        _carry_open(xf, i, nc, x_refs, land_refs, sem_refs)

        @pl.when(i == 0)
        def _():
            for ref in (db_ref, dnw_ref, dc_s, dn_s, dm_s):
                ref[...] = jnp.zeros_like(ref)

        _, vjp = jax.vjp(ml_chunk, q_ref[...], k_ref[...], v_ref[...], o_ref[...], z_ref[...], sm_ref[...],
                         b_ref[...], nw_ref[...], cs_ref[0], ns_ref[0], ms_ref[0])
        dq, dk, dv, do, dz, dsm, db, dnw, dc, dn, dm = vjp((dy_ref[...], dc_s[...], dn_s[...], dm_s[...]))
        for j, val in enumerate((dq, dk, dv, do, dz)):
            dp_ref[:, j * SEG:(j + 1) * SEG] = val.astype(bf16)
        dsm_ref[...] = dsm
        db_ref[0:1, :] += db
        dnw_ref[0:1, :] += dnw
        dc_s[...] = dc
        dn_s[...] = dn
        dm_s[...] = dm
        _carry_close(xf, i, nc, x_refs, land_refs, sem_refs)

    rev3 = lambda i: (nc - 1 - i, 0, 0)
    return pl.pallas_call(
        body, grid=(nc,),
        in_specs=[_seg_spec(LR, sg, nc) for sg in ML_SEGS]
        + [pl.BlockSpec((LR, HD), lambda i: (nc - 1 - i, SMALL_BLK)), pl.BlockSpec((LR, SEG), lambda i: (nc - 1 - i, 0)),
           pl.BlockSpec((1, SEG, HD), rev3), pl.BlockSpec((1, 8, HD), rev3), pl.BlockSpec((1, 8, HD), rev3),
           _const_spec((1, HD)), _const_spec((1, SEG))] + x_in_specs,
        out_specs=[pl.BlockSpec((LR, 5 * SEG), lambda i: (nc - 1 - i, 0)), pl.BlockSpec((LR, HD), lambda i: (nc - 1 - i, 0)),
                   _const_spec((8, HD)), _const_spec((8, SEG))] + x_out_specs,
        out_shape=[jax.ShapeDtypeStruct((s, 5 * SEG), bf16), jax.ShapeDtypeStruct((s, HD), f32),
                   jax.ShapeDtypeStruct((8, HD), f32), jax.ShapeDtypeStruct((8, SEG), f32)] + x_out_shapes,
        scratch_shapes=[pltpu.VMEM((SEG, HD), f32), pltpu.VMEM((8, HD), f32), pltpu.VMEM((8, HD), f32)] + x_sems,
        input_output_aliases=aliases, name="mlstm_bwd" if xf is None else "mlstm_bwd_carrying", compiler_params=_cp(48))(
            proj, proj, proj, proj, proj, proj, dy, cs, ns, ms, bias_row, norm_w, *x_ops)


GD_SEGS = (7, 8, 9)


def _carried(xf, n_in, n_out):
    operands, out_shapes, aliases, sems, _, n_xin, n_land = _carry_plumb(xf, n_in, n_out)
    return operands, [_ANY] * n_xin, out_shapes, [_ANY] * n_land, aliases, sems, n_xin, n_land


def _carry_open(xf, i, n_steps, x_refs, land_refs, sem_refs):
    if xf is None:
        return
    srcs = x_refs[:len(xf.srcs)]

    @pl.when(i == 0)
    def _():
        _carry_start(xf, srcs, land_refs, sem_refs)

    @pl.when(i == max(n_steps - 2, 0))
    def _():
        _carry_middle(xf, srcs, land_refs, sem_refs)


def _carry_close(xf, i, n_steps, x_refs, land_refs, sem_refs):
    if xf is None:
        return

    @pl.when(i == n_steps - 1)
    def _():
        _carry_finish(xf, x_refs[:len(xf.srcs)], land_refs, sem_refs)


def gdn_fwd(proj, conv_w, alog_row, dt_row, norm_w, xf=None):
    s = proj.shape[0]
    nc = s // LR
    n_in, n_out = 12, 3
    x_ops, x_in_specs, x_out_shapes, x_out_specs, aliases, x_sems, n_xin, n_land = _carried(xf, n_in, n_out)

    def body(*refs):
        qh_ref, q_ref, kh_ref, k_ref, vh_ref, v_ref, z_ref, sm_ref, cw_ref, al_ref, dt_ref, nw_ref = refs[:n_in]
        x_refs = refs[n_in:n_in + n_xin]
        y_ref, ss_ref, ti_ref = refs[n_in + n_xin:n_in + n_xin + n_out]
        land_refs = refs[n_in + n_xin + n_out:n_in + n_xin + n_out + n_land]
        st_s = refs[n_in + n_xin + n_out + n_land]
        sem_refs = refs[n_in + n_xin + n_out + n_land + 1:]
        i = pl.program_id(0)
        first = i == 0
        _carry_open(xf, i, nc, x_refs, land_refs, sem_refs)

        @pl.when(first)
        def _():
            st_s[...] = jnp.zeros_like(st_s)

        ss_ref[0] = st_s[...]
        halo = [jnp.where(first, 0.0, r[...]) for r in (qh_ref, kh_ref, vh_ref)]
        y, st2, t_invs = gd_chunk(halo[0], q_ref[...], halo[1], k_ref[...], halo[2], v_ref[...], z_ref[...], sm_ref[...],
                                  cw_ref[...], al_ref[...], dt_ref[...], nw_ref[...], st_s[...])
        y_ref[...] = y.astype(bf16)
        ti_ref[...] = t_invs
        st_s[...] = st2
        _carry_close(xf, i, nc, x_refs, land_refs, sem_refs)

    qkv_specs = []
    for sg in GD_SEGS:
        qkv_specs += [_halo_spec(LR, sg), _seg_spec(LR, sg)]
    return pl.pallas_call(
        body, grid=(nc,),
        in_specs=qkv_specs + [_seg_spec(LR, 10), pl.BlockSpec((LR, HD), lambda i: (i, SMALL_BLK)),
                              _const_spec((4, 3 * SEG)), _const_spec((1, HD)), _const_spec((1, HD)), _const_spec((1, HD))]
        + x_in_specs,
        out_specs=[pl.BlockSpec((LR, SEG), lambda i: (i, 0)), pl.BlockSpec((1, SEG, HD), lambda i: (i, 0, 0)),
                   pl.BlockSpec((RUN * NH, LC, LC), lambda i: (i, 0, 0))] + x_out_specs,
        out_shape=[jax.ShapeDtypeStruct((s, SEG), bf16), jax.ShapeDtypeStruct((nc, SEG, HD), f32),
                   jax.ShapeDtypeStruct((s // LC * NH, LC, LC), f32)] + x_out_shapes,
        scratch_shapes=[pltpu.VMEM((SEG, HD), f32)] + x_sems, input_output_aliases=aliases,
        name="gdn_fwd" if xf is None else "gdn_fwd_carrying")(
            proj, proj, proj, proj, proj, proj, proj, proj, conv_w, alog_row, dt_row, norm_w, *x_ops)


def gdn_bwd(proj, dy, ss, t_invs, conv_w, alog_row, dt_row, norm_w, xf=None):
    s = proj.shape[0]
    nc = s // LR
    n_in, n_out = 15, 6
    x_ops, x_in_specs, x_out_shapes, x_out_specs, aliases, x_sems, n_xin, n_land = _carried(xf, n_in, n_out)

    def body(*refs):
        (qh_ref, q_ref, kh_ref, k_ref, vh_ref, v_ref, z_ref, sm_ref, dy_ref, ss_ref, ti_ref,
         cw_ref, al_ref, dt_ref, nw_ref) = refs[:n_in]
        x_refs = refs[n_in:n_in + n_xin]
        dp_ref, dsm_ref, dcw_ref, dal_ref, ddt_ref, dnw_ref = refs[n_in + n_xin:n_in + n_xin + n_out]
        land_refs = refs[n_in + n_xin + n_out:n_in + n_xin + n_out + n_land]
        dst_s, dhalo = refs[n_in + n_xin + n_out + n_land:n_in + n_xin + n_out + n_land + 2]
        sem_refs = refs[n_in + n_xin + n_out + n_land + 2:]
        i = pl.program_id(0)
        first_chunk = i == nc - 1
        _carry_open(xf, i, nc, x_refs, land_refs, sem_refs)

        @pl.when(i == 0)
        def _():
            for ref in (dcw_ref, dal_ref, ddt_ref, dnw_ref, dst_s, dhalo):
                ref[...] = jnp.zeros_like(ref)

        halo = [jnp.where(first_chunk, 0.0, r[...]) for r in (qh_ref, kh_ref, vh_ref)]
        with_known = functools.partial(gd_chunk, known_t=ti_ref[...])
        _, vjp = jax.vjp(with_known, halo[0], q_ref[...], halo[1], k_ref[...], halo[2], v_ref[...], z_ref[...], sm_ref[...],
                         cw_ref[...], al_ref[...], dt_ref[...], nw_ref[...], ss_ref[0])
        dqh, dq, dkh, dk, dvh, dv, dz, dsm, dcw, dal, ddt, dnw, dst = vjp((dy_ref[...], dst_s[...]))
        for j, val in enumerate((dq, dk, dv)):
            val = val + jnp.concatenate([jnp.zeros((LR - 8, SEG), f32), dhalo[:, j * SEG:(j + 1) * SEG]], axis=0)
            dp_ref[:, j * SEG:(j + 1) * SEG] = val.astype(bf16)
        dp_ref[:, 3 * SEG:4 * SEG] = dz.astype(bf16)
        for j, val in enumerate((dqh, dkh, dvh)):
            dhalo[:, j * SEG:(j + 1) * SEG] = val
        dsm_ref[...] = dsm
        dcw_ref[...] += dcw
        dal_ref[0:1, :] += dal
        ddt_ref[0:1, :] += ddt
        dnw_ref[0:1, :] += dnw
        dst_s[...] = dst
        _carry_close(xf, i, nc, x_refs, land_refs, sem_refs)

    qkv_specs = []
    for sg in GD_SEGS:
        qkv_specs += [_halo_spec(LR, sg, nc), _seg_spec(LR, sg, nc)]
    return pl.pallas_call(
        body, grid=(nc,),
        in_specs=qkv_specs + [_seg_spec(LR, 10, nc), pl.BlockSpec((LR, HD), lambda i: (nc - 1 - i, SMALL_BLK)),
                              pl.BlockSpec((LR, SEG), lambda i: (nc - 1 - i, 0)),
                              pl.BlockSpec((1, SEG, HD), lambda i: (nc - 1 - i, 0, 0)),
                              pl.BlockSpec((RUN * NH, LC, LC), lambda i: (nc - 1 - i, 0, 0)),
                              _const_spec((4, 3 * SEG)), _const_spec((1, HD)), _const_spec((1, HD)), _const_spec((1, HD))]
        + x_in_specs,
        out_specs=[pl.BlockSpec((LR, 4 * SEG), lambda i: (nc - 1 - i, 0)), pl.BlockSpec((LR, HD), lambda i: (nc - 1 - i, 0)),
                   _const_spec((4, 3 * SEG)), _const_spec((8, HD)), _const_spec((8, HD)), _const_spec((8, HD))] + x_out_specs,
        out_shape=[jax.ShapeDtypeStruct((s, 4 * SEG), bf16), jax.ShapeDtypeStruct((s, HD), f32),
                   jax.ShapeDtypeStruct((4, 3 * SEG), f32), jax.ShapeDtypeStruct((8, HD), f32),
                   jax.ShapeDtypeStruct((8, HD), f32), jax.ShapeDtypeStruct((8, HD), f32)] + x_out_shapes,
        scratch_shapes=[pltpu.VMEM((SEG, HD), f32), pltpu.VMEM((8, 3 * SEG), f32)] + x_sems, input_output_aliases=aliases,
        name="gdn_bwd" if xf is None else "gdn_bwd_carrying", compiler_params=_cp(48))(
            proj, proj, proj, proj, proj, proj, proj, proj, dy, ss, t_invs, conv_w, alog_row, dt_row, norm_w, *x_ops)


def _my_place():
    return lax.axis_index("x"), lax.axis_index("y"), lax.axis_index("c")


def _slot(p):
    return 4 * p[0] + 2 * p[1] + p[2]


def _peer(me, j):
    flips = ((j >> 2) & 1, (j >> 1) & 1, j & 1)
    return tuple((1 - v) if fl else v for v, fl in zip(me, flips))


_ANY = pl.BlockSpec(memory_space=pl.ANY)


class Scatter(NamedTuple):
    srcs: list
    lands: list
    land_of: list
    layer: list
    span: list


class Gather2(NamedTuple):
    srcs: list
    lands: list
    span: list


def _carry_plumb(cx, n_in, n_out):
    if cx is None:
        return [], [], {}, [], 0, 0, 0
    n_src = len(cx.srcs)
    passed = [li for li, ld in enumerate(cx.lands) if not isinstance(ld, jax.ShapeDtypeStruct)]
    operands = list(cx.srcs) + [cx.lands[li] for li in passed]
    aliases = {n_in + n_src + k: n_out + li for k, li in enumerate(passed)}
    out_shapes = [jax.ShapeDtypeStruct(ld.shape, ld.dtype) for ld in cx.lands]
    sems = [pltpu.SemaphoreType.DMA((n_src, N_DEV - 1)), pltpu.SemaphoreType.DMA((n_src, N_DEV - 1)),
            pltpu.SemaphoreType.DMA((n_src,))]
    return operands, out_shapes, aliases, sems, n_src, len(operands), len(cx.lands)


def _scatter_copies(sc, src_refs, land_refs, send_sems, recv_sems, local_sems):
    me = _my_place()
    mine = _slot(me)
    local, remote = [], []
    for a, src_ref in enumerate(src_refs):
        lo, hi = sc.span[a]
        rows = pl.ds(lo, hi - lo)
        dst = land_refs[sc.land_of[a]].at[mine, sc.layer[a], rows]
        local.append(pltpu.make_async_copy(src_ref.at[mine, rows], dst, local_sems.at[a]))
        for j in range(1, N_DEV):
            peer = _peer(me, j)
            remote.append(pltpu.make_async_remote_copy(
                src_ref=src_ref.at[_slot(peer), rows], dst_ref=dst, send_sem=send_sems.at[a, j - 1],
                recv_sem=recv_sems.at[a, j - 1], device_id=peer, device_id_type=MESH))
    return local, remote


def _gather2_copy(land, sems, a, k, block_of, to, rows, src=None):
    dst = land.at[_slot(block_of), rows]
    return pltpu.make_async_remote_copy(src_ref=dst if src is None else src, dst_ref=dst, send_sem=sems[0].at[a, k],
                                        recv_sem=sems[1].at[a, k], device_id=to, device_id_type=MESH)


def _gather2_places():
    x, y, c = _my_place()
    return (x, y, c), (x, y, 1 - c), [(1 - x, y), (x, 1 - y), (1 - x, 1 - y)], c


def _carry_start(cx, src_refs, land_refs, sems):
    if isinstance(cx, Scatter):
        local, remote = _scatter_copies(cx, src_refs, land_refs, *sems)
        for cp in local + remote:
            cp.start()
        return
    me, sib, chips, c = _gather2_places()
    for a, (src_ref, land) in enumerate(zip(src_refs, land_refs)):
        rows = pl.ds(cx.span[a][0], cx.span[a][1] - cx.span[a][0])
        src = src_ref.at[rows]
        pltpu.make_async_copy(src, land.at[_slot(me), rows], sems[2].at[a]).start()
        _gather2_copy(land, sems, a, 0, me, sib, rows, src=src).start()
        for j, chip in enumerate(chips):
            _gather2_copy(land, sems, a, 1 + j, me, (*chip, c), rows, src=src).start()


def _carry_middle(cx, src_refs, land_refs, sems):
    if isinstance(cx, Scatter):
        return
    me, sib, chips, c = _gather2_places()
    for a, land in enumerate(land_refs):
        rows = pl.ds(cx.span[a][0], cx.span[a][1] - cx.span[a][0])
        for j, chip in enumerate(chips):
            _gather2_copy(land, sems, a, 1 + j, (*chip, c), me, rows).wait_recv()
            _gather2_copy(land, sems, a, 4 + j, (*chip, c), sib, rows).start()


def _carry_finish(cx, src_refs, land_refs, sems):
    if isinstance(cx, Scatter):
        local, remote = _scatter_copies(cx, src_refs, land_refs, *sems)
        for cp in remote:
            cp.wait()
        for cp in local:
            cp.wait()
        return
    me, sib, chips, c = _gather2_places()
    for a, (src_ref, land) in enumerate(zip(src_refs, land_refs)):
        rows = pl.ds(cx.span[a][0], cx.span[a][1] - cx.span[a][0])
        src = src_ref.at[rows]
        _gather2_copy(land, sems, a, 0, sib, me, rows).wait_recv()
        for j, chip in enumerate(chips):
            _gather2_copy(land, sems, a, 4 + j, (*chip, 1 - c), me, rows).wait_recv()
        _gather2_copy(land, sems, a, 0, me, sib, rows, src=src).wait_send()
        for j, chip in enumerate(chips):
            _gather2_copy(land, sems, a, 1 + j, me, (*chip, c), rows, src=src).wait_send()
            _gather2_copy(land, sems, a, 4 + j, (*chip, c), sib, rows).wait_send()
        pltpu.make_async_copy(src, land.at[_slot(me), rows], sems[2].at[a]).wait()


def exchange(cx, name):
    operands, out_shapes, aliases, sems, n_src, n_xin, n_land = _carry_plumb(cx, 0, 0)

    def body(*refs):
        src_refs, land_refs, sem_refs = refs[:n_src], refs[n_xin:n_xin + n_land], refs[n_xin + n_land:]
        _carry_start(cx, src_refs, land_refs, sem_refs)
        _carry_middle(cx, src_refs, land_refs, sem_refs)
        _carry_finish(cx, src_refs, land_refs, sem_refs)

    return pl.pallas_call(body, in_specs=[_ANY] * n_xin, out_specs=[_ANY] * n_land, out_shape=out_shapes,
                          scratch_shapes=sems, input_output_aliases=aliases, name=name)(*operands)


def small_allreduce(packed, name, xf=None):
    r = packed.shape[0]
    rc = r // N_DEV
    x_ops, x_in_specs, x_out_shapes, x_out_specs, aliases, x_sems, n_xin, n_land = _carried(xf, 1, 1)

    def body(*refs):
        (in_ref,), x_refs, (out_ref,), land_refs, scratch, sem_refs = _cut_refs(refs, 1, n_xin, 1, n_land, 5)
        recv_buf, send1, recv1, send2, recv2 = scratch
        if xf is not None:
            _carry_start(xf, x_refs[:len(xf.srcs)], land_refs, sem_refs)
        me = _my_place()
        mine = _slot(me)
        my_rows = pl.ds(pl.multiple_of(mine * rc, 8), rc)
        first, second = [], []
        for j in range(1, N_DEV):
            peer = _peer(me, j)
            peer_rows = pl.ds(pl.multiple_of(_slot(peer) * rc, 8), rc)
            first.append(pltpu.make_async_remote_copy(
                src_ref=in_ref.at[peer_rows], dst_ref=recv_buf.at[mine], send_sem=send1.at[j - 1], recv_sem=recv1.at[j - 1],
                device_id=peer, device_id_type=MESH))
            second.append(pltpu.make_async_remote_copy(
                src_ref=out_ref.at[my_rows], dst_ref=out_ref.at[my_rows], send_sem=send2.at[j - 1], recv_sem=recv2.at[j - 1],
                device_id=peer, device_id_type=MESH))
        for cp in first:
            cp.start()
        recv_buf[mine] = in_ref[my_rows]
        for cp in first:
            cp.wait()
        acc = recv_buf[0]
        for k in range(1, N_DEV):
            acc = acc + recv_buf[k]
        out_ref[my_rows] = acc
        for cp in second:
            cp.start()
        for cp in second:
            cp.wait()
        if xf is not None:
            _carry_finish(xf, x_refs[:len(xf.srcs)], land_refs, sem_refs)

    vmem = pl.BlockSpec(memory_space=pltpu.VMEM)
    return pl.pallas_call(
        body, in_specs=[vmem] + x_in_specs, out_specs=[vmem] + x_out_specs,
        out_shape=[jax.ShapeDtypeStruct((r, 128), f32)] + x_out_shapes,
        scratch_shapes=[pltpu.VMEM((N_DEV, rc, 128), f32)] + [pltpu.SemaphoreType.DMA((N_DEV - 1,))] * 4 + x_sems,
        input_output_aliases=aliases, name=name, compiler_params=_cp(32))(packed, *x_ops)


def _adamw(w, g, m, v):
    m = ADAM_B1 * m + (1.0 - ADAM_B1) * g
    v = ADAM_B2 * v + (1.0 - ADAM_B2) * (g * g)
    m_hat = m / (1.0 - ADAM_B1 ** ADAM_STEP)
    v_hat = v / (1.0 - ADAM_B2 ** ADAM_STEP)
    delta = -ADAM_LR * (m_hat / (jnp.sqrt(v_hat) + ADAM_EPS) + ADAM_WD * w)
    return delta, m, v


def adam_slots(slots, w, m, v, rows, name):
    _, depth, r, c = slots.shape

    def body(s_ref, w_ref, m_ref, v_ref, g_ref, d_ref, m2_ref, v2_ref):
        g = s_ref[0, 0].astype(f32)
        for k in range(1, N_DEV):
            g = g + s_ref[k, 0].astype(f32)
        d, m2, v2 = _adamw(w_ref[0], g, m_ref[0], v_ref[0])
        g_ref[0] = g
        d_ref[0] = d
        m2_ref[0] = m2
        v2_ref[0] = v2

    blk = pl.BlockSpec((1, rows, c), lambda l, i: (l, i, 0))
    sh = jax.ShapeDtypeStruct((depth, r, c), f32)
    return pl.pallas_call(
        body, grid=(depth, r // rows),
        in_specs=[pl.BlockSpec((N_DEV, 1, rows, c), lambda l, i: (0, l, i, 0)), blk, blk, blk],
        out_specs=[blk] * 4, out_shape=[sh] * 4, name=name, compiler_params=_cp(40))(slots, w, m, v)


def adam_flat(g, w, m, v, name):
    def body(g_ref, w_ref, m_ref, v_ref, d_ref, m2_ref, v2_ref):
        d, m2, v2 = _adamw(w_ref[...], g_ref[...], m_ref[...], v_ref[...])
        d_ref[...] = d
        m2_ref[...] = m2
        v2_ref[...] = v2

    sh = jax.ShapeDtypeStruct(g.shape, f32)
    return pl.pallas_call(body, out_shape=[sh] * 3, name=name)(g, w, m, v)


def _rows_of(shape):
    n = 1
    for dim in shape:
        n *= dim
    return n, -(-n // (8 * 128)) * 8


def _pack(arrs):
    parts = []
    for a in arrs:
        n, rows = _rows_of(a.shape)
        parts.append(jnp.pad(a.reshape(-1).astype(f32), (0, rows * 128 - n)).reshape(rows, 128))
    return jnp.concatenate(parts, axis=0)


def _unpack(packed, shapes):
    out, row = [], 0
    for sh in shapes:
        n, rows = _rows_of(sh)
        out.append(packed[row:row + rows].reshape(-1)[:n].reshape(sh))
        row += rows
    return out


def _block_diag(gw):
    eye = jnp.eye(8, dtype=gw.dtype)
    return (gw[:, :, None, :] * eye[:, None, :, None]).reshape(SEG, SEG)


def _diag_blocks(dense):
    eye = jnp.eye(8, dtype=dense.dtype)
    return (dense.reshape(8, 64, 8, 64) * eye[:, None, :, None]).sum(axis=2)


def _lane_row(vals, first_lane):
    return jnp.zeros((1, HD), f32).at[0, first_lane:first_lane + NH].set(vals)


def kernel(x, norm_w, w_in, rg_conv_w, rg_conv_b, rg_gate_w, rg_gate_b, rg_lambda, ml_gate_b, ml_norm_w, gd_conv_w, gd_a_log, gd_dt_bias, gd_norm_w, w_out, final_norm_w, loss_target, m_norm_w, m_w_in, m_rg_conv_w, m_rg_conv_b, m_rg_gate_w, m_rg_gate_b, m_rg_lambda, m_ml_gate_b, m_ml_norm_w, m_gd_conv_w, m_gd_a_log, m_gd_dt_bias, m_gd_norm_w, m_w_out, m_final_norm_w, v_norm_w, v_w_in, v_rg_conv_w, v_rg_conv_b, v_rg_gate_w, v_rg_gate_b, v_rg_lambda, v_ml_gate_b, v_ml_norm_w, v_gd_conv_w, v_gd_a_log, v_gd_dt_bias, v_gd_norm_w, v_w_out, v_final_norm_w):
    s = x.shape[1]
    xs = x.reshape(s, D)
    tgt = loss_target.reshape(s, D)
    me = 4 * lax.axis_index("x") + 2 * lax.axis_index("y") + lax.axis_index("c")

    comm = MeshComm(w_in, w_out, [rg_conv_w, rg_gate_b, gd_conv_w])
    rg_conv_full, rg_gb_full, gd_conv_full = comm.small_weights
    loss_part, dx, d_fw, g_small = local_step(
        xs, tgt, comm, rg_conv_full, rg_gb_full, gd_conv_full, norm_w, rg_conv_b, rg_gate_w, rg_lambda,
        ml_gate_b, ml_norm_w, gd_a_log, gd_dt_bias, gd_norm_w, final_norm_w)
    given_w = dict(norm_w=norm_w, rg_conv_w=rg_conv_w, rg_conv_b=rg_conv_b, rg_gate_w=rg_gate_w, rg_gate_b=rg_gate_b,
                   rg_lambda=rg_lambda, ml_gate_b=ml_gate_b, ml_norm_w=ml_norm_w, gd_conv_w=gd_conv_w, gd_a_log=gd_a_log,
                   gd_dt_bias=gd_dt_bias, gd_norm_w=gd_norm_w, final_norm_w=final_norm_w, w_in=w_in, w_out=w_out)
    given_m = dict(norm_w=m_norm_w, rg_conv_w=m_rg_conv_w, rg_conv_b=m_rg_conv_b, rg_gate_w=m_rg_gate_w, rg_gate_b=m_rg_gate_b,
                   rg_lambda=m_rg_lambda, ml_gate_b=m_ml_gate_b, ml_norm_w=m_ml_norm_w, gd_conv_w=m_gd_conv_w,
                   gd_a_log=m_gd_a_log, gd_dt_bias=m_gd_dt_bias, gd_norm_w=m_gd_norm_w, final_norm_w=m_final_norm_w,
                   w_in=m_w_in, w_out=m_w_out)
    given_v = dict(norm_w=v_norm_w, rg_conv_w=v_rg_conv_w, rg_conv_b=v_rg_conv_b, rg_gate_w=v_rg_gate_w, rg_gate_b=v_rg_gate_b,
                   rg_lambda=v_rg_lambda, ml_gate_b=v_ml_gate_b, ml_norm_w=v_ml_norm_w, gd_conv_w=v_gd_conv_w,
                   gd_a_log=v_gd_a_log, gd_dt_bias=v_gd_dt_bias, gd_norm_w=v_gd_norm_w, final_norm_w=v_final_norm_w,
                   w_in=v_w_in, w_out=v_w_out)
    return finish_step(loss_part, dx, d_fw, g_small, comm, s, me, given_w, given_m, given_v)


def _gathered_pieces():
    per = D_IN // N_DEV
    pieces = []
    for lo, hi in ((0, 3584), (3592, 5640), (3584, 3592), (5640, 5648)):
        col = lo
        while col < hi:
            k = col // per
            end = min(hi, (k + 1) * per)
            pieces.append((k, col - k * per, end - k * per))
            col = end
    return pieces


def regroup_w_in(wi_g):
    rows = 256
    pieces = _gathered_pieces()

    def body(x_ref, o_ref):
        cols = [x_ref[k, :, a:b] for k, a, b in pieces] + [jnp.zeros((rows, DP - D_IN), wi_g.dtype)]
        o_ref[...] = jnp.concatenate(cols, axis=1)

    return pl.pallas_call(
        body, grid=(D // rows,), in_specs=[pl.BlockSpec((N_DEV, rows, D_IN // N_DEV), lambda i: (0, i, 0))],
        out_specs=pl.BlockSpec((rows, DP), lambda i: (i, 0)), out_shape=jax.ShapeDtypeStruct((D, DP), wi_g.dtype),
        name="regroup_w_in")(wi_g)


def slots_of_w_in_grad(gw_rg, gw_ml, gw_gd, gw_sm):
    rows = 256
    per = D_IN // N_DEV

    def body(rg_ref, ml_ref, gd_ref, sm_ref, o_ref):
        sm = sm_ref[...]
        g = jnp.concatenate([rg_ref[...], ml_ref[...], sm[:, 0:2 * NH], gd_ref[...], sm[:, 2 * NH:4 * NH]], axis=1)
        for k in range(N_DEV):
            o_ref[k] = g[:, k * per:(k + 1) * per]

    return pl.pallas_call(
        body, grid=(D // rows,),
        in_specs=[pl.BlockSpec((rows, a.shape[1]), lambda i: (i, 0)) for a in (gw_rg, gw_ml, gw_gd, gw_sm)],
        out_specs=pl.BlockSpec((N_DEV, rows, per), lambda i: (0, i, 0)),
        out_shape=jax.ShapeDtypeStruct((N_DEV, D, per), gw_rg.dtype), name="slots_of_w_in_grad",
        compiler_params=_cp(40))(gw_rg, gw_ml, gw_gd, gw_sm)


class MeshComm:
    GWI_SPLIT = 448
    WI_SPLIT = 384

    def __init__(self, w_in, w_out, small_shards):
        per = D_IN // N_DEV
        self.wi_sh = [w_in[l].astype(bf16) for l in range(DEPTH)]
        self.wo_sh = [w_out[l].astype(bf16) for l in range(DEPTH)]
        self.wi_land = jax.ShapeDtypeStruct((N_DEV, D, per), bf16)
        self.wo_land = jax.ShapeDtypeStruct((N_DEV, 3 * SEG // N_DEV, D), bf16)
        packed = _pack(small_shards)
        first = Gather2([self.wi_sh[0], self.wo_sh[0], packed],
                        [self.wi_land, self.wo_land, jax.ShapeDtypeStruct((N_DEV,) + packed.shape, f32)],
                        [(0, D), (0, 3 * SEG // N_DEV), (0, packed.shape[0])])
        wi_g, wo_g, sm_g = exchange(first, "gather_first")
        self.wi_g, self.wo_g = {0: wi_g}, {0: wo_g}
        shapes = [a.shape for a in small_shards]
        parts = [_unpack(sm_g[k], shapes) for k in range(N_DEV)]
        self.small_weights = [jnp.concatenate([p[j] for p in parts], axis=-1) for j in range(len(small_shards))]
        self.gwi_land = lax.empty((N_DEV, DEPTH, D, per), bf16)
        self.gwo_land = lax.empty((N_DEV, DEPTH, 3 * SEG // N_DEV, D), bf16)
        self.gwi_slots = {}
        self.gwo_slots = {}

    def weights(self, l):
        return regroup_w_in(self.wi_g[l]), self.wo_g[l].reshape(3 * SEG, D)

    def fwd_carry(self, l, host):
        if l + 1 >= DEPTH:
            return None
        if host == "mlstm":
            return Gather2([self.wo_sh[l + 1]], [self.wo_land], [(0, 3 * SEG // N_DEV)])
        if host == "inproj":
            return Gather2([self.wi_sh[l + 1]], [self.wi_land], [(0, self.WI_SPLIT)])
        return Gather2([self.wi_sh[l + 1]], [self.wi_g[l + 1]], [(self.WI_SPLIT, D)])

    def fwd_landed(self, l, host, landed):
        (self.wo_g if host == "mlstm" else self.wi_g)[l + 1] = landed[0]

    def own_w_out_grad(self, l, g_wo):
        self.gwo_slots[l] = g_wo.reshape(N_DEV, 3 * SEG // N_DEV, D).astype(bf16)

    def bwd_carry(self, l, host):
        rows_o = 3 * SEG // N_DEV
        srcs, land_of, layer, span = [], [], [], []
        if l + 1 < DEPTH:
            if host == "rglru":
                srcs, land_of, layer, span = [self.gwo_slots[l + 1]], [1], [l + 1], [(0, rows_o)]
            elif host == "mlstm":
                srcs, land_of, layer, span = [self.gwi_slots[l + 1]], [0], [l + 1], [(0, self.GWI_SPLIT)]
            else:
                srcs, land_of, layer, span = [self.gwi_slots[l + 1]], [0], [l + 1], [(self.GWI_SPLIT, D)]
        if l == 0 and host == "mlstm":
            srcs, land_of, layer, span = srcs + [self.gwo_slots[0]], land_of + [1], layer + [0], span + [(0, rows_o)]
        if not srcs:
            return None
        return Scatter(srcs, [self.gwi_land, self.gwo_land], land_of, layer, span)

    def bwd_landed(self, landed):
        self.gwi_land, self.gwo_land = landed

    def grads_ready(self, l, gw_rg, gw_ml, gw_gd, gw_sm):
        self.gwi_slots[l] = slots_of_w_in_grad(gw_rg, gw_ml, gw_gd, gw_sm)

    def last_carry(self):
        return Scatter([self.gwi_slots[0]], [self.gwi_land, self.gwo_land], [0], [0], [(0, D)])


def local_step(xs, tgt, comm, rg_conv_full, rg_gb_full, gd_conv_full, norm_w, rg_conv_b, rg_gate_w,
               rg_lambda, ml_gate_b, ml_norm_w, gd_a_log, gd_dt_bias, gd_norm_w, final_norm_w):
    acts = []
    for l in range(DEPTH):
        nw = norm_w[l].reshape(1, D)
        w_in_l, w_out_l = comm.weights(l)
        xf = comm.fwd_carry(l, "inproj")
        proj, hn_t, *landed = inproj(xs, nw, w_in_l, xf=xf)
        if xf is not None:
            comm.fwd_landed(l, "inproj", landed)
        rg_p = (rg_conv_full[l], rg_conv_b[l].reshape(1, SEG), _block_diag(rg_gate_w[l, 0]), _block_diag(rg_gate_w[l, 1]),
                rg_gb_full[l], rg_lambda[l].reshape(1, SEG))
        y_rg, hs = rglru_fwd(proj, *rg_p)
        ml_p = (jnp.zeros((1, HD), f32).at[0, 0:2 * NH].set(ml_gate_b[l].reshape(-1)), ml_norm_w[l].reshape(1, SEG))
        xf = comm.fwd_carry(l, "mlstm")
        y_ml, cs, ns, ms, *landed = mlstm_fwd(proj, *ml_p, xf=xf)
        if xf is not None:
            comm.fwd_landed(l, "mlstm", landed)
        gd_p = (gd_conv_full[l], _lane_row(gd_a_log[l], 2 * NH), _lane_row(gd_dt_bias[l], 2 * NH), gd_norm_w[l].reshape(1, HD))
        xf = comm.fwd_carry(l, "gdn")
        y_gd, ss, t_invs, *landed = gdn_fwd(proj, *gd_p, xf=xf)
        if xf is not None:
            comm.fwd_landed(l, "gdn", landed)
        acts.append((xs, nw, proj, hn_t, w_in_l, w_out_l, rg_p, y_rg, hs, ml_p, y_ml, cs, ns, ms, gd_p, y_gd, ss, t_invs))
        xs = outproj(xs, y_rg, y_ml, y_gd, w_out_l)

    dx, loss_part, d_fw = head(xs, final_norm_w.reshape(1, D), tgt)

    g_small = {k: [None] * DEPTH for k in ("norm_w", "rg_conv_w", "rg_conv_b", "rg_gate_w", "rg_gate_b", "rg_lambda",
                                           "ml_gate_b", "ml_norm_w", "gd_conv_w", "gd_a_log", "gd_dt_bias", "gd_norm_w")}
    for l in reversed(range(DEPTH)):
        x_l, nw, proj, hn_t, w_in_l, w_out_l, rg_p, y_rg, hs, ml_p, y_ml, cs, ns, ms, gd_p, y_gd, ss, t_invs = acts[l]
        dy_rg, dy_ml, dy_gd, g_wo = outproj_bwd(dx, y_rg, y_ml, y_gd, w_out_l)
        comm.own_w_out_grad(l, g_wo)
        xf = comm.bwd_carry(l, "rglru")
        d_rg, d_cw, d_cb, d_gr, d_gi, d_gb, d_lam, *landed = rglru_bwd(proj, hs, dy_rg, *rg_p, xf=xf)
        if xf is not None:
            comm.bwd_landed(landed)
        xf = comm.bwd_carry(l, "mlstm")
        d_ml, d_sm_ml, d_bias, d_mnw, *landed = mlstm_bwd(proj, dy_ml, cs, ns, ms, *ml_p, xf=xf)
        if xf is not None:
            comm.bwd_landed(landed)
        xf = comm.bwd_carry(l, "gdn")
        d_gd, d_sm_gd, d_gcw, d_al, d_dt, d_gnw, *landed = gdn_bwd(proj, dy_gd, ss, t_invs, *gd_p, xf=xf)
        if xf is not None:
            comm.bwd_landed(landed)
        dx, d_nw = inproj_bwd_x(x_l, nw, w_in_l, dx, d_rg, d_ml, d_gd, d_sm_ml, d_sm_gd)
        gw_rg = wgrad(hn_t, [d_rg], "wgrad_rg")
        gw_ml = wgrad(hn_t, [d_ml], "wgrad_ml")
        gw_gd = wgrad(hn_t, [d_gd], "wgrad_gd")
        gw_sm = wgrad(hn_t, [d_sm_ml, d_sm_gd], "wgrad_small")
        comm.grads_ready(l, gw_rg, gw_ml, gw_gd, gw_sm)
        g_small["norm_w"][l] = d_nw[0]
        g_small["rg_conv_w"][l] = d_cw
        g_small["rg_conv_b"][l] = d_cb[0]
        g_small["rg_gate_w"][l] = jnp.stack([_diag_blocks(d_gr), _diag_blocks(d_gi)])
        g_small["rg_gate_b"][l] = d_gb
        g_small["rg_lambda"][l] = d_lam[0]
        g_small["ml_gate_b"][l] = d_bias[0, 0:2 * NH].reshape(2, NH)
        g_small["ml_norm_w"][l] = d_mnw[0]
        g_small["gd_conv_w"][l] = d_gcw
        g_small["gd_a_log"][l] = d_al[0, 2 * NH:3 * NH]
        g_small["gd_dt_bias"][l] = d_dt[0, 2 * NH:3 * NH]
        g_small["gd_norm_w"][l] = d_gnw[0]
    return loss_part, dx, d_fw, g_small


def finish_step(loss_part, dx, d_fw, g_small, comm, s, me, given_w, given_m, given_v):
    small_names = ["norm_w", "rg_conv_w", "rg_conv_b", "rg_gate_w", "rg_gate_b", "rg_lambda", "ml_gate_b", "ml_norm_w",
                   "gd_conv_w", "gd_a_log", "gd_dt_bias", "gd_norm_w"]
    small_list = [loss_part[0, 0:1], d_fw[0]] + [jnp.stack(g_small[k]) for k in small_names]
    small_shapes = [a.shape for a in small_list]
    packed = _pack(small_list)
    packed = jnp.pad(packed, ((0, -packed.shape[0] % (8 * N_DEV)), (0, 0)))
    summed, gwi_r, gwo_r = small_allreduce(packed, "last_exchange", xf=comm.last_carry())
    g_all = _unpack(summed, small_shapes)

    g_wi, d_wi, m_wi, v_wi = adam_slots(gwi_r, given_w["w_in"], given_m["w_in"], given_v["w_in"], 256, "adam_w_in")
    g_wo, d_wo, m_wo, v_wo = adam_slots(gwo_r, given_w["w_out"], given_m["w_out"], given_v["w_out"], 192, "adam_w_out")
    loss = g_all[0][0]
    grads = {"final_norm_w": g_all[1]}
    for k, g in zip(small_names, g_all[2:]):
        grads[k] = g
    for k, width in (("rg_conv_w", 64), ("rg_gate_b", 64), ("gd_conv_w", 192)):
        grads[k] = lax.dynamic_slice_in_dim(grads[k], me * width, width, axis=2)
    names = small_names + ["final_norm_w"]
    shapes = [given_w[k].shape for k in names]
    d_p, m_p, v_p = adam_flat(_pack([grads[k] for k in names]), _pack([given_w[k] for k in names]),
                              _pack([given_m[k] for k in names]), _pack([given_v[k] for k in names]), "adam_small")
    deltas = dict(zip(names, _unpack(d_p, shapes)))
    new_m = dict(zip(names, _unpack(m_p, shapes)))
    new_v = dict(zip(names, _unpack(v_p, shapes)))
    grads["w_in"], deltas["w_in"], new_m["w_in"], new_v["w_in"] = g_wi, d_wi, m_wi, v_wi
    grads["w_out"], deltas["w_out"], new_m["w_out"], new_v["w_out"] = g_wo, d_wo, m_wo, v_wo

    order = ["norm_w", "w_in", "rg_conv_w", "rg_conv_b", "rg_gate_w", "rg_gate_b", "rg_lambda", "ml_gate_b", "ml_norm_w",
             "gd_conv_w", "gd_a_log", "gd_dt_bias", "gd_norm_w", "w_out", "final_norm_w"]
    return (loss, dx.reshape(1, s, D), *[grads[k] for k in order], *[deltas[k] for k in order],
            *[new_m[k] for k in order], *[new_v[k] for k in order])
```

```python
import functools
from typing import NamedTuple

import jax
import jax.numpy as jnp
from jax import lax
from jax.experimental import pallas as pl
from jax.experimental.pallas import tpu as pltpu

f32 = jnp.float32
bf16 = jnp.bfloat16
MESH = pl.DeviceIdType.MESH

N_DEV = 8
D = 1024
DEPTH = 4
EPS = 1e-6
SEG = 512
HD = 128
NH = 4
LC = 64
RUN = 4
LR = RUN * LC
D_IN = 5648
DP = 5760
SMALL0 = 5632
RG_TILE = 256
RG_C = 8.0

ADAM_LR = 0.001
ADAM_B1 = 0.9
ADAM_B2 = 0.999
ADAM_EPS = 1e-08
ADAM_WD = 0.01
ADAM_STEP = 10


def _cp(vmem_mb):
    return pltpu.CompilerParams(vmem_limit_bytes=vmem_mb * 2 ** 20)


def _dot(a, b, ca, cb):
    return lax.dot_general(a.astype(bf16), b.astype(bf16), (((ca,), (cb,)), ((), ())), preferred_element_type=f32)


@jax.custom_vjp
def mm_nn(a, b):
    return _dot(a, b, 1, 0)


@jax.custom_vjp
def mm_nt(a, b):
    return _dot(a, b, 1, 1)


@jax.custom_vjp
def mm_tn(a, b):
    return _dot(a, b, 0, 0)


mm_nn.defvjp(lambda a, b: (mm_nn(a, b), (a, b)), lambda r, g: (mm_nt(g, r[1]), mm_tn(r[0], g)))
mm_nt.defvjp(lambda a, b: (mm_nt(a, b), (a, b)), lambda r, g: (mm_nn(g, r[1]), mm_tn(g, r[0])))
mm_tn.defvjp(lambda a, b: (mm_tn(a, b), (a, b)), lambda r, g: (mm_nt(r[1], g), mm_nn(r[0], g)))


def _split(x):
    hi = x.astype(bf16)
    return hi, (x - hi.astype(f32)).astype(bf16)


def dot3(a, b):
    ah, al = _split(a)
    bh, bl = _split(b)
    d = functools.partial(jnp.dot, preferred_element_type=f32)
    return d(ah, bh) + (d(al, bh) + d(ah, bl))


def _tri_sum(x, reverse):
    n = x.shape[0]
    r = lax.broadcasted_iota(jnp.int32, (n, 3 * n), 0)
    c = lax.broadcasted_iota(jnp.int32, (n, 3 * n), 1) & (n - 1)
    ones = ((c >= r) if reverse else (c <= r)).astype(bf16)
    hi = x.astype(bf16)
    rest = x - hi.astype(f32)
    mid = rest.astype(bf16)
    lo = (rest - mid.astype(f32)).astype(bf16)
    return jnp.dot(ones, jnp.concatenate([hi, mid, lo], axis=0), preferred_element_type=f32)


@jax.custom_vjp
def cumsum_rows(x):
    return _tri_sum(x, False)


@jax.custom_vjp
def rev_cumsum_rows(x):
    return _tri_sum(x, True)


cumsum_rows.defvjp(lambda x: (cumsum_rows(x), None), lambda _, g: (rev_cumsum_rows(g),))
rev_cumsum_rows.defvjp(lambda x: (rev_cumsum_rows(x), None), lambda _, g: (cumsum_rows(g),))


def _tri(n, strict=False):
    r = lax.broadcasted_iota(jnp.int32, (n, n), 0)
    c = lax.broadcasted_iota(jnp.int32, (n, n), 1)
    return (r > c) if strict else (r >= c)


def _lane_col(v, j):
    lane = lax.broadcasted_iota(jnp.int32, v.shape, 1)
    return jnp.sum(jnp.where(lane == j, v, 0.0), axis=1, keepdims=True)


def _rows_from(pieces, rows, width):
    ridx = lax.broadcasted_iota(jnp.int32, (rows, width), 0)
    out = jnp.zeros((rows, width), f32)
    for h, p in enumerate(pieces):
        out = out + jnp.where(ridx == h, jnp.broadcast_to(p, (rows, width)), 0.0)
    return out


def causal_conv(halo8, x, w4):
    t = x.shape[0]
    xe = jnp.concatenate([halo8, x], axis=0)
    y = xe[5:5 + t] * w4[0:1]
    for k in range(1, 4):
        y = y + xe[5 + k:5 + k + t] * w4[k:k + 1]
    return y


def ml_chunk(q, k, v, o_pre, z, small, bias_row, norm_w, C, n, m):
    n_ch = q.shape[0] // LC
    lane = lax.broadcasted_iota(jnp.int32, small.shape, 1)
    pre = small + bias_row
    lg = jnp.where(lane < NH, pre, jnp.where(lane < 2 * NH, jax.nn.log_sigmoid(pre), 0.0))
    rows = [slice(c * LC, (c + 1) * LC) for c in range(n_ch)]
    lgs = [lg[r] for r in rows]
    bcs = [cumsum_rows(x) for x in lgs]
    lg_ts = [x.T for x in lgs]
    bc_ts = [x.T for x in bcs]
    causal = _tri(LC)
    pairs = [(c, h) for c in range(n_ch) for h in range(NH)]
    idx = range(len(pairs))
    sls = [slice(h * HD, (h + 1) * HD) for h in range(NH)]
    qs = [q[rows[c], sls[h]] * (HD ** -0.5) for c, h in pairs]
    ks = [k[rows[c], sls[h]] for c, h in pairs]
    vs = [v[rows[c], sls[h]] for c, h in pairs]
    li_cols = [_lane_col(lgs[c], h) for c, h in pairs]
    b_cols = [_lane_col(bcs[c], NH + h) for c, h in pairs]
    dms = [jnp.where(causal, b_cols[i] - bc_ts[c][NH + h:NH + h + 1, :] + lg_ts[c][h:h + 1, :], -jnp.inf)
           for i, (c, h) in enumerate(pairs)]
    dm_maxs = [jnp.max(dms[i], axis=-1, keepdims=True) for i in idx]
    gs = [b_cols[i][LC - 1:LC, :] for i in idx]
    wss = [gs[i] - b_cols[i] + li_cols[i] for i in idx]
    ws_maxs = [jnp.max(wss[i], axis=0, keepdims=True) for i in idx]
    qks = [mm_nt(qs[i], ks[i]) for i in idx]
    m_in = [None] * len(pairs)
    m_out = [None] * len(pairs)
    for h in range(NH):
        cur = m[h:h + 1, 0:1]
        for c in range(n_ch):
            i = c * NH + h
            m_in[i] = cur
            cur = jnp.maximum(gs[i] + cur, ws_maxs[i])
            m_out[i] = cur
    m_inters = [b_cols[i] + m_in[i] for i in idx]
    m_ts = [jnp.maximum(m_inters[i], dm_maxs[i]) for i in idx]
    ss = [qks[i] * jnp.exp(dms[i] - m_ts[i]) for i in idx]
    scs = [jnp.exp(m_inters[i] - m_ts[i]) for i in idx]
    decs = [jnp.exp(gs[i] + m_in[i] - m_out[i]) for i in idx]
    kws = [jnp.exp(wss[i] - m_out[i]) * ks[i] for i in idx]
    c_adds = [mm_tn(kws[i], vs[i]) for i in idx]
    n_adds = [jnp.sum(kws[i], axis=0, keepdims=True) for i in idx]
    svs = [mm_nn(ss[i], vs[i]) for i in idx]
    s_sums = [jnp.sum(ss[i], axis=-1, keepdims=True) for i in idx]
    c_hs = [C[sl, :] for sl in sls]
    n_hs = [n[h:h + 1, :] for h in range(NH)]
    hhs = [None] * len(pairs)
    for c in range(n_ch):
        for h in range(NH):
            i = c * NH + h
            num = svs[i] + scs[i] * mm_nn(qs[i], c_hs[h])
            den = s_sums[i] + scs[i] * jnp.sum(qs[i] * n_hs[h], axis=-1, keepdims=True)
            hhs[i] = num / jnp.maximum(jnp.abs(den), jnp.exp(-m_ts[i]))
        c_hs = [decs[c * NH + h] * c_hs[h] + c_adds[c * NH + h] for h in range(NH)]
        n_hs = [decs[c * NH + h] * n_hs[h] + n_adds[c * NH + h] for h in range(NH)]
    ys = [hhs[i] * lax.rsqrt(jnp.mean(hhs[i] * hhs[i], axis=-1, keepdims=True) + EPS) * norm_w[:, sls[h]]
          * jax.nn.sigmoid(o_pre[rows[c], sls[h]]) * jax.nn.silu(z[rows[c], sls[h]]) for i, (c, h) in enumerate(pairs)]
    y = jnp.concatenate([jnp.concatenate(ys[c * NH:(c + 1) * NH], axis=1) for c in range(n_ch)], axis=0)
    last = (n_ch - 1) * NH
    m_rows = [jnp.broadcast_to(m_out[last + h], (1, HD)) for h in range(NH)]
    return y, jnp.concatenate(c_hs, axis=0), _rows_from(n_hs, 8, HD), _rows_from(m_rows, 8, HD)


@jax.custom_vjp
def _unit_lower_inverses(mats):
    eye = (lax.broadcasted_iota(jnp.int32, (LC, LC), 0) == lax.broadcasted_iota(jnp.int32, (LC, LC), 1)).astype(f32)
    ps = [-m for m in mats]
    ts = [eye + p for p in ps]
    for _ in range(5):
        ps = [dot3(p, p) for p in ps]
        ts = [t + dot3(t, p) for t, p in zip(ts, ps)]
    return tuple(ts)


def _unit_lower_inverses_fwd(mats):
    ts = _unit_lower_inverses(mats)
    return ts, ts


def _unit_lower_inverses_bwd(ts, gs):
    tts = [t.T for t in ts]
    mid = [dot3(tt, g) for tt, g in zip(tts, gs)]
    return (tuple(-dot3(m, tt) for m, tt in zip(mid, tts)),)


_unit_lower_inverses.defvjp(_unit_lower_inverses_fwd, _unit_lower_inverses_bwd)


@jax.custom_vjp
def _known_inverses(mats, ts):
    return tuple(ts)


_known_inverses.defvjp(lambda mats, ts: (tuple(ts), tuple(ts)),
                       lambda ts, gs: (_unit_lower_inverses_bwd(ts, gs)[0], tuple(jnp.zeros_like(t) for t in ts)))


def gd_chunk(qh8, q, kh8, k, vh8, v, z, small, conv_w, alog_row, dt_row, norm_w, st, known_t=None):
    n_ch = q.shape[0] // LC
    lane = lax.broadcasted_iota(jnp.int32, small.shape, 1)
    is_a = (lane >= 2 * NH) & (lane < 3 * NH)
    g_all = jnp.where(is_a, -jnp.exp(alog_row) * jax.nn.softplus(small + dt_row), 0.0)
    beta_all = jax.nn.sigmoid(small)
    qc = jax.nn.silu(causal_conv(qh8, q, conv_w[:, 0:SEG]))
    kc = jax.nn.silu(causal_conv(kh8, k, conv_w[:, SEG:2 * SEG]))
    vc = jax.nn.silu(causal_conv(vh8, v, conv_w[:, 2 * SEG:3 * SEG]))
    incl = _tri(LC)
    strict = _tri(LC, strict=True)
    rows = [slice(c * LC, (c + 1) * LC) for c in range(n_ch)]
    gc_alls = [cumsum_rows(g_all[r]) for r in rows]
    gc_ts = [g.T for g in gc_alls]
    pairs = [(c, h) for c in range(n_ch) for h in range(NH)]
    idx = range(len(pairs))
    sls = [slice(h * HD, (h + 1) * HD) for h in range(NH)]
    qs = [qc[rows[c], sls[h]] for c, h in pairs]
    ks = [kc[rows[c], sls[h]] for c, h in pairs]
    vs = [vc[rows[c], sls[h]] for c, h in pairs]
    qs = [x * lax.rsqrt(jnp.sum(x * x, axis=-1, keepdims=True) + EPS) * (HD ** -0.5) for x in qs]
    ks = [x * lax.rsqrt(jnp.sum(x * x, axis=-1, keepdims=True) + EPS) for x in ks]
    betas = [_lane_col(beta_all[rows[c]], 3 * NH + h) for c, h in pairs]
    gcs = [_lane_col(gc_alls[c], 2 * NH + h) for c, h in pairs]
    gams = [jnp.exp(jnp.where(incl, gcs[i] - gc_ts[c][2 * NH + h:2 * NH + h + 1, :], -jnp.inf))
            for i, (c, h) in enumerate(pairs)]
    kbs = [ks[i] * betas[i] for i in idx]
    mats = tuple(jnp.where(strict, mm_nt(kbs[i], ks[i]) * gams[i], 0.0) for i in idx)
    aqks = [mm_nt(qs[i], ks[i]) * gams[i] for i in idx]
    if known_t is None:
        t_invs = _unit_lower_inverses(mats)
    else:
        t_invs = _known_inverses(mats, tuple(known_t[i] for i in idx))
    us = [mm_nn(t_invs[i], vs[i] * betas[i]) for i in idx]
    ws = [mm_nn(t_invs[i], kbs[i] * jnp.exp(gcs[i])) for i in idx]
    g_lasts = [gcs[i][LC - 1:LC, :] for i in idx]
    q_decs = [qs[i] * jnp.exp(gcs[i]) for i in idx]
    k_decs = [ks[i] * jnp.exp(g_lasts[i] - gcs[i]) for i in idx]
    e_lasts = [jnp.exp(g_lasts[i]) for i in idx]
    s_hs = [st[sl, :] for sl in sls]
    os_ = [None] * len(pairs)
    for c in range(n_ch):
        ids = [c * NH + h for h in range(NH)]
        v_news = [us[i] - mm_nn(ws[i], s_hs[h]) for h, i in enumerate(ids)]
        for h, i in enumerate(ids):
            os_[i] = mm_nn(q_decs[i], s_hs[h]) + mm_nn(aqks[i], v_news[h])
        s_hs = [s_hs[h] * e_lasts[i] + mm_tn(k_decs[i], v_news[h]) for h, i in enumerate(ids)]
    ys = [os_[i] * lax.rsqrt(jnp.mean(os_[i] * os_[i], axis=-1, keepdims=True) + EPS) * norm_w
          * jax.nn.silu(z[rows[c], sls[h]]) for i, (c, h) in enumerate(pairs)]
    y = jnp.concatenate([jnp.concatenate(ys[c * NH:(c + 1) * NH], axis=1) for c in range(n_ch)], axis=0)
    if known_t is None:
        return y, jnp.concatenate(s_hs, axis=0), jnp.stack(t_invs)
    return y, jnp.concatenate(s_hs, axis=0)


def rg_pre(xh8, x, conv_w, conv_b, gw_r, gw_i, gate_b, lam):
    xc = causal_conv(xh8, x, conv_w) + conv_b
    r = jax.nn.sigmoid(mm_nn(xc, gw_r) + gate_b[0:1])
    i = jax.nn.sigmoid(mm_nn(xc, gw_i) + gate_b[1:2])
    log_a = -RG_C * r * jax.nn.softplus(-lam)
    a = jnp.exp(log_a)
    th = jnp.tanh(log_a)
    one_minus_a2 = -2.0 * th / (1.0 - th)
    b = jnp.sqrt(one_minus_a2) * (i * xc)
    return a, b


def _scan_rows(a_ref, b_ref, out_ref, h0, n_rows, reverse):
    n_groups = n_rows // 8
    width = a_ref.shape[1]
    row = lax.broadcasted_iota(jnp.int32, (8, width), 0)

    def body(j, h):
        g = (n_groups - 1 - j) if reverse else j
        r0 = pl.multiple_of(g * 8, 8)
        av = a_ref[pl.ds(r0, 8), :]
        bv = b_ref[pl.ds(r0, 8), :]
        for d in (1, 2, 4):
            sh = (8 - d) if reverse else d
            a_s = pltpu.roll(av, sh, 0)
            b_s = pltpu.roll(bv, sh, 0)
            valid = (row < 8 - d) if reverse else (row >= d)
            bv = jnp.where(valid, av * b_s + bv, bv)
            av = jnp.where(valid, av * a_s, av)
        hv = av * h + bv
        out_ref[pl.ds(r0, 8), :] = hv
        return hv[0:1, :] if reverse else hv[7:8, :]

    return lax.fori_loop(0, n_groups, body, h0)


def _norm_rows(xv):
    r = lax.rsqrt(jnp.mean(xv * xv, axis=-1, keepdims=True) + EPS)
    return xv * r, r


def inproj(x, nw, w, xf=None):
    s, ts = x.shape[0], 256
    nt = s // ts
    n_in, n_out = 3, 2
    x_ops, x_in_specs, x_out_shapes, x_out_specs, aliases, x_sems, n_xin, n_land = _carried(xf, n_in, n_out)

    def body(*refs):
        (x_ref, nw_ref, w_ref), x_refs, (o_ref, ht_ref), land_refs, _, sem_refs = _cut_refs(refs, n_in, n_xin, n_out, n_land, 0)
        i = pl.program_id(0)
        _carry_open(xf, i, nt, x_refs, land_refs, sem_refs)
        xn, _ = _norm_rows(x_ref[...])
        hn = xn * nw_ref[...]
        o_ref[...] = jnp.dot(hn.astype(bf16), w_ref[...], preferred_element_type=f32)
        ht_ref[...] = hn.T.astype(bf16)
        _carry_close(xf, i, nt, x_refs, land_refs, sem_refs)

    return pl.pallas_call(
        body, grid=(nt,),
        in_specs=[pl.BlockSpec((ts, D), lambda i: (i, 0)), pl.BlockSpec((1, D), lambda i: (0, 0)),
                  pl.BlockSpec((D, DP), lambda i: (0, 0))] + x_in_specs,
        out_specs=[pl.BlockSpec((ts, DP), lambda i: (i, 0)), pl.BlockSpec((D, ts), lambda i: (0, i))] + x_out_specs,
        out_shape=[jax.ShapeDtypeStruct((s, DP), f32), jax.ShapeDtypeStruct((D, s), bf16)] + x_out_shapes,
        scratch_shapes=x_sems, input_output_aliases=aliases,
        name="inproj" if xf is None else "inproj_carrying", compiler_params=_cp(56))(x, nw, w, *x_ops)


def outproj(x, yr, ym, yg, wo):
    s, ts = x.shape[0], 256

    def body(x_ref, yr_ref, ym_ref, yg_ref, wo_ref, o_ref):
        acc = x_ref[...]
        for j, y_ref in enumerate((yr_ref, ym_ref, yg_ref)):
            acc = acc + jnp.dot(y_ref[...].astype(bf16), wo_ref[j * SEG:(j + 1) * SEG, :], preferred_element_type=f32)
        o_ref[...] = acc

    yspec = pl.BlockSpec((ts, SEG), lambda i: (i, 0))
    return pl.pallas_call(
        body, grid=(s // ts,),
        in_specs=[pl.BlockSpec((ts, D), lambda i: (i, 0)), yspec, yspec, yspec,
                  pl.BlockSpec((3 * SEG, D), lambda i: (0, 0))],
        out_specs=pl.BlockSpec((ts, D), lambda i: (i, 0)),
        out_shape=jax.ShapeDtypeStruct((s, D), f32), name="outproj", compiler_params=_cp(40))(x, yr, ym, yg, wo)


def head(x, fw, tgt):
    s, ts = x.shape[0], 256

    def body(x_ref, fw_ref, t_ref, dx_ref, loss_ref, dfw_ref):
        @pl.when(pl.program_id(0) == 0)
        def _():
            loss_ref[...] = jnp.zeros_like(loss_ref)
            dfw_ref[...] = jnp.zeros_like(dfw_ref)

        xn, r = _norm_rows(x_ref[...])
        fw_v = fw_ref[...]
        err = xn * fw_v - t_ref[...]
        loss_ref[...] += 0.5 * jnp.sum(jnp.mean(err * err, axis=-1, keepdims=True))
        dy = err * (1.0 / D)
        dfw_ref[0:1, :] += jnp.sum(dy * xn, axis=0, keepdims=True)
        dxn = dy * fw_v
        dx_ref[...] = r * (dxn - xn * jnp.mean(dxn * xn, axis=-1, keepdims=True))

    tile = pl.BlockSpec((ts, D), lambda i: (i, 0))
    return pl.pallas_call(
        body, grid=(s // ts,),
        in_specs=[tile, pl.BlockSpec((1, D), lambda i: (0, 0)), tile],
        out_specs=[tile, pl.BlockSpec((8, 128), lambda i: (0, 0)), pl.BlockSpec((8, D), lambda i: (0, 0))],
        out_shape=[jax.ShapeDtypeStruct((s, D), f32), jax.ShapeDtypeStruct((8, 128), f32),
                   jax.ShapeDtypeStruct((8, D), f32)], name="head")(x, fw, tgt)


def outproj_bwd(dx, yr, ym, yg, wo):
    s, ts = dx.shape[0], 256

    def body(dx_ref, yr_ref, ym_ref, yg_ref, wo_ref, dyr_ref, dym_ref, dyg_ref, gwo_ref):
        @pl.when(pl.program_id(0) == 0)
        def _():
            gwo_ref[...] = jnp.zeros_like(gwo_ref)

        dxb = dx_ref[...].astype(bf16)
        for j, (y_ref, dy_ref) in enumerate(((yr_ref, dyr_ref), (ym_ref, dym_ref), (yg_ref, dyg_ref))):
            rows = slice(j * SEG, (j + 1) * SEG)
            dy_ref[...] = lax.dot_general(dxb, wo_ref[rows, :], (((1,), (1,)), ((), ())), preferred_element_type=f32)
            gwo_ref[rows, :] += lax.dot_general(y_ref[...].astype(bf16), dxb, (((0,), (0,)), ((), ())),
                                                preferred_element_type=f32)

    yspec = pl.BlockSpec((ts, SEG), lambda i: (i, 0))
    wspec = pl.BlockSpec((3 * SEG, D), lambda i: (0, 0))
    ysh = jax.ShapeDtypeStruct((s, SEG), f32)
    return pl.pallas_call(
        body, grid=(s // ts,),
        in_specs=[pl.BlockSpec((ts, D), lambda i: (i, 0)), yspec, yspec, yspec, wspec],
        out_specs=[yspec, yspec, yspec, wspec],
        out_shape=[ysh, ysh, ysh, jax.ShapeDtypeStruct((3 * SEG, D), f32)],
        name="outproj_bwd", compiler_params=_cp(48))(dx, yr, ym, yg, wo)


def inproj_bwd_x(x, nw, w, dxo, d_rg, d_ml, d_gd, d_sa, d_sb):
    s, ts = x.shape[0], 256
    widths = (d_rg.shape[1], d_ml.shape[1], d_gd.shape[1], HD)

    def body(x_ref, nw_ref, w_ref, dxo_ref, rg_ref, ml_ref, gd_ref, sa_ref, sb_ref, dx_ref, dnw_ref):
        @pl.when(pl.program_id(0) == 0)
        def _():
            dnw_ref[...] = jnp.zeros_like(dnw_ref)

        xn, r = _norm_rows(x_ref[...])
        pieces = (rg_ref[...], ml_ref[...], gd_ref[...], sa_ref[...] + sb_ref[...])
        dhn = jnp.zeros((ts, D), f32)
        c0 = 0
        for piece, wd in zip(pieces, widths):
            dhn = dhn + lax.dot_general(piece.astype(bf16), w_ref[:, c0:c0 + wd], (((1,), (1,)), ((), ())),
                                        preferred_element_type=f32)
            c0 += wd
        dnw_ref[0:1, :] += jnp.sum(dhn * xn, axis=0, keepdims=True)
        dxn = dhn * nw_ref[...]
        dx_ref[...] = dxo_ref[...] + r * (dxn - xn * jnp.mean(dxn * xn, axis=-1, keepdims=True))

    tile = pl.BlockSpec((ts, D), lambda i: (i, 0))
    return pl.pallas_call(
        body, grid=(s // ts,),
        in_specs=[tile, pl.BlockSpec((1, D), lambda i: (0, 0)), pl.BlockSpec((D, DP), lambda i: (0, 0)), tile]
        + [pl.BlockSpec((ts, wd), lambda i: (i, 0)) for wd in widths] + [pl.BlockSpec((ts, HD), lambda i: (i, 0))],
        out_specs=[tile, pl.BlockSpec((8, D), lambda i: (0, 0))],
        out_shape=[jax.ShapeDtypeStruct((s, D), f32), jax.ShapeDtypeStruct((8, D), f32)],
        name="inproj_bwd_x", compiler_params=_cp(56))(x, nw, w, dxo, d_rg, d_ml, d_gd, d_sa, d_sb)


def wgrad(hn_t, dps, name):
    s = hn_t.shape[1]
    c = dps[0].shape[1]
    ct = min(c, SEG)
    n_dp = len(dps)

    def body(*refs):
        ht_ref = refs[0]
        dp_refs = refs[1:1 + n_dp]
        o_ref = refs[1 + n_dp]
        dp = dp_refs[0][...]
        for extra in dp_refs[1:]:
            dp = dp + extra[...]
        o_ref[...] = jnp.dot(ht_ref[...], dp.astype(bf16), preferred_element_type=f32).astype(bf16)

    return pl.pallas_call(
        body, grid=(c // ct,),
        in_specs=[pl.BlockSpec((D, s), lambda j: (0, 0))] + [pl.BlockSpec((s, ct), lambda j: (0, j)) for _ in dps],
        out_specs=pl.BlockSpec((D, ct), lambda j: (0, j)),
        out_shape=jax.ShapeDtypeStruct((D, c), bf16), name=name, compiler_params=_cp(40))(hn_t, *dps)


def _seg_spec(rows, seg, n_tiles=None):
    if n_tiles is None:
        return pl.BlockSpec((rows, SEG), lambda i: (i, seg))
    return pl.BlockSpec((rows, SEG), lambda i: (n_tiles - 1 - i, seg))


def _halo_spec(rows, seg, n_tiles=None):
    per = rows // 8
    if n_tiles is None:
        return pl.BlockSpec((8, SEG), lambda i: (jnp.maximum(i * per - 1, 0), seg))
    return pl.BlockSpec((8, SEG), lambda i: (jnp.maximum((n_tiles - 1 - i) * per - 1, 0), seg))


def _const_spec(shape):
    return pl.BlockSpec(shape, lambda i: tuple(0 for _ in shape))


def rglru_fwd(proj, conv_w, conv_b, gw_r, gw_i, gate_b, lam):
    s = proj.shape[0]
    tr = RG_TILE

    def body(xh_ref, x_ref, z_ref, cw_ref, cb_ref, gr_ref, gi_ref, gb_ref, lam_ref, y_ref, h_ref, a_s, b_s, hc):
        first = pl.program_id(0) == 0

        @pl.when(first)
        def _():
            hc[...] = jnp.zeros_like(hc)

        xh = jnp.where(first, 0.0, xh_ref[...])
        a, b = rg_pre(xh, x_ref[...], cw_ref[...], cb_ref[...], gr_ref[...], gi_ref[...], gb_ref[...], lam_ref[...])
        a_s[...] = a
        b_s[...] = b
        hc[0:1, :] = _scan_rows(a_s, b_s, h_ref, hc[0:1, :], tr, False)
        y_ref[...] = (h_ref[...] * jax.nn.silu(z_ref[...])).astype(bf16)

    out = pl.BlockSpec((tr, SEG), lambda i: (i, 0))
    return pl.pallas_call(
        body, grid=(s // tr,),
        in_specs=[_halo_spec(tr, 0), _seg_spec(tr, 0), _seg_spec(tr, 1), _const_spec((4, SEG)), _const_spec((1, SEG)),
                  _const_spec((SEG, SEG)), _const_spec((SEG, SEG)), _const_spec((2, SEG)), _const_spec((1, SEG))],
        out_specs=[out, out],
        out_shape=[jax.ShapeDtypeStruct((s, SEG), bf16), jax.ShapeDtypeStruct((s, SEG), f32)],
        scratch_shapes=[pltpu.VMEM((tr, SEG), f32), pltpu.VMEM((tr, SEG), f32), pltpu.VMEM((8, SEG), f32)],
        name="rglru_fwd", compiler_params=_cp(40))(proj, proj, proj, conv_w, conv_b, gw_r, gw_i, gate_b, lam)


def rglru_bwd(proj, hs, dy, conv_w, conv_b, gw_r, gw_i, gate_b, lam, xf=None):
    s = proj.shape[0]
    tr = RG_TILE
    nt = s // tr
    n_in, n_out = 12, 7
    x_ops, x_in_specs, x_out_shapes, x_out_specs, aliases, x_sems, n_xin, n_land = _carried(xf, n_in, n_out)

    def body(*refs):
        ins, x_refs, outs, land_refs, scratch, sem_refs = _cut_refs(refs, n_in, n_xin, n_out, n_land, 6)
        xh_ref, x_ref, z_ref, hh_ref, h_ref, dy_ref, cw_ref, cb_ref, gr_ref, gi_ref, gb_ref, lam_ref = ins
        dp_ref, dcw_ref, dcb_ref, dgr_ref, dgi_ref, dgb_ref, dlam_ref = outs
        an_s, g_s, dh_s, a_first, dh_first, dhalo = scratch
        i = pl.program_id(0)
        first_tile = i == nt - 1
        _carry_open(xf, i, nt, x_refs, land_refs, sem_refs)

        @pl.when(i == 0)
        def _():
            for ref in (dcw_ref, dcb_ref, dgr_ref, dgi_ref, dgb_ref, dlam_ref, a_first, dh_first, dhalo):
                ref[...] = jnp.zeros_like(ref)

        xh = jnp.where(first_tile, 0.0, xh_ref[...])
        params = (cw_ref[...], cb_ref[...], gr_ref[...], gi_ref[...], gb_ref[...], lam_ref[...])
        (a, _), vjp = jax.vjp(rg_pre, xh, x_ref[...], *params)
        zv = z_ref[...]
        hv = h_ref[...]
        dyv = dy_ref[...]
        sig = jax.nn.sigmoid(zv)
        g_s[...] = dyv * (zv * sig)
        dp_ref[:, SEG:2 * SEG] = (dyv * hv * (sig * (1.0 + zv * (1.0 - sig)))).astype(bf16)
        ridx = lax.broadcasted_iota(jnp.int32, (tr, SEG), 0)
        an_s[...] = jnp.where(ridx == tr - 1, jnp.broadcast_to(a_first[0:1, :], (tr, SEG)), pltpu.roll(a, tr - 1, 0))
        _scan_rows(an_s, g_s, dh_s, dh_first[0:1, :], tr, True)
        dh = dh_s[...]
        h_prev_last = jnp.where(first_tile, 0.0, hh_ref[...])[7:8, :]
        h_prev = pltpu.roll(hv, 1, 0)
        h_prev = jnp.where(ridx == 0, jnp.broadcast_to(h_prev_last, (tr, SEG)), h_prev)
        dxh, dx, dcw, dcb, dgr, dgi, dgb, dlam = vjp((dh * h_prev, dh))
        dx = dx + jnp.concatenate([jnp.zeros((tr - 8, SEG), f32), dhalo[...]], axis=0)
        dp_ref[:, 0:SEG] = dx.astype(bf16)
        dhalo[...] = dxh
        a_first[0:1, :] = a[0:1, :]
        dh_first[0:1, :] = dh[0:1, :]
        dcw_ref[...] += dcw
        dcb_ref[...] += dcb
        dgr_ref[...] += dgr
        dgi_ref[...] += dgi
        dgb_ref[...] += dgb
        dlam_ref[...] += dlam
        _carry_close(xf, i, nt, x_refs, land_refs, sem_refs)

    pspecs = [_const_spec((4, SEG)), _const_spec((1, SEG)), _const_spec((SEG, SEG)), _const_spec((SEG, SEG)),
              _const_spec((2, SEG)), _const_spec((1, SEG))]
    pshapes = [jax.ShapeDtypeStruct(sh, f32) for sh in ((4, SEG), (1, SEG), (SEG, SEG), (SEG, SEG), (2, SEG), (1, SEG))]
    tile = pl.BlockSpec((tr, SEG), lambda i: (nt - 1 - i, 0))
    return pl.pallas_call(
        body, grid=(nt,),
        in_specs=[_halo_spec(tr, 0, nt), _seg_spec(tr, 0, nt), _seg_spec(tr, 1, nt),
                  pl.BlockSpec((8, SEG), lambda i: (jnp.maximum((nt - 1 - i) * (tr // 8) - 1, 0), 0)), tile, tile] + pspecs
        + x_in_specs,
        out_specs=[pl.BlockSpec((tr, 2 * SEG), lambda i: (nt - 1 - i, 0))] + pspecs + x_out_specs,
        out_shape=[jax.ShapeDtypeStruct((s, 2 * SEG), bf16)] + pshapes + x_out_shapes,
        scratch_shapes=[pltpu.VMEM((tr, SEG), f32), pltpu.VMEM((tr, SEG), f32), pltpu.VMEM((tr, SEG), f32),
                        pltpu.VMEM((8, SEG), f32), pltpu.VMEM((8, SEG), f32), pltpu.VMEM((8, SEG), f32)] + x_sems,
        input_output_aliases=aliases, name="rglru_bwd" if xf is None else "rglru_bwd_carrying", compiler_params=_cp(48))(
            proj, proj, proj, hs, hs, dy, conv_w, conv_b, gw_r, gw_i, gate_b, lam, *x_ops)


ML_SEGS = (2, 3, 4, 5, 6)
SMALL_BLK = SMALL0 // HD


def _cut_refs(refs, n_in, n_xin, n_out, n_land, n_scratch):
    bounds = [0, n_in, n_in + n_xin, n_in + n_xin + n_out, n_in + n_xin + n_out + n_land,
              n_in + n_xin + n_out + n_land + n_scratch, len(refs)]
    return [refs[a:b] for a, b in zip(bounds[:-1], bounds[1:])]


def mlstm_fwd(proj, bias_row, norm_w, xf=None):
    s = proj.shape[0]
    nc = s // LR
    n_in, n_out = 8, 4
    x_ops, x_in_specs, x_out_shapes, x_out_specs, aliases, x_sems, n_xin, n_land = _carried(xf, n_in, n_out)

    def body(*refs):
        ins, x_refs, outs, land_refs, scratch, sem_refs = _cut_refs(refs, n_in, n_xin, n_out, n_land, 3)
        q_ref, k_ref, v_ref, o_ref, z_ref, sm_ref, b_ref, nw_ref = ins
        y_ref, cs_ref, ns_ref, ms_ref = outs
        c_s, n_s, m_s = scratch
        i = pl.program_id(0)
        _carry_open(xf, i, nc, x_refs, land_refs, sem_refs)

        @pl.when(i == 0)
        def _():
            c_s[...] = jnp.zeros_like(c_s)
            n_s[...] = jnp.zeros_like(n_s)
            m_s[...] = jnp.zeros_like(m_s)

        cs_ref[0] = c_s[...]
        ns_ref[0] = n_s[...]
        ms_ref[0] = m_s[...]
        y, c2, n2, m2 = ml_chunk(q_ref[...], k_ref[...], v_ref[...], o_ref[...], z_ref[...], sm_ref[...],
                                 b_ref[...], nw_ref[...], c_s[...], n_s[...], m_s[...])
        y_ref[...] = y.astype(bf16)
        c_s[...] = c2
        n_s[...] = n2
        m_s[...] = m2
        _carry_close(xf, i, nc, x_refs, land_refs, sem_refs)

    return pl.pallas_call(
        body, grid=(nc,),
        in_specs=[_seg_spec(LR, sg) for sg in ML_SEGS]
        + [pl.BlockSpec((LR, HD), lambda i: (i, SMALL_BLK)), _const_spec((1, HD)), _const_spec((1, SEG))] + x_in_specs,
        out_specs=[pl.BlockSpec((LR, SEG), lambda i: (i, 0)), pl.BlockSpec((1, SEG, HD), lambda i: (i, 0, 0)),
                   pl.BlockSpec((1, 8, HD), lambda i: (i, 0, 0)), pl.BlockSpec((1, 8, HD), lambda i: (i, 0, 0))] + x_out_specs,
        out_shape=[jax.ShapeDtypeStruct((s, SEG), bf16), jax.ShapeDtypeStruct((nc, SEG, HD), f32),
                   jax.ShapeDtypeStruct((nc, 8, HD), f32), jax.ShapeDtypeStruct((nc, 8, HD), f32)] + x_out_shapes,
        scratch_shapes=[pltpu.VMEM((SEG, HD), f32), pltpu.VMEM((8, HD), f32), pltpu.VMEM((8, HD), f32)] + x_sems,
        input_output_aliases=aliases, name="mlstm_fwd" if xf is None else "mlstm_fwd_carrying")(
            proj, proj, proj, proj, proj, proj, bias_row, norm_w, *x_ops)


def mlstm_bwd(proj, dy, cs, ns, ms, bias_row, norm_w, xf=None):
    s = proj.shape[0]
    nc = s // LR
    n_in, n_out = 12, 4
    x_ops, x_in_specs, x_out_shapes, x_out_specs, aliases, x_sems, n_xin, n_land = _carried(xf, n_in, n_out)

    def body(*refs):
        ins, x_refs, outs, land_refs, scratch, sem_refs = _cut_refs(refs, n_in, n_xin, n_out, n_land, 3)
        q_ref, k_ref, v_ref, o_ref, z_ref, sm_ref, dy_ref, cs_ref, ns_ref, ms_ref, b_ref, nw_ref = ins
        dp_ref, dsm_ref, db_ref, dnw_ref = outs
        dc_s, dn_s, dm_s = scratch
        i = pl.program_id(0)
        _carry_open(xf, i, nc, x_refs, land_refs, sem_refs)

        @pl.when(i == 0)
        def _():
            for ref in (db_ref, dnw_ref, dc_s, dn_s, dm_s):
                ref[...] = jnp.zeros_like(ref)

        _, vjp = jax.vjp(ml_chunk, q_ref[...], k_ref[...], v_ref[...], o_ref[...], z_ref[...], sm_ref[...],
                         b_ref[...], nw_ref[...], cs_ref[0], ns_ref[0], ms_ref[0])
        dq, dk, dv, do, dz, dsm, db, dnw, dc, dn, dm = vjp((dy_ref[...], dc_s[...], dn_s[...], dm_s[...]))
        for j, val in enumerate((dq, dk, dv, do, dz)):
            dp_ref[:, j * SEG:(j + 1) * SEG] = val.astype(bf16)
        dsm_ref[...] = dsm
        db_ref[0:1, :] += db
        dnw_ref[0:1, :] += dnw
        dc_s[...] = dc
        dn_s[...] = dn
        dm_s[...] = dm
        _carry_close(xf, i, nc, x_refs, land_refs, sem_refs)

    rev3 = lambda i: (nc - 1 - i, 0, 0)
    return pl.pallas_call(
        body, grid=(nc,),
        in_specs=[_seg_spec(LR, sg, nc) for sg in ML_SEGS]
        + [pl.BlockSpec((LR, HD), lambda i: (nc - 1 - i, SMALL_BLK)), pl.BlockSpec((LR, SEG), lambda i: (nc - 1 - i, 0)),
           pl.BlockSpec((1, SEG, HD), rev3), pl.BlockSpec((1, 8, HD), rev3), pl.BlockSpec((1, 8, HD), rev3),
           _const_spec((1, HD)), _const_spec((1, SEG))] + x_in_specs,
        out_specs=[pl.BlockSpec((LR, 5 * SEG), lambda i: (nc - 1 - i, 0)), pl.BlockSpec((LR, HD), lambda i: (nc - 1 - i, 0)),
                   _const_spec((8, HD)), _const_spec((8, SEG))] + x_out_specs,
        out_shape=[jax.ShapeDtypeStruct((s, 5 * SEG), bf16), jax.ShapeDtypeStruct((s, HD), f32),
                   jax.ShapeDtypeStruct((8, HD), f32), jax.ShapeDtypeStruct((8, SEG), f32)] + x_out_shapes,
        scratch_shapes=[pltpu.VMEM((SEG, HD), f32), pltpu.VMEM((8, HD), f32), pltpu.VMEM((8, HD), f32)] + x_sems,
        input_output_aliases=aliases, name="mlstm_bwd" if xf is None else "mlstm_bwd_carrying", compiler_params=_cp(48))(
            proj, proj, proj, proj, proj, proj, dy, cs, ns, ms, bias_row, norm_w, *x_ops)


GD_SEGS = (7, 8, 9)


def _carried(xf, n_in, n_out):
    operands, out_shapes, aliases, sems, _, n_xin, n_land = _carry_plumb(xf, n_in, n_out)
    return operands, [_ANY] * n_xin, out_shapes, [_ANY] * n_land, aliases, sems, n_xin, n_land


def _carry_open(xf, i, n_steps, x_refs, land_refs, sem_refs):
    if xf is None:
        return
    srcs = x_refs[:len(xf.srcs)]

    @pl.when(i == 0)
    def _():
        _carry_start(xf, srcs, land_refs, sem_refs)

    @pl.when(i == n_steps - 1)
    def _():
        _carry_middle(xf, srcs, land_refs, sem_refs)


def _carry_close(xf, i, n_steps, x_refs, land_refs, sem_refs):
    if xf is None:
        return

    @pl.when(i == n_steps - 1)
    def _():
        _carry_finish(xf, x_refs[:len(xf.srcs)], land_refs, sem_refs)


def gdn_fwd(proj, conv_w, alog_row, dt_row, norm_w, xf=None):
    s = proj.shape[0]
    nc = s // LR
    n_in, n_out = 12, 3
    x_ops, x_in_specs, x_out_shapes, x_out_specs, aliases, x_sems, n_xin, n_land = _carried(xf, n_in, n_out)

    def body(*refs):
        qh_ref, q_ref, kh_ref, k_ref, vh_ref, v_ref, z_ref, sm_ref, cw_ref, al_ref, dt_ref, nw_ref = refs[:n_in]
        x_refs = refs[n_in:n_in + n_xin]
        y_ref, ss_ref, ti_ref = refs[n_in + n_xin:n_in + n_xin + n_out]
        land_refs = refs[n_in + n_xin + n_out:n_in + n_xin + n_out + n_land]
        st_s = refs[n_in + n_xin + n_out + n_land]
        sem_refs = refs[n_in + n_xin + n_out + n_land + 1:]
        i = pl.program_id(0)
        first = i == 0
        _carry_open(xf, i, nc, x_refs, land_refs, sem_refs)

        @pl.when(first)
        def _():
            st_s[...] = jnp.zeros_like(st_s)

        ss_ref[0] = st_s[...]
        halo = [jnp.where(first, 0.0, r[...]) for r in (qh_ref, kh_ref, vh_ref)]
        y, st2, t_invs = gd_chunk(halo[0], q_ref[...], halo[1], k_ref[...], halo[2], v_ref[...], z_ref[...], sm_ref[...],
                                  cw_ref[...], al_ref[...], dt_ref[...], nw_ref[...], st_s[...])
        y_ref[...] = y.astype(bf16)
        ti_ref[...] = t_invs
        st_s[...] = st2
        _carry_close(xf, i, nc, x_refs, land_refs, sem_refs)

    qkv_specs = []
    for sg in GD_SEGS:
        qkv_specs += [_halo_spec(LR, sg), _seg_spec(LR, sg)]
    return pl.pallas_call(
        body, grid=(nc,),
        in_specs=qkv_specs + [_seg_spec(LR, 10), pl.BlockSpec((LR, HD), lambda i: (i, SMALL_BLK)),
                              _const_spec((4, 3 * SEG)), _const_spec((1, HD)), _const_spec((1, HD)), _const_spec((1, HD))]
        + x_in_specs,
        out_specs=[pl.BlockSpec((LR, SEG), lambda i: (i, 0)), pl.BlockSpec((1, SEG, HD), lambda i: (i, 0, 0)),
                   pl.BlockSpec((RUN * NH, LC, LC), lambda i: (i, 0, 0))] + x_out_specs,
        out_shape=[jax.ShapeDtypeStruct((s, SEG), bf16), jax.ShapeDtypeStruct((nc, SEG, HD), f32),
                   jax.ShapeDtypeStruct((s // LC * NH, LC, LC), f32)] + x_out_shapes,
        scratch_shapes=[pltpu.VMEM((SEG, HD), f32)] + x_sems, input_output_aliases=aliases,
        name="gdn_fwd" if xf is None else "gdn_fwd_carrying")(
            proj, proj, proj, proj, proj, proj, proj, proj, conv_w, alog_row, dt_row, norm_w, *x_ops)


def gdn_bwd(proj, dy, ss, t_invs, conv_w, alog_row, dt_row, norm_w, xf=None):
    s = proj.shape[0]
    nc = s // LR
    n_in, n_out = 15, 6
    x_ops, x_in_specs, x_out_shapes, x_out_specs, aliases, x_sems, n_xin, n_land = _carried(xf, n_in, n_out)

    def body(*refs):
        (qh_ref, q_ref, kh_ref, k_ref, vh_ref, v_ref, z_ref, sm_ref, dy_ref, ss_ref, ti_ref,
         cw_ref, al_ref, dt_ref, nw_ref) = refs[:n_in]
        x_refs = refs[n_in:n_in + n_xin]
        dp_ref, dsm_ref, dcw_ref, dal_ref, ddt_ref, dnw_ref = refs[n_in + n_xin:n_in + n_xin + n_out]
        land_refs = refs[n_in + n_xin + n_out:n_in + n_xin + n_out + n_land]
        dst_s, dhalo = refs[n_in + n_xin + n_out + n_land:n_in + n_xin + n_out + n_land + 2]
        sem_refs = refs[n_in + n_xin + n_out + n_land + 2:]
        i = pl.program_id(0)
        first_chunk = i == nc - 1
        _carry_open(xf, i, nc, x_refs, land_refs, sem_refs)

        @pl.when(i == 0)
        def _():
            for ref in (dcw_ref, dal_ref, ddt_ref, dnw_ref, dst_s, dhalo):
                ref[...] = jnp.zeros_like(ref)

        halo = [jnp.where(first_chunk, 0.0, r[...]) for r in (qh_ref, kh_ref, vh_ref)]
        with_known = functools.partial(gd_chunk, known_t=ti_ref[...])
        _, vjp = jax.vjp(with_known, halo[0], q_ref[...], halo[1], k_ref[...], halo[2], v_ref[...], z_ref[...], sm_ref[...],
                         cw_ref[...], al_ref[...], dt_ref[...], nw_ref[...], ss_ref[0])
        dqh, dq, dkh, dk, dvh, dv, dz, dsm, dcw, dal, ddt, dnw, dst = vjp((dy_ref[...], dst_s[...]))
        for j, val in enumerate((dq, dk, dv)):
            val = val + jnp.concatenate([jnp.zeros((LR - 8, SEG), f32), dhalo[:, j * SEG:(j + 1) * SEG]], axis=0)
            dp_ref[:, j * SEG:(j + 1) * SEG] = val.astype(bf16)
        dp_ref[:, 3 * SEG:4 * SEG] = dz.astype(bf16)
        for j, val in enumerate((dqh, dkh, dvh)):
            dhalo[:, j * SEG:(j + 1) * SEG] = val
        dsm_ref[...] = dsm
        dcw_ref[...] += dcw
        dal_ref[0:1, :] += dal
        ddt_ref[0:1, :] += ddt
        dnw_ref[0:1, :] += dnw
        dst_s[...] = dst
        _carry_close(xf, i, nc, x_refs, land_refs, sem_refs)

    qkv_specs = []
    for sg in GD_SEGS:
        qkv_specs += [_halo_spec(LR, sg, nc), _seg_spec(LR, sg, nc)]
    return pl.pallas_call(
        body, grid=(nc,),
        in_specs=qkv_specs + [_seg_spec(LR, 10, nc), pl.BlockSpec((LR, HD), lambda i: (nc - 1 - i, SMALL_BLK)),
                              pl.BlockSpec((LR, SEG), lambda i: (nc - 1 - i, 0)),
                              pl.BlockSpec((1, SEG, HD), lambda i: (nc - 1 - i, 0, 0)),
                              pl.BlockSpec((RUN * NH, LC, LC), lambda i: (nc - 1 - i, 0, 0)),
                              _const_spec((4, 3 * SEG)), _const_spec((1, HD)), _const_spec((1, HD)), _const_spec((1, HD))]
        + x_in_specs,
        out_specs=[pl.BlockSpec((LR, 4 * SEG), lambda i: (nc - 1 - i, 0)), pl.BlockSpec((LR, HD), lambda i: (nc - 1 - i, 0)),
                   _const_spec((4, 3 * SEG)), _const_spec((8, HD)), _const_spec((8, HD)), _const_spec((8, HD))] + x_out_specs,
        out_shape=[jax.ShapeDtypeStruct((s, 4 * SEG), bf16), jax.ShapeDtypeStruct((s, HD), f32),
                   jax.ShapeDtypeStruct((4, 3 * SEG), f32), jax.ShapeDtypeStruct((8, HD), f32),
                   jax.ShapeDtypeStruct((8, HD), f32), jax.ShapeDtypeStruct((8, HD), f32)] + x_out_shapes,
        scratch_shapes=[pltpu.VMEM((SEG, HD), f32), pltpu.VMEM((8, 3 * SEG), f32)] + x_sems, input_output_aliases=aliases,
        name="gdn_bwd" if xf is None else "gdn_bwd_carrying", compiler_params=_cp(48))(
            proj, proj, proj, proj, proj, proj, proj, proj, dy, ss, t_invs, conv_w, alog_row, dt_row, norm_w, *x_ops)


def _my_place():
    return lax.axis_index("x"), lax.axis_index("y"), lax.axis_index("c")


def _slot(p):
    return 4 * p[0] + 2 * p[1] + p[2]


def _peer(me, j):
    flips = ((j >> 2) & 1, (j >> 1) & 1, j & 1)
    return tuple((1 - v) if fl else v for v, fl in zip(me, flips))


_ANY = pl.BlockSpec(memory_space=pl.ANY)


class Scatter(NamedTuple):
    srcs: list
    lands: list
    land_of: list
    layer: list
    span: list


class Gather2(NamedTuple):
    srcs: list
    lands: list
    span: list


def _carry_plumb(cx, n_in, n_out):
    if cx is None:
        return [], [], {}, [], 0, 0, 0
    n_src = len(cx.srcs)
    passed = [li for li, ld in enumerate(cx.lands) if not isinstance(ld, jax.ShapeDtypeStruct)]
    operands = list(cx.srcs) + [cx.lands[li] for li in passed]
    aliases = {n_in + n_src + k: n_out + li for k, li in enumerate(passed)}
    out_shapes = [jax.ShapeDtypeStruct(ld.shape, ld.dtype) for ld in cx.lands]
    sems = [pltpu.SemaphoreType.DMA((n_src, N_DEV - 1)), pltpu.SemaphoreType.DMA((n_src, N_DEV - 1)),
            pltpu.SemaphoreType.DMA((n_src,))]
    return operands, out_shapes, aliases, sems, n_src, len(operands), len(cx.lands)


def _scatter_copies(sc, src_refs, land_refs, send_sems, recv_sems, local_sems):
    me = _my_place()
    mine = _slot(me)
    local, remote = [], []
    for a, src_ref in enumerate(src_refs):
        lo, hi = sc.span[a]
        rows = pl.ds(lo, hi - lo)
        dst = land_refs[sc.land_of[a]].at[mine, sc.layer[a], rows]
        local.append(pltpu.make_async_copy(src_ref.at[mine, rows], dst, local_sems.at[a]))
        for j in range(1, N_DEV):
            peer = _peer(me, j)
            remote.append(pltpu.make_async_remote_copy(
                src_ref=src_ref.at[_slot(peer), rows], dst_ref=dst, send_sem=send_sems.at[a, j - 1],
                recv_sem=recv_sems.at[a, j - 1], device_id=peer, device_id_type=MESH))
    return local, remote


def _gather2_copy(land, sems, a, k, block_of, to, rows, src=None):
    dst = land.at[_slot(block_of), rows]
    return pltpu.make_async_remote_copy(src_ref=dst if src is None else src, dst_ref=dst, send_sem=sems[0].at[a, k],
                                        recv_sem=sems[1].at[a, k], device_id=to, device_id_type=MESH)


def _gather2_places():
    x, y, c = _my_place()
    return (x, y, c), (x, y, 1 - c), [(1 - x, y), (x, 1 - y), (1 - x, 1 - y)], c


def _carry_start(cx, src_refs, land_refs, sems):
    if isinstance(cx, Scatter):
        local, remote = _scatter_copies(cx, src_refs, land_refs, *sems)
        for cp in local + remote:
            cp.start()
        return
    me, sib, chips, c = _gather2_places()
    for a, (src_ref, land) in enumerate(zip(src_refs, land_refs)):
        rows = pl.ds(cx.span[a][0], cx.span[a][1] - cx.span[a][0])
        src = src_ref.at[rows]
        pltpu.make_async_copy(src, land.at[_slot(me), rows], sems[2].at[a]).start()
        _gather2_copy(land, sems, a, 0, me, sib, rows, src=src).start()
        for j, chip in enumerate(chips):
            _gather2_copy(land, sems, a, 1 + j, me, (*chip, c), rows, src=src).start()


def _carry_middle(cx, src_refs, land_refs, sems):
    if isinstance(cx, Scatter):
        return
    me, sib, chips, c = _gather2_places()
    for a, land in enumerate(land_refs):
        rows = pl.ds(cx.span[a][0], cx.span[a][1] - cx.span[a][0])
        for j, chip in enumerate(chips):
            _gather2_copy(land, sems, a, 1 + j, (*chip, c), me, rows).wait_recv()
            _gather2_copy(land, sems, a, 4 + j, (*chip, c), sib, rows).start()


def _carry_finish(cx, src_refs, land_refs, sems):
    if isinstance(cx, Scatter):
        local, remote = _scatter_copies(cx, src_refs, land_refs, *sems)
        for cp in remote:
            cp.wait()
        for cp in local:
            cp.wait()
        return
    me, sib, chips, c = _gather2_places()
    for a, (src_ref, land) in enumerate(zip(src_refs, land_refs)):
        rows = pl.ds(cx.span[a][0], cx.span[a][1] - cx.span[a][0])
        src = src_ref.at[rows]
        _gather2_copy(land, sems, a, 0, sib, me, rows).wait_recv()
        for j, chip in enumerate(chips):
            _gather2_copy(land, sems, a, 4 + j, (*chip, 1 - c), me, rows).wait_recv()
        _gather2_copy(land, sems, a, 0, me, sib, rows, src=src).wait_send()
        for j, chip in enumerate(chips):
            _gather2_copy(land, sems, a, 1 + j, me, (*chip, c), rows, src=src).wait_send()
            _gather2_copy(land, sems, a, 4 + j, (*chip, c), sib, rows).wait_send()
        pltpu.make_async_copy(src, land.at[_slot(me), rows], sems[2].at[a]).wait()


def exchange(cx, name):
    operands, out_shapes, aliases, sems, n_src, n_xin, n_land = _carry_plumb(cx, 0, 0)

    def body(*refs):
        src_refs, land_refs, sem_refs = refs[:n_src], refs[n_xin:n_xin + n_land], refs[n_xin + n_land:]
        _carry_start(cx, src_refs, land_refs, sem_refs)
        _carry_middle(cx, src_refs, land_refs, sem_refs)
        _carry_finish(cx, src_refs, land_refs, sem_refs)

    return pl.pallas_call(body, in_specs=[_ANY] * n_xin, out_specs=[_ANY] * n_land, out_shape=out_shapes,
                          scratch_shapes=sems, input_output_aliases=aliases, name=name)(*operands)


def small_allreduce(packed, name, xf=None):
    r = packed.shape[0]
    rc = r // N_DEV
    x_ops, x_in_specs, x_out_shapes, x_out_specs, aliases, x_sems, n_xin, n_land = _carried(xf, 1, 1)

    def body(*refs):
        (in_ref,), x_refs, (out_ref,), land_refs, scratch, sem_refs = _cut_refs(refs, 1, n_xin, 1, n_land, 5)
        recv_buf, send1, recv1, send2, recv2 = scratch
        if xf is not None:
            _carry_start(xf, x_refs[:len(xf.srcs)], land_refs, sem_refs)
        me = _my_place()
        mine = _slot(me)
        my_rows = pl.ds(pl.multiple_of(mine * rc, 8), rc)
        first, second = [], []
        for j in range(1, N_DEV):
            peer = _peer(me, j)
            peer_rows = pl.ds(pl.multiple_of(_slot(peer) * rc, 8), rc)
            first.append(pltpu.make_async_remote_copy(
                src_ref=in_ref.at[peer_rows], dst_ref=recv_buf.at[mine], send_sem=send1.at[j - 1], recv_sem=recv1.at[j - 1],
                device_id=peer, device_id_type=MESH))
            second.append(pltpu.make_async_remote_copy(
                src_ref=out_ref.at[my_rows], dst_ref=out_ref.at[my_rows], send_sem=send2.at[j - 1], recv_sem=recv2.at[j - 1],
                device_id=peer, device_id_type=MESH))
        for cp in first:
            cp.start()
        recv_buf[mine] = in_ref[my_rows]
        for cp in first:
            cp.wait()
        acc = recv_buf[0]
        for k in range(1, N_DEV):
            acc = acc + recv_buf[k]
        out_ref[my_rows] = acc
        for cp in second:
            cp.start()
        for cp in second:
            cp.wait()
        if xf is not None:
            _carry_finish(xf, x_refs[:len(xf.srcs)], land_refs, sem_refs)

    vmem = pl.BlockSpec(memory_space=pltpu.VMEM)
    return pl.pallas_call(
        body, in_specs=[vmem] + x_in_specs, out_specs=[vmem] + x_out_specs,
        out_shape=[jax.ShapeDtypeStruct((r, 128), f32)] + x_out_shapes,
        scratch_shapes=[pltpu.VMEM((N_DEV, rc, 128), f32)] + [pltpu.SemaphoreType.DMA((N_DEV - 1,))] * 4 + x_sems,
        input_output_aliases=aliases, name=name, compiler_params=_cp(32))(packed, *x_ops)


def _adamw(w, g, m, v):
    m = ADAM_B1 * m + (1.0 - ADAM_B1) * g
    v = ADAM_B2 * v + (1.0 - ADAM_B2) * (g * g)
    m_hat = m / (1.0 - ADAM_B1 ** ADAM_STEP)
    v_hat = v / (1.0 - ADAM_B2 ** ADAM_STEP)
    delta = -ADAM_LR * (m_hat / (jnp.sqrt(v_hat) + ADAM_EPS) + ADAM_WD * w)
    return delta, m, v


def adam_slots(slots, w, m, v, rows, name):
    _, depth, r, c = slots.shape

    def body(s_ref, w_ref, m_ref, v_ref, g_ref, d_ref, m2_ref, v2_ref):
        g = s_ref[0, 0].astype(f32)
        for k in range(1, N_DEV):
            g = g + s_ref[k, 0].astype(f32)
        d, m2, v2 = _adamw(w_ref[0], g, m_ref[0], v_ref[0])
        g_ref[0] = g
        d_ref[0] = d
        m2_ref[0] = m2
        v2_ref[0] = v2

    blk = pl.BlockSpec((1, rows, c), lambda l, i: (l, i, 0))
    sh = jax.ShapeDtypeStruct((depth, r, c), f32)
    return pl.pallas_call(
        body, grid=(depth, r // rows),
        in_specs=[pl.BlockSpec((N_DEV, 1, rows, c), lambda l, i: (0, l, i, 0)), blk, blk, blk],
        out_specs=[blk] * 4, out_shape=[sh] * 4, name=name, compiler_params=_cp(40))(slots, w, m, v)


def adam_flat(g, w, m, v, name):
    def body(g_ref, w_ref, m_ref, v_ref, d_ref, m2_ref, v2_ref):
        d, m2, v2 = _adamw(w_ref[...], g_ref[...], m_ref[...], v_ref[...])
        d_ref[...] = d
        m2_ref[...] = m2
        v2_ref[...] = v2

    sh = jax.ShapeDtypeStruct(g.shape, f32)
    return pl.pallas_call(body, out_shape=[sh] * 3, name=name)(g, w, m, v)


def _rows_of(shape):
    n = 1
    for dim in shape:
        n *= dim
    return n, -(-n // (8 * 128)) * 8


def _pack(arrs):
    parts = []
    for a in arrs:
        n, rows = _rows_of(a.shape)
        parts.append(jnp.pad(a.reshape(-1).astype(f32), (0, rows * 128 - n)).reshape(rows, 128))
    return jnp.concatenate(parts, axis=0)


def _unpack(packed, shapes):
    out, row = [], 0
    for sh in shapes:
        n, rows = _rows_of(sh)
        out.append(packed[row:row + rows].reshape(-1)[:n].reshape(sh))
        row += rows
    return out


def _block_diag(gw):
    eye = jnp.eye(8, dtype=gw.dtype)
    return (gw[..., :, :, None, :] * eye[:, None, :, None]).reshape(gw.shape[:-3] + (SEG, SEG))


def _diag_blocks(dense):
    eye = jnp.eye(8, dtype=dense.dtype)
    return (dense.reshape(8, 64, 8, 64) * eye[:, None, :, None]).sum(axis=2)


def _lane_rows(vals, first_lane):
    return jnp.zeros((vals.shape[0], 1, HD), f32).at[:, 0, first_lane:first_lane + vals.shape[1]].set(vals)


def kernel(x, norm_w, w_in, rg_conv_w, rg_conv_b, rg_gate_w, rg_gate_b, rg_lambda, ml_gate_b, ml_norm_w, gd_conv_w, gd_a_log, gd_dt_bias, gd_norm_w, w_out, final_norm_w, loss_target, m_norm_w, m_w_in, m_rg_conv_w, m_rg_conv_b, m_rg_gate_w, m_rg_gate_b, m_rg_lambda, m_ml_gate_b, m_ml_norm_w, m_gd_conv_w, m_gd_a_log, m_gd_dt_bias, m_gd_norm_w, m_w_out, m_final_norm_w, v_norm_w, v_w_in, v_rg_conv_w, v_rg_conv_b, v_rg_gate_w, v_rg_gate_b, v_rg_lambda, v_ml_gate_b, v_ml_norm_w, v_gd_conv_w, v_gd_a_log, v_gd_dt_bias, v_gd_norm_w, v_w_out, v_final_norm_w):
    s = x.shape[1]
    xs = x.reshape(s, D)
    tgt = loss_target.reshape(s, D)
    me = 4 * lax.axis_index("x") + 2 * lax.axis_index("y") + lax.axis_index("c")

    comm = MeshComm(w_in, w_out, [rg_conv_w, rg_gate_b, gd_conv_w])
    rg_conv_full, rg_gb_full, gd_conv_full = comm.small_weights
    loss_part, dx, d_fw, g_small = local_step(
        xs, tgt, comm, rg_conv_full, rg_gb_full, gd_conv_full, norm_w, rg_conv_b, rg_gate_w, rg_lambda,
        ml_gate_b, ml_norm_w, gd_a_log, gd_dt_bias, gd_norm_w, final_norm_w)
    given_w = dict(norm_w=norm_w, rg_conv_w=rg_conv_w, rg_conv_b=rg_conv_b, rg_gate_w=rg_gate_w, rg_gate_b=rg_gate_b,
                   rg_lambda=rg_lambda, ml_gate_b=ml_gate_b, ml_norm_w=ml_norm_w, gd_conv_w=gd_conv_w, gd_a_log=gd_a_log,
                   gd_dt_bias=gd_dt_bias, gd_norm_w=gd_norm_w, final_norm_w=final_norm_w, w_in=w_in, w_out=w_out)
    given_m = dict(norm_w=m_norm_w, rg_conv_w=m_rg_conv_w, rg_conv_b=m_rg_conv_b, rg_gate_w=m_rg_gate_w, rg_gate_b=m_rg_gate_b,
                   rg_lambda=m_rg_lambda, ml_gate_b=m_ml_gate_b, ml_norm_w=m_ml_norm_w, gd_conv_w=m_gd_conv_w,
                   gd_a_log=m_gd_a_log, gd_dt_bias=m_gd_dt_bias, gd_norm_w=m_gd_norm_w, final_norm_w=m_final_norm_w,
                   w_in=m_w_in, w_out=m_w_out)
    given_v = dict(norm_w=v_norm_w, rg_conv_w=v_rg_conv_w, rg_conv_b=v_rg_conv_b, rg_gate_w=v_rg_gate_w, rg_gate_b=v_rg_gate_b,
                   rg_lambda=v_rg_lambda, ml_gate_b=v_ml_gate_b, ml_norm_w=v_ml_norm_w, gd_conv_w=v_gd_conv_w,
                   gd_a_log=v_gd_a_log, gd_dt_bias=v_gd_dt_bias, gd_norm_w=v_gd_norm_w, final_norm_w=v_final_norm_w,
                   w_in=v_w_in, w_out=v_w_out)
    return finish_step(loss_part, dx, d_fw, g_small, comm, s, me, given_w, given_m, given_v)


def _gathered_pieces():
    per = D_IN // N_DEV
    pieces = []
    for lo, hi in ((0, 3584), (3592, 5640), (3584, 3592), (5640, 5648)):
        col = lo
        while col < hi:
            k = col // per
            end = min(hi, (k + 1) * per)
            pieces.append((k, col - k * per, end - k * per))
            col = end
    return pieces


def regroup_w_in(wi_g):
    rows = 256
    pieces = _gathered_pieces()

    def body(x_ref, o_ref):
        cols = [x_ref[k, :, a:b] for k, a, b in pieces] + [jnp.zeros((rows, DP - D_IN), wi_g.dtype)]
        o_ref[...] = jnp.concatenate(cols, axis=1)

    return pl.pallas_call(
        body, grid=(D // rows,), in_specs=[pl.BlockSpec((N_DEV, rows, D_IN // N_DEV), lambda i: (0, i, 0))],
        out_specs=pl.BlockSpec((rows, DP), lambda i: (i, 0)), out_shape=jax.ShapeDtypeStruct((D, DP), wi_g.dtype),
        name="regroup_w_in")(wi_g)


def slots_of_w_in_grad(gw_rg, gw_ml, gw_gd, gw_sm):
    rows = 256
    per = D_IN // N_DEV

    def body(rg_ref, ml_ref, gd_ref, sm_ref, o_ref):
        sm = sm_ref[...]
        g = jnp.concatenate([rg_ref[...], ml_ref[...], sm[:, 0:2 * NH], gd_ref[...], sm[:, 2 * NH:4 * NH]], axis=1)
        for k in range(N_DEV):
            o_ref[k] = g[:, k * per:(k + 1) * per]

    return pl.pallas_call(
        body, grid=(D // rows,),
        in_specs=[pl.BlockSpec((rows, a.shape[1]), lambda i: (i, 0)) for a in (gw_rg, gw_ml, gw_gd, gw_sm)],
        out_specs=pl.BlockSpec((N_DEV, rows, per), lambda i: (0, i, 0)),
        out_shape=jax.ShapeDtypeStruct((N_DEV, D, per), gw_rg.dtype), name="slots_of_w_in_grad",
        compiler_params=_cp(40))(gw_rg, gw_ml, gw_gd, gw_sm)


class MeshComm:
    GWI_SPLIT = 448
    WI_SPLIT = 384

    def __init__(self, w_in, w_out, small_shards):
        per = D_IN // N_DEV
        self.wi_sh = [w_in[l].astype(bf16) for l in range(DEPTH)]
        self.wo_sh = [w_out[l].astype(bf16) for l in range(DEPTH)]
        self.wi_land = jax.ShapeDtypeStruct((N_DEV, D, per), bf16)
        self.wo_land = jax.ShapeDtypeStruct((N_DEV, 3 * SEG // N_DEV, D), bf16)
        packed = _pack(small_shards)
        first = Gather2([self.wi_sh[0], self.wo_sh[0], packed],
                        [self.wi_land, self.wo_land, jax.ShapeDtypeStruct((N_DEV,) + packed.shape, f32)],
                        [(0, D), (0, 3 * SEG // N_DEV), (0, packed.shape[0])])
        wi_g, wo_g, sm_g = exchange(first, "gather_first")
        self.wi_g, self.wo_g = {0: wi_g}, {0: wo_g}
        shapes = [a.shape for a in small_shards]
        parts = [_unpack(sm_g[k], shapes) for k in range(N_DEV)]
        self.small_weights = [jnp.concatenate([p[j] for p in parts], axis=-1) for j in range(len(small_shards))]
        self.gwi_land = lax.empty((N_DEV, DEPTH, D, per), bf16)
        self.gwo_land = lax.empty((N_DEV, DEPTH, 3 * SEG // N_DEV, D), bf16)
        self.gwi_slots = {}
        self.gwo_slots = {}

    def weights(self, l):
        return regroup_w_in(self.wi_g[l]), self.wo_g[l].reshape(3 * SEG, D)

    def fwd_carry(self, l, host):
        if l + 1 >= DEPTH:
            return None
        if host == "mlstm":
            return Gather2([self.wo_sh[l + 1]], [self.wo_land], [(0, 3 * SEG // N_DEV)])
        if host == "inproj":
            return Gather2([self.wi_sh[l + 1]], [self.wi_land], [(0, self.WI_SPLIT)])
        return Gather2([self.wi_sh[l + 1]], [self.wi_g[l + 1]], [(self.WI_SPLIT, D)])

    def fwd_landed(self, l, host, landed):
        (self.wo_g if host == "mlstm" else self.wi_g)[l + 1] = landed[0]

    def own_w_out_grad(self, l, g_wo):
        self.gwo_slots[l] = g_wo.reshape(N_DEV, 3 * SEG // N_DEV, D).astype(bf16)

    def bwd_carry(self, l, host):
        rows_o = 3 * SEG // N_DEV
        srcs, land_of, layer, span = [], [], [], []
        if l + 1 < DEPTH:
            if host == "rglru":
                srcs, land_of, layer, span = [self.gwo_slots[l + 1]], [1], [l + 1], [(0, rows_o)]
            elif host == "mlstm":
                srcs, land_of, layer, span = [self.gwi_slots[l + 1]], [0], [l + 1], [(0, self.GWI_SPLIT)]
            else:
                srcs, land_of, layer, span = [self.gwi_slots[l + 1]], [0], [l + 1], [(self.GWI_SPLIT, D)]
        if l == 0 and host == "mlstm":
            srcs, land_of, layer, span = srcs + [self.gwo_slots[0]], land_of + [1], layer + [0], span + [(0, rows_o)]
        if not srcs:
            return None
        return Scatter(srcs, [self.gwi_land, self.gwo_land], land_of, layer, span)

    def bwd_landed(self, landed):
        self.gwi_land, self.gwo_land = landed

    def grads_ready(self, l, gw_rg, gw_ml, gw_gd, gw_sm):
        self.gwi_slots[l] = slots_of_w_in_grad(gw_rg, gw_ml, gw_gd, gw_sm)

    def last_carry(self):
        return Scatter([self.gwi_slots[0]], [self.gwi_land, self.gwo_land], [0], [0], [(0, D)])


def local_step(xs, tgt, comm, rg_conv_full, rg_gb_full, gd_conv_full, norm_w, rg_conv_b, rg_gate_w,
               rg_lambda, ml_gate_b, ml_norm_w, gd_a_log, gd_dt_bias, gd_norm_w, final_norm_w):
    gate_dense = _block_diag(rg_gate_w)
    ml_bias_rows = _lane_rows(ml_gate_b.reshape(DEPTH, 2 * NH), 0)
    alog_rows = _lane_rows(gd_a_log, 2 * NH)
    dt_rows = _lane_rows(gd_dt_bias, 2 * NH)
    acts = []
    for l in range(DEPTH):
        nw = norm_w[l].reshape(1, D)
        w_in_l, w_out_l = comm.weights(l)
        xf = comm.fwd_carry(l, "inproj")
        proj, hn_t, *landed = inproj(xs, nw, w_in_l, xf=xf)
        if xf is not None:
            comm.fwd_landed(l, "inproj", landed)
        rg_p = (rg_conv_full[l], rg_conv_b[l].reshape(1, SEG), gate_dense[l, 0], gate_dense[l, 1],
                rg_gb_full[l], rg_lambda[l].reshape(1, SEG))
        y_rg, hs = rglru_fwd(proj, *rg_p)
        ml_p = (ml_bias_rows[l], ml_norm_w[l].reshape(1, SEG))
        xf = comm.fwd_carry(l, "mlstm")
        y_ml, cs, ns, ms, *landed = mlstm_fwd(proj, *ml_p, xf=xf)
        if xf is not None:
            comm.fwd_landed(l, "mlstm", landed)
        gd_p = (gd_conv_full[l], alog_rows[l], dt_rows[l], gd_norm_w[l].reshape(1, HD))
        xf = comm.fwd_carry(l, "gdn")
        y_gd, ss, t_invs, *landed = gdn_fwd(proj, *gd_p, xf=xf)
        if xf is not None:
            comm.fwd_landed(l, "gdn", landed)
        acts.append((xs, nw, proj, hn_t, w_in_l, w_out_l, rg_p, y_rg, hs, ml_p, y_ml, cs, ns, ms, gd_p, y_gd, ss, t_invs))
        xs = outproj(xs, y_rg, y_ml, y_gd, w_out_l)

    dx, loss_part, d_fw = head(xs, final_norm_w.reshape(1, D), tgt)

    g_small = {k: [None] * DEPTH for k in ("norm_w", "rg_conv_w", "rg_conv_b", "rg_gate_w", "rg_gate_b", "rg_lambda",
                                           "ml_gate_b", "ml_norm_w", "gd_conv_w", "gd_a_log", "gd_dt_bias", "gd_norm_w")}
    for l in reversed(range(DEPTH)):
        x_l, nw, proj, hn_t, w_in_l, w_out_l, rg_p, y_rg, hs, ml_p, y_ml, cs, ns, ms, gd_p, y_gd, ss, t_invs = acts[l]
        dy_rg, dy_ml, dy_gd, g_wo = outproj_bwd(dx, y_rg, y_ml, y_gd, w_out_l)
        comm.own_w_out_grad(l, g_wo)
        xf = comm.bwd_carry(l, "rglru")
        d_rg, d_cw, d_cb, d_gr, d_gi, d_gb, d_lam, *landed = rglru_bwd(proj, hs, dy_rg, *rg_p, xf=xf)
        if xf is not None:
            comm.bwd_landed(landed)
        xf = comm.bwd_carry(l, "mlstm")
        d_ml, d_sm_ml, d_bias, d_mnw, *landed = mlstm_bwd(proj, dy_ml, cs, ns, ms, *ml_p, xf=xf)
        if xf is not None:
            comm.bwd_landed(landed)
        xf = comm.bwd_carry(l, "gdn")
        d_gd, d_sm_gd, d_gcw, d_al, d_dt, d_gnw, *landed = gdn_bwd(proj, dy_gd, ss, t_invs, *gd_p, xf=xf)
        if xf is not None:
            comm.bwd_landed(landed)
        dx, d_nw = inproj_bwd_x(x_l, nw, w_in_l, dx, d_rg, d_ml, d_gd, d_sm_ml, d_sm_gd)
        gw_rg = wgrad(hn_t, [d_rg], "wgrad_rg")
        gw_ml = wgrad(hn_t, [d_ml], "wgrad_ml")
        gw_gd = wgrad(hn_t, [d_gd], "wgrad_gd")
        gw_sm = wgrad(hn_t, [d_sm_ml, d_sm_gd], "wgrad_small")
        comm.grads_ready(l, gw_rg, gw_ml, gw_gd, gw_sm)
        g_small["norm_w"][l] = d_nw[0]
        g_small["rg_conv_w"][l] = d_cw
        g_small["rg_conv_b"][l] = d_cb[0]
        g_small["rg_gate_w"][l] = jnp.stack([_diag_blocks(d_gr), _diag_blocks(d_gi)])
        g_small["rg_gate_b"][l] = d_gb
        g_small["rg_lambda"][l] = d_lam[0]
        g_small["ml_gate_b"][l] = d_bias[0, 0:2 * NH].reshape(2, NH)
        g_small["ml_norm_w"][l] = d_mnw[0]
        g_small["gd_conv_w"][l] = d_gcw
        g_small["gd_a_log"][l] = d_al[0, 2 * NH:3 * NH]
        g_small["gd_dt_bias"][l] = d_dt[0, 2 * NH:3 * NH]
        g_small["gd_norm_w"][l] = d_gnw[0]
    return loss_part, dx, d_fw, g_small


def finish_step(loss_part, dx, d_fw, g_small, comm, s, me, given_w, given_m, given_v):
    small_names = ["norm_w", "rg_conv_w", "rg_conv_b", "rg_gate_w", "rg_gate_b", "rg_lambda", "ml_gate_b", "ml_norm_w",
                   "gd_conv_w", "gd_a_log", "gd_dt_bias", "gd_norm_w"]
    small_list = [loss_part[0, 0:1], d_fw[0]] + [jnp.stack(g_small[k]) for k in small_names]
    small_shapes = [a.shape for a in small_list]
    packed = _pack(small_list)
    packed = jnp.pad(packed, ((0, -packed.shape[0] % (8 * N_DEV)), (0, 0)))
    summed, gwi_r, gwo_r = small_allreduce(packed, "last_exchange", xf=comm.last_carry())
    g_all = _unpack(summed, small_shapes)

    g_wi, d_wi, m_wi, v_wi = adam_slots(gwi_r, given_w["w_in"], given_m["w_in"], given_v["w_in"], 256, "adam_w_in")
    g_wo, d_wo, m_wo, v_wo = adam_slots(gwo_r, given_w["w_out"], given_m["w_out"], given_v["w_out"], 192, "adam_w_out")
    loss = g_all[0][0]
    grads = {"final_norm_w": g_all[1]}
    for k, g in zip(small_names, g_all[2:]):
        grads[k] = g
    for k, width in (("rg_conv_w", 64), ("rg_gate_b", 64), ("gd_conv_w", 192)):
        grads[k] = lax.dynamic_slice_in_dim(grads[k], me * width, width, axis=2)
    names = small_names + ["final_norm_w"]
    shapes = [given_w[k].shape for k in names]
    d_p, m_p, v_p = adam_flat(_pack([grads[k] for k in names]), _pack([given_w[k] for k in names]),
                              _pack([given_m[k] for k in names]), _pack([given_v[k] for k in names]), "adam_small")
    deltas = dict(zip(names, _unpack(d_p, shapes)))
    new_m = dict(zip(names, _unpack(m_p, shapes)))
    new_v = dict(zip(names, _unpack(v_p, shapes)))
    grads["w_in"], deltas["w_in"], new_m["w_in"], new_v["w_in"] = g_wi, d_wi, m_wi, v_wi
    grads["w_out"], deltas["w_out"], new_m["w_out"], new_v["w_out"] = g_wo, d_wo, m_wo, v_wo

    order = ["norm_w", "w_in", "rg_conv_w", "rg_conv_b", "rg_gate_w", "rg_gate_b", "rg_lambda", "ml_gate_b", "ml_norm_w",
             "gd_conv_w", "gd_a_log", "gd_dt_bias", "gd_norm_w", "w_out", "final_norm_w"]
    return (loss, dx.reshape(1, s, D), *[grads[k] for k in order], *[deltas[k] for k in order],
            *[new_m[k] for k in order], *[new_v[k] for k in order])
```

```python
import functools
from typing import NamedTuple

import jax
import jax.numpy as jnp
from jax import lax
from jax.experimental import pallas as pl
from jax.experimental.pallas import tpu as pltpu

f32 = jnp.float32
bf16 = jnp.bfloat16
MESH = pl.DeviceIdType.MESH

N_DEV = 8
D = 1024
DEPTH = 4
EPS = 1e-6
SEG = 512
HD = 128
NH = 4
LC = 64
RUN = 4
LR = RUN * LC
D_IN = 5648
DP = 5760
SMALL0 = 5632
RG_TILE = 256
RG_C = 8.0

ADAM_LR = 0.001
ADAM_B1 = 0.9
ADAM_B2 = 0.999
ADAM_EPS = 1e-08
ADAM_WD = 0.01
ADAM_STEP = 10


def _cp(vmem_mb):
    return pltpu.CompilerParams(vmem_limit_bytes=vmem_mb * 2 ** 20)


def _dot(a, b, ca, cb):
    return lax.dot_general(a.astype(bf16), b.astype(bf16), (((ca,), (cb,)), ((), ())), preferred_element_type=f32)


@jax.custom_vjp
def mm_nn(a, b):
    return _dot(a, b, 1, 0)


@jax.custom_vjp
def mm_nt(a, b):
    return _dot(a, b, 1, 1)


@jax.custom_vjp
def mm_tn(a, b):
    return _dot(a, b, 0, 0)


mm_nn.defvjp(lambda a, b: (mm_nn(a, b), (a, b)), lambda r, g: (mm_nt(g, r[1]), mm_tn(r[0], g)))
mm_nt.defvjp(lambda a, b: (mm_nt(a, b), (a, b)), lambda r, g: (mm_nn(g, r[1]), mm_tn(g, r[0])))
mm_tn.defvjp(lambda a, b: (mm_tn(a, b), (a, b)), lambda r, g: (mm_nt(r[1], g), mm_nn(r[0], g)))


def _split(x):
    hi = x.astype(bf16)
    return hi, (x - hi.astype(f32)).astype(bf16)


def dot3(a, b):
    ah, al = _split(a)
    bh, bl = _split(b)
    d = functools.partial(jnp.dot, preferred_element_type=f32)
    return d(ah, bh) + (d(al, bh) + d(ah, bl))


def _tri_sum(x, reverse):
    n = x.shape[0]
    r = lax.broadcasted_iota(jnp.int32, (n, 3 * n), 0)
    c = lax.broadcasted_iota(jnp.int32, (n, 3 * n), 1) & (n - 1)
    ones = ((c >= r) if reverse else (c <= r)).astype(bf16)
    hi = x.astype(bf16)
    rest = x - hi.astype(f32)
    mid = rest.astype(bf16)
    lo = (rest - mid.astype(f32)).astype(bf16)
    return jnp.dot(ones, jnp.concatenate([hi, mid, lo], axis=0), preferred_element_type=f32)


@jax.custom_vjp
def cumsum_rows(x):
    return _tri_sum(x, False)


@jax.custom_vjp
def rev_cumsum_rows(x):
    return _tri_sum(x, True)


cumsum_rows.defvjp(lambda x: (cumsum_rows(x), None), lambda _, g: (rev_cumsum_rows(g),))
rev_cumsum_rows.defvjp(lambda x: (rev_cumsum_rows(x), None), lambda _, g: (cumsum_rows(g),))


def _tri(n, strict=False):
    r = lax.broadcasted_iota(jnp.int32, (n, n), 0)
    c = lax.broadcasted_iota(jnp.int32, (n, n), 1)
    return (r > c) if strict else (r >= c)


def _lane_col(v, j):
    lane = lax.broadcasted_iota(jnp.int32, v.shape, 1)
    return jnp.sum(jnp.where(lane == j, v, 0.0), axis=1, keepdims=True)


def _rows_from(pieces, rows, width):
    ridx = lax.broadcasted_iota(jnp.int32, (rows, width), 0)
    out = jnp.zeros((rows, width), f32)
    for h, p in enumerate(pieces):
        out = out + jnp.where(ridx == h, jnp.broadcast_to(p, (rows, width)), 0.0)
    return out


def causal_conv(halo8, x, w4):
    t = x.shape[0]
    xe = jnp.concatenate([halo8, x], axis=0)
    y = xe[5:5 + t] * w4[0:1]
    for k in range(1, 4):
        y = y + xe[5 + k:5 + k + t] * w4[k:k + 1]
    return y


def ml_chunk(q, k, v, o_pre, z, small, bias_row, norm_w, C, n, m):
    n_ch = q.shape[0] // LC
    lane = lax.broadcasted_iota(jnp.int32, small.shape, 1)
    pre = small + bias_row
    lg = jnp.where(lane < NH, pre, jnp.where(lane < 2 * NH, jax.nn.log_sigmoid(pre), 0.0))
    rows = [slice(c * LC, (c + 1) * LC) for c in range(n_ch)]
    lgs = [lg[r] for r in rows]
    bcs = [cumsum_rows(x) for x in lgs]
    lg_ts = [x.T for x in lgs]
    bc_ts = [x.T for x in bcs]
    causal = _tri(LC)
    pairs = [(c, h) for c in range(n_ch) for h in range(NH)]
    idx = range(len(pairs))
    sls = [slice(h * HD, (h + 1) * HD) for h in range(NH)]
    qs = [q[rows[c], sls[h]] * (HD ** -0.5) for c, h in pairs]
    ks = [k[rows[c], sls[h]] for c, h in pairs]
    vs = [v[rows[c], sls[h]] for c, h in pairs]
    li_cols = [_lane_col(lgs[c], h) for c, h in pairs]
    b_cols = [_lane_col(bcs[c], NH + h) for c, h in pairs]
    dms = [jnp.where(causal, b_cols[i] - bc_ts[c][NH + h:NH + h + 1, :] + lg_ts[c][h:h + 1, :], -jnp.inf)
           for i, (c, h) in enumerate(pairs)]
    dm_maxs = [jnp.max(dms[i], axis=-1, keepdims=True) for i in idx]
    gs = [b_cols[i][LC - 1:LC, :] for i in idx]
    wss = [gs[i] - b_cols[i] + li_cols[i] for i in idx]
    ws_maxs = [jnp.max(wss[i], axis=0, keepdims=True) for i in idx]
    qks = [mm_nt(qs[i], ks[i]) for i in idx]
    m_in = [None] * len(pairs)
    m_out = [None] * len(pairs)
    for h in range(NH):
        cur = m[h:h + 1, 0:1]
        for c in range(n_ch):
            i = c * NH + h
            m_in[i] = cur
            cur = jnp.maximum(gs[i] + cur, ws_maxs[i])
            m_out[i] = cur
    m_inters = [b_cols[i] + m_in[i] for i in idx]
    m_ts = [jnp.maximum(m_inters[i], dm_maxs[i]) for i in idx]
    ss = [qks[i] * jnp.exp(dms[i] - m_ts[i]) for i in idx]
    scs = [jnp.exp(m_inters[i] - m_ts[i]) for i in idx]
    decs = [jnp.exp(gs[i] + m_in[i] - m_out[i]) for i in idx]
    kws = [jnp.exp(wss[i] - m_out[i]) * ks[i] for i in idx]
    c_adds = [mm_tn(kws[i], vs[i]) for i in idx]
    n_adds = [jnp.sum(kws[i], axis=0, keepdims=True) for i in idx]
    svs = [mm_nn(ss[i], vs[i]) for i in idx]
    s_sums = [jnp.sum(ss[i], axis=-1, keepdims=True) for i in idx]
    c_hs = [C[sl, :] for sl in sls]
    n_hs = [n[h:h + 1, :] for h in range(NH)]
    hhs = [None] * len(pairs)
    for c in range(n_ch):
        for h in range(NH):
            i = c * NH + h
            num = svs[i] + scs[i] * mm_nn(qs[i], c_hs[h])
            den = s_sums[i] + scs[i] * jnp.sum(qs[i] * n_hs[h], axis=-1, keepdims=True)
            hhs[i] = num / jnp.maximum(jnp.abs(den), jnp.exp(-m_ts[i]))
        c_hs = [decs[c * NH + h] * c_hs[h] + c_adds[c * NH + h] for h in range(NH)]
        n_hs = [decs[c * NH + h] * n_hs[h] + n_adds[c * NH + h] for h in range(NH)]
    ys = [hhs[i] * lax.rsqrt(jnp.mean(hhs[i] * hhs[i], axis=-1, keepdims=True) + EPS) * norm_w[:, sls[h]]
          * jax.nn.sigmoid(o_pre[rows[c], sls[h]]) * jax.nn.silu(z[rows[c], sls[h]]) for i, (c, h) in enumerate(pairs)]
    y = jnp.concatenate([jnp.concatenate(ys[c * NH:(c + 1) * NH], axis=1) for c in range(n_ch)], axis=0)
    last = (n_ch - 1) * NH
    m_rows = [jnp.broadcast_to(m_out[last + h], (1, HD)) for h in range(NH)]
    return y, jnp.concatenate(c_hs, axis=0), _rows_from(n_hs, 8, HD), _rows_from(m_rows, 8, HD)


@jax.custom_vjp
def _unit_lower_inverses(mats):
    eye = (lax.broadcasted_iota(jnp.int32, (LC, LC), 0) == lax.broadcasted_iota(jnp.int32, (LC, LC), 1)).astype(f32)
    ps = [-m for m in mats]
    ts = [eye + p for p in ps]
    for _ in range(5):
        ps = [dot3(p, p) for p in ps]
        ts = [t + dot3(t, p) for t, p in zip(ts, ps)]
    return tuple(ts)


def _unit_lower_inverses_fwd(mats):
    ts = _unit_lower_inverses(mats)
    return ts, ts


def _unit_lower_inverses_bwd(ts, gs):
    tts = [t.T for t in ts]
    mid = [dot3(tt, g) for tt, g in zip(tts, gs)]
    return (tuple(-dot3(m, tt) for m, tt in zip(mid, tts)),)


_unit_lower_inverses.defvjp(_unit_lower_inverses_fwd, _unit_lower_inverses_bwd)


@jax.custom_vjp
def _known_inverses(mats, ts):
    return tuple(ts)


_known_inverses.defvjp(lambda mats, ts: (tuple(ts), tuple(ts)),
                       lambda ts, gs: (_unit_lower_inverses_bwd(ts, gs)[0], tuple(jnp.zeros_like(t) for t in ts)))


def gd_chunk(qh8, q, kh8, k, vh8, v, z, small, conv_w, alog_row, dt_row, norm_w, st, known_t=None):
    n_ch = q.shape[0] // LC
    lane = lax.broadcasted_iota(jnp.int32, small.shape, 1)
    is_a = (lane >= 2 * NH) & (lane < 3 * NH)
    g_all = jnp.where(is_a, -jnp.exp(alog_row) * jax.nn.softplus(small + dt_row), 0.0)
    beta_all = jax.nn.sigmoid(small)
    qc = jax.nn.silu(causal_conv(qh8, q, conv_w[:, 0:SEG]))
    kc = jax.nn.silu(causal_conv(kh8, k, conv_w[:, SEG:2 * SEG]))
    vc = jax.nn.silu(causal_conv(vh8, v, conv_w[:, 2 * SEG:3 * SEG]))
    incl = _tri(LC)
    strict = _tri(LC, strict=True)
    rows = [slice(c * LC, (c + 1) * LC) for c in range(n_ch)]
    gc_alls = [cumsum_rows(g_all[r]) for r in rows]
    gc_ts = [g.T for g in gc_alls]
    pairs = [(c, h) for c in range(n_ch) for h in range(NH)]
    idx = range(len(pairs))
    sls = [slice(h * HD, (h + 1) * HD) for h in range(NH)]
    qs = [qc[rows[c], sls[h]] for c, h in pairs]
    ks = [kc[rows[c], sls[h]] for c, h in pairs]
    vs = [vc[rows[c], sls[h]] for c, h in pairs]
    qs = [x * lax.rsqrt(jnp.sum(x * x, axis=-1, keepdims=True) + EPS) * (HD ** -0.5) for x in qs]
    ks = [x * lax.rsqrt(jnp.sum(x * x, axis=-1, keepdims=True) + EPS) for x in ks]
    betas = [_lane_col(beta_all[rows[c]], 3 * NH + h) for c, h in pairs]
    gcs = [_lane_col(gc_alls[c], 2 * NH + h) for c, h in pairs]
    gams = [jnp.exp(jnp.where(incl, gcs[i] - gc_ts[c][2 * NH + h:2 * NH + h + 1, :], -jnp.inf))
            for i, (c, h) in enumerate(pairs)]
    kbs = [ks[i] * betas[i] for i in idx]
    mats = tuple(jnp.where(strict, mm_nt(kbs[i], ks[i]) * gams[i], 0.0) for i in idx)
    aqks = [mm_nt(qs[i], ks[i]) * gams[i] for i in idx]
    if known_t is None:
        t_invs = _unit_lower_inverses(mats)
    else:
        t_invs = _known_inverses(mats, tuple(known_t[i] for i in idx))
    us = [mm_nn(t_invs[i], vs[i] * betas[i]) for i in idx]
    ws = [mm_nn(t_invs[i], kbs[i] * jnp.exp(gcs[i])) for i in idx]
    g_lasts = [gcs[i][LC - 1:LC, :] for i in idx]
    q_decs = [qs[i] * jnp.exp(gcs[i]) for i in idx]
    k_decs = [ks[i] * jnp.exp(g_lasts[i] - gcs[i]) for i in idx]
    e_lasts = [jnp.exp(g_lasts[i]) for i in idx]
    s_hs = [st[sl, :] for sl in sls]
    os_ = [None] * len(pairs)
    for c in range(n_ch):
        ids = [c * NH + h for h in range(NH)]
        v_news = [us[i] - mm_nn(ws[i], s_hs[h]) for h, i in enumerate(ids)]
        for h, i in enumerate(ids):
            os_[i] = mm_nn(q_decs[i], s_hs[h]) + mm_nn(aqks[i], v_news[h])
        s_hs = [s_hs[h] * e_lasts[i] + mm_tn(k_decs[i], v_news[h]) for h, i in enumerate(ids)]
    ys = [os_[i] * lax.rsqrt(jnp.mean(os_[i] * os_[i], axis=-1, keepdims=True) + EPS) * norm_w
          * jax.nn.silu(z[rows[c], sls[h]]) for i, (c, h) in enumerate(pairs)]
    y = jnp.concatenate([jnp.concatenate(ys[c * NH:(c + 1) * NH], axis=1) for c in range(n_ch)], axis=0)
    if known_t is None:
        return y, jnp.concatenate(s_hs, axis=0), jnp.stack(t_invs)
    return y, jnp.concatenate(s_hs, axis=0)


def rg_pre(xh8, x, conv_w, conv_b, gw_r, gw_i, gate_b, lam):
    xc = causal_conv(xh8, x, conv_w) + conv_b
    r = jax.nn.sigmoid(mm_nn(xc, gw_r) + gate_b[0:1])
    i = jax.nn.sigmoid(mm_nn(xc, gw_i) + gate_b[1:2])
    log_a = -RG_C * r * jax.nn.softplus(-lam)
    a = jnp.exp(log_a)
    th = jnp.tanh(log_a)
    one_minus_a2 = -2.0 * th / (1.0 - th)
    b = jnp.sqrt(one_minus_a2) * (i * xc)
    return a, b


def _scan_rows(a_ref, b_ref, out_ref, h0, n_rows, reverse):
    n_groups = n_rows // 8
    width = a_ref.shape[1]
    row = lax.broadcasted_iota(jnp.int32, (8, width), 0)

    def body(j, h):
        g = (n_groups - 1 - j) if reverse else j
        r0 = pl.multiple_of(g * 8, 8)
        av = a_ref[pl.ds(r0, 8), :]
        bv = b_ref[pl.ds(r0, 8), :]
        for d in (1, 2, 4):
            sh = (8 - d) if reverse else d
            a_s = pltpu.roll(av, sh, 0)
            b_s = pltpu.roll(bv, sh, 0)
            valid = (row < 8 - d) if reverse else (row >= d)
            bv = jnp.where(valid, av * b_s + bv, bv)
            av = jnp.where(valid, av * a_s, av)
        hv = av * h + bv
        out_ref[pl.ds(r0, 8), :] = hv
        return hv[0:1, :] if reverse else hv[7:8, :]

    return lax.fori_loop(0, n_groups, body, h0)


def _norm_rows(xv):
    r = lax.rsqrt(jnp.mean(xv * xv, axis=-1, keepdims=True) + EPS)
    return xv * r, r


def inproj(x, nw, w, xf=None):
    s, ts = x.shape[0], 512
    nt = s // ts
    n_in, n_out = 3, 2
    x_ops, x_in_specs, x_out_shapes, x_out_specs, aliases, x_sems, n_xin, n_land = _carried(xf, n_in, n_out)

    def body(*refs):
        (x_ref, nw_ref, w_ref), x_refs, (o_ref, ht_ref), land_refs, _, sem_refs = _cut_refs(refs, n_in, n_xin, n_out, n_land, 0)
        i = pl.program_id(0)
        _carry_open(xf, i, nt, x_refs, land_refs, sem_refs)
        xn, _ = _norm_rows(x_ref[...])
        hn = xn * nw_ref[...]
        o_ref[...] = jnp.dot(hn.astype(bf16), w_ref[...], preferred_element_type=f32)
        ht_ref[...] = hn.T.astype(bf16)
        _carry_close(xf, i, nt, x_refs, land_refs, sem_refs)

    return pl.pallas_call(
        body, grid=(nt,),
        in_specs=[pl.BlockSpec((ts, D), lambda i: (i, 0)), pl.BlockSpec((1, D), lambda i: (0, 0)),
                  pl.BlockSpec((D, DP), lambda i: (0, 0))] + x_in_specs,
        out_specs=[pl.BlockSpec((ts, DP), lambda i: (i, 0)), pl.BlockSpec((D, ts), lambda i: (0, i))] + x_out_specs,
        out_shape=[jax.ShapeDtypeStruct((s, DP), f32), jax.ShapeDtypeStruct((D, s), bf16)] + x_out_shapes,
        scratch_shapes=x_sems, input_output_aliases=aliases,
        name="inproj" if xf is None else "inproj_carrying", compiler_params=_cp(56))(x, nw, w, *x_ops)


def outproj(x, yr, ym, yg, wo):
    s, ts = x.shape[0], 512

    def body(x_ref, yr_ref, ym_ref, yg_ref, wo_ref, o_ref):
        acc = x_ref[...]
        for j, y_ref in enumerate((yr_ref, ym_ref, yg_ref)):
            acc = acc + jnp.dot(y_ref[...].astype(bf16), wo_ref[j * SEG:(j + 1) * SEG, :], preferred_element_type=f32)
        o_ref[...] = acc

    yspec = pl.BlockSpec((ts, SEG), lambda i: (i, 0))
    return pl.pallas_call(
        body, grid=(s // ts,),
        in_specs=[pl.BlockSpec((ts, D), lambda i: (i, 0)), yspec, yspec, yspec,
                  pl.BlockSpec((3 * SEG, D), lambda i: (0, 0))],
        out_specs=pl.BlockSpec((ts, D), lambda i: (i, 0)),
        out_shape=jax.ShapeDtypeStruct((s, D), f32), name="outproj", compiler_params=_cp(40))(x, yr, ym, yg, wo)


def head(x, fw, tgt):
    s, ts = x.shape[0], 512

    def body(x_ref, fw_ref, t_ref, dx_ref, loss_ref, dfw_ref):
        @pl.when(pl.program_id(0) == 0)
        def _():
            loss_ref[...] = jnp.zeros_like(loss_ref)
            dfw_ref[...] = jnp.zeros_like(dfw_ref)

        xn, r = _norm_rows(x_ref[...])
        fw_v = fw_ref[...]
        err = xn * fw_v - t_ref[...]
        loss_ref[...] += 0.5 * jnp.sum(jnp.mean(err * err, axis=-1, keepdims=True))
        dy = err * (1.0 / D)
        dfw_ref[0:1, :] += jnp.sum(dy * xn, axis=0, keepdims=True)
        dxn = dy * fw_v
        dx_ref[...] = r * (dxn - xn * jnp.mean(dxn * xn, axis=-1, keepdims=True))

    tile = pl.BlockSpec((ts, D), lambda i: (i, 0))
    return pl.pallas_call(
        body, grid=(s // ts,),
        in_specs=[tile, pl.BlockSpec((1, D), lambda i: (0, 0)), tile],
        out_specs=[tile, pl.BlockSpec((8, 128), lambda i: (0, 0)), pl.BlockSpec((8, D), lambda i: (0, 0))],
        out_shape=[jax.ShapeDtypeStruct((s, D), f32), jax.ShapeDtypeStruct((8, 128), f32),
                   jax.ShapeDtypeStruct((8, D), f32)], name="head")(x, fw, tgt)


def outproj_bwd(dx, yr, ym, yg, wo):
    s, ts = dx.shape[0], 256

    def body(dx_ref, yr_ref, ym_ref, yg_ref, wo_ref, dyr_ref, dym_ref, dyg_ref, gwo_ref):
        @pl.when(pl.program_id(0) == 0)
        def _():
            gwo_ref[...] = jnp.zeros_like(gwo_ref)

        dxb = dx_ref[...].astype(bf16)
        for j, (y_ref, dy_ref) in enumerate(((yr_ref, dyr_ref), (ym_ref, dym_ref), (yg_ref, dyg_ref))):
            rows = slice(j * SEG, (j + 1) * SEG)
            dy_ref[...] = lax.dot_general(dxb, wo_ref[rows, :], (((1,), (1,)), ((), ())), preferred_element_type=f32)
            gwo_ref[rows, :] += lax.dot_general(y_ref[...].astype(bf16), dxb, (((0,), (0,)), ((), ())),
                                                preferred_element_type=f32)

    yspec = pl.BlockSpec((ts, SEG), lambda i: (i, 0))
    wspec = pl.BlockSpec((3 * SEG, D), lambda i: (0, 0))
    ysh = jax.ShapeDtypeStruct((s, SEG), f32)
    return pl.pallas_call(
        body, grid=(s // ts,),
        in_specs=[pl.BlockSpec((ts, D), lambda i: (i, 0)), yspec, yspec, yspec, wspec],
        out_specs=[yspec, yspec, yspec, wspec],
        out_shape=[ysh, ysh, ysh, jax.ShapeDtypeStruct((3 * SEG, D), f32)],
        name="outproj_bwd", compiler_params=_cp(48))(dx, yr, ym, yg, wo)


def inproj_bwd_x(x, nw, w, dxo, d_rg, d_ml, d_gd, d_sa, d_sb):
    s, ts = x.shape[0], 512
    widths = (d_rg.shape[1], d_ml.shape[1], d_gd.shape[1], HD)

    def body(x_ref, nw_ref, w_ref, dxo_ref, rg_ref, ml_ref, gd_ref, sa_ref, sb_ref, dx_ref, dnw_ref):
        @pl.when(pl.program_id(0) == 0)
        def _():
            dnw_ref[...] = jnp.zeros_like(dnw_ref)

        xn, r = _norm_rows(x_ref[...])
        pieces = (rg_ref[...], ml_ref[...], gd_ref[...], sa_ref[...] + sb_ref[...])
        dhn = jnp.zeros((ts, D), f32)
        c0 = 0
        for piece, wd in zip(pieces, widths):
            dhn = dhn + lax.dot_general(piece.astype(bf16), w_ref[:, c0:c0 + wd], (((1,), (1,)), ((), ())),
                                        preferred_element_type=f32)
            c0 += wd
        dnw_ref[0:1, :] += jnp.sum(dhn * xn, axis=0, keepdims=True)
        dxn = dhn * nw_ref[...]
        dx_ref[...] = dxo_ref[...] + r * (dxn - xn * jnp.mean(dxn * xn, axis=-1, keepdims=True))

    tile = pl.BlockSpec((ts, D), lambda i: (i, 0))
    return pl.pallas_call(
        body, grid=(s // ts,),
        in_specs=[tile, pl.BlockSpec((1, D), lambda i: (0, 0)), pl.BlockSpec((D, DP), lambda i: (0, 0)), tile]
        + [pl.BlockSpec((ts, wd), lambda i: (i, 0)) for wd in widths] + [pl.BlockSpec((ts, HD), lambda i: (i, 0))],
        out_specs=[tile, pl.BlockSpec((8, D), lambda i: (0, 0))],
        out_shape=[jax.ShapeDtypeStruct((s, D), f32), jax.ShapeDtypeStruct((8, D), f32)],
        name="inproj_bwd_x", compiler_params=_cp(56))(x, nw, w, dxo, d_rg, d_ml, d_gd, d_sa, d_sb)


def wgrad(hn_t, dps, name):
    s = hn_t.shape[1]
    c = dps[0].shape[1]
    ct = min(c, SEG)
    n_dp = len(dps)

    def body(*refs):
        ht_ref = refs[0]
        dp_refs = refs[1:1 + n_dp]
        o_ref = refs[1 + n_dp]
        dp = dp_refs[0][...]
        for extra in dp_refs[1:]:
            dp = dp + extra[...]
        o_ref[...] = jnp.dot(ht_ref[...], dp.astype(bf16), preferred_element_type=f32).astype(bf16)

    return pl.pallas_call(
        body, grid=(c // ct,),
        in_specs=[pl.BlockSpec((D, s), lambda j: (0, 0))] + [pl.BlockSpec((s, ct), lambda j: (0, j)) for _ in dps],
        out_specs=pl.BlockSpec((D, ct), lambda j: (0, j)),
        out_shape=jax.ShapeDtypeStruct((D, c), bf16), name=name, compiler_params=_cp(40))(hn_t, *dps)


def _seg_spec(rows, seg, n_tiles=None):
    if n_tiles is None:
        return pl.BlockSpec((rows, SEG), lambda i: (i, seg))
    return pl.BlockSpec((rows, SEG), lambda i: (n_tiles - 1 - i, seg))


def _halo_spec(rows, seg, n_tiles=None):
    per = rows // 8
    if n_tiles is None:
        return pl.BlockSpec((8, SEG), lambda i: (jnp.maximum(i * per - 1, 0), seg))
    return pl.BlockSpec((8, SEG), lambda i: (jnp.maximum((n_tiles - 1 - i) * per - 1, 0), seg))


def _const_spec(shape):
    return pl.BlockSpec(shape, lambda i: tuple(0 for _ in shape))


def rglru_fwd(proj, conv_w, conv_b, gw_r, gw_i, gate_b, lam):
    s = proj.shape[0]
    tr = RG_TILE

    def body(xh_ref, x_ref, z_ref, cw_ref, cb_ref, gr_ref, gi_ref, gb_ref, lam_ref, y_ref, h_ref, a_s, b_s, hc):
        first = pl.program_id(0) == 0

        @pl.when(first)
        def _():
            hc[...] = jnp.zeros_like(hc)

        xh = jnp.where(first, 0.0, xh_ref[...])
        a, b = rg_pre(xh, x_ref[...], cw_ref[...], cb_ref[...], gr_ref[...], gi_ref[...], gb_ref[...], lam_ref[...])
        a_s[...] = a
        b_s[...] = b
        hc[0:1, :] = _scan_rows(a_s, b_s, h_ref, hc[0:1, :], tr, False)
        y_ref[...] = (h_ref[...] * jax.nn.silu(z_ref[...])).astype(bf16)

    out = pl.BlockSpec((tr, SEG), lambda i: (i, 0))
    return pl.pallas_call(
        body, grid=(s // tr,),
        in_specs=[_halo_spec(tr, 0), _seg_spec(tr, 0), _seg_spec(tr, 1), _const_spec((4, SEG)), _const_spec((1, SEG)),
                  _const_spec((SEG, SEG)), _const_spec((SEG, SEG)), _const_spec((2, SEG)), _const_spec((1, SEG))],
        out_specs=[out, out],
        out_shape=[jax.ShapeDtypeStruct((s, SEG), bf16), jax.ShapeDtypeStruct((s, SEG), f32)],
        scratch_shapes=[pltpu.VMEM((tr, SEG), f32), pltpu.VMEM((tr, SEG), f32), pltpu.VMEM((8, SEG), f32)],
        name="rglru_fwd", compiler_params=_cp(40))(proj, proj, proj, conv_w, conv_b, gw_r, gw_i, gate_b, lam)


def rglru_bwd(proj, hs, dy, conv_w, conv_b, gw_r, gw_i, gate_b, lam, xf=None):
    s = proj.shape[0]
    tr = RG_TILE
    nt = s // tr
    n_in, n_out = 12, 7
    x_ops, x_in_specs, x_out_shapes, x_out_specs, aliases, x_sems, n_xin, n_land = _carried(xf, n_in, n_out)

    def body(*refs):
        ins, x_refs, outs, land_refs, scratch, sem_refs = _cut_refs(refs, n_in, n_xin, n_out, n_land, 6)
        xh_ref, x_ref, z_ref, hh_ref, h_ref, dy_ref, cw_ref, cb_ref, gr_ref, gi_ref, gb_ref, lam_ref = ins
        dp_ref, dcw_ref, dcb_ref, dgr_ref, dgi_ref, dgb_ref, dlam_ref = outs
        an_s, g_s, dh_s, a_first, dh_first, dhalo = scratch
        i = pl.program_id(0)
        first_tile = i == nt - 1
        _carry_open(xf, i, nt, x_refs, land_refs, sem_refs)

        @pl.when(i == 0)
        def _():
            for ref in (dcw_ref, dcb_ref, dgr_ref, dgi_ref, dgb_ref, dlam_ref, a_first, dh_first, dhalo):
                ref[...] = jnp.zeros_like(ref)

        xh = jnp.where(first_tile, 0.0, xh_ref[...])
        params = (cw_ref[...], cb_ref[...], gr_ref[...], gi_ref[...], gb_ref[...], lam_ref[...])
        (a, _), vjp = jax.vjp(rg_pre, xh, x_ref[...], *params)
        zv = z_ref[...]
        hv = h_ref[...]
        dyv = dy_ref[...]
        sig = jax.nn.sigmoid(zv)
        g_s[...] = dyv * (zv * sig)
        dp_ref[:, SEG:2 * SEG] = (dyv * hv * (sig * (1.0 + zv * (1.0 - sig)))).astype(bf16)
        ridx = lax.broadcasted_iota(jnp.int32, (tr, SEG), 0)
        an_s[...] = jnp.where(ridx == tr - 1, jnp.broadcast_to(a_first[0:1, :], (tr, SEG)), pltpu.roll(a, tr - 1, 0))
        _scan_rows(an_s, g_s, dh_s, dh_first[0:1, :], tr, True)
        dh = dh_s[...]
        h_prev_last = jnp.where(first_tile, 0.0, hh_ref[...])[7:8, :]
        h_prev = pltpu.roll(hv, 1, 0)
        h_prev = jnp.where(ridx == 0, jnp.broadcast_to(h_prev_last, (tr, SEG)), h_prev)
        dxh, dx, dcw, dcb, dgr, dgi, dgb, dlam = vjp((dh * h_prev, dh))
        dx = dx + jnp.concatenate([jnp.zeros((tr - 8, SEG), f32), dhalo[...]], axis=0)
        dp_ref[:, 0:SEG] = dx.astype(bf16)
        dhalo[...] = dxh
        a_first[0:1, :] = a[0:1, :]
        dh_first[0:1, :] = dh[0:1, :]
        dcw_ref[...] += dcw
        dcb_ref[...] += dcb
        dgr_ref[...] += dgr
        dgi_ref[...] += dgi
        dgb_ref[...] += dgb
        dlam_ref[...] += dlam
        _carry_close(xf, i, nt, x_refs, land_refs, sem_refs)

    pspecs = [_const_spec((4, SEG)), _const_spec((1, SEG)), _const_spec((SEG, SEG)), _const_spec((SEG, SEG)),
              _const_spec((2, SEG)), _const_spec((1, SEG))]
    pshapes = [jax.ShapeDtypeStruct(sh, f32) for sh in ((4, SEG), (1, SEG), (SEG, SEG), (SEG, SEG), (2, SEG), (1, SEG))]
    tile = pl.BlockSpec((tr, SEG), lambda i: (nt - 1 - i, 0))
    return pl.pallas_call(
        body, grid=(nt,),
        in_specs=[_halo_spec(tr, 0, nt), _seg_spec(tr, 0, nt), _seg_spec(tr, 1, nt),
                  pl.BlockSpec((8, SEG), lambda i: (jnp.maximum((nt - 1 - i) * (tr // 8) - 1, 0), 0)), tile, tile] + pspecs
        + x_in_specs,
        out_specs=[pl.BlockSpec((tr, 2 * SEG), lambda i: (nt - 1 - i, 0))] + pspecs + x_out_specs,
        out_shape=[jax.ShapeDtypeStruct((s, 2 * SEG), bf16)] + pshapes + x_out_shapes,
        scratch_shapes=[pltpu.VMEM((tr, SEG), f32), pltpu.VMEM((tr, SEG), f32), pltpu.VMEM((tr, SEG), f32),
                        pltpu.VMEM((8, SEG), f32), pltpu.VMEM((8, SEG), f32), pltpu.VMEM((8, SEG), f32)] + x_sems,
        input_output_aliases=aliases, name="rglru_bwd" if xf is None else "rglru_bwd_carrying", compiler_params=_cp(48))(
            proj, proj, proj, hs, hs, dy, conv_w, conv_b, gw_r, gw_i, gate_b, lam, *x_ops)


ML_SEGS = (2, 3, 4, 5, 6)
SMALL_BLK = SMALL0 // HD


def _cut_refs(refs, n_in, n_xin, n_out, n_land, n_scratch):
    bounds = [0, n_in, n_in + n_xin, n_in + n_xin + n_out, n_in + n_xin + n_out + n_land,
              n_in + n_xin + n_out + n_land + n_scratch, len(refs)]
    return [refs[a:b] for a, b in zip(bounds[:-1], bounds[1:])]


def mlstm_fwd(proj, bias_row, norm_w, xf=None):
    s = proj.shape[0]
    nc = s // LR
    n_in, n_out = 8, 4
    x_ops, x_in_specs, x_out_shapes, x_out_specs, aliases, x_sems, n_xin, n_land = _carried(xf, n_in, n_out)

    def body(*refs):
        ins, x_refs, outs, land_refs, scratch, sem_refs = _cut_refs(refs, n_in, n_xin, n_out, n_land, 3)
        q_ref, k_ref, v_ref, o_ref, z_ref, sm_ref, b_ref, nw_ref = ins
        y_ref, cs_ref, ns_ref, ms_ref = outs
        c_s, n_s, m_s = scratch
        i = pl.program_id(0)
        _carry_open(xf, i, nc, x_refs, land_refs, sem_refs)

        @pl.when(i == 0)
        def _():
            c_s[...] = jnp.zeros_like(c_s)
            n_s[...] = jnp.zeros_like(n_s)
            m_s[...] = jnp.zeros_like(m_s)

        cs_ref[0] = c_s[...]
        ns_ref[0] = n_s[...]
        ms_ref[0] = m_s[...]
        y, c2, n2, m2 = ml_chunk(q_ref[...], k_ref[...], v_ref[...], o_ref[...], z_ref[...], sm_ref[...],
                                 b_ref[...], nw_ref[...], c_s[...], n_s[...], m_s[...])
        y_ref[...] = y.astype(bf16)
        c_s[...] = c2
        n_s[...] = n2
        m_s[...] = m2
        _carry_close(xf, i, nc, x_refs, land_refs, sem_refs)

    return pl.pallas_call(
        body, grid=(nc,),
        in_specs=[_seg_spec(LR, sg) for sg in ML_SEGS]
        + [pl.BlockSpec((LR, HD), lambda i: (i, SMALL_BLK)), _const_spec((1, HD)), _const_spec((1, SEG))] + x_in_specs,
        out_specs=[pl.BlockSpec((LR, SEG), lambda i: (i, 0)), pl.BlockSpec((1, SEG, HD), lambda i: (i, 0, 0)),
                   pl.BlockSpec((1, 8, HD), lambda i: (i, 0, 0)), pl.BlockSpec((1, 8, HD), lambda i: (i, 0, 0))] + x_out_specs,
        out_shape=[jax.ShapeDtypeStruct((s, SEG), bf16), jax.ShapeDtypeStruct((nc, SEG, HD), f32),
                   jax.ShapeDtypeStruct((nc, 8, HD), f32), jax.ShapeDtypeStruct((nc, 8, HD), f32)] + x_out_shapes,
        scratch_shapes=[pltpu.VMEM((SEG, HD), f32), pltpu.VMEM((8, HD), f32), pltpu.VMEM((8, HD), f32)] + x_sems,
        input_output_aliases=aliases, name="mlstm_fwd" if xf is None else "mlstm_fwd_carrying")(
            proj, proj, proj, proj, proj, proj, bias_row, norm_w, *x_ops)


def mlstm_bwd(proj, dy, cs, ns, ms, bias_row, norm_w, xf=None):
    s = proj.shape[0]
    nc = s // LR
    n_in, n_out = 12, 4
    x_ops, x_in_specs, x_out_shapes, x_out_specs, aliases, x_sems, n_xin, n_land = _carried(xf, n_in, n_out)

    def body(*refs):
        ins, x_refs, outs, land_refs, scratch, sem_refs = _cut_refs(refs, n_in, n_xin, n_out, n_land, 3)
        q_ref, k_ref, v_ref, o_ref, z_ref, sm_ref, dy_ref, cs_ref, ns_ref, ms_ref, b_ref, nw_ref = ins
        dp_ref, dsm_ref, db_ref, dnw_ref = outs
        dc_s, dn_s, dm_s = scratch
        i = pl.program_id(0)
        _carry_open(xf, i, nc, x_refs, land_refs, sem_refs)

        @pl.when(i == 0)
        def _():
            for ref in (db_ref, dnw_ref, dc_s, dn_s, dm_s):
                ref[...] = jnp.zeros_like(ref)

        _, vjp = jax.vjp(ml_chunk, q_ref[...], k_ref[...], v_ref[...], o_ref[...], z_ref[...], sm_ref[...],
                         b_ref[...], nw_ref[...], cs_ref[0], ns_ref[0], ms_ref[0])
        dq, dk, dv, do, dz, dsm, db, dnw, dc, dn, dm = vjp((dy_ref[...], dc_s[...], dn_s[...], dm_s[...]))
        for j, val in enumerate((dq, dk, dv, do, dz)):
            dp_ref[:, j * SEG:(j + 1) * SEG] = val.astype(bf16)
        dsm_ref[...] = dsm
        db_ref[0:1, :] += db
        dnw_ref[0:1, :] += dnw
        dc_s[...] = dc
        dn_s[...] = dn
        dm_s[...] = dm
        _carry_close(xf, i, nc, x_refs, land_refs, sem_refs)

    rev3 = lambda i: (nc - 1 - i, 0, 0)
    return pl.pallas_call(
        body, grid=(nc,),
        in_specs=[_seg_spec(LR, sg, nc) for sg in ML_SEGS]
        + [pl.BlockSpec((LR, HD), lambda i: (nc - 1 - i, SMALL_BLK)), pl.BlockSpec((LR, SEG), lambda i: (nc - 1 - i, 0)),
           pl.BlockSpec((1, SEG, HD), rev3), pl.BlockSpec((1, 8, HD), rev3), pl.BlockSpec((1, 8, HD), rev3),
           _const_spec((1, HD)), _const_spec((1, SEG))] + x_in_specs,
        out_specs=[pl.BlockSpec((LR, 5 * SEG), lambda i: (nc - 1 - i, 0)), pl.BlockSpec((LR, HD), lambda i: (nc - 1 - i, 0)),
                   _const_spec((8, HD)), _const_spec((8, SEG))] + x_out_specs,
        out_shape=[jax.ShapeDtypeStruct((s, 5 * SEG), bf16), jax.ShapeDtypeStruct((s, HD), f32),
                   jax.ShapeDtypeStruct((8, HD), f32), jax.ShapeDtypeStruct((8, SEG), f32)] + x_out_shapes,
        scratch_shapes=[pltpu.VMEM((SEG, HD), f32), pltpu.VMEM((8, HD), f32), pltpu.VMEM((8, HD), f32)] + x_sems,
        input_output_aliases=aliases, name="mlstm_bwd" if xf is None else "mlstm_bwd_carrying", compiler_params=_cp(48))(
            proj, proj, proj, proj, proj, proj, dy, cs, ns, ms, bias_row, norm_w, *x_ops)


GD_SEGS = (7, 8, 9)


def _carried(xf, n_in, n_out):
    operands, out_shapes, aliases, sems, _, n_xin, n_land = _carry_plumb(xf, n_in, n_out)
    return operands, [_ANY] * n_xin, out_shapes, [_ANY] * n_land, aliases, sems, n_xin, n_land


def _carry_open(xf, i, n_steps, x_refs, land_refs, sem_refs):
    if xf is None:
        return
    srcs = x_refs[:len(xf.srcs)]

    @pl.when(i == 0)
    def _():
        _carry_start(xf, srcs, land_refs, sem_refs)

    @pl.when(i == n_steps - 1)
    def _():
        _carry_middle(xf, srcs, land_refs, sem_refs)


def _carry_close(xf, i, n_steps, x_refs, land_refs, sem_refs):
    if xf is None:
        return

    @pl.when(i == n_steps - 1)
    def _():
        _carry_finish(xf, x_refs[:len(xf.srcs)], land_refs, sem_refs)


def gdn_fwd(proj, conv_w, alog_row, dt_row, norm_w, xf=None):
    s = proj.shape[0]
    nc = s // LR
    n_in, n_out = 12, 3
    x_ops, x_in_specs, x_out_shapes, x_out_specs, aliases, x_sems, n_xin, n_land = _carried(xf, n_in, n_out)

    def body(*refs):
        qh_ref, q_ref, kh_ref, k_ref, vh_ref, v_ref, z_ref, sm_ref, cw_ref, al_ref, dt_ref, nw_ref = refs[:n_in]
        x_refs = refs[n_in:n_in + n_xin]
        y_ref, ss_ref, ti_ref = refs[n_in + n_xin:n_in + n_xin + n_out]
        land_refs = refs[n_in + n_xin + n_out:n_in + n_xin + n_out + n_land]
        st_s = refs[n_in + n_xin + n_out + n_land]
        sem_refs = refs[n_in + n_xin + n_out + n_land + 1:]
        i = pl.program_id(0)
        first = i == 0
        _carry_open(xf, i, nc, x_refs, land_refs, sem_refs)

        @pl.when(first)
        def _():
            st_s[...] = jnp.zeros_like(st_s)

        ss_ref[0] = st_s[...]
        halo = [jnp.where(first, 0.0, r[...]) for r in (qh_ref, kh_ref, vh_ref)]
        y, st2, t_invs = gd_chunk(halo[0], q_ref[...], halo[1], k_ref[...], halo[2], v_ref[...], z_ref[...], sm_ref[...],
                                  cw_ref[...], al_ref[...], dt_ref[...], nw_ref[...], st_s[...])
        y_ref[...] = y.astype(bf16)
        ti_ref[...] = t_invs
        st_s[...] = st2
        _carry_close(xf, i, nc, x_refs, land_refs, sem_refs)

    qkv_specs = []
    for sg in GD_SEGS:
        qkv_specs += [_halo_spec(LR, sg), _seg_spec(LR, sg)]
    return pl.pallas_call(
        body, grid=(nc,),
        in_specs=qkv_specs + [_seg_spec(LR, 10), pl.BlockSpec((LR, HD), lambda i: (i, SMALL_BLK)),
                              _const_spec((4, 3 * SEG)), _const_spec((1, HD)), _const_spec((1, HD)), _const_spec((1, HD))]
        + x_in_specs,
        out_specs=[pl.BlockSpec((LR, SEG), lambda i: (i, 0)), pl.BlockSpec((1, SEG, HD), lambda i: (i, 0, 0)),
                   pl.BlockSpec((RUN * NH, LC, LC), lambda i: (i, 0, 0))] + x_out_specs,
        out_shape=[jax.ShapeDtypeStruct((s, SEG), bf16), jax.ShapeDtypeStruct((nc, SEG, HD), f32),
                   jax.ShapeDtypeStruct((s // LC * NH, LC, LC), f32)] + x_out_shapes,
        scratch_shapes=[pltpu.VMEM((SEG, HD), f32)] + x_sems, input_output_aliases=aliases,
        name="gdn_fwd" if xf is None else "gdn_fwd_carrying")(
            proj, proj, proj, proj, proj, proj, proj, proj, conv_w, alog_row, dt_row, norm_w, *x_ops)


def gdn_bwd(proj, dy, ss, t_invs, conv_w, alog_row, dt_row, norm_w, xf=None):
    s = proj.shape[0]
    nc = s // LR
    n_in, n_out = 15, 6
    x_ops, x_in_specs, x_out_shapes, x_out_specs, aliases, x_sems, n_xin, n_land = _carried(xf, n_in, n_out)

    def body(*refs):
        (qh_ref, q_ref, kh_ref, k_ref, vh_ref, v_ref, z_ref, sm_ref, dy_ref, ss_ref, ti_ref,
         cw_ref, al_ref, dt_ref, nw_ref) = refs[:n_in]
        x_refs = refs[n_in:n_in + n_xin]
        dp_ref, dsm_ref, dcw_ref, dal_ref, ddt_ref, dnw_ref = refs[n_in + n_xin:n_in + n_xin + n_out]
        land_refs = refs[n_in + n_xin + n_out:n_in + n_xin + n_out + n_land]
        dst_s, dhalo = refs[n_in + n_xin + n_out + n_land:n_in + n_xin + n_out + n_land + 2]
        sem_refs = refs[n_in + n_xin + n_out + n_land + 2:]
        i = pl.program_id(0)
        first_chunk = i == nc - 1
        _carry_open(xf, i, nc, x_refs, land_refs, sem_refs)

        @pl.when(i == 0)
        def _():
            for ref in (dcw_ref, dal_ref, ddt_ref, dnw_ref, dst_s, dhalo):
                ref[...] = jnp.zeros_like(ref)

        halo = [jnp.where(first_chunk, 0.0, r[...]) for r in (qh_ref, kh_ref, vh_ref)]
        with_known = functools.partial(gd_chunk, known_t=ti_ref[...])
        _, vjp = jax.vjp(with_known, halo[0], q_ref[...], halo[1], k_ref[...], halo[2], v_ref[...], z_ref[...], sm_ref[...],
                         cw_ref[...], al_ref[...], dt_ref[...], nw_ref[...], ss_ref[0])
        dqh, dq, dkh, dk, dvh, dv, dz, dsm, dcw, dal, ddt, dnw, dst = vjp((dy_ref[...], dst_s[...]))
        for j, val in enumerate((dq, dk, dv)):
            val = val + jnp.concatenate([jnp.zeros((LR - 8, SEG), f32), dhalo[:, j * SEG:(j + 1) * SEG]], axis=0)
            dp_ref[:, j * SEG:(j + 1) * SEG] = val.astype(bf16)
        dp_ref[:, 3 * SEG:4 * SEG] = dz.astype(bf16)
        for j, val in enumerate((dqh, dkh, dvh)):
            dhalo[:, j * SEG:(j + 1) * SEG] = val
        dsm_ref[...] = dsm
        dcw_ref[...] += dcw
        dal_ref[0:1, :] += dal
        ddt_ref[0:1, :] += ddt
        dnw_ref[0:1, :] += dnw
        dst_s[...] = dst
        _carry_close(xf, i, nc, x_refs, land_refs, sem_refs)

    qkv_specs = []
    for sg in GD_SEGS:
        qkv_specs += [_halo_spec(LR, sg, nc), _seg_spec(LR, sg, nc)]
    return pl.pallas_call(
        body, grid=(nc,),
        in_specs=qkv_specs + [_seg_spec(LR, 10, nc), pl.BlockSpec((LR, HD), lambda i: (nc - 1 - i, SMALL_BLK)),
                              pl.BlockSpec((LR, SEG), lambda i: (nc - 1 - i, 0)),
                              pl.BlockSpec((1, SEG, HD), lambda i: (nc - 1 - i, 0, 0)),
                              pl.BlockSpec((RUN * NH, LC, LC), lambda i: (nc - 1 - i, 0, 0)),
                              _const_spec((4, 3 * SEG)), _const_spec((1, HD)), _const_spec((1, HD)), _const_spec((1, HD))]
        + x_in_specs,
        out_specs=[pl.BlockSpec((LR, 4 * SEG), lambda i: (nc - 1 - i, 0)), pl.BlockSpec((LR, HD), lambda i: (nc - 1 - i, 0)),
                   _const_spec((4, 3 * SEG)), _const_spec((8, HD)), _const_spec((8, HD)), _const_spec((8, HD))] + x_out_specs,
        out_shape=[jax.ShapeDtypeStruct((s, 4 * SEG), bf16), jax.ShapeDtypeStruct((s, HD), f32),
                   jax.ShapeDtypeStruct((4, 3 * SEG), f32), jax.ShapeDtypeStruct((8, HD), f32),
                   jax.ShapeDtypeStruct((8, HD), f32), jax.ShapeDtypeStruct((8, HD), f32)] + x_out_shapes,
        scratch_shapes=[pltpu.VMEM((SEG, HD), f32), pltpu.VMEM((8, 3 * SEG), f32)] + x_sems, input_output_aliases=aliases,
        name="gdn_bwd" if xf is None else "gdn_bwd_carrying", compiler_params=_cp(48))(
            proj, proj, proj, proj, proj, proj, proj, proj, dy, ss, t_invs, conv_w, alog_row, dt_row, norm_w, *x_ops)


def _my_place():
    return lax.axis_index("x"), lax.axis_index("y"), lax.axis_index("c")


def _slot(p):
    return 4 * p[0] + 2 * p[1] + p[2]


def _peer(me, j):
    flips = ((j >> 2) & 1, (j >> 1) & 1, j & 1)
    return tuple((1 - v) if fl else v for v, fl in zip(me, flips))


_ANY = pl.BlockSpec(memory_space=pl.ANY)


class Scatter(NamedTuple):
    srcs: list
    lands: list
    land_of: list
    layer: list
    span: list


class Gather2(NamedTuple):
    srcs: list
    lands: list
    span: list


def _carry_plumb(cx, n_in, n_out):
    if cx is None:
        return [], [], {}, [], 0, 0, 0
    n_src = len(cx.srcs)
    passed = [li for li, ld in enumerate(cx.lands) if not isinstance(ld, jax.ShapeDtypeStruct)]
    operands = list(cx.srcs) + [cx.lands[li] for li in passed]
    aliases = {n_in + n_src + k: n_out + li for k, li in enumerate(passed)}
    out_shapes = [jax.ShapeDtypeStruct(ld.shape, ld.dtype) for ld in cx.lands]
    sems = [pltpu.SemaphoreType.DMA((n_src, N_DEV - 1)), pltpu.SemaphoreType.DMA((n_src, N_DEV - 1)),
            pltpu.SemaphoreType.DMA((n_src,))]
    return operands, out_shapes, aliases, sems, n_src, len(operands), len(cx.lands)


def _scatter_copies(sc, src_refs, land_refs, send_sems, recv_sems, local_sems):
    me = _my_place()
    mine = _slot(me)
    local, remote = [], []
    for a, src_ref in enumerate(src_refs):
        lo, hi = sc.span[a]
        rows = pl.ds(lo, hi - lo)
        dst = land_refs[sc.land_of[a]].at[mine, sc.layer[a], rows]
        local.append(pltpu.make_async_copy(src_ref.at[mine, rows], dst, local_sems.at[a]))
        for j in range(1, N_DEV):
            peer = _peer(me, j)
            remote.append(pltpu.make_async_remote_copy(
                src_ref=src_ref.at[_slot(peer), rows], dst_ref=dst, send_sem=send_sems.at[a, j - 1],
                recv_sem=recv_sems.at[a, j - 1], device_id=peer, device_id_type=MESH))
    return local, remote


def _gather2_copy(land, sems, a, k, block_of, to, rows, src=None):
    dst = land.at[_slot(block_of), rows]
    return pltpu.make_async_remote_copy(src_ref=dst if src is None else src, dst_ref=dst, send_sem=sems[0].at[a, k],
                                        recv_sem=sems[1].at[a, k], device_id=to, device_id_type=MESH)


def _gather2_places():
    x, y, c = _my_place()
    return (x, y, c), (x, y, 1 - c), [(1 - x, y), (x, 1 - y), (1 - x, 1 - y)], c


def _carry_start(cx, src_refs, land_refs, sems):
    if isinstance(cx, Scatter):
        local, remote = _scatter_copies(cx, src_refs, land_refs, *sems)
        for cp in local + remote:
            cp.start()
        return
    me, sib, chips, c = _gather2_places()
    for a, (src_ref, land) in enumerate(zip(src_refs, land_refs)):
        rows = pl.ds(cx.span[a][0], cx.span[a][1] - cx.span[a][0])
        src = src_ref.at[rows]
        pltpu.make_async_copy(src, land.at[_slot(me), rows], sems[2].at[a]).start()
        _gather2_copy(land, sems, a, 0, me, sib, rows, src=src).start()
        for j, chip in enumerate(chips):
            _gather2_copy(land, sems, a, 1 + j, me, (*chip, c), rows, src=src).start()


def _carry_middle(cx, src_refs, land_refs, sems):
    if isinstance(cx, Scatter):
        return
    me, sib, chips, c = _gather2_places()
    for a, land in enumerate(land_refs):
        rows = pl.ds(cx.span[a][0], cx.span[a][1] - cx.span[a][0])
        for j, chip in enumerate(chips):
            _gather2_copy(land, sems, a, 1 + j, (*chip, c), me, rows).wait_recv()
            _gather2_copy(land, sems, a, 4 + j, (*chip, c), sib, rows).start()


def _carry_finish(cx, src_refs, land_refs, sems):
    if isinstance(cx, Scatter):
        local, remote = _scatter_copies(cx, src_refs, land_refs, *sems)
        for cp in remote:
            cp.wait()
        for cp in local:
            cp.wait()
        return
    me, sib, chips, c = _gather2_places()
    for a, (src_ref, land) in enumerate(zip(src_refs, land_refs)):
        rows = pl.ds(cx.span[a][0], cx.span[a][1] - cx.span[a][0])
        src = src_ref.at[rows]
        _gather2_copy(land, sems, a, 0, sib, me, rows).wait_recv()
        for j, chip in enumerate(chips):
            _gather2_copy(land, sems, a, 4 + j, (*chip, 1 - c), me, rows).wait_recv()
        _gather2_copy(land, sems, a, 0, me, sib, rows, src=src).wait_send()
        for j, chip in enumerate(chips):
            _gather2_copy(land, sems, a, 1 + j, me, (*chip, c), rows, src=src).wait_send()
            _gather2_copy(land, sems, a, 4 + j, (*chip, c), sib, rows).wait_send()
        pltpu.make_async_copy(src, land.at[_slot(me), rows], sems[2].at[a]).wait()


def exchange(cx, name):
    operands, out_shapes, aliases, sems, n_src, n_xin, n_land = _carry_plumb(cx, 0, 0)

    def body(*refs):
        src_refs, land_refs, sem_refs = refs[:n_src], refs[n_xin:n_xin + n_land], refs[n_xin + n_land:]
        _carry_start(cx, src_refs, land_refs, sem_refs)
        _carry_middle(cx, src_refs, land_refs, sem_refs)
        _carry_finish(cx, src_refs, land_refs, sem_refs)

    return pl.pallas_call(body, in_specs=[_ANY] * n_xin, out_specs=[_ANY] * n_land, out_shape=out_shapes,
                          scratch_shapes=sems, input_output_aliases=aliases, name=name)(*operands)


def small_allreduce(packed, name, xf=None):
    r = packed.shape[0]
    rc = r // N_DEV
    x_ops, x_in_specs, x_out_shapes, x_out_specs, aliases, x_sems, n_xin, n_land = _carried(xf, 1, 1)

    def body(*refs):
        (in_ref,), x_refs, (out_ref,), land_refs, scratch, sem_refs = _cut_refs(refs, 1, n_xin, 1, n_land, 5)
        recv_buf, send1, recv1, send2, recv2 = scratch
        if xf is not None:
            _carry_start(xf, x_refs[:len(xf.srcs)], land_refs, sem_refs)
        me = _my_place()
        mine = _slot(me)
        my_rows = pl.ds(pl.multiple_of(mine * rc, 8), rc)
        first, second = [], []
        for j in range(1, N_DEV):
            peer = _peer(me, j)
            peer_rows = pl.ds(pl.multiple_of(_slot(peer) * rc, 8), rc)
            first.append(pltpu.make_async_remote_copy(
                src_ref=in_ref.at[peer_rows], dst_ref=recv_buf.at[mine], send_sem=send1.at[j - 1], recv_sem=recv1.at[j - 1],
                device_id=peer, device_id_type=MESH))
            second.append(pltpu.make_async_remote_copy(
                src_ref=out_ref.at[my_rows], dst_ref=out_ref.at[my_rows], send_sem=send2.at[j - 1], recv_sem=recv2.at[j - 1],
                device_id=peer, device_id_type=MESH))
        for cp in first:
            cp.start()
        recv_buf[mine] = in_ref[my_rows]
        for cp in first:
            cp.wait()
        acc = recv_buf[0]
        for k in range(1, N_DEV):
            acc = acc + recv_buf[k]
        out_ref[my_rows] = acc
        for cp in second:
            cp.start()
        for cp in second:
            cp.wait()
        if xf is not None:
            _carry_finish(xf, x_refs[:len(xf.srcs)], land_refs, sem_refs)

    vmem = pl.BlockSpec(memory_space=pltpu.VMEM)
    return pl.pallas_call(
        body, in_specs=[vmem] + x_in_specs, out_specs=[vmem] + x_out_specs,
        out_shape=[jax.ShapeDtypeStruct((r, 128), f32)] + x_out_shapes,
        scratch_shapes=[pltpu.VMEM((N_DEV, rc, 128), f32)] + [pltpu.SemaphoreType.DMA((N_DEV - 1,))] * 4 + x_sems,
        input_output_aliases=aliases, name=name, compiler_params=_cp(32))(packed, *x_ops)


def _adamw(w, g, m, v):
    m = ADAM_B1 * m + (1.0 - ADAM_B1) * g
    v = ADAM_B2 * v + (1.0 - ADAM_B2) * (g * g)
    m_hat = m / (1.0 - ADAM_B1 ** ADAM_STEP)
    v_hat = v / (1.0 - ADAM_B2 ** ADAM_STEP)
    delta = -ADAM_LR * (m_hat / (jnp.sqrt(v_hat) + ADAM_EPS) + ADAM_WD * w)
    return delta, m, v


def adam_slots(slots, w, m, v, rows, name):
    _, depth, r, c = slots.shape

    def body(s_ref, w_ref, m_ref, v_ref, g_ref, d_ref, m2_ref, v2_ref):
        g = s_ref[0, 0].astype(f32)
        for k in range(1, N_DEV):
            g = g + s_ref[k, 0].astype(f32)
        d, m2, v2 = _adamw(w_ref[0], g, m_ref[0], v_ref[0])
        g_ref[0] = g
        d_ref[0] = d
        m2_ref[0] = m2
        v2_ref[0] = v2

    blk = pl.BlockSpec((1, rows, c), lambda l, i: (l, i, 0))
    sh = jax.ShapeDtypeStruct((depth, r, c), f32)
    return pl.pallas_call(
        body, grid=(depth, r // rows),
        in_specs=[pl.BlockSpec((N_DEV, 1, rows, c), lambda l, i: (0, l, i, 0)), blk, blk, blk],
        out_specs=[blk] * 4, out_shape=[sh] * 4, name=name, compiler_params=_cp(40))(slots, w, m, v)


def adam_flat(g, w, m, v, name):
    def body(g_ref, w_ref, m_ref, v_ref, d_ref, m2_ref, v2_ref):
        d, m2, v2 = _adamw(w_ref[...], g_ref[...], m_ref[...], v_ref[...])
        d_ref[...] = d
        m2_ref[...] = m2
        v2_ref[...] = v2

    sh = jax.ShapeDtypeStruct(g.shape, f32)
    return pl.pallas_call(body, out_shape=[sh] * 3, name=name)(g, w, m, v)


def _rows_of(shape):
    n = 1
    for dim in shape:
        n *= dim
    return n, -(-n // (8 * 128)) * 8


def _pack(arrs):
    parts = []
    for a in arrs:
        n, rows = _rows_of(a.shape)
        parts.append(jnp.pad(a.reshape(-1).astype(f32), (0, rows * 128 - n)).reshape(rows, 128))
    return jnp.concatenate(parts, axis=0)


def _unpack(packed, shapes):
    out, row = [], 0
    for sh in shapes:
        n, rows = _rows_of(sh)
        out.append(packed[row:row + rows].reshape(-1)[:n].reshape(sh))
        row += rows
    return out


def _block_diag(gw):
    eye = jnp.eye(8, dtype=gw.dtype)
    return (gw[..., :, :, None, :] * eye[:, None, :, None]).reshape(gw.shape[:-3] + (SEG, SEG))


def _diag_blocks(dense):
    eye = jnp.eye(8, dtype=dense.dtype)
    return (dense.reshape(8, 64, 8, 64) * eye[:, None, :, None]).sum(axis=2)


def _lane_rows(vals, first_lane):
    return jnp.zeros((vals.shape[0], 1, HD), f32).at[:, 0, first_lane:first_lane + vals.shape[1]].set(vals)


def kernel(x, norm_w, w_in, rg_conv_w, rg_conv_b, rg_gate_w, rg_gate_b, rg_lambda, ml_gate_b, ml_norm_w, gd_conv_w, gd_a_log, gd_dt_bias, gd_norm_w, w_out, final_norm_w, loss_target, m_norm_w, m_w_in, m_rg_conv_w, m_rg_conv_b, m_rg_gate_w, m_rg_gate_b, m_rg_lambda, m_ml_gate_b, m_ml_norm_w, m_gd_conv_w, m_gd_a_log, m_gd_dt_bias, m_gd_norm_w, m_w_out, m_final_norm_w, v_norm_w, v_w_in, v_rg_conv_w, v_rg_conv_b, v_rg_gate_w, v_rg_gate_b, v_rg_lambda, v_ml_gate_b, v_ml_norm_w, v_gd_conv_w, v_gd_a_log, v_gd_dt_bias, v_gd_norm_w, v_w_out, v_final_norm_w):
    s = x.shape[1]
    xs = x.reshape(s, D)
    tgt = loss_target.reshape(s, D)
    me = 4 * lax.axis_index("x") + 2 * lax.axis_index("y") + lax.axis_index("c")

    comm = MeshComm(w_in, w_out, [rg_conv_w, rg_gate_b, gd_conv_w])
    rg_conv_full, rg_gb_full, gd_conv_full = comm.small_weights
    loss_part, dx, d_fw, g_small = local_step(
        xs, tgt, comm, rg_conv_full, rg_gb_full, gd_conv_full, norm_w, rg_conv_b, rg_gate_w, rg_lambda,
        ml_gate_b, ml_norm_w, gd_a_log, gd_dt_bias, gd_norm_w, final_norm_w)
    given_w = dict(norm_w=norm_w, rg_conv_w=rg_conv_w, rg_conv_b=rg_conv_b, rg_gate_w=rg_gate_w, rg_gate_b=rg_gate_b,
                   rg_lambda=rg_lambda, ml_gate_b=ml_gate_b, ml_norm_w=ml_norm_w, gd_conv_w=gd_conv_w, gd_a_log=gd_a_log,
                   gd_dt_bias=gd_dt_bias, gd_norm_w=gd_norm_w, final_norm_w=final_norm_w, w_in=w_in, w_out=w_out)
    given_m = dict(norm_w=m_norm_w, rg_conv_w=m_rg_conv_w, rg_conv_b=m_rg_conv_b, rg_gate_w=m_rg_gate_w, rg_gate_b=m_rg_gate_b,
                   rg_lambda=m_rg_lambda, ml_gate_b=m_ml_gate_b, ml_norm_w=m_ml_norm_w, gd_conv_w=m_gd_conv_w,
                   gd_a_log=m_gd_a_log, gd_dt_bias=m_gd_dt_bias, gd_norm_w=m_gd_norm_w, final_norm_w=m_final_norm_w,
                   w_in=m_w_in, w_out=m_w_out)
    given_v = dict(norm_w=v_norm_w, rg_conv_w=v_rg_conv_w, rg_conv_b=v_rg_conv_b, rg_gate_w=v_rg_gate_w, rg_gate_b=v_rg_gate_b,
                   rg_lambda=v_rg_lambda, ml_gate_b=v_ml_gate_b, ml_norm_w=v_ml_norm_w, gd_conv_w=v_gd_conv_w,
                   gd_a_log=v_gd_a_log, gd_dt_bias=v_gd_dt_bias, gd_norm_w=v_gd_norm_w, final_norm_w=v_final_norm_w,
                   w_in=v_w_in, w_out=v_w_out)
    return finish_step(loss_part, dx, d_fw, g_small, comm, s, me, given_w, given_m, given_v)


def _gathered_pieces():
    per = D_IN // N_DEV
    pieces = []
    for lo, hi in ((0, 3584), (3592, 5640), (3584, 3592), (5640, 5648)):
        col = lo
        while col < hi:
            k = col // per
            end = min(hi, (k + 1) * per)
            pieces.append((k, col - k * per, end - k * per))
            col = end
    return pieces


def regroup_w_in(wi_g):
    rows = 256
    pieces = _gathered_pieces()

    def body(x_ref, o_ref):
        cols = [x_ref[k, :, a:b] for k, a, b in pieces] + [jnp.zeros((rows, DP - D_IN), wi_g.dtype)]
        o_ref[...] = jnp.concatenate(cols, axis=1)

    return pl.pallas_call(
        body, grid=(D // rows,), in_specs=[pl.BlockSpec((N_DEV, rows, D_IN // N_DEV), lambda i: (0, i, 0))],
        out_specs=pl.BlockSpec((rows, DP), lambda i: (i, 0)), out_shape=jax.ShapeDtypeStruct((D, DP), wi_g.dtype),
        name="regroup_w_in")(wi_g)


def slots_of_w_in_grad(gw_rg, gw_ml, gw_gd, gw_sm):
    rows = 256
    per = D_IN // N_DEV

    def body(rg_ref, ml_ref, gd_ref, sm_ref, o_ref):
        sm = sm_ref[...]
        g = jnp.concatenate([rg_ref[...], ml_ref[...], sm[:, 0:2 * NH], gd_ref[...], sm[:, 2 * NH:4 * NH]], axis=1)
        for k in range(N_DEV):
            o_ref[k] = g[:, k * per:(k + 1) * per]

    return pl.pallas_call(
        body, grid=(D // rows,),
        in_specs=[pl.BlockSpec((rows, a.shape[1]), lambda i: (i, 0)) for a in (gw_rg, gw_ml, gw_gd, gw_sm)],
        out_specs=pl.BlockSpec((N_DEV, rows, per), lambda i: (0, i, 0)),
        out_shape=jax.ShapeDtypeStruct((N_DEV, D, per), gw_rg.dtype), name="slots_of_w_in_grad",
        compiler_params=_cp(40))(gw_rg, gw_ml, gw_gd, gw_sm)


class MeshComm:
    GWI_SPLIT = 448
    WI_SPLIT = 384

    def __init__(self, w_in, w_out, small_shards):
        per = D_IN // N_DEV
        self.wi_sh = [w_in[l].astype(bf16) for l in range(DEPTH)]
        self.wo_sh = [w_out[l].astype(bf16) for l in range(DEPTH)]
        self.wi_land = jax.ShapeDtypeStruct((N_DEV, D, per), bf16)
        self.wo_land = jax.ShapeDtypeStruct((N_DEV, 3 * SEG // N_DEV, D), bf16)
        packed = _pack(small_shards)
        first = Gather2([self.wi_sh[0], self.wo_sh[0], packed],
                        [self.wi_land, self.wo_land, jax.ShapeDtypeStruct((N_DEV,) + packed.shape, f32)],
                        [(0, D), (0, 3 * SEG // N_DEV), (0, packed.shape[0])])
        wi_g, wo_g, sm_g = exchange(first, "gather_first")
        self.wi_g, self.wo_g = {0: wi_g}, {0: wo_g}
        shapes = [a.shape for a in small_shards]
        parts = [_unpack(sm_g[k], shapes) for k in range(N_DEV)]
        self.small_weights = [jnp.concatenate([p[j] for p in parts], axis=-1) for j in range(len(small_shards))]
        self.gwi_land = lax.empty((N_DEV, DEPTH, D, per), bf16)
        self.gwo_land = lax.empty((N_DEV, DEPTH, 3 * SEG // N_DEV, D), bf16)
        self.gwi_slots = {}
        self.gwo_slots = {}

    def weights(self, l):
        return regroup_w_in(self.wi_g[l]), self.wo_g[l].reshape(3 * SEG, D)

    def fwd_carry(self, l, host):
        if l + 1 >= DEPTH:
            return None
        if host == "mlstm":
            return Gather2([self.wo_sh[l + 1]], [self.wo_land], [(0, 3 * SEG // N_DEV)])
        if host == "inproj":
            return Gather2([self.wi_sh[l + 1]], [self.wi_land], [(0, self.WI_SPLIT)])
        return Gather2([self.wi_sh[l + 1]], [self.wi_g[l + 1]], [(self.WI_SPLIT, D)])

    def fwd_landed(self, l, host, landed):
        (self.wo_g if host == "mlstm" else self.wi_g)[l + 1] = landed[0]

    def own_w_out_grad(self, l, g_wo):
        self.gwo_slots[l] = g_wo.reshape(N_DEV, 3 * SEG // N_DEV, D).astype(bf16)

    def bwd_carry(self, l, host):
        rows_o = 3 * SEG // N_DEV
        srcs, land_of, layer, span = [], [], [], []
        if l + 1 < DEPTH:
            if host == "rglru":
                srcs, land_of, layer, span = [self.gwo_slots[l + 1]], [1], [l + 1], [(0, rows_o)]
            elif host == "mlstm":
                srcs, land_of, layer, span = [self.gwi_slots[l + 1]], [0], [l + 1], [(0, self.GWI_SPLIT)]
            else:
                srcs, land_of, layer, span = [self.gwi_slots[l + 1]], [0], [l + 1], [(self.GWI_SPLIT, D)]
        if l == 0 and host == "mlstm":
            srcs, land_of, layer, span = srcs + [self.gwo_slots[0]], land_of + [1], layer + [0], span + [(0, rows_o)]
        if not srcs:
            return None
        return Scatter(srcs, [self.gwi_land, self.gwo_land], land_of, layer, span)

    def bwd_landed(self, landed):
        self.gwi_land, self.gwo_land = landed

    def grads_ready(self, l, gw_rg, gw_ml, gw_gd, gw_sm):
        self.gwi_slots[l] = slots_of_w_in_grad(gw_rg, gw_ml, gw_gd, gw_sm)

    def last_carry(self):
        return Scatter([self.gwi_slots[0]], [self.gwi_land, self.gwo_land], [0], [0], [(0, D)])


def local_step(xs, tgt, comm, rg_conv_full, rg_gb_full, gd_conv_full, norm_w, rg_conv_b, rg_gate_w,
               rg_lambda, ml_gate_b, ml_norm_w, gd_a_log, gd_dt_bias, gd_norm_w, final_norm_w):
    gate_dense = _block_diag(rg_gate_w)
    ml_bias_rows = _lane_rows(ml_gate_b.reshape(DEPTH, 2 * NH), 0)
    alog_rows = _lane_rows(gd_a_log, 2 * NH)
    dt_rows = _lane_rows(gd_dt_bias, 2 * NH)
    acts = []
    for l in range(DEPTH):
        nw = norm_w[l].reshape(1, D)
        w_in_l, w_out_l = comm.weights(l)
        xf = comm.fwd_carry(l, "inproj")
        proj, hn_t, *landed = inproj(xs, nw, w_in_l, xf=xf)
        if xf is not None:
            comm.fwd_landed(l, "inproj", landed)
        rg_p = (rg_conv_full[l], rg_conv_b[l].reshape(1, SEG), gate_dense[l, 0], gate_dense[l, 1],
                rg_gb_full[l], rg_lambda[l].reshape(1, SEG))
        y_rg, hs = rglru_fwd(proj, *rg_p)
        ml_p = (ml_bias_rows[l], ml_norm_w[l].reshape(1, SEG))
        xf = comm.fwd_carry(l, "mlstm")
        y_ml, cs, ns, ms, *landed = mlstm_fwd(proj, *ml_p, xf=xf)
        if xf is not None:
            comm.fwd_landed(l, "mlstm", landed)
        gd_p = (gd_conv_full[l], alog_rows[l], dt_rows[l], gd_norm_w[l].reshape(1, HD))
        xf = comm.fwd_carry(l, "gdn")
        y_gd, ss, t_invs, *landed = gdn_fwd(proj, *gd_p, xf=xf)
        if xf is not None:
            comm.fwd_landed(l, "gdn", landed)
        acts.append((xs, nw, proj, hn_t, w_in_l, w_out_l, rg_p, y_rg, hs, ml_p, y_ml, cs, ns, ms, gd_p, y_gd, ss, t_invs))
        xs = outproj(xs, y_rg, y_ml, y_gd, w_out_l)

    dx, loss_part, d_fw = head(xs, final_norm_w.reshape(1, D), tgt)

    g_small = {k: [None] * DEPTH for k in ("norm_w", "rg_conv_w", "rg_conv_b", "rg_gate_w", "rg_gate_b", "rg_lambda",
                                           "ml_gate_b", "ml_norm_w", "gd_conv_w", "gd_a_log", "gd_dt_bias", "gd_norm_w")}
    for l in reversed(range(DEPTH)):
        x_l, nw, proj, hn_t, w_in_l, w_out_l, rg_p, y_rg, hs, ml_p, y_ml, cs, ns, ms, gd_p, y_gd, ss, t_invs = acts[l]
        dy_rg, dy_ml, dy_gd, g_wo = outproj_bwd(dx, y_rg, y_ml, y_gd, w_out_l)
        comm.own_w_out_grad(l, g_wo)
        xf = comm.bwd_carry(l, "rglru")
        d_rg, d_cw, d_cb, d_gr, d_gi, d_gb, d_lam, *landed = rglru_bwd(proj, hs, dy_rg, *rg_p, xf=xf)
        if xf is not None:
            comm.bwd_landed(landed)
        xf = comm.bwd_carry(l, "mlstm")
        d_ml, d_sm_ml, d_bias, d_mnw, *landed = mlstm_bwd(proj, dy_ml, cs, ns, ms, *ml_p, xf=xf)
        if xf is not None:
            comm.bwd_landed(landed)
        xf = comm.bwd_carry(l, "gdn")
        d_gd, d_sm_gd, d_gcw, d_al, d_dt, d_gnw, *landed = gdn_bwd(proj, dy_gd, ss, t_invs, *gd_p, xf=xf)
        if xf is not None:
            comm.bwd_landed(landed)
        dx, d_nw = inproj_bwd_x(x_l, nw, w_in_l, dx, d_rg, d_ml, d_gd, d_sm_ml, d_sm_gd)
        gw_rg = wgrad(hn_t, [d_rg], "wgrad_rg")
        gw_ml = wgrad(hn_t, [d_ml], "wgrad_ml")
        gw_gd = wgrad(hn_t, [d_gd], "wgrad_gd")
        gw_sm = wgrad(hn_t, [d_sm_ml, d_sm_gd], "wgrad_small")
        comm.grads_ready(l, gw_rg, gw_ml, gw_gd, gw_sm)
        g_small["norm_w"][l] = d_nw[0]
        g_small["rg_conv_w"][l] = d_cw
        g_small["rg_conv_b"][l] = d_cb[0]
        g_small["rg_gate_w"][l] = jnp.stack([_diag_blocks(d_gr), _diag_blocks(d_gi)])
        g_small["rg_gate_b"][l] = d_gb
        g_small["rg_lambda"][l] = d_lam[0]
        g_small["ml_gate_b"][l] = d_bias[0, 0:2 * NH].reshape(2, NH)
        g_small["ml_norm_w"][l] = d_mnw[0]
        g_small["gd_conv_w"][l] = d_gcw
        g_small["gd_a_log"][l] = d_al[0, 2 * NH:3 * NH]
        g_small["gd_dt_bias"][l] = d_dt[0, 2 * NH:3 * NH]
        g_small["gd_norm_w"][l] = d_gnw[0]
    return loss_part, dx, d_fw, g_small


def finish_step(loss_part, dx, d_fw, g_small, comm, s, me, given_w, given_m, given_v):
    small_names = ["norm_w", "rg_conv_w", "rg_conv_b", "rg_gate_w", "rg_gate_b", "rg_lambda", "ml_gate_b", "ml_norm_w",
                   "gd_conv_w", "gd_a_log", "gd_dt_bias", "gd_norm_w"]
    small_list = [loss_part[0, 0:1], d_fw[0]] + [jnp.stack(g_small[k]) for k in small_names]
    small_shapes = [a.shape for a in small_list]
    packed = _pack(small_list)
    packed = jnp.pad(packed, ((0, -packed.shape[0] % (8 * N_DEV)), (0, 0)))
    summed, gwi_r, gwo_r = small_allreduce(packed, "last_exchange", xf=comm.last_carry())
    g_all = _unpack(summed, small_shapes)

    g_wi, d_wi, m_wi, v_wi = adam_slots(gwi_r, given_w["w_in"], given_m["w_in"], given_v["w_in"], 256, "adam_w_in")
    g_wo, d_wo, m_wo, v_wo = adam_slots(gwo_r, given_w["w_out"], given_m["w_out"], given_v["w_out"], 192, "adam_w_out")
    loss = g_all[0][0]
    grads = {"final_norm_w": g_all[1]}
    for k, g in zip(small_names, g_all[2:]):
        grads[k] = g
    for k, width in (("rg_conv_w", 64), ("rg_gate_b", 64), ("gd_conv_w", 192)):
        grads[k] = lax.dynamic_slice_in_dim(grads[k], me * width, width, axis=2)
    names = small_names + ["final_norm_w"]
    shapes = [given_w[k].shape for k in names]
    d_p, m_p, v_p = adam_flat(_pack([grads[k] for k in names]), _pack([given_w[k] for k in names]),
                              _pack([given_m[k] for k in names]), _pack([given_v[k] for k in names]), "adam_small")
    deltas = dict(zip(names, _unpack(d_p, shapes)))
    new_m = dict(zip(names, _unpack(m_p, shapes)))
    new_v = dict(zip(names, _unpack(v_p, shapes)))
    grads["w_in"], deltas["w_in"], new_m["w_in"], new_v["w_in"] = g_wi, d_wi, m_wi, v_wi
    grads["w_out"], deltas["w_out"], new_m["w_out"], new_v["w_out"] = g_wo, d_wo, m_wo, v_wo

    order = ["norm_w", "w_in", "rg_conv_w", "rg_conv_b", "rg_gate_w", "rg_gate_b", "rg_lambda", "ml_gate_b", "ml_norm_w",
             "gd_conv_w", "gd_a_log", "gd_dt_bias", "gd_norm_w", "w_out", "final_norm_w"]
    return (loss, dx.reshape(1, s, D), *[grads[k] for k in order], *[deltas[k] for k in order],
            *[new_m[k] for k in order], *[new_v[k] for k in order])
```

```python
import functools
from typing import NamedTuple

import jax
import jax.numpy as jnp
from jax import lax
from jax.experimental import pallas as pl
from jax.experimental.pallas import tpu as pltpu

f32 = jnp.float32
bf16 = jnp.bfloat16
MESH = pl.DeviceIdType.MESH

N_DEV = 8
D = 1024
DEPTH = 4
EPS = 1e-6
SEG = 512
HD = 128
NH = 4
LC = 64
RUN = 4
LR = RUN * LC
ML_LR = 2 * LC
D_IN = 5648
DP = 5760
SMALL0 = 5632
RG_TILE = 256
RG_C = 8.0

ADAM_LR = 0.001
ADAM_B1 = 0.9
ADAM_B2 = 0.999
ADAM_EPS = 1e-08
ADAM_WD = 0.01
ADAM_STEP = 10


def _cp(vmem_mb):
    return pltpu.CompilerParams(vmem_limit_bytes=vmem_mb * 2 ** 20)


def _dot(a, b, ca, cb):
    return lax.dot_general(a.astype(bf16), b.astype(bf16), (((ca,), (cb,)), ((), ())), preferred_element_type=f32)


@jax.custom_vjp
def mm_nn(a, b):
    return _dot(a, b, 1, 0)


@jax.custom_vjp
def mm_nt(a, b):
    return _dot(a, b, 1, 1)


@jax.custom_vjp
def mm_tn(a, b):
    return _dot(a, b, 0, 0)


mm_nn.defvjp(lambda a, b: (mm_nn(a, b), (a, b)), lambda r, g: (mm_nt(g, r[1]), mm_tn(r[0], g)))
mm_nt.defvjp(lambda a, b: (mm_nt(a, b), (a, b)), lambda r, g: (mm_nn(g, r[1]), mm_tn(g, r[0])))
mm_tn.defvjp(lambda a, b: (mm_tn(a, b), (a, b)), lambda r, g: (mm_nt(r[1], g), mm_nn(r[0], g)))


def _split(x):
    hi = x.astype(bf16)
    return hi, (x - hi.astype(f32)).astype(bf16)


def dot3(a, b):
    ah, al = _split(a)
    bh, bl = _split(b)
    d = functools.partial(jnp.dot, preferred_element_type=f32)
    return d(ah, bh) + (d(al, bh) + d(ah, bl))


def _tri_sum(x, reverse):
    n = x.shape[0]
    r = lax.broadcasted_iota(jnp.int32, (n, 3 * n), 0)
    c = lax.broadcasted_iota(jnp.int32, (n, 3 * n), 1) & (n - 1)
    ones = ((c >= r) if reverse else (c <= r)).astype(bf16)
    hi = x.astype(bf16)
    rest = x - hi.astype(f32)
    mid = rest.astype(bf16)
    lo = (rest - mid.astype(f32)).astype(bf16)
    return jnp.dot(ones, jnp.concatenate([hi, mid, lo], axis=0), preferred_element_type=f32)


@jax.custom_vjp
def cumsum_rows(x):
    return _tri_sum(x, False)


@jax.custom_vjp
def rev_cumsum_rows(x):
    return _tri_sum(x, True)


cumsum_rows.defvjp(lambda x: (cumsum_rows(x), None), lambda _, g: (rev_cumsum_rows(g),))
rev_cumsum_rows.defvjp(lambda x: (rev_cumsum_rows(x), None), lambda _, g: (cumsum_rows(g),))


def _tri(n, strict=False):
    r = lax.broadcasted_iota(jnp.int32, (n, n), 0)
    c = lax.broadcasted_iota(jnp.int32, (n, n), 1)
    return (r > c) if strict else (r >= c)


def _lane_col(v, j):
    lane = lax.broadcasted_iota(jnp.int32, v.shape, 1)
    return jnp.sum(jnp.where(lane == j, v, 0.0), axis=1, keepdims=True)


def _rows_from(pieces, rows, width):
    ridx = lax.broadcasted_iota(jnp.int32, (rows, width), 0)
    out = jnp.zeros((rows, width), f32)
    for h, p in enumerate(pieces):
        out = out + jnp.where(ridx == h, jnp.broadcast_to(p, (rows, width)), 0.0)
    return out


def causal_conv(halo8, x, w4):
    t = x.shape[0]
    xe = jnp.concatenate([halo8, x], axis=0)
    y = xe[5:5 + t] * w4[0:1]
    for k in range(1, 4):
        y = y + xe[5 + k:5 + k + t] * w4[k:k + 1]
    return y


def ml_chunk(q, k, v, o_pre, z, small, bias_row, norm_w, C, n, m):
    n_ch = q.shape[0] // LC
    lane = lax.broadcasted_iota(jnp.int32, small.shape, 1)
    pre = small + bias_row
    lg = jnp.where(lane < NH, pre, jnp.where(lane < 2 * NH, jax.nn.log_sigmoid(pre), 0.0))
    rows = [slice(c * LC, (c + 1) * LC) for c in range(n_ch)]
    lgs = [lg[r] for r in rows]
    bcs = [cumsum_rows(x) for x in lgs]
    lg_ts = [x.T for x in lgs]
    bc_ts = [x.T for x in bcs]
    causal = _tri(LC)
    pairs = [(c, h) for c in range(n_ch) for h in range(NH)]
    idx = range(len(pairs))
    sls = [slice(h * HD, (h + 1) * HD) for h in range(NH)]
    qs = [q[rows[c], sls[h]] * (HD ** -0.5) for c, h in pairs]
    ks = [k[rows[c], sls[h]] for c, h in pairs]
    vs = [v[rows[c], sls[h]] for c, h in pairs]
    li_cols = [_lane_col(lgs[c], h) for c, h in pairs]
    b_cols = [_lane_col(bcs[c], NH + h) for c, h in pairs]
    dms = [jnp.where(causal, b_cols[i] - bc_ts[c][NH + h:NH + h + 1, :] + lg_ts[c][h:h + 1, :], -jnp.inf)
           for i, (c, h) in enumerate(pairs)]
    dm_maxs = [jnp.max(dms[i], axis=-1, keepdims=True) for i in idx]
    gs = [b_cols[i][LC - 1:LC, :] for i in idx]
    wss = [gs[i] - b_cols[i] + li_cols[i] for i in idx]
    ws_maxs = [jnp.max(wss[i], axis=0, keepdims=True) for i in idx]
    qks = [mm_nt(qs[i], ks[i]) for i in idx]
    m_in = [None] * len(pairs)
    m_out = [None] * len(pairs)
    for h in range(NH):
        cur = m[h:h + 1, 0:1]
        for c in range(n_ch):
            i = c * NH + h
            m_in[i] = cur
            cur = jnp.maximum(gs[i] + cur, ws_maxs[i])
            m_out[i] = cur
    m_inters = [b_cols[i] + m_in[i] for i in idx]
    m_ts = [jnp.maximum(m_inters[i], dm_maxs[i]) for i in idx]
    ss = [qks[i] * jnp.exp(dms[i] - m_ts[i]) for i in idx]
    scs = [jnp.exp(m_inters[i] - m_ts[i]) for i in idx]
    decs = [jnp.exp(gs[i] + m_in[i] - m_out[i]) for i in idx]
    kws = [jnp.exp(wss[i] - m_out[i]) * ks[i] for i in idx]
    c_adds = [mm_tn(kws[i], vs[i]) for i in idx]
    n_adds = [jnp.sum(kws[i], axis=0, keepdims=True) for i in idx]
    svs = [mm_nn(ss[i], vs[i]) for i in idx]
    s_sums = [jnp.sum(ss[i], axis=-1, keepdims=True) for i in idx]
    c_hs = [C[sl, :] for sl in sls]
    n_hs = [n[h:h + 1, :] for h in range(NH)]
    hhs = [None] * len(pairs)
    for c in range(n_ch):
        for h in range(NH):
            i = c * NH + h
            num = svs[i] + scs[i] * mm_nn(qs[i], c_hs[h])
            den = s_sums[i] + scs[i] * jnp.sum(qs[i] * n_hs[h], axis=-1, keepdims=True)
            hhs[i] = num / jnp.maximum(jnp.abs(den), jnp.exp(-m_ts[i]))
        c_hs = [decs[c * NH + h] * c_hs[h] + c_adds[c * NH + h] for h in range(NH)]
        n_hs = [decs[c * NH + h] * n_hs[h] + n_adds[c * NH + h] for h in range(NH)]
    ys = [hhs[i] * lax.rsqrt(jnp.mean(hhs[i] * hhs[i], axis=-1, keepdims=True) + EPS) * norm_w[:, sls[h]]
          * jax.nn.sigmoid(o_pre[rows[c], sls[h]]) * jax.nn.silu(z[rows[c], sls[h]]) for i, (c, h) in enumerate(pairs)]
    y = jnp.concatenate([jnp.concatenate(ys[c * NH:(c + 1) * NH], axis=1) for c in range(n_ch)], axis=0)
    last = (n_ch - 1) * NH
    m_rows = [jnp.broadcast_to(m_out[last + h], (1, HD)) for h in range(NH)]
    return y, jnp.concatenate(c_hs, axis=0), _rows_from(n_hs, 8, HD), _rows_from(m_rows, 8, HD)


@jax.custom_vjp
def _unit_lower_inverses(mats):
    eye = (lax.broadcasted_iota(jnp.int32, (LC, LC), 0) == lax.broadcasted_iota(jnp.int32, (LC, LC), 1)).astype(f32)
    ps = [-m for m in mats]
    ts = [eye + p for p in ps]
    for _ in range(5):
        ps = [dot3(p, p) for p in ps]
        ts = [t + dot3(t, p) for t, p in zip(ts, ps)]
    return tuple(ts)


def _unit_lower_inverses_fwd(mats):
    ts = _unit_lower_inverses(mats)
    return ts, ts


def _unit_lower_inverses_bwd(ts, gs):
    tts = [t.T for t in ts]
    mid = [dot3(tt, g) for tt, g in zip(tts, gs)]
    return (tuple(-dot3(m, tt) for m, tt in zip(mid, tts)),)


_unit_lower_inverses.defvjp(_unit_lower_inverses_fwd, _unit_lower_inverses_bwd)


@jax.custom_vjp
def _known_inverses(mats, ts):
    return tuple(ts)


_known_inverses.defvjp(lambda mats, ts: (tuple(ts), tuple(ts)),
                       lambda ts, gs: (_unit_lower_inverses_bwd(ts, gs)[0], tuple(jnp.zeros_like(t) for t in ts)))


def gd_chunk(qh8, q, kh8, k, vh8, v, z, small, conv_w, alog_row, dt_row, norm_w, st, known_t=None):
    n_ch = q.shape[0] // LC
    lane = lax.broadcasted_iota(jnp.int32, small.shape, 1)
    is_a = (lane >= 2 * NH) & (lane < 3 * NH)
    g_all = jnp.where(is_a, -jnp.exp(alog_row) * jax.nn.softplus(small + dt_row), 0.0)
    beta_all = jax.nn.sigmoid(small)
    qc = jax.nn.silu(causal_conv(qh8, q, conv_w[:, 0:SEG]))
    kc = jax.nn.silu(causal_conv(kh8, k, conv_w[:, SEG:2 * SEG]))
    vc = jax.nn.silu(causal_conv(vh8, v, conv_w[:, 2 * SEG:3 * SEG]))
    incl = _tri(LC)
    strict = _tri(LC, strict=True)
    rows = [slice(c * LC, (c + 1) * LC) for c in range(n_ch)]
    gc_alls = [cumsum_rows(g_all[r]) for r in rows]
    gc_ts = [g.T for g in gc_alls]
    pairs = [(c, h) for c in range(n_ch) for h in range(NH)]
    idx = range(len(pairs))
    sls = [slice(h * HD, (h + 1) * HD) for h in range(NH)]
    qs = [qc[rows[c], sls[h]] for c, h in pairs]
    ks = [kc[rows[c], sls[h]] for c, h in pairs]
    vs = [vc[rows[c], sls[h]] for c, h in pairs]
    qs = [x * lax.rsqrt(jnp.sum(x * x, axis=-1, keepdims=True) + EPS) * (HD ** -0.5) for x in qs]
    ks = [x * lax.rsqrt(jnp.sum(x * x, axis=-1, keepdims=True) + EPS) for x in ks]
    betas = [_lane_col(beta_all[rows[c]], 3 * NH + h) for c, h in pairs]
    gcs = [_lane_col(gc_alls[c], 2 * NH + h) for c, h in pairs]
    gams = [jnp.exp(jnp.where(incl, gcs[i] - gc_ts[c][2 * NH + h:2 * NH + h + 1, :], -jnp.inf))
            for i, (c, h) in enumerate(pairs)]
    kbs = [ks[i] * betas[i] for i in idx]
    mats = tuple(jnp.where(strict, mm_nt(kbs[i], ks[i]) * gams[i], 0.0) for i in idx)
    aqks = [mm_nt(qs[i], ks[i]) * gams[i] for i in idx]
    if known_t is None:
        t_invs = _unit_lower_inverses(mats)
    else:
        t_invs = _known_inverses(mats, tuple(known_t[i] for i in idx))
    us = [mm_nn(t_invs[i], vs[i] * betas[i]) for i in idx]
    ws = [mm_nn(t_invs[i], kbs[i] * jnp.exp(gcs[i])) for i in idx]
    g_lasts = [gcs[i][LC - 1:LC, :] for i in idx]
    q_decs = [qs[i] * jnp.exp(gcs[i]) for i in idx]
    k_decs = [ks[i] * jnp.exp(g_lasts[i] - gcs[i]) for i in idx]
    e_lasts = [jnp.exp(g_lasts[i]) for i in idx]
    s_hs = [st[sl, :] for sl in sls]
    os_ = [None] * len(pairs)
    for c in range(n_ch):
        ids = [c * NH + h for h in range(NH)]
        v_news = [us[i] - mm_nn(ws[i], s_hs[h]) for h, i in enumerate(ids)]
        for h, i in enumerate(ids):
            os_[i] = mm_nn(q_decs[i], s_hs[h]) + mm_nn(aqks[i], v_news[h])
        s_hs = [s_hs[h] * e_lasts[i] + mm_tn(k_decs[i], v_news[h]) for h, i in enumerate(ids)]
    ys = [os_[i] * lax.rsqrt(jnp.mean(os_[i] * os_[i], axis=-1, keepdims=True) + EPS) * norm_w
          * jax.nn.silu(z[rows[c], sls[h]]) for i, (c, h) in enumerate(pairs)]
    y = jnp.concatenate([jnp.concatenate(ys[c * NH:(c + 1) * NH], axis=1) for c in range(n_ch)], axis=0)
    if known_t is None:
        return y, jnp.concatenate(s_hs, axis=0), jnp.stack(t_invs)
    return y, jnp.concatenate(s_hs, axis=0)


def rg_pre(xh8, x, conv_w, conv_b, gw_r, gw_i, gate_b, lam):
    xc = causal_conv(xh8, x, conv_w) + conv_b
    r = jax.nn.sigmoid(mm_nn(xc, gw_r) + gate_b[0:1])
    i = jax.nn.sigmoid(mm_nn(xc, gw_i) + gate_b[1:2])
    log_a = -RG_C * r * jax.nn.softplus(-lam)
    a = jnp.exp(log_a)
    th = jnp.tanh(log_a)
    one_minus_a2 = -2.0 * th / (1.0 - th)
    b = jnp.sqrt(one_minus_a2) * (i * xc)
    return a, b


def _scan_rows(a_ref, b_ref, out_ref, h0, n_rows, reverse):
    n_groups = n_rows // 8
    width = a_ref.shape[1]
    row = lax.broadcasted_iota(jnp.int32, (8, width), 0)

    def body(j, h):
        g = (n_groups - 1 - j) if reverse else j
        r0 = pl.multiple_of(g * 8, 8)
        av = a_ref[pl.ds(r0, 8), :]
        bv = b_ref[pl.ds(r0, 8), :]
        for d in (1, 2, 4):
            sh = (8 - d) if reverse else d
            a_s = pltpu.roll(av, sh, 0)
            b_s = pltpu.roll(bv, sh, 0)
            valid = (row < 8 - d) if reverse else (row >= d)
            bv = jnp.where(valid, av * b_s + bv, bv)
            av = jnp.where(valid, av * a_s, av)
        hv = av * h + bv
        out_ref[pl.ds(r0, 8), :] = hv
        return hv[0:1, :] if reverse else hv[7:8, :]

    return lax.fori_loop(0, n_groups, body, h0)


def _norm_rows(xv):
    r = lax.rsqrt(jnp.mean(xv * xv, axis=-1, keepdims=True) + EPS)
    return xv * r, r


def inproj(x, nw, w, xf=None):
    s, ts = x.shape[0], 256
    nt = s // ts
    n_in, n_out = 3, 2
    x_ops, x_in_specs, x_out_shapes, x_out_specs, aliases, x_sems, n_xin, n_land = _carried(xf, n_in, n_out)

    def body(*refs):
        (x_ref, nw_ref, w_ref), x_refs, (o_ref, ht_ref), land_refs, _, sem_refs = _cut_refs(refs, n_in, n_xin, n_out, n_land, 0)
        i = pl.program_id(0)
        _carry_open(xf, i, nt, x_refs, land_refs, sem_refs)
        xn, _ = _norm_rows(x_ref[...])
        hn = xn * nw_ref[...]
        o_ref[...] = jnp.dot(hn.astype(bf16), w_ref[...], preferred_element_type=f32)
        ht_ref[...] = hn.T.astype(bf16)
        _carry_close(xf, i, nt, x_refs, land_refs, sem_refs)

    return pl.pallas_call(
        body, grid=(nt,),
        in_specs=[pl.BlockSpec((ts, D), lambda i: (i, 0)), pl.BlockSpec((1, D), lambda i: (0, 0)),
                  pl.BlockSpec((D, DP), lambda i: (0, 0))] + x_in_specs,
        out_specs=[pl.BlockSpec((ts, DP), lambda i: (i, 0)), pl.BlockSpec((D, ts), lambda i: (0, i))] + x_out_specs,
        out_shape=[jax.ShapeDtypeStruct((s, DP), f32), jax.ShapeDtypeStruct((D, s), bf16)] + x_out_shapes,
        scratch_shapes=x_sems, input_output_aliases=aliases,
        name="inproj" if xf is None else "inproj_carrying", compiler_params=_cp(56))(x, nw, w, *x_ops)


def outproj(x, yr, ym, yg, wo):
    s, ts = x.shape[0], 256

    def body(x_ref, yr_ref, ym_ref, yg_ref, wo_ref, o_ref):
        acc = x_ref[...]
        for j, y_ref in enumerate((yr_ref, ym_ref, yg_ref)):
            acc = acc + jnp.dot(y_ref[...].astype(bf16), wo_ref[j * SEG:(j + 1) * SEG, :], preferred_element_type=f32)
        o_ref[...] = acc

    yspec = pl.BlockSpec((ts, SEG), lambda i: (i, 0))
    return pl.pallas_call(
        body, grid=(s // ts,),
        in_specs=[pl.BlockSpec((ts, D), lambda i: (i, 0)), yspec, yspec, yspec,
                  pl.BlockSpec((3 * SEG, D), lambda i: (0, 0))],
        out_specs=pl.BlockSpec((ts, D), lambda i: (i, 0)),
        out_shape=jax.ShapeDtypeStruct((s, D), f32), name="outproj", compiler_params=_cp(40))(x, yr, ym, yg, wo)


def head(x, fw, tgt):
    s, ts = x.shape[0], 256

    def body(x_ref, fw_ref, t_ref, dx_ref, loss_ref, dfw_ref):
        @pl.when(pl.program_id(0) == 0)
        def _():
            loss_ref[...] = jnp.zeros_like(loss_ref)
            dfw_ref[...] = jnp.zeros_like(dfw_ref)

        xn, r = _norm_rows(x_ref[...])
        fw_v = fw_ref[...]
        err = xn * fw_v - t_ref[...]
        loss_ref[...] += 0.5 * jnp.sum(jnp.mean(err * err, axis=-1, keepdims=True))
        dy = err * (1.0 / D)
        dfw_ref[0:1, :] += jnp.sum(dy * xn, axis=0, keepdims=True)
        dxn = dy * fw_v
        dx_ref[...] = r * (dxn - xn * jnp.mean(dxn * xn, axis=-1, keepdims=True))

    tile = pl.BlockSpec((ts, D), lambda i: (i, 0))
    return pl.pallas_call(
        body, grid=(s // ts,),
        in_specs=[tile, pl.BlockSpec((1, D), lambda i: (0, 0)), tile],
        out_specs=[tile, pl.BlockSpec((8, 128), lambda i: (0, 0)), pl.BlockSpec((8, D), lambda i: (0, 0))],
        out_shape=[jax.ShapeDtypeStruct((s, D), f32), jax.ShapeDtypeStruct((8, 128), f32),
                   jax.ShapeDtypeStruct((8, D), f32)], name="head")(x, fw, tgt)


def outproj_bwd(dx, yr, ym, yg, wo):
    s, ts = dx.shape[0], 256

    def body(dx_ref, yr_ref, ym_ref, yg_ref, wo_ref, dyr_ref, dym_ref, dyg_ref, gwo_ref):
        @pl.when(pl.program_id(0) == 0)
        def _():
            gwo_ref[...] = jnp.zeros_like(gwo_ref)

        dxb = dx_ref[...].astype(bf16)
        for j, (y_ref, dy_ref) in enumerate(((yr_ref, dyr_ref), (ym_ref, dym_ref), (yg_ref, dyg_ref))):
            rows = slice(j * SEG, (j + 1) * SEG)
            dy_ref[...] = lax.dot_general(dxb, wo_ref[rows, :], (((1,), (1,)), ((), ())), preferred_element_type=f32)
            gwo_ref[rows, :] += lax.dot_general(y_ref[...].astype(bf16), dxb, (((0,), (0,)), ((), ())),
                                                preferred_element_type=f32)

    yspec = pl.BlockSpec((ts, SEG), lambda i: (i, 0))
    wspec = pl.BlockSpec((3 * SEG, D), lambda i: (0, 0))
    ysh = jax.ShapeDtypeStruct((s, SEG), f32)
    return pl.pallas_call(
        body, grid=(s // ts,),
        in_specs=[pl.BlockSpec((ts, D), lambda i: (i, 0)), yspec, yspec, yspec, wspec],
        out_specs=[yspec, yspec, yspec, wspec],
        out_shape=[ysh, ysh, ysh, jax.ShapeDtypeStruct((3 * SEG, D), f32)],
        name="outproj_bwd", compiler_params=_cp(48))(dx, yr, ym, yg, wo)


def inproj_bwd_x(x, nw, w, dxo, d_rg, d_ml, d_gd, d_sa, d_sb):
    s, ts = x.shape[0], 256
    widths = (d_rg.shape[1], d_ml.shape[1], d_gd.shape[1], HD)

    def body(x_ref, nw_ref, w_ref, dxo_ref, rg_ref, ml_ref, gd_ref, sa_ref, sb_ref, dx_ref, dnw_ref):
        @pl.when(pl.program_id(0) == 0)
        def _():
            dnw_ref[...] = jnp.zeros_like(dnw_ref)

        xn, r = _norm_rows(x_ref[...])
        pieces = (rg_ref[...], ml_ref[...], gd_ref[...], sa_ref[...] + sb_ref[...])
        dhn = jnp.zeros((ts, D), f32)
        c0 = 0
        for piece, wd in zip(pieces, widths):
            dhn = dhn + lax.dot_general(piece.astype(bf16), w_ref[:, c0:c0 + wd], (((1,), (1,)), ((), ())),
                                        preferred_element_type=f32)
            c0 += wd
        dnw_ref[0:1, :] += jnp.sum(dhn * xn, axis=0, keepdims=True)
        dxn = dhn * nw_ref[...]
        dx_ref[...] = dxo_ref[...] + r * (dxn - xn * jnp.mean(dxn * xn, axis=-1, keepdims=True))

    tile = pl.BlockSpec((ts, D), lambda i: (i, 0))
    return pl.pallas_call(
        body, grid=(s // ts,),
        in_specs=[tile, pl.BlockSpec((1, D), lambda i: (0, 0)), pl.BlockSpec((D, DP), lambda i: (0, 0)), tile]
        + [pl.BlockSpec((ts, wd), lambda i: (i, 0)) for wd in widths] + [pl.BlockSpec((ts, HD), lambda i: (i, 0))],
        out_specs=[tile, pl.BlockSpec((8, D), lambda i: (0, 0))],
        out_shape=[jax.ShapeDtypeStruct((s, D), f32), jax.ShapeDtypeStruct((8, D), f32)],
        name="inproj_bwd_x", compiler_params=_cp(56))(x, nw, w, dxo, d_rg, d_ml, d_gd, d_sa, d_sb)


def wgrad(hn_t, dps, name):
    s = hn_t.shape[1]
    c = dps[0].shape[1]
    ct = min(c, SEG)
    n_dp = len(dps)

    def body(*refs):
        ht_ref = refs[0]
        dp_refs = refs[1:1 + n_dp]
        o_ref = refs[1 + n_dp]
        dp = dp_refs[0][...]
        for extra in dp_refs[1:]:
            dp = dp + extra[...]
        o_ref[...] = jnp.dot(ht_ref[...], dp.astype(bf16), preferred_element_type=f32).astype(bf16)

    return pl.pallas_call(
        body, grid=(c // ct,),
        in_specs=[pl.BlockSpec((D, s), lambda j: (0, 0))] + [pl.BlockSpec((s, ct), lambda j: (0, j)) for _ in dps],
        out_specs=pl.BlockSpec((D, ct), lambda j: (0, j)),
        out_shape=jax.ShapeDtypeStruct((D, c), bf16), name=name, compiler_params=_cp(40))(hn_t, *dps)


def _seg_spec(rows, seg, n_tiles=None):
    if n_tiles is None:
        return pl.BlockSpec((rows, SEG), lambda i: (i, seg))
    return pl.BlockSpec((rows, SEG), lambda i: (n_tiles - 1 - i, seg))


def _halo_spec(rows, seg, n_tiles=None):
    per = rows // 8
    if n_tiles is None:
        return pl.BlockSpec((8, SEG), lambda i: (jnp.maximum(i * per - 1, 0), seg))
    return pl.BlockSpec((8, SEG), lambda i: (jnp.maximum((n_tiles - 1 - i) * per - 1, 0), seg))


def _const_spec(shape):
    return pl.BlockSpec(shape, lambda i: tuple(0 for _ in shape))


def rglru_fwd(proj, conv_w, conv_b, gw_r, gw_i, gate_b, lam):
    s = proj.shape[0]
    tr = RG_TILE

    def body(xh_ref, x_ref, z_ref, cw_ref, cb_ref, gr_ref, gi_ref, gb_ref, lam_ref, y_ref, h_ref, a_s, b_s, hc):
        first = pl.program_id(0) == 0

        @pl.when(first)
        def _():
            hc[...] = jnp.zeros_like(hc)

        xh = jnp.where(first, 0.0, xh_ref[...])
        a, b = rg_pre(xh, x_ref[...], cw_ref[...], cb_ref[...], gr_ref[...], gi_ref[...], gb_ref[...], lam_ref[...])
        a_s[...] = a
        b_s[...] = b
        hc[0:1, :] = _scan_rows(a_s, b_s, h_ref, hc[0:1, :], tr, False)
        y_ref[...] = (h_ref[...] * jax.nn.silu(z_ref[...])).astype(bf16)

    out = pl.BlockSpec((tr, SEG), lambda i: (i, 0))
    return pl.pallas_call(
        body, grid=(s // tr,),
        in_specs=[_halo_spec(tr, 0), _seg_spec(tr, 0), _seg_spec(tr, 1), _const_spec((4, SEG)), _const_spec((1, SEG)),
                  _const_spec((SEG, SEG)), _const_spec((SEG, SEG)), _const_spec((2, SEG)), _const_spec((1, SEG))],
        out_specs=[out, out],
        out_shape=[jax.ShapeDtypeStruct((s, SEG), bf16), jax.ShapeDtypeStruct((s, SEG), f32)],
        scratch_shapes=[pltpu.VMEM((tr, SEG), f32), pltpu.VMEM((tr, SEG), f32), pltpu.VMEM((8, SEG), f32)],
        name="rglru_fwd", compiler_params=_cp(40))(proj, proj, proj, conv_w, conv_b, gw_r, gw_i, gate_b, lam)


def rglru_bwd(proj, hs, dy, conv_w, conv_b, gw_r, gw_i, gate_b, lam, xf=None):
    s = proj.shape[0]
    tr = RG_TILE
    nt = s // tr
    n_in, n_out = 12, 7
    x_ops, x_in_specs, x_out_shapes, x_out_specs, aliases, x_sems, n_xin, n_land = _carried(xf, n_in, n_out)

    def body(*refs):
        ins, x_refs, outs, land_refs, scratch, sem_refs = _cut_refs(refs, n_in, n_xin, n_out, n_land, 6)
        xh_ref, x_ref, z_ref, hh_ref, h_ref, dy_ref, cw_ref, cb_ref, gr_ref, gi_ref, gb_ref, lam_ref = ins
        dp_ref, dcw_ref, dcb_ref, dgr_ref, dgi_ref, dgb_ref, dlam_ref = outs
        an_s, g_s, dh_s, a_first, dh_first, dhalo = scratch
        i = pl.program_id(0)
        first_tile = i == nt - 1
        _carry_open(xf, i, nt, x_refs, land_refs, sem_refs)

        @pl.when(i == 0)
        def _():
            for ref in (dcw_ref, dcb_ref, dgr_ref, dgi_ref, dgb_ref, dlam_ref, a_first, dh_first, dhalo):
                ref[...] = jnp.zeros_like(ref)

        xh = jnp.where(first_tile, 0.0, xh_ref[...])
        params = (cw_ref[...], cb_ref[...], gr_ref[...], gi_ref[...], gb_ref[...], lam_ref[...])
        (a, _), vjp = jax.vjp(rg_pre, xh, x_ref[...], *params)
        zv = z_ref[...]
        hv = h_ref[...]
        dyv = dy_ref[...]
        sig = jax.nn.sigmoid(zv)
        g_s[...] = dyv * (zv * sig)
        dp_ref[:, SEG:2 * SEG] = (dyv * hv * (sig * (1.0 + zv * (1.0 - sig)))).astype(bf16)
        ridx = lax.broadcasted_iota(jnp.int32, (tr, SEG), 0)
        an_s[...] = jnp.where(ridx == tr - 1, jnp.broadcast_to(a_first[0:1, :], (tr, SEG)), pltpu.roll(a, tr - 1, 0))
        _scan_rows(an_s, g_s, dh_s, dh_first[0:1, :], tr, True)
        dh = dh_s[...]
        h_prev_last = jnp.where(first_tile, 0.0, hh_ref[...])[7:8, :]
        h_prev = pltpu.roll(hv, 1, 0)
        h_prev = jnp.where(ridx == 0, jnp.broadcast_to(h_prev_last, (tr, SEG)), h_prev)
        dxh, dx, dcw, dcb, dgr, dgi, dgb, dlam = vjp((dh * h_prev, dh))
        dx = dx + jnp.concatenate([jnp.zeros((tr - 8, SEG), f32), dhalo[...]], axis=0)
        dp_ref[:, 0:SEG] = dx.astype(bf16)
        dhalo[...] = dxh
        a_first[0:1, :] = a[0:1, :]
        dh_first[0:1, :] = dh[0:1, :]
        dcw_ref[...] += dcw
        dcb_ref[...] += dcb
        dgr_ref[...] += dgr
        dgi_ref[...] += dgi
        dgb_ref[...] += dgb
        dlam_ref[...] += dlam
        _carry_close(xf, i, nt, x_refs, land_refs, sem_refs)

    pspecs = [_const_spec((4, SEG)), _const_spec((1, SEG)), _const_spec((SEG, SEG)), _const_spec((SEG, SEG)),
              _const_spec((2, SEG)), _const_spec((1, SEG))]
    pshapes = [jax.ShapeDtypeStruct(sh, f32) for sh in ((4, SEG), (1, SEG), (SEG, SEG), (SEG, SEG), (2, SEG), (1, SEG))]
    tile = pl.BlockSpec((tr, SEG), lambda i: (nt - 1 - i, 0))
    return pl.pallas_call(
        body, grid=(nt,),
        in_specs=[_halo_spec(tr, 0, nt), _seg_spec(tr, 0, nt), _seg_spec(tr, 1, nt),
                  pl.BlockSpec((8, SEG), lambda i: (jnp.maximum((nt - 1 - i) * (tr // 8) - 1, 0), 0)), tile, tile] + pspecs
        + x_in_specs,
        out_specs=[pl.BlockSpec((tr, 2 * SEG), lambda i: (nt - 1 - i, 0))] + pspecs + x_out_specs,
        out_shape=[jax.ShapeDtypeStruct((s, 2 * SEG), bf16)] + pshapes + x_out_shapes,
        scratch_shapes=[pltpu.VMEM((tr, SEG), f32), pltpu.VMEM((tr, SEG), f32), pltpu.VMEM((tr, SEG), f32),
                        pltpu.VMEM((8, SEG), f32), pltpu.VMEM((8, SEG), f32), pltpu.VMEM((8, SEG), f32)] + x_sems,
        input_output_aliases=aliases, name="rglru_bwd" if xf is None else "rglru_bwd_carrying", compiler_params=_cp(48))(
            proj, proj, proj, hs, hs, dy, conv_w, conv_b, gw_r, gw_i, gate_b, lam, *x_ops)


ML_SEGS = (2, 3, 4, 5, 6)
SMALL_BLK = SMALL0 // HD


def _cut_refs(refs, n_in, n_xin, n_out, n_land, n_scratch):
    bounds = [0, n_in, n_in + n_xin, n_in + n_xin + n_out, n_in + n_xin + n_out + n_land,
              n_in + n_xin + n_out + n_land + n_scratch, len(refs)]
    return [refs[a:b] for a, b in zip(bounds[:-1], bounds[1:])]


def mlstm_fwd(proj, bias_row, norm_w, xf=None):
    s = proj.shape[0]
    nc = s // ML_LR
    n_in, n_out = 8, 4
    x_ops, x_in_specs, x_out_shapes, x_out_specs, aliases, x_sems, n_xin, n_land = _carried(xf, n_in, n_out)

    def body(*refs):
        ins, x_refs, outs, land_refs, scratch, sem_refs = _cut_refs(refs, n_in, n_xin, n_out, n_land, 3)
        q_ref, k_ref, v_ref, o_ref, z_ref, sm_ref, b_ref, nw_ref = ins
        y_ref, cs_ref, ns_ref, ms_ref = outs
        c_s, n_s, m_s = scratch
        i = pl.program_id(0)
        _carry_open(xf, i, nc, x_refs, land_refs, sem_refs)

        @pl.when(i == 0)
        def _():
            c_s[...] = jnp.zeros_like(c_s)
            n_s[...] = jnp.zeros_like(n_s)
            m_s[...] = jnp.zeros_like(m_s)

        cs_ref[0] = c_s[...]
        ns_ref[0] = n_s[...]
        ms_ref[0] = m_s[...]
        y, c2, n2, m2 = ml_chunk(q_ref[...], k_ref[...], v_ref[...], o_ref[...], z_ref[...], sm_ref[...],
                                 b_ref[...], nw_ref[...], c_s[...], n_s[...], m_s[...])
        y_ref[...] = y.astype(bf16)
        c_s[...] = c2
        n_s[...] = n2
        m_s[...] = m2
        _carry_close(xf, i, nc, x_refs, land_refs, sem_refs)

    return pl.pallas_call(
        body, grid=(nc,),
        in_specs=[_seg_spec(ML_LR, sg) for sg in ML_SEGS]
        + [pl.BlockSpec((ML_LR, HD), lambda i: (i, SMALL_BLK)), _const_spec((1, HD)), _const_spec((1, SEG))] + x_in_specs,
        out_specs=[pl.BlockSpec((ML_LR, SEG), lambda i: (i, 0)), pl.BlockSpec((1, SEG, HD), lambda i: (i, 0, 0)),
                   pl.BlockSpec((1, 8, HD), lambda i: (i, 0, 0)), pl.BlockSpec((1, 8, HD), lambda i: (i, 0, 0))] + x_out_specs,
        out_shape=[jax.ShapeDtypeStruct((s, SEG), bf16), jax.ShapeDtypeStruct((nc, SEG, HD), f32),
                   jax.ShapeDtypeStruct((nc, 8, HD), f32), jax.ShapeDtypeStruct((nc, 8, HD), f32)] + x_out_shapes,
        scratch_shapes=[pltpu.VMEM((SEG, HD), f32), pltpu.VMEM((8, HD), f32), pltpu.VMEM((8, HD), f32)] + x_sems,
        input_output_aliases=aliases, name="mlstm_fwd" if xf is None else "mlstm_fwd_carrying")(
            proj, proj, proj, proj, proj, proj, bias_row, norm_w, *x_ops)


def mlstm_bwd(proj, dy, cs, ns, ms, bias_row, norm_w, xf=None):
    s = proj.shape[0]
    nc = s // ML_LR
    n_in, n_out = 12, 4
    x_ops, x_in_specs, x_out_shapes, x_out_specs, aliases, x_sems, n_xin, n_land = _carried(xf, n_in, n_out)

    def body(*refs):
        ins, x_refs, outs, land_refs, scratch, sem_refs = _cut_refs(refs, n_in, n_xin, n_out, n_land, 3)
        q_ref, k_ref, v_ref, o_ref, z_ref, sm_ref, dy_ref, cs_ref, ns_ref, ms_ref, b_ref, nw_ref = ins
        dp_ref, dsm_ref, db_ref, dnw_ref = outs
        dc_s, dn_s, dm_s = scratch
        i = pl.program_id(0)
        _carry_open(xf, i, nc, x_refs, land_refs, sem_refs)

        @pl.when(i == 0)
        def _():
            for ref in (db_ref, dnw_ref, dc_s, dn_s, dm_s):
                ref[...] = jnp.zeros_like(ref)

        _, vjp = jax.vjp(ml_chunk, q_ref[...], k_ref[...], v_ref[...], o_ref[...], z_ref[...], sm_ref[...],
                         b_ref[...], nw_ref[...], cs_ref[0], ns_ref[0], ms_ref[0])
        dq, dk, dv, do, dz, dsm, db, dnw, dc, dn, dm = vjp((dy_ref[...], dc_s[...], dn_s[...], dm_s[...]))
        for j, val in enumerate((dq, dk, dv, do, dz)):
            dp_ref[:, j * SEG:(j + 1) * SEG] = val.astype(bf16)
        dsm_ref[...] = dsm
        db_ref[0:1, :] += db
        dnw_ref[0:1, :] += dnw
        dc_s[...] = dc
        dn_s[...] = dn
        dm_s[...] = dm
        _carry_close(xf, i, nc, x_refs, land_refs, sem_refs)

    rev3 = lambda i: (nc - 1 - i, 0, 0)
    return pl.pallas_call(
        body, grid=(nc,),
        in_specs=[_seg_spec(ML_LR, sg, nc) for sg in ML_SEGS]
        + [pl.BlockSpec((ML_LR, HD), lambda i: (nc - 1 - i, SMALL_BLK)), pl.BlockSpec((ML_LR, SEG), lambda i: (nc - 1 - i, 0)),
           pl.BlockSpec((1, SEG, HD), rev3), pl.BlockSpec((1, 8, HD), rev3), pl.BlockSpec((1, 8, HD), rev3),
           _const_spec((1, HD)), _const_spec((1, SEG))] + x_in_specs,
        out_specs=[pl.BlockSpec((ML_LR, 5 * SEG), lambda i: (nc - 1 - i, 0)), pl.BlockSpec((ML_LR, HD), lambda i: (nc - 1 - i, 0)),
                   _const_spec((8, HD)), _const_spec((8, SEG))] + x_out_specs,
        out_shape=[jax.ShapeDtypeStruct((s, 5 * SEG), bf16), jax.ShapeDtypeStruct((s, HD), f32),
                   jax.ShapeDtypeStruct((8, HD), f32), jax.ShapeDtypeStruct((8, SEG), f32)] + x_out_shapes,
        scratch_shapes=[pltpu.VMEM((SEG, HD), f32), pltpu.VMEM((8, HD), f32), pltpu.VMEM((8, HD), f32)] + x_sems,
        input_output_aliases=aliases, name="mlstm_bwd" if xf is None else "mlstm_bwd_carrying", compiler_params=_cp(48))(
            proj, proj, proj, proj, proj, proj, dy, cs, ns, ms, bias_row, norm_w, *x_ops)


GD_SEGS = (7, 8, 9)


def _carried(xf, n_in, n_out):
    operands, out_shapes, aliases, sems, _, n_xin, n_land = _carry_plumb(xf, n_in, n_out)
    return operands, [_ANY] * n_xin, out_shapes, [_ANY] * n_land, aliases, sems, n_xin, n_land


def _carry_open(xf, i, n_steps, x_refs, land_refs, sem_refs):
    if xf is None:
        return
    srcs = x_refs[:len(xf.srcs)]

    @pl.when(i == 0)
    def _():
        _carry_start(xf, srcs, land_refs, sem_refs)

    @pl.when(i == n_steps - 1)
    def _():
        _carry_middle(xf, srcs, land_refs, sem_refs)


def _carry_close(xf, i, n_steps, x_refs, land_refs, sem_refs):
    if xf is None:
        return

    @pl.when(i == n_steps - 1)
    def _():
        _carry_finish(xf, x_refs[:len(xf.srcs)], land_refs, sem_refs)


def gdn_fwd(proj, conv_w, alog_row, dt_row, norm_w, xf=None):
    s = proj.shape[0]
    nc = s // LR
    n_in, n_out = 12, 3
    x_ops, x_in_specs, x_out_shapes, x_out_specs, aliases, x_sems, n_xin, n_land = _carried(xf, n_in, n_out)

    def body(*refs):
        qh_ref, q_ref, kh_ref, k_ref, vh_ref, v_ref, z_ref, sm_ref, cw_ref, al_ref, dt_ref, nw_ref = refs[:n_in]
        x_refs = refs[n_in:n_in + n_xin]
        y_ref, ss_ref, ti_ref = refs[n_in + n_xin:n_in + n_xin + n_out]
        land_refs = refs[n_in + n_xin + n_out:n_in + n_xin + n_out + n_land]
        st_s = refs[n_in + n_xin + n_out + n_land]
        sem_refs = refs[n_in + n_xin + n_out + n_land + 1:]
        i = pl.program_id(0)
        first = i == 0
        _carry_open(xf, i, nc, x_refs, land_refs, sem_refs)

        @pl.when(first)
        def _():
            st_s[...] = jnp.zeros_like(st_s)

        ss_ref[0] = st_s[...]
        halo = [jnp.where(first, 0.0, r[...]) for r in (qh_ref, kh_ref, vh_ref)]
        y, st2, t_invs = gd_chunk(halo[0], q_ref[...], halo[1], k_ref[...], halo[2], v_ref[...], z_ref[...], sm_ref[...],
                                  cw_ref[...], al_ref[...], dt_ref[...], nw_ref[...], st_s[...])
        y_ref[...] = y.astype(bf16)
        ti_ref[...] = t_invs
        st_s[...] = st2
        _carry_close(xf, i, nc, x_refs, land_refs, sem_refs)

    qkv_specs = []
    for sg in GD_SEGS:
        qkv_specs += [_halo_spec(LR, sg), _seg_spec(LR, sg)]
    return pl.pallas_call(
        body, grid=(nc,),
        in_specs=qkv_specs + [_seg_spec(LR, 10), pl.BlockSpec((LR, HD), lambda i: (i, SMALL_BLK)),
                              _const_spec((4, 3 * SEG)), _const_spec((1, HD)), _const_spec((1, HD)), _const_spec((1, HD))]
        + x_in_specs,
        out_specs=[pl.BlockSpec((LR, SEG), lambda i: (i, 0)), pl.BlockSpec((1, SEG, HD), lambda i: (i, 0, 0)),
                   pl.BlockSpec((RUN * NH, LC, LC), lambda i: (i, 0, 0))] + x_out_specs,
        out_shape=[jax.ShapeDtypeStruct((s, SEG), bf16), jax.ShapeDtypeStruct((nc, SEG, HD), f32),
                   jax.ShapeDtypeStruct((s // LC * NH, LC, LC), f32)] + x_out_shapes,
        scratch_shapes=[pltpu.VMEM((SEG, HD), f32)] + x_sems, input_output_aliases=aliases,
        name="gdn_fwd" if xf is None else "gdn_fwd_carrying")(
            proj, proj, proj, proj, proj, proj, proj, proj, conv_w, alog_row, dt_row, norm_w, *x_ops)


def gdn_bwd(proj, dy, ss, t_invs, conv_w, alog_row, dt_row, norm_w, xf=None):
    s = proj.shape[0]
    nc = s // LR
    n_in, n_out = 15, 6
    x_ops, x_in_specs, x_out_shapes, x_out_specs, aliases, x_sems, n_xin, n_land = _carried(xf, n_in, n_out)

    def body(*refs):
        (qh_ref, q_ref, kh_ref, k_ref, vh_ref, v_ref, z_ref, sm_ref, dy_ref, ss_ref, ti_ref,
         cw_ref, al_ref, dt_ref, nw_ref) = refs[:n_in]
        x_refs = refs[n_in:n_in + n_xin]
        dp_ref, dsm_ref, dcw_ref, dal_ref, ddt_ref, dnw_ref = refs[n_in + n_xin:n_in + n_xin + n_out]
        land_refs = refs[n_in + n_xin + n_out:n_in + n_xin + n_out + n_land]
        dst_s, dhalo = refs[n_in + n_xin + n_out + n_land:n_in + n_xin + n_out + n_land + 2]
        sem_refs = refs[n_in + n_xin + n_out + n_land + 2:]
        i = pl.program_id(0)
        first_chunk = i == nc - 1
        _carry_open(xf, i, nc, x_refs, land_refs, sem_refs)

        @pl.when(i == 0)
        def _():
            for ref in (dcw_ref, dal_ref, ddt_ref, dnw_ref, dst_s, dhalo):
                ref[...] = jnp.zeros_like(ref)

        halo = [jnp.where(first_chunk, 0.0, r[...]) for r in (qh_ref, kh_ref, vh_ref)]
        with_known = functools.partial(gd_chunk, known_t=ti_ref[...])
        _, vjp = jax.vjp(with_known, halo[0], q_ref[...], halo[1], k_ref[...], halo[2], v_ref[...], z_ref[...], sm_ref[...],
                         cw_ref[...], al_ref[...], dt_ref[...], nw_ref[...], ss_ref[0])
        dqh, dq, dkh, dk, dvh, dv, dz, dsm, dcw, dal, ddt, dnw, dst = vjp((dy_ref[...], dst_s[...]))
        for j, val in enumerate((dq, dk, dv)):
            val = val + jnp.concatenate([jnp.zeros((LR - 8, SEG), f32), dhalo[:, j * SEG:(j + 1) * SEG]], axis=0)
            dp_ref[:, j * SEG:(j + 1) * SEG] = val.astype(bf16)
        dp_ref[:, 3 * SEG:4 * SEG] = dz.astype(bf16)
        for j, val in enumerate((dqh, dkh, dvh)):
            dhalo[:, j * SEG:(j + 1) * SEG] = val
        dsm_ref[...] = dsm
        dcw_ref[...] += dcw
        dal_ref[0:1, :] += dal
        ddt_ref[0:1, :] += ddt
        dnw_ref[0:1, :] += dnw
        dst_s[...] = dst
        _carry_close(xf, i, nc, x_refs, land_refs, sem_refs)

    qkv_specs = []
    for sg in GD_SEGS:
        qkv_specs += [_halo_spec(LR, sg, nc), _seg_spec(LR, sg, nc)]
    return pl.pallas_call(
        body, grid=(nc,),
        in_specs=qkv_specs + [_seg_spec(LR, 10, nc), pl.BlockSpec((LR, HD), lambda i: (nc - 1 - i, SMALL_BLK)),
                              pl.BlockSpec((LR, SEG), lambda i: (nc - 1 - i, 0)),
                              pl.BlockSpec((1, SEG, HD), lambda i: (nc - 1 - i, 0, 0)),
                              pl.BlockSpec((RUN * NH, LC, LC), lambda i: (nc - 1 - i, 0, 0)),
                              _const_spec((4, 3 * SEG)), _const_spec((1, HD)), _const_spec((1, HD)), _const_spec((1, HD))]
        + x_in_specs,
        out_specs=[pl.BlockSpec((LR, 4 * SEG), lambda i: (nc - 1 - i, 0)), pl.BlockSpec((LR, HD), lambda i: (nc - 1 - i, 0)),
                   _const_spec((4, 3 * SEG)), _const_spec((8, HD)), _const_spec((8, HD)), _const_spec((8, HD))] + x_out_specs,
        out_shape=[jax.ShapeDtypeStruct((s, 4 * SEG), bf16), jax.ShapeDtypeStruct((s, HD), f32),
                   jax.ShapeDtypeStruct((4, 3 * SEG), f32), jax.ShapeDtypeStruct((8, HD), f32),
                   jax.ShapeDtypeStruct((8, HD), f32), jax.ShapeDtypeStruct((8, HD), f32)] + x_out_shapes,
        scratch_shapes=[pltpu.VMEM((SEG, HD), f32), pltpu.VMEM((8, 3 * SEG), f32)] + x_sems, input_output_aliases=aliases,
        name="gdn_bwd" if xf is None else "gdn_bwd_carrying", compiler_params=_cp(48))(
            proj, proj, proj, proj, proj, proj, proj, proj, dy, ss, t_invs, conv_w, alog_row, dt_row, norm_w, *x_ops)


def _my_place():
    return lax.axis_index("x"), lax.axis_index("y"), lax.axis_index("c")


def _slot(p):
    return 4 * p[0] + 2 * p[1] + p[2]


def _peer(me, j):
    flips = ((j >> 2) & 1, (j >> 1) & 1, j & 1)
    return tuple((1 - v) if fl else v for v, fl in zip(me, flips))


_ANY = pl.BlockSpec(memory_space=pl.ANY)


class Scatter(NamedTuple):
    srcs: list
    lands: list
    land_of: list
    layer: list
    span: list


class Gather2(NamedTuple):
    srcs: list
    lands: list
    span: list


def _carry_plumb(cx, n_in, n_out):
    if cx is None:
        return [], [], {}, [], 0, 0, 0
    n_src = len(cx.srcs)
    passed = [li for li, ld in enumerate(cx.lands) if not isinstance(ld, jax.ShapeDtypeStruct)]
    operands = list(cx.srcs) + [cx.lands[li] for li in passed]
    aliases = {n_in + n_src + k: n_out + li for k, li in enumerate(passed)}
    out_shapes = [jax.ShapeDtypeStruct(ld.shape, ld.dtype) for ld in cx.lands]
    sems = [pltpu.SemaphoreType.DMA((n_src, N_DEV - 1)), pltpu.SemaphoreType.DMA((n_src, N_DEV - 1)),
            pltpu.SemaphoreType.DMA((n_src,))]
    return operands, out_shapes, aliases, sems, n_src, len(operands), len(cx.lands)


def _scatter_copies(sc, src_refs, land_refs, send_sems, recv_sems, local_sems):
    me = _my_place()
    mine = _slot(me)
    local, remote = [], []
    for a, src_ref in enumerate(src_refs):
        lo, hi = sc.span[a]
        rows = pl.ds(lo, hi - lo)
        dst = land_refs[sc.land_of[a]].at[mine, sc.layer[a], rows]
        local.append(pltpu.make_async_copy(src_ref.at[mine, rows], dst, local_sems.at[a]))
        for j in range(1, N_DEV):
            peer = _peer(me, j)
            remote.append(pltpu.make_async_remote_copy(
                src_ref=src_ref.at[_slot(peer), rows], dst_ref=dst, send_sem=send_sems.at[a, j - 1],
                recv_sem=recv_sems.at[a, j - 1], device_id=peer, device_id_type=MESH))
    return local, remote


def _gather2_copy(land, sems, a, k, block_of, to, rows, src=None):
    dst = land.at[_slot(block_of), rows]
    return pltpu.make_async_remote_copy(src_ref=dst if src is None else src, dst_ref=dst, send_sem=sems[0].at[a, k],
                                        recv_sem=sems[1].at[a, k], device_id=to, device_id_type=MESH)


def _gather2_places():
    x, y, c = _my_place()
    return (x, y, c), (x, y, 1 - c), [(1 - x, y), (x, 1 - y), (1 - x, 1 - y)], c


def _carry_start(cx, src_refs, land_refs, sems):
    if isinstance(cx, Scatter):
        local, remote = _scatter_copies(cx, src_refs, land_refs, *sems)
        for cp in local + remote:
            cp.start()
        return
    me, sib, chips, c = _gather2_places()
    for a, (src_ref, land) in enumerate(zip(src_refs, land_refs)):
        rows = pl.ds(cx.span[a][0], cx.span[a][1] - cx.span[a][0])
        src = src_ref.at[rows]
        pltpu.make_async_copy(src, land.at[_slot(me), rows], sems[2].at[a]).start()
        _gather2_copy(land, sems, a, 0, me, sib, rows, src=src).start()
        for j, chip in enumerate(chips):
            _gather2_copy(land, sems, a, 1 + j, me, (*chip, c), rows, src=src).start()


def _carry_middle(cx, src_refs, land_refs, sems):
    if isinstance(cx, Scatter):
        return
    me, sib, chips, c = _gather2_places()
    for a, land in enumerate(land_refs):
        rows = pl.ds(cx.span[a][0], cx.span[a][1] - cx.span[a][0])
        for j, chip in enumerate(chips):
            _gather2_copy(land, sems, a, 1 + j, (*chip, c), me, rows).wait_recv()
            _gather2_copy(land, sems, a, 4 + j, (*chip, c), sib, rows).start()


def _carry_finish(cx, src_refs, land_refs, sems):
    if isinstance(cx, Scatter):
        local, remote = _scatter_copies(cx, src_refs, land_refs, *sems)
        for cp in remote:
            cp.wait()
        for cp in local:
            cp.wait()
        return
    me, sib, chips, c = _gather2_places()
    for a, (src_ref, land) in enumerate(zip(src_refs, land_refs)):
        rows = pl.ds(cx.span[a][0], cx.span[a][1] - cx.span[a][0])
        src = src_ref.at[rows]
        _gather2_copy(land, sems, a, 0, sib, me, rows).wait_recv()
        for j, chip in enumerate(chips):
            _gather2_copy(land, sems, a, 4 + j, (*chip, 1 - c), me, rows).wait_recv()
        _gather2_copy(land, sems, a, 0, me, sib, rows, src=src).wait_send()
        for j, chip in enumerate(chips):
            _gather2_copy(land, sems, a, 1 + j, me, (*chip, c), rows, src=src).wait_send()
            _gather2_copy(land, sems, a, 4 + j, (*chip, c), sib, rows).wait_send()
        pltpu.make_async_copy(src, land.at[_slot(me), rows], sems[2].at[a]).wait()


def exchange(cx, name):
    operands, out_shapes, aliases, sems, n_src, n_xin, n_land = _carry_plumb(cx, 0, 0)

    def body(*refs):
        src_refs, land_refs, sem_refs = refs[:n_src], refs[n_xin:n_xin + n_land], refs[n_xin + n_land:]
        _carry_start(cx, src_refs, land_refs, sem_refs)
        _carry_middle(cx, src_refs, land_refs, sem_refs)
        _carry_finish(cx, src_refs, land_refs, sem_refs)

    return pl.pallas_call(body, in_specs=[_ANY] * n_xin, out_specs=[_ANY] * n_land, out_shape=out_shapes,
                          scratch_shapes=sems, input_output_aliases=aliases, name=name)(*operands)


def small_allreduce(packed, name, xf=None):
    r = packed.shape[0]
    rc = r // N_DEV
    x_ops, x_in_specs, x_out_shapes, x_out_specs, aliases, x_sems, n_xin, n_land = _carried(xf, 1, 1)

    def body(*refs):
        (in_ref,), x_refs, (out_ref,), land_refs, scratch, sem_refs = _cut_refs(refs, 1, n_xin, 1, n_land, 5)
        recv_buf, send1, recv1, send2, recv2 = scratch
        if xf is not None:
            _carry_start(xf, x_refs[:len(xf.srcs)], land_refs, sem_refs)
        me = _my_place()
        mine = _slot(me)
        my_rows = pl.ds(pl.multiple_of(mine * rc, 8), rc)
        first, second = [], []
        for j in range(1, N_DEV):
            peer = _peer(me, j)
            peer_rows = pl.ds(pl.multiple_of(_slot(peer) * rc, 8), rc)
            first.append(pltpu.make_async_remote_copy(
                src_ref=in_ref.at[peer_rows], dst_ref=recv_buf.at[mine], send_sem=send1.at[j - 1], recv_sem=recv1.at[j - 1],
                device_id=peer, device_id_type=MESH))
            second.append(pltpu.make_async_remote_copy(
                src_ref=out_ref.at[my_rows], dst_ref=out_ref.at[my_rows], send_sem=send2.at[j - 1], recv_sem=recv2.at[j - 1],
                device_id=peer, device_id_type=MESH))
        for cp in first:
            cp.start()
        recv_buf[mine] = in_ref[my_rows]
        for cp in first:
            cp.wait()
        acc = recv_buf[0]
        for k in range(1, N_DEV):
            acc = acc + recv_buf[k]
        out_ref[my_rows] = acc
        for cp in second:
            cp.start()
        for cp in second:
            cp.wait()
        if xf is not None:
            _carry_finish(xf, x_refs[:len(xf.srcs)], land_refs, sem_refs)

    vmem = pl.BlockSpec(memory_space=pltpu.VMEM)
    return pl.pallas_call(
        body, in_specs=[vmem] + x_in_specs, out_specs=[vmem] + x_out_specs,
        out_shape=[jax.ShapeDtypeStruct((r, 128), f32)] + x_out_shapes,
        scratch_shapes=[pltpu.VMEM((N_DEV, rc, 128), f32)] + [pltpu.SemaphoreType.DMA((N_DEV - 1,))] * 4 + x_sems,
        input_output_aliases=aliases, name=name, compiler_params=_cp(32))(packed, *x_ops)


def _adamw(w, g, m, v):
    m = ADAM_B1 * m + (1.0 - ADAM_B1) * g
    v = ADAM_B2 * v + (1.0 - ADAM_B2) * (g * g)
    m_hat = m / (1.0 - ADAM_B1 ** ADAM_STEP)
    v_hat = v / (1.0 - ADAM_B2 ** ADAM_STEP)
    delta = -ADAM_LR * (m_hat / (jnp.sqrt(v_hat) + ADAM_EPS) + ADAM_WD * w)
    return delta, m, v


def adam_slots(slots, w, m, v, rows, name):
    _, depth, r, c = slots.shape

    def body(s_ref, w_ref, m_ref, v_ref, g_ref, d_ref, m2_ref, v2_ref):
        g = s_ref[0, 0].astype(f32)
        for k in range(1, N_DEV):
            g = g + s_ref[k, 0].astype(f32)
        d, m2, v2 = _adamw(w_ref[0], g, m_ref[0], v_ref[0])
        g_ref[0] = g
        d_ref[0] = d
        m2_ref[0] = m2
        v2_ref[0] = v2

    blk = pl.BlockSpec((1, rows, c), lambda l, i: (l, i, 0))
    sh = jax.ShapeDtypeStruct((depth, r, c), f32)
    return pl.pallas_call(
        body, grid=(depth, r // rows),
        in_specs=[pl.BlockSpec((N_DEV, 1, rows, c), lambda l, i: (0, l, i, 0)), blk, blk, blk],
        out_specs=[blk] * 4, out_shape=[sh] * 4, name=name, compiler_params=_cp(40))(slots, w, m, v)


def adam_flat(g, w, m, v, name):
    def body(g_ref, w_ref, m_ref, v_ref, d_ref, m2_ref, v2_ref):
        d, m2, v2 = _adamw(w_ref[...], g_ref[...], m_ref[...], v_ref[...])
        d_ref[...] = d
        m2_ref[...] = m2
        v2_ref[...] = v2

    sh = jax.ShapeDtypeStruct(g.shape, f32)
    return pl.pallas_call(body, out_shape=[sh] * 3, name=name)(g, w, m, v)


def _rows_of(shape):
    n = 1
    for dim in shape:
        n *= dim
    return n, -(-n // (8 * 128)) * 8


def _pack(arrs):
    parts = []
    for a in arrs:
        n, rows = _rows_of(a.shape)
        parts.append(jnp.pad(a.reshape(-1).astype(f32), (0, rows * 128 - n)).reshape(rows, 128))
    return jnp.concatenate(parts, axis=0)


def _unpack(packed, shapes):
    out, row = [], 0
    for sh in shapes:
        n, rows = _rows_of(sh)
        out.append(packed[row:row + rows].reshape(-1)[:n].reshape(sh))
        row += rows
    return out


def _block_diag(gw):
    eye = jnp.eye(8, dtype=gw.dtype)
    return (gw[..., :, :, None, :] * eye[:, None, :, None]).reshape(gw.shape[:-3] + (SEG, SEG))


def _diag_blocks(dense):
    eye = jnp.eye(8, dtype=dense.dtype)
    return (dense.reshape(8, 64, 8, 64) * eye[:, None, :, None]).sum(axis=2)


def _lane_rows(vals, first_lane):
    return jnp.zeros((vals.shape[0], 1, HD), f32).at[:, 0, first_lane:first_lane + vals.shape[1]].set(vals)


def kernel(x, norm_w, w_in, rg_conv_w, rg_conv_b, rg_gate_w, rg_gate_b, rg_lambda, ml_gate_b, ml_norm_w, gd_conv_w, gd_a_log, gd_dt_bias, gd_norm_w, w_out, final_norm_w, loss_target, m_norm_w, m_w_in, m_rg_conv_w, m_rg_conv_b, m_rg_gate_w, m_rg_gate_b, m_rg_lambda, m_ml_gate_b, m_ml_norm_w, m_gd_conv_w, m_gd_a_log, m_gd_dt_bias, m_gd_norm_w, m_w_out, m_final_norm_w, v_norm_w, v_w_in, v_rg_conv_w, v_rg_conv_b, v_rg_gate_w, v_rg_gate_b, v_rg_lambda, v_ml_gate_b, v_ml_norm_w, v_gd_conv_w, v_gd_a_log, v_gd_dt_bias, v_gd_norm_w, v_w_out, v_final_norm_w):
    s = x.shape[1]
    xs = x.reshape(s, D)
    tgt = loss_target.reshape(s, D)
    me = 4 * lax.axis_index("x") + 2 * lax.axis_index("y") + lax.axis_index("c")

    comm = MeshComm(w_in, w_out, [rg_conv_w, rg_gate_b, gd_conv_w])
    rg_conv_full, rg_gb_full, gd_conv_full = comm.small_weights
    loss_part, dx, d_fw, g_small = local_step(
        xs, tgt, comm, rg_conv_full, rg_gb_full, gd_conv_full, norm_w, rg_conv_b, rg_gate_w, rg_lambda,
        ml_gate_b, ml_norm_w, gd_a_log, gd_dt_bias, gd_norm_w, final_norm_w)
    given_w = dict(norm_w=norm_w, rg_conv_w=rg_conv_w, rg_conv_b=rg_conv_b, rg_gate_w=rg_gate_w, rg_gate_b=rg_gate_b,
                   rg_lambda=rg_lambda, ml_gate_b=ml_gate_b, ml_norm_w=ml_norm_w, gd_conv_w=gd_conv_w, gd_a_log=gd_a_log,
                   gd_dt_bias=gd_dt_bias, gd_norm_w=gd_norm_w, final_norm_w=final_norm_w, w_in=w_in, w_out=w_out)
    given_m = dict(norm_w=m_norm_w, rg_conv_w=m_rg_conv_w, rg_conv_b=m_rg_conv_b, rg_gate_w=m_rg_gate_w, rg_gate_b=m_rg_gate_b,
                   rg_lambda=m_rg_lambda, ml_gate_b=m_ml_gate_b, ml_norm_w=m_ml_norm_w, gd_conv_w=m_gd_conv_w,
                   gd_a_log=m_gd_a_log, gd_dt_bias=m_gd_dt_bias, gd_norm_w=m_gd_norm_w, final_norm_w=m_final_norm_w,
                   w_in=m_w_in, w_out=m_w_out)
    given_v = dict(norm_w=v_norm_w, rg_conv_w=v_rg_conv_w, rg_conv_b=v_rg_conv_b, rg_gate_w=v_rg_gate_w, rg_gate_b=v_rg_gate_b,
                   rg_lambda=v_rg_lambda, ml_gate_b=v_ml_gate_b, ml_norm_w=v_ml_norm_w, gd_conv_w=v_gd_conv_w,
                   gd_a_log=v_gd_a_log, gd_dt_bias=v_gd_dt_bias, gd_norm_w=v_gd_norm_w, final_norm_w=v_final_norm_w,
                   w_in=v_w_in, w_out=v_w_out)
    return finish_step(loss_part, dx, d_fw, g_small, comm, s, me, given_w, given_m, given_v)


def _gathered_pieces():
    per = D_IN // N_DEV
    pieces = []
    for lo, hi in ((0, 3584), (3592, 5640), (3584, 3592), (5640, 5648)):
        col = lo
        while col < hi:
            k = col // per
            end = min(hi, (k + 1) * per)
            pieces.append((k, col - k * per, end - k * per))
            col = end
    return pieces


def regroup_w_in(wi_g):
    rows = 256
    pieces = _gathered_pieces()

    def body(x_ref, o_ref):
        cols = [x_ref[k, :, a:b] for k, a, b in pieces] + [jnp.zeros((rows, DP - D_IN), wi_g.dtype)]
        o_ref[...] = jnp.concatenate(cols, axis=1)

    return pl.pallas_call(
        body, grid=(D // rows,), in_specs=[pl.BlockSpec((N_DEV, rows, D_IN // N_DEV), lambda i: (0, i, 0))],
        out_specs=pl.BlockSpec((rows, DP), lambda i: (i, 0)), out_shape=jax.ShapeDtypeStruct((D, DP), wi_g.dtype),
        name="regroup_w_in")(wi_g)


def slots_of_w_in_grad(gw_rg, gw_ml, gw_gd, gw_sm):
    rows = 256
    per = D_IN // N_DEV

    def body(rg_ref, ml_ref, gd_ref, sm_ref, o_ref):
        sm = sm_ref[...]
        g = jnp.concatenate([rg_ref[...], ml_ref[...], sm[:, 0:2 * NH], gd_ref[...], sm[:, 2 * NH:4 * NH]], axis=1)
        for k in range(N_DEV):
            o_ref[k] = g[:, k * per:(k + 1) * per]

    return pl.pallas_call(
        body, grid=(D // rows,),
        in_specs=[pl.BlockSpec((rows, a.shape[1]), lambda i: (i, 0)) for a in (gw_rg, gw_ml, gw_gd, gw_sm)],
        out_specs=pl.BlockSpec((N_DEV, rows, per), lambda i: (0, i, 0)),
        out_shape=jax.ShapeDtypeStruct((N_DEV, D, per), gw_rg.dtype), name="slots_of_w_in_grad",
        compiler_params=_cp(40))(gw_rg, gw_ml, gw_gd, gw_sm)


class MeshComm:
    GWI_SPLIT = 448
    WI_SPLIT = 384

    def __init__(self, w_in, w_out, small_shards):
        per = D_IN // N_DEV
        self.wi_sh = [w_in[l].astype(bf16) for l in range(DEPTH)]
        self.wo_sh = [w_out[l].astype(bf16) for l in range(DEPTH)]
        self.wi_land = jax.ShapeDtypeStruct((N_DEV, D, per), bf16)
        self.wo_land = jax.ShapeDtypeStruct((N_DEV, 3 * SEG // N_DEV, D), bf16)
        packed = _pack(small_shards)
        first = Gather2([self.wi_sh[0], self.wo_sh[0], packed],
                        [self.wi_land, self.wo_land, jax.ShapeDtypeStruct((N_DEV,) + packed.shape, f32)],
                        [(0, D), (0, 3 * SEG // N_DEV), (0, packed.shape[0])])
        wi_g, wo_g, sm_g = exchange(first, "gather_first")
        self.wi_g, self.wo_g = {0: wi_g}, {0: wo_g}
        shapes = [a.shape for a in small_shards]
        parts = [_unpack(sm_g[k], shapes) for k in range(N_DEV)]
        self.small_weights = [jnp.concatenate([p[j] for p in parts], axis=-1) for j in range(len(small_shards))]
        self.gwi_land = lax.empty((N_DEV, DEPTH, D, per), bf16)
        self.gwo_land = lax.empty((N_DEV, DEPTH, 3 * SEG // N_DEV, D), bf16)
        self.gwi_slots = {}
        self.gwo_slots = {}

    def weights(self, l):
        return regroup_w_in(self.wi_g[l]), self.wo_g[l].reshape(3 * SEG, D)

    def fwd_carry(self, l, host):
        if l + 1 >= DEPTH:
            return None
        if host == "mlstm":
            return Gather2([self.wo_sh[l + 1]], [self.wo_land], [(0, 3 * SEG // N_DEV)])
        if host == "inproj":
            return Gather2([self.wi_sh[l + 1]], [self.wi_land], [(0, self.WI_SPLIT)])
        return Gather2([self.wi_sh[l + 1]], [self.wi_g[l + 1]], [(self.WI_SPLIT, D)])

    def fwd_landed(self, l, host, landed):
        (self.wo_g if host == "mlstm" else self.wi_g)[l + 1] = landed[0]

    def own_w_out_grad(self, l, g_wo):
        self.gwo_slots[l] = g_wo.reshape(N_DEV, 3 * SEG // N_DEV, D).astype(bf16)

    def bwd_carry(self, l, host):
        rows_o = 3 * SEG // N_DEV
        srcs, land_of, layer, span = [], [], [], []
        if l + 1 < DEPTH:
            if host == "rglru":
                srcs, land_of, layer, span = [self.gwo_slots[l + 1]], [1], [l + 1], [(0, rows_o)]
            elif host == "mlstm":
                srcs, land_of, layer, span = [self.gwi_slots[l + 1]], [0], [l + 1], [(0, self.GWI_SPLIT)]
            else:
                srcs, land_of, layer, span = [self.gwi_slots[l + 1]], [0], [l + 1], [(self.GWI_SPLIT, D)]
        if l == 0 and host == "mlstm":
            srcs, land_of, layer, span = srcs + [self.gwo_slots[0]], land_of + [1], layer + [0], span + [(0, rows_o)]
        if not srcs:
            return None
        return Scatter(srcs, [self.gwi_land, self.gwo_land], land_of, layer, span)

    def bwd_landed(self, landed):
        self.gwi_land, self.gwo_land = landed

    def grads_ready(self, l, gw_rg, gw_ml, gw_gd, gw_sm):
        self.gwi_slots[l] = slots_of_w_in_grad(gw_rg, gw_ml, gw_gd, gw_sm)

    def last_carry(self):
        return Scatter([self.gwi_slots[0]], [self.gwi_land, self.gwo_land], [0], [0], [(0, D)])


def local_step(xs, tgt, comm, rg_conv_full, rg_gb_full, gd_conv_full, norm_w, rg_conv_b, rg_gate_w,
               rg_lambda, ml_gate_b, ml_norm_w, gd_a_log, gd_dt_bias, gd_norm_w, final_norm_w):
    gate_dense = _block_diag(rg_gate_w)
    ml_bias_rows = _lane_rows(ml_gate_b.reshape(DEPTH, 2 * NH), 0)
    alog_rows = _lane_rows(gd_a_log, 2 * NH)
    dt_rows = _lane_rows(gd_dt_bias, 2 * NH)
    acts = []
    for l in range(DEPTH):
        nw = norm_w[l].reshape(1, D)
        w_in_l, w_out_l = comm.weights(l)
        xf = comm.fwd_carry(l, "inproj")
        proj, hn_t, *landed = inproj(xs, nw, w_in_l, xf=xf)
        if xf is not None:
            comm.fwd_landed(l, "inproj", landed)
        rg_p = (rg_conv_full[l], rg_conv_b[l].reshape(1, SEG), gate_dense[l, 0], gate_dense[l, 1],
                rg_gb_full[l], rg_lambda[l].reshape(1, SEG))
        y_rg, hs = rglru_fwd(proj, *rg_p)
        ml_p = (ml_bias_rows[l], ml_norm_w[l].reshape(1, SEG))
        xf = comm.fwd_carry(l, "mlstm")
        y_ml, cs, ns, ms, *landed = mlstm_fwd(proj, *ml_p, xf=xf)
        if xf is not None:
            comm.fwd_landed(l, "mlstm", landed)
        gd_p = (gd_conv_full[l], alog_rows[l], dt_rows[l], gd_norm_w[l].reshape(1, HD))
        xf = comm.fwd_carry(l, "gdn")
        y_gd, ss, t_invs, *landed = gdn_fwd(proj, *gd_p, xf=xf)
        if xf is not None:
            comm.fwd_landed(l, "gdn", landed)
        acts.append((xs, nw, proj, hn_t, w_in_l, w_out_l, rg_p, y_rg, hs, ml_p, y_ml, cs, ns, ms, gd_p, y_gd, ss, t_invs))
        xs = outproj(xs, y_rg, y_ml, y_gd, w_out_l)

    dx, loss_part, d_fw = head(xs, final_norm_w.reshape(1, D), tgt)

    g_small = {k: [None] * DEPTH for k in ("norm_w", "rg_conv_w", "rg_conv_b", "rg_gate_w", "rg_gate_b", "rg_lambda",
                                           "ml_gate_b", "ml_norm_w", "gd_conv_w", "gd_a_log", "gd_dt_bias", "gd_norm_w")}
    for l in reversed(range(DEPTH)):
        x_l, nw, proj, hn_t, w_in_l, w_out_l, rg_p, y_rg, hs, ml_p, y_ml, cs, ns, ms, gd_p, y_gd, ss, t_invs = acts[l]
        dy_rg, dy_ml, dy_gd, g_wo = outproj_bwd(dx, y_rg, y_ml, y_gd, w_out_l)
        comm.own_w_out_grad(l, g_wo)
        xf = comm.bwd_carry(l, "rglru")
        d_rg, d_cw, d_cb, d_gr, d_gi, d_gb, d_lam, *landed = rglru_bwd(proj, hs, dy_rg, *rg_p, xf=xf)
        if xf is not None:
            comm.bwd_landed(landed)
        xf = comm.bwd_carry(l, "mlstm")
        d_ml, d_sm_ml, d_bias, d_mnw, *landed = mlstm_bwd(proj, dy_ml, cs, ns, ms, *ml_p, xf=xf)
        if xf is not None:
            comm.bwd_landed(landed)
        xf = comm.bwd_carry(l, "gdn")
        d_gd, d_sm_gd, d_gcw, d_al, d_dt, d_gnw, *landed = gdn_bwd(proj, dy_gd, ss, t_invs, *gd_p, xf=xf)
        if xf is not None:
            comm.bwd_landed(landed)
        dx, d_nw = inproj_bwd_x(x_l, nw, w_in_l, dx, d_rg, d_ml, d_gd, d_sm_ml, d_sm_gd)
        gw_rg = wgrad(hn_t, [d_rg], "wgrad_rg")
        gw_ml = wgrad(hn_t, [d_ml], "wgrad_ml")
        gw_gd = wgrad(hn_t, [d_gd], "wgrad_gd")
        gw_sm = wgrad(hn_t, [d_sm_ml, d_sm_gd], "wgrad_small")
        comm.grads_ready(l, gw_rg, gw_ml, gw_gd, gw_sm)
        g_small["norm_w"][l] = d_nw[0]
        g_small["rg_conv_w"][l] = d_cw
        g_small["rg_conv_b"][l] = d_cb[0]
        g_small["rg_gate_w"][l] = jnp.stack([_diag_blocks(d_gr), _diag_blocks(d_gi)])
        g_small["rg_gate_b"][l] = d_gb
        g_small["rg_lambda"][l] = d_lam[0]
        g_small["ml_gate_b"][l] = d_bias[0, 0:2 * NH].reshape(2, NH)
        g_small["ml_norm_w"][l] = d_mnw[0]
        g_small["gd_conv_w"][l] = d_gcw
        g_small["gd_a_log"][l] = d_al[0, 2 * NH:3 * NH]
        g_small["gd_dt_bias"][l] = d_dt[0, 2 * NH:3 * NH]
        g_small["gd_norm_w"][l] = d_gnw[0]
    return loss_part, dx, d_fw, g_small


def finish_step(loss_part, dx, d_fw, g_small, comm, s, me, given_w, given_m, given_v):
    small_names = ["norm_w", "rg_conv_w", "rg_conv_b", "rg_gate_w", "rg_gate_b", "rg_lambda", "ml_gate_b", "ml_norm_w",
                   "gd_conv_w", "gd_a_log", "gd_dt_bias", "gd_norm_w"]
    small_list = [loss_part[0, 0:1], d_fw[0]] + [jnp.stack(g_small[k]) for k in small_names]
    small_shapes = [a.shape for a in small_list]
    packed = _pack(small_list)
    packed = jnp.pad(packed, ((0, -packed.shape[0] % (8 * N_DEV)), (0, 0)))
    summed, gwi_r, gwo_r = small_allreduce(packed, "last_exchange", xf=comm.last_carry())
    g_all = _unpack(summed, small_shapes)

    g_wi, d_wi, m_wi, v_wi = adam_slots(gwi_r, given_w["w_in"], given_m["w_in"], given_v["w_in"], 256, "adam_w_in")
    g_wo, d_wo, m_wo, v_wo = adam_slots(gwo_r, given_w["w_out"], given_m["w_out"], given_v["w_out"], 192, "adam_w_out")
    loss = g_all[0][0]
    grads = {"final_norm_w": g_all[1]}
    for k, g in zip(small_names, g_all[2:]):
        grads[k] = g
    for k, width in (("rg_conv_w", 64), ("rg_gate_b", 64), ("gd_conv_w", 192)):
        grads[k] = lax.dynamic_slice_in_dim(grads[k], me * width, width, axis=2)
    names = small_names + ["final_norm_w"]
    shapes = [given_w[k].shape for k in names]
    d_p, m_p, v_p = adam_flat(_pack([grads[k] for k in names]), _pack([given_w[k] for k in names]),
                              _pack([given_m[k] for k in names]), _pack([given_v[k] for k in names]), "adam_small")
    deltas = dict(zip(names, _unpack(d_p, shapes)))
    new_m = dict(zip(names, _unpack(m_p, shapes)))
    new_v = dict(zip(names, _unpack(v_p, shapes)))
    grads["w_in"], deltas["w_in"], new_m["w_in"], new_v["w_in"] = g_wi, d_wi, m_wi, v_wi
    grads["w_out"], deltas["w_out"], new_m["w_out"], new_v["w_out"] = g_wo, d_wo, m_wo, v_wo

    order = ["norm_w", "w_in", "rg_conv_w", "rg_conv_b", "rg_gate_w", "rg_gate_b", "rg_lambda", "ml_gate_b", "ml_norm_w",
             "gd_conv_w", "gd_a_log", "gd_dt_bias", "gd_norm_w", "w_out", "final_norm_w"]
    return (loss, dx.reshape(1, s, D), *[grads[k] for k in order], *[deltas[k] for k in order],
            *[new_m[k] for k in order], *[new_v[k] for k in order])
```
